```python
import math
import jax, jax.numpy as jnp
from jax import lax
import numpy as np

D_MODEL = 2048
BATCH = 8
SEQ = 2048
DEPTH = 1

CHUNK = 64
N_META = 16
Q_BLOCK = 128
EPS = 1e-6

D_SSM = D_MODEL // 2
SSM_GROUP = 16
N_SSM_GROUPS = D_SSM // SSM_GROUP
SSM_STATE = 64
DT_MIN = 1e-3
DT_MAX = 1e-1

MLA_HEADS = 8
QK_NOPE = 128
QK_ROPE = 64
V_HEAD = 128
Q_LORA = 512
KV_LORA = 256
D_ATTN = MLA_HEADS * V_HEAD
ROPE_BASE = 10000.0

D_MIX = D_SSM + D_ATTN
D_IN = D_SSM + Q_LORA + KV_LORA + QK_ROPE

D_FF = 5504
CONV_W = 3

kernel_name = "hybrid_s5_mla_convffn_block"


def rmsnorm(x, g):
    xf = x.astype(jnp.float32)
    y = xf * lax.rsqrt(jnp.mean(xf * xf, axis=-1, keepdims=True) + EPS)
    return (y * g.astype(jnp.float32)).astype(x.dtype)


def rotary(x, cos, sin):
    x1, x2 = jnp.split(x, 2, axis=-1)
    return jnp.concatenate([x1 * cos - x2 * sin, x2 * cos + x1 * sin], axis=-1)


def s5_mixer(u, lam_re, lam_im, log_dt, b_re, b_im, c_re, c_im, d_skip, w_glu, b_glu):
    bsz, L, _ = u.shape
    f32 = jnp.float32
    uf = u.astype(f32).reshape(bsz, L, N_SSM_GROUPS, SSM_GROUP)
    lam = lax.complex(lam_re.astype(f32), lam_im.astype(f32))
    dt = jnp.exp(log_dt.astype(f32))[:, None]
    lam_bar = jnp.exp(lam * dt)
    b = lax.complex(b_re.astype(f32), b_im.astype(f32))
    b_bar = ((lam_bar - 1.0) / lam)[..., None] * b
    bu = jnp.einsum('blgc,gpc->blgp', uf.astype(jnp.complex64), b_bar)
    a = jnp.broadcast_to(lam_bar, bu.shape)

    def combine(e1, e2):
        a1, s1 = e1
        a2, s2 = e2
        return a1 * a2, a2 * s1 + s2

    _, h = lax.associative_scan(combine, (a, bu), axis=1)
    c = lax.complex(c_re.astype(f32), c_im.astype(f32))
    y = jnp.real(jnp.einsum('blgp,gcp->blgc', h, c))
    y = y + d_skip.astype(f32).reshape(N_SSM_GROUPS, SSM_GROUP) * uf
    y = y.reshape(bsz, L, D_SSM)
    g = jax.nn.gelu(y)
    out = g * jax.nn.sigmoid(g @ w_glu.astype(f32) + b_glu.astype(f32))
    return out.astype(u.dtype)


def mla_mixer(q_a, kv_a, k_pe, q_a_norm, w_q_b, kv_a_norm, w_kv_b, cos, sin, chunk_id):
    bsz, L, _ = q_a.shape
    q = (rmsnorm(q_a, q_a_norm) @ w_q_b).reshape(bsz, L, MLA_HEADS, QK_NOPE + QK_ROPE)
    q_nope, q_pe = q[..., :QK_NOPE], q[..., QK_NOPE:]
    q_pe = rotary(q_pe, cos[:, None, :], sin[:, None, :])
    kv = (rmsnorm(kv_a, kv_a_norm) @ w_kv_b).reshape(bsz, L, MLA_HEADS, QK_NOPE + V_HEAD)
    k_nope, v = kv[..., :QK_NOPE], kv[..., QK_NOPE:]
    k_pe = rotary(k_pe, cos, sin)
    scale = 1.0 / math.sqrt(QK_NOPE + QK_ROPE)

    n_blk = -(-L // Q_BLOCK)
    pad = n_blk * Q_BLOCK - L

    def to_blocks(t):
        t = jnp.pad(t, ((0, 0), (0, pad)) + ((0, 0),) * (t.ndim - 2))
        return jnp.moveaxis(t.reshape(bsz, n_blk, Q_BLOCK, *t.shape[2:]), 1, 0)

    q_cid = jnp.pad(chunk_id, (0, pad), constant_values=2 ** 30).reshape(n_blk, Q_BLOCK)

    def attend(args):
        qn, qp, qc = args
        s = jnp.einsum('bqhd,bkhd->bhqk', qn, k_nope, preferred_element_type=jnp.float32)
        s = s + jnp.einsum('bqhr,bkr->bhqk', qp, k_pe, preferred_element_type=jnp.float32)
        mask = chunk_id[None, :] <= qc[:, None]
        s = jnp.where(mask[None, None], s * scale, jnp.finfo(jnp.float32).min)
        p = jax.nn.softmax(s, axis=-1).astype(v.dtype)
        return jnp.einsum('bhqk,bkhd->bqhd', p, v)

    o = lax.map(attend, (to_blocks(q_nope), to_blocks(q_pe), q_cid))
    o = jnp.moveaxis(o, 0, 1).reshape(bsz, n_blk * Q_BLOCK, D_ATTN)[:, :L]
    return o


def conv_ffn(x, w_up, conv_w, conv_b, w_down):
    L = x.shape[1]
    gate, val = jnp.split(x @ w_up, 2, axis=-1)
    gp = jnp.pad(gate, ((0, 0), (CONV_W - 1, 0), (0, 0)))
    gate = sum(conv_w[k] * gp[:, k:k + L] for k in range(CONV_W)) + conv_b
    return (jax.nn.silu(gate) * val) @ w_down


def _fwd_setup_inputs(seed: int = 0) -> dict:
    key = jax.random.key(seed)
    ks = jax.random.split(key, 32)
    f32 = jnp.float32
    nrm = lambda k, shape, s: jax.random.normal(k, shape, f32) * s
    gain = lambda k, shape: 1.0 + 0.01 * jax.random.normal(k, shape, f32)
    G, P, C = N_SSM_GROUPS, SSM_STATE, SSM_GROUP
    lam_re = -0.5 + 0.01 * jax.random.normal(ks[4], (DEPTH, G, P), f32)
    lam_im = jnp.pi * jnp.arange(P, dtype=f32)[None, None, :] + 0.01 * jax.random.normal(ks[5], (DEPTH, G, P), f32)
    log_dt = jax.random.uniform(ks[6], (DEPTH, G), f32, math.log(DT_MIN), math.log(DT_MAX))
    return {
        "x": jax.random.normal(ks[0], (BATCH, SEQ, D_MODEL), f32),
        "meta_tokens": nrm(ks[1], (N_META, D_MODEL), 1.0),
        "mix_norm": gain(ks[2], (DEPTH, D_MODEL)),
        "w_in": nrm(ks[3], (DEPTH, D_MODEL, D_IN), D_MODEL ** -0.5),
        "lam_re": lam_re,
        "lam_im": lam_im,
        "log_dt": log_dt,
        "b_re": nrm(ks[7], (DEPTH, G, P, C), (2 * C) ** -0.5),
        "b_im": nrm(ks[8], (DEPTH, G, P, C), (2 * C) ** -0.5),
        "c_re": nrm(ks[9], (DEPTH, G, C, P), (2 * P) ** -0.5),
        "c_im": nrm(ks[10], (DEPTH, G, C, P), (2 * P) ** -0.5),
        "d_skip": nrm(ks[11], (DEPTH, D_SSM), 1.0),
        "w_glu": nrm(ks[12], (DEPTH, D_SSM, D_SSM), D_SSM ** -0.5),
        "b_glu": nrm(ks[13], (DEPTH, D_SSM), 0.01),
        "q_a_norm": gain(ks[14], (DEPTH, Q_LORA)),
        "w_q_b": nrm(ks[15], (DEPTH, Q_LORA, MLA_HEADS * (QK_NOPE + QK_ROPE)), Q_LORA ** -0.5),
        "kv_a_norm": gain(ks[16], (DEPTH, KV_LORA)),
        "w_kv_b": nrm(ks[17], (DEPTH, KV_LORA, MLA_HEADS * (QK_NOPE + V_HEAD)), KV_LORA ** -0.5),
        "out_norm_ssm": gain(ks[18], (DEPTH, D_SSM)),
        "out_norm_attn": gain(ks[19], (DEPTH, D_ATTN)),
        "w_out": nrm(ks[20], (DEPTH, D_MIX, D_MODEL), D_MIX ** -0.5),
        "ffn_norm": gain(ks[21], (DEPTH, D_MODEL)),
        "w_up": nrm(ks[22], (DEPTH, D_MODEL, 2 * D_FF), D_MODEL ** -0.5),
        "conv_w": nrm(ks[23], (DEPTH, CONV_W, D_FF), CONV_W ** -0.5),
        "conv_b": nrm(ks[24], (DEPTH, D_FF), 0.01),
        "w_down": nrm(ks[25], (DEPTH, D_FF, D_MODEL), D_FF ** -0.5),
        "final_norm": gain(ks[26], (D_MODEL,)),
    }


def _fwd_reference(x, meta_tokens, mix_norm, w_in, lam_re, lam_im, log_dt, b_re, b_im, c_re, c_im,
              d_skip, w_glu, b_glu, q_a_norm, w_q_b, kv_a_norm, w_kv_b, out_norm_ssm,
              out_norm_attn, w_out, ffn_norm, w_up, conv_w, conv_b, w_down, final_norm):
    bsz = x.shape[0]
    meta = jnp.broadcast_to(meta_tokens.astype(x.dtype)[None], (bsz, N_META, D_MODEL))
    h = jnp.concatenate([meta, x], axis=1)
    L = h.shape[1]

    pos = jnp.arange(L, dtype=jnp.int32)
    chunk_id = jnp.where(pos < N_META, 0, 1 + (pos - N_META) // CHUNK)
    inv_freq = 1.0 / (ROPE_BASE ** (jnp.arange(0, QK_ROPE, 2, dtype=jnp.float32) / QK_ROPE))
    ang = pos.astype(jnp.float32)[:, None] * inv_freq[None, :]
    cos = jnp.cos(ang).astype(x.dtype)
    sin = jnp.sin(ang).astype(x.dtype)

    for i in range(DEPTH):
        xn = rmsnorm(h, mix_norm[i])
        z = xn @ w_in[i]
        o1 = D_SSM
        o2 = o1 + Q_LORA
        o3 = o2 + KV_LORA
        u, q_a, kv_a, k_pe = z[..., :o1], z[..., o1:o2], z[..., o2:o3], z[..., o3:]
        ya = s5_mixer(u, lam_re[i], lam_im[i], log_dt[i], b_re[i], b_im[i], c_re[i], c_im[i],
                      d_skip[i], w_glu[i], b_glu[i])
        yb = mla_mixer(q_a, kv_a, k_pe, q_a_norm[i], w_q_b[i], kv_a_norm[i], w_kv_b[i],
                       cos, sin, chunk_id)
        y = jnp.concatenate([rmsnorm(ya, out_norm_ssm[i]), rmsnorm(yb, out_norm_attn[i])], axis=-1)
        h = h + y @ w_out[i]
        h = h + conv_ffn(rmsnorm(h, ffn_norm[i]), w_up[i], conv_w[i], conv_b[i], w_down[i])

    return rmsnorm(h, final_norm)[:, N_META:]


import jax as _jax
import jax.numpy as _jnp

TWIN_FORMAT = 'train_step'
FWD_PARAMS = ['x', 'meta_tokens', 'mix_norm', 'w_in', 'lam_re', 'lam_im', 'log_dt', 'b_re', 'b_im', 'c_re', 'c_im', 'd_skip', 'w_glu', 'b_glu', 'q_a_norm', 'w_q_b', 'kv_a_norm', 'w_kv_b', 'out_norm_ssm', 'out_norm_attn', 'w_out', 'ffn_norm', 'w_up', 'conv_w', 'conv_b', 'w_down', 'final_norm']
TWIN_WEIGHTS = ['meta_tokens', 'mix_norm', 'w_in', 'lam_re', 'lam_im', 'log_dt', 'b_re', 'b_im', 'c_re', 'c_im', 'd_skip', 'w_glu', 'b_glu', 'q_a_norm', 'w_q_b', 'kv_a_norm', 'w_kv_b', 'out_norm_ssm', 'out_norm_attn', 'w_out', 'ffn_norm', 'w_up', 'conv_w', 'conv_b', 'w_down', 'final_norm']
TWIN_DIFF_INPUT = 'x'
TWIN_INPUTS = ['x', 'meta_tokens', 'mix_norm', 'w_in', 'lam_re', 'lam_im', 'log_dt', 'b_re', 'b_im', 'c_re', 'c_im', 'd_skip', 'w_glu', 'b_glu', 'q_a_norm', 'w_q_b', 'kv_a_norm', 'w_kv_b', 'out_norm_ssm', 'out_norm_attn', 'w_out', 'ffn_norm', 'w_up', 'conv_w', 'conv_b', 'w_down', 'final_norm', 'loss_target', 'm_meta_tokens', 'm_mix_norm', 'm_w_in', 'm_lam_re', 'm_lam_im', 'm_log_dt', 'm_b_re', 'm_b_im', 'm_c_re', 'm_c_im', 'm_d_skip', 'm_w_glu', 'm_b_glu', 'm_q_a_norm', 'm_w_q_b', 'm_kv_a_norm', 'm_w_kv_b', 'm_out_norm_ssm', 'm_out_norm_attn', 'm_w_out', 'm_ffn_norm', 'm_w_up', 'm_conv_w', 'm_conv_b', 'm_w_down', 'm_final_norm', 'v_meta_tokens', 'v_mix_norm', 'v_w_in', 'v_lam_re', 'v_lam_im', 'v_log_dt', 'v_b_re', 'v_b_im', 'v_c_re', 'v_c_im', 'v_d_skip', 'v_w_glu', 'v_b_glu', 'v_q_a_norm', 'v_w_q_b', 'v_kv_a_norm', 'v_w_kv_b', 'v_out_norm_ssm', 'v_out_norm_attn', 'v_w_out', 'v_ffn_norm', 'v_w_up', 'v_conv_w', 'v_conv_b', 'v_w_down', 'v_final_norm']
TWIN_OUTPUTS = ['loss', 'grad_x', 'grad_meta_tokens', 'grad_mix_norm', 'grad_w_in', 'grad_lam_re', 'grad_lam_im', 'grad_log_dt', 'grad_b_re', 'grad_b_im', 'grad_c_re', 'grad_c_im', 'grad_d_skip', 'grad_w_glu', 'grad_b_glu', 'grad_q_a_norm', 'grad_w_q_b', 'grad_kv_a_norm', 'grad_w_kv_b', 'grad_out_norm_ssm', 'grad_out_norm_attn', 'grad_w_out', 'grad_ffn_norm', 'grad_w_up', 'grad_conv_w', 'grad_conv_b', 'grad_w_down', 'grad_final_norm', 'delta_meta_tokens', 'delta_mix_norm', 'delta_w_in', 'delta_lam_re', 'delta_lam_im', 'delta_log_dt', 'delta_b_re', 'delta_b_im', 'delta_c_re', 'delta_c_im', 'delta_d_skip', 'delta_w_glu', 'delta_b_glu', 'delta_q_a_norm', 'delta_w_q_b', 'delta_kv_a_norm', 'delta_w_kv_b', 'delta_out_norm_ssm', 'delta_out_norm_attn', 'delta_w_out', 'delta_ffn_norm', 'delta_w_up', 'delta_conv_w', 'delta_conv_b', 'delta_w_down', 'delta_final_norm', 'new_m_meta_tokens', 'new_m_mix_norm', 'new_m_w_in', 'new_m_lam_re', 'new_m_lam_im', 'new_m_log_dt', 'new_m_b_re', 'new_m_b_im', 'new_m_c_re', 'new_m_c_im', 'new_m_d_skip', 'new_m_w_glu', 'new_m_b_glu', 'new_m_q_a_norm', 'new_m_w_q_b', 'new_m_kv_a_norm', 'new_m_w_kv_b', 'new_m_out_norm_ssm', 'new_m_out_norm_attn', 'new_m_w_out', 'new_m_ffn_norm', 'new_m_w_up', 'new_m_conv_w', 'new_m_conv_b', 'new_m_w_down', 'new_m_final_norm', 'new_v_meta_tokens', 'new_v_mix_norm', 'new_v_w_in', 'new_v_lam_re', 'new_v_lam_im', 'new_v_log_dt', 'new_v_b_re', 'new_v_b_im', 'new_v_c_re', 'new_v_c_im', 'new_v_d_skip', 'new_v_w_glu', 'new_v_b_glu', 'new_v_q_a_norm', 'new_v_w_q_b', 'new_v_kv_a_norm', 'new_v_w_kv_b', 'new_v_out_norm_ssm', 'new_v_out_norm_attn', 'new_v_w_out', 'new_v_ffn_norm', 'new_v_w_up', 'new_v_conv_w', 'new_v_conv_b', 'new_v_w_down', 'new_v_final_norm']
TWIN_LEAF_KINDS = {'loss': 'loss', 'grad_x': 'grad_x', 'grad_meta_tokens': 'grad_w', 'grad_mix_norm': 'grad_w', 'grad_w_in': 'grad_w', 'grad_lam_re': 'grad_w', 'grad_lam_im': 'grad_w', 'grad_log_dt': 'grad_w', 'grad_b_re': 'grad_w', 'grad_b_im': 'grad_w', 'grad_c_re': 'grad_w', 'grad_c_im': 'grad_w', 'grad_d_skip': 'grad_w', 'grad_w_glu': 'grad_w', 'grad_b_glu': 'grad_w', 'grad_q_a_norm': 'grad_w', 'grad_w_q_b': 'grad_w', 'grad_kv_a_norm': 'grad_w', 'grad_w_kv_b': 'grad_w', 'grad_out_norm_ssm': 'grad_w', 'grad_out_norm_attn': 'grad_w', 'grad_w_out': 'grad_w', 'grad_ffn_norm': 'grad_w', 'grad_w_up': 'grad_w', 'grad_conv_w': 'grad_w', 'grad_conv_b': 'grad_w', 'grad_w_down': 'grad_w', 'grad_final_norm': 'grad_w', 'delta_meta_tokens': 'delta_w', 'delta_mix_norm': 'delta_w', 'delta_w_in': 'delta_w', 'delta_lam_re': 'delta_w', 'delta_lam_im': 'delta_w', 'delta_log_dt': 'delta_w', 'delta_b_re': 'delta_w', 'delta_b_im': 'delta_w', 'delta_c_re': 'delta_w', 'delta_c_im': 'delta_w', 'delta_d_skip': 'delta_w', 'delta_w_glu': 'delta_w', 'delta_b_glu': 'delta_w', 'delta_q_a_norm': 'delta_w', 'delta_w_q_b': 'delta_w', 'delta_kv_a_norm': 'delta_w', 'delta_w_kv_b': 'delta_w', 'delta_out_norm_ssm': 'delta_w', 'delta_out_norm_attn': 'delta_w', 'delta_w_out': 'delta_w', 'delta_ffn_norm': 'delta_w', 'delta_w_up': 'delta_w', 'delta_conv_w': 'delta_w', 'delta_conv_b': 'delta_w', 'delta_w_down': 'delta_w', 'delta_final_norm': 'delta_w', 'new_m_meta_tokens': 'new_m', 'new_m_mix_norm': 'new_m', 'new_m_w_in': 'new_m', 'new_m_lam_re': 'new_m', 'new_m_lam_im': 'new_m', 'new_m_log_dt': 'new_m', 'new_m_b_re': 'new_m', 'new_m_b_im': 'new_m', 'new_m_c_re': 'new_m', 'new_m_c_im': 'new_m', 'new_m_d_skip': 'new_m', 'new_m_w_glu': 'new_m', 'new_m_b_glu': 'new_m', 'new_m_q_a_norm': 'new_m', 'new_m_w_q_b': 'new_m', 'new_m_kv_a_norm': 'new_m', 'new_m_w_kv_b': 'new_m', 'new_m_out_norm_ssm': 'new_m', 'new_m_out_norm_attn': 'new_m', 'new_m_w_out': 'new_m', 'new_m_ffn_norm': 'new_m', 'new_m_w_up': 'new_m', 'new_m_conv_w': 'new_m', 'new_m_conv_b': 'new_m', 'new_m_w_down': 'new_m', 'new_m_final_norm': 'new_m', 'new_v_meta_tokens': 'new_v', 'new_v_mix_norm': 'new_v', 'new_v_w_in': 'new_v', 'new_v_lam_re': 'new_v', 'new_v_lam_im': 'new_v', 'new_v_log_dt': 'new_v', 'new_v_b_re': 'new_v', 'new_v_b_im': 'new_v', 'new_v_c_re': 'new_v', 'new_v_c_im': 'new_v', 'new_v_d_skip': 'new_v', 'new_v_w_glu': 'new_v', 'new_v_b_glu': 'new_v', 'new_v_q_a_norm': 'new_v', 'new_v_w_q_b': 'new_v', 'new_v_kv_a_norm': 'new_v', 'new_v_w_kv_b': 'new_v', 'new_v_out_norm_ssm': 'new_v', 'new_v_out_norm_attn': 'new_v', 'new_v_w_out': 'new_v', 'new_v_ffn_norm': 'new_v', 'new_v_w_up': 'new_v', 'new_v_conv_w': 'new_v', 'new_v_conv_b': 'new_v', 'new_v_w_down': 'new_v', 'new_v_final_norm': 'new_v'}


def _forward(args):
    return _fwd_reference(*[args[k] for k in FWD_PARAMS])


def _output_shape():
    out = _jax.eval_shape(lambda: _forward(_fwd_setup_inputs(0)))
    return out.shape, out.dtype

N_MICROBATCH = 1
ADAM_LR = 0.001
ADAM_B1 = 0.9
ADAM_B2 = 0.999
ADAM_EPS = 1e-08
ADAM_WD = 0.01
ADAM_STEP = 10
PER_EXAMPLE_BATCH_AXIS = {'x': 0, 'loss_target': 0}
SHARED_INPUTS = []
_WEIGHT_DTYPES = {'meta_tokens': _jnp.float32, 'mix_norm': _jnp.float32, 'w_in': _jnp.float32, 'lam_re': _jnp.float32, 'lam_im': _jnp.float32, 'log_dt': _jnp.float32, 'b_re': _jnp.float32, 'b_im': _jnp.float32, 'c_re': _jnp.float32, 'c_im': _jnp.float32, 'd_skip': _jnp.float32, 'w_glu': _jnp.float32, 'b_glu': _jnp.float32, 'q_a_norm': _jnp.float32, 'w_q_b': _jnp.float32, 'kv_a_norm': _jnp.float32, 'w_kv_b': _jnp.float32, 'out_norm_ssm': _jnp.float32, 'out_norm_attn': _jnp.float32, 'w_out': _jnp.float32, 'ffn_norm': _jnp.float32, 'w_up': _jnp.float32, 'conv_w': _jnp.float32, 'conv_b': _jnp.float32, 'w_down': _jnp.float32, 'final_norm': _jnp.float32}
MOMENT_SCALE = {'meta_tokens': 3.642886e-03, 'mix_norm': 6.539534e-02, 'w_in': 6.887540e-02, 'lam_re': 2.245174e-03, 'lam_im': 2.441625e-03, 'log_dt': 2.379970e+00, 'b_re': 1.641651e-03, 'b_im': 1.653824e-03, 'c_re': 3.211165e-03, 'c_im': 3.194102e-03, 'd_skip': 5.844810e-02, 'w_glu': 1.438019e-02, 'b_glu': 2.445420e-02, 'q_a_norm': 7.689705e-02, 'w_q_b': 3.945592e-02, 'kv_a_norm': 1.523933e-01, 'w_kv_b': 4.642859e-02, 'out_norm_ssm': 5.631848e-02, 'out_norm_attn': 4.769157e-02, 'w_out': 4.876266e-02, 'ffn_norm': 3.628824e-02, 'w_up': 1.556001e-02, 'conv_w': 1.584983e-02, 'conv_b': 1.521699e-02, 'w_down': 2.501769e-02, 'final_norm': 8.011271e+00}


def _to_microbatches(a, axis):
    t = _jnp.moveaxis(a, axis, 0)
    t = t.reshape((N_MICROBATCH, t.shape[0] // N_MICROBATCH) + t.shape[1:])
    return _jnp.moveaxis(t, 1, axis + 1)


def setup_inputs(seed: int = 0) -> dict:
    inp = _fwd_setup_inputs(seed)
    key = _jax.random.fold_in(_jax.random.key(seed), 7919)
    shape, _ = _output_shape()
    out = dict(inp)
    out["loss_target"] = _jax.random.normal(_jax.random.fold_in(key, 0), shape, _jnp.float32)
    for i, name in enumerate(TWIN_WEIGHTS):
        w = inp[name].astype(_jnp.float32)
        if MOMENT_SCALE is None:
            s = _jnp.sqrt(_jnp.mean(_jnp.square(w)) + 1e-30)
        else:
            s = MOMENT_SCALE[name]
        km, kv = _jax.random.split(_jax.random.fold_in(key, i + 1))
        out[name] = w
        out["m_" + name] = s * _jax.random.normal(km, w.shape, _jnp.float32)
        out["v_" + name] = (s * s) * _jax.random.uniform(kv, w.shape, _jnp.float32, 0.5, 1.5)
    if N_MICROBATCH > 1:
        for name, axis in PER_EXAMPLE_BATCH_AXIS.items():
            out[name] = _to_microbatches(out[name], axis)
    return {'x': out['x'], 'meta_tokens': out['meta_tokens'], 'mix_norm': out['mix_norm'], 'w_in': out['w_in'], 'lam_re': out['lam_re'], 'lam_im': out['lam_im'], 'log_dt': out['log_dt'], 'b_re': out['b_re'], 'b_im': out['b_im'], 'c_re': out['c_re'], 'c_im': out['c_im'], 'd_skip': out['d_skip'], 'w_glu': out['w_glu'], 'b_glu': out['b_glu'], 'q_a_norm': out['q_a_norm'], 'w_q_b': out['w_q_b'], 'kv_a_norm': out['kv_a_norm'], 'w_kv_b': out['w_kv_b'], 'out_norm_ssm': out['out_norm_ssm'], 'out_norm_attn': out['out_norm_attn'], 'w_out': out['w_out'], 'ffn_norm': out['ffn_norm'], 'w_up': out['w_up'], 'conv_w': out['conv_w'], 'conv_b': out['conv_b'], 'w_down': out['w_down'], 'final_norm': out['final_norm'], 'loss_target': out['loss_target'], 'm_meta_tokens': out['m_meta_tokens'], 'm_mix_norm': out['m_mix_norm'], 'm_w_in': out['m_w_in'], 'm_lam_re': out['m_lam_re'], 'm_lam_im': out['m_lam_im'], 'm_log_dt': out['m_log_dt'], 'm_b_re': out['m_b_re'], 'm_b_im': out['m_b_im'], 'm_c_re': out['m_c_re'], 'm_c_im': out['m_c_im'], 'm_d_skip': out['m_d_skip'], 'm_w_glu': out['m_w_glu'], 'm_b_glu': out['m_b_glu'], 'm_q_a_norm': out['m_q_a_norm'], 'm_w_q_b': out['m_w_q_b'], 'm_kv_a_norm': out['m_kv_a_norm'], 'm_w_kv_b': out['m_w_kv_b'], 'm_out_norm_ssm': out['m_out_norm_ssm'], 'm_out_norm_attn': out['m_out_norm_attn'], 'm_w_out': out['m_w_out'], 'm_ffn_norm': out['m_ffn_norm'], 'm_w_up': out['m_w_up'], 'm_conv_w': out['m_conv_w'], 'm_conv_b': out['m_conv_b'], 'm_w_down': out['m_w_down'], 'm_final_norm': out['m_final_norm'], 'v_meta_tokens': out['v_meta_tokens'], 'v_mix_norm': out['v_mix_norm'], 'v_w_in': out['v_w_in'], 'v_lam_re': out['v_lam_re'], 'v_lam_im': out['v_lam_im'], 'v_log_dt': out['v_log_dt'], 'v_b_re': out['v_b_re'], 'v_b_im': out['v_b_im'], 'v_c_re': out['v_c_re'], 'v_c_im': out['v_c_im'], 'v_d_skip': out['v_d_skip'], 'v_w_glu': out['v_w_glu'], 'v_b_glu': out['v_b_glu'], 'v_q_a_norm': out['v_q_a_norm'], 'v_w_q_b': out['v_w_q_b'], 'v_kv_a_norm': out['v_kv_a_norm'], 'v_w_kv_b': out['v_w_kv_b'], 'v_out_norm_ssm': out['v_out_norm_ssm'], 'v_out_norm_attn': out['v_out_norm_attn'], 'v_w_out': out['v_w_out'], 'v_ffn_norm': out['v_ffn_norm'], 'v_w_up': out['v_w_up'], 'v_conv_w': out['v_conv_w'], 'v_conv_b': out['v_conv_b'], 'v_w_down': out['v_w_down'], 'v_final_norm': out['v_final_norm']}


def _loss(weights, diff, rest, loss_target):
    with _jax.named_scope("forward"):
        args = {**rest, TWIN_DIFF_INPUT: diff, **{k: w.astype(_WEIGHT_DTYPES[k]) for k, w in weights.items()}}
        y = _forward(args)
    with _jax.named_scope("loss_head"):
        err = _jnp.square(y.astype(_jnp.float32) - loss_target)
        return 0.5 * _jnp.sum(_jnp.mean(err, axis=-1)) if err.ndim else 0.5 * err


def _adamw(w, g, m, v):
    m = ADAM_B1 * m + (1.0 - ADAM_B1) * g
    v = ADAM_B2 * v + (1.0 - ADAM_B2) * _jnp.square(g)
    m_hat = m / (1.0 - ADAM_B1 ** ADAM_STEP)
    v_hat = v / (1.0 - ADAM_B2 ** ADAM_STEP)
    delta = -ADAM_LR * (m_hat / (_jnp.sqrt(v_hat) + ADAM_EPS) + ADAM_WD * w)
    return delta, m, v


def reference(x, meta_tokens, mix_norm, w_in, lam_re, lam_im, log_dt, b_re, b_im, c_re, c_im, d_skip, w_glu, b_glu, q_a_norm, w_q_b, kv_a_norm, w_kv_b, out_norm_ssm, out_norm_attn, w_out, ffn_norm, w_up, conv_w, conv_b, w_down, final_norm, loss_target, m_meta_tokens, m_mix_norm, m_w_in, m_lam_re, m_lam_im, m_log_dt, m_b_re, m_b_im, m_c_re, m_c_im, m_d_skip, m_w_glu, m_b_glu, m_q_a_norm, m_w_q_b, m_kv_a_norm, m_w_kv_b, m_out_norm_ssm, m_out_norm_attn, m_w_out, m_ffn_norm, m_w_up, m_conv_w, m_conv_b, m_w_down, m_final_norm, v_meta_tokens, v_mix_norm, v_w_in, v_lam_re, v_lam_im, v_log_dt, v_b_re, v_b_im, v_c_re, v_c_im, v_d_skip, v_w_glu, v_b_glu, v_q_a_norm, v_w_q_b, v_kv_a_norm, v_w_kv_b, v_out_norm_ssm, v_out_norm_attn, v_w_out, v_ffn_norm, v_w_up, v_conv_w, v_conv_b, v_w_down, v_final_norm):
    given = dict(x=x, meta_tokens=meta_tokens, mix_norm=mix_norm, w_in=w_in, lam_re=lam_re, lam_im=lam_im, log_dt=log_dt, b_re=b_re, b_im=b_im, c_re=c_re, c_im=c_im, d_skip=d_skip, w_glu=w_glu, b_glu=b_glu, q_a_norm=q_a_norm, w_q_b=w_q_b, kv_a_norm=kv_a_norm, w_kv_b=w_kv_b, out_norm_ssm=out_norm_ssm, out_norm_attn=out_norm_attn, w_out=w_out, ffn_norm=ffn_norm, w_up=w_up, conv_w=conv_w, conv_b=conv_b, w_down=w_down, final_norm=final_norm, loss_target=loss_target, m_meta_tokens=m_meta_tokens, m_mix_norm=m_mix_norm, m_w_in=m_w_in, m_lam_re=m_lam_re, m_lam_im=m_lam_im, m_log_dt=m_log_dt, m_b_re=m_b_re, m_b_im=m_b_im, m_c_re=m_c_re, m_c_im=m_c_im, m_d_skip=m_d_skip, m_w_glu=m_w_glu, m_b_glu=m_b_glu, m_q_a_norm=m_q_a_norm, m_w_q_b=m_w_q_b, m_kv_a_norm=m_kv_a_norm, m_w_kv_b=m_w_kv_b, m_out_norm_ssm=m_out_norm_ssm, m_out_norm_attn=m_out_norm_attn, m_w_out=m_w_out, m_ffn_norm=m_ffn_norm, m_w_up=m_w_up, m_conv_w=m_conv_w, m_conv_b=m_conv_b, m_w_down=m_w_down, m_final_norm=m_final_norm, v_meta_tokens=v_meta_tokens, v_mix_norm=v_mix_norm, v_w_in=v_w_in, v_lam_re=v_lam_re, v_lam_im=v_lam_im, v_log_dt=v_log_dt, v_b_re=v_b_re, v_b_im=v_b_im, v_c_re=v_c_re, v_c_im=v_c_im, v_d_skip=v_d_skip, v_w_glu=v_w_glu, v_b_glu=v_b_glu, v_q_a_norm=v_q_a_norm, v_w_q_b=v_w_q_b, v_kv_a_norm=v_kv_a_norm, v_w_kv_b=v_w_kv_b, v_out_norm_ssm=v_out_norm_ssm, v_out_norm_attn=v_out_norm_attn, v_w_out=v_w_out, v_ffn_norm=v_ffn_norm, v_w_up=v_w_up, v_conv_w=v_conv_w, v_conv_b=v_conv_b, v_w_down=v_w_down, v_final_norm=v_final_norm)
    weights = {n: given[n] for n in TWIN_WEIGHTS}
    shared = {n: given[n] for n in SHARED_INPUTS}
    per_example = {n: given[n] for n in ['x']}
    grad_fn = _jax.value_and_grad(_loss, argnums=(0, 1))

    def one_microbatch(ex, loss_target):
        ex = dict(ex)
        diff = ex.pop(TWIN_DIFF_INPUT)
        return grad_fn(weights, diff, {**shared, **ex}, loss_target)

    if N_MICROBATCH == 1:
        loss, (grad_w, grad_x) = one_microbatch(per_example, given["loss_target"])
    else:
        def body(carry, xs):
            loss_sum, grad_sum = carry
            l_k, (gw_k, gx_k) = one_microbatch(xs[0], xs[1])
            with _jax.named_scope("update"):
                return (loss_sum + l_k, _jax.tree.map(_jnp.add, grad_sum, gw_k)), gx_k

        init = (_jnp.zeros((), _jnp.float32), _jax.tree.map(_jnp.zeros_like, weights))
        (loss, grad_w), grad_x = _jax.lax.scan(body, init, (per_example, given["loss_target"]))
    with _jax.named_scope("update"):
        delta_w, new_m, new_v = {}, {}, {}
        for n in TWIN_WEIGHTS:
            delta_w[n], new_m[n], new_v[n] = _adamw(weights[n], grad_w[n], given["m_" + n], given["v_" + n])
    return (loss, grad_x, *[grad_w[n] for n in TWIN_WEIGHTS], *[delta_w[n] for n in TWIN_WEIGHTS],
            *[new_m[n] for n in TWIN_WEIGHTS], *[new_v[n] for n in TWIN_WEIGHTS])
```

```python
import functools
import math
from typing import NamedTuple

import jax
import jax.numpy as jnp
from jax import lax
from jax.experimental import pallas as pl
from jax.experimental.pallas import tpu as pltpu

F32, BF16 = jnp.float32, jnp.bfloat16
MESH = pl.DeviceIdType.MESH
LANE = 128
ROW_ALIGN = 16
N_META = 16
PAD = 112
CHUNK = 64
SSM_GROUP = 16
SSM_STATE = 64
GROUPS_PER_BLOCK = 8
QK_NOPE, QK_ROPE, V_HEAD = 128, 64, 128
HEAD_SLOT = 256
ROPE_BASE = 10000.0
EPS = 1e-6
ADAM_LR, ADAM_B1, ADAM_B2, ADAM_EPS, ADAM_WD, ADAM_STEP = 0.001, 0.9, 0.999, 1e-08, 0.01, 10
DT_F32_BLOCK_BYTES = 1 << 20


class Cfg(NamedTuple):
    D: int
    S: int
    DS: int
    H: int
    QL: int
    KVL: int
    F: int

    @property
    def LP(self):
        return PAD + N_META + self.S

    @property
    def G(self):
        return self.DS // SSM_GROUP

    @property
    def NB(self):
        return self.G // GROUPS_PER_BLOCK

    @property
    def NL(self):
        return 2 * self.G * SSM_STATE

    @property
    def DATTN(self):
        return self.H * V_HEAD

    @property
    def DMIX(self):
        return self.DS + self.DATTN

    @property
    def DIN(self):
        return self.DS + self.QL + self.KVL + QK_ROPE

    @property
    def DINP(self):
        return self.DS + self.QL + self.KVL + LANE

    @property
    def FQ(self):
        return -(-(self.F // 4) // LANE) * LANE

    @property
    def FP(self):
        return 4 * self.FQ


PROD = Cfg(D=2048, S=2048, DS=1024, H=8, QL=512, KVL=256, F=5504)

WEIGHTS = ['meta_tokens', 'mix_norm', 'w_in', 'lam_re', 'lam_im', 'log_dt', 'b_re', 'b_im', 'c_re', 'c_im', 'd_skip',
           'w_glu', 'b_glu', 'q_a_norm', 'w_q_b', 'kv_a_norm', 'w_kv_b', 'out_norm_ssm', 'out_norm_attn', 'w_out',
           'ffn_norm', 'w_up', 'conv_w', 'conv_b', 'w_down', 'final_norm']
BIG = ['w_in', 'w_glu', 'w_q_b', 'w_kv_b', 'w_out', 'w_up', 'w_down']
SMALL = [n for n in WEIGHTS if n not in BIG]


def _pc(body, **kw):
    return pl.pallas_call(body, **kw)


def _tile(n, target, align=LANE):
    best = None
    d = align
    while d <= min(n, target):
        if n % d == 0:
            best = d
        d += align
    return best if best is not None else n


def _row_tile(rows, cols):
    return _tile(rows, max(ROW_ALIGN, DT_F32_BLOCK_BYTES // (4 * cols)), ROW_ALIGN)


def _mm(a, b, *, name, ta=False, tb=False, tm=None, tn=512, tk=None, out_dtype=F32, res=None,
        a_idx=None, b_idx=None, dims=None):
    if dims is None:
        m, k = (a.shape[1], a.shape[0]) if ta else a.shape
        n = b.shape[0] if tb else b.shape[1]
    else:
        m, n, k = dims
    tm = _tile(m, tm or m, LANE if ta else ROW_ALIGN)
    tn = _tile(n, tn)
    tk = _tile(k, tk or k, ROW_ALIGN if (ta and not tb) else LANE)
    nm, nn, nk = m // tm, n // tn, k // tk
    a_idx = a_idx or ((lambda i, j, kk: (kk, i)) if ta else (lambda i, j, kk: (i, kk)))
    b_idx = b_idx or ((lambda i, j, kk: (j, kk)) if tb else (lambda i, j, kk: (kk, j)))
    dn = (((0 if ta else 1,), (1 if tb else 0,)), ((), ()))

    def body(*refs):
        a_ref, b_ref = refs[0], refs[1]
        r_ref = refs[2] if res is not None else None
        o_ref = refs[3] if res is not None else refs[2]
        d = lax.dot_general(a_ref[...].astype(BF16), b_ref[...].astype(BF16), dn, preferred_element_type=F32)

        def finish(r):
            if r_ref is not None:
                r = r + r_ref[...].astype(F32)
            o_ref[...] = r.astype(out_dtype)

        if nk == 1:
            finish(d)
        else:
            acc = refs[-1]
            kk = pl.program_id(2)

            @pl.when(kk == 0)
            def _():
                acc[...] = d

            @pl.when(kk > 0)
            def _():
                acc[...] += d

            @pl.when(kk == nk - 1)
            def _():
                finish(acc[...])

    in_specs = [pl.BlockSpec((tk, tm) if ta else (tm, tk), a_idx), pl.BlockSpec((tn, tk) if tb else (tk, tn), b_idx)]
    args = [a, b]
    if res is not None:
        in_specs.append(pl.BlockSpec((tm, tn), lambda i, j, kk: (i, j)))
        args.append(res)
    return _pc(body, name=name, grid=(nm, nn, nk), in_specs=in_specs,
               out_specs=pl.BlockSpec((tm, tn), lambda i, j, kk: (i, j)),
               out_shape=jax.ShapeDtypeStruct((m, n), out_dtype),
               scratch_shapes=[pltpu.VMEM((tm, tn), F32)] if nk > 1 else [],
               compiler_params=pltpu.CompilerParams(dimension_semantics=("parallel", "parallel", "arbitrary")))(*args)


def _ew(fn, ins, vecs, outs, sums=(), *, name, tm=None):
    ins = [x if isinstance(x, tuple) else (x, x.shape[1], 0) for x in ins]
    rows = ins[0][0].shape[0]
    cmax = max([c for _, c, _ in ins] + [c for c, _ in outs])
    tm = tm or _row_tile(rows, cmax)
    n_in, n_vec, n_out, n_sum = len(ins), len(vecs), len(outs), len(sums)

    def body(*refs):
        i = pl.program_id(0)
        rid = i * tm + lax.broadcasted_iota(jnp.int32, (tm, 1), 0)
        vals = [r[...] for r in refs[:n_in + n_vec]]
        res = fn(rid, *vals)
        res = res if isinstance(res, (tuple, list)) else (res,)
        o_refs = refs[n_in + n_vec:]
        for o_ref, r in zip(o_refs[:n_out], res[:n_out]):
            o_ref[...] = r.astype(o_ref.dtype)
        for o_ref, r in zip(o_refs[n_out:], res[n_out:]):
            part = jnp.sum(r.astype(F32), axis=0, keepdims=True)

            @pl.when(i == 0)
            def _():
                o_ref[...] = part

            @pl.when(i > 0)
            def _():
                o_ref[...] += part

    in_specs = [pl.BlockSpec((tm, c), functools.partial(lambda i, cb: (i, cb), cb=cb)) for _, c, cb in ins]
    in_specs += [pl.BlockSpec(v.shape, functools.partial(lambda i, nd: (0,) * nd, nd=v.ndim)) for v in vecs]
    out_specs = [pl.BlockSpec((tm, c), lambda i: (i, 0)) for c, _ in outs] + [pl.BlockSpec((1, c), lambda i: (0, 0)) for c in sums]
    out_shape = [jax.ShapeDtypeStruct((rows, c), dt) for c, dt in outs] + [jax.ShapeDtypeStruct((1, c), F32) for c in sums]
    return _pc(body, name=name, grid=(rows // tm,), in_specs=in_specs, out_specs=out_specs, out_shape=out_shape,
               compiler_params=pltpu.CompilerParams(dimension_semantics=("arbitrary",)))(*[x for x, _, _ in ins], *vecs)


def _rms_parts(x, g):
    r = lax.rsqrt(jnp.mean(x * x, axis=-1, keepdims=True) + EPS)
    return x * r, r


def _rms_bwd_block(x, g, dy):
    xhat, r = _rms_parts(x, g)
    dxhat = dy * g
    dx = r * (dxhat - xhat * jnp.mean(dxhat * xhat, axis=-1, keepdims=True))
    return dx, dy * xhat


def _rms_fwd(x, g, *, name):
    c = x[1] if isinstance(x, tuple) else x.shape[1]
    return _ew(lambda rid, xv, gv: _rms_parts(xv.astype(F32), gv)[0] * gv, [x], [g], [(c, BF16)], name=name)[0]


def _rms_bwd(x, g, dy, *, name, res=None, mask=False, with_bf16=False):
    c = x[1] if isinstance(x, tuple) else x.shape[1]

    def fn(rid, xv, dyv, *rest):
        gv = rest[-1]
        dx, dg = _rms_bwd_block(xv.astype(F32), gv, dyv.astype(F32))
        if res is not None:
            dx = dx + rest[0]
        if mask:
            dx = jnp.where(rid >= PAD, dx, 0.0)
        return (dx, dx, dg) if with_bf16 else (dx, dg)

    ins = [x, dy] + ([res] if res is not None else [])
    outs = [(c, F32)] + ([(c, BF16)] if with_bf16 else [])
    return _ew(fn, ins, [g], outs, [c], name=name)


def _s5_scan(bu, a_l, *, reverse, name):
    lp, nl = bu.shape
    w = GROUPS_PER_BLOCK * SSM_STATE
    unroll = 8

    def body(bu_ref, a_ref, hs_ref):
        ar = a_ref[:, :w]
        ai = -a_ref[:, w:] if reverse else a_ref[:, w:]

        def step(n, carry):
            hr, hi = carry
            for q in range(unroll):
                t = n * unroll + q
                t = lp - 1 - t if reverse else t
                nr = ar * hr - ai * hi + bu_ref[pl.ds(t, 1), :w]
                ni = ar * hi + ai * hr + bu_ref[pl.ds(t, 1), w:]
                hs_ref[pl.ds(t, 1), :w] = nr
                hs_ref[pl.ds(t, 1), w:] = ni
                hr, hi = nr, ni
            return hr, hi

        z = jnp.zeros((1, w), F32)
        lax.fori_loop(0, lp // unroll, step, (z, z))

    return _pc(body, name=name, grid=(nl // (2 * w),),
               in_specs=[pl.BlockSpec((lp, 2 * w), lambda j: (0, j)), pl.BlockSpec((1, 2 * w), lambda j: (0, j))],
               out_specs=pl.BlockSpec((lp, 2 * w), lambda j: (0, j)),
               out_shape=jax.ShapeDtypeStruct((lp, nl), F32),
               compiler_params=pltpu.CompilerParams(dimension_semantics=("parallel",)))(bu, a_l)


def _s5_da(gs, hs, cfg, *, name):
    lp, nl = gs.shape
    w = GROUPS_PER_BLOCK * SSM_STATE
    tc = w // 2
    per = 2 * w // tc

    def body(gr_ref, gi_ref, hr_ref, hi_ref, dre_ref, dim_ref):
        keep = lax.broadcasted_iota(jnp.int32, (lp, 1), 0) >= 1
        hr = jnp.where(keep, pltpu.roll(hr_ref[...], 1, 0), 0.0)
        hi = jnp.where(keep, pltpu.roll(hi_ref[...], 1, 0), 0.0)
        gr, gi = gr_ref[...], gi_ref[...]
        dre_ref[...] = jnp.sum(gr * hr + gi * hi, axis=0, keepdims=True)
        dim_ref[...] = jnp.sum(gi * hr - gr * hi, axis=0, keepdims=True)

    re_blk = pl.BlockSpec((lp, tc), lambda j, q: (0, per * j + q))
    im_blk = pl.BlockSpec((lp, tc), lambda j, q: (0, per * j + per // 2 + q))
    out_blk = pl.BlockSpec((1, tc), lambda j, q: (0, (per // 2) * j + q))
    dre, dim = _pc(body, name=name, grid=(cfg.NB, per // 2), in_specs=[re_blk, im_blk, re_blk, im_blk],
                   out_specs=[out_blk, out_blk], out_shape=[jax.ShapeDtypeStruct((1, nl // 2), F32)] * 2,
                   compiler_params=pltpu.CompilerParams(dimension_semantics=("parallel", "parallel")))(gs, gs, hs, hs)
    return jnp.stack([dre.reshape(cfg.NB, w), dim.reshape(cfg.NB, w)], axis=1).reshape(1, nl)


def _conv_gate(pre, cw, cb):
    return cw[0:1] * pltpu.roll(pre, 2, 0) + cw[1:2] * pltpu.roll(pre, 1, 0) + cw[2:3] * pre + cb


def _conv_fwd(up, cw, cb, *, name):
    lp, fp2 = up.shape
    fp = fp2 // 2
    tc = _tile(fp, 256)
    nb = fp // tc

    def body(pre_ref, val_ref, cw_ref, cb_ref, o_ref):
        gate = _conv_gate(pre_ref[...], cw_ref[...], cb_ref[...])
        o_ref[...] = (jax.nn.silu(gate) * val_ref[...]).astype(BF16)

    return _pc(body, name=name, grid=(nb,),
               in_specs=[pl.BlockSpec((lp, tc), lambda j: (0, j)), pl.BlockSpec((lp, tc), lambda j: (0, nb + j)),
                         pl.BlockSpec((3, tc), lambda j: (0, j)), pl.BlockSpec((1, tc), lambda j: (0, j))],
               out_specs=pl.BlockSpec((lp, tc), lambda j: (0, j)),
               out_shape=jax.ShapeDtypeStruct((lp, fp), BF16),
               compiler_params=pltpu.CompilerParams(dimension_semantics=("parallel",)))(up, up, cw, cb)


def _conv_bwd(up, dact, cw, cb, *, name):
    lp, fp2 = up.shape
    fp = fp2 // 2
    tc = _tile(fp, 256)
    nb = fp // tc

    def body(pre_ref, val_ref, da_ref, cw_ref, cb_ref, dup_ref, dcw_ref, dcb_ref):
        which = pl.program_id(0)
        pre, val, da, cwv = pre_ref[...], val_ref[...], da_ref[...].astype(F32), cw_ref[...]
        gate = _conv_gate(pre, cwv, cb_ref[...])
        sg = jax.nn.sigmoid(gate)
        silu = gate * sg

        @pl.when(which == 1)
        def _():
            dup_ref[...] = (da * silu).astype(BF16)

        @pl.when(which == 0)
        def _():
            dgate = da * val * (sg * (1.0 + gate * (1.0 - sg)))
            dpre = cwv[2:3] * dgate + cwv[1:2] * pltpu.roll(dgate, lp - 1, 0) + cwv[0:1] * pltpu.roll(dgate, lp - 2, 0)
            dup_ref[...] = dpre.astype(BF16)
            dcb_ref[...] = jnp.sum(dgate, axis=0, keepdims=True)
            dcw_ref[0:1, :] = jnp.sum(dgate * pltpu.roll(pre, 2, 0), axis=0, keepdims=True)
            dcw_ref[1:2, :] = jnp.sum(dgate * pltpu.roll(pre, 1, 0), axis=0, keepdims=True)
            dcw_ref[2:3, :] = jnp.sum(dgate * pre, axis=0, keepdims=True)

    return _pc(body, name=name, grid=(2, nb),
               in_specs=[pl.BlockSpec((lp, tc), lambda s, j: (0, j)), pl.BlockSpec((lp, tc), lambda s, j: (0, nb + j)),
                         pl.BlockSpec((lp, tc), lambda s, j: (0, j)),
                         pl.BlockSpec((3, tc), lambda s, j: (0, j)), pl.BlockSpec((1, tc), lambda s, j: (0, j))],
               out_specs=[pl.BlockSpec((lp, tc), lambda s, j: (0, s * nb + j)),
                          pl.BlockSpec((3, tc), lambda s, j: (0, j * (1 - s) + (nb - 1) * s)),
                          pl.BlockSpec((1, tc), lambda s, j: (0, j * (1 - s) + (nb - 1) * s))],
               out_shape=[jax.ShapeDtypeStruct((lp, fp2), BF16), jax.ShapeDtypeStruct((3, fp), F32),
                          jax.ShapeDtypeStruct((1, fp), F32)],
               compiler_params=pltpu.CompilerParams(dimension_semantics=("arbitrary", "arbitrary")))(up, up, dact, cw, cb)


def _attn_mask(i, tq, lp):
    qrow = i * tq + lax.broadcasted_iota(jnp.int32, (tq, 1), 0)
    krow = lax.broadcasted_iota(jnp.int32, (1, lp), 1)
    return (krow >= PAD) & ((krow // CHUNK) <= (qrow // CHUNK)), qrow >= PAD


def _attn_scores(q, kn, kr, i, tq, lp, scale):
    nt = (((1,), (1,)), ((), ()))
    s = lax.dot_general(q[:, :QK_NOPE], kn, nt, preferred_element_type=F32)
    s = s + lax.dot_general(q[:, QK_NOPE:], kr, nt, preferred_element_type=F32)
    mask, qvalid = _attn_mask(i, tq, lp)
    return jnp.where(mask, s * scale, jnp.finfo(F32).min), qvalid


def _attn_fwd(qx, kv, kr, cfg, *, name):
    lp, h = cfg.LP, cfg.H
    tq = _tile(lp, 272, ROW_ALIGN)
    scale = 1.0 / math.sqrt(QK_NOPE + QK_ROPE)

    def body(q_ref, kn_ref, v_ref, kr_ref, o_ref, lse_ref):
        i = pl.program_id(1)
        s, qvalid = _attn_scores(q_ref[...], kn_ref[...], kr_ref[...], i, tq, lp, scale)
        m = jnp.max(s, axis=-1, keepdims=True)
        p = jnp.exp(s - m)
        l = jnp.sum(p, axis=-1, keepdims=True)
        o = jnp.dot(p.astype(BF16), v_ref[...], preferred_element_type=F32) / l
        o_ref[...] = jnp.where(qvalid, o, 0.0)
        lse_ref[...] = m + jnp.log(l)

    return _pc(body, name=name, grid=(h, lp // tq),
               in_specs=[pl.BlockSpec((tq, HEAD_SLOT), lambda hh, i: (i, hh)),
                         pl.BlockSpec((lp, QK_NOPE), lambda hh, i: (0, 2 * hh)),
                         pl.BlockSpec((lp, V_HEAD), lambda hh, i: (0, 2 * hh + 1)),
                         pl.BlockSpec((lp, LANE), lambda hh, i: (0, 0))],
               out_specs=[pl.BlockSpec((tq, V_HEAD), lambda hh, i: (i, hh)),
                          pl.BlockSpec((None, tq, 1), lambda hh, i: (hh, i, 0))],
               out_shape=[jax.ShapeDtypeStruct((lp, h * V_HEAD), F32), jax.ShapeDtypeStruct((h, lp, 1), F32)],
               compiler_params=pltpu.CompilerParams(dimension_semantics=("parallel", "parallel")))(qx, kv, kv, kr)


def _attn_bwd(qx, kv, kr, o, lse, do, cfg, *, name):
    lp, h = cfg.LP, cfg.H
    tq = _tile(lp, 272, ROW_ALIGN)
    scale = 1.0 / math.sqrt(QK_NOPE + QK_ROPE)
    tn_dims = (((0,), (0,)), ((), ()))

    def body(q_ref, kn_ref, v_ref, kr_ref, o_ref, lse_ref, do_ref, dq_ref, dkn_ref, dv_ref, dkr_ref):
        hh, i = pl.program_id(0), pl.program_id(1)
        q, kn, v, krv = q_ref[...], kn_ref[...], v_ref[...], kr_ref[...]
        s, qvalid = _attn_scores(q, kn, krv, i, tq, lp, scale)
        dov = jnp.where(qvalid, do_ref[...], 0.0)
        p = jnp.exp(s - lse_ref[...])
        delta = jnp.sum(dov * o_ref[...], axis=-1, keepdims=True)
        dob = dov.astype(BF16)
        dp = lax.dot_general(dob, v, (((1,), (1,)), ((), ())), preferred_element_type=F32)
        ds = (p * (dp - delta) * scale).astype(BF16)
        dq_ref[:, :QK_NOPE] = jnp.dot(ds, kn, preferred_element_type=F32)
        dq_ref[:, QK_NOPE:] = jnp.dot(ds, krv, preferred_element_type=F32)
        dkn = lax.dot_general(ds, q[:, :QK_NOPE], tn_dims, preferred_element_type=F32)
        dkr = lax.dot_general(ds, q[:, QK_NOPE:], tn_dims, preferred_element_type=F32)
        dv = lax.dot_general(p.astype(BF16), dob, tn_dims, preferred_element_type=F32)

        @pl.when(i == 0)
        def _():
            dkn_ref[...] = dkn
            dv_ref[...] = dv

        @pl.when(i > 0)
        def _():
            dkn_ref[...] += dkn
            dv_ref[...] += dv

        @pl.when((i == 0) & (hh == 0))
        def _():
            dkr_ref[...] = dkr

        @pl.when((i > 0) | (hh > 0))
        def _():
            dkr_ref[...] += dkr

    return _pc(body, name=name, grid=(h, lp // tq),
               in_specs=[pl.BlockSpec((tq, HEAD_SLOT), lambda hh, i: (i, hh)),
                         pl.BlockSpec((lp, QK_NOPE), lambda hh, i: (0, 2 * hh)),
                         pl.BlockSpec((lp, V_HEAD), lambda hh, i: (0, 2 * hh + 1)),
                         pl.BlockSpec((lp, LANE), lambda hh, i: (0, 0)),
                         pl.BlockSpec((tq, V_HEAD), lambda hh, i: (i, hh)),
                         pl.BlockSpec((None, tq, 1), lambda hh, i: (hh, i, 0)),
                         pl.BlockSpec((tq, V_HEAD), lambda hh, i: (i, hh))],
               out_specs=[pl.BlockSpec((tq, HEAD_SLOT), lambda hh, i: (i, hh)),
                          pl.BlockSpec((lp, QK_NOPE), lambda hh, i: (0, hh)),
                          pl.BlockSpec((lp, V_HEAD), lambda hh, i: (0, hh)),
                          pl.BlockSpec((lp, LANE), lambda hh, i: (0, 0))],
               out_shape=[jax.ShapeDtypeStruct((lp, h * HEAD_SLOT), F32), jax.ShapeDtypeStruct((lp, h * QK_NOPE), F32),
                          jax.ShapeDtypeStruct((lp, h * V_HEAD), F32), jax.ShapeDtypeStruct((lp, LANE), F32)],
               compiler_params=pltpu.CompilerParams(dimension_semantics=("arbitrary", "arbitrary")))(qx, kv, kv, kr, o, lse, do)


def _rot_half(x):
    lane = lax.broadcasted_iota(jnp.int32, x.shape, 1)
    half = QK_ROPE // 2
    return jnp.where(lane < half, -pltpu.roll(x, LANE - half, 1), pltpu.roll(x, half, 1))


def _rope(x, cos, sin):
    return x * cos + _rot_half(x) * sin


def _unrope(dy, cos, sin):
    return dy * cos - _rot_half(dy * sin)


def _rope_heads(fn, h):
    def apply(rid, q, cos, sin):
        parts = []
        for hh in range(h):
            parts.append(q[:, hh * HEAD_SLOT: hh * HEAD_SLOT + QK_NOPE])
            parts.append(fn(q[:, hh * HEAD_SLOT + QK_NOPE: (hh + 1) * HEAD_SLOT], cos, sin))
        return jnp.concatenate(parts, axis=1)
    return apply


ANY = pl.BlockSpec(memory_space=pl.ANY)


def _place():
    x, y, c = lax.axis_index("x"), lax.axis_index("y"), lax.axis_index("c")
    chips = [(1 - x, y), (x, 1 - y), (1 - x, 1 - y)]
    return x, y, c, chips


def _rcopy(src, dst, send_sem, recv_sem, dev):
    return pltpu.make_async_remote_copy(src_ref=src, dst_ref=dst, send_sem=send_sem, recv_sem=recv_sem,
                                        device_id=dev, device_id_type=MESH)


def _allgather(shards, *, name):
    n = len(shards)

    def body(*refs):
        ins, outs = refs[:n], refs[n:2 * n]
        send_sems, recv_sems, loc_sems = refs[2 * n:]
        x, y, c, chips = _place()
        sib = (x, y, 1 - c)
        me = 2 * x + y

        def rows(t, s, half):
            hrows = ins[t].shape[0] // 2
            return outs[t].at[pl.ds(s * 2 * hrows + half * hrows, hrows)]

        mine, sent = [], []
        for t in range(n):
            r = ins[t].shape[0]
            cp = pltpu.make_async_copy(ins[t], outs[t].at[pl.ds(me * r, r)], loc_sems.at[t])
            cp.start()
            mine.append(cp)
            for j, (cx, cy) in enumerate(chips):
                cp = _rcopy(ins[t].at[pl.ds(c * (r // 2), r // 2)], rows(t, me, c), send_sems.at[6 * t + j],
                            recv_sems.at[6 * t + j], (cx, cy, c))
                cp.start()
                sent.append(cp)
        for t in range(n):
            for j, (cx, cy) in enumerate(chips):
                landed = rows(t, 2 * cx + cy, c)
                _rcopy(landed, landed, send_sems.at[6 * t + j], recv_sems.at[6 * t + j], (cx, cy, c)).wait_recv()
                cp = _rcopy(landed, landed, send_sems.at[6 * t + 3 + j], recv_sems.at[6 * t + 3 + j], sib)
                cp.start()
                sent.append(cp)
        for t in range(n):
            for j, (cx, cy) in enumerate(chips):
                other = rows(t, 2 * cx + cy, 1 - c)
                _rcopy(other, other, send_sems.at[6 * t + 3 + j], recv_sems.at[6 * t + 3 + j], sib).wait_recv()
        for cp in sent:
            cp.wait_send()
        for cp in mine:
            cp.wait()

    return _pc(body, name=name, in_specs=[ANY] * n, out_specs=[ANY] * n,
               out_shape=[jax.ShapeDtypeStruct((4 * s.shape[0], s.shape[1]), s.dtype) for s in shards],
               scratch_shapes=[pltpu.SemaphoreType.DMA((6 * n,)), pltpu.SemaphoreType.DMA((6 * n,)),
                               pltpu.SemaphoreType.DMA((n,))])(*shards)


def _rs_sibling(grads, *, name):
    n = len(grads)

    def body(*refs):
        ins, outs = refs[:n], refs[n:2 * n]
        send_sems, recv_sems = refs[2 * n:]
        x, y, c, _ = _place()
        cps = []
        for t in range(n):
            h = ins[t].shape[0] // 8
            for s in range(4):
                cp = _rcopy(ins[t].at[pl.ds((2 * s + 1 - c) * h, h)], outs[t].at[s], send_sems.at[4 * t + s],
                            recv_sems.at[4 * t + s], (x, y, 1 - c))
                cp.start()
                cps.append(cp)
        for cp in cps:
            cp.wait()

    return _pc(body, name=name, in_specs=[ANY] * n, out_specs=[ANY] * n,
               out_shape=[jax.ShapeDtypeStruct((4, g.shape[0] // 8, g.shape[1]), g.dtype) for g in grads],
               scratch_shapes=[pltpu.SemaphoreType.DMA((4 * n,)), pltpu.SemaphoreType.DMA((4 * n,))])(*grads)


def _rs_chips(parts, sends, *, name):
    n = len(parts)

    def body(*refs):
        p_refs, s_refs, b_refs, own_refs = refs[:n], refs[n:2 * n], refs[2 * n:3 * n], refs[3 * n:4 * n]
        send_sems, recv_sems, loc_sems = refs[4 * n:]
        x, y, c, chips = _place()
        me = 2 * x + y
        cps = []
        for t in range(n):
            cp = pltpu.make_async_copy(p_refs[t].at[me], own_refs[t], loc_sems.at[t])
            cp.start()
            cps.append(cp)
            for j, (cx, cy) in enumerate(chips):
                cp = _rcopy(s_refs[t].at[2 * cx + cy], b_refs[t].at[j], send_sems.at[3 * t + j], recv_sems.at[3 * t + j],
                            (cx, cy, c))
                cp.start()
                cps.append(cp)
        for cp in cps:
            cp.wait()

    return _pc(body, name=name, in_specs=[ANY] * (2 * n), out_specs=[ANY] * (2 * n),
               out_shape=[jax.ShapeDtypeStruct((3,) + s.shape[1:], s.dtype) for s in sends]
               + [jax.ShapeDtypeStruct(p.shape[1:], F32) for p in parts],
               scratch_shapes=[pltpu.SemaphoreType.DMA((3 * n,)), pltpu.SemaphoreType.DMA((3 * n,)),
                               pltpu.SemaphoreType.DMA((n,))])(*parts, *sends)


def _rs_final(halves, *, name):
    n = len(halves)

    def body(*refs):
        ins, outs = refs[:n], refs[n:2 * n]
        send_sems, recv_sems, loc_sems = refs[2 * n:]
        x, y, c, _ = _place()
        cps = []
        for t in range(n):
            cp = pltpu.make_async_copy(ins[t], outs[t].at[c], loc_sems.at[t])
            cp.start()
            cps.append(cp)
            cp = _rcopy(ins[t], outs[t].at[c], send_sems.at[t], recv_sems.at[t], (x, y, 1 - c))
            cp.start()
            cps.append(cp)
        for cp in cps:
            cp.wait()

    return _pc(body, name=name, in_specs=[ANY] * n, out_specs=[ANY] * n,
               out_shape=[jax.ShapeDtypeStruct((2,) + hv.shape, hv.dtype) for hv in halves],
               scratch_shapes=[pltpu.SemaphoreType.DMA((n,)), pltpu.SemaphoreType.DMA((n,)),
                               pltpu.SemaphoreType.DMA((n,))])(*halves)


def _add_halves(g, a, send_dtype, *, name):
    _, h, cols = a.shape
    th = _row_tile(h, cols)
    g4 = g.reshape(4, 2, h, cols)
    c = lax.axis_index("c").astype(jnp.int32).reshape(1)

    def body(c_ref, g_ref, a_ref, p_ref, s_ref):
        v = g_ref[...] + a_ref[...]
        p_ref[...] = v
        s_ref[...] = v.astype(send_dtype)

    return _pc(body, name=name,
               grid_spec=pltpu.PrefetchScalarGridSpec(
                   num_scalar_prefetch=1, grid=(4, h // th),
                   in_specs=[pl.BlockSpec((None, None, th, cols), lambda s, i, cr: (s, cr[0], i, 0)),
                             pl.BlockSpec((None, th, cols), lambda s, i, cr: (s, i, 0))],
                   out_specs=[pl.BlockSpec((None, th, cols), lambda s, i, cr: (s, i, 0))] * 2),
               out_shape=[jax.ShapeDtypeStruct(a.shape, F32), jax.ShapeDtypeStruct(a.shape, send_dtype)],
               compiler_params=pltpu.CompilerParams(dimension_semantics=("arbitrary", "arbitrary")))(c, g4, a)


def _add_chips(own, b, *, name):
    h, cols = own.shape
    th = _row_tile(h, cols)

    def body(o_ref, b_ref, r_ref):
        r_ref[...] = ((o_ref[...] + b_ref[0].astype(F32)) + b_ref[1].astype(F32)) + b_ref[2].astype(F32)

    return _pc(body, name=name, grid=(h // th,),
               in_specs=[pl.BlockSpec((th, cols), lambda i: (i, 0)), pl.BlockSpec((3, th, cols), lambda i: (0, i, 0))],
               out_specs=pl.BlockSpec((th, cols), lambda i: (i, 0)),
               out_shape=jax.ShapeDtypeStruct((h, cols), F32),
               compiler_params=pltpu.CompilerParams(dimension_semantics=("arbitrary",)))(own, b)


def _reduce_scatter(grads, send_dtypes, tag):
    recv = _rs_sibling(grads, name=f"rs_sibling_{tag}")
    parts, sends = [], []
    for t, (g, a) in enumerate(zip(grads, recv)):
        p, s = _add_halves(g, a, send_dtypes[t], name=f"rs_add_halves_{tag}{t}")
        parts.append(p)
        sends.append(s)
    res = _rs_chips(parts, sends, name=f"rs_chips_{tag}")
    n = len(grads)
    halves = [_add_chips(res[n + t], res[t], name=f"rs_add_chips_{tag}{t}") for t in range(n)]
    full = _rs_final(halves, name=f"rs_final_{tag}")
    return [f.reshape(-1, f.shape[-1]) for f in full]


def _s5_discretize(lam_re, lam_im, log_dt, b_re, b_im):
    lam = lax.complex(lam_re, lam_im)
    dt = jnp.exp(log_dt)[:, None]
    lam_bar = jnp.exp(lam * dt)
    b_bar = ((lam_bar - 1.0) / lam)[..., None] * lax.complex(b_re, b_im)
    return jnp.real(lam_bar), jnp.imag(lam_bar), jnp.real(b_bar), jnp.imag(b_bar)


def _lanes_from_gp(re, im, cfg):
    v = jnp.stack([re, im]).reshape(2, cfg.NB, GROUPS_PER_BLOCK, SSM_STATE)
    return jnp.transpose(v, (1, 0, 2, 3)).reshape(1, cfg.NL)


def _gp_from_lanes(v, cfg):
    v = jnp.transpose(v.reshape(cfg.NB, 2, GROUPS_PER_BLOCK, SSM_STATE), (1, 0, 2, 3)).reshape(2, cfg.G, SSM_STATE)
    return v[0], v[1]


def _bb_band(bb_re, bb_im, cfg):
    eye = jnp.eye(GROUPS_PER_BLOCK, dtype=F32)
    bb = jnp.stack([bb_re, bb_im]).reshape(2, cfg.NB, GROUPS_PER_BLOCK, SSM_STATE, SSM_GROUP)
    return jnp.einsum('rjgpc,gh->jgcrhp', bb, eye).reshape(cfg.DS, 2 * GROUPS_PER_BLOCK * SSM_STATE)


def _bb_from_band(m, cfg):
    eye = jnp.eye(GROUPS_PER_BLOCK, dtype=F32)
    m = m.reshape(cfg.NB, GROUPS_PER_BLOCK, SSM_GROUP, 2, GROUPS_PER_BLOCK, SSM_STATE)
    v = jnp.einsum('jgcrhp,gh->rjgpc', m, eye).reshape(2, cfg.G, SSM_STATE, SSM_GROUP)
    return v[0], v[1]


def _cc_band(c_re, c_im, cfg):
    eye = jnp.eye(GROUPS_PER_BLOCK, dtype=F32)
    cc = jnp.stack([c_re, -c_im]).reshape(2, cfg.NB, GROUPS_PER_BLOCK, SSM_GROUP, SSM_STATE)
    return jnp.einsum('rjgcp,gh->jrhpgc', cc, eye).reshape(cfg.NL, GROUPS_PER_BLOCK * SSM_GROUP)


def _cc_from_band(m, cfg):
    eye = jnp.eye(GROUPS_PER_BLOCK, dtype=F32)
    m = m.reshape(cfg.NB, 2, GROUPS_PER_BLOCK, SSM_STATE, GROUPS_PER_BLOCK, SSM_GROUP)
    v = jnp.einsum('jrhpgc,gh->rjgcp', m, eye).reshape(2, cfg.G, SSM_GROUP, SSM_STATE)
    return v[0], -v[1]


PACK_COLS = 512
PACK_ROW_ALIGN = 64


def _pack(arrs):
    flat = jnp.concatenate([a.reshape(-1).astype(F32) for a in arrs])
    unit = PACK_COLS * PACK_ROW_ALIGN
    total = -(-flat.shape[0] // unit) * unit
    return jnp.pad(flat, (0, total - flat.shape[0])).reshape(-1, PACK_COLS)


def _unpack(p, shapes):
    flat = p.reshape(-1)
    out, off = [], 0
    for shp in shapes:
        size = math.prod(shp)
        out.append(flat[off:off + size].reshape(shp))
        off += size
    return out


def _adamw(w, g, m, v, *, name):
    c1 = 1.0 / (1.0 - ADAM_B1 ** ADAM_STEP)
    c2 = 1.0 / (1.0 - ADAM_B2 ** ADAM_STEP)

    def fn(rid, wv, gv, mv, vv):
        mn = ADAM_B1 * mv + (1.0 - ADAM_B1) * gv
        vn = ADAM_B2 * vv + (1.0 - ADAM_B2) * (gv * gv)
        delta = -ADAM_LR * ((mn * c1) / (jnp.sqrt(vn * c2) + ADAM_EPS) + ADAM_WD * wv)
        return delta, mn, vn

    cols = w.shape[1]
    return _ew(fn, [w, g, m, v], [], [(cols, F32)] * 3, name=name)


def _to_comm_layout(name, w, cfg):
    w = w[0]
    if name == 'w_in':
        return jnp.pad(w, ((0, 0), (0, cfg.DINP - cfg.DIN)))
    if name == 'w_q_b':
        hs = w.shape[1] // (QK_NOPE + QK_ROPE)
        wt = w.T.reshape(hs, QK_NOPE + QK_ROPE, cfg.QL)
        return jnp.pad(wt, ((0, 0), (0, HEAD_SLOT - QK_NOPE - QK_ROPE), (0, 0))).reshape(hs * HEAD_SLOT, cfg.QL)
    if name == 'w_kv_b':
        return w.T
    if name == 'w_up':
        wt = w.T.reshape(2, cfg.F // 4, cfg.D)
        return jnp.pad(wt, ((0, 0), (0, cfg.FQ - cfg.F // 4), (0, 0))).reshape(2 * cfg.FQ, cfg.D)
    if name == 'w_down':
        return jnp.pad(w, ((0, cfg.FQ - cfg.F // 4), (0, 0)))
    return w


def _from_comm_layout(name, g, cfg):
    if name == 'w_in':
        g = g[:, :cfg.DIN]
    elif name == 'w_q_b':
        hs = g.shape[0] // HEAD_SLOT
        g = g.reshape(hs, HEAD_SLOT, cfg.QL)[:, :QK_NOPE + QK_ROPE].reshape(hs * (QK_NOPE + QK_ROPE), cfg.QL).T
    elif name == 'w_kv_b':
        g = g.T
    elif name == 'w_up':
        g = g.reshape(2, cfg.FQ, cfg.D)[:, :cfg.F // 4].reshape(cfg.F // 2, cfg.D).T
    elif name == 'w_down':
        g = g[:cfg.F // 4]
    return g[None]


def _ff_pad(v, cfg):
    k = v.shape[0]
    return jnp.pad(v.reshape(k, 4, cfg.F // 4), ((0, 0), (0, 0), (0, cfg.FQ - cfg.F // 4))).reshape(k, cfg.FP)


def _ff_unpad(v, cfg):
    k = v.shape[0]
    return v.reshape(k, 4, cfg.FQ)[:, :, :cfg.F // 4].reshape(k, cfg.F)


def _step(cfg, w, m, v, x, loss_target):
    lp, d, ds, nl = cfg.LP, cfg.D, cfg.DS, cfg.NL
    blk = 2 * GROUPS_PER_BLOCK * SSM_STATE
    gw = GROUPS_PER_BLOCK * SSM_GROUP
    xi, yi = lax.axis_index("x"), lax.axis_index("y")
    me = 2 * xi + yi

    shards = [_to_comm_layout(n, w[n], cfg).astype(BF16) for n in BIG]
    conv_w_shard = jnp.pad(w['conv_w'][0], ((0, ROW_ALIGN - 3), (0, cfg.FQ - cfg.F // 4)))
    full = _allgather(shards + [w['meta_tokens'], conv_w_shard], name="allgather_weights")
    w_in, w_glu, w_qt, w_kvt, w_out, w_upt, w_down = full[:7]
    meta = jnp.transpose(full[7].reshape(4, N_META, d // 4), (1, 0, 2)).reshape(N_META, d)
    conv_w = jnp.transpose(full[8].reshape(4, ROW_ALIGN, cfg.FQ)[:, :3], (1, 0, 2)).reshape(3, cfg.FP)
    conv_b = _ff_pad(w['conv_b'], cfg)

    pos = (jnp.arange(lp, dtype=jnp.int32) - PAD).astype(F32)
    inv_freq = 1.0 / (ROPE_BASE ** (jnp.arange(0, QK_ROPE, 2, dtype=F32) / QK_ROPE))
    ang = pos[:, None] * inv_freq[None, :]
    zpad = jnp.zeros((lp, LANE - QK_ROPE), F32)
    cos_t = jnp.concatenate([jnp.cos(ang), jnp.cos(ang), zpad], axis=1)
    sin_t = jnp.concatenate([jnp.sin(ang), jnp.sin(ang), zpad], axis=1)

    s5_in = (w['lam_re'][0], w['lam_im'][0], w['log_dt'][0], w['b_re'][0], w['b_im'][0])
    (a_re, a_im, bb_re, bb_im), s5_vjp = jax.vjp(_s5_discretize, *s5_in)
    a_l = _lanes_from_gp(a_re, a_im, cfg)
    bb_band = _bb_band(bb_re, bb_im, cfg).astype(BF16)
    cc_band = _cc_band(w['c_re'][0], w['c_im'][0], cfg).astype(BF16)
    d_skip, b_glu = w['d_skip'], w['b_glu']

    h0 = jnp.concatenate([jnp.zeros((PAD, d), F32), meta, x[0]], axis=0)
    xn = _rms_fwd(h0, w['mix_norm'], name="rms_mix")
    z = _mm(xn, w_in, name="mm_in", tn=_tile(cfg.DINP, 640))
    u = (z, ds, 0)
    q_a = (z, cfg.QL, ds // cfg.QL)
    kv_a = (z, cfg.KVL, (ds + cfg.QL) // cfg.KVL)
    k_pe = (z, LANE, (ds + cfg.QL + cfg.KVL) // LANE)

    bu = _mm(z, bb_band, name="mm_s5_bu", dims=(lp, nl, gw), tn=blk, tk=gw,
             a_idx=lambda i, j, k: (i, j), b_idx=lambda i, j, k: (j, 0))
    hs = _s5_scan(bu, a_l, reverse=False, name="s5_scan_fwd")
    yc = _mm(hs, cc_band, name="mm_s5_y", dims=(lp, ds, blk), tn=gw, tk=blk,
             a_idx=lambda i, j, k: (i, j), b_idx=lambda i, j, k: (j, 0))

    def s5_y(ycv, uv, dk):
        return ycv + dk * uv

    gl = _ew(lambda rid, ycv, uv, dk: jax.nn.gelu(s5_y(ycv, uv, dk)), [yc, u], [d_skip], [(ds, BF16)], name="s5_gelu")[0]
    tg = _mm(gl, w_glu, name="mm_glu")
    ya = _ew(lambda rid, ycv, uv, tv, dk, bg: jax.nn.gelu(s5_y(ycv, uv, dk)) * jax.nn.sigmoid(tv + bg),
             [yc, u, tg], [d_skip, b_glu], [(ds, F32)], name="s5_glu")[0]

    qn = _rms_fwd(q_a, w['q_a_norm'], name="rms_q")
    kvn = _rms_fwd(kv_a, w['kv_a_norm'], name="rms_kv")
    q_raw = _mm(qn, w_qt, tb=True, name="mm_q")
    qx = _ew(_rope_heads(_rope, cfg.H), [q_raw, cos_t, sin_t], [], [(cfg.H * HEAD_SLOT, BF16)], name="rope_q")[0]
    kv = _mm(kvn, w_kvt, tb=True, out_dtype=BF16, name="mm_kv")
    kr = _ew(lambda rid, kp, cs, sn: _rope(kp, cs, sn), [k_pe, cos_t, sin_t], [], [(LANE, BF16)], name="rope_k")[0]
    o, lse = _attn_fwd(qx, kv, kr, cfg, name="attn_fwd")

    def norm2(rid, yav, ov, gs, ga):
        return jnp.concatenate([_rms_parts(yav, gs)[0] * gs, _rms_parts(ov, ga)[0] * ga], axis=1)

    yn = _ew(norm2, [ya, o], [w['out_norm_ssm'], w['out_norm_attn']], [(cfg.DMIX, BF16)], name="rms_out")[0]
    h1 = _mm(yn, w_out, res=h0, name="mm_out")
    xn2 = _rms_fwd(h1, w['ffn_norm'], name="rms_ffn")
    up = _mm(xn2, w_upt, tb=True, name="mm_up")
    act = _conv_fwd(up, conv_w, conv_b, name="conv_fwd")
    h2 = _mm(act, w_down, res=h1, tm=_tile(lp, 544, ROW_ALIGN), name="mm_down")

    tgt = jnp.concatenate([jnp.zeros((PAD + N_META, d), F32), loss_target[0]], axis=0)
    g_final = w['final_norm'].reshape(1, d)

    def head(rid, hv, tv, gv):
        xhat, r = _rms_parts(hv, gv)
        valid = rid >= PAD + N_META
        diff = jnp.where(valid, xhat * gv - tv, 0.0)
        dout = diff * (1.0 / d)
        dxhat = dout * gv
        dx = r * (dxhat - xhat * jnp.mean(dxhat * xhat, axis=-1, keepdims=True))
        return dx, dx, dout * xhat, 0.5 * diff * dout

    dh2, dh2_b, dg_final, loss_cols = _ew(head, [h2, tgt], [g_final], [(d, F32), (d, BF16)], [d, d], name="loss_head")
    loss = lax.psum(jnp.sum(loss_cols), ("x", "y", "c"))

    dact = _mm(dh2_b, w_down, tb=True, out_dtype=BF16, name="mm_dact")
    dw_down = _mm(act, dh2_b, ta=True, tn=d, tm=512, name="mm_dw_down")
    dup, dconv_w, dconv_b = _conv_bwd(up, dact, conv_w, conv_b, name="conv_bwd")
    dxn2 = _mm(dup, w_upt, tk=512, tn=1024, name="mm_dxn2")
    dw_upt = _mm(dup, xn2, ta=True, tn=d, tm=512, name="mm_dw_up")
    dh1, dh1_b, dg_ffn = _rms_bwd(h1, w['ffn_norm'], dxn2, res=dh2, mask=True, with_bf16=True, name="rms_ffn_bwd")

    dyn = _mm(dh1_b, w_out, tb=True, name="mm_dyn")
    dw_out = _mm(yn, dh1_b, ta=True, tn=d, tm=512, name="mm_dw_out")
    dya, dg_ssm = _rms_bwd(ya, w['out_norm_ssm'], (dyn, ds, 0), name="rms_ssm_bwd")
    do, dg_attn = _rms_bwd(o, w['out_norm_attn'], (dyn, cfg.DATTN, ds // cfg.DATTN), name="rms_attn_bwd")

    dqx, dkn, dv, dkr = _attn_bwd(qx, kv, kr, o, lse, do, cfg, name="attn_bwd")
    dq_raw = _ew(_rope_heads(_unrope, cfg.H), [dqx, cos_t, sin_t], [], [(cfg.H * HEAD_SLOT, BF16)], name="unrope_q")[0]
    dk_pe = _ew(lambda rid, dk, cs, sn: _unrope(dk, cs, sn), [dkr, cos_t, sin_t], [], [(LANE, F32)], name="unrope_k")[0]
    dqn = _mm(dq_raw, w_qt, name="mm_dqn")
    dw_qt = _mm(dq_raw, qn, ta=True, tm=512, name="mm_dw_q")
    dkv = jnp.stack([dkn.reshape(lp, cfg.H, QK_NOPE), dv.reshape(lp, cfg.H, V_HEAD)], axis=2).reshape(lp, -1).astype(BF16)
    dkvn = _mm(dkv, w_kvt, name="mm_dkvn")
    dw_kvt = _mm(dkv, kvn, ta=True, tm=512, name="mm_dw_kv")
    dq_a, dg_q = _rms_bwd(q_a, w['q_a_norm'], dqn, name="rms_q_bwd")
    dkv_a, dg_kv = _rms_bwd(kv_a, w['kv_a_norm'], dkvn, name="rms_kv_bwd")

    def glu_bwd(rid, ycv, uv, tv, dyav, dk, bg):
        gelu = jax.nn.gelu(s5_y(ycv, uv, dk))
        sg = jax.nn.sigmoid(tv + bg)
        dt = dyav * gelu * sg * (1.0 - sg)
        return dt, dyav * sg, dt

    dt_b, dgl1, db_glu = _ew(glu_bwd, [yc, u, tg, dya], [d_skip, b_glu], [(ds, BF16), (ds, F32)], [ds], name="s5_glu_bwd")
    dgl = _mm(dt_b, w_glu, tb=True, res=dgl1, name="mm_dgl")
    dw_glu = _mm(gl, dt_b, ta=True, tm=512, name="mm_dw_glu")

    def gelu_bwd(rid, ycv, uv, dglv, dk):
        _, vjp = jax.vjp(jax.nn.gelu, s5_y(ycv, uv, dk))
        dy = vjp(dglv)[0]
        return dy, dy * dk, dy * uv

    dy_b, du_skip, dd_skip = _ew(gelu_bwd, [yc, u, dgl], [d_skip], [(ds, BF16), (ds, F32)], [ds], name="s5_gelu_bwd")
    dhs = _mm(dy_b, cc_band, tb=True, name="mm_s5_dhs", dims=(lp, nl, gw), tn=blk, tk=gw,
              a_idx=lambda i, j, k: (i, j), b_idx=lambda i, j, k: (j, 0))
    dcc_band = _mm(hs, dy_b, ta=True, name="mm_s5_dcc", dims=(nl, gw, lp), tm=blk, tn=gw,
                   a_idx=lambda i, j, k: (0, i), b_idx=lambda i, j, k: (0, i))
    gs = _s5_scan(dhs, a_l, reverse=True, name="s5_scan_bwd")
    da_l = _s5_da(gs, hs, cfg, name="s5_da")
    dbb_band = _mm(z, gs, ta=True, name="mm_s5_dbb", dims=(ds, blk, lp), tm=gw, tn=blk,
                   a_idx=lambda i, j, k: (0, i), b_idx=lambda i, j, k: (0, i))
    du = _mm(gs, bb_band, tb=True, res=du_skip, name="mm_s5_du", dims=(lp, ds, blk), tn=gw, tk=blk,
             a_idx=lambda i, j, k: (i, j), b_idx=lambda i, j, k: (j, 0))

    dz = jnp.concatenate([du, dq_a, dkv_a, dk_pe], axis=1).astype(BF16)
    dxn = _mm(dz, w_in, tb=True, name="mm_dxn")
    dw_in = _mm(xn, dz, ta=True, tm=512, tn=_tile(cfg.DINP, 1024), name="mm_dw_in")
    dh0, dg_mix = _rms_bwd(h0, w['mix_norm'], dxn, res=dh1, name="rms_mix_bwd")
    grad_x = dh0[PAD + N_META:][None]

    da_re, da_im = _gp_from_lanes(da_l, cfg)
    dbb_re, dbb_im = _bb_from_band(dbb_band, cfg)
    dlam_re, dlam_im, dlog_dt, db_re, db_im = s5_vjp((da_re, da_im, dbb_re, dbb_im))
    dc_re, dc_im = _cc_from_band(dcc_band, cfg)
    local_small = {
        'meta_tokens': dh0[PAD:PAD + N_META], 'mix_norm': dg_mix, 'lam_re': dlam_re, 'lam_im': dlam_im, 'log_dt': dlog_dt,
        'b_re': db_re, 'b_im': db_im, 'c_re': dc_re, 'c_im': dc_im, 'd_skip': dd_skip, 'b_glu': db_glu, 'q_a_norm': dg_q,
        'kv_a_norm': dg_kv, 'out_norm_ssm': dg_ssm, 'out_norm_attn': dg_attn, 'ffn_norm': dg_ffn,
        'conv_w': _ff_unpad(dconv_w, cfg), 'conv_b': _ff_unpad(dconv_b, cfg), 'final_norm': dg_final,
    }
    small_shapes = [local_small[n].shape for n in SMALL]

    big_local = [dw_in, dw_glu, dw_qt, dw_kvt, dw_out, dw_upt, dw_down]
    reduced = _reduce_scatter(big_local + [_pack([local_small[n] for n in SMALL])], [BF16] * 7 + [F32], "grads")
    small_full = _allgather([reduced[7]], name="allgather_small")[0]
    small_sum = dict(zip(SMALL, _unpack(small_full, small_shapes)))

    grads = {n: _from_comm_layout(n, g, cfg) for n, g in zip(BIG, reduced[:7])}
    for n in SMALL:
        g = small_sum[n]
        if n == 'meta_tokens':
            g = lax.dynamic_slice_in_dim(g, me * (d // 4), d // 4, axis=1)
        elif n == 'conv_w':
            g = lax.dynamic_slice_in_dim(g, me * (cfg.F // 4), cfg.F // 4, axis=1)[None]
        else:
            g = g.reshape(w[n].shape)
        grads[n] = g

    delta, new_m, new_v = {}, {}, {}
    for n in BIG:
        shp = w[n].shape
        r = [t.reshape(shp[-2], shp[-1]) for t in (w[n], grads[n], m[n], v[n])]
        dl, mn, vn = _adamw(*r, name=f"adamw_{n}")
        delta[n], new_m[n], new_v[n] = dl.reshape(shp), mn.reshape(shp), vn.reshape(shp)
    shapes = [w[n].shape for n in SMALL]
    packs = [_pack([src[n] for n in SMALL]) for src in (w, grads, m, v)]
    for dst, p in zip((delta, new_m, new_v), _adamw(*packs, name="adamw_small")):
        dst.update(zip(SMALL, _unpack(p, shapes)))

    return (loss, grad_x, *[grads[n] for n in WEIGHTS], *[delta[n] for n in WEIGHTS],
            *[new_m[n] for n in WEIGHTS], *[new_v[n] for n in WEIGHTS])


def kernel(x, meta_tokens, mix_norm, w_in, lam_re, lam_im, log_dt, b_re, b_im, c_re, c_im, d_skip, w_glu, b_glu, q_a_norm, w_q_b, kv_a_norm, w_kv_b, out_norm_ssm, out_norm_attn, w_out, ffn_norm, w_up, conv_w, conv_b, w_down, final_norm, loss_target, m_meta_tokens, m_mix_norm, m_w_in, m_lam_re, m_lam_im, m_log_dt, m_b_re, m_b_im, m_c_re, m_c_im, m_d_skip, m_w_glu, m_b_glu, m_q_a_norm, m_w_q_b, m_kv_a_norm, m_w_kv_b, m_out_norm_ssm, m_out_norm_attn, m_w_out, m_ffn_norm, m_w_up, m_conv_w, m_conv_b, m_w_down, m_final_norm, v_meta_tokens, v_mix_norm, v_w_in, v_lam_re, v_lam_im, v_log_dt, v_b_re, v_b_im, v_c_re, v_c_im, v_d_skip, v_w_glu, v_b_glu, v_q_a_norm, v_w_q_b, v_kv_a_norm, v_w_kv_b, v_out_norm_ssm, v_out_norm_attn, v_w_out, v_ffn_norm, v_w_up, v_conv_w, v_conv_b, v_w_down, v_final_norm):
    args = dict(locals())
    w = {n: args[n] for n in WEIGHTS}
    m = {n: args["m_" + n] for n in WEIGHTS}
    v = {n: args["v_" + n] for n in WEIGHTS}
    return _step(PROD, w, m, v, x, loss_target)
```

```python
import functools
import math
from typing import NamedTuple

import jax
import jax.numpy as jnp
from jax import lax
from jax.experimental import pallas as pl
from jax.experimental.pallas import tpu as pltpu

F32, BF16 = jnp.float32, jnp.bfloat16
MESH = pl.DeviceIdType.MESH
LANE = 128
ROW_ALIGN = 16
N_META = 16
PAD = 112
CHUNK = 64
SSM_GROUP = 16
SSM_STATE = 64
GROUPS_PER_BLOCK = 8
QK_NOPE, QK_ROPE, V_HEAD = 128, 64, 128
HEAD_SLOT = 256
ROPE_BASE = 10000.0
EPS = 1e-6
ADAM_LR, ADAM_B1, ADAM_B2, ADAM_EPS, ADAM_WD, ADAM_STEP = 0.001, 0.9, 0.999, 1e-08, 0.01, 10
DT_F32_BLOCK_BYTES = 1 << 20


class Cfg(NamedTuple):
    D: int
    S: int
    DS: int
    H: int
    QL: int
    KVL: int
    F: int

    @property
    def LP(self):
        return PAD + N_META + self.S

    @property
    def G(self):
        return self.DS // SSM_GROUP

    @property
    def NB(self):
        return self.G // GROUPS_PER_BLOCK

    @property
    def NL(self):
        return 2 * self.G * SSM_STATE

    @property
    def DATTN(self):
        return self.H * V_HEAD

    @property
    def DMIX(self):
        return self.DS + self.DATTN

    @property
    def DIN(self):
        return self.DS + self.QL + self.KVL + QK_ROPE

    @property
    def DINP(self):
        return self.DS + self.QL + self.KVL + LANE

    @property
    def FQ(self):
        return -(-(self.F // 4) // LANE) * LANE

    @property
    def FP(self):
        return 4 * self.FQ


PROD = Cfg(D=2048, S=2048, DS=1024, H=8, QL=512, KVL=256, F=5504)

WEIGHTS = ['meta_tokens', 'mix_norm', 'w_in', 'lam_re', 'lam_im', 'log_dt', 'b_re', 'b_im', 'c_re', 'c_im', 'd_skip',
           'w_glu', 'b_glu', 'q_a_norm', 'w_q_b', 'kv_a_norm', 'w_kv_b', 'out_norm_ssm', 'out_norm_attn', 'w_out',
           'ffn_norm', 'w_up', 'conv_w', 'conv_b', 'w_down', 'final_norm']
BIG = ['w_in', 'w_glu', 'w_q_b', 'w_kv_b', 'w_out', 'w_up', 'w_down']
SMALL = [n for n in WEIGHTS if n not in BIG]


def _pc(body, **kw):
    return pl.pallas_call(body, **kw)


def _tile(n, target, align=LANE):
    best = None
    d = align
    while d <= min(n, target):
        if n % d == 0:
            best = d
        d += align
    return best if best is not None else n


def _row_tile(rows, cols):
    return _tile(rows, max(ROW_ALIGN, DT_F32_BLOCK_BYTES // (4 * cols)), ROW_ALIGN)


def _mm(a, b, *, name, ta=False, tb=False, tm=None, tn=512, tk=None, out_dtype=F32, res=None,
        a_idx=None, b_idx=None, dims=None):
    if dims is None:
        m, k = (a.shape[1], a.shape[0]) if ta else a.shape
        n = b.shape[0] if tb else b.shape[1]
    else:
        m, n, k = dims
    tm = _tile(m, tm or m, LANE if ta else ROW_ALIGN)
    tn = _tile(n, tn)
    tk = _tile(k, tk or k, ROW_ALIGN if (ta and not tb) else LANE)
    nm, nn, nk = m // tm, n // tn, k // tk
    a_idx = a_idx or ((lambda i, j, kk: (kk, i)) if ta else (lambda i, j, kk: (i, kk)))
    b_idx = b_idx or ((lambda i, j, kk: (j, kk)) if tb else (lambda i, j, kk: (kk, j)))
    dn = (((0 if ta else 1,), (1 if tb else 0,)), ((), ()))

    def body(*refs):
        a_ref, b_ref = refs[0], refs[1]
        r_ref = refs[2] if res is not None else None
        o_ref = refs[3] if res is not None else refs[2]
        d = lax.dot_general(a_ref[...].astype(BF16), b_ref[...].astype(BF16), dn, preferred_element_type=F32)

        def finish(r):
            if r_ref is not None:
                r = r + r_ref[...].astype(F32)
            o_ref[...] = r.astype(out_dtype)

        if nk == 1:
            finish(d)
        else:
            acc = refs[-1]
            kk = pl.program_id(2)

            @pl.when(kk == 0)
            def _():
                acc[...] = d

            @pl.when(kk > 0)
            def _():
                acc[...] += d

            @pl.when(kk == nk - 1)
            def _():
                finish(acc[...])

    in_specs = [pl.BlockSpec((tk, tm) if ta else (tm, tk), a_idx), pl.BlockSpec((tn, tk) if tb else (tk, tn), b_idx)]
    args = [a, b]
    if res is not None:
        in_specs.append(pl.BlockSpec((tm, tn), lambda i, j, kk: (i, j)))
        args.append(res)
    return _pc(body, name=name, grid=(nm, nn, nk), in_specs=in_specs,
               out_specs=pl.BlockSpec((tm, tn), lambda i, j, kk: (i, j)),
               out_shape=jax.ShapeDtypeStruct((m, n), out_dtype),
               scratch_shapes=[pltpu.VMEM((tm, tn), F32)] if nk > 1 else [],
               compiler_params=pltpu.CompilerParams(dimension_semantics=("parallel", "parallel", "arbitrary")))(*args)


def _ew(fn, ins, vecs, outs, sums=(), *, name, tm=None):
    ins = [x if isinstance(x, tuple) else (x, x.shape[1], 0) for x in ins]
    rows = ins[0][0].shape[0]
    cmax = max([c for _, c, _ in ins] + [c for c, _ in outs])
    tm = tm or _row_tile(rows, cmax)
    n_in, n_vec, n_out, n_sum = len(ins), len(vecs), len(outs), len(sums)

    def body(*refs):
        i = pl.program_id(0)
        rid = i * tm + lax.broadcasted_iota(jnp.int32, (tm, 1), 0)
        vals = [r[...] for r in refs[:n_in + n_vec]]
        res = fn(rid, *vals)
        res = res if isinstance(res, (tuple, list)) else (res,)
        o_refs = refs[n_in + n_vec:]
        for o_ref, r in zip(o_refs[:n_out], res[:n_out]):
            o_ref[...] = r.astype(o_ref.dtype)
        for o_ref, r in zip(o_refs[n_out:], res[n_out:]):
            part = jnp.sum(r.astype(F32), axis=0, keepdims=True)

            @pl.when(i == 0)
            def _():
                o_ref[...] = part

            @pl.when(i > 0)
            def _():
                o_ref[...] += part

    in_specs = [pl.BlockSpec((tm, c), functools.partial(lambda i, cb: (i, cb), cb=cb)) for _, c, cb in ins]
    in_specs += [pl.BlockSpec(v.shape, functools.partial(lambda i, nd: (0,) * nd, nd=v.ndim)) for v in vecs]
    out_specs = [pl.BlockSpec((tm, c), lambda i: (i, 0)) for c, _ in outs] + [pl.BlockSpec((1, c), lambda i: (0, 0)) for c in sums]
    out_shape = [jax.ShapeDtypeStruct((rows, c), dt) for c, dt in outs] + [jax.ShapeDtypeStruct((1, c), F32) for c in sums]
    return _pc(body, name=name, grid=(rows // tm,), in_specs=in_specs, out_specs=out_specs, out_shape=out_shape,
               compiler_params=pltpu.CompilerParams(dimension_semantics=("arbitrary",)))(*[x for x, _, _ in ins], *vecs)


def _rms_parts(x, g):
    r = lax.rsqrt(jnp.mean(x * x, axis=-1, keepdims=True) + EPS)
    return x * r, r


def _rms_bwd_block(x, g, dy):
    xhat, r = _rms_parts(x, g)
    dxhat = dy * g
    dx = r * (dxhat - xhat * jnp.mean(dxhat * xhat, axis=-1, keepdims=True))
    return dx, dy * xhat


def _rms_fwd(x, g, *, name):
    c = x[1] if isinstance(x, tuple) else x.shape[1]
    return _ew(lambda rid, xv, gv: _rms_parts(xv.astype(F32), gv)[0] * gv, [x], [g], [(c, BF16)], name=name)[0]


def _rms_bwd(x, g, dy, *, name, res=None, mask=False, with_bf16=False):
    c = x[1] if isinstance(x, tuple) else x.shape[1]

    def fn(rid, xv, dyv, *rest):
        gv = rest[-1]
        dx, dg = _rms_bwd_block(xv.astype(F32), gv, dyv.astype(F32))
        if res is not None:
            dx = dx + rest[0]
        if mask:
            dx = jnp.where(rid >= PAD, dx, 0.0)
        return (dx, dx, dg) if with_bf16 else (dx, dg)

    ins = [x, dy] + ([res] if res is not None else [])
    outs = [(c, F32)] + ([(c, BF16)] if with_bf16 else [])
    return _ew(fn, ins, [g], outs, [c], name=name)


def _s5_scan(bu, a_l, *, reverse, name):
    lp, nl = bu.shape
    w = GROUPS_PER_BLOCK * SSM_STATE
    unroll = 8

    def body(bu_ref, a_ref, hs_ref):
        ar = a_ref[:, :w]
        ai = -a_ref[:, w:] if reverse else a_ref[:, w:]

        def step(n, carry):
            hr, hi = carry
            for q in range(unroll):
                t = n * unroll + q
                t = lp - 1 - t if reverse else t
                nr = ar * hr - ai * hi + bu_ref[pl.ds(t, 1), :w]
                ni = ar * hi + ai * hr + bu_ref[pl.ds(t, 1), w:]
                hs_ref[pl.ds(t, 1), :w] = nr
                hs_ref[pl.ds(t, 1), w:] = ni
                hr, hi = nr, ni
            return hr, hi

        z = jnp.zeros((1, w), F32)
        lax.fori_loop(0, lp // unroll, step, (z, z))

    return _pc(body, name=name, grid=(nl // (2 * w),),
               in_specs=[pl.BlockSpec((lp, 2 * w), lambda j: (0, j)), pl.BlockSpec((1, 2 * w), lambda j: (0, j))],
               out_specs=pl.BlockSpec((lp, 2 * w), lambda j: (0, j)),
               out_shape=jax.ShapeDtypeStruct((lp, nl), F32),
               compiler_params=pltpu.CompilerParams(dimension_semantics=("parallel",)))(bu, a_l)


def _s5_da(gs, hs, cfg, *, name):
    lp, nl = gs.shape
    w = GROUPS_PER_BLOCK * SSM_STATE
    tc = w // 2
    per = 2 * w // tc

    def body(gr_ref, gi_ref, hr_ref, hi_ref, dre_ref, dim_ref):
        keep = lax.broadcasted_iota(jnp.int32, (lp, 1), 0) >= 1
        hr = jnp.where(keep, pltpu.roll(hr_ref[...], 1, 0), 0.0)
        hi = jnp.where(keep, pltpu.roll(hi_ref[...], 1, 0), 0.0)
        gr, gi = gr_ref[...], gi_ref[...]
        dre_ref[...] = jnp.sum(gr * hr + gi * hi, axis=0, keepdims=True)
        dim_ref[...] = jnp.sum(gi * hr - gr * hi, axis=0, keepdims=True)

    re_blk = pl.BlockSpec((lp, tc), lambda j, q: (0, per * j + q))
    im_blk = pl.BlockSpec((lp, tc), lambda j, q: (0, per * j + per // 2 + q))
    out_blk = pl.BlockSpec((1, tc), lambda j, q: (0, (per // 2) * j + q))
    dre, dim = _pc(body, name=name, grid=(cfg.NB, per // 2), in_specs=[re_blk, im_blk, re_blk, im_blk],
                   out_specs=[out_blk, out_blk], out_shape=[jax.ShapeDtypeStruct((1, nl // 2), F32)] * 2,
                   compiler_params=pltpu.CompilerParams(dimension_semantics=("parallel", "parallel")))(gs, gs, hs, hs)
    return jnp.stack([dre.reshape(cfg.NB, w), dim.reshape(cfg.NB, w)], axis=1).reshape(1, nl)


def _conv_gate(pre, cw, cb):
    return cw[0:1] * pltpu.roll(pre, 2, 0) + cw[1:2] * pltpu.roll(pre, 1, 0) + cw[2:3] * pre + cb


def _conv_fwd(up, cw, cb, *, name):
    lp, fp2 = up.shape
    fp = fp2 // 2
    tc = _tile(fp, 256)
    nb = fp // tc

    def body(pre_ref, val_ref, cw_ref, cb_ref, o_ref):
        gate = _conv_gate(pre_ref[...], cw_ref[...], cb_ref[...])
        o_ref[...] = (jax.nn.silu(gate) * val_ref[...]).astype(BF16)

    return _pc(body, name=name, grid=(nb,),
               in_specs=[pl.BlockSpec((lp, tc), lambda j: (0, j)), pl.BlockSpec((lp, tc), lambda j: (0, nb + j)),
                         pl.BlockSpec((3, tc), lambda j: (0, j)), pl.BlockSpec((1, tc), lambda j: (0, j))],
               out_specs=pl.BlockSpec((lp, tc), lambda j: (0, j)),
               out_shape=jax.ShapeDtypeStruct((lp, fp), BF16),
               compiler_params=pltpu.CompilerParams(dimension_semantics=("parallel",)))(up, up, cw, cb)


def _conv_bwd(up, dact, cw, cb, *, name):
    lp, fp2 = up.shape
    fp = fp2 // 2
    tc = _tile(fp, 256)
    nb = fp // tc

    def body(pre_ref, val_ref, da_ref, cw_ref, cb_ref, dup_ref, dcw_ref, dcb_ref):
        which = pl.program_id(0)
        pre, val, da, cwv = pre_ref[...], val_ref[...], da_ref[...].astype(F32), cw_ref[...]
        gate = _conv_gate(pre, cwv, cb_ref[...])
        sg = jax.nn.sigmoid(gate)
        silu = gate * sg

        @pl.when(which == 1)
        def _():
            dup_ref[...] = (da * silu).astype(BF16)

        @pl.when(which == 0)
        def _():
            dgate = da * val * (sg * (1.0 + gate * (1.0 - sg)))
            dpre = cwv[2:3] * dgate + cwv[1:2] * pltpu.roll(dgate, lp - 1, 0) + cwv[0:1] * pltpu.roll(dgate, lp - 2, 0)
            dup_ref[...] = dpre.astype(BF16)
            dcb_ref[...] = jnp.sum(dgate, axis=0, keepdims=True)
            dcw_ref[0:1, :] = jnp.sum(dgate * pltpu.roll(pre, 2, 0), axis=0, keepdims=True)
            dcw_ref[1:2, :] = jnp.sum(dgate * pltpu.roll(pre, 1, 0), axis=0, keepdims=True)
            dcw_ref[2:3, :] = jnp.sum(dgate * pre, axis=0, keepdims=True)

    return _pc(body, name=name, grid=(2, nb),
               in_specs=[pl.BlockSpec((lp, tc), lambda s, j: (0, j)), pl.BlockSpec((lp, tc), lambda s, j: (0, nb + j)),
                         pl.BlockSpec((lp, tc), lambda s, j: (0, j)),
                         pl.BlockSpec((3, tc), lambda s, j: (0, j)), pl.BlockSpec((1, tc), lambda s, j: (0, j))],
               out_specs=[pl.BlockSpec((lp, tc), lambda s, j: (0, s * nb + j)),
                          pl.BlockSpec((3, tc), lambda s, j: (0, j * (1 - s) + (nb - 1) * s)),
                          pl.BlockSpec((1, tc), lambda s, j: (0, j * (1 - s) + (nb - 1) * s))],
               out_shape=[jax.ShapeDtypeStruct((lp, fp2), BF16), jax.ShapeDtypeStruct((3, fp), F32),
                          jax.ShapeDtypeStruct((1, fp), F32)],
               compiler_params=pltpu.CompilerParams(dimension_semantics=("arbitrary", "arbitrary")))(up, up, dact, cw, cb)


def _attn_mask(i, tq, lp):
    qrow = i * tq + lax.broadcasted_iota(jnp.int32, (tq, 1), 0)
    krow = lax.broadcasted_iota(jnp.int32, (1, lp), 1)
    return (krow >= PAD) & ((krow // CHUNK) <= (qrow // CHUNK)), qrow >= PAD


def _attn_scores(q, kn, kr, i, tq, lp, scale):
    nt = (((1,), (1,)), ((), ()))
    s = lax.dot_general(q[:, :QK_NOPE], kn, nt, preferred_element_type=F32)
    s = s + lax.dot_general(q[:, QK_NOPE:], kr, nt, preferred_element_type=F32)
    mask, qvalid = _attn_mask(i, tq, lp)
    return jnp.where(mask, s * scale, jnp.finfo(F32).min), qvalid


def _attn_fwd(qx, kv, kr, cfg, *, name):
    lp, h = cfg.LP, cfg.H
    tq = _tile(lp, 272, ROW_ALIGN)
    scale = 1.0 / math.sqrt(QK_NOPE + QK_ROPE)

    def body(q_ref, kn_ref, v_ref, kr_ref, o_ref, lse_ref):
        i = pl.program_id(1)
        s, qvalid = _attn_scores(q_ref[...], kn_ref[...], kr_ref[...], i, tq, lp, scale)
        m = jnp.max(s, axis=-1, keepdims=True)
        p = jnp.exp(s - m)
        l = jnp.sum(p, axis=-1, keepdims=True)
        o = jnp.dot(p.astype(BF16), v_ref[...], preferred_element_type=F32) / l
        o_ref[...] = jnp.where(qvalid, o, 0.0)
        lse_ref[...] = m + jnp.log(l)

    return _pc(body, name=name, grid=(h, lp // tq),
               in_specs=[pl.BlockSpec((tq, HEAD_SLOT), lambda hh, i: (i, hh)),
                         pl.BlockSpec((lp, QK_NOPE), lambda hh, i: (0, 2 * hh)),
                         pl.BlockSpec((lp, V_HEAD), lambda hh, i: (0, 2 * hh + 1)),
                         pl.BlockSpec((lp, LANE), lambda hh, i: (0, 0))],
               out_specs=[pl.BlockSpec((tq, V_HEAD), lambda hh, i: (i, hh)),
                          pl.BlockSpec((None, tq, 1), lambda hh, i: (hh, i, 0))],
               out_shape=[jax.ShapeDtypeStruct((lp, h * V_HEAD), F32), jax.ShapeDtypeStruct((h, lp, 1), F32)],
               compiler_params=pltpu.CompilerParams(dimension_semantics=("parallel", "parallel")))(qx, kv, kv, kr)


def _attn_bwd(qx, kv, kr, o, lse, do, cfg, *, name):
    lp, h = cfg.LP, cfg.H
    tq = _tile(lp, 272, ROW_ALIGN)
    scale = 1.0 / math.sqrt(QK_NOPE + QK_ROPE)
    tn_dims = (((0,), (0,)), ((), ()))

    def body(q_ref, kn_ref, v_ref, kr_ref, o_ref, lse_ref, do_ref, dq_ref, dkn_ref, dv_ref, dkr_ref):
        hh, i = pl.program_id(0), pl.program_id(1)
        q, kn, v, krv = q_ref[...], kn_ref[...], v_ref[...], kr_ref[...]
        s, qvalid = _attn_scores(q, kn, krv, i, tq, lp, scale)
        dov = jnp.where(qvalid, do_ref[...], 0.0)
        p = jnp.exp(s - lse_ref[...])
        delta = jnp.sum(dov * o_ref[...], axis=-1, keepdims=True)
        dob = dov.astype(BF16)
        dp = lax.dot_general(dob, v, (((1,), (1,)), ((), ())), preferred_element_type=F32)
        ds = (p * (dp - delta) * scale).astype(BF16)
        dq_ref[:, :QK_NOPE] = jnp.dot(ds, kn, preferred_element_type=F32)
        dq_ref[:, QK_NOPE:] = jnp.dot(ds, krv, preferred_element_type=F32)
        dkn = lax.dot_general(ds, q[:, :QK_NOPE], tn_dims, preferred_element_type=F32)
        dkr = lax.dot_general(ds, q[:, QK_NOPE:], tn_dims, preferred_element_type=F32)
        dv = lax.dot_general(p.astype(BF16), dob, tn_dims, preferred_element_type=F32)

        @pl.when(i == 0)
        def _():
            dkn_ref[...] = dkn
            dv_ref[...] = dv

        @pl.when(i > 0)
        def _():
            dkn_ref[...] += dkn
            dv_ref[...] += dv

        @pl.when((i == 0) & (hh == 0))
        def _():
            dkr_ref[...] = dkr

        @pl.when((i > 0) | (hh > 0))
        def _():
            dkr_ref[...] += dkr

    return _pc(body, name=name, grid=(h, lp // tq),
               in_specs=[pl.BlockSpec((tq, HEAD_SLOT), lambda hh, i: (i, hh)),
                         pl.BlockSpec((lp, QK_NOPE), lambda hh, i: (0, 2 * hh)),
                         pl.BlockSpec((lp, V_HEAD), lambda hh, i: (0, 2 * hh + 1)),
                         pl.BlockSpec((lp, LANE), lambda hh, i: (0, 0)),
                         pl.BlockSpec((tq, V_HEAD), lambda hh, i: (i, hh)),
                         pl.BlockSpec((None, tq, 1), lambda hh, i: (hh, i, 0)),
                         pl.BlockSpec((tq, V_HEAD), lambda hh, i: (i, hh))],
               out_specs=[pl.BlockSpec((tq, HEAD_SLOT), lambda hh, i: (i, hh)),
                          pl.BlockSpec((lp, QK_NOPE), lambda hh, i: (0, hh)),
                          pl.BlockSpec((lp, V_HEAD), lambda hh, i: (0, hh)),
                          pl.BlockSpec((lp, LANE), lambda hh, i: (0, 0))],
               out_shape=[jax.ShapeDtypeStruct((lp, h * HEAD_SLOT), F32), jax.ShapeDtypeStruct((lp, h * QK_NOPE), F32),
                          jax.ShapeDtypeStruct((lp, h * V_HEAD), F32), jax.ShapeDtypeStruct((lp, LANE), F32)],
               compiler_params=pltpu.CompilerParams(dimension_semantics=("arbitrary", "arbitrary")))(qx, kv, kv, kr, o, lse, do)


def _rot_half(x):
    lane = lax.broadcasted_iota(jnp.int32, x.shape, 1)
    half = QK_ROPE // 2
    return jnp.where(lane < half, -pltpu.roll(x, LANE - half, 1), pltpu.roll(x, half, 1))


def _rope(x, cos, sin):
    return x * cos + _rot_half(x) * sin


def _unrope(dy, cos, sin):
    return dy * cos - _rot_half(dy * sin)


def _rope_heads(fn, h):
    def apply(rid, q, cos, sin):
        parts = []
        for hh in range(h):
            parts.append(q[:, hh * HEAD_SLOT: hh * HEAD_SLOT + QK_NOPE])
            parts.append(fn(q[:, hh * HEAD_SLOT + QK_NOPE: (hh + 1) * HEAD_SLOT], cos, sin))
        return jnp.concatenate(parts, axis=1)
    return apply


ANY = pl.BlockSpec(memory_space=pl.ANY)


def _place():
    x, y, c = lax.axis_index("x"), lax.axis_index("y"), lax.axis_index("c")
    chips = [(1 - x, y), (x, 1 - y), (1 - x, 1 - y)]
    return x, y, c, chips


def _rcopy(src, dst, send_sem, recv_sem, dev):
    return pltpu.make_async_remote_copy(src_ref=src, dst_ref=dst, send_sem=send_sem, recv_sem=recv_sem,
                                        device_id=dev, device_id_type=MESH)


def _place_shard(shard, dtype, *, name):
    r, cols = shard.shape
    tm = _row_tile(r, cols)
    nblk = r // tm
    me = (2 * lax.axis_index("x") + lax.axis_index("y")).astype(jnp.int32).reshape(1)

    def body(me_ref, s_ref, o_ref):
        o_ref[...] = s_ref[...].astype(dtype)

    return _pc(body, name=name,
               grid_spec=pltpu.PrefetchScalarGridSpec(
                   num_scalar_prefetch=1, grid=(nblk,),
                   in_specs=[pl.BlockSpec((tm, cols), lambda i, mr: (i, 0))],
                   out_specs=pl.BlockSpec((tm, cols), lambda i, mr: (mr[0] * nblk + i, 0))),
               out_shape=jax.ShapeDtypeStruct((4 * r, cols), dtype),
               compiler_params=pltpu.CompilerParams(dimension_semantics=("arbitrary",)))(me, shard)


def _allgather(fulls, *, name):
    n = len(fulls)

    def body(*refs):
        outs = refs[n:2 * n]
        send_sems, recv_sems = refs[2 * n:]
        x, y, c, chips = _place()
        sib = (x, y, 1 - c)
        me = 2 * x + y

        def rows(t, s, half):
            hrows = outs[t].shape[0] // 8
            return outs[t].at[pl.ds((2 * s + half) * hrows, hrows)]

        sent = []
        for t in range(n):
            for j, (cx, cy) in enumerate(chips):
                cp = _rcopy(rows(t, me, c), rows(t, me, c), send_sems.at[6 * t + j], recv_sems.at[6 * t + j], (cx, cy, c))
                cp.start()
                sent.append(cp)
        for t in range(n):
            for j, (cx, cy) in enumerate(chips):
                landed = rows(t, 2 * cx + cy, c)
                _rcopy(landed, landed, send_sems.at[6 * t + j], recv_sems.at[6 * t + j], (cx, cy, c)).wait_recv()
                cp = _rcopy(landed, landed, send_sems.at[6 * t + 3 + j], recv_sems.at[6 * t + 3 + j], sib)
                cp.start()
                sent.append(cp)
        for t in range(n):
            for j, (cx, cy) in enumerate(chips):
                other = rows(t, 2 * cx + cy, 1 - c)
                _rcopy(other, other, send_sems.at[6 * t + 3 + j], recv_sems.at[6 * t + 3 + j], sib).wait_recv()
        for cp in sent:
            cp.wait_send()

    return _pc(body, name=name, in_specs=[ANY] * n, out_specs=[ANY] * n,
               out_shape=[jax.ShapeDtypeStruct(f.shape, f.dtype) for f in fulls],
               input_output_aliases={t: t for t in range(n)},
               scratch_shapes=[pltpu.SemaphoreType.DMA((6 * n,)), pltpu.SemaphoreType.DMA((6 * n,))])(*fulls)


def _rs_sibling(grads, *, name):
    n = len(grads)

    def body(*refs):
        ins, outs = refs[:n], refs[n:2 * n]
        send_sems, recv_sems = refs[2 * n:]
        x, y, c, _ = _place()
        cps = []
        for t in range(n):
            h = ins[t].shape[0] // 8
            for s in range(4):
                cp = _rcopy(ins[t].at[pl.ds((2 * s + 1 - c) * h, h)], outs[t].at[s], send_sems.at[4 * t + s],
                            recv_sems.at[4 * t + s], (x, y, 1 - c))
                cp.start()
                cps.append(cp)
        for cp in cps:
            cp.wait()

    return _pc(body, name=name, in_specs=[ANY] * n, out_specs=[ANY] * n,
               out_shape=[jax.ShapeDtypeStruct((4, g.shape[0] // 8, g.shape[1]), g.dtype) for g in grads],
               scratch_shapes=[pltpu.SemaphoreType.DMA((4 * n,)), pltpu.SemaphoreType.DMA((4 * n,))])(*grads)


def _rs_chips(sends, *, name):
    n = len(sends)

    def body(*refs):
        s_refs, b_refs = refs[:n], refs[n:2 * n]
        send_sems, recv_sems = refs[2 * n:]
        x, y, c, chips = _place()
        cps = []
        for t in range(n):
            for j, (cx, cy) in enumerate(chips):
                cp = _rcopy(s_refs[t].at[2 * cx + cy], b_refs[t].at[j], send_sems.at[3 * t + j], recv_sems.at[3 * t + j],
                            (cx, cy, c))
                cp.start()
                cps.append(cp)
        for cp in cps:
            cp.wait()

    return _pc(body, name=name, in_specs=[ANY] * n, out_specs=[ANY] * n,
               out_shape=[jax.ShapeDtypeStruct((3,) + s.shape[1:], s.dtype) for s in sends],
               scratch_shapes=[pltpu.SemaphoreType.DMA((3 * n,)), pltpu.SemaphoreType.DMA((3 * n,))])(*sends)


def _rs_final(fulls, *, name):
    n = len(fulls)

    def body(*refs):
        outs = refs[n:2 * n]
        send_sems, recv_sems = refs[2 * n:]
        x, y, c, _ = _place()
        cps = []
        for t in range(n):
            cp = _rcopy(outs[t].at[c], outs[t].at[c], send_sems.at[t], recv_sems.at[t], (x, y, 1 - c))
            cp.start()
            cps.append(cp)
        for cp in cps:
            cp.wait()

    return _pc(body, name=name, in_specs=[ANY] * n, out_specs=[ANY] * n,
               out_shape=[jax.ShapeDtypeStruct(f.shape, f.dtype) for f in fulls],
               input_output_aliases={t: t for t in range(n)},
               scratch_shapes=[pltpu.SemaphoreType.DMA((n,)), pltpu.SemaphoreType.DMA((n,))])(*fulls)


def _add_halves(g, a, send_dtype, *, name):
    _, h, cols = a.shape
    th = _row_tile(h, cols)
    g4 = g.reshape(4, 2, h, cols)
    c = lax.axis_index("c").astype(jnp.int32).reshape(1)

    def body(c_ref, g_ref, a_ref, p_ref, s_ref):
        v = g_ref[...] + a_ref[...]
        p_ref[...] = v
        s_ref[...] = v.astype(send_dtype)

    return _pc(body, name=name,
               grid_spec=pltpu.PrefetchScalarGridSpec(
                   num_scalar_prefetch=1, grid=(4, h // th),
                   in_specs=[pl.BlockSpec((None, None, th, cols), lambda s, i, cr: (s, cr[0], i, 0)),
                             pl.BlockSpec((None, th, cols), lambda s, i, cr: (s, i, 0))],
                   out_specs=[pl.BlockSpec((None, th, cols), lambda s, i, cr: (s, i, 0))] * 2),
               out_shape=[jax.ShapeDtypeStruct(a.shape, F32), jax.ShapeDtypeStruct(a.shape, send_dtype)],
               compiler_params=pltpu.CompilerParams(dimension_semantics=("arbitrary", "arbitrary")))(c, g4, a)


def _add_chips(p, b, *, name):
    _, h, cols = p.shape
    th = _row_tile(h, cols)
    idx = jnp.stack([2 * lax.axis_index("x") + lax.axis_index("y"), lax.axis_index("c")]).astype(jnp.int32)

    def body(idx_ref, p_ref, b_ref, r_ref):
        r_ref[...] = ((p_ref[...] + b_ref[0].astype(F32)) + b_ref[1].astype(F32)) + b_ref[2].astype(F32)

    return _pc(body, name=name,
               grid_spec=pltpu.PrefetchScalarGridSpec(
                   num_scalar_prefetch=1, grid=(h // th,),
                   in_specs=[pl.BlockSpec((None, th, cols), lambda i, ir: (ir[0], i, 0)),
                             pl.BlockSpec((3, th, cols), lambda i, ir: (0, i, 0))],
                   out_specs=pl.BlockSpec((None, th, cols), lambda i, ir: (ir[1], i, 0))),
               out_shape=jax.ShapeDtypeStruct((2, h, cols), F32),
               compiler_params=pltpu.CompilerParams(dimension_semantics=("arbitrary",)))(idx, p, b)


def _reduce_scatter(grads, send_dtypes, tag):
    recv = _rs_sibling(grads, name=f"rs_sibling_{tag}")
    parts, sends = [], []
    for t, (g, a) in enumerate(zip(grads, recv)):
        p, s = _add_halves(g, a, send_dtypes[t], name=f"rs_add_halves_{tag}{t}")
        parts.append(p)
        sends.append(s)
    others = _rs_chips(sends, name=f"rs_chips_{tag}")
    halves = [_add_chips(p, b, name=f"rs_add_chips_{tag}{t}") for t, (p, b) in enumerate(zip(parts, others))]
    full = _rs_final(halves, name=f"rs_final_{tag}")
    return [f.reshape(-1, f.shape[-1]) for f in full]


def _s5_discretize(lam_re, lam_im, log_dt, b_re, b_im):
    lam = lax.complex(lam_re, lam_im)
    dt = jnp.exp(log_dt)[:, None]
    lam_bar = jnp.exp(lam * dt)
    b_bar = ((lam_bar - 1.0) / lam)[..., None] * lax.complex(b_re, b_im)
    return jnp.real(lam_bar), jnp.imag(lam_bar), jnp.real(b_bar), jnp.imag(b_bar)


def _lanes_from_gp(re, im, cfg):
    v = jnp.stack([re, im]).reshape(2, cfg.NB, GROUPS_PER_BLOCK, SSM_STATE)
    return jnp.transpose(v, (1, 0, 2, 3)).reshape(1, cfg.NL)


def _gp_from_lanes(v, cfg):
    v = jnp.transpose(v.reshape(cfg.NB, 2, GROUPS_PER_BLOCK, SSM_STATE), (1, 0, 2, 3)).reshape(2, cfg.G, SSM_STATE)
    return v[0], v[1]


def _bb_band(bb_re, bb_im, cfg):
    eye = jnp.eye(GROUPS_PER_BLOCK, dtype=F32)
    bb = jnp.stack([bb_re, bb_im]).reshape(2, cfg.NB, GROUPS_PER_BLOCK, SSM_STATE, SSM_GROUP)
    return jnp.einsum('rjgpc,gh->jgcrhp', bb, eye).reshape(cfg.DS, 2 * GROUPS_PER_BLOCK * SSM_STATE)


def _bb_from_band(m, cfg):
    eye = jnp.eye(GROUPS_PER_BLOCK, dtype=F32)
    m = m.reshape(cfg.NB, GROUPS_PER_BLOCK, SSM_GROUP, 2, GROUPS_PER_BLOCK, SSM_STATE)
    v = jnp.einsum('jgcrhp,gh->rjgpc', m, eye).reshape(2, cfg.G, SSM_STATE, SSM_GROUP)
    return v[0], v[1]


def _cc_band(c_re, c_im, cfg):
    eye = jnp.eye(GROUPS_PER_BLOCK, dtype=F32)
    cc = jnp.stack([c_re, -c_im]).reshape(2, cfg.NB, GROUPS_PER_BLOCK, SSM_GROUP, SSM_STATE)
    return jnp.einsum('rjgcp,gh->jrhpgc', cc, eye).reshape(cfg.NL, GROUPS_PER_BLOCK * SSM_GROUP)


def _cc_from_band(m, cfg):
    eye = jnp.eye(GROUPS_PER_BLOCK, dtype=F32)
    m = m.reshape(cfg.NB, 2, GROUPS_PER_BLOCK, SSM_STATE, GROUPS_PER_BLOCK, SSM_GROUP)
    v = jnp.einsum('jrhpgc,gh->rjgcp', m, eye).reshape(2, cfg.G, SSM_GROUP, SSM_STATE)
    return v[0], -v[1]


PACK_COLS = 512
PACK_ROW_ALIGN = 64


def _pack(arrs):
    flat = jnp.concatenate([a.reshape(-1).astype(F32) for a in arrs])
    unit = PACK_COLS * PACK_ROW_ALIGN
    total = -(-flat.shape[0] // unit) * unit
    return jnp.pad(flat, (0, total - flat.shape[0])).reshape(-1, PACK_COLS)


def _unpack(p, shapes):
    flat = p.reshape(-1)
    out, off = [], 0
    for shp in shapes:
        size = math.prod(shp)
        out.append(flat[off:off + size].reshape(shp))
        off += size
    return out


def _adamw(w, g, m, v, *, name):
    c1 = 1.0 / (1.0 - ADAM_B1 ** ADAM_STEP)
    c2 = 1.0 / (1.0 - ADAM_B2 ** ADAM_STEP)

    def fn(rid, wv, gv, mv, vv):
        mn = ADAM_B1 * mv + (1.0 - ADAM_B1) * gv
        vn = ADAM_B2 * vv + (1.0 - ADAM_B2) * (gv * gv)
        delta = -ADAM_LR * ((mn * c1) / (jnp.sqrt(vn * c2) + ADAM_EPS) + ADAM_WD * wv)
        return delta, mn, vn

    cols = w.shape[1]
    return _ew(fn, [w, g, m, v], [], [(cols, F32)] * 3, name=name)


def _to_comm_layout(name, w, cfg):
    w = w[0]
    if name == 'w_in':
        return jnp.pad(w, ((0, 0), (0, cfg.DINP - cfg.DIN)))
    if name == 'w_q_b':
        hs = w.shape[1] // (QK_NOPE + QK_ROPE)
        wt = w.T.reshape(hs, QK_NOPE + QK_ROPE, cfg.QL)
        return jnp.pad(wt, ((0, 0), (0, HEAD_SLOT - QK_NOPE - QK_ROPE), (0, 0))).reshape(hs * HEAD_SLOT, cfg.QL)
    if name == 'w_kv_b':
        return w.T
    if name == 'w_up':
        wt = w.T.reshape(2, cfg.F // 4, cfg.D)
        return jnp.pad(wt, ((0, 0), (0, cfg.FQ - cfg.F // 4), (0, 0))).reshape(2 * cfg.FQ, cfg.D)
    if name == 'w_down':
        return jnp.pad(w, ((0, cfg.FQ - cfg.F // 4), (0, 0)))
    return w


def _from_comm_layout(name, g, cfg):
    if name == 'w_in':
        g = g[:, :cfg.DIN]
    elif name == 'w_q_b':
        hs = g.shape[0] // HEAD_SLOT
        g = g.reshape(hs, HEAD_SLOT, cfg.QL)[:, :QK_NOPE + QK_ROPE].reshape(hs * (QK_NOPE + QK_ROPE), cfg.QL).T
    elif name == 'w_kv_b':
        g = g.T
    elif name == 'w_up':
        g = g.reshape(2, cfg.FQ, cfg.D)[:, :cfg.F // 4].reshape(cfg.F // 2, cfg.D).T
    elif name == 'w_down':
        g = g[:cfg.F // 4]
    return g[None]


def _ff_pad(v, cfg):
    k = v.shape[0]
    return jnp.pad(v.reshape(k, 4, cfg.F // 4), ((0, 0), (0, 0), (0, cfg.FQ - cfg.F // 4))).reshape(k, cfg.FP)


def _ff_unpad(v, cfg):
    k = v.shape[0]
    return v.reshape(k, 4, cfg.FQ)[:, :, :cfg.F // 4].reshape(k, cfg.F)


def _step(cfg, w, m, v, x, loss_target):
    lp, d, ds, nl = cfg.LP, cfg.D, cfg.DS, cfg.NL
    blk = 2 * GROUPS_PER_BLOCK * SSM_STATE
    gw = GROUPS_PER_BLOCK * SSM_GROUP
    xi, yi = lax.axis_index("x"), lax.axis_index("y")
    me = 2 * xi + yi

    placed = [_place_shard(_to_comm_layout(n, w[n], cfg), BF16, name=f"place_{n}") for n in BIG]
    conv_w_shard = jnp.pad(w['conv_w'][0], ((0, ROW_ALIGN - 3), (0, cfg.FQ - cfg.F // 4)))
    placed += [_place_shard(w['meta_tokens'], F32, name="place_meta"), _place_shard(conv_w_shard, F32, name="place_conv_w")]
    full = _allgather(placed, name="allgather_weights")
    w_in, w_glu, w_qt, w_kvt, w_out, w_upt, w_down = full[:7]
    meta = jnp.transpose(full[7].reshape(4, N_META, d // 4), (1, 0, 2)).reshape(N_META, d)
    conv_w = jnp.transpose(full[8].reshape(4, ROW_ALIGN, cfg.FQ)[:, :3], (1, 0, 2)).reshape(3, cfg.FP)
    conv_b = _ff_pad(w['conv_b'], cfg)

    pos = (jnp.arange(lp, dtype=jnp.int32) - PAD).astype(F32)
    inv_freq = 1.0 / (ROPE_BASE ** (jnp.arange(0, QK_ROPE, 2, dtype=F32) / QK_ROPE))
    ang = pos[:, None] * inv_freq[None, :]
    zpad = jnp.zeros((lp, LANE - QK_ROPE), F32)
    cos_t = jnp.concatenate([jnp.cos(ang), jnp.cos(ang), zpad], axis=1)
    sin_t = jnp.concatenate([jnp.sin(ang), jnp.sin(ang), zpad], axis=1)

    s5_in = (w['lam_re'][0], w['lam_im'][0], w['log_dt'][0], w['b_re'][0], w['b_im'][0])
    (a_re, a_im, bb_re, bb_im), s5_vjp = jax.vjp(_s5_discretize, *s5_in)
    a_l = _lanes_from_gp(a_re, a_im, cfg)
    bb_band = _bb_band(bb_re, bb_im, cfg).astype(BF16)
    cc_band = _cc_band(w['c_re'][0], w['c_im'][0], cfg).astype(BF16)
    d_skip, b_glu = w['d_skip'], w['b_glu']

    h0 = jnp.concatenate([jnp.zeros((PAD, d), F32), meta, x[0]], axis=0)
    xn = _rms_fwd(h0, w['mix_norm'], name="rms_mix")
    z = _mm(xn, w_in, name="mm_in", tn=_tile(cfg.DINP, 640))
    u = (z, ds, 0)
    q_a = (z, cfg.QL, ds // cfg.QL)
    kv_a = (z, cfg.KVL, (ds + cfg.QL) // cfg.KVL)
    k_pe = (z, LANE, (ds + cfg.QL + cfg.KVL) // LANE)

    bu = _mm(z, bb_band, name="mm_s5_bu", dims=(lp, nl, gw), tn=blk, tk=gw,
             a_idx=lambda i, j, k: (i, j), b_idx=lambda i, j, k: (j, 0))
    hs = _s5_scan(bu, a_l, reverse=False, name="s5_scan_fwd")
    yc = _mm(hs, cc_band, name="mm_s5_y", dims=(lp, ds, blk), tn=gw, tk=blk,
             a_idx=lambda i, j, k: (i, j), b_idx=lambda i, j, k: (j, 0))

    def s5_y(ycv, uv, dk):
        return ycv + dk * uv

    gl = _ew(lambda rid, ycv, uv, dk: jax.nn.gelu(s5_y(ycv, uv, dk)), [yc, u], [d_skip], [(ds, BF16)], name="s5_gelu")[0]
    tg = _mm(gl, w_glu, name="mm_glu")
    ya = _ew(lambda rid, ycv, uv, tv, dk, bg: jax.nn.gelu(s5_y(ycv, uv, dk)) * jax.nn.sigmoid(tv + bg),
             [yc, u, tg], [d_skip, b_glu], [(ds, F32)], name="s5_glu")[0]

    qn = _rms_fwd(q_a, w['q_a_norm'], name="rms_q")
    kvn = _rms_fwd(kv_a, w['kv_a_norm'], name="rms_kv")
    q_raw = _mm(qn, w_qt, tb=True, name="mm_q")
    qx = _ew(_rope_heads(_rope, cfg.H), [q_raw, cos_t, sin_t], [], [(cfg.H * HEAD_SLOT, BF16)], name="rope_q")[0]
    kv = _mm(kvn, w_kvt, tb=True, out_dtype=BF16, name="mm_kv")
    kr = _ew(lambda rid, kp, cs, sn: _rope(kp, cs, sn), [k_pe, cos_t, sin_t], [], [(LANE, BF16)], name="rope_k")[0]
    o, lse = _attn_fwd(qx, kv, kr, cfg, name="attn_fwd")

    def norm2(rid, yav, ov, gs, ga):
        return jnp.concatenate([_rms_parts(yav, gs)[0] * gs, _rms_parts(ov, ga)[0] * ga], axis=1)

    yn = _ew(norm2, [ya, o], [w['out_norm_ssm'], w['out_norm_attn']], [(cfg.DMIX, BF16)], name="rms_out")[0]
    h1 = _mm(yn, w_out, res=h0, name="mm_out")
    xn2 = _rms_fwd(h1, w['ffn_norm'], name="rms_ffn")
    up = _mm(xn2, w_upt, tb=True, name="mm_up")
    act = _conv_fwd(up, conv_w, conv_b, name="conv_fwd")
    h2 = _mm(act, w_down, res=h1, tm=_tile(lp, 544, ROW_ALIGN), name="mm_down")

    tgt = jnp.concatenate([jnp.zeros((PAD + N_META, d), F32), loss_target[0]], axis=0)
    g_final = w['final_norm'].reshape(1, d)

    def head(rid, hv, tv, gv):
        xhat, r = _rms_parts(hv, gv)
        valid = rid >= PAD + N_META
        diff = jnp.where(valid, xhat * gv - tv, 0.0)
        dout = diff * (1.0 / d)
        dxhat = dout * gv
        dx = r * (dxhat - xhat * jnp.mean(dxhat * xhat, axis=-1, keepdims=True))
        return dx, dx, dout * xhat, 0.5 * diff * dout

    dh2, dh2_b, dg_final, loss_cols = _ew(head, [h2, tgt], [g_final], [(d, F32), (d, BF16)], [d, d], name="loss_head")
    loss = lax.psum(jnp.sum(loss_cols), ("x", "y", "c"))

    dact = _mm(dh2_b, w_down, tb=True, out_dtype=BF16, name="mm_dact")
    dw_down = _mm(act, dh2_b, ta=True, tn=d, tm=512, name="mm_dw_down")
    dup, dconv_w, dconv_b = _conv_bwd(up, dact, conv_w, conv_b, name="conv_bwd")
    dxn2 = _mm(dup, w_upt, tk=1024, tn=1024, name="mm_dxn2")
    dw_upt = _mm(dup, xn2, ta=True, tn=d, tm=512, name="mm_dw_up")
    dh1, dh1_b, dg_ffn = _rms_bwd(h1, w['ffn_norm'], dxn2, res=dh2, mask=True, with_bf16=True, name="rms_ffn_bwd")

    dyn = _mm(dh1_b, w_out, tb=True, name="mm_dyn")
    dw_out = _mm(yn, dh1_b, ta=True, tn=d, tm=512, name="mm_dw_out")
    dya, dg_ssm = _rms_bwd(ya, w['out_norm_ssm'], (dyn, ds, 0), name="rms_ssm_bwd")
    do, dg_attn = _rms_bwd(o, w['out_norm_attn'], (dyn, cfg.DATTN, ds // cfg.DATTN), name="rms_attn_bwd")

    dqx, dkn, dv, dkr = _attn_bwd(qx, kv, kr, o, lse, do, cfg, name="attn_bwd")
    dq_raw = _ew(_rope_heads(_unrope, cfg.H), [dqx, cos_t, sin_t], [], [(cfg.H * HEAD_SLOT, BF16)], name="unrope_q")[0]
    dk_pe = _ew(lambda rid, dk, cs, sn: _unrope(dk, cs, sn), [dkr, cos_t, sin_t], [], [(LANE, F32)], name="unrope_k")[0]
    dqn = _mm(dq_raw, w_qt, name="mm_dqn")
    dw_qt = _mm(dq_raw, qn, ta=True, tm=512, name="mm_dw_q")
    dkv = jnp.stack([dkn.reshape(lp, cfg.H, QK_NOPE), dv.reshape(lp, cfg.H, V_HEAD)], axis=2).reshape(lp, -1).astype(BF16)
    dkvn = _mm(dkv, w_kvt, name="mm_dkvn")
    dw_kvt = _mm(dkv, kvn, ta=True, tm=512, name="mm_dw_kv")
    dq_a, dg_q = _rms_bwd(q_a, w['q_a_norm'], dqn, name="rms_q_bwd")
    dkv_a, dg_kv = _rms_bwd(kv_a, w['kv_a_norm'], dkvn, name="rms_kv_bwd")

    def glu_bwd(rid, ycv, uv, tv, dyav, dk, bg):
        gelu = jax.nn.gelu(s5_y(ycv, uv, dk))
        sg = jax.nn.sigmoid(tv + bg)
        dt = dyav * gelu * sg * (1.0 - sg)
        return dt, dyav * sg, dt

    dt_b, dgl1, db_glu = _ew(glu_bwd, [yc, u, tg, dya], [d_skip, b_glu], [(ds, BF16), (ds, F32)], [ds], name="s5_glu_bwd")
    dgl = _mm(dt_b, w_glu, tb=True, res=dgl1, name="mm_dgl")
    dw_glu = _mm(gl, dt_b, ta=True, tm=512, name="mm_dw_glu")

    def gelu_bwd(rid, ycv, uv, dglv, dk):
        _, vjp = jax.vjp(jax.nn.gelu, s5_y(ycv, uv, dk))
        dy = vjp(dglv)[0]
        return dy, dy * dk, dy * uv

    dy_b, du_skip, dd_skip = _ew(gelu_bwd, [yc, u, dgl], [d_skip], [(ds, BF16), (ds, F32)], [ds], name="s5_gelu_bwd")
    dhs = _mm(dy_b, cc_band, tb=True, name="mm_s5_dhs", dims=(lp, nl, gw), tn=blk, tk=gw,
              a_idx=lambda i, j, k: (i, j), b_idx=lambda i, j, k: (j, 0))
    dcc_band = _mm(hs, dy_b, ta=True, name="mm_s5_dcc", dims=(nl, gw, lp), tm=blk, tn=gw,
                   a_idx=lambda i, j, k: (0, i), b_idx=lambda i, j, k: (0, i))
    gs = _s5_scan(dhs, a_l, reverse=True, name="s5_scan_bwd")
    da_l = _s5_da(gs, hs, cfg, name="s5_da")
    dbb_band = _mm(z, gs, ta=True, name="mm_s5_dbb", dims=(ds, blk, lp), tm=gw, tn=blk,
                   a_idx=lambda i, j, k: (0, i), b_idx=lambda i, j, k: (0, i))
    du = _mm(gs, bb_band, tb=True, res=du_skip, name="mm_s5_du", dims=(lp, ds, blk), tn=gw, tk=blk,
             a_idx=lambda i, j, k: (i, j), b_idx=lambda i, j, k: (j, 0))

    dz = jnp.concatenate([du, dq_a, dkv_a, dk_pe], axis=1).astype(BF16)
    dxn = _mm(dz, w_in, tb=True, name="mm_dxn")
    dw_in = _mm(xn, dz, ta=True, tm=512, tn=_tile(cfg.DINP, 1024), name="mm_dw_in")
    dh0, dg_mix = _rms_bwd(h0, w['mix_norm'], dxn, res=dh1, name="rms_mix_bwd")
    grad_x = dh0[PAD + N_META:][None]

    da_re, da_im = _gp_from_lanes(da_l, cfg)
    dbb_re, dbb_im = _bb_from_band(dbb_band, cfg)
    dlam_re, dlam_im, dlog_dt, db_re, db_im = s5_vjp((da_re, da_im, dbb_re, dbb_im))
    dc_re, dc_im = _cc_from_band(dcc_band, cfg)
    local_small = {
        'meta_tokens': dh0[PAD:PAD + N_META], 'mix_norm': dg_mix, 'lam_re': dlam_re, 'lam_im': dlam_im, 'log_dt': dlog_dt,
        'b_re': db_re, 'b_im': db_im, 'c_re': dc_re, 'c_im': dc_im, 'd_skip': dd_skip, 'b_glu': db_glu, 'q_a_norm': dg_q,
        'kv_a_norm': dg_kv, 'out_norm_ssm': dg_ssm, 'out_norm_attn': dg_attn, 'ffn_norm': dg_ffn,
        'conv_w': _ff_unpad(dconv_w, cfg), 'conv_b': _ff_unpad(dconv_b, cfg), 'final_norm': dg_final,
    }
    small_shapes = [local_small[n].shape for n in SMALL]

    big_local = [dw_in, dw_glu, dw_qt, dw_kvt, dw_out, dw_upt, dw_down]
    reduced = _reduce_scatter(big_local + [_pack([local_small[n] for n in SMALL])], [BF16] * 7 + [F32], "grads")
    small_full = _allgather([_place_shard(reduced[7], F32, name="place_small")], name="allgather_small")[0]
    small_sum = dict(zip(SMALL, _unpack(small_full, small_shapes)))

    grads = {n: _from_comm_layout(n, g, cfg) for n, g in zip(BIG, reduced[:7])}
    for n in SMALL:
        g = small_sum[n]
        if n == 'meta_tokens':
            g = lax.dynamic_slice_in_dim(g, me * (d // 4), d // 4, axis=1)
        elif n == 'conv_w':
            g = lax.dynamic_slice_in_dim(g, me * (cfg.F // 4), cfg.F // 4, axis=1)[None]
        else:
            g = g.reshape(w[n].shape)
        grads[n] = g

    delta, new_m, new_v = {}, {}, {}
    for n in BIG:
        shp = w[n].shape
        r = [t.reshape(shp[-2], shp[-1]) for t in (w[n], grads[n], m[n], v[n])]
        dl, mn, vn = _adamw(*r, name=f"adamw_{n}")
        delta[n], new_m[n], new_v[n] = dl.reshape(shp), mn.reshape(shp), vn.reshape(shp)
    shapes = [w[n].shape for n in SMALL]
    packs = [_pack([src[n] for n in SMALL]) for src in (w, grads, m, v)]
    for dst, p in zip((delta, new_m, new_v), _adamw(*packs, name="adamw_small")):
        dst.update(zip(SMALL, _unpack(p, shapes)))

    return (loss, grad_x, *[grads[n] for n in WEIGHTS], *[delta[n] for n in WEIGHTS],
            *[new_m[n] for n in WEIGHTS], *[new_v[n] for n in WEIGHTS])


def kernel(x, meta_tokens, mix_norm, w_in, lam_re, lam_im, log_dt, b_re, b_im, c_re, c_im, d_skip, w_glu, b_glu, q_a_norm, w_q_b, kv_a_norm, w_kv_b, out_norm_ssm, out_norm_attn, w_out, ffn_norm, w_up, conv_w, conv_b, w_down, final_norm, loss_target, m_meta_tokens, m_mix_norm, m_w_in, m_lam_re, m_lam_im, m_log_dt, m_b_re, m_b_im, m_c_re, m_c_im, m_d_skip, m_w_glu, m_b_glu, m_q_a_norm, m_w_q_b, m_kv_a_norm, m_w_kv_b, m_out_norm_ssm, m_out_norm_attn, m_w_out, m_ffn_norm, m_w_up, m_conv_w, m_conv_b, m_w_down, m_final_norm, v_meta_tokens, v_mix_norm, v_w_in, v_lam_re, v_lam_im, v_log_dt, v_b_re, v_b_im, v_c_re, v_c_im, v_d_skip, v_w_glu, v_b_glu, v_q_a_norm, v_w_q_b, v_kv_a_norm, v_w_kv_b, v_out_norm_ssm, v_out_norm_attn, v_w_out, v_ffn_norm, v_w_up, v_conv_w, v_conv_b, v_w_down, v_final_norm):
    args = dict(locals())
    w = {n: args[n] for n in WEIGHTS}
    m = {n: args["m_" + n] for n in WEIGHTS}
    v = {n: args["v_" + n] for n in WEIGHTS}
    return _step(PROD, w, m, v, x, loss_target)
```

```python
import functools
import math
from typing import NamedTuple

import jax
import jax.numpy as jnp
from jax import lax
from jax.experimental import pallas as pl
from jax.experimental.pallas import tpu as pltpu

F32, BF16 = jnp.float32, jnp.bfloat16
MESH = pl.DeviceIdType.MESH
LANE = 128
ROW_ALIGN = 16
N_META = 16
PAD = 112
CHUNK = 64
SSM_GROUP = 16
SSM_STATE = 64
GROUPS_PER_BLOCK = 8
QK_NOPE, QK_ROPE, V_HEAD = 128, 64, 128
HEAD_SLOT = 256
ROPE_BASE = 10000.0
EPS = 1e-6
ADAM_LR, ADAM_B1, ADAM_B2, ADAM_EPS, ADAM_WD, ADAM_STEP = 0.001, 0.9, 0.999, 1e-08, 0.01, 10
DT_F32_BLOCK_BYTES = 1 << 20


class Cfg(NamedTuple):
    D: int
    S: int
    DS: int
    H: int
    QL: int
    KVL: int
    F: int

    @property
    def LP(self):
        return PAD + N_META + self.S

    @property
    def G(self):
        return self.DS // SSM_GROUP

    @property
    def NB(self):
        return self.G // GROUPS_PER_BLOCK

    @property
    def NL(self):
        return 2 * self.G * SSM_STATE

    @property
    def DATTN(self):
        return self.H * V_HEAD

    @property
    def DMIX(self):
        return self.DS + self.DATTN

    @property
    def DIN(self):
        return self.DS + self.QL + self.KVL + QK_ROPE

    @property
    def DINP(self):
        return self.DS + self.QL + self.KVL + LANE

    @property
    def FQ(self):
        return -(-(self.F // 4) // LANE) * LANE

    @property
    def FP(self):
        return 4 * self.FQ


PROD = Cfg(D=2048, S=2048, DS=1024, H=8, QL=512, KVL=256, F=5504)

WEIGHTS = ['meta_tokens', 'mix_norm', 'w_in', 'lam_re', 'lam_im', 'log_dt', 'b_re', 'b_im', 'c_re', 'c_im', 'd_skip',
           'w_glu', 'b_glu', 'q_a_norm', 'w_q_b', 'kv_a_norm', 'w_kv_b', 'out_norm_ssm', 'out_norm_attn', 'w_out',
           'ffn_norm', 'w_up', 'conv_w', 'conv_b', 'w_down', 'final_norm']
BIG = ['w_in', 'w_glu', 'w_q_b', 'w_kv_b', 'w_out', 'w_up', 'w_down']
SMALL = [n for n in WEIGHTS if n not in BIG]


def _pc(body, **kw):
    return pl.pallas_call(body, **kw)


def _tile(n, target, align=LANE):
    best = None
    d = align
    while d <= min(n, target):
        if n % d == 0:
            best = d
        d += align
    return best if best is not None else n


def _row_tile(rows, cols):
    return _tile(rows, max(ROW_ALIGN, DT_F32_BLOCK_BYTES // (4 * cols)), ROW_ALIGN)


def _mm(a, b, *, name, ta=False, tb=False, tm=None, tn=512, tk=None, out_dtype=F32, res=None,
        a_idx=None, b_idx=None, dims=None):
    if dims is None:
        m, k = (a.shape[1], a.shape[0]) if ta else a.shape
        n = b.shape[0] if tb else b.shape[1]
    else:
        m, n, k = dims
    tm = _tile(m, tm or m, LANE if ta else ROW_ALIGN)
    tn = _tile(n, tn)
    tk = _tile(k, tk or k, ROW_ALIGN if (ta and not tb) else LANE)
    nm, nn, nk = m // tm, n // tn, k // tk
    a_idx = a_idx or ((lambda i, j, kk: (kk, i)) if ta else (lambda i, j, kk: (i, kk)))
    b_idx = b_idx or ((lambda i, j, kk: (j, kk)) if tb else (lambda i, j, kk: (kk, j)))
    dn = (((0 if ta else 1,), (1 if tb else 0,)), ((), ()))

    def body(*refs):
        a_ref, b_ref = refs[0], refs[1]
        r_ref = refs[2] if res is not None else None
        o_ref = refs[3] if res is not None else refs[2]
        d = lax.dot_general(a_ref[...].astype(BF16), b_ref[...].astype(BF16), dn, preferred_element_type=F32)

        def finish(r):
            if r_ref is not None:
                r = r + r_ref[...].astype(F32)
            o_ref[...] = r.astype(out_dtype)

        if nk == 1:
            finish(d)
        else:
            acc = refs[-1]
            kk = pl.program_id(2)

            @pl.when(kk == 0)
            def _():
                acc[...] = d

            @pl.when(kk > 0)
            def _():
                acc[...] += d

            @pl.when(kk == nk - 1)
            def _():
                finish(acc[...])

    in_specs = [pl.BlockSpec((tk, tm) if ta else (tm, tk), a_idx), pl.BlockSpec((tn, tk) if tb else (tk, tn), b_idx)]
    args = [a, b]
    if res is not None:
        in_specs.append(pl.BlockSpec((tm, tn), lambda i, j, kk: (i, j)))
        args.append(res)
    return _pc(body, name=name, grid=(nm, nn, nk), in_specs=in_specs,
               out_specs=pl.BlockSpec((tm, tn), lambda i, j, kk: (i, j)),
               out_shape=jax.ShapeDtypeStruct((m, n), out_dtype),
               scratch_shapes=[pltpu.VMEM((tm, tn), F32)] if nk > 1 else [],
               compiler_params=pltpu.CompilerParams(dimension_semantics=("parallel", "parallel", "arbitrary")))(*args)


def _ew(fn, ins, vecs, outs, sums=(), *, name, tm=None):
    ins = [x if isinstance(x, tuple) else (x, x.shape[1], 0) for x in ins]
    rows = ins[0][0].shape[0]
    cmax = max([c for _, c, _ in ins] + [c for c, _ in outs])
    tm = tm or _row_tile(rows, cmax)
    n_in, n_vec, n_out, n_sum = len(ins), len(vecs), len(outs), len(sums)

    def body(*refs):
        i = pl.program_id(0)
        rid = i * tm + lax.broadcasted_iota(jnp.int32, (tm, 1), 0)
        vals = [r[...] for r in refs[:n_in + n_vec]]
        res = fn(rid, *vals)
        res = res if isinstance(res, (tuple, list)) else (res,)
        o_refs = refs[n_in + n_vec:]
        for o_ref, r in zip(o_refs[:n_out], res[:n_out]):
            o_ref[...] = r.astype(o_ref.dtype)
        for o_ref, r in zip(o_refs[n_out:], res[n_out:]):
            part = jnp.sum(r.astype(F32), axis=0, keepdims=True)

            @pl.when(i == 0)
            def _():
                o_ref[...] = part

            @pl.when(i > 0)
            def _():
                o_ref[...] += part

    in_specs = [pl.BlockSpec((tm, c), functools.partial(lambda i, cb: (i, cb), cb=cb)) for _, c, cb in ins]
    in_specs += [pl.BlockSpec(v.shape, functools.partial(lambda i, nd: (0,) * nd, nd=v.ndim)) for v in vecs]
    out_specs = [pl.BlockSpec((tm, c), lambda i: (i, 0)) for c, _ in outs] + [pl.BlockSpec((1, c), lambda i: (0, 0)) for c in sums]
    out_shape = [jax.ShapeDtypeStruct((rows, c), dt) for c, dt in outs] + [jax.ShapeDtypeStruct((1, c), F32) for c in sums]
    return _pc(body, name=name, grid=(rows // tm,), in_specs=in_specs, out_specs=out_specs, out_shape=out_shape,
               compiler_params=pltpu.CompilerParams(dimension_semantics=("arbitrary",)))(*[x for x, _, _ in ins], *vecs)


def _rms_parts(x, g):
    r = lax.rsqrt(jnp.mean(x * x, axis=-1, keepdims=True) + EPS)
    return x * r, r


def _rms_bwd_block(x, g, dy):
    xhat, r = _rms_parts(x, g)
    dxhat = dy * g
    dx = r * (dxhat - xhat * jnp.mean(dxhat * xhat, axis=-1, keepdims=True))
    return dx, dy * xhat


def _rms_fwd(x, g, *, name):
    c = x[1] if isinstance(x, tuple) else x.shape[1]
    return _ew(lambda rid, xv, gv: _rms_parts(xv.astype(F32), gv)[0] * gv, [x], [g], [(c, BF16)], name=name)[0]


def _rms_bwd(x, g, dy, *, name, res=None, mask=False, with_bf16=False):
    c = x[1] if isinstance(x, tuple) else x.shape[1]

    def fn(rid, xv, dyv, *rest):
        gv = rest[-1]
        dx, dg = _rms_bwd_block(xv.astype(F32), gv, dyv.astype(F32))
        if res is not None:
            dx = dx + rest[0]
        if mask:
            dx = jnp.where(rid >= PAD, dx, 0.0)
        return (dx, dx, dg) if with_bf16 else (dx, dg)

    ins = [x, dy] + ([res] if res is not None else [])
    outs = [(c, F32)] + ([(c, BF16)] if with_bf16 else [])
    return _ew(fn, ins, [g], outs, [c], name=name)


def _s5_scan(bu, a_l, *, reverse, name):
    lp, nl = bu.shape
    w = GROUPS_PER_BLOCK * SSM_STATE
    unroll = 8

    def body(bu_ref, a_ref, hs_ref):
        ar = a_ref[:, :w]
        ai = -a_ref[:, w:] if reverse else a_ref[:, w:]

        def step(n, carry):
            hr, hi = carry
            for q in range(unroll):
                t = n * unroll + q
                t = lp - 1 - t if reverse else t
                nr = ar * hr - ai * hi + bu_ref[pl.ds(t, 1), :w]
                ni = ar * hi + ai * hr + bu_ref[pl.ds(t, 1), w:]
                hs_ref[pl.ds(t, 1), :w] = nr
                hs_ref[pl.ds(t, 1), w:] = ni
                hr, hi = nr, ni
            return hr, hi

        z = jnp.zeros((1, w), F32)
        lax.fori_loop(0, lp // unroll, step, (z, z))

    return _pc(body, name=name, grid=(nl // (2 * w),),
               in_specs=[pl.BlockSpec((lp, 2 * w), lambda j: (0, j)), pl.BlockSpec((1, 2 * w), lambda j: (0, j))],
               out_specs=pl.BlockSpec((lp, 2 * w), lambda j: (0, j)),
               out_shape=jax.ShapeDtypeStruct((lp, nl), F32),
               compiler_params=pltpu.CompilerParams(dimension_semantics=("parallel",)))(bu, a_l)


def _s5_da(gs, hs, cfg, *, name):
    lp, nl = gs.shape
    w = GROUPS_PER_BLOCK * SSM_STATE
    tc = w // 2
    per = 2 * w // tc

    def body(gr_ref, gi_ref, hr_ref, hi_ref, dre_ref, dim_ref):
        keep = lax.broadcasted_iota(jnp.int32, (lp, 1), 0) >= 1
        hr = jnp.where(keep, pltpu.roll(hr_ref[...], 1, 0), 0.0)
        hi = jnp.where(keep, pltpu.roll(hi_ref[...], 1, 0), 0.0)
        gr, gi = gr_ref[...], gi_ref[...]
        dre_ref[...] = jnp.sum(gr * hr + gi * hi, axis=0, keepdims=True)
        dim_ref[...] = jnp.sum(gi * hr - gr * hi, axis=0, keepdims=True)

    re_blk = pl.BlockSpec((lp, tc), lambda j, q: (0, per * j + q))
    im_blk = pl.BlockSpec((lp, tc), lambda j, q: (0, per * j + per // 2 + q))
    out_blk = pl.BlockSpec((1, tc), lambda j, q: (0, (per // 2) * j + q))
    dre, dim = _pc(body, name=name, grid=(cfg.NB, per // 2), in_specs=[re_blk, im_blk, re_blk, im_blk],
                   out_specs=[out_blk, out_blk], out_shape=[jax.ShapeDtypeStruct((1, nl // 2), F32)] * 2,
                   compiler_params=pltpu.CompilerParams(dimension_semantics=("parallel", "parallel")))(gs, gs, hs, hs)
    return jnp.stack([dre.reshape(cfg.NB, w), dim.reshape(cfg.NB, w)], axis=1).reshape(1, nl)


def _conv_gate(pre, cw, cb):
    return cw[0:1] * pltpu.roll(pre, 2, 0) + cw[1:2] * pltpu.roll(pre, 1, 0) + cw[2:3] * pre + cb


def _conv_fwd(up, cw, cb, *, name):
    lp, fp2 = up.shape
    fp = fp2 // 2
    tc = _tile(fp, 256)
    nb = fp // tc

    def body(pre_ref, val_ref, cw_ref, cb_ref, o_ref):
        gate = _conv_gate(pre_ref[...], cw_ref[...], cb_ref[...])
        o_ref[...] = (jax.nn.silu(gate) * val_ref[...]).astype(BF16)

    return _pc(body, name=name, grid=(nb,),
               in_specs=[pl.BlockSpec((lp, tc), lambda j: (0, j)), pl.BlockSpec((lp, tc), lambda j: (0, nb + j)),
                         pl.BlockSpec((3, tc), lambda j: (0, j)), pl.BlockSpec((1, tc), lambda j: (0, j))],
               out_specs=pl.BlockSpec((lp, tc), lambda j: (0, j)),
               out_shape=jax.ShapeDtypeStruct((lp, fp), BF16),
               compiler_params=pltpu.CompilerParams(dimension_semantics=("parallel",)))(up, up, cw, cb)


def _conv_bwd(up, dact, cw, cb, *, name):
    lp, fp2 = up.shape
    fp = fp2 // 2
    tc = _tile(fp, 256)
    nb = fp // tc

    def body(pre_ref, val_ref, da_ref, cw_ref, cb_ref, dup_ref, dcw_ref, dcb_ref):
        which = pl.program_id(0)
        pre, val, da, cwv = pre_ref[...], val_ref[...], da_ref[...].astype(F32), cw_ref[...]
        gate = _conv_gate(pre, cwv, cb_ref[...])
        sg = jax.nn.sigmoid(gate)
        silu = gate * sg

        @pl.when(which == 1)
        def _():
            dup_ref[...] = (da * silu).astype(BF16)

        @pl.when(which == 0)
        def _():
            dgate = da * val * (sg * (1.0 + gate * (1.0 - sg)))
            dpre = cwv[2:3] * dgate + cwv[1:2] * pltpu.roll(dgate, lp - 1, 0) + cwv[0:1] * pltpu.roll(dgate, lp - 2, 0)
            dup_ref[...] = dpre.astype(BF16)
            dcb_ref[...] = jnp.sum(dgate, axis=0, keepdims=True)
            dcw_ref[0:1, :] = jnp.sum(dgate * pltpu.roll(pre, 2, 0), axis=0, keepdims=True)
            dcw_ref[1:2, :] = jnp.sum(dgate * pltpu.roll(pre, 1, 0), axis=0, keepdims=True)
            dcw_ref[2:3, :] = jnp.sum(dgate * pre, axis=0, keepdims=True)

    return _pc(body, name=name, grid=(2, nb),
               in_specs=[pl.BlockSpec((lp, tc), lambda s, j: (0, j)), pl.BlockSpec((lp, tc), lambda s, j: (0, nb + j)),
                         pl.BlockSpec((lp, tc), lambda s, j: (0, j)),
                         pl.BlockSpec((3, tc), lambda s, j: (0, j)), pl.BlockSpec((1, tc), lambda s, j: (0, j))],
               out_specs=[pl.BlockSpec((lp, tc), lambda s, j: (0, s * nb + j)),
                          pl.BlockSpec((3, tc), lambda s, j: (0, j * (1 - s) + (nb - 1) * s)),
                          pl.BlockSpec((1, tc), lambda s, j: (0, j * (1 - s) + (nb - 1) * s))],
               out_shape=[jax.ShapeDtypeStruct((lp, fp2), BF16), jax.ShapeDtypeStruct((3, fp), F32),
                          jax.ShapeDtypeStruct((1, fp), F32)],
               compiler_params=pltpu.CompilerParams(dimension_semantics=("arbitrary", "arbitrary")))(up, up, dact, cw, cb)


def _attn_mask(i, tq, lp):
    qrow = i * tq + lax.broadcasted_iota(jnp.int32, (tq, 1), 0)
    krow = lax.broadcasted_iota(jnp.int32, (1, lp), 1)
    return (krow >= PAD) & ((krow // CHUNK) <= (qrow // CHUNK)), qrow >= PAD


def _attn_scores(q, kn, kr, i, tq, lp, scale):
    nt = (((1,), (1,)), ((), ()))
    s = lax.dot_general(q[:, :QK_NOPE], kn, nt, preferred_element_type=F32)
    s = s + lax.dot_general(q[:, QK_NOPE:], kr, nt, preferred_element_type=F32)
    mask, qvalid = _attn_mask(i, tq, lp)
    return jnp.where(mask, s * scale, jnp.finfo(F32).min), qvalid


def _attn_fwd(qx, kv, kr, cfg, *, name):
    lp, h = cfg.LP, cfg.H
    tq = _tile(lp, 272, ROW_ALIGN)
    scale = 1.0 / math.sqrt(QK_NOPE + QK_ROPE)

    def body(q_ref, kn_ref, v_ref, kr_ref, o_ref, lse_ref):
        i = pl.program_id(1)
        s, qvalid = _attn_scores(q_ref[...], kn_ref[...], kr_ref[...], i, tq, lp, scale)
        m = jnp.max(s, axis=-1, keepdims=True)
        p = jnp.exp(s - m)
        l = jnp.sum(p, axis=-1, keepdims=True)
        o = jnp.dot(p.astype(BF16), v_ref[...], preferred_element_type=F32) / l
        o_ref[...] = jnp.where(qvalid, o, 0.0)
        lse_ref[...] = m + jnp.log(l)

    return _pc(body, name=name, grid=(h, lp // tq),
               in_specs=[pl.BlockSpec((tq, HEAD_SLOT), lambda hh, i: (i, hh)),
                         pl.BlockSpec((lp, QK_NOPE), lambda hh, i: (0, 2 * hh)),
                         pl.BlockSpec((lp, V_HEAD), lambda hh, i: (0, 2 * hh + 1)),
                         pl.BlockSpec((lp, LANE), lambda hh, i: (0, 0))],
               out_specs=[pl.BlockSpec((tq, V_HEAD), lambda hh, i: (i, hh)),
                          pl.BlockSpec((None, tq, 1), lambda hh, i: (hh, i, 0))],
               out_shape=[jax.ShapeDtypeStruct((lp, h * V_HEAD), F32), jax.ShapeDtypeStruct((h, lp, 1), F32)],
               compiler_params=pltpu.CompilerParams(dimension_semantics=("parallel", "parallel")))(qx, kv, kv, kr)


def _attn_bwd(qx, kv, kr, o, lse, do, cfg, *, name):
    lp, h = cfg.LP, cfg.H
    tq = _tile(lp, 272, ROW_ALIGN)
    scale = 1.0 / math.sqrt(QK_NOPE + QK_ROPE)
    tn_dims = (((0,), (0,)), ((), ()))

    def body(q_ref, kn_ref, v_ref, kr_ref, o_ref, lse_ref, do_ref, dq_ref, dkn_ref, dv_ref, dkr_ref):
        hh, i = pl.program_id(0), pl.program_id(1)
        q, kn, v, krv = q_ref[...], kn_ref[...], v_ref[...], kr_ref[...]
        s, qvalid = _attn_scores(q, kn, krv, i, tq, lp, scale)
        dov = jnp.where(qvalid, do_ref[...], 0.0)
        p = jnp.exp(s - lse_ref[...])
        delta = jnp.sum(dov * o_ref[...], axis=-1, keepdims=True)
        dob = dov.astype(BF16)
        dp = lax.dot_general(dob, v, (((1,), (1,)), ((), ())), preferred_element_type=F32)
        ds = (p * (dp - delta) * scale).astype(BF16)
        dq_ref[:, :QK_NOPE] = jnp.dot(ds, kn, preferred_element_type=F32)
        dq_ref[:, QK_NOPE:] = jnp.dot(ds, krv, preferred_element_type=F32)
        dkn = lax.dot_general(ds, q[:, :QK_NOPE], tn_dims, preferred_element_type=F32)
        dkr = lax.dot_general(ds, q[:, QK_NOPE:], tn_dims, preferred_element_type=F32)
        dv = lax.dot_general(p.astype(BF16), dob, tn_dims, preferred_element_type=F32)

        @pl.when(i == 0)
        def _():
            dkn_ref[...] = dkn
            dv_ref[...] = dv

        @pl.when(i > 0)
        def _():
            dkn_ref[...] += dkn
            dv_ref[...] += dv

        @pl.when((i == 0) & (hh == 0))
        def _():
            dkr_ref[...] = dkr

        @pl.when((i > 0) | (hh > 0))
        def _():
            dkr_ref[...] += dkr

    return _pc(body, name=name, grid=(h, lp // tq),
               in_specs=[pl.BlockSpec((tq, HEAD_SLOT), lambda hh, i: (i, hh)),
                         pl.BlockSpec((lp, QK_NOPE), lambda hh, i: (0, 2 * hh)),
                         pl.BlockSpec((lp, V_HEAD), lambda hh, i: (0, 2 * hh + 1)),
                         pl.BlockSpec((lp, LANE), lambda hh, i: (0, 0)),
                         pl.BlockSpec((tq, V_HEAD), lambda hh, i: (i, hh)),
                         pl.BlockSpec((None, tq, 1), lambda hh, i: (hh, i, 0)),
                         pl.BlockSpec((tq, V_HEAD), lambda hh, i: (i, hh))],
               out_specs=[pl.BlockSpec((tq, HEAD_SLOT), lambda hh, i: (i, hh)),
                          pl.BlockSpec((lp, QK_NOPE), lambda hh, i: (0, hh)),
                          pl.BlockSpec((lp, V_HEAD), lambda hh, i: (0, hh)),
                          pl.BlockSpec((lp, LANE), lambda hh, i: (0, 0))],
               out_shape=[jax.ShapeDtypeStruct((lp, h * HEAD_SLOT), F32), jax.ShapeDtypeStruct((lp, h * QK_NOPE), F32),
                          jax.ShapeDtypeStruct((lp, h * V_HEAD), F32), jax.ShapeDtypeStruct((lp, LANE), F32)],
               compiler_params=pltpu.CompilerParams(dimension_semantics=("arbitrary", "arbitrary")))(qx, kv, kv, kr, o, lse, do)


def _rot_half(x):
    lane = lax.broadcasted_iota(jnp.int32, x.shape, 1)
    half = QK_ROPE // 2
    return jnp.where(lane < half, -pltpu.roll(x, LANE - half, 1), pltpu.roll(x, half, 1))


def _rope(x, cos, sin):
    return x * cos + _rot_half(x) * sin


def _unrope(dy, cos, sin):
    return dy * cos - _rot_half(dy * sin)


def _rope_heads(fn, h):
    def apply(rid, q, cos, sin):
        parts = []
        for hh in range(h):
            parts.append(q[:, hh * HEAD_SLOT: hh * HEAD_SLOT + QK_NOPE])
            parts.append(fn(q[:, hh * HEAD_SLOT + QK_NOPE: (hh + 1) * HEAD_SLOT], cos, sin))
        return jnp.concatenate(parts, axis=1)
    return apply


ANY = pl.BlockSpec(memory_space=pl.ANY)


def _place():
    x, y, c = lax.axis_index("x"), lax.axis_index("y"), lax.axis_index("c")
    chips = [(1 - x, y), (x, 1 - y), (1 - x, 1 - y)]
    return x, y, c, chips


def _rcopy(src, dst, send_sem, recv_sem, dev):
    return pltpu.make_async_remote_copy(src_ref=src, dst_ref=dst, send_sem=send_sem, recv_sem=recv_sem,
                                        device_id=dev, device_id_type=MESH)


def _place_shard(shard, dtype, *, name):
    r, cols = shard.shape
    tm = _row_tile(r, cols)
    nblk = r // tm
    me = (2 * lax.axis_index("x") + lax.axis_index("y")).astype(jnp.int32).reshape(1)

    def body(me_ref, s_ref, o_ref):
        o_ref[...] = s_ref[...].astype(dtype)

    return _pc(body, name=name,
               grid_spec=pltpu.PrefetchScalarGridSpec(
                   num_scalar_prefetch=1, grid=(nblk,),
                   in_specs=[pl.BlockSpec((tm, cols), lambda i, mr: (i, 0))],
                   out_specs=pl.BlockSpec((tm, cols), lambda i, mr: (mr[0] * nblk + i, 0))),
               out_shape=jax.ShapeDtypeStruct((4 * r, cols), dtype),
               compiler_params=pltpu.CompilerParams(dimension_semantics=("arbitrary",)))(me, shard)


def _allgather(fulls, *, name):
    n = len(fulls)

    def body(*refs):
        outs = refs[n:2 * n]
        send_sems, recv_sems = refs[2 * n:]
        x, y, c, chips = _place()
        sib = (x, y, 1 - c)
        me = 2 * x + y

        def rows(t, s, half):
            hrows = outs[t].shape[0] // 8
            return outs[t].at[pl.ds((2 * s + half) * hrows, hrows)]

        sent = []
        for t in range(n):
            for j, (cx, cy) in enumerate(chips):
                cp = _rcopy(rows(t, me, c), rows(t, me, c), send_sems.at[6 * t + j], recv_sems.at[6 * t + j], (cx, cy, c))
                cp.start()
                sent.append(cp)
        for t in range(n):
            for j, (cx, cy) in enumerate(chips):
                landed = rows(t, 2 * cx + cy, c)
                _rcopy(landed, landed, send_sems.at[6 * t + j], recv_sems.at[6 * t + j], (cx, cy, c)).wait_recv()
                cp = _rcopy(landed, landed, send_sems.at[6 * t + 3 + j], recv_sems.at[6 * t + 3 + j], sib)
                cp.start()
                sent.append(cp)
        for t in range(n):
            for j, (cx, cy) in enumerate(chips):
                other = rows(t, 2 * cx + cy, 1 - c)
                _rcopy(other, other, send_sems.at[6 * t + 3 + j], recv_sems.at[6 * t + 3 + j], sib).wait_recv()
        for cp in sent:
            cp.wait_send()

    return _pc(body, name=name, in_specs=[ANY] * n, out_specs=[ANY] * n,
               out_shape=[jax.ShapeDtypeStruct(f.shape, f.dtype) for f in fulls],
               input_output_aliases={t: t for t in range(n)},
               scratch_shapes=[pltpu.SemaphoreType.DMA((6 * n,)), pltpu.SemaphoreType.DMA((6 * n,))])(*fulls)


HBM = pl.BlockSpec(memory_space=pltpu.HBM)
SEM = pl.BlockSpec(memory_space=pltpu.SEMAPHORE)
EFFECT = pltpu.SideEffectType.DATAFLOW_SIDE_EFFECTING
TOKEN = jax.ShapeDtypeStruct((8, LANE), F32)


def _in_hbm(a):
    return pltpu.with_memory_space_constraint(a, pltpu.HBM)


def _half_rows(ref, s, half):
    hrows = ref.shape[0] // 8
    return ref.at[pl.ds((2 * s + half) * hrows, hrows)]


def _allgather_ici_start(fulls, before, *, name):
    n = len(fulls)

    def body(*refs):
        ins, send_sems, recv_sems, token = refs[:n], refs[n + 1], refs[n + 2], refs[-1]
        x, y, c, chips = _place()
        me = 2 * x + y
        for t in range(n):
            for j, (cx, cy) in enumerate(chips):
                mine = _half_rows(ins[t], me, c)
                _rcopy(mine, mine, send_sems.at[3 * t + j], recv_sems.at[3 * t + j], (cx, cy, c)).start()
        token[...] = jnp.zeros_like(token)

    res = _pc(body, name=name, in_specs=[HBM] * n + [ANY],
              out_specs=[SEM, SEM] + [HBM] * n + [pl.BlockSpec(memory_space=pltpu.VMEM)],
              out_shape=[pltpu.SemaphoreType.DMA((3 * n,)), pltpu.SemaphoreType.DMA((3 * n,))]
              + [pltpu.HBM(f.shape, f.dtype) for f in fulls] + [TOKEN],
              input_output_aliases={t: 2 + t for t in range(n)},
              compiler_params=pltpu.CompilerParams(has_side_effects=EFFECT))(*[_in_hbm(f) for f in fulls], before)
    return res[0], res[1], res[2:2 + n], res[-1]


def _allgather_ici_wait(send_sems, recv_sems, fulls, after, *, name):
    n = len(fulls)

    def body(*refs):
        ins, send_ref, recv_ref = refs[:n], refs[n], refs[n + 1]
        x, y, c, chips = _place()
        me = 2 * x + y
        for t in range(n):
            for j, (cx, cy) in enumerate(chips):
                cp = _rcopy(_half_rows(ins[t], me, c), _half_rows(ins[t], 2 * cx + cy, c), send_ref.at[3 * t + j],
                            recv_ref.at[3 * t + j], (cx, cy, c))
                cp.wait_send()
                cp.wait_recv()

    return _pc(body, name=name, in_specs=[HBM] * n + [SEM, SEM, ANY], out_specs=[HBM] * n,
               out_shape=[pltpu.HBM(f.shape, f.dtype) for f in fulls],
               input_output_aliases={t: t for t in range(n)},
               compiler_params=pltpu.CompilerParams(has_side_effects=EFFECT))(*fulls, send_sems, recv_sems, after)


def _allgather_forward(fulls, *, name):
    n = len(fulls)

    def body(*refs):
        outs = refs[n:2 * n]
        send_sems, recv_sems = refs[2 * n:]
        x, y, c, chips = _place()
        sent = []
        for t in range(n):
            for j, (cx, cy) in enumerate(chips):
                landed = _half_rows(outs[t], 2 * cx + cy, c)
                cp = _rcopy(landed, landed, send_sems.at[3 * t + j], recv_sems.at[3 * t + j], (x, y, 1 - c))
                cp.start()
                sent.append(cp)
        for t in range(n):
            for j, (cx, cy) in enumerate(chips):
                other = _half_rows(outs[t], 2 * cx + cy, 1 - c)
                _rcopy(other, other, send_sems.at[3 * t + j], recv_sems.at[3 * t + j], (x, y, 1 - c)).wait_recv()
        for cp in sent:
            cp.wait_send()

    return _pc(body, name=name, in_specs=[ANY] * n, out_specs=[ANY] * n,
               out_shape=[jax.ShapeDtypeStruct(f.shape, f.dtype) for f in fulls],
               input_output_aliases={t: t for t in range(n)},
               scratch_shapes=[pltpu.SemaphoreType.DMA((3 * n,)), pltpu.SemaphoreType.DMA((3 * n,))])(*fulls)


def _rs_chips_start(sends, *, name):
    n = len(sends)

    def body(*refs):
        s_refs, b_refs, send_sems, recv_sems, token = refs[:n], refs[n:2 * n], refs[2 * n], refs[2 * n + 1], refs[-1]
        x, y, c, chips = _place()
        for t in range(n):
            for j, (cx, cy) in enumerate(chips):
                _rcopy(s_refs[t].at[2 * cx + cy], b_refs[t].at[j], send_sems.at[3 * t + j], recv_sems.at[3 * t + j],
                       (cx, cy, c)).start()
        token[...] = jnp.zeros_like(token)

    lands = [lax.empty((3,) + s.shape[1:], s.dtype) for s in sends]
    res = _pc(body, name=name, in_specs=[HBM] * (2 * n),
              out_specs=[SEM, SEM] + [HBM] * (2 * n) + [pl.BlockSpec(memory_space=pltpu.VMEM)],
              out_shape=[pltpu.SemaphoreType.DMA((3 * n,)), pltpu.SemaphoreType.DMA((3 * n,))]
              + [pltpu.HBM(a.shape, a.dtype) for a in sends + lands] + [TOKEN],
              input_output_aliases={t: 2 + t for t in range(2 * n)},
              compiler_params=pltpu.CompilerParams(has_side_effects=EFFECT))(*[_in_hbm(a) for a in sends + lands])
    return res[0], res[1], res[2:2 + n], res[2 + n:2 + 2 * n], res[-1]


def _rs_chips_wait(send_sems, recv_sems, sends, lands, after, *, name):
    n = len(sends)

    def body(*refs):
        s_refs, b_refs, send_ref, recv_ref = refs[:n], refs[n:2 * n], refs[2 * n], refs[2 * n + 1]
        x, y, c, chips = _place()
        for t in range(n):
            for j, (cx, cy) in enumerate(chips):
                cp = _rcopy(s_refs[t].at[2 * cx + cy], b_refs[t].at[j], send_ref.at[3 * t + j], recv_ref.at[3 * t + j],
                            (cx, cy, c))
                cp.wait_send()
                cp.wait_recv()

    res = _pc(body, name=name, in_specs=[HBM] * (2 * n) + [SEM, SEM, ANY], out_specs=[HBM] * (2 * n),
              out_shape=[pltpu.HBM(a.shape, a.dtype) for a in list(sends) + list(lands)],
              input_output_aliases={t: t for t in range(2 * n)},
              compiler_params=pltpu.CompilerParams(has_side_effects=EFFECT))(*sends, *lands, send_sems, recv_sems, after)
    return res[n:]


def _rs_sibling(grads, *, name):
    n = len(grads)

    def body(*refs):
        ins, outs = refs[:n], refs[n:2 * n]
        send_sems, recv_sems = refs[2 * n:]
        x, y, c, _ = _place()
        cps = []
        for t in range(n):
            h = ins[t].shape[0] // 8
            for s in range(4):
                cp = _rcopy(ins[t].at[pl.ds((2 * s + 1 - c) * h, h)], outs[t].at[s], send_sems.at[4 * t + s],
                            recv_sems.at[4 * t + s], (x, y, 1 - c))
                cp.start()
                cps.append(cp)
        for cp in cps:
            cp.wait()

    return _pc(body, name=name, in_specs=[ANY] * n, out_specs=[ANY] * n,
               out_shape=[jax.ShapeDtypeStruct((4, g.shape[0] // 8, g.shape[1]), g.dtype) for g in grads],
               scratch_shapes=[pltpu.SemaphoreType.DMA((4 * n,)), pltpu.SemaphoreType.DMA((4 * n,))])(*grads)


def _rs_chips(sends, *, name):
    n = len(sends)

    def body(*refs):
        s_refs, b_refs = refs[:n], refs[n:2 * n]
        send_sems, recv_sems = refs[2 * n:]
        x, y, c, chips = _place()
        cps = []
        for t in range(n):
            for j, (cx, cy) in enumerate(chips):
                cp = _rcopy(s_refs[t].at[2 * cx + cy], b_refs[t].at[j], send_sems.at[3 * t + j], recv_sems.at[3 * t + j],
                            (cx, cy, c))
                cp.start()
                cps.append(cp)
        for cp in cps:
            cp.wait()

    return _pc(body, name=name, in_specs=[ANY] * n, out_specs=[ANY] * n,
               out_shape=[jax.ShapeDtypeStruct((3,) + s.shape[1:], s.dtype) for s in sends],
               scratch_shapes=[pltpu.SemaphoreType.DMA((3 * n,)), pltpu.SemaphoreType.DMA((3 * n,))])(*sends)


def _rs_final(fulls, *, name):
    n = len(fulls)

    def body(*refs):
        outs = refs[n:2 * n]
        send_sems, recv_sems = refs[2 * n:]
        x, y, c, _ = _place()
        cps = []
        for t in range(n):
            cp = _rcopy(outs[t].at[c], outs[t].at[c], send_sems.at[t], recv_sems.at[t], (x, y, 1 - c))
            cp.start()
            cps.append(cp)
        for cp in cps:
            cp.wait()

    return _pc(body, name=name, in_specs=[ANY] * n, out_specs=[ANY] * n,
               out_shape=[jax.ShapeDtypeStruct(f.shape, f.dtype) for f in fulls],
               input_output_aliases={t: t for t in range(n)},
               scratch_shapes=[pltpu.SemaphoreType.DMA((n,)), pltpu.SemaphoreType.DMA((n,))])(*fulls)


def _add_halves(g, a, send_dtype, *, name):
    _, h, cols = a.shape
    th = _row_tile(h, cols)
    g4 = g.reshape(4, 2, h, cols)
    c = lax.axis_index("c").astype(jnp.int32).reshape(1)

    def body(c_ref, g_ref, a_ref, p_ref, s_ref):
        v = g_ref[...] + a_ref[...]
        p_ref[...] = v
        s_ref[...] = v.astype(send_dtype)

    return _pc(body, name=name,
               grid_spec=pltpu.PrefetchScalarGridSpec(
                   num_scalar_prefetch=1, grid=(4, h // th),
                   in_specs=[pl.BlockSpec((None, None, th, cols), lambda s, i, cr: (s, cr[0], i, 0)),
                             pl.BlockSpec((None, th, cols), lambda s, i, cr: (s, i, 0))],
                   out_specs=[pl.BlockSpec((None, th, cols), lambda s, i, cr: (s, i, 0))] * 2),
               out_shape=[jax.ShapeDtypeStruct(a.shape, F32), jax.ShapeDtypeStruct(a.shape, send_dtype)],
               compiler_params=pltpu.CompilerParams(dimension_semantics=("arbitrary", "arbitrary")))(c, g4, a)


def _add_chips(p, b, *, name):
    _, h, cols = p.shape
    th = _row_tile(h, cols)
    idx = jnp.stack([2 * lax.axis_index("x") + lax.axis_index("y"), lax.axis_index("c")]).astype(jnp.int32)

    def body(idx_ref, p_ref, b_ref, r_ref):
        r_ref[...] = ((p_ref[...] + b_ref[0].astype(F32)) + b_ref[1].astype(F32)) + b_ref[2].astype(F32)

    return _pc(body, name=name,
               grid_spec=pltpu.PrefetchScalarGridSpec(
                   num_scalar_prefetch=1, grid=(h // th,),
                   in_specs=[pl.BlockSpec((None, th, cols), lambda i, ir: (ir[0], i, 0)),
                             pl.BlockSpec((3, th, cols), lambda i, ir: (0, i, 0))],
                   out_specs=pl.BlockSpec((None, th, cols), lambda i, ir: (ir[1], i, 0))),
               out_shape=jax.ShapeDtypeStruct((2, h, cols), F32),
               compiler_params=pltpu.CompilerParams(dimension_semantics=("arbitrary",)))(idx, p, b)


def _rs_chip_sums(grads, send_dtypes, tag):
    recv = _rs_sibling(grads, name=f"rs_sibling_{tag}")
    parts, sends = [], []
    for t, (g, a) in enumerate(zip(grads, recv)):
        p, s = _add_halves(g, a, send_dtypes[t], name=f"rs_add_halves_{tag}{t}")
        parts.append(p)
        sends.append(s)
    return parts, sends


def _rs_finish(parts, others, tag):
    halves = [_add_chips(p, b, name=f"rs_add_chips_{tag}{t}") for t, (p, b) in enumerate(zip(parts, others))]
    full = _rs_final(halves, name=f"rs_final_{tag}")
    return [f.reshape(-1, f.shape[-1]) for f in full]


def _s5_discretize(lam_re, lam_im, log_dt, b_re, b_im):
    lam = lax.complex(lam_re, lam_im)
    dt = jnp.exp(log_dt)[:, None]
    lam_bar = jnp.exp(lam * dt)
    b_bar = ((lam_bar - 1.0) / lam)[..., None] * lax.complex(b_re, b_im)
    return jnp.real(lam_bar), jnp.imag(lam_bar), jnp.real(b_bar), jnp.imag(b_bar)


def _lanes_from_gp(re, im, cfg):
    v = jnp.stack([re, im]).reshape(2, cfg.NB, GROUPS_PER_BLOCK, SSM_STATE)
    return jnp.transpose(v, (1, 0, 2, 3)).reshape(1, cfg.NL)


def _gp_from_lanes(v, cfg):
    v = jnp.transpose(v.reshape(cfg.NB, 2, GROUPS_PER_BLOCK, SSM_STATE), (1, 0, 2, 3)).reshape(2, cfg.G, SSM_STATE)
    return v[0], v[1]


def _bb_band(bb_re, bb_im, cfg):
    eye = jnp.eye(GROUPS_PER_BLOCK, dtype=F32)
    bb = jnp.stack([bb_re, bb_im]).reshape(2, cfg.NB, GROUPS_PER_BLOCK, SSM_STATE, SSM_GROUP)
    return jnp.einsum('rjgpc,gh->jgcrhp', bb, eye).reshape(cfg.DS, 2 * GROUPS_PER_BLOCK * SSM_STATE)


def _bb_from_band(m, cfg):
    eye = jnp.eye(GROUPS_PER_BLOCK, dtype=F32)
    m = m.reshape(cfg.NB, GROUPS_PER_BLOCK, SSM_GROUP, 2, GROUPS_PER_BLOCK, SSM_STATE)
    v = jnp.einsum('jgcrhp,gh->rjgpc', m, eye).reshape(2, cfg.G, SSM_STATE, SSM_GROUP)
    return v[0], v[1]


def _cc_band(c_re, c_im, cfg):
    eye = jnp.eye(GROUPS_PER_BLOCK, dtype=F32)
    cc = jnp.stack([c_re, -c_im]).reshape(2, cfg.NB, GROUPS_PER_BLOCK, SSM_GROUP, SSM_STATE)
    return jnp.einsum('rjgcp,gh->jrhpgc', cc, eye).reshape(cfg.NL, GROUPS_PER_BLOCK * SSM_GROUP)


def _cc_from_band(m, cfg):
    eye = jnp.eye(GROUPS_PER_BLOCK, dtype=F32)
    m = m.reshape(cfg.NB, 2, GROUPS_PER_BLOCK, SSM_STATE, GROUPS_PER_BLOCK, SSM_GROUP)
    v = jnp.einsum('jrhpgc,gh->rjgcp', m, eye).reshape(2, cfg.G, SSM_GROUP, SSM_STATE)
    return v[0], -v[1]


PACK_COLS = 512
PACK_ROW_ALIGN = 64


def _pack(arrs):
    flat = jnp.concatenate([a.reshape(-1).astype(F32) for a in arrs])
    unit = PACK_COLS * PACK_ROW_ALIGN
    total = -(-flat.shape[0] // unit) * unit
    return jnp.pad(flat, (0, total - flat.shape[0])).reshape(-1, PACK_COLS)


def _unpack(p, shapes):
    flat = p.reshape(-1)
    out, off = [], 0
    for shp in shapes:
        size = math.prod(shp)
        out.append(flat[off:off + size].reshape(shp))
        off += size
    return out


def _adamw(w, g, m, v, *, name):
    c1 = 1.0 / (1.0 - ADAM_B1 ** ADAM_STEP)
    c2 = 1.0 / (1.0 - ADAM_B2 ** ADAM_STEP)

    def fn(rid, wv, gv, mv, vv):
        mn = ADAM_B1 * mv + (1.0 - ADAM_B1) * gv
        vn = ADAM_B2 * vv + (1.0 - ADAM_B2) * (gv * gv)
        delta = -ADAM_LR * ((mn * c1) / (jnp.sqrt(vn * c2) + ADAM_EPS) + ADAM_WD * wv)
        return delta, mn, vn

    cols = w.shape[1]
    return _ew(fn, [w, g, m, v], [], [(cols, F32)] * 3, name=name)


def _to_comm_layout(name, w, cfg):
    w = w[0]
    if name == 'w_in':
        return jnp.pad(w, ((0, 0), (0, cfg.DINP - cfg.DIN)))
    if name == 'w_q_b':
        hs = w.shape[1] // (QK_NOPE + QK_ROPE)
        wt = w.T.reshape(hs, QK_NOPE + QK_ROPE, cfg.QL)
        return jnp.pad(wt, ((0, 0), (0, HEAD_SLOT - QK_NOPE - QK_ROPE), (0, 0))).reshape(hs * HEAD_SLOT, cfg.QL)
    if name == 'w_kv_b':
        return w.T
    if name == 'w_up':
        wt = w.T.reshape(2, cfg.F // 4, cfg.D)
        return jnp.pad(wt, ((0, 0), (0, cfg.FQ - cfg.F // 4), (0, 0))).reshape(2 * cfg.FQ, cfg.D)
    if name == 'w_down':
        return jnp.pad(w, ((0, cfg.FQ - cfg.F // 4), (0, 0)))
    return w


def _from_comm_layout(name, g, cfg):
    if name == 'w_in':
        g = g[:, :cfg.DIN]
    elif name == 'w_q_b':
        hs = g.shape[0] // HEAD_SLOT
        g = g.reshape(hs, HEAD_SLOT, cfg.QL)[:, :QK_NOPE + QK_ROPE].reshape(hs * (QK_NOPE + QK_ROPE), cfg.QL).T
    elif name == 'w_kv_b':
        g = g.T
    elif name == 'w_up':
        g = g.reshape(2, cfg.FQ, cfg.D)[:, :cfg.F // 4].reshape(cfg.F // 2, cfg.D).T
    elif name == 'w_down':
        g = g[:cfg.F // 4]
    return g[None]


def _ff_pad(v, cfg):
    k = v.shape[0]
    return jnp.pad(v.reshape(k, 4, cfg.F // 4), ((0, 0), (0, 0), (0, cfg.FQ - cfg.F // 4))).reshape(k, cfg.FP)


def _ff_unpad(v, cfg):
    k = v.shape[0]
    return v.reshape(k, 4, cfg.FQ)[:, :, :cfg.F // 4].reshape(k, cfg.F)


def _step(cfg, w, m, v, x, loss_target):
    lp, d, ds, nl = cfg.LP, cfg.D, cfg.DS, cfg.NL
    blk = 2 * GROUPS_PER_BLOCK * SSM_STATE
    gw = GROUPS_PER_BLOCK * SSM_GROUP
    xi, yi = lax.axis_index("x"), lax.axis_index("y")
    me = 2 * xi + yi

    placed = [_place_shard(_to_comm_layout(n, w[n], cfg), BF16, name=f"place_{n}") for n in BIG]
    conv_w_shard = jnp.pad(w['conv_w'][0], ((0, ROW_ALIGN - 3), (0, cfg.FQ - cfg.F // 4)))
    placed += [_place_shard(w['meta_tokens'], F32, name="place_meta"), _place_shard(conv_w_shard, F32, name="place_conv_w")]
    full = _allgather(placed[:5] + placed[7:], name="allgather_weights")
    w_in, w_glu, w_qt, w_kvt, w_out = full[:5]
    meta = jnp.transpose(full[5].reshape(4, N_META, d // 4), (1, 0, 2)).reshape(N_META, d)
    conv_w = jnp.transpose(full[6].reshape(4, ROW_ALIGN, cfg.FQ)[:, :3], (1, 0, 2)).reshape(3, cfg.FP)
    conv_b = _ff_pad(w['conv_b'], cfg)
    ffn_send, ffn_recv, ffn_flying, ffn_token = _allgather_ici_start(placed[5:7], full[5], name="allgather_ffn_start")
    mix_norm = w['mix_norm'] + ffn_token[0:1, 0:1]

    pos = (jnp.arange(lp, dtype=jnp.int32) - PAD).astype(F32)
    inv_freq = 1.0 / (ROPE_BASE ** (jnp.arange(0, QK_ROPE, 2, dtype=F32) / QK_ROPE))
    ang = pos[:, None] * inv_freq[None, :]
    zpad = jnp.zeros((lp, LANE - QK_ROPE), F32)
    cos_t = jnp.concatenate([jnp.cos(ang), jnp.cos(ang), zpad], axis=1)
    sin_t = jnp.concatenate([jnp.sin(ang), jnp.sin(ang), zpad], axis=1)

    s5_in = (w['lam_re'][0], w['lam_im'][0], w['log_dt'][0], w['b_re'][0], w['b_im'][0])
    (a_re, a_im, bb_re, bb_im), s5_vjp = jax.vjp(_s5_discretize, *s5_in)
    a_l = _lanes_from_gp(a_re, a_im, cfg)
    bb_band = _bb_band(bb_re, bb_im, cfg).astype(BF16)
    cc_band = _cc_band(w['c_re'][0], w['c_im'][0], cfg).astype(BF16)
    d_skip, b_glu = w['d_skip'], w['b_glu']

    h0 = jnp.concatenate([jnp.zeros((PAD, d), F32), meta, x[0]], axis=0)
    xn = _rms_fwd(h0, mix_norm, name="rms_mix")
    z = _mm(xn, w_in, name="mm_in", tn=_tile(cfg.DINP, 640))
    u = (z, ds, 0)
    q_a = (z, cfg.QL, ds // cfg.QL)
    kv_a = (z, cfg.KVL, (ds + cfg.QL) // cfg.KVL)
    k_pe = (z, LANE, (ds + cfg.QL + cfg.KVL) // LANE)

    bu = _mm(z, bb_band, name="mm_s5_bu", dims=(lp, nl, gw), tn=blk, tk=gw,
             a_idx=lambda i, j, k: (i, j), b_idx=lambda i, j, k: (j, 0))
    hs = _s5_scan(bu, a_l, reverse=False, name="s5_scan_fwd")
    yc = _mm(hs, cc_band, name="mm_s5_y", dims=(lp, ds, blk), tn=gw, tk=blk,
             a_idx=lambda i, j, k: (i, j), b_idx=lambda i, j, k: (j, 0))

    def s5_y(ycv, uv, dk):
        return ycv + dk * uv

    gl = _ew(lambda rid, ycv, uv, dk: jax.nn.gelu(s5_y(ycv, uv, dk)), [yc, u], [d_skip], [(ds, BF16)], name="s5_gelu")[0]
    tg = _mm(gl, w_glu, name="mm_glu")
    ya = _ew(lambda rid, ycv, uv, tv, dk, bg: jax.nn.gelu(s5_y(ycv, uv, dk)) * jax.nn.sigmoid(tv + bg),
             [yc, u, tg], [d_skip, b_glu], [(ds, F32)], name="s5_glu")[0]

    qn = _rms_fwd(q_a, w['q_a_norm'], name="rms_q")
    kvn = _rms_fwd(kv_a, w['kv_a_norm'], name="rms_kv")
    q_raw = _mm(qn, w_qt, tb=True, name="mm_q")
    qx = _ew(_rope_heads(_rope, cfg.H), [q_raw, cos_t, sin_t], [], [(cfg.H * HEAD_SLOT, BF16)], name="rope_q")[0]
    kv = _mm(kvn, w_kvt, tb=True, out_dtype=BF16, name="mm_kv")
    kr = _ew(lambda rid, kp, cs, sn: _rope(kp, cs, sn), [k_pe, cos_t, sin_t], [], [(LANE, BF16)], name="rope_k")[0]
    o, lse = _attn_fwd(qx, kv, kr, cfg, name="attn_fwd")

    def norm2(rid, yav, ov, gs, ga):
        return jnp.concatenate([_rms_parts(yav, gs)[0] * gs, _rms_parts(ov, ga)[0] * ga], axis=1)

    yn = _ew(norm2, [ya, o], [w['out_norm_ssm'], w['out_norm_attn']], [(cfg.DMIX, BF16)], name="rms_out")[0]
    h1 = _mm(yn, w_out, res=h0, name="mm_out")
    xn2 = _rms_fwd(h1, w['ffn_norm'], name="rms_ffn")
    ffn_landed = _allgather_ici_wait(ffn_send, ffn_recv, ffn_flying, xn2, name="allgather_ffn_wait")
    w_upt, w_down = _allgather_forward(ffn_landed, name="allgather_ffn_forward")
    up = _mm(xn2, w_upt, tb=True, name="mm_up")
    act = _conv_fwd(up, conv_w, conv_b, name="conv_fwd")
    h2 = _mm(act, w_down, res=h1, tm=_tile(lp, 544, ROW_ALIGN), name="mm_down")

    tgt = jnp.concatenate([jnp.zeros((PAD + N_META, d), F32), loss_target[0]], axis=0)
    g_final = w['final_norm'].reshape(1, d)

    def head(rid, hv, tv, gv):
        xhat, r = _rms_parts(hv, gv)
        valid = rid >= PAD + N_META
        diff = jnp.where(valid, xhat * gv - tv, 0.0)
        dout = diff * (1.0 / d)
        dxhat = dout * gv
        dx = r * (dxhat - xhat * jnp.mean(dxhat * xhat, axis=-1, keepdims=True))
        return dx, dx, dout * xhat, 0.5 * diff * dout

    dh2, dh2_b, dg_final, loss_cols = _ew(head, [h2, tgt], [g_final], [(d, F32), (d, BF16)], [d, d], name="loss_head")
    loss = lax.psum(jnp.sum(loss_cols), ("x", "y", "c"))

    dact = _mm(dh2_b, w_down, tb=True, out_dtype=BF16, name="mm_dact")
    dw_down = _mm(act, dh2_b, ta=True, tn=d, tm=512, name="mm_dw_down")
    dup, dconv_w, dconv_b = _conv_bwd(up, dact, conv_w, conv_b, name="conv_bwd")
    dxn2 = _mm(dup, w_upt, tk=1024, tn=1024, name="mm_dxn2")
    dw_upt = _mm(dup, xn2, ta=True, tn=d, tm=512, name="mm_dw_up")
    ffn_parts, ffn_sends = _rs_chip_sums([dw_upt, dw_down], [BF16, BF16], "ffn")
    rs_send, rs_recv, rs_flying, rs_lands, rs_token = _rs_chips_start(ffn_sends, name="rs_chips_ffn_start")
    ffn_norm = w['ffn_norm'] + rs_token[0:1, 0:1]
    dh1, dh1_b, dg_ffn = _rms_bwd(h1, ffn_norm, dxn2, res=dh2, mask=True, with_bf16=True, name="rms_ffn_bwd")

    dyn = _mm(dh1_b, w_out, tb=True, name="mm_dyn")
    dw_out = _mm(yn, dh1_b, ta=True, tn=d, tm=512, name="mm_dw_out")
    dya, dg_ssm = _rms_bwd(ya, w['out_norm_ssm'], (dyn, ds, 0), name="rms_ssm_bwd")
    do, dg_attn = _rms_bwd(o, w['out_norm_attn'], (dyn, cfg.DATTN, ds // cfg.DATTN), name="rms_attn_bwd")

    dqx, dkn, dv, dkr = _attn_bwd(qx, kv, kr, o, lse, do, cfg, name="attn_bwd")
    dq_raw = _ew(_rope_heads(_unrope, cfg.H), [dqx, cos_t, sin_t], [], [(cfg.H * HEAD_SLOT, BF16)], name="unrope_q")[0]
    dk_pe = _ew(lambda rid, dk, cs, sn: _unrope(dk, cs, sn), [dkr, cos_t, sin_t], [], [(LANE, F32)], name="unrope_k")[0]
    dqn = _mm(dq_raw, w_qt, name="mm_dqn")
    dw_qt = _mm(dq_raw, qn, ta=True, tm=512, name="mm_dw_q")
    dkv = jnp.stack([dkn.reshape(lp, cfg.H, QK_NOPE), dv.reshape(lp, cfg.H, V_HEAD)], axis=2).reshape(lp, -1).astype(BF16)
    dkvn = _mm(dkv, w_kvt, name="mm_dkvn")
    dw_kvt = _mm(dkv, kvn, ta=True, tm=512, name="mm_dw_kv")
    dq_a, dg_q = _rms_bwd(q_a, w['q_a_norm'], dqn, name="rms_q_bwd")
    dkv_a, dg_kv = _rms_bwd(kv_a, w['kv_a_norm'], dkvn, name="rms_kv_bwd")

    def glu_bwd(rid, ycv, uv, tv, dyav, dk, bg):
        gelu = jax.nn.gelu(s5_y(ycv, uv, dk))
        sg = jax.nn.sigmoid(tv + bg)
        dt = dyav * gelu * sg * (1.0 - sg)
        return dt, dyav * sg, dt

    dt_b, dgl1, db_glu = _ew(glu_bwd, [yc, u, tg, dya], [d_skip, b_glu], [(ds, BF16), (ds, F32)], [ds], name="s5_glu_bwd")
    dgl = _mm(dt_b, w_glu, tb=True, res=dgl1, name="mm_dgl")
    dw_glu = _mm(gl, dt_b, ta=True, tm=512, name="mm_dw_glu")

    def gelu_bwd(rid, ycv, uv, dglv, dk):
        _, vjp = jax.vjp(jax.nn.gelu, s5_y(ycv, uv, dk))
        dy = vjp(dglv)[0]
        return dy, dy * dk, dy * uv

    dy_b, du_skip, dd_skip = _ew(gelu_bwd, [yc, u, dgl], [d_skip], [(ds, BF16), (ds, F32)], [ds], name="s5_gelu_bwd")
    dhs = _mm(dy_b, cc_band, tb=True, name="mm_s5_dhs", dims=(lp, nl, gw), tn=blk, tk=gw,
              a_idx=lambda i, j, k: (i, j), b_idx=lambda i, j, k: (j, 0))
    dcc_band = _mm(hs, dy_b, ta=True, name="mm_s5_dcc", dims=(nl, gw, lp), tm=blk, tn=gw,
                   a_idx=lambda i, j, k: (0, i), b_idx=lambda i, j, k: (0, i))
    gs = _s5_scan(dhs, a_l, reverse=True, name="s5_scan_bwd")
    da_l = _s5_da(gs, hs, cfg, name="s5_da")
    dbb_band = _mm(z, gs, ta=True, name="mm_s5_dbb", dims=(ds, blk, lp), tm=gw, tn=blk,
                   a_idx=lambda i, j, k: (0, i), b_idx=lambda i, j, k: (0, i))
    du = _mm(gs, bb_band, tb=True, res=du_skip, name="mm_s5_du", dims=(lp, ds, blk), tn=gw, tk=blk,
             a_idx=lambda i, j, k: (i, j), b_idx=lambda i, j, k: (j, 0))

    dz = jnp.concatenate([du, dq_a, dkv_a, dk_pe], axis=1).astype(BF16)
    dxn = _mm(dz, w_in, tb=True, name="mm_dxn")
    dw_in = _mm(xn, dz, ta=True, tm=512, tn=_tile(cfg.DINP, 1024), name="mm_dw_in")
    dh0, dg_mix = _rms_bwd(h0, w['mix_norm'], dxn, res=dh1, name="rms_mix_bwd")
    grad_x = dh0[PAD + N_META:][None]

    da_re, da_im = _gp_from_lanes(da_l, cfg)
    dbb_re, dbb_im = _bb_from_band(dbb_band, cfg)
    dlam_re, dlam_im, dlog_dt, db_re, db_im = s5_vjp((da_re, da_im, dbb_re, dbb_im))
    dc_re, dc_im = _cc_from_band(dcc_band, cfg)
    local_small = {
        'meta_tokens': dh0[PAD:PAD + N_META], 'mix_norm': dg_mix, 'lam_re': dlam_re, 'lam_im': dlam_im, 'log_dt': dlog_dt,
        'b_re': db_re, 'b_im': db_im, 'c_re': dc_re, 'c_im': dc_im, 'd_skip': dd_skip, 'b_glu': db_glu, 'q_a_norm': dg_q,
        'kv_a_norm': dg_kv, 'out_norm_ssm': dg_ssm, 'out_norm_attn': dg_attn, 'ffn_norm': dg_ffn,
        'conv_w': _ff_unpad(dconv_w, cfg), 'conv_b': _ff_unpad(dconv_b, cfg), 'final_norm': dg_final,
    }
    small_shapes = [local_small[n].shape for n in SMALL]

    rest_local = [dw_in, dw_glu, dw_qt, dw_kvt, dw_out, _pack([local_small[n] for n in SMALL])]
    rest_parts, rest_sends = _rs_chip_sums(rest_local, [BF16] * 5 + [F32], "rest")
    ffn_others = _rs_chips_wait(rs_send, rs_recv, rs_flying, rs_lands, rest_sends[0], name="rs_chips_ffn_wait")
    rest_others = _rs_chips(rest_sends, name="rs_chips_rest")
    reduced = _rs_finish(rest_parts + ffn_parts, list(rest_others) + list(ffn_others), "grads")
    reduced = reduced[:5] + reduced[6:8] + [reduced[5]]
    small_full = _allgather([_place_shard(reduced[7], F32, name="place_small")], name="allgather_small")[0]
    small_sum = dict(zip(SMALL, _unpack(small_full, small_shapes)))

    grads = {n: _from_comm_layout(n, g, cfg) for n, g in zip(BIG, reduced[:7])}
    for n in SMALL:
        g = small_sum[n]
        if n == 'meta_tokens':
            g = lax.dynamic_slice_in_dim(g, me * (d // 4), d // 4, axis=1)
        elif n == 'conv_w':
            g = lax.dynamic_slice_in_dim(g, me * (cfg.F // 4), cfg.F // 4, axis=1)[None]
        else:
            g = g.reshape(w[n].shape)
        grads[n] = g

    delta, new_m, new_v = {}, {}, {}
    for n in BIG:
        shp = w[n].shape
        r = [t.reshape(shp[-2], shp[-1]) for t in (w[n], grads[n], m[n], v[n])]
        dl, mn, vn = _adamw(*r, name=f"adamw_{n}")
        delta[n], new_m[n], new_v[n] = dl.reshape(shp), mn.reshape(shp), vn.reshape(shp)
    shapes = [w[n].shape for n in SMALL]
    packs = [_pack([src[n] for n in SMALL]) for src in (w, grads, m, v)]
    for dst, p in zip((delta, new_m, new_v), _adamw(*packs, name="adamw_small")):
        dst.update(zip(SMALL, _unpack(p, shapes)))

    return (loss, grad_x, *[grads[n] for n in WEIGHTS], *[delta[n] for n in WEIGHTS],
            *[new_m[n] for n in WEIGHTS], *[new_v[n] for n in WEIGHTS])


def kernel(x, meta_tokens, mix_norm, w_in, lam_re, lam_im, log_dt, b_re, b_im, c_re, c_im, d_skip, w_glu, b_glu, q_a_norm, w_q_b, kv_a_norm, w_kv_b, out_norm_ssm, out_norm_attn, w_out, ffn_norm, w_up, conv_w, conv_b, w_down, final_norm, loss_target, m_meta_tokens, m_mix_norm, m_w_in, m_lam_re, m_lam_im, m_log_dt, m_b_re, m_b_im, m_c_re, m_c_im, m_d_skip, m_w_glu, m_b_glu, m_q_a_norm, m_w_q_b, m_kv_a_norm, m_w_kv_b, m_out_norm_ssm, m_out_norm_attn, m_w_out, m_ffn_norm, m_w_up, m_conv_w, m_conv_b, m_w_down, m_final_norm, v_meta_tokens, v_mix_norm, v_w_in, v_lam_re, v_lam_im, v_log_dt, v_b_re, v_b_im, v_c_re, v_c_im, v_d_skip, v_w_glu, v_b_glu, v_q_a_norm, v_w_q_b, v_kv_a_norm, v_w_kv_b, v_out_norm_ssm, v_out_norm_attn, v_w_out, v_ffn_norm, v_w_up, v_conv_w, v_conv_b, v_w_down, v_final_norm):
    args = dict(locals())
    w = {n: args[n] for n in WEIGHTS}
    m = {n: args["m_" + n] for n in WEIGHTS}
    v = {n: args["v_" + n] for n in WEIGHTS}
    return _step(PROD, w, m, v, x, loss_target)
```

```python
import functools
import math
from typing import NamedTuple

import jax
import jax.numpy as jnp
from jax import lax
from jax.experimental import pallas as pl
from jax.experimental.pallas import tpu as pltpu

F32, BF16 = jnp.float32, jnp.bfloat16
MESH = pl.DeviceIdType.MESH
LANE = 128
ROW_ALIGN = 16
N_META = 16
PAD = 112
CHUNK = 64
SSM_GROUP = 16
SSM_STATE = 64
GROUPS_PER_BLOCK = 8
QK_NOPE, QK_ROPE, V_HEAD = 128, 64, 128
HEAD_SLOT = 256
ROPE_BASE = 10000.0
EPS = 1e-6
ADAM_LR, ADAM_B1, ADAM_B2, ADAM_EPS, ADAM_WD, ADAM_STEP = 0.001, 0.9, 0.999, 1e-08, 0.01, 10
DT_F32_BLOCK_BYTES = 1 << 20
SKIP, FIRST = "skip", "first"


class Cfg(NamedTuple):
    D: int
    S: int
    DS: int
    H: int
    QL: int
    KVL: int
    F: int

    @property
    def LP(self):
        return PAD + N_META + self.S

    @property
    def G(self):
        return self.DS // SSM_GROUP

    @property
    def NB(self):
        return self.G // GROUPS_PER_BLOCK

    @property
    def NL(self):
        return 2 * self.G * SSM_STATE

    @property
    def DATTN(self):
        return self.H * V_HEAD

    @property
    def DMIX(self):
        return self.DS + self.DATTN

    @property
    def DIN(self):
        return self.DS + self.QL + self.KVL + QK_ROPE

    @property
    def DINP(self):
        return self.DS + self.QL + self.KVL + LANE

    @property
    def FQ(self):
        return -(-(self.F // 4) // LANE) * LANE

    @property
    def FP(self):
        return 4 * self.FQ


PROD = Cfg(D=2048, S=2048, DS=1024, H=8, QL=512, KVL=256, F=5504)

WEIGHTS = ['meta_tokens', 'mix_norm', 'w_in', 'lam_re', 'lam_im', 'log_dt', 'b_re', 'b_im', 'c_re', 'c_im', 'd_skip',
           'w_glu', 'b_glu', 'q_a_norm', 'w_q_b', 'kv_a_norm', 'w_kv_b', 'out_norm_ssm', 'out_norm_attn', 'w_out',
           'ffn_norm', 'w_up', 'conv_w', 'conv_b', 'w_down', 'final_norm']
BIG = ['w_in', 'w_glu', 'w_q_b', 'w_kv_b', 'w_out', 'w_up', 'w_down']
SMALL = [n for n in WEIGHTS if n not in BIG]


def _pc(body, **kw):
    return pl.pallas_call(body, **kw)


def _tile(n, target, align=LANE):
    best = None
    d = align
    while d <= min(n, target):
        if n % d == 0:
            best = d
        d += align
    return best if best is not None else n


def _row_tile(rows, cols):
    return _tile(rows, max(ROW_ALIGN, DT_F32_BLOCK_BYTES // (4 * cols)), ROW_ALIGN)


def _mm(a, b, *, name, ta=False, tb=False, tm=None, tn=512, tk=None, out_dtype=F32, res=None,
        a_idx=None, b_idx=None, dims=None, a_lead=False):
    if dims is None:
        m, k = (a.shape[1], a.shape[0]) if ta else a.shape
        n = b.shape[0] if tb else b.shape[1]
    else:
        m, n, k = dims
    tm = _tile(m, tm or m, LANE if ta else ROW_ALIGN)
    tn = _tile(n, tn)
    tk = _tile(k, tk or k, ROW_ALIGN if (ta and not tb) else LANE)
    nm, nn, nk = m // tm, n // tn, k // tk
    a_idx = a_idx or ((lambda i, j, kk: (kk, i)) if ta else (lambda i, j, kk: (i, kk)))
    b_idx = b_idx or ((lambda i, j, kk: (j, kk)) if tb else (lambda i, j, kk: (kk, j)))
    dn = (((0 if ta else 1,), (1 if tb else 0,)), ((), ()))

    def body(*refs):
        a_ref, b_ref = refs[0], refs[1]
        r_ref = refs[2] if res is not None else None
        o_ref = refs[3] if res is not None else refs[2]
        d = lax.dot_general(a_ref[...].astype(BF16), b_ref[...].astype(BF16), dn, preferred_element_type=F32)

        def finish(r):
            if r_ref is not None:
                r = r + r_ref[...].astype(F32)
            o_ref[...] = r.astype(out_dtype)

        if nk == 1:
            finish(d)
        else:
            acc = refs[-1]
            kk = pl.program_id(2)

            @pl.when(kk == 0)
            def _():
                acc[...] = d

            @pl.when(kk > 0)
            def _():
                acc[...] += d

            @pl.when(kk == nk - 1)
            def _():
                finish(acc[...])

    a_blk = ((None,) if a_lead else ()) + ((tk, tm) if ta else (tm, tk))
    in_specs = [pl.BlockSpec(a_blk, a_idx), pl.BlockSpec((tn, tk) if tb else (tk, tn), b_idx)]
    args = [a, b]
    if res is not None:
        in_specs.append(pl.BlockSpec((tm, tn), lambda i, j, kk: (i, j)))
        args.append(res)
    return _pc(body, name=name, grid=(nm, nn, nk), in_specs=in_specs,
               out_specs=pl.BlockSpec((tm, tn), lambda i, j, kk: (i, j)),
               out_shape=jax.ShapeDtypeStruct((m, n), out_dtype),
               scratch_shapes=[pltpu.VMEM((tm, tn), F32)] if nk > 1 else [],
               compiler_params=pltpu.CompilerParams(dimension_semantics=("parallel", "parallel", "arbitrary")))(*args)


def _ew(fn, ins, vecs, outs, sums=(), *, name, tm=None):
    ins = [x if isinstance(x, tuple) else (x, x.shape[1], 0) for x in ins]
    ins = [x if len(x) == 4 else x + (None,) for x in ins]
    outs = [o if len(o) == 3 else o + (None,) for o in outs]
    rows = ins[0][0].shape[0]
    cmax = max([c for _, c, _, _ in ins] + [c for c, _, _ in outs])
    tm = tm or _row_tile(rows, cmax)
    n_in, n_vec, n_out, n_sum = len(ins), len(vecs), len(outs), len(sums)

    def body(*refs):
        i = pl.program_id(0)
        rid = i * tm + lax.broadcasted_iota(jnp.int32, (tm, 1), 0)
        vals = [r[...] for r in refs[:n_in + n_vec]]
        res = fn(rid, *vals)
        res = res if isinstance(res, (tuple, list)) else (res,)
        o_refs = refs[n_in + n_vec:]
        for o_ref, r, (_, _, mode) in zip(o_refs[:n_out], res[:n_out], outs):
            if mode == FIRST:
                @pl.when(i == 0)
                def _():
                    o_ref[...] = r.astype(o_ref.dtype)
            else:
                o_ref[...] = r.astype(o_ref.dtype)
        for o_ref, r in zip(o_refs[n_out:], res[n_out:]):
            part = jnp.sum(r.astype(F32), axis=0, keepdims=True)

            @pl.when(i == 0)
            def _():
                o_ref[...] = part

            @pl.when(i > 0)
            def _():
                o_ref[...] += part

    def row_idx(mode):
        if mode == SKIP:
            return lambda i, cb=0: (jnp.maximum(i - 1, 0), cb)
        if mode == FIRST:
            return lambda i, cb=0: (0, cb)
        return lambda i, cb=0: (i, cb)

    in_specs = [pl.BlockSpec((tm, c), functools.partial(row_idx(mode), cb=cb)) for _, c, cb, mode in ins]
    in_specs += [pl.BlockSpec(v.shape, functools.partial(lambda i, nd: (0,) * nd, nd=v.ndim)) for v in vecs]
    out_specs = [pl.BlockSpec((tm, c), row_idx(mode)) for c, _, mode in outs]
    out_specs += [pl.BlockSpec((1, c), lambda i: (0, 0)) for c in sums]
    out_rows = {None: rows, SKIP: rows - tm, FIRST: tm}
    out_shape = [jax.ShapeDtypeStruct((out_rows[mode], c), dt) for c, dt, mode in outs]
    out_shape += [jax.ShapeDtypeStruct((1, c), F32) for c in sums]
    return _pc(body, name=name, grid=(rows // tm,), in_specs=in_specs, out_specs=out_specs, out_shape=out_shape,
               compiler_params=pltpu.CompilerParams(dimension_semantics=("arbitrary",)))(*[x[0] for x in ins], *vecs)


def _rms_parts(x, g):
    r = lax.rsqrt(jnp.mean(x * x, axis=-1, keepdims=True) + EPS)
    return x * r, r


def _rms_bwd_block(x, g, dy):
    xhat, r = _rms_parts(x, g)
    dxhat = dy * g
    dx = r * (dxhat - xhat * jnp.mean(dxhat * xhat, axis=-1, keepdims=True))
    return dx, dy * xhat


def _rms_fwd(x, g, *, name):
    c = x[1] if isinstance(x, tuple) else x.shape[1]
    return _ew(lambda rid, xv, gv: _rms_parts(xv.astype(F32), gv)[0] * gv, [x], [g], [(c, BF16)], name=name)[0]


def _rms_bwd(x, g, dy, *, name, res=None, mask=False, with_bf16=False):
    c = x[1] if isinstance(x, tuple) else x.shape[1]

    def fn(rid, xv, dyv, *rest):
        gv = rest[-1]
        dx, dg = _rms_bwd_block(xv.astype(F32), gv, dyv.astype(F32))
        if res is not None:
            dx = dx + rest[0]
        if mask:
            dx = jnp.where(rid >= PAD, dx, 0.0)
        return (dx, dx, dg) if with_bf16 else (dx, dg)

    ins = [x, dy] + ([res] if res is not None else [])
    outs = [(c, F32)] + ([(c, BF16)] if with_bf16 else [])
    return _ew(fn, ins, [g], outs, [c], name=name)


def _s5_scan(bu, a_l, *, reverse, name):
    lp, nl = bu.shape
    w = GROUPS_PER_BLOCK * SSM_STATE
    unroll = 8

    def body(bu_ref, a_ref, hs_ref):
        ar = a_ref[:, :w]
        ai = -a_ref[:, w:] if reverse else a_ref[:, w:]

        def step(n, carry):
            hr, hi = carry
            for q in range(unroll):
                t = n * unroll + q
                t = lp - 1 - t if reverse else t
                nr = ar * hr - ai * hi + bu_ref[pl.ds(t, 1), :w]
                ni = ar * hi + ai * hr + bu_ref[pl.ds(t, 1), w:]
                hs_ref[pl.ds(t, 1), :w] = nr
                hs_ref[pl.ds(t, 1), w:] = ni
                hr, hi = nr, ni
            return hr, hi

        z = jnp.zeros((1, w), F32)
        lax.fori_loop(0, lp // unroll, step, (z, z))

    return _pc(body, name=name, grid=(nl // (2 * w),),
               in_specs=[pl.BlockSpec((lp, 2 * w), lambda j: (0, j)), pl.BlockSpec((1, 2 * w), lambda j: (0, j))],
               out_specs=pl.BlockSpec((lp, 2 * w), lambda j: (0, j)),
               out_shape=jax.ShapeDtypeStruct((lp, nl), F32),
               compiler_params=pltpu.CompilerParams(dimension_semantics=("parallel",)))(bu, a_l)


def _s5_da(gs, hs, cfg, *, name):
    lp, nl = gs.shape
    w = GROUPS_PER_BLOCK * SSM_STATE
    tc = w // 2
    per = 2 * w // tc

    def body(gr_ref, gi_ref, hr_ref, hi_ref, dre_ref, dim_ref):
        keep = lax.broadcasted_iota(jnp.int32, (lp, 1), 0) >= 1
        hr = jnp.where(keep, pltpu.roll(hr_ref[...], 1, 0), 0.0)
        hi = jnp.where(keep, pltpu.roll(hi_ref[...], 1, 0), 0.0)
        gr, gi = gr_ref[...], gi_ref[...]
        dre_ref[...] = jnp.sum(gr * hr + gi * hi, axis=0, keepdims=True)
        dim_ref[...] = jnp.sum(gi * hr - gr * hi, axis=0, keepdims=True)

    re_blk = pl.BlockSpec((lp, tc), lambda j, q: (0, per * j + q))
    im_blk = pl.BlockSpec((lp, tc), lambda j, q: (0, per * j + per // 2 + q))
    out_blk = pl.BlockSpec((1, tc), lambda j, q: (0, (per // 2) * j + q))
    dre, dim = _pc(body, name=name, grid=(cfg.NB, per // 2), in_specs=[re_blk, im_blk, re_blk, im_blk],
                   out_specs=[out_blk, out_blk], out_shape=[jax.ShapeDtypeStruct((1, nl // 2), F32)] * 2,
                   compiler_params=pltpu.CompilerParams(dimension_semantics=("parallel", "parallel")))(gs, gs, hs, hs)
    return jnp.stack([dre.reshape(cfg.NB, w), dim.reshape(cfg.NB, w)], axis=1).reshape(1, nl)


def _conv_gate(pre, cw, cb):
    return cw[0:1] * pltpu.roll(pre, 2, 0) + cw[1:2] * pltpu.roll(pre, 1, 0) + cw[2:3] * pre + cb


def _conv_fwd(up, cw, cb, *, name):
    lp, fp2 = up.shape
    fp = fp2 // 2
    tc = _tile(fp, 256)
    nb = fp // tc

    def body(pre_ref, val_ref, cw_ref, cb_ref, o_ref):
        gate = _conv_gate(pre_ref[...], cw_ref[...], cb_ref[...])
        o_ref[...] = (jax.nn.silu(gate) * val_ref[...]).astype(BF16)

    return _pc(body, name=name, grid=(nb,),
               in_specs=[pl.BlockSpec((lp, tc), lambda j: (0, j)), pl.BlockSpec((lp, tc), lambda j: (0, nb + j)),
                         pl.BlockSpec((3, tc), lambda j: (0, j)), pl.BlockSpec((1, tc), lambda j: (0, j))],
               out_specs=pl.BlockSpec((lp, tc), lambda j: (0, j)),
               out_shape=jax.ShapeDtypeStruct((lp, fp), BF16),
               compiler_params=pltpu.CompilerParams(dimension_semantics=("parallel",)))(up, up, cw, cb)


def _conv_bwd(up, dact, cw, cb, *, name):
    lp, fp2 = up.shape
    fp = fp2 // 2
    tc = _tile(fp, 256)
    nb = fp // tc

    def body(pre_ref, val_ref, da_ref, cw_ref, cb_ref, dup_ref, dcw_ref, dcb_ref):
        pre, val, da, cwv = pre_ref[...], val_ref[...], da_ref[...].astype(F32), cw_ref[...]
        gate = _conv_gate(pre, cwv, cb_ref[...])
        sg = jax.nn.sigmoid(gate)
        dup_ref[1] = (da * (gate * sg)).astype(BF16)
        dgate = da * val * (sg * (1.0 + gate * (1.0 - sg)))
        dpre = cwv[2:3] * dgate + cwv[1:2] * pltpu.roll(dgate, lp - 1, 0) + cwv[0:1] * pltpu.roll(dgate, lp - 2, 0)
        dup_ref[0] = dpre.astype(BF16)
        dcb_ref[...] = jnp.sum(dgate, axis=0, keepdims=True)
        dcw_ref[0:1, :] = jnp.sum(dgate * pltpu.roll(pre, 2, 0), axis=0, keepdims=True)
        dcw_ref[1:2, :] = jnp.sum(dgate * pltpu.roll(pre, 1, 0), axis=0, keepdims=True)
        dcw_ref[2:3, :] = jnp.sum(dgate * pre, axis=0, keepdims=True)

    return _pc(body, name=name, grid=(nb,),
               in_specs=[pl.BlockSpec((lp, tc), lambda j: (0, j)), pl.BlockSpec((lp, tc), lambda j: (0, nb + j)),
                         pl.BlockSpec((lp, tc), lambda j: (0, j)),
                         pl.BlockSpec((3, tc), lambda j: (0, j)), pl.BlockSpec((1, tc), lambda j: (0, j))],
               out_specs=[pl.BlockSpec((2, lp, tc), lambda j: (0, 0, j)),
                          pl.BlockSpec((3, tc), lambda j: (0, j)), pl.BlockSpec((1, tc), lambda j: (0, j))],
               out_shape=[jax.ShapeDtypeStruct((2, lp, fp), BF16), jax.ShapeDtypeStruct((3, fp), F32),
                          jax.ShapeDtypeStruct((1, fp), F32)],
               compiler_params=pltpu.CompilerParams(dimension_semantics=("parallel",)))(up, up, dact, cw, cb)


def _key_limit(i, tq, lp):
    return min(lp, -(-((i + 1) * tq) // LANE) * LANE)


def _attn_mask(i, tq, nk):
    qrow = i * tq + lax.broadcasted_iota(jnp.int32, (tq, 1), 0)
    krow = lax.broadcasted_iota(jnp.int32, (1, nk), 1)
    return (krow >= PAD) & ((krow // CHUNK) <= (qrow // CHUNK)), qrow >= PAD


def _attn_scores(q, kn, kr, i, tq, scale):
    nt = (((1,), (1,)), ((), ()))
    s = lax.dot_general(q[:, :QK_NOPE], kn, nt, preferred_element_type=F32)
    s = s + lax.dot_general(q[:, QK_NOPE:], kr, nt, preferred_element_type=F32)
    mask, qvalid = _attn_mask(i, tq, kn.shape[0])
    return jnp.where(mask, s * scale, jnp.finfo(F32).min), qvalid


def _per_q_block(nq, fn):
    i = pl.program_id(1)
    for blk in range(nq):
        pl.when(i == blk)(functools.partial(fn, blk))


def _attn_fwd(qx, kv, kr, cfg, *, name):
    lp, h = cfg.LP, cfg.H
    tq = _tile(lp, 272, ROW_ALIGN)
    nq = lp // tq
    scale = 1.0 / math.sqrt(QK_NOPE + QK_ROPE)

    def body(q_ref, kn_ref, v_ref, kr_ref, o_ref, lse_ref):
        def block(blk):
            nk = _key_limit(blk, tq, lp)
            s, qvalid = _attn_scores(q_ref[...], kn_ref[:nk], kr_ref[:nk], blk, tq, scale)
            m = jnp.max(s, axis=-1, keepdims=True)
            p = jnp.exp(s - m)
            l = jnp.sum(p, axis=-1, keepdims=True)
            o = jnp.dot(p.astype(BF16), v_ref[:nk], preferred_element_type=F32) / l
            o_ref[...] = jnp.where(qvalid, o, 0.0)
            lse_ref[...] = m + jnp.log(l)

        _per_q_block(nq, block)

    return _pc(body, name=name, grid=(h, nq),
               in_specs=[pl.BlockSpec((tq, HEAD_SLOT), lambda hh, i: (i, hh)),
                         pl.BlockSpec((lp, QK_NOPE), lambda hh, i: (0, 2 * hh)),
                         pl.BlockSpec((lp, V_HEAD), lambda hh, i: (0, 2 * hh + 1)),
                         pl.BlockSpec((lp, LANE), lambda hh, i: (0, 0))],
               out_specs=[pl.BlockSpec((tq, V_HEAD), lambda hh, i: (i, hh)),
                          pl.BlockSpec((None, tq, 1), lambda hh, i: (hh, i, 0))],
               out_shape=[jax.ShapeDtypeStruct((lp, h * V_HEAD), F32), jax.ShapeDtypeStruct((h, lp, 1), F32)],
               compiler_params=pltpu.CompilerParams(dimension_semantics=("parallel", "parallel")))(qx, kv, kv, kr)


def _attn_bwd(qx, kv, kr, o, lse, do, cfg, *, name):
    lp, h = cfg.LP, cfg.H
    tq = _tile(lp, 272, ROW_ALIGN)
    nq = lp // tq
    scale = 1.0 / math.sqrt(QK_NOPE + QK_ROPE)
    tn_dims = (((0,), (0,)), ((), ()))

    def body(q_ref, kn_ref, v_ref, kr_ref, o_ref, lse_ref, do_ref, dq_ref, dkv_ref, dkr_ref, dkv_acc):
        hh, i = pl.program_id(0), pl.program_id(1)

        @pl.when(i == 0)
        def _():
            dkv_acc[...] = jnp.zeros_like(dkv_acc)

        @pl.when((i == 0) & (hh == 0))
        def _():
            dkr_ref[...] = jnp.zeros_like(dkr_ref)

        def block(blk):
            nk = _key_limit(blk, tq, lp)
            q, kn, v, krv = q_ref[...], kn_ref[:nk], v_ref[:nk], kr_ref[:nk]
            s, qvalid = _attn_scores(q, kn, krv, blk, tq, scale)
            dov = jnp.where(qvalid, do_ref[...], 0.0)
            p = jnp.exp(s - lse_ref[...])
            delta = jnp.sum(dov * o_ref[...], axis=-1, keepdims=True)
            dob = dov.astype(BF16)
            dp = lax.dot_general(dob, v, (((1,), (1,)), ((), ())), preferred_element_type=F32)
            ds = (p * (dp - delta) * scale).astype(BF16)
            dq_ref[:, :QK_NOPE] = jnp.dot(ds, kn, preferred_element_type=F32)
            dq_ref[:, QK_NOPE:] = jnp.dot(ds, krv, preferred_element_type=F32)
            dkv_acc[:nk, :QK_NOPE] += lax.dot_general(ds, q[:, :QK_NOPE], tn_dims, preferred_element_type=F32)
            dkv_acc[:nk, QK_NOPE:] += lax.dot_general(p.astype(BF16), dob, tn_dims, preferred_element_type=F32)
            dkr_ref[:nk, :] += lax.dot_general(ds, q[:, QK_NOPE:], tn_dims, preferred_element_type=F32)

        _per_q_block(nq, block)

        @pl.when(i == nq - 1)
        def _():
            dkv_ref[...] = dkv_acc[...].astype(BF16)

    return _pc(body, name=name, grid=(h, nq),
               in_specs=[pl.BlockSpec((tq, HEAD_SLOT), lambda hh, i: (i, hh)),
                         pl.BlockSpec((lp, QK_NOPE), lambda hh, i: (0, 2 * hh)),
                         pl.BlockSpec((lp, V_HEAD), lambda hh, i: (0, 2 * hh + 1)),
                         pl.BlockSpec((lp, LANE), lambda hh, i: (0, 0)),
                         pl.BlockSpec((tq, V_HEAD), lambda hh, i: (i, hh)),
                         pl.BlockSpec((None, tq, 1), lambda hh, i: (hh, i, 0)),
                         pl.BlockSpec((tq, V_HEAD), lambda hh, i: (i, hh))],
               out_specs=[pl.BlockSpec((tq, HEAD_SLOT), lambda hh, i: (i, hh)),
                          pl.BlockSpec((lp, QK_NOPE + V_HEAD), lambda hh, i: (0, hh)),
                          pl.BlockSpec((lp, LANE), lambda hh, i: (0, 0))],
               out_shape=[jax.ShapeDtypeStruct((lp, h * HEAD_SLOT), F32),
                          jax.ShapeDtypeStruct((lp, h * (QK_NOPE + V_HEAD)), BF16),
                          jax.ShapeDtypeStruct((lp, LANE), F32)],
               scratch_shapes=[pltpu.VMEM((lp, QK_NOPE + V_HEAD), F32)],
               compiler_params=pltpu.CompilerParams(dimension_semantics=("arbitrary", "arbitrary")))(qx, kv, kv, kr, o, lse, do)


def _rot_half(x):
    lane = lax.broadcasted_iota(jnp.int32, x.shape, 1)
    half = QK_ROPE // 2
    return jnp.where(lane < half, -pltpu.roll(x, LANE - half, 1), pltpu.roll(x, half, 1))


def _rope(x, cos, sin):
    return x * cos + _rot_half(x) * sin


def _unrope(dy, cos, sin):
    return dy * cos - _rot_half(dy * sin)


def _rope_heads(fn, h):
    def apply(rid, q, cos, sin):
        parts = []
        for hh in range(h):
            parts.append(q[:, hh * HEAD_SLOT: hh * HEAD_SLOT + QK_NOPE])
            parts.append(fn(q[:, hh * HEAD_SLOT + QK_NOPE: (hh + 1) * HEAD_SLOT], cos, sin))
        return jnp.concatenate(parts, axis=1)
    return apply


ANY = pl.BlockSpec(memory_space=pl.ANY)


def _place():
    x, y, c = lax.axis_index("x"), lax.axis_index("y"), lax.axis_index("c")
    chips = [(1 - x, y), (x, 1 - y), (1 - x, 1 - y)]
    return x, y, c, chips


def _rcopy(src, dst, send_sem, recv_sem, dev):
    return pltpu.make_async_remote_copy(src_ref=src, dst_ref=dst, send_sem=send_sem, recv_sem=recv_sem,
                                        device_id=dev, device_id_type=MESH)


def _place_shard(shard, dtype, *, name):
    r, cols = shard.shape
    tm = _row_tile(r, cols)
    nblk = r // tm
    me = (2 * lax.axis_index("x") + lax.axis_index("y")).astype(jnp.int32).reshape(1)

    def body(me_ref, s_ref, o_ref):
        o_ref[...] = s_ref[...].astype(dtype)

    return _pc(body, name=name,
               grid_spec=pltpu.PrefetchScalarGridSpec(
                   num_scalar_prefetch=1, grid=(nblk,),
                   in_specs=[pl.BlockSpec((tm, cols), lambda i, mr: (i, 0))],
                   out_specs=pl.BlockSpec((tm, cols), lambda i, mr: (mr[0] * nblk + i, 0))),
               out_shape=jax.ShapeDtypeStruct((4 * r, cols), dtype),
               compiler_params=pltpu.CompilerParams(dimension_semantics=("arbitrary",)))(me, shard)


def _allgather(fulls, *, name):
    n = len(fulls)

    def body(*refs):
        outs = refs[n:2 * n]
        send_sems, recv_sems = refs[2 * n:]
        x, y, c, chips = _place()
        sib = (x, y, 1 - c)
        me = 2 * x + y

        def rows(t, s, half):
            hrows = outs[t].shape[0] // 8
            return outs[t].at[pl.ds((2 * s + half) * hrows, hrows)]

        sent = []
        for t in range(n):
            for j, (cx, cy) in enumerate(chips):
                cp = _rcopy(rows(t, me, c), rows(t, me, c), send_sems.at[6 * t + j], recv_sems.at[6 * t + j], (cx, cy, c))
                cp.start()
                sent.append(cp)
        for t in range(n):
            for j, (cx, cy) in enumerate(chips):
                landed = rows(t, 2 * cx + cy, c)
                _rcopy(landed, landed, send_sems.at[6 * t + j], recv_sems.at[6 * t + j], (cx, cy, c)).wait_recv()
                cp = _rcopy(landed, landed, send_sems.at[6 * t + 3 + j], recv_sems.at[6 * t + 3 + j], sib)
                cp.start()
                sent.append(cp)
        for t in range(n):
            for j, (cx, cy) in enumerate(chips):
                other = rows(t, 2 * cx + cy, 1 - c)
                _rcopy(other, other, send_sems.at[6 * t + 3 + j], recv_sems.at[6 * t + 3 + j], sib).wait_recv()
        for cp in sent:
            cp.wait_send()

    return _pc(body, name=name, in_specs=[ANY] * n, out_specs=[ANY] * n,
               out_shape=[jax.ShapeDtypeStruct(f.shape, f.dtype) for f in fulls],
               input_output_aliases={t: t for t in range(n)},
               scratch_shapes=[pltpu.SemaphoreType.DMA((6 * n,)), pltpu.SemaphoreType.DMA((6 * n,))])(*fulls)


HBM = pl.BlockSpec(memory_space=pltpu.HBM)
SEM = pl.BlockSpec(memory_space=pltpu.SEMAPHORE)
EFFECT = pltpu.SideEffectType.DATAFLOW_SIDE_EFFECTING
TOKEN = jax.ShapeDtypeStruct((8, LANE), F32)


def _in_hbm(a):
    return pltpu.with_memory_space_constraint(a, pltpu.HBM)


def _half_rows(ref, s, half):
    hrows = ref.shape[0] // 8
    return ref.at[pl.ds((2 * s + half) * hrows, hrows)]


def _allgather_ici_start(fulls, before, *, name):
    n = len(fulls)

    def body(*refs):
        ins, send_sems, recv_sems, token = refs[:n], refs[n + 1], refs[n + 2], refs[-1]
        x, y, c, chips = _place()
        me = 2 * x + y
        for t in range(n):
            for j, (cx, cy) in enumerate(chips):
                mine = _half_rows(ins[t], me, c)
                _rcopy(mine, mine, send_sems.at[3 * t + j], recv_sems.at[3 * t + j], (cx, cy, c)).start()
        token[...] = jnp.zeros_like(token)

    res = _pc(body, name=name, in_specs=[HBM] * n + [ANY],
              out_specs=[SEM, SEM] + [HBM] * n + [pl.BlockSpec(memory_space=pltpu.VMEM)],
              out_shape=[pltpu.SemaphoreType.DMA((3 * n,)), pltpu.SemaphoreType.DMA((3 * n,))]
              + [pltpu.HBM(f.shape, f.dtype) for f in fulls] + [TOKEN],
              input_output_aliases={t: 2 + t for t in range(n)},
              compiler_params=pltpu.CompilerParams(has_side_effects=EFFECT))(*[_in_hbm(f) for f in fulls], before)
    return res[0], res[1], res[2:2 + n], res[-1]


def _allgather_ici_wait(send_sems, recv_sems, fulls, after, *, name):
    n = len(fulls)

    def body(*refs):
        ins, send_ref, recv_ref = refs[:n], refs[n], refs[n + 1]
        x, y, c, chips = _place()
        me = 2 * x + y
        for t in range(n):
            for j, (cx, cy) in enumerate(chips):
                cp = _rcopy(_half_rows(ins[t], me, c), _half_rows(ins[t], 2 * cx + cy, c), send_ref.at[3 * t + j],
                            recv_ref.at[3 * t + j], (cx, cy, c))
                cp.wait_send()
                cp.wait_recv()

    return _pc(body, name=name, in_specs=[HBM] * n + [SEM, SEM, ANY], out_specs=[HBM] * n,
               out_shape=[pltpu.HBM(f.shape, f.dtype) for f in fulls],
               input_output_aliases={t: t for t in range(n)},
               compiler_params=pltpu.CompilerParams(has_side_effects=EFFECT))(*fulls, send_sems, recv_sems, after)


def _allgather_forward(fulls, *, name):
    n = len(fulls)

    def body(*refs):
        outs = refs[n:2 * n]
        send_sems, recv_sems = refs[2 * n:]
        x, y, c, chips = _place()
        sent = []
        for t in range(n):
            for j, (cx, cy) in enumerate(chips):
                landed = _half_rows(outs[t], 2 * cx + cy, c)
                cp = _rcopy(landed, landed, send_sems.at[3 * t + j], recv_sems.at[3 * t + j], (x, y, 1 - c))
                cp.start()
                sent.append(cp)
        for t in range(n):
            for j, (cx, cy) in enumerate(chips):
                other = _half_rows(outs[t], 2 * cx + cy, 1 - c)
                _rcopy(other, other, send_sems.at[3 * t + j], recv_sems.at[3 * t + j], (x, y, 1 - c)).wait_recv()
        for cp in sent:
            cp.wait_send()

    return _pc(body, name=name, in_specs=[ANY] * n, out_specs=[ANY] * n,
               out_shape=[jax.ShapeDtypeStruct(f.shape, f.dtype) for f in fulls],
               input_output_aliases={t: t for t in range(n)},
               scratch_shapes=[pltpu.SemaphoreType.DMA((3 * n,)), pltpu.SemaphoreType.DMA((3 * n,))])(*fulls)


def _rs_chips_start(sends, *, name):
    n = len(sends)

    def body(*refs):
        s_refs, b_refs, send_sems, recv_sems, token = refs[:n], refs[n:2 * n], refs[2 * n], refs[2 * n + 1], refs[-1]
        x, y, c, chips = _place()
        for t in range(n):
            for j, (cx, cy) in enumerate(chips):
                _rcopy(s_refs[t].at[2 * cx + cy], b_refs[t].at[j], send_sems.at[3 * t + j], recv_sems.at[3 * t + j],
                       (cx, cy, c)).start()
        token[...] = jnp.zeros_like(token)

    lands = [lax.empty((3,) + s.shape[1:], s.dtype) for s in sends]
    res = _pc(body, name=name, in_specs=[HBM] * (2 * n),
              out_specs=[SEM, SEM] + [HBM] * (2 * n) + [pl.BlockSpec(memory_space=pltpu.VMEM)],
              out_shape=[pltpu.SemaphoreType.DMA((3 * n,)), pltpu.SemaphoreType.DMA((3 * n,))]
              + [pltpu.HBM(a.shape, a.dtype) for a in sends + lands] + [TOKEN],
              input_output_aliases={t: 2 + t for t in range(2 * n)},
              compiler_params=pltpu.CompilerParams(has_side_effects=EFFECT))(*[_in_hbm(a) for a in sends + lands])
    return res[0], res[1], res[2:2 + n], res[2 + n:2 + 2 * n], res[-1]


def _rs_chips_wait(send_sems, recv_sems, sends, lands, after, *, name):
    n = len(sends)

    def body(*refs):
        s_refs, b_refs, send_ref, recv_ref = refs[:n], refs[n:2 * n], refs[2 * n], refs[2 * n + 1]
        x, y, c, chips = _place()
        for t in range(n):
            for j, (cx, cy) in enumerate(chips):
                cp = _rcopy(s_refs[t].at[2 * cx + cy], b_refs[t].at[j], send_ref.at[3 * t + j], recv_ref.at[3 * t + j],
                            (cx, cy, c))
                cp.wait_send()
                cp.wait_recv()

    res = _pc(body, name=name, in_specs=[HBM] * (2 * n) + [SEM, SEM, ANY], out_specs=[HBM] * (2 * n),
              out_shape=[pltpu.HBM(a.shape, a.dtype) for a in list(sends) + list(lands)],
              input_output_aliases={t: t for t in range(2 * n)},
              compiler_params=pltpu.CompilerParams(has_side_effects=EFFECT))(*sends, *lands, send_sems, recv_sems, after)
    return res[n:]


def _rs_sibling(grads, *, name):
    n = len(grads)

    def body(*refs):
        ins, outs = refs[:n], refs[n:2 * n]
        send_sems, recv_sems = refs[2 * n:]
        x, y, c, _ = _place()
        cps = []
        for t in range(n):
            h = ins[t].shape[0] // 8
            for s in range(4):
                cp = _rcopy(ins[t].at[pl.ds((2 * s + 1 - c) * h, h)], outs[t].at[s], send_sems.at[4 * t + s],
                            recv_sems.at[4 * t + s], (x, y, 1 - c))
                cp.start()
                cps.append(cp)
        for cp in cps:
            cp.wait()

    return _pc(body, name=name, in_specs=[ANY] * n, out_specs=[ANY] * n,
               out_shape=[jax.ShapeDtypeStruct((4, g.shape[0] // 8, g.shape[1]), g.dtype) for g in grads],
               scratch_shapes=[pltpu.SemaphoreType.DMA((4 * n,)), pltpu.SemaphoreType.DMA((4 * n,))])(*grads)


def _rs_chips(sends, *, name):
    n = len(sends)

    def body(*refs):
        s_refs, b_refs = refs[:n], refs[n:2 * n]
        send_sems, recv_sems = refs[2 * n:]
        x, y, c, chips = _place()
        cps = []
        for t in range(n):
            for j, (cx, cy) in enumerate(chips):
                cp = _rcopy(s_refs[t].at[2 * cx + cy], b_refs[t].at[j], send_sems.at[3 * t + j], recv_sems.at[3 * t + j],
                            (cx, cy, c))
                cp.start()
                cps.append(cp)
        for cp in cps:
            cp.wait()

    return _pc(body, name=name, in_specs=[ANY] * n, out_specs=[ANY] * n,
               out_shape=[jax.ShapeDtypeStruct((3,) + s.shape[1:], s.dtype) for s in sends],
               scratch_shapes=[pltpu.SemaphoreType.DMA((3 * n,)), pltpu.SemaphoreType.DMA((3 * n,))])(*sends)


def _rs_final(fulls, *, name):
    n = len(fulls)

    def body(*refs):
        outs = refs[n:2 * n]
        send_sems, recv_sems = refs[2 * n:]
        x, y, c, _ = _place()
        cps = []
        for t in range(n):
            cp = _rcopy(outs[t].at[c], outs[t].at[c], send_sems.at[t], recv_sems.at[t], (x, y, 1 - c))
            cp.start()
            cps.append(cp)
        for cp in cps:
            cp.wait()

    return _pc(body, name=name, in_specs=[ANY] * n, out_specs=[ANY] * n,
               out_shape=[jax.ShapeDtypeStruct(f.shape, f.dtype) for f in fulls],
               input_output_aliases={t: t for t in range(n)},
               scratch_shapes=[pltpu.SemaphoreType.DMA((n,)), pltpu.SemaphoreType.DMA((n,))])(*fulls)


def _add_halves(g, a, send_dtype, *, name):
    _, h, cols = a.shape
    th = _row_tile(h, cols)
    g4 = g.reshape(4, 2, h, cols)
    c = lax.axis_index("c").astype(jnp.int32).reshape(1)

    def body(c_ref, g_ref, a_ref, p_ref, s_ref):
        v = g_ref[...] + a_ref[...]
        p_ref[...] = v
        s_ref[...] = v.astype(send_dtype)

    return _pc(body, name=name,
               grid_spec=pltpu.PrefetchScalarGridSpec(
                   num_scalar_prefetch=1, grid=(4, h // th),
                   in_specs=[pl.BlockSpec((None, None, th, cols), lambda s, i, cr: (s, cr[0], i, 0)),
                             pl.BlockSpec((None, th, cols), lambda s, i, cr: (s, i, 0))],
                   out_specs=[pl.BlockSpec((None, th, cols), lambda s, i, cr: (s, i, 0))] * 2),
               out_shape=[jax.ShapeDtypeStruct(a.shape, F32), jax.ShapeDtypeStruct(a.shape, send_dtype)],
               compiler_params=pltpu.CompilerParams(dimension_semantics=("arbitrary", "arbitrary")))(c, g4, a)


def _add_chips(p, b, *, name):
    _, h, cols = p.shape
    th = _row_tile(h, cols)
    idx = jnp.stack([2 * lax.axis_index("x") + lax.axis_index("y"), lax.axis_index("c")]).astype(jnp.int32)

    def body(idx_ref, p_ref, b_ref, r_ref):
        r_ref[...] = ((p_ref[...] + b_ref[0].astype(F32)) + b_ref[1].astype(F32)) + b_ref[2].astype(F32)

    return _pc(body, name=name,
               grid_spec=pltpu.PrefetchScalarGridSpec(
                   num_scalar_prefetch=1, grid=(h // th,),
                   in_specs=[pl.BlockSpec((None, th, cols), lambda i, ir: (ir[0], i, 0)),
                             pl.BlockSpec((3, th, cols), lambda i, ir: (0, i, 0))],
                   out_specs=pl.BlockSpec((None, th, cols), lambda i, ir: (ir[1], i, 0))),
               out_shape=jax.ShapeDtypeStruct((2, h, cols), F32),
               compiler_params=pltpu.CompilerParams(dimension_semantics=("arbitrary",)))(idx, p, b)


def _rs_chip_sums(grads, send_dtypes, tag):
    recv = _rs_sibling(grads, name=f"rs_sibling_{tag}")
    parts, sends = [], []
    for t, (g, a) in enumerate(zip(grads, recv)):
        p, s = _add_halves(g, a, send_dtypes[t], name=f"rs_add_halves_{tag}{t}")
        parts.append(p)
        sends.append(s)
    return parts, sends


def _rs_finish(parts, others, tag):
    halves = [_add_chips(p, b, name=f"rs_add_chips_{tag}{t}") for t, (p, b) in enumerate(zip(parts, others))]
    full = _rs_final(halves, name=f"rs_final_{tag}")
    return [f.reshape(-1, f.shape[-1]) for f in full]


def _s5_discretize(lam_re, lam_im, log_dt, b_re, b_im):
    lam = lax.complex(lam_re, lam_im)
    dt = jnp.exp(log_dt)[:, None]
    lam_bar = jnp.exp(lam * dt)
    b_bar = ((lam_bar - 1.0) / lam)[..., None] * lax.complex(b_re, b_im)
    return jnp.real(lam_bar), jnp.imag(lam_bar), jnp.real(b_bar), jnp.imag(b_bar)


def _lanes_from_gp(re, im, cfg):
    v = jnp.stack([re, im]).reshape(2, cfg.NB, GROUPS_PER_BLOCK, SSM_STATE)
    return jnp.transpose(v, (1, 0, 2, 3)).reshape(1, cfg.NL)


def _gp_from_lanes(v, cfg):
    v = jnp.transpose(v.reshape(cfg.NB, 2, GROUPS_PER_BLOCK, SSM_STATE), (1, 0, 2, 3)).reshape(2, cfg.G, SSM_STATE)
    return v[0], v[1]


def _bb_band(bb_re, bb_im, cfg):
    eye = jnp.eye(GROUPS_PER_BLOCK, dtype=F32)
    bb = jnp.stack([bb_re, bb_im]).reshape(2, cfg.NB, GROUPS_PER_BLOCK, SSM_STATE, SSM_GROUP)
    return jnp.einsum('rjgpc,gh->jgcrhp', bb, eye).reshape(cfg.DS, 2 * GROUPS_PER_BLOCK * SSM_STATE)


def _bb_from_band(m, cfg):
    eye = jnp.eye(GROUPS_PER_BLOCK, dtype=F32)
    m = m.reshape(cfg.NB, GROUPS_PER_BLOCK, SSM_GROUP, 2, GROUPS_PER_BLOCK, SSM_STATE)
    v = jnp.einsum('jgcrhp,gh->rjgpc', m, eye).reshape(2, cfg.G, SSM_STATE, SSM_GROUP)
    return v[0], v[1]


def _cc_band(c_re, c_im, cfg):
    eye = jnp.eye(GROUPS_PER_BLOCK, dtype=F32)
    cc = jnp.stack([c_re, -c_im]).reshape(2, cfg.NB, GROUPS_PER_BLOCK, SSM_GROUP, SSM_STATE)
    return jnp.einsum('rjgcp,gh->jrhpgc', cc, eye).reshape(cfg.NL, GROUPS_PER_BLOCK * SSM_GROUP)


def _cc_from_band(m, cfg):
    eye = jnp.eye(GROUPS_PER_BLOCK, dtype=F32)
    m = m.reshape(cfg.NB, 2, GROUPS_PER_BLOCK, SSM_STATE, GROUPS_PER_BLOCK, SSM_GROUP)
    v = jnp.einsum('jrhpgc,gh->rjgcp', m, eye).reshape(2, cfg.G, SSM_GROUP, SSM_STATE)
    return v[0], -v[1]


PACK_COLS = 512
PACK_ROW_ALIGN = 64


def _pack(arrs):
    flat = jnp.concatenate([a.reshape(-1).astype(F32) for a in arrs])
    unit = PACK_COLS * PACK_ROW_ALIGN
    total = -(-flat.shape[0] // unit) * unit
    return jnp.pad(flat, (0, total - flat.shape[0])).reshape(-1, PACK_COLS)


def _unpack(p, shapes):
    flat = p.reshape(-1)
    out, off = [], 0
    for shp in shapes:
        size = math.prod(shp)
        out.append(flat[off:off + size].reshape(shp))
        off += size
    return out


def _adamw(w, g, m, v, *, name, emit_grad=False):
    c1 = 1.0 / (1.0 - ADAM_B1 ** ADAM_STEP)
    c2 = 1.0 / (1.0 - ADAM_B2 ** ADAM_STEP)

    def fn(rid, wv, gv, mv, vv):
        mn = ADAM_B1 * mv + (1.0 - ADAM_B1) * gv
        vn = ADAM_B2 * vv + (1.0 - ADAM_B2) * (gv * gv)
        delta = -ADAM_LR * ((mn * c1) / (jnp.sqrt(vn * c2) + ADAM_EPS) + ADAM_WD * wv)
        return (gv, delta, mn, vn) if emit_grad else (delta, mn, vn)

    cols = w.shape[1]
    return _ew(fn, [w, g, m, v], [], [(cols, F32)] * (4 if emit_grad else 3), name=name)


def _to_comm_layout(name, w, cfg):
    w = w[0]
    if name == 'w_in':
        return jnp.pad(w, ((0, 0), (0, cfg.DINP - cfg.DIN)))
    if name == 'w_q_b':
        hs = w.shape[1] // (QK_NOPE + QK_ROPE)
        wt = w.T.reshape(hs, QK_NOPE + QK_ROPE, cfg.QL)
        return jnp.pad(wt, ((0, 0), (0, HEAD_SLOT - QK_NOPE - QK_ROPE), (0, 0))).reshape(hs * HEAD_SLOT, cfg.QL)
    if name == 'w_kv_b':
        return w.T
    if name == 'w_up':
        wt = w.T.reshape(2, cfg.F // 4, cfg.D)
        return jnp.pad(wt, ((0, 0), (0, cfg.FQ - cfg.F // 4), (0, 0))).reshape(2 * cfg.FQ, cfg.D)
    if name == 'w_down':
        return jnp.pad(w, ((0, cfg.FQ - cfg.F // 4), (0, 0)))
    return w


def _from_comm_layout(name, g, cfg):
    if name == 'w_in':
        g = g[:, :cfg.DIN]
    elif name == 'w_q_b':
        hs = g.shape[0] // HEAD_SLOT
        g = g.reshape(hs, HEAD_SLOT, cfg.QL)[:, :QK_NOPE + QK_ROPE].reshape(hs * (QK_NOPE + QK_ROPE), cfg.QL).T
    elif name == 'w_kv_b':
        g = g.T
    elif name == 'w_up':
        g = g.reshape(2, cfg.FQ, cfg.D)[:, :cfg.F // 4].reshape(cfg.F // 2, cfg.D).T
    elif name == 'w_down':
        g = g[:cfg.F // 4]
    return g[None]


def _ff_pad(v, cfg):
    k = v.shape[0]
    return jnp.pad(v.reshape(k, 4, cfg.F // 4), ((0, 0), (0, 0), (0, cfg.FQ - cfg.F // 4))).reshape(k, cfg.FP)


def _ff_unpad(v, cfg):
    k = v.shape[0]
    return v.reshape(k, 4, cfg.FQ)[:, :, :cfg.F // 4].reshape(k, cfg.F)


def _step(cfg, w, m, v, x, loss_target):
    lp, d, ds, nl = cfg.LP, cfg.D, cfg.DS, cfg.NL
    blk = 2 * GROUPS_PER_BLOCK * SSM_STATE
    gw = GROUPS_PER_BLOCK * SSM_GROUP
    xi, yi = lax.axis_index("x"), lax.axis_index("y")
    me = 2 * xi + yi

    placed = [_place_shard(_to_comm_layout(n, w[n], cfg), BF16, name=f"place_{n}") for n in BIG]
    conv_w_shard = jnp.pad(w['conv_w'][0], ((0, ROW_ALIGN - 3), (0, cfg.FQ - cfg.F // 4)))
    placed += [_place_shard(w['meta_tokens'], F32, name="place_meta"), _place_shard(conv_w_shard, F32, name="place_conv_w")]
    full = _allgather(placed[:5] + placed[7:], name="allgather_weights")
    w_in, w_glu, w_qt, w_kvt, w_out = full[:5]
    meta = jnp.transpose(full[5].reshape(4, N_META, d // 4), (1, 0, 2)).reshape(N_META, d)
    conv_w = jnp.transpose(full[6].reshape(4, ROW_ALIGN, cfg.FQ)[:, :3], (1, 0, 2)).reshape(3, cfg.FP)
    conv_b = _ff_pad(w['conv_b'], cfg)
    ffn_send, ffn_recv, ffn_flying, ffn_token = _allgather_ici_start(placed[5:7], full[5], name="allgather_ffn_start")
    mix_norm = w['mix_norm'] + ffn_token[0:1, 0:1]

    pos = (jnp.arange(lp, dtype=jnp.int32) - PAD).astype(F32)
    inv_freq = 1.0 / (ROPE_BASE ** (jnp.arange(0, QK_ROPE, 2, dtype=F32) / QK_ROPE))
    ang = pos[:, None] * inv_freq[None, :]
    zpad = jnp.zeros((lp, LANE - QK_ROPE), F32)
    cos_t = jnp.concatenate([jnp.cos(ang), jnp.cos(ang), zpad], axis=1)
    sin_t = jnp.concatenate([jnp.sin(ang), jnp.sin(ang), zpad], axis=1)

    s5_in = (w['lam_re'][0], w['lam_im'][0], w['log_dt'][0], w['b_re'][0], w['b_im'][0])
    (a_re, a_im, bb_re, bb_im), s5_vjp = jax.vjp(_s5_discretize, *s5_in)
    a_l = _lanes_from_gp(a_re, a_im, cfg)
    bb_band = _bb_band(bb_re, bb_im, cfg).astype(BF16)
    cc_band = _cc_band(w['c_re'][0], w['c_im'][0], cfg).astype(BF16)
    d_skip, b_glu = w['d_skip'], w['b_glu']

    h0 = jnp.concatenate([jnp.zeros((PAD, d), F32), meta, x[0]], axis=0)
    xn = _rms_fwd(h0, mix_norm, name="rms_mix")
    z = _mm(xn, w_in, name="mm_in", tn=_tile(cfg.DINP, 640))
    u = (z, ds, 0)
    q_a = (z, cfg.QL, ds // cfg.QL)
    kv_a = (z, cfg.KVL, (ds + cfg.QL) // cfg.KVL)
    k_pe = (z, LANE, (ds + cfg.QL + cfg.KVL) // LANE)

    bu = _mm(z, bb_band, name="mm_s5_bu", dims=(lp, nl, gw), tn=blk, tk=gw,
             a_idx=lambda i, j, k: (i, j), b_idx=lambda i, j, k: (j, 0))
    hs = _s5_scan(bu, a_l, reverse=False, name="s5_scan_fwd")
    yc = _mm(hs, cc_band, name="mm_s5_y", dims=(lp, ds, blk), tn=gw, tk=blk,
             a_idx=lambda i, j, k: (i, j), b_idx=lambda i, j, k: (j, 0))

    def s5_y(ycv, uv, dk):
        return ycv + dk * uv

    gl = _ew(lambda rid, ycv, uv, dk: jax.nn.gelu(s5_y(ycv, uv, dk)), [yc, u], [d_skip], [(ds, BF16)], name="s5_gelu")[0]
    tg = _mm(gl, w_glu, name="mm_glu")
    ya = _ew(lambda rid, ycv, uv, tv, dk, bg: jax.nn.gelu(s5_y(ycv, uv, dk)) * jax.nn.sigmoid(tv + bg),
             [yc, u, tg], [d_skip, b_glu], [(ds, F32)], name="s5_glu")[0]

    qn = _rms_fwd(q_a, w['q_a_norm'], name="rms_q")
    kvn = _rms_fwd(kv_a, w['kv_a_norm'], name="rms_kv")
    q_raw = _mm(qn, w_qt, tb=True, name="mm_q")
    qx = _ew(_rope_heads(_rope, cfg.H), [q_raw, cos_t, sin_t], [], [(cfg.H * HEAD_SLOT, BF16)], name="rope_q")[0]
    kv = _mm(kvn, w_kvt, tb=True, out_dtype=BF16, name="mm_kv")
    kr = _ew(lambda rid, kp, cs, sn: _rope(kp, cs, sn), [k_pe, cos_t, sin_t], [], [(LANE, BF16)], name="rope_k")[0]
    o, lse = _attn_fwd(qx, kv, kr, cfg, name="attn_fwd")

    def norm2(rid, yav, ov, gs, ga):
        return jnp.concatenate([_rms_parts(yav, gs)[0] * gs, _rms_parts(ov, ga)[0] * ga], axis=1)

    yn = _ew(norm2, [ya, o], [w['out_norm_ssm'], w['out_norm_attn']], [(cfg.DMIX, BF16)], name="rms_out")[0]
    h1 = _mm(yn, w_out, res=h0, name="mm_out")
    xn2 = _rms_fwd(h1, w['ffn_norm'], name="rms_ffn")
    ffn_landed = _allgather_ici_wait(ffn_send, ffn_recv, ffn_flying, xn2, name="allgather_ffn_wait")
    w_upt, w_down = _allgather_forward(ffn_landed, name="allgather_ffn_forward")
    up = _mm(xn2, w_upt, tb=True, name="mm_up")
    act = _conv_fwd(up, conv_w, conv_b, name="conv_fwd")
    h2 = _mm(act, w_down, res=h1, tm=_tile(lp, 544, ROW_ALIGN), name="mm_down")

    g_final = w['final_norm'].reshape(1, d)

    def head(rid, hv, tv, gv):
        xhat, r = _rms_parts(hv, gv)
        valid = rid >= PAD + N_META
        diff = jnp.where(valid, xhat * gv - tv, 0.0)
        dout = diff * (1.0 / d)
        dxhat = dout * gv
        dx = r * (dxhat - xhat * jnp.mean(dxhat * xhat, axis=-1, keepdims=True))
        return dx, dx, dout * xhat, 0.5 * diff * dout

    dh2, dh2_b, dg_final, loss_cols = _ew(head, [h2, (loss_target[0], d, 0, SKIP)], [g_final], [(d, F32), (d, BF16)], [d, d],
                                          tm=PAD + N_META, name="loss_head")
    loss = lax.psum(jnp.sum(loss_cols), ("x", "y", "c"))

    dact = _mm(dh2_b, w_down, tb=True, out_dtype=BF16, name="mm_dact")
    dw_down = _mm(act, dh2_b, ta=True, tn=d, tm=512, name="mm_dw_down")
    dup, dconv_w, dconv_b = _conv_bwd(up, dact, conv_w, conv_b, name="conv_bwd")
    tk_up, tm_up = _tile(cfg.FP, 1408), _tile(cfg.FP, 512)
    dxn2 = _mm(dup, w_upt, dims=(lp, d, 2 * cfg.FP), tk=tk_up, a_lead=True, name="mm_dxn2",
               a_idx=lambda i, j, k: (k // (cfg.FP // tk_up), i, k % (cfg.FP // tk_up)))
    dw_upt = _mm(dup, xn2, ta=True, dims=(2 * cfg.FP, d, lp), tn=d, tm=tm_up, a_lead=True, name="mm_dw_up",
                 a_idx=lambda i, j, k: (i // (cfg.FP // tm_up), 0, i % (cfg.FP // tm_up)))
    ffn_parts, ffn_sends = _rs_chip_sums([dw_upt, dw_down], [BF16, BF16], "ffn")
    rs_send, rs_recv, rs_flying, rs_lands, rs_token = _rs_chips_start(ffn_sends, name="rs_chips_ffn_start")
    ffn_norm = w['ffn_norm'] + rs_token[0:1, 0:1]
    dh1, dh1_b, dg_ffn = _rms_bwd(h1, ffn_norm, dxn2, res=dh2, mask=True, with_bf16=True, name="rms_ffn_bwd")

    dyn = _mm(dh1_b, w_out, tb=True, name="mm_dyn")
    dw_out = _mm(yn, dh1_b, ta=True, tn=d, tm=512, name="mm_dw_out")
    dya, dg_ssm = _rms_bwd(ya, w['out_norm_ssm'], (dyn, ds, 0), name="rms_ssm_bwd")
    do, dg_attn = _rms_bwd(o, w['out_norm_attn'], (dyn, cfg.DATTN, ds // cfg.DATTN), name="rms_attn_bwd")

    dqx, dkv, dkr = _attn_bwd(qx, kv, kr, o, lse, do, cfg, name="attn_bwd")
    dq_raw = _ew(_rope_heads(_unrope, cfg.H), [dqx, cos_t, sin_t], [], [(cfg.H * HEAD_SLOT, BF16)], name="unrope_q")[0]
    dk_pe = _ew(lambda rid, dk, cs, sn: _unrope(dk, cs, sn), [dkr, cos_t, sin_t], [], [(LANE, F32)], name="unrope_k")[0]
    dqn = _mm(dq_raw, w_qt, name="mm_dqn")
    dw_qt = _mm(dq_raw, qn, ta=True, tm=512, name="mm_dw_q")
    dkvn = _mm(dkv, w_kvt, name="mm_dkvn")
    dw_kvt = _mm(dkv, kvn, ta=True, tm=512, name="mm_dw_kv")
    dq_a, dg_q = _rms_bwd(q_a, w['q_a_norm'], dqn, name="rms_q_bwd")
    dkv_a, dg_kv = _rms_bwd(kv_a, w['kv_a_norm'], dkvn, name="rms_kv_bwd")

    def glu_bwd(rid, ycv, uv, tv, dyav, dk, bg):
        gelu = jax.nn.gelu(s5_y(ycv, uv, dk))
        sg = jax.nn.sigmoid(tv + bg)
        dt = dyav * gelu * sg * (1.0 - sg)
        return dt, dyav * sg, dt

    dt_b, dgl1, db_glu = _ew(glu_bwd, [yc, u, tg, dya], [d_skip, b_glu], [(ds, BF16), (ds, F32)], [ds], name="s5_glu_bwd")
    dgl = _mm(dt_b, w_glu, tb=True, res=dgl1, name="mm_dgl")
    dw_glu = _mm(gl, dt_b, ta=True, tm=512, name="mm_dw_glu")

    def gelu_bwd(rid, ycv, uv, dglv, dk):
        _, vjp = jax.vjp(jax.nn.gelu, s5_y(ycv, uv, dk))
        dy = vjp(dglv)[0]
        return dy, dy * dk, dy * uv

    dy_b, du_skip, dd_skip = _ew(gelu_bwd, [yc, u, dgl], [d_skip], [(ds, BF16), (ds, F32)], [ds], name="s5_gelu_bwd")
    dhs = _mm(dy_b, cc_band, tb=True, name="mm_s5_dhs", dims=(lp, nl, gw), tn=blk, tk=gw,
              a_idx=lambda i, j, k: (i, j), b_idx=lambda i, j, k: (j, 0))
    dcc_band = _mm(hs, dy_b, ta=True, name="mm_s5_dcc", dims=(nl, gw, lp), tm=blk, tn=gw,
                   a_idx=lambda i, j, k: (0, i), b_idx=lambda i, j, k: (0, i))
    gs = _s5_scan(dhs, a_l, reverse=True, name="s5_scan_bwd")
    da_l = _s5_da(gs, hs, cfg, name="s5_da")
    dbb_band = _mm(z, gs, ta=True, name="mm_s5_dbb", dims=(ds, blk, lp), tm=gw, tn=blk,
                   a_idx=lambda i, j, k: (0, i), b_idx=lambda i, j, k: (0, i))
    du = _mm(gs, bb_band, tb=True, res=du_skip, name="mm_s5_du", dims=(lp, ds, blk), tn=gw, tk=blk,
             a_idx=lambda i, j, k: (i, j), b_idx=lambda i, j, k: (j, 0))

    dz = jnp.concatenate([du, dq_a, dkv_a, dk_pe], axis=1).astype(BF16)
    dxn = _mm(dz, w_in, tb=True, name="mm_dxn")
    dw_in = _mm(xn, dz, ta=True, tm=512, tn=_tile(cfg.DINP, 1024), name="mm_dw_in")
    def mix_bwd(rid, xv, dyv, resv, gv):
        dx, dg = _rms_bwd_block(xv, gv, dyv)
        dx = dx + resv
        return dx, dx, dg

    grad_x, dh0_head, dg_mix = _ew(mix_bwd, [h0, dxn, dh1], [mix_norm], [(d, F32, SKIP), (d, F32, FIRST)], [d],
                                   tm=PAD + N_META, name="rms_mix_bwd")
    grad_x = grad_x[None]

    da_re, da_im = _gp_from_lanes(da_l, cfg)
    dbb_re, dbb_im = _bb_from_band(dbb_band, cfg)
    dlam_re, dlam_im, dlog_dt, db_re, db_im = s5_vjp((da_re, da_im, dbb_re, dbb_im))
    dc_re, dc_im = _cc_from_band(dcc_band, cfg)
    local_small = {
        'meta_tokens': dh0_head[PAD:], 'mix_norm': dg_mix, 'lam_re': dlam_re, 'lam_im': dlam_im, 'log_dt': dlog_dt,
        'b_re': db_re, 'b_im': db_im, 'c_re': dc_re, 'c_im': dc_im, 'd_skip': dd_skip, 'b_glu': db_glu, 'q_a_norm': dg_q,
        'kv_a_norm': dg_kv, 'out_norm_ssm': dg_ssm, 'out_norm_attn': dg_attn, 'ffn_norm': dg_ffn,
        'conv_w': _ff_unpad(dconv_w, cfg), 'conv_b': _ff_unpad(dconv_b, cfg), 'final_norm': dg_final,
    }
    small_shapes = [local_small[n].shape for n in SMALL]

    rest_local = [dw_in, dw_glu, dw_qt, dw_kvt, dw_out, _pack([local_small[n] for n in SMALL])]
    rest_parts, rest_sends = _rs_chip_sums(rest_local, [BF16] * 5 + [F32], "rest")
    ffn_others = _rs_chips_wait(rs_send, rs_recv, rs_flying, rs_lands, rest_sends[0], name="rs_chips_ffn_wait")
    rest_others = _rs_chips(rest_sends, name="rs_chips_rest")
    reduced = _rs_finish(rest_parts + ffn_parts, list(rest_others) + list(ffn_others), "grads")
    reduced = reduced[:5] + reduced[6:8] + [reduced[5]]
    small_full = _allgather([_place_shard(reduced[7], F32, name="place_small")], name="allgather_small")[0]
    small_sum = dict(zip(SMALL, _unpack(small_full, small_shapes)))

    padded_rows = ('w_down',)
    grads = {n: _from_comm_layout(n, g, cfg) for n, g in zip(BIG, reduced[:7]) if n not in padded_rows}
    for n in SMALL:
        g = small_sum[n]
        if n == 'meta_tokens':
            g = lax.dynamic_slice_in_dim(g, me * (d // 4), d // 4, axis=1)
        elif n == 'conv_w':
            g = lax.dynamic_slice_in_dim(g, me * (cfg.F // 4), cfg.F // 4, axis=1)[None]
        else:
            g = g.reshape(w[n].shape)
        grads[n] = g

    delta, new_m, new_v = {}, {}, {}
    for n, red in zip(BIG, reduced[:7]):
        shp = w[n].shape
        w2, m2, v2 = [t.reshape(shp[-2], shp[-1]) for t in (w[n], m[n], v[n])]
        if n in padded_rows:
            g2, dl, mn, vn = _adamw(w2, red, m2, v2, emit_grad=True, name=f"adamw_{n}")
            grads[n] = g2.reshape(shp)
        else:
            dl, mn, vn = _adamw(w2, grads[n].reshape(shp[-2], shp[-1]), m2, v2, name=f"adamw_{n}")
        delta[n], new_m[n], new_v[n] = dl.reshape(shp), mn.reshape(shp), vn.reshape(shp)
    shapes = [w[n].shape for n in SMALL]
    packs = [_pack([src[n] for n in SMALL]) for src in (w, grads, m, v)]
    for dst, p in zip((delta, new_m, new_v), _adamw(*packs, name="adamw_small")):
        dst.update(zip(SMALL, _unpack(p, shapes)))

    return (loss, grad_x, *[grads[n] for n in WEIGHTS], *[delta[n] for n in WEIGHTS],
            *[new_m[n] for n in WEIGHTS], *[new_v[n] for n in WEIGHTS])


def kernel(x, meta_tokens, mix_norm, w_in, lam_re, lam_im, log_dt, b_re, b_im, c_re, c_im, d_skip, w_glu, b_glu, q_a_norm, w_q_b, kv_a_norm, w_kv_b, out_norm_ssm, out_norm_attn, w_out, ffn_norm, w_up, conv_w, conv_b, w_down, final_norm, loss_target, m_meta_tokens, m_mix_norm, m_w_in, m_lam_re, m_lam_im, m_log_dt, m_b_re, m_b_im, m_c_re, m_c_im, m_d_skip, m_w_glu, m_b_glu, m_q_a_norm, m_w_q_b, m_kv_a_norm, m_w_kv_b, m_out_norm_ssm, m_out_norm_attn, m_w_out, m_ffn_norm, m_w_up, m_conv_w, m_conv_b, m_w_down, m_final_norm, v_meta_tokens, v_mix_norm, v_w_in, v_lam_re, v_lam_im, v_log_dt, v_b_re, v_b_im, v_c_re, v_c_im, v_d_skip, v_w_glu, v_b_glu, v_q_a_norm, v_w_q_b, v_kv_a_norm, v_w_kv_b, v_out_norm_ssm, v_out_norm_attn, v_w_out, v_ffn_norm, v_w_up, v_conv_w, v_conv_b, v_w_down, v_final_norm):
    args = dict(locals())
    w = {n: args[n] for n in WEIGHTS}
    m = {n: args["m_" + n] for n in WEIGHTS}
    v = {n: args["v_" + n] for n in WEIGHTS}
    return _step(PROD, w, m, v, x, loss_target)
```

```python
import functools
import math
from typing import NamedTuple

import jax
import jax.numpy as jnp
from jax import lax
from jax.experimental import pallas as pl
from jax.experimental.pallas import tpu as pltpu

F32, BF16 = jnp.float32, jnp.bfloat16
MESH = pl.DeviceIdType.MESH
LANE = 128
ROW_ALIGN = 16
N_META = 16
PAD = 112
CHUNK = 64
SSM_GROUP = 16
SSM_STATE = 64
GROUPS_PER_BLOCK = 8
QK_NOPE, QK_ROPE, V_HEAD = 128, 64, 128
HEAD_SLOT = 256
ROPE_BASE = 10000.0
EPS = 1e-6
ADAM_LR, ADAM_B1, ADAM_B2, ADAM_EPS, ADAM_WD, ADAM_STEP = 0.001, 0.9, 0.999, 1e-08, 0.01, 10
DT_F32_BLOCK_BYTES = 1 << 20
SKIP, FIRST = "skip", "first"


class Cfg(NamedTuple):
    D: int
    S: int
    DS: int
    H: int
    QL: int
    KVL: int
    F: int

    @property
    def LP(self):
        return PAD + N_META + self.S

    @property
    def G(self):
        return self.DS // SSM_GROUP

    @property
    def NB(self):
        return self.G // GROUPS_PER_BLOCK

    @property
    def NL(self):
        return 2 * self.G * SSM_STATE

    @property
    def DATTN(self):
        return self.H * V_HEAD

    @property
    def DMIX(self):
        return self.DS + self.DATTN

    @property
    def DIN(self):
        return self.DS + self.QL + self.KVL + QK_ROPE

    @property
    def DINP(self):
        return self.DS + self.QL + self.KVL + LANE

    @property
    def FQ(self):
        return -(-(self.F // 4) // LANE) * LANE

    @property
    def FP(self):
        return 4 * self.FQ


PROD = Cfg(D=2048, S=2048, DS=1024, H=8, QL=512, KVL=256, F=5504)

WEIGHTS = ['meta_tokens', 'mix_norm', 'w_in', 'lam_re', 'lam_im', 'log_dt', 'b_re', 'b_im', 'c_re', 'c_im', 'd_skip',
           'w_glu', 'b_glu', 'q_a_norm', 'w_q_b', 'kv_a_norm', 'w_kv_b', 'out_norm_ssm', 'out_norm_attn', 'w_out',
           'ffn_norm', 'w_up', 'conv_w', 'conv_b', 'w_down', 'final_norm']
BIG = ['w_in', 'w_glu', 'w_q_b', 'w_kv_b', 'w_out', 'w_up', 'w_down']
SMALL = [n for n in WEIGHTS if n not in BIG]


def _pc(body, **kw):
    return pl.pallas_call(body, **kw)


def _tile(n, target, align=LANE):
    best = None
    d = align
    while d <= min(n, target):
        if n % d == 0:
            best = d
        d += align
    return best if best is not None else n


def _row_tile(rows, cols):
    return _tile(rows, max(ROW_ALIGN, DT_F32_BLOCK_BYTES // (4 * cols)), ROW_ALIGN)


def _mm(a, b, *, name, ta=False, tb=False, tm=None, tn=512, tk=None, out_dtype=F32, res=None,
        a_idx=None, b_idx=None, dims=None, a_lead=False):
    if dims is None:
        m, k = (a.shape[1], a.shape[0]) if ta else a.shape
        n = b.shape[0] if tb else b.shape[1]
    else:
        m, n, k = dims
    tm = _tile(m, tm or m, LANE if ta else ROW_ALIGN)
    tn = _tile(n, tn)
    tk = _tile(k, tk or k, ROW_ALIGN if (ta and not tb) else LANE)
    nm, nn, nk = m // tm, n // tn, k // tk
    a_idx = a_idx or ((lambda i, j, kk: (kk, i)) if ta else (lambda i, j, kk: (i, kk)))
    b_idx = b_idx or ((lambda i, j, kk: (j, kk)) if tb else (lambda i, j, kk: (kk, j)))
    dn = (((0 if ta else 1,), (1 if tb else 0,)), ((), ()))

    def body(*refs):
        a_ref, b_ref = refs[0], refs[1]
        r_ref = refs[2] if res is not None else None
        o_ref = refs[3] if res is not None else refs[2]
        d = lax.dot_general(a_ref[...].astype(BF16), b_ref[...].astype(BF16), dn, preferred_element_type=F32)

        def finish(r):
            if r_ref is not None:
                r = r + r_ref[...].astype(F32)
            o_ref[...] = r.astype(out_dtype)

        if nk == 1:
            finish(d)
        else:
            acc = refs[-1]
            kk = pl.program_id(2)

            @pl.when(kk == 0)
            def _():
                acc[...] = d

            @pl.when(kk > 0)
            def _():
                acc[...] += d

            @pl.when(kk == nk - 1)
            def _():
                finish(acc[...])

    a_blk = ((None,) if a_lead else ()) + ((tk, tm) if ta else (tm, tk))
    in_specs = [pl.BlockSpec(a_blk, a_idx), pl.BlockSpec((tn, tk) if tb else (tk, tn), b_idx)]
    args = [a, b]
    if res is not None:
        in_specs.append(pl.BlockSpec((tm, tn), lambda i, j, kk: (i, j)))
        args.append(res)
    return _pc(body, name=name, grid=(nm, nn, nk), in_specs=in_specs,
               out_specs=pl.BlockSpec((tm, tn), lambda i, j, kk: (i, j)),
               out_shape=jax.ShapeDtypeStruct((m, n), out_dtype),
               scratch_shapes=[pltpu.VMEM((tm, tn), F32)] if nk > 1 else [],
               compiler_params=pltpu.CompilerParams(dimension_semantics=("parallel", "parallel", "arbitrary")))(*args)


def _ew(fn, ins, vecs, outs, sums=(), *, name, tm=None):
    ins = [x if isinstance(x, tuple) else (x, x.shape[1], 0) for x in ins]
    ins = [x if len(x) == 4 else x + (None,) for x in ins]
    outs = [o if len(o) == 3 else o + (None,) for o in outs]
    rows = ins[0][0].shape[0]
    cmax = max([c for _, c, _, _ in ins] + [c for c, _, _ in outs])
    tm = tm or _row_tile(rows, cmax)
    n_in, n_vec, n_out, n_sum = len(ins), len(vecs), len(outs), len(sums)

    def body(*refs):
        i = pl.program_id(0)
        rid = i * tm + lax.broadcasted_iota(jnp.int32, (tm, 1), 0)
        vals = [r[...] for r in refs[:n_in + n_vec]]
        res = fn(rid, *vals)
        res = res if isinstance(res, (tuple, list)) else (res,)
        o_refs = refs[n_in + n_vec:]
        for o_ref, r, (_, _, mode) in zip(o_refs[:n_out], res[:n_out], outs):
            if mode == FIRST:
                @pl.when(i == 0)
                def _():
                    o_ref[...] = r.astype(o_ref.dtype)
            else:
                o_ref[...] = r.astype(o_ref.dtype)
        for o_ref, r in zip(o_refs[n_out:], res[n_out:]):
            part = jnp.sum(r.astype(F32), axis=0, keepdims=True)

            @pl.when(i == 0)
            def _():
                o_ref[...] = part

            @pl.when(i > 0)
            def _():
                o_ref[...] += part

    def row_idx(mode):
        if mode == SKIP:
            return lambda i, cb=0: (jnp.maximum(i - 1, 0), cb)
        if mode == FIRST:
            return lambda i, cb=0: (0, cb)
        return lambda i, cb=0: (i, cb)

    in_specs = [pl.BlockSpec((tm, c), functools.partial(row_idx(mode), cb=cb)) for _, c, cb, mode in ins]
    in_specs += [pl.BlockSpec(v.shape, functools.partial(lambda i, nd: (0,) * nd, nd=v.ndim)) for v in vecs]
    out_specs = [pl.BlockSpec((tm, c), row_idx(mode)) for c, _, mode in outs]
    out_specs += [pl.BlockSpec((1, c), lambda i: (0, 0)) for c in sums]
    out_rows = {None: rows, SKIP: rows - tm, FIRST: tm}
    out_shape = [jax.ShapeDtypeStruct((out_rows[mode], c), dt) for c, dt, mode in outs]
    out_shape += [jax.ShapeDtypeStruct((1, c), F32) for c in sums]
    return _pc(body, name=name, grid=(rows // tm,), in_specs=in_specs, out_specs=out_specs, out_shape=out_shape,
               compiler_params=pltpu.CompilerParams(dimension_semantics=("arbitrary",)))(*[x[0] for x in ins], *vecs)


def _rms_parts(x, g):
    r = lax.rsqrt(jnp.mean(x * x, axis=-1, keepdims=True) + EPS)
    return x * r, r


def _rms_bwd_block(x, g, dy):
    xhat, r = _rms_parts(x, g)
    dxhat = dy * g
    dx = r * (dxhat - xhat * jnp.mean(dxhat * xhat, axis=-1, keepdims=True))
    return dx, dy * xhat


def _rms_fwd(x, g, *, name):
    c = x[1] if isinstance(x, tuple) else x.shape[1]
    return _ew(lambda rid, xv, gv: _rms_parts(xv.astype(F32), gv)[0] * gv, [x], [g], [(c, BF16)], name=name)[0]


def _rms_bwd(x, g, dy, *, name, res=None, mask=False, with_bf16=False):
    c = x[1] if isinstance(x, tuple) else x.shape[1]

    def fn(rid, xv, dyv, *rest):
        gv = rest[-1]
        dx, dg = _rms_bwd_block(xv.astype(F32), gv, dyv.astype(F32))
        if res is not None:
            dx = dx + rest[0]
        if mask:
            dx = jnp.where(rid >= PAD, dx, 0.0)
        return (dx, dx, dg) if with_bf16 else (dx, dg)

    ins = [x, dy] + ([res] if res is not None else [])
    outs = [(c, F32)] + ([(c, BF16)] if with_bf16 else [])
    return _ew(fn, ins, [g], outs, [c], name=name)


S5_W = GROUPS_PER_BLOCK * SSM_STATE
S5_GW = GROUPS_PER_BLOCK * SSM_GROUP
S5_UNROLL = 8
S5_DA_ROWS = 272


def _s5_scan_in_place(ref, a_ref, *, reverse):
    lp = ref.shape[0]
    ar = a_ref[:, :S5_W]
    ai = -a_ref[:, S5_W:] if reverse else a_ref[:, S5_W:]

    def step(n, carry):
        hr, hi = carry
        for q in range(S5_UNROLL):
            t = n * S5_UNROLL + q
            t = lp - 1 - t if reverse else t
            nr = ar * hr - ai * hi + ref[pl.ds(t, 1), :S5_W]
            ni = ar * hi + ai * hr + ref[pl.ds(t, 1), S5_W:]
            ref[pl.ds(t, 1), :S5_W] = nr
            ref[pl.ds(t, 1), S5_W:] = ni
            hr, hi = nr, ni
        return hr, hi

    z = jnp.zeros((1, S5_W), F32)
    lax.fori_loop(0, lp // S5_UNROLL, step, (z, z))


def _s5_fwd(z, bb_band, cc_band, a_l, cfg, *, name):
    lp, ds, nl = cfg.LP, cfg.DS, cfg.NL

    def body(u_ref, bb_ref, cc_ref, a_ref, hs_ref, y_ref):
        hs_ref[...] = jnp.dot(u_ref[...].astype(BF16), bb_ref[...], preferred_element_type=F32)
        _s5_scan_in_place(hs_ref, a_ref, reverse=False)
        y_ref[...] = jnp.dot(hs_ref[...].astype(BF16), cc_ref[...], preferred_element_type=F32)

    return _pc(body, name=name, grid=(cfg.NB,),
               in_specs=[pl.BlockSpec((lp, S5_GW), lambda j: (0, j)), pl.BlockSpec((S5_GW, 2 * S5_W), lambda j: (j, 0)),
                         pl.BlockSpec((2 * S5_W, S5_GW), lambda j: (j, 0)), pl.BlockSpec((1, 2 * S5_W), lambda j: (0, j))],
               out_specs=[pl.BlockSpec((lp, 2 * S5_W), lambda j: (0, j)), pl.BlockSpec((lp, S5_GW), lambda j: (0, j))],
               out_shape=[jax.ShapeDtypeStruct((lp, nl), F32), jax.ShapeDtypeStruct((lp, ds), F32)],
               compiler_params=pltpu.CompilerParams(dimension_semantics=("parallel",)))(z, bb_band, cc_band, a_l)


def _s5_bwd(dy, hs, z, bb_band, cc_band, a_l, du_skip, cfg, *, name):
    lp, ds, nl = cfg.LP, cfg.DS, cfg.NL
    nt = (((1,), (1,)), ((), ()))
    tn = (((0,), (0,)), ((), ()))

    def body(dy_ref, hs_ref, u_ref, bb_ref, cc_ref, a_ref, sk_ref, du_ref, dbb_ref, dcc_ref, da_ref, g_ref):
        dyv = dy_ref[...]
        g_ref[...] = lax.dot_general(dyv, cc_ref[...], nt, preferred_element_type=F32)
        _s5_scan_in_place(g_ref, a_ref, reverse=True)
        dcc_ref[...] = lax.dot_general(hs_ref[...].astype(BF16), dyv, tn, preferred_element_type=F32)
        gb = g_ref[...].astype(BF16)
        dbb_ref[...] = lax.dot_general(u_ref[...].astype(BF16), gb, tn, preferred_element_type=F32)
        du_ref[...] = lax.dot_general(gb, bb_ref[...], nt, preferred_element_type=F32) + sk_ref[...]
        dre = jnp.zeros((1, S5_W), F32)
        dim = jnp.zeros((1, S5_W), F32)
        for r0 in range(0, lp, S5_DA_ROWS):
            rows = min(S5_DA_ROWS, lp - r0)
            first = lax.broadcasted_iota(jnp.int32, (rows, 1), 0) == 0
            prev = hs_ref[r0 - 1:r0, :] if r0 else jnp.zeros((1, 2 * S5_W), F32)
            hr = jnp.where(first, prev[:, :S5_W], pltpu.roll(hs_ref[r0:r0 + rows, :S5_W], 1, 0))
            hi = jnp.where(first, prev[:, S5_W:], pltpu.roll(hs_ref[r0:r0 + rows, S5_W:], 1, 0))
            gr, gi = g_ref[r0:r0 + rows, :S5_W], g_ref[r0:r0 + rows, S5_W:]
            dre = dre + jnp.sum(gr * hr + gi * hi, axis=0, keepdims=True)
            dim = dim + jnp.sum(gi * hr - gr * hi, axis=0, keepdims=True)
        da_ref[:, :S5_W] = dre
        da_ref[:, S5_W:] = dim

    col_blk = pl.BlockSpec((lp, S5_GW), lambda j: (0, j))
    lane_blk = pl.BlockSpec((lp, 2 * S5_W), lambda j: (0, j))
    bb_blk = pl.BlockSpec((S5_GW, 2 * S5_W), lambda j: (j, 0))
    cc_blk = pl.BlockSpec((2 * S5_W, S5_GW), lambda j: (j, 0))
    a_blk = pl.BlockSpec((1, 2 * S5_W), lambda j: (0, j))
    return _pc(body, name=name, grid=(cfg.NB,),
               in_specs=[col_blk, lane_blk, col_blk, bb_blk, cc_blk, a_blk, col_blk],
               out_specs=[col_blk, bb_blk, cc_blk, a_blk],
               out_shape=[jax.ShapeDtypeStruct((lp, ds), F32), jax.ShapeDtypeStruct((ds, 2 * S5_W), F32),
                          jax.ShapeDtypeStruct((nl, S5_GW), F32), jax.ShapeDtypeStruct((1, nl), F32)],
               scratch_shapes=[pltpu.VMEM((lp, 2 * S5_W), F32)],
               compiler_params=pltpu.CompilerParams(dimension_semantics=("parallel",)))(dy, hs, z, bb_band, cc_band, a_l, du_skip)


def _conv_gate(pre, cw, cb):
    return cw[0:1] * pltpu.roll(pre, 2, 0) + cw[1:2] * pltpu.roll(pre, 1, 0) + cw[2:3] * pre + cb


def _conv_fwd(up, cw, cb, *, name):
    lp, fp2 = up.shape
    fp = fp2 // 2
    tc = _tile(fp, 256)
    nb = fp // tc

    def body(pre_ref, val_ref, cw_ref, cb_ref, o_ref):
        gate = _conv_gate(pre_ref[...], cw_ref[...], cb_ref[...])
        o_ref[...] = (jax.nn.silu(gate) * val_ref[...]).astype(BF16)

    return _pc(body, name=name, grid=(nb,),
               in_specs=[pl.BlockSpec((lp, tc), lambda j: (0, j)), pl.BlockSpec((lp, tc), lambda j: (0, nb + j)),
                         pl.BlockSpec((3, tc), lambda j: (0, j)), pl.BlockSpec((1, tc), lambda j: (0, j))],
               out_specs=pl.BlockSpec((lp, tc), lambda j: (0, j)),
               out_shape=jax.ShapeDtypeStruct((lp, fp), BF16),
               compiler_params=pltpu.CompilerParams(dimension_semantics=("parallel",)))(up, up, cw, cb)


def _conv_bwd(up, dact, cw, cb, *, name):
    lp, fp2 = up.shape
    fp = fp2 // 2
    tc = _tile(fp, 256)
    nb = fp // tc

    def body(pre_ref, val_ref, da_ref, cw_ref, cb_ref, dup_ref, dcw_ref, dcb_ref):
        pre, val, da, cwv = pre_ref[...], val_ref[...], da_ref[...].astype(F32), cw_ref[...]
        gate = _conv_gate(pre, cwv, cb_ref[...])
        sg = jax.nn.sigmoid(gate)
        dup_ref[1] = (da * (gate * sg)).astype(BF16)
        dgate = da * val * (sg * (1.0 + gate * (1.0 - sg)))
        dpre = cwv[2:3] * dgate + cwv[1:2] * pltpu.roll(dgate, lp - 1, 0) + cwv[0:1] * pltpu.roll(dgate, lp - 2, 0)
        dup_ref[0] = dpre.astype(BF16)
        dcb_ref[...] = jnp.sum(dgate, axis=0, keepdims=True)
        dcw_ref[0:1, :] = jnp.sum(dgate * pltpu.roll(pre, 2, 0), axis=0, keepdims=True)
        dcw_ref[1:2, :] = jnp.sum(dgate * pltpu.roll(pre, 1, 0), axis=0, keepdims=True)
        dcw_ref[2:3, :] = jnp.sum(dgate * pre, axis=0, keepdims=True)

    return _pc(body, name=name, grid=(nb,),
               in_specs=[pl.BlockSpec((lp, tc), lambda j: (0, j)), pl.BlockSpec((lp, tc), lambda j: (0, nb + j)),
                         pl.BlockSpec((lp, tc), lambda j: (0, j)),
                         pl.BlockSpec((3, tc), lambda j: (0, j)), pl.BlockSpec((1, tc), lambda j: (0, j))],
               out_specs=[pl.BlockSpec((2, lp, tc), lambda j: (0, 0, j)),
                          pl.BlockSpec((3, tc), lambda j: (0, j)), pl.BlockSpec((1, tc), lambda j: (0, j))],
               out_shape=[jax.ShapeDtypeStruct((2, lp, fp), BF16), jax.ShapeDtypeStruct((3, fp), F32),
                          jax.ShapeDtypeStruct((1, fp), F32)],
               compiler_params=pltpu.CompilerParams(dimension_semantics=("parallel",)))(up, up, dact, cw, cb)


def _key_limit(i, tq, lp):
    return min(lp, -(-((i + 1) * tq) // LANE) * LANE)


def _attn_mask(i, tq, nk):
    qrow = i * tq + lax.broadcasted_iota(jnp.int32, (tq, 1), 0)
    krow = lax.broadcasted_iota(jnp.int32, (1, nk), 1)
    return (krow >= PAD) & ((krow // CHUNK) <= (qrow // CHUNK)), qrow >= PAD


def _attn_scores(q, kn, kr, i, tq, scale):
    nt = (((1,), (1,)), ((), ()))
    s = lax.dot_general(q[:, :QK_NOPE], kn, nt, preferred_element_type=F32)
    s = s + lax.dot_general(q[:, QK_NOPE:], kr, nt, preferred_element_type=F32)
    mask, qvalid = _attn_mask(i, tq, kn.shape[0])
    return jnp.where(mask, s * scale, jnp.finfo(F32).min), qvalid


def _per_q_block(nq, fn):
    i = pl.program_id(1)
    for blk in range(nq):
        pl.when(i == blk)(functools.partial(fn, blk))


def _attn_fwd(qx, kv, kr, cfg, *, name):
    lp, h = cfg.LP, cfg.H
    tq = _tile(lp, 272, ROW_ALIGN)
    nq = lp // tq
    scale = 1.0 / math.sqrt(QK_NOPE + QK_ROPE)

    def body(q_ref, kn_ref, v_ref, kr_ref, o_ref, lse_ref):
        def block(blk):
            nk = _key_limit(blk, tq, lp)
            s, qvalid = _attn_scores(q_ref[...], kn_ref[:nk], kr_ref[:nk], blk, tq, scale)
            m = jnp.max(s, axis=-1, keepdims=True)
            p = jnp.exp(s - m)
            l = jnp.sum(p, axis=-1, keepdims=True)
            o = jnp.dot(p.astype(BF16), v_ref[:nk], preferred_element_type=F32) / l
            o_ref[...] = jnp.where(qvalid, o, 0.0)
            lse_ref[...] = m + jnp.log(l)

        _per_q_block(nq, block)

    return _pc(body, name=name, grid=(h, nq),
               in_specs=[pl.BlockSpec((tq, HEAD_SLOT), lambda hh, i: (i, hh)),
                         pl.BlockSpec((lp, QK_NOPE), lambda hh, i: (0, 2 * hh)),
                         pl.BlockSpec((lp, V_HEAD), lambda hh, i: (0, 2 * hh + 1)),
                         pl.BlockSpec((lp, LANE), lambda hh, i: (0, 0))],
               out_specs=[pl.BlockSpec((tq, V_HEAD), lambda hh, i: (i, hh)),
                          pl.BlockSpec((None, tq, 1), lambda hh, i: (hh, i, 0))],
               out_shape=[jax.ShapeDtypeStruct((lp, h * V_HEAD), F32), jax.ShapeDtypeStruct((h, lp, 1), F32)],
               compiler_params=pltpu.CompilerParams(dimension_semantics=("parallel", "parallel")))(qx, kv, kv, kr)


def _attn_bwd(qx, kv, kr, o, lse, do, cfg, *, name):
    lp, h = cfg.LP, cfg.H
    tq = _tile(lp, 272, ROW_ALIGN)
    nq = lp // tq
    scale = 1.0 / math.sqrt(QK_NOPE + QK_ROPE)
    tn_dims = (((0,), (0,)), ((), ()))

    def body(q_ref, kn_ref, v_ref, kr_ref, o_ref, lse_ref, do_ref, dq_ref, dkv_ref, dkr_ref, dkv_acc):
        hh, i = pl.program_id(0), pl.program_id(1)

        @pl.when(i == 0)
        def _():
            dkv_acc[...] = jnp.zeros_like(dkv_acc)

        @pl.when((i == 0) & (hh == 0))
        def _():
            dkr_ref[...] = jnp.zeros_like(dkr_ref)

        def block(blk):
            nk = _key_limit(blk, tq, lp)
            q, kn, v, krv = q_ref[...], kn_ref[:nk], v_ref[:nk], kr_ref[:nk]
            s, qvalid = _attn_scores(q, kn, krv, blk, tq, scale)
            dov = jnp.where(qvalid, do_ref[...], 0.0)
            p = jnp.exp(s - lse_ref[...])
            delta = jnp.sum(dov * o_ref[...], axis=-1, keepdims=True)
            dob = dov.astype(BF16)
            dp = lax.dot_general(dob, v, (((1,), (1,)), ((), ())), preferred_element_type=F32)
            ds = (p * (dp - delta) * scale).astype(BF16)
            dq_ref[:, :QK_NOPE] = jnp.dot(ds, kn, preferred_element_type=F32)
            dq_ref[:, QK_NOPE:] = jnp.dot(ds, krv, preferred_element_type=F32)
            dkv_acc[:nk, :QK_NOPE] += lax.dot_general(ds, q[:, :QK_NOPE], tn_dims, preferred_element_type=F32)
            dkv_acc[:nk, QK_NOPE:] += lax.dot_general(p.astype(BF16), dob, tn_dims, preferred_element_type=F32)
            dkr_ref[:nk, :] += lax.dot_general(ds, q[:, QK_NOPE:], tn_dims, preferred_element_type=F32)

        _per_q_block(nq, block)

        @pl.when(i == nq - 1)
        def _():
            dkv_ref[...] = dkv_acc[...].astype(BF16)

    return _pc(body, name=name, grid=(h, nq),
               in_specs=[pl.BlockSpec((tq, HEAD_SLOT), lambda hh, i: (i, hh)),
                         pl.BlockSpec((lp, QK_NOPE), lambda hh, i: (0, 2 * hh)),
                         pl.BlockSpec((lp, V_HEAD), lambda hh, i: (0, 2 * hh + 1)),
                         pl.BlockSpec((lp, LANE), lambda hh, i: (0, 0)),
                         pl.BlockSpec((tq, V_HEAD), lambda hh, i: (i, hh)),
                         pl.BlockSpec((None, tq, 1), lambda hh, i: (hh, i, 0)),
                         pl.BlockSpec((tq, V_HEAD), lambda hh, i: (i, hh))],
               out_specs=[pl.BlockSpec((tq, HEAD_SLOT), lambda hh, i: (i, hh)),
                          pl.BlockSpec((lp, QK_NOPE + V_HEAD), lambda hh, i: (0, hh)),
                          pl.BlockSpec((lp, LANE), lambda hh, i: (0, 0))],
               out_shape=[jax.ShapeDtypeStruct((lp, h * HEAD_SLOT), F32),
                          jax.ShapeDtypeStruct((lp, h * (QK_NOPE + V_HEAD)), BF16),
                          jax.ShapeDtypeStruct((lp, LANE), F32)],
               scratch_shapes=[pltpu.VMEM((lp, QK_NOPE + V_HEAD), F32)],
               compiler_params=pltpu.CompilerParams(dimension_semantics=("arbitrary", "arbitrary")))(qx, kv, kv, kr, o, lse, do)


def _rot_half(x):
    lane = lax.broadcasted_iota(jnp.int32, x.shape, 1)
    half = QK_ROPE // 2
    return jnp.where(lane < half, -pltpu.roll(x, LANE - half, 1), pltpu.roll(x, half, 1))


def _rope(x, cos, sin):
    return x * cos + _rot_half(x) * sin


def _unrope(dy, cos, sin):
    return dy * cos - _rot_half(dy * sin)


def _rope_heads(fn, h):
    def apply(rid, q, cos, sin):
        parts = []
        for hh in range(h):
            parts.append(q[:, hh * HEAD_SLOT: hh * HEAD_SLOT + QK_NOPE])
            parts.append(fn(q[:, hh * HEAD_SLOT + QK_NOPE: (hh + 1) * HEAD_SLOT], cos, sin))
        return jnp.concatenate(parts, axis=1)
    return apply


ANY = pl.BlockSpec(memory_space=pl.ANY)


def _place():
    x, y, c = lax.axis_index("x"), lax.axis_index("y"), lax.axis_index("c")
    chips = [(1 - x, y), (x, 1 - y), (1 - x, 1 - y)]
    return x, y, c, chips


def _rcopy(src, dst, send_sem, recv_sem, dev):
    return pltpu.make_async_remote_copy(src_ref=src, dst_ref=dst, send_sem=send_sem, recv_sem=recv_sem,
                                        device_id=dev, device_id_type=MESH)


def _place_shard(shard, dtype, *, name):
    r, cols = shard.shape
    tm = _row_tile(r, cols)
    nblk = r // tm
    me = (2 * lax.axis_index("x") + lax.axis_index("y")).astype(jnp.int32).reshape(1)

    def body(me_ref, s_ref, o_ref):
        o_ref[...] = s_ref[...].astype(dtype)

    return _pc(body, name=name,
               grid_spec=pltpu.PrefetchScalarGridSpec(
                   num_scalar_prefetch=1, grid=(nblk,),
                   in_specs=[pl.BlockSpec((tm, cols), lambda i, mr: (i, 0))],
                   out_specs=pl.BlockSpec((tm, cols), lambda i, mr: (mr[0] * nblk + i, 0))),
               out_shape=jax.ShapeDtypeStruct((4 * r, cols), dtype),
               compiler_params=pltpu.CompilerParams(dimension_semantics=("arbitrary",)))(me, shard)


def _allgather(fulls, *, name):
    n = len(fulls)

    def body(*refs):
        outs = refs[n:2 * n]
        send_sems, recv_sems = refs[2 * n:]
        x, y, c, chips = _place()
        sib = (x, y, 1 - c)
        me = 2 * x + y

        def rows(t, s, half):
            hrows = outs[t].shape[0] // 8
            return outs[t].at[pl.ds((2 * s + half) * hrows, hrows)]

        sent = []
        for t in range(n):
            for j, (cx, cy) in enumerate(chips):
                cp = _rcopy(rows(t, me, c), rows(t, me, c), send_sems.at[6 * t + j], recv_sems.at[6 * t + j], (cx, cy, c))
                cp.start()
                sent.append(cp)
        for t in range(n):
            for j, (cx, cy) in enumerate(chips):
                landed = rows(t, 2 * cx + cy, c)
                _rcopy(landed, landed, send_sems.at[6 * t + j], recv_sems.at[6 * t + j], (cx, cy, c)).wait_recv()
                cp = _rcopy(landed, landed, send_sems.at[6 * t + 3 + j], recv_sems.at[6 * t + 3 + j], sib)
                cp.start()
                sent.append(cp)
        for t in range(n):
            for j, (cx, cy) in enumerate(chips):
                other = rows(t, 2 * cx + cy, 1 - c)
                _rcopy(other, other, send_sems.at[6 * t + 3 + j], recv_sems.at[6 * t + 3 + j], sib).wait_recv()
        for cp in sent:
            cp.wait_send()

    return _pc(body, name=name, in_specs=[ANY] * n, out_specs=[ANY] * n,
               out_shape=[jax.ShapeDtypeStruct(f.shape, f.dtype) for f in fulls],
               input_output_aliases={t: t for t in range(n)},
               scratch_shapes=[pltpu.SemaphoreType.DMA((6 * n,)), pltpu.SemaphoreType.DMA((6 * n,))])(*fulls)


HBM = pl.BlockSpec(memory_space=pltpu.HBM)
SEM = pl.BlockSpec(memory_space=pltpu.SEMAPHORE)
EFFECT = pltpu.SideEffectType.DATAFLOW_SIDE_EFFECTING
TOKEN = jax.ShapeDtypeStruct((8, LANE), F32)


def _in_hbm(a):
    return pltpu.with_memory_space_constraint(a, pltpu.HBM)


def _half_rows(ref, s, half):
    hrows = ref.shape[0] // 8
    return ref.at[pl.ds((2 * s + half) * hrows, hrows)]


def _split_start(bufs, copies, n_copies, *, name, before=None):
    n = len(bufs)
    extra = [] if before is None else [before]

    def body(*refs):
        send_sems, recv_sems, token = refs[n + len(extra)], refs[n + len(extra) + 1], refs[-1]
        for k, (src, dst, dev) in enumerate(copies(refs[:n])):
            _rcopy(src, dst, send_sems.at[k], recv_sems.at[k], dev).start()
        token[...] = jnp.zeros_like(token)

    res = _pc(body, name=name, in_specs=[HBM] * n + [ANY] * len(extra),
              out_specs=[SEM, SEM] + [HBM] * n + [pl.BlockSpec(memory_space=pltpu.VMEM)],
              out_shape=[pltpu.SemaphoreType.DMA((n_copies,)), pltpu.SemaphoreType.DMA((n_copies,))]
              + [pltpu.HBM(b.shape, b.dtype) for b in bufs] + [TOKEN],
              input_output_aliases={t: 2 + t for t in range(n)},
              compiler_params=pltpu.CompilerParams(has_side_effects=EFFECT))(*[_in_hbm(b) for b in bufs], *extra)
    return res[0], res[1], list(res[2:2 + n]), res[-1]


def _split_wait(send_sems, recv_sems, bufs, copies, after, *, name):
    n = len(bufs)

    def body(*refs):
        send_ref, recv_ref = refs[n], refs[n + 1]
        for k, (src, dst, dev) in enumerate(copies(refs[:n])):
            cp = _rcopy(src, dst, send_ref.at[k], recv_ref.at[k], dev)
            cp.wait_send()
            cp.wait_recv()

    return _pc(body, name=name, in_specs=[HBM] * n + [SEM, SEM, ANY], out_specs=[HBM] * n,
               out_shape=[pltpu.HBM(b.shape, b.dtype) for b in bufs],
               input_output_aliases={t: t for t in range(n)},
               compiler_params=pltpu.CompilerParams(has_side_effects=EFFECT))(*bufs, send_sems, recv_sems, after)


def _allgather_ici_copies(refs):
    x, y, c, chips = _place()
    return [(_half_rows(r, 2 * x + y, c), _half_rows(r, 2 * x + y, c), (cx, cy, c)) for r in refs for cx, cy in chips]


def _rs_chips_copies(refs):
    x, y, c, chips = _place()
    n = len(refs) // 2
    return [(refs[t].at[2 * cx + cy], refs[n + t].at[j], (cx, cy, c)) for t in range(n) for j, (cx, cy) in enumerate(chips)]


def _rs_sibling_copies(refs):
    x, y, c, _ = _place()
    n = len(refs) // 2
    out = []
    for t in range(n):
        h = refs[t].shape[0] // 8
        out += [(refs[t].at[pl.ds((2 * s + 1 - c) * h, h)], refs[n + t].at[s], (x, y, 1 - c)) for s in range(4)]
    return out


def _allgather_forward(fulls, *, name):
    n = len(fulls)

    def body(*refs):
        outs = refs[n:2 * n]
        send_sems, recv_sems = refs[2 * n:]
        x, y, c, chips = _place()
        sent = []
        for t in range(n):
            for j, (cx, cy) in enumerate(chips):
                landed = _half_rows(outs[t], 2 * cx + cy, c)
                cp = _rcopy(landed, landed, send_sems.at[3 * t + j], recv_sems.at[3 * t + j], (x, y, 1 - c))
                cp.start()
                sent.append(cp)
        for t in range(n):
            for j, (cx, cy) in enumerate(chips):
                other = _half_rows(outs[t], 2 * cx + cy, 1 - c)
                _rcopy(other, other, send_sems.at[3 * t + j], recv_sems.at[3 * t + j], (x, y, 1 - c)).wait_recv()
        for cp in sent:
            cp.wait_send()

    return _pc(body, name=name, in_specs=[ANY] * n, out_specs=[ANY] * n,
               out_shape=[jax.ShapeDtypeStruct(f.shape, f.dtype) for f in fulls],
               input_output_aliases={t: t for t in range(n)},
               scratch_shapes=[pltpu.SemaphoreType.DMA((3 * n,)), pltpu.SemaphoreType.DMA((3 * n,))])(*fulls)


def _rs_sibling(grads, *, name):
    n = len(grads)

    def body(*refs):
        ins, outs = refs[:n], refs[n:2 * n]
        send_sems, recv_sems = refs[2 * n:]
        x, y, c, _ = _place()
        cps = []
        for t in range(n):
            h = ins[t].shape[0] // 8
            for s in range(4):
                cp = _rcopy(ins[t].at[pl.ds((2 * s + 1 - c) * h, h)], outs[t].at[s], send_sems.at[4 * t + s],
                            recv_sems.at[4 * t + s], (x, y, 1 - c))
                cp.start()
                cps.append(cp)
        for cp in cps:
            cp.wait()

    return _pc(body, name=name, in_specs=[ANY] * n, out_specs=[ANY] * n,
               out_shape=[jax.ShapeDtypeStruct((4, g.shape[0] // 8, g.shape[1]), g.dtype) for g in grads],
               scratch_shapes=[pltpu.SemaphoreType.DMA((4 * n,)), pltpu.SemaphoreType.DMA((4 * n,))])(*grads)


def _rs_chips(sends, *, name):
    n = len(sends)

    def body(*refs):
        s_refs, b_refs = refs[:n], refs[n:2 * n]
        send_sems, recv_sems = refs[2 * n:]
        x, y, c, chips = _place()
        cps = []
        for t in range(n):
            for j, (cx, cy) in enumerate(chips):
                cp = _rcopy(s_refs[t].at[2 * cx + cy], b_refs[t].at[j], send_sems.at[3 * t + j], recv_sems.at[3 * t + j],
                            (cx, cy, c))
                cp.start()
                cps.append(cp)
        for cp in cps:
            cp.wait()

    return _pc(body, name=name, in_specs=[ANY] * n, out_specs=[ANY] * n,
               out_shape=[jax.ShapeDtypeStruct((3,) + s.shape[1:], s.dtype) for s in sends],
               scratch_shapes=[pltpu.SemaphoreType.DMA((3 * n,)), pltpu.SemaphoreType.DMA((3 * n,))])(*sends)


def _rs_final(fulls, *, name):
    n = len(fulls)

    def body(*refs):
        outs = refs[n:2 * n]
        send_sems, recv_sems = refs[2 * n:]
        x, y, c, _ = _place()
        cps = []
        for t in range(n):
            cp = _rcopy(outs[t].at[c], outs[t].at[c], send_sems.at[t], recv_sems.at[t], (x, y, 1 - c))
            cp.start()
            cps.append(cp)
        for cp in cps:
            cp.wait()

    return _pc(body, name=name, in_specs=[ANY] * n, out_specs=[ANY] * n,
               out_shape=[jax.ShapeDtypeStruct(f.shape, f.dtype) for f in fulls],
               input_output_aliases={t: t for t in range(n)},
               scratch_shapes=[pltpu.SemaphoreType.DMA((n,)), pltpu.SemaphoreType.DMA((n,))])(*fulls)


def _add_halves(g, a, send_dtype, *, name):
    _, h, cols = a.shape
    th = _row_tile(h, cols)
    g4 = g.reshape(4, 2, h, cols)
    c = lax.axis_index("c").astype(jnp.int32).reshape(1)

    def body(c_ref, g_ref, a_ref, p_ref, s_ref):
        v = g_ref[...] + a_ref[...]
        p_ref[...] = v
        s_ref[...] = v.astype(send_dtype)

    return _pc(body, name=name,
               grid_spec=pltpu.PrefetchScalarGridSpec(
                   num_scalar_prefetch=1, grid=(4, h // th),
                   in_specs=[pl.BlockSpec((None, None, th, cols), lambda s, i, cr: (s, cr[0], i, 0)),
                             pl.BlockSpec((None, th, cols), lambda s, i, cr: (s, i, 0))],
                   out_specs=[pl.BlockSpec((None, th, cols), lambda s, i, cr: (s, i, 0))] * 2),
               out_shape=[jax.ShapeDtypeStruct(a.shape, F32), jax.ShapeDtypeStruct(a.shape, send_dtype)],
               compiler_params=pltpu.CompilerParams(dimension_semantics=("arbitrary", "arbitrary")))(c, g4, a)


def _add_chips(p, b, *, name):
    _, h, cols = p.shape
    th = _row_tile(h, cols)
    idx = jnp.stack([2 * lax.axis_index("x") + lax.axis_index("y"), lax.axis_index("c")]).astype(jnp.int32)

    def body(idx_ref, p_ref, b_ref, r_ref):
        r_ref[...] = ((p_ref[...] + b_ref[0].astype(F32)) + b_ref[1].astype(F32)) + b_ref[2].astype(F32)

    return _pc(body, name=name,
               grid_spec=pltpu.PrefetchScalarGridSpec(
                   num_scalar_prefetch=1, grid=(h // th,),
                   in_specs=[pl.BlockSpec((None, th, cols), lambda i, ir: (ir[0], i, 0)),
                             pl.BlockSpec((3, th, cols), lambda i, ir: (0, i, 0))],
                   out_specs=pl.BlockSpec((None, th, cols), lambda i, ir: (ir[1], i, 0))),
               out_shape=jax.ShapeDtypeStruct((2, h, cols), F32),
               compiler_params=pltpu.CompilerParams(dimension_semantics=("arbitrary",)))(idx, p, b)


def _add_halves_all(grads, recv, send_dtypes, tag):
    parts, sends = [], []
    for t, (g, a) in enumerate(zip(grads, recv)):
        p, s = _add_halves(g, a, send_dtypes[t], name=f"rs_add_halves_{tag}{t}")
        parts.append(p)
        sends.append(s)
    return parts, sends


def _rs_finish(parts, others, tag):
    halves = [_add_chips(p, b, name=f"rs_add_chips_{tag}{t}") for t, (p, b) in enumerate(zip(parts, others))]
    full = _rs_final(halves, name=f"rs_final_{tag}")
    return [f.reshape(-1, f.shape[-1]) for f in full]


def _s5_discretize(lam_re, lam_im, log_dt, b_re, b_im):
    lam = lax.complex(lam_re, lam_im)
    dt = jnp.exp(log_dt)[:, None]
    lam_bar = jnp.exp(lam * dt)
    b_bar = ((lam_bar - 1.0) / lam)[..., None] * lax.complex(b_re, b_im)
    return jnp.real(lam_bar), jnp.imag(lam_bar), jnp.real(b_bar), jnp.imag(b_bar)


def _lanes_from_gp(re, im, cfg):
    v = jnp.stack([re, im]).reshape(2, cfg.NB, GROUPS_PER_BLOCK, SSM_STATE)
    return jnp.transpose(v, (1, 0, 2, 3)).reshape(1, cfg.NL)


def _gp_from_lanes(v, cfg):
    v = jnp.transpose(v.reshape(cfg.NB, 2, GROUPS_PER_BLOCK, SSM_STATE), (1, 0, 2, 3)).reshape(2, cfg.G, SSM_STATE)
    return v[0], v[1]


def _bb_band(bb_re, bb_im, cfg):
    eye = jnp.eye(GROUPS_PER_BLOCK, dtype=F32)
    bb = jnp.stack([bb_re, bb_im]).reshape(2, cfg.NB, GROUPS_PER_BLOCK, SSM_STATE, SSM_GROUP)
    return jnp.einsum('rjgpc,gh->jgcrhp', bb, eye).reshape(cfg.DS, 2 * GROUPS_PER_BLOCK * SSM_STATE)


def _bb_from_band(m, cfg):
    eye = jnp.eye(GROUPS_PER_BLOCK, dtype=F32)
    m = m.reshape(cfg.NB, GROUPS_PER_BLOCK, SSM_GROUP, 2, GROUPS_PER_BLOCK, SSM_STATE)
    v = jnp.einsum('jgcrhp,gh->rjgpc', m, eye).reshape(2, cfg.G, SSM_STATE, SSM_GROUP)
    return v[0], v[1]


def _cc_band(c_re, c_im, cfg):
    eye = jnp.eye(GROUPS_PER_BLOCK, dtype=F32)
    cc = jnp.stack([c_re, -c_im]).reshape(2, cfg.NB, GROUPS_PER_BLOCK, SSM_GROUP, SSM_STATE)
    return jnp.einsum('rjgcp,gh->jrhpgc', cc, eye).reshape(cfg.NL, GROUPS_PER_BLOCK * SSM_GROUP)


def _cc_from_band(m, cfg):
    eye = jnp.eye(GROUPS_PER_BLOCK, dtype=F32)
    m = m.reshape(cfg.NB, 2, GROUPS_PER_BLOCK, SSM_STATE, GROUPS_PER_BLOCK, SSM_GROUP)
    v = jnp.einsum('jrhpgc,gh->rjgcp', m, eye).reshape(2, cfg.G, SSM_GROUP, SSM_STATE)
    return v[0], -v[1]


PACK_COLS = 512
PACK_ROW_ALIGN = 64


def _pack(arrs):
    flat = jnp.concatenate([a.reshape(-1).astype(F32) for a in arrs])
    unit = PACK_COLS * PACK_ROW_ALIGN
    total = -(-flat.shape[0] // unit) * unit
    return jnp.pad(flat, (0, total - flat.shape[0])).reshape(-1, PACK_COLS)


def _unpack(p, shapes):
    flat = p.reshape(-1)
    out, off = [], 0
    for shp in shapes:
        size = math.prod(shp)
        out.append(flat[off:off + size].reshape(shp))
        off += size
    return out


def _adamw(w, g, m, v, *, name, emit_grad=False):
    c1 = 1.0 / (1.0 - ADAM_B1 ** ADAM_STEP)
    c2 = 1.0 / (1.0 - ADAM_B2 ** ADAM_STEP)

    def fn(rid, wv, gv, mv, vv):
        mn = ADAM_B1 * mv + (1.0 - ADAM_B1) * gv
        vn = ADAM_B2 * vv + (1.0 - ADAM_B2) * (gv * gv)
        delta = -ADAM_LR * ((mn * c1) / (jnp.sqrt(vn * c2) + ADAM_EPS) + ADAM_WD * wv)
        return (gv, delta, mn, vn) if emit_grad else (delta, mn, vn)

    cols = w.shape[1]
    return _ew(fn, [w, g, m, v], [], [(cols, F32)] * (4 if emit_grad else 3), name=name)


def _to_comm_layout(name, w, cfg):
    w = w[0]
    if name == 'w_in':
        return jnp.pad(w, ((0, 0), (0, cfg.DINP - cfg.DIN)))
    if name == 'w_q_b':
        hs = w.shape[1] // (QK_NOPE + QK_ROPE)
        wt = w.T.reshape(hs, QK_NOPE + QK_ROPE, cfg.QL)
        return jnp.pad(wt, ((0, 0), (0, HEAD_SLOT - QK_NOPE - QK_ROPE), (0, 0))).reshape(hs * HEAD_SLOT, cfg.QL)
    if name == 'w_kv_b':
        return w.T
    if name == 'w_up':
        wt = w.T.reshape(2, cfg.F // 4, cfg.D)
        return jnp.pad(wt, ((0, 0), (0, cfg.FQ - cfg.F // 4), (0, 0))).reshape(2 * cfg.FQ, cfg.D)
    if name == 'w_down':
        return jnp.pad(w, ((0, cfg.FQ - cfg.F // 4), (0, 0)))
    return w


def _from_comm_layout(name, g, cfg):
    if name == 'w_in':
        g = g[:, :cfg.DIN]
    elif name == 'w_q_b':
        hs = g.shape[0] // HEAD_SLOT
        g = g.reshape(hs, HEAD_SLOT, cfg.QL)[:, :QK_NOPE + QK_ROPE].reshape(hs * (QK_NOPE + QK_ROPE), cfg.QL).T
    elif name == 'w_kv_b':
        g = g.T
    elif name == 'w_up':
        g = g.reshape(2, cfg.FQ, cfg.D)[:, :cfg.F // 4].reshape(cfg.F // 2, cfg.D).T
    elif name == 'w_down':
        g = g[:cfg.F // 4]
    return g[None]


def _ff_pad(v, cfg):
    k = v.shape[0]
    return jnp.pad(v.reshape(k, 4, cfg.F // 4), ((0, 0), (0, 0), (0, cfg.FQ - cfg.F // 4))).reshape(k, cfg.FP)


def _ff_unpad(v, cfg):
    k = v.shape[0]
    return v.reshape(k, 4, cfg.FQ)[:, :, :cfg.F // 4].reshape(k, cfg.F)


def _step(cfg, w, m, v, x, loss_target):
    lp, d, ds, nl = cfg.LP, cfg.D, cfg.DS, cfg.NL
    xi, yi = lax.axis_index("x"), lax.axis_index("y")
    me = 2 * xi + yi

    placed = [_place_shard(_to_comm_layout(n, w[n], cfg), BF16, name=f"place_{n}") for n in BIG]
    conv_w_shard = jnp.pad(w['conv_w'][0], ((0, ROW_ALIGN - 3), (0, cfg.FQ - cfg.F // 4)))
    placed += [_place_shard(w['meta_tokens'], F32, name="place_meta"), _place_shard(conv_w_shard, F32, name="place_conv_w")]
    w_in, meta_full = _allgather([placed[0], placed[7]], name="allgather_first")
    meta = jnp.transpose(meta_full.reshape(4, N_META, d // 4), (1, 0, 2)).reshape(N_META, d)
    conv_b = _ff_pad(w['conv_b'], cfg)
    mid = placed[1:5] + [placed[8]]
    mid_send, mid_recv, mid_flying, mid_token = _split_start(mid, _allgather_ici_copies, 3 * len(mid), before=meta_full,
                                                             name="allgather_mid_start")
    ffn_send, ffn_recv, ffn_flying, ffn_token = _split_start(placed[5:7], _allgather_ici_copies, 6, before=mid_token,
                                                             name="allgather_ffn_start")
    mix_norm = w['mix_norm'] + (mid_token[0:1, 0:1] + ffn_token[0:1, 0:1])

    pos = (jnp.arange(lp, dtype=jnp.int32) - PAD).astype(F32)
    inv_freq = 1.0 / (ROPE_BASE ** (jnp.arange(0, QK_ROPE, 2, dtype=F32) / QK_ROPE))
    ang = pos[:, None] * inv_freq[None, :]
    zpad = jnp.zeros((lp, LANE - QK_ROPE), F32)
    cos_t = jnp.concatenate([jnp.cos(ang), jnp.cos(ang), zpad], axis=1)
    sin_t = jnp.concatenate([jnp.sin(ang), jnp.sin(ang), zpad], axis=1)

    s5_in = (w['lam_re'][0], w['lam_im'][0], w['log_dt'][0], w['b_re'][0], w['b_im'][0])
    (a_re, a_im, bb_re, bb_im), s5_vjp = jax.vjp(_s5_discretize, *s5_in)
    a_l = _lanes_from_gp(a_re, a_im, cfg)
    bb_band = _bb_band(bb_re, bb_im, cfg).astype(BF16)
    cc_band = _cc_band(w['c_re'][0], w['c_im'][0], cfg).astype(BF16)
    d_skip, b_glu = w['d_skip'], w['b_glu']

    h0 = jnp.concatenate([jnp.zeros((PAD, d), F32), meta, x[0]], axis=0)
    xn = _rms_fwd(h0, mix_norm, name="rms_mix")
    z = _mm(xn, w_in, name="mm_in", tn=_tile(cfg.DINP, 640))
    u = (z, ds, 0)
    q_a = (z, cfg.QL, ds // cfg.QL)
    kv_a = (z, cfg.KVL, (ds + cfg.QL) // cfg.KVL)
    k_pe = (z, LANE, (ds + cfg.QL + cfg.KVL) // LANE)

    hs, yc = _s5_fwd(z, bb_band, cc_band, a_l, cfg, name="s5_fwd")

    def s5_y(ycv, uv, dk):
        return ycv + dk * uv

    gl = _ew(lambda rid, ycv, uv, dk: jax.nn.gelu(s5_y(ycv, uv, dk)), [yc, u], [d_skip], [(ds, BF16)], name="s5_gelu")[0]
    mid_landed = _split_wait(mid_send, mid_recv, mid_flying, _allgather_ici_copies, gl, name="allgather_mid_wait")
    w_glu, w_qt, w_kvt, w_out, conv_full = _allgather_forward(mid_landed, name="allgather_mid_forward")
    conv_w = jnp.transpose(conv_full.reshape(4, ROW_ALIGN, cfg.FQ)[:, :3], (1, 0, 2)).reshape(3, cfg.FP)
    tg = _mm(gl, w_glu, name="mm_glu")
    ya = _ew(lambda rid, ycv, uv, tv, dk, bg: jax.nn.gelu(s5_y(ycv, uv, dk)) * jax.nn.sigmoid(tv + bg),
             [yc, u, tg], [d_skip, b_glu], [(ds, F32)], name="s5_glu")[0]

    qn = _rms_fwd(q_a, w['q_a_norm'], name="rms_q")
    kvn = _rms_fwd(kv_a, w['kv_a_norm'], name="rms_kv")
    q_raw = _mm(qn, w_qt, tb=True, name="mm_q")
    qx = _ew(_rope_heads(_rope, cfg.H), [q_raw, cos_t, sin_t], [], [(cfg.H * HEAD_SLOT, BF16)], name="rope_q")[0]
    kv = _mm(kvn, w_kvt, tb=True, out_dtype=BF16, name="mm_kv")
    kr = _ew(lambda rid, kp, cs, sn: _rope(kp, cs, sn), [k_pe, cos_t, sin_t], [], [(LANE, BF16)], name="rope_k")[0]
    o, lse = _attn_fwd(qx, kv, kr, cfg, name="attn_fwd")

    def norm2(rid, yav, ov, gs, ga):
        return jnp.concatenate([_rms_parts(yav, gs)[0] * gs, _rms_parts(ov, ga)[0] * ga], axis=1)

    yn = _ew(norm2, [ya, o], [w['out_norm_ssm'], w['out_norm_attn']], [(cfg.DMIX, BF16)], name="rms_out")[0]
    h1 = _mm(yn, w_out, res=h0, name="mm_out")
    xn2 = _rms_fwd(h1, w['ffn_norm'], name="rms_ffn")
    ffn_landed = _split_wait(ffn_send, ffn_recv, ffn_flying, _allgather_ici_copies, xn2, name="allgather_ffn_wait")
    w_upt, w_down = _allgather_forward(ffn_landed, name="allgather_ffn_forward")
    up = _mm(xn2, w_upt, tb=True, name="mm_up")
    act = _conv_fwd(up, conv_w, conv_b, name="conv_fwd")
    h2 = _mm(act, w_down, res=h1, tm=_tile(lp, 544, ROW_ALIGN), name="mm_down")

    g_final = w['final_norm'].reshape(1, d)

    def head(rid, hv, tv, gv):
        xhat, r = _rms_parts(hv, gv)
        valid = rid >= PAD + N_META
        diff = jnp.where(valid, xhat * gv - tv, 0.0)
        dout = diff * (1.0 / d)
        dxhat = dout * gv
        dx = r * (dxhat - xhat * jnp.mean(dxhat * xhat, axis=-1, keepdims=True))
        return dx, dx, dout * xhat, 0.5 * diff * dout

    dh2, dh2_b, dg_final, loss_cols = _ew(head, [h2, (loss_target[0], d, 0, SKIP)], [g_final], [(d, F32), (d, BF16)], [d, d],
                                          tm=PAD + N_META, name="loss_head")
    loss = lax.psum(jnp.sum(loss_cols), ("x", "y", "c"))

    dact = _mm(dh2_b, w_down, tb=True, out_dtype=BF16, name="mm_dact")
    dw_down = _mm(act, dh2_b, ta=True, tn=d, tm=512, name="mm_dw_down")
    dup, dconv_w, dconv_b = _conv_bwd(up, dact, conv_w, conv_b, name="conv_bwd")
    tk_up, tm_up = _tile(cfg.FP, 1408), _tile(cfg.FP, 512)
    dxn2 = _mm(dup, w_upt, dims=(lp, d, 2 * cfg.FP), tk=tk_up, a_lead=True, name="mm_dxn2",
               a_idx=lambda i, j, k: (k // (cfg.FP // tk_up), i, k % (cfg.FP // tk_up)))
    dw_upt = _mm(dup, xn2, ta=True, dims=(2 * cfg.FP, d, lp), tn=d, tm=tm_up, a_lead=True, name="mm_dw_up",
                 a_idx=lambda i, j, k: (i // (cfg.FP // tm_up), 0, i % (cfg.FP // tm_up)))
    dh1, dh1_b, dg_ffn = _rms_bwd(h1, w['ffn_norm'], dxn2, res=dh2, mask=True, with_bf16=True, name="rms_ffn_bwd")

    dyn = _mm(dh1_b, w_out, tb=True, name="mm_dyn")
    dw_out = _mm(yn, dh1_b, ta=True, tn=d, tm=512, name="mm_dw_out")
    early = [dw_upt, dw_down, dw_out]
    early_lands = [lax.empty((4, g.shape[0] // 8, g.shape[1]), F32) for g in early]
    sb_send, sb_recv, sb_flying, sb_token = _split_start(early + early_lands, _rs_sibling_copies, 4 * len(early),
                                                         name="rs_sibling_early_start")
    dya, dg_ssm = _rms_bwd(ya, w['out_norm_ssm'] + sb_token[0:1, 0:1], (dyn, ds, 0), name="rms_ssm_bwd")
    do, dg_attn = _rms_bwd(o, w['out_norm_attn'], (dyn, cfg.DATTN, ds // cfg.DATTN), name="rms_attn_bwd")

    dqx, dkv, dkr = _attn_bwd(qx, kv, kr, o, lse, do, cfg, name="attn_bwd")
    sb_done = _split_wait(sb_send, sb_recv, sb_flying, _rs_sibling_copies, dqx, name="rs_sibling_early_wait")
    early_parts, early_sends = _add_halves_all(sb_done[:len(early)], sb_done[len(early):], [BF16] * len(early), "early")
    chip_lands = [lax.empty((3,) + s.shape[1:], s.dtype) for s in early_sends]
    ch_send, ch_recv, ch_flying, ch_token = _split_start(early_sends + chip_lands, _rs_chips_copies, 3 * len(early),
                                                         name="rs_chips_early_start")
    dq_raw = _ew(_rope_heads(_unrope, cfg.H), [dqx, cos_t, sin_t], [], [(cfg.H * HEAD_SLOT, BF16)], name="unrope_q")[0]
    dk_pe = _ew(lambda rid, dk, cs, sn: _unrope(dk, cs, sn), [dkr, cos_t, sin_t], [], [(LANE, F32)], name="unrope_k")[0]
    dqn = _mm(dq_raw, w_qt, name="mm_dqn")
    dw_qt = _mm(dq_raw, qn, ta=True, tm=512, name="mm_dw_q")
    dkvn = _mm(dkv, w_kvt, name="mm_dkvn")
    dw_kvt = _mm(dkv, kvn, ta=True, tm=512, name="mm_dw_kv")
    dq_a, dg_q = _rms_bwd(q_a, w['q_a_norm'] + ch_token[0:1, 0:1], dqn, name="rms_q_bwd")
    dkv_a, dg_kv = _rms_bwd(kv_a, w['kv_a_norm'], dkvn, name="rms_kv_bwd")

    def glu_bwd(rid, ycv, uv, tv, dyav, dk, bg):
        gelu = jax.nn.gelu(s5_y(ycv, uv, dk))
        sg = jax.nn.sigmoid(tv + bg)
        dt = dyav * gelu * sg * (1.0 - sg)
        return dt, dyav * sg, dt

    dt_b, dgl1, db_glu = _ew(glu_bwd, [yc, u, tg, dya], [d_skip, b_glu], [(ds, BF16), (ds, F32)], [ds], name="s5_glu_bwd")
    dgl = _mm(dt_b, w_glu, tb=True, res=dgl1, name="mm_dgl")
    dw_glu = _mm(gl, dt_b, ta=True, tm=512, name="mm_dw_glu")

    def gelu_bwd(rid, ycv, uv, dglv, dk):
        _, vjp = jax.vjp(jax.nn.gelu, s5_y(ycv, uv, dk))
        dy = vjp(dglv)[0]
        return dy, dy * dk, dy * uv

    dy_b, du_skip, dd_skip = _ew(gelu_bwd, [yc, u, dgl], [d_skip], [(ds, BF16), (ds, F32)], [ds], name="s5_gelu_bwd")
    du, dbb_band, dcc_band, da_l = _s5_bwd(dy_b, hs, z, bb_band, cc_band, a_l, du_skip, cfg, name="s5_bwd")

    dz = jnp.concatenate([du, dq_a, dkv_a, dk_pe], axis=1).astype(BF16)
    dxn = _mm(dz, w_in, tb=True, name="mm_dxn")
    dw_in = _mm(xn, dz, ta=True, tm=512, tn=_tile(cfg.DINP, 1024), name="mm_dw_in")
    def mix_bwd(rid, xv, dyv, resv, gv):
        dx, dg = _rms_bwd_block(xv, gv, dyv)
        dx = dx + resv
        return dx, dx, dg

    grad_x, dh0_head, dg_mix = _ew(mix_bwd, [h0, dxn, dh1], [mix_norm], [(d, F32, SKIP), (d, F32, FIRST)], [d],
                                   tm=PAD + N_META, name="rms_mix_bwd")
    grad_x = grad_x[None]

    da_re, da_im = _gp_from_lanes(da_l, cfg)
    dbb_re, dbb_im = _bb_from_band(dbb_band, cfg)
    dlam_re, dlam_im, dlog_dt, db_re, db_im = s5_vjp((da_re, da_im, dbb_re, dbb_im))
    dc_re, dc_im = _cc_from_band(dcc_band, cfg)
    local_small = {
        'meta_tokens': dh0_head[PAD:], 'mix_norm': dg_mix, 'lam_re': dlam_re, 'lam_im': dlam_im, 'log_dt': dlog_dt,
        'b_re': db_re, 'b_im': db_im, 'c_re': dc_re, 'c_im': dc_im, 'd_skip': dd_skip, 'b_glu': db_glu, 'q_a_norm': dg_q,
        'kv_a_norm': dg_kv, 'out_norm_ssm': dg_ssm, 'out_norm_attn': dg_attn, 'ffn_norm': dg_ffn,
        'conv_w': _ff_unpad(dconv_w, cfg), 'conv_b': _ff_unpad(dconv_b, cfg), 'final_norm': dg_final,
    }
    small_shapes = [local_small[n].shape for n in SMALL]

    rest_local = [dw_in, dw_glu, dw_qt, dw_kvt, _pack([local_small[n] for n in SMALL])]
    rest_recv = _rs_sibling(rest_local, name="rs_sibling_rest")
    rest_parts, rest_sends = _add_halves_all(rest_local, rest_recv, [BF16] * 4 + [F32], "rest")
    ch_done = _split_wait(ch_send, ch_recv, ch_flying, _rs_chips_copies, rest_sends[0], name="rs_chips_early_wait")
    rest_others = _rs_chips(rest_sends, name="rs_chips_rest")
    red = _rs_finish(rest_parts + early_parts, list(rest_others) + list(ch_done[len(early):]), "grads")
    reduced = [red[0], red[1], red[2], red[3], red[7], red[5], red[6], red[4]]
    small_full = _allgather([_place_shard(reduced[7], F32, name="place_small")], name="allgather_small")[0]
    small_sum = dict(zip(SMALL, _unpack(small_full, small_shapes)))

    padded_rows = ('w_down',)
    grads = {n: _from_comm_layout(n, g, cfg) for n, g in zip(BIG, reduced[:7]) if n not in padded_rows}
    for n in SMALL:
        g = small_sum[n]
        if n == 'meta_tokens':
            g = lax.dynamic_slice_in_dim(g, me * (d // 4), d // 4, axis=1)
        elif n == 'conv_w':
            g = lax.dynamic_slice_in_dim(g, me * (cfg.F // 4), cfg.F // 4, axis=1)[None]
        else:
            g = g.reshape(w[n].shape)
        grads[n] = g

    delta, new_m, new_v = {}, {}, {}
    for n, red in zip(BIG, reduced[:7]):
        shp = w[n].shape
        w2, m2, v2 = [t.reshape(shp[-2], shp[-1]) for t in (w[n], m[n], v[n])]
        if n in padded_rows:
            g2, dl, mn, vn = _adamw(w2, red, m2, v2, emit_grad=True, name=f"adamw_{n}")
            grads[n] = g2.reshape(shp)
        else:
            dl, mn, vn = _adamw(w2, grads[n].reshape(shp[-2], shp[-1]), m2, v2, name=f"adamw_{n}")
        delta[n], new_m[n], new_v[n] = dl.reshape(shp), mn.reshape(shp), vn.reshape(shp)
    shapes = [w[n].shape for n in SMALL]
    packs = [_pack([src[n] for n in SMALL]) for src in (w, grads, m, v)]
    for dst, p in zip((delta, new_m, new_v), _adamw(*packs, name="adamw_small")):
        dst.update(zip(SMALL, _unpack(p, shapes)))

    return (loss, grad_x, *[grads[n] for n in WEIGHTS], *[delta[n] for n in WEIGHTS],
            *[new_m[n] for n in WEIGHTS], *[new_v[n] for n in WEIGHTS])


def kernel(x, meta_tokens, mix_norm, w_in, lam_re, lam_im, log_dt, b_re, b_im, c_re, c_im, d_skip, w_glu, b_glu, q_a_norm, w_q_b, kv_a_norm, w_kv_b, out_norm_ssm, out_norm_attn, w_out, ffn_norm, w_up, conv_w, conv_b, w_down, final_norm, loss_target, m_meta_tokens, m_mix_norm, m_w_in, m_lam_re, m_lam_im, m_log_dt, m_b_re, m_b_im, m_c_re, m_c_im, m_d_skip, m_w_glu, m_b_glu, m_q_a_norm, m_w_q_b, m_kv_a_norm, m_w_kv_b, m_out_norm_ssm, m_out_norm_attn, m_w_out, m_ffn_norm, m_w_up, m_conv_w, m_conv_b, m_w_down, m_final_norm, v_meta_tokens, v_mix_norm, v_w_in, v_lam_re, v_lam_im, v_log_dt, v_b_re, v_b_im, v_c_re, v_c_im, v_d_skip, v_w_glu, v_b_glu, v_q_a_norm, v_w_q_b, v_kv_a_norm, v_w_kv_b, v_out_norm_ssm, v_out_norm_attn, v_w_out, v_ffn_norm, v_w_up, v_conv_w, v_conv_b, v_w_down, v_final_norm):
    args = dict(locals())
    w = {n: args[n] for n in WEIGHTS}
    m = {n: args["m_" + n] for n in WEIGHTS}
    v = {n: args["v_" + n] for n in WEIGHTS}
    return _step(PROD, w, m, v, x, loss_target)
```

```python
import functools
import math
from typing import NamedTuple

import jax
import jax.numpy as jnp
from jax import lax
from jax.experimental import pallas as pl
from jax.experimental.pallas import tpu as pltpu

F32, BF16 = jnp.float32, jnp.bfloat16
MESH = pl.DeviceIdType.MESH
LANE = 128
ROW_ALIGN = 16
N_META = 16
PAD = 112
CHUNK = 64
SSM_GROUP = 16
SSM_STATE = 64
GROUPS_PER_BLOCK = 8
QK_NOPE, QK_ROPE, V_HEAD = 128, 64, 128
HEAD_SLOT = 256
ROPE_BASE = 10000.0
EPS = 1e-6
ADAM_LR, ADAM_B1, ADAM_B2, ADAM_EPS, ADAM_WD, ADAM_STEP = 0.001, 0.9, 0.999, 1e-08, 0.01, 10
DT_F32_BLOCK_BYTES = 1 << 20
SKIP, FIRST = "skip", "first"


class Cfg(NamedTuple):
    D: int
    S: int
    DS: int
    H: int
    QL: int
    KVL: int
    F: int

    @property
    def LP(self):
        return PAD + N_META + self.S

    @property
    def G(self):
        return self.DS // SSM_GROUP

    @property
    def NB(self):
        return self.G // GROUPS_PER_BLOCK

    @property
    def NL(self):
        return 2 * self.G * SSM_STATE

    @property
    def DATTN(self):
        return self.H * V_HEAD

    @property
    def DMIX(self):
        return self.DS + self.DATTN

    @property
    def DIN(self):
        return self.DS + self.QL + self.KVL + QK_ROPE

    @property
    def DINP(self):
        return self.DS + self.QL + self.KVL + LANE

    @property
    def FQ(self):
        return -(-(self.F // 4) // LANE) * LANE

    @property
    def FP(self):
        return 4 * self.FQ


PROD = Cfg(D=2048, S=2048, DS=1024, H=8, QL=512, KVL=256, F=5504)

WEIGHTS = ['meta_tokens', 'mix_norm', 'w_in', 'lam_re', 'lam_im', 'log_dt', 'b_re', 'b_im', 'c_re', 'c_im', 'd_skip',
           'w_glu', 'b_glu', 'q_a_norm', 'w_q_b', 'kv_a_norm', 'w_kv_b', 'out_norm_ssm', 'out_norm_attn', 'w_out',
           'ffn_norm', 'w_up', 'conv_w', 'conv_b', 'w_down', 'final_norm']
BIG = ['w_in', 'w_glu', 'w_q_b', 'w_kv_b', 'w_out', 'w_up', 'w_down']
SMALL = [n for n in WEIGHTS if n not in BIG]


def _pc(body, **kw):
    return pl.pallas_call(body, **kw)


def _tile(n, target, align=LANE):
    best = None
    d = align
    while d <= min(n, target):
        if n % d == 0:
            best = d
        d += align
    return best if best is not None else n


def _row_tile(rows, cols):
    return _tile(rows, max(ROW_ALIGN, DT_F32_BLOCK_BYTES // (4 * cols)), ROW_ALIGN)


def _mm(a, b, *, name, ta=False, tb=False, tm=None, tn=512, tk=None, out_dtype=F32, res=None,
        a_idx=None, b_idx=None, dims=None, a_lead=False):
    if dims is None:
        m, k = (a.shape[1], a.shape[0]) if ta else a.shape
        n = b.shape[0] if tb else b.shape[1]
    else:
        m, n, k = dims
    tm = _tile(m, tm or m, LANE if ta else ROW_ALIGN)
    tn = _tile(n, tn)
    tk = _tile(k, tk or k, ROW_ALIGN if (ta and not tb) else LANE)
    nm, nn, nk = m // tm, n // tn, k // tk
    a_idx = a_idx or ((lambda i, j, kk: (kk, i)) if ta else (lambda i, j, kk: (i, kk)))
    b_idx = b_idx or ((lambda i, j, kk: (j, kk)) if tb else (lambda i, j, kk: (kk, j)))
    dn = (((0 if ta else 1,), (1 if tb else 0,)), ((), ()))

    def body(*refs):
        a_ref, b_ref = refs[0], refs[1]
        r_ref = refs[2] if res is not None else None
        o_ref = refs[3] if res is not None else refs[2]
        d = lax.dot_general(a_ref[...].astype(BF16), b_ref[...].astype(BF16), dn, preferred_element_type=F32)

        def finish(r):
            if r_ref is not None:
                r = r + r_ref[...].astype(F32)
            o_ref[...] = r.astype(out_dtype)

        if nk == 1:
            finish(d)
        else:
            acc = refs[-1]
            kk = pl.program_id(2)

            @pl.when(kk == 0)
            def _():
                acc[...] = d

            @pl.when(kk > 0)
            def _():
                acc[...] += d

            @pl.when(kk == nk - 1)
            def _():
                finish(acc[...])

    a_blk = ((None,) if a_lead else ()) + ((tk, tm) if ta else (tm, tk))
    in_specs = [pl.BlockSpec(a_blk, a_idx), pl.BlockSpec((tn, tk) if tb else (tk, tn), b_idx)]
    args = [a, b]
    if res is not None:
        in_specs.append(pl.BlockSpec((tm, tn), lambda i, j, kk: (i, j)))
        args.append(res)
    return _pc(body, name=name, grid=(nm, nn, nk), in_specs=in_specs,
               out_specs=pl.BlockSpec((tm, tn), lambda i, j, kk: (i, j)),
               out_shape=jax.ShapeDtypeStruct((m, n), out_dtype),
               scratch_shapes=[pltpu.VMEM((tm, tn), F32)] if nk > 1 else [],
               compiler_params=pltpu.CompilerParams(dimension_semantics=("parallel", "parallel", "arbitrary")))(*args)


def _ew(fn, ins, vecs, outs, sums=(), *, name, tm=None):
    ins = [x if isinstance(x, tuple) else (x, x.shape[1], 0) for x in ins]
    ins = [x if len(x) == 4 else x + (None,) for x in ins]
    outs = [o if len(o) == 3 else o + (None,) for o in outs]
    rows = ins[0][0].shape[0]
    cmax = max([c for _, c, _, _ in ins] + [c for c, _, _ in outs])
    tm = tm or _row_tile(rows, cmax)
    n_in, n_vec, n_out, n_sum = len(ins), len(vecs), len(outs), len(sums)

    def body(*refs):
        i = pl.program_id(0)
        rid = i * tm + lax.broadcasted_iota(jnp.int32, (tm, 1), 0)
        vals = [r[...] for r in refs[:n_in + n_vec]]
        res = fn(rid, *vals)
        res = res if isinstance(res, (tuple, list)) else (res,)
        o_refs = refs[n_in + n_vec:]
        for o_ref, r, (_, _, mode) in zip(o_refs[:n_out], res[:n_out], outs):
            if mode == FIRST:
                @pl.when(i == 0)
                def _():
                    o_ref[...] = r.astype(o_ref.dtype)
            else:
                o_ref[...] = r.astype(o_ref.dtype)
        for o_ref, r in zip(o_refs[n_out:], res[n_out:]):
            part = jnp.sum(r.astype(F32), axis=0, keepdims=True)

            @pl.when(i == 0)
            def _():
                o_ref[...] = part

            @pl.when(i > 0)
            def _():
                o_ref[...] += part

    def row_idx(mode):
        if mode == SKIP:
            return lambda i, cb=0: (jnp.maximum(i - 1, 0), cb)
        if mode == FIRST:
            return lambda i, cb=0: (0, cb)
        return lambda i, cb=0: (i, cb)

    in_specs = [pl.BlockSpec((tm, c), functools.partial(row_idx(mode), cb=cb)) for _, c, cb, mode in ins]
    in_specs += [pl.BlockSpec(v.shape, functools.partial(lambda i, nd: (0,) * nd, nd=v.ndim)) for v in vecs]
    out_specs = [pl.BlockSpec((tm, c), row_idx(mode)) for c, _, mode in outs]
    out_specs += [pl.BlockSpec((1, c), lambda i: (0, 0)) for c in sums]
    out_rows = {None: rows, SKIP: rows - tm, FIRST: tm}
    out_shape = [jax.ShapeDtypeStruct((out_rows[mode], c), dt) for c, dt, mode in outs]
    out_shape += [jax.ShapeDtypeStruct((1, c), F32) for c in sums]
    return _pc(body, name=name, grid=(rows // tm,), in_specs=in_specs, out_specs=out_specs, out_shape=out_shape,
               compiler_params=pltpu.CompilerParams(dimension_semantics=("arbitrary",)))(*[x[0] for x in ins], *vecs)


def _rms_parts(x, g):
    r = lax.rsqrt(jnp.mean(x * x, axis=-1, keepdims=True) + EPS)
    return x * r, r


def _rms_bwd_block(x, g, dy):
    xhat, r = _rms_parts(x, g)
    dxhat = dy * g
    dx = r * (dxhat - xhat * jnp.mean(dxhat * xhat, axis=-1, keepdims=True))
    return dx, dy * xhat


def _rms_fwd(x, g, *, name):
    c = x[1] if isinstance(x, tuple) else x.shape[1]
    return _ew(lambda rid, xv, gv: _rms_parts(xv.astype(F32), gv)[0] * gv, [x], [g], [(c, BF16)], name=name)[0]


def _rms_bwd(x, g, dy, *, name, res=None, mask=False, with_bf16=False):
    c = x[1] if isinstance(x, tuple) else x.shape[1]

    def fn(rid, xv, dyv, *rest):
        gv = rest[-1]
        dx, dg = _rms_bwd_block(xv.astype(F32), gv, dyv.astype(F32))
        if res is not None:
            dx = dx + rest[0]
        if mask:
            dx = jnp.where(rid >= PAD, dx, 0.0)
        return (dx, dx, dg) if with_bf16 else (dx, dg)

    ins = [x, dy] + ([res] if res is not None else [])
    outs = [(c, F32)] + ([(c, BF16)] if with_bf16 else [])
    return _ew(fn, ins, [g], outs, [c], name=name)


S5_W = GROUPS_PER_BLOCK * SSM_STATE
S5_GW = GROUPS_PER_BLOCK * SSM_GROUP
S5_UNROLL = 8
S5_DA_ROWS = 272


def _s5_scan_in_place(ref, a_ref, *, reverse):
    lp = ref.shape[0]
    ar = a_ref[:, :S5_W]
    ai = -a_ref[:, S5_W:] if reverse else a_ref[:, S5_W:]

    def step(n, carry):
        hr, hi = carry
        for q in range(S5_UNROLL):
            t = n * S5_UNROLL + q
            t = lp - 1 - t if reverse else t
            nr = ar * hr - ai * hi + ref[pl.ds(t, 1), :S5_W]
            ni = ar * hi + ai * hr + ref[pl.ds(t, 1), S5_W:]
            ref[pl.ds(t, 1), :S5_W] = nr
            ref[pl.ds(t, 1), S5_W:] = ni
            hr, hi = nr, ni
        return hr, hi

    z = jnp.zeros((1, S5_W), F32)
    lax.fori_loop(0, lp // S5_UNROLL, step, (z, z))


def _s5_fwd(z, bb_band, cc_band, a_l, cfg, *, name):
    lp, ds, nl = cfg.LP, cfg.DS, cfg.NL

    def body(u_ref, bb_ref, cc_ref, a_ref, hs_ref, y_ref):
        hs_ref[...] = jnp.dot(u_ref[...].astype(BF16), bb_ref[...], preferred_element_type=F32)
        _s5_scan_in_place(hs_ref, a_ref, reverse=False)
        y_ref[...] = jnp.dot(hs_ref[...].astype(BF16), cc_ref[...], preferred_element_type=F32)

    return _pc(body, name=name, grid=(cfg.NB,),
               in_specs=[pl.BlockSpec((lp, S5_GW), lambda j: (0, j)), pl.BlockSpec((S5_GW, 2 * S5_W), lambda j: (j, 0)),
                         pl.BlockSpec((2 * S5_W, S5_GW), lambda j: (j, 0)), pl.BlockSpec((1, 2 * S5_W), lambda j: (0, j))],
               out_specs=[pl.BlockSpec((lp, 2 * S5_W), lambda j: (0, j)), pl.BlockSpec((lp, S5_GW), lambda j: (0, j))],
               out_shape=[jax.ShapeDtypeStruct((lp, nl), F32), jax.ShapeDtypeStruct((lp, ds), F32)],
               compiler_params=pltpu.CompilerParams(dimension_semantics=("parallel",)))(z, bb_band, cc_band, a_l)


def _s5_bwd(dy, hs, z, bb_band, cc_band, a_l, du_skip, cfg, *, name):
    lp, ds, nl = cfg.LP, cfg.DS, cfg.NL
    nt = (((1,), (1,)), ((), ()))
    tn = (((0,), (0,)), ((), ()))

    def body(dy_ref, hs_ref, u_ref, bb_ref, cc_ref, a_ref, sk_ref, du_ref, dbb_ref, dcc_ref, da_ref, g_ref):
        dyv = dy_ref[...]
        g_ref[...] = lax.dot_general(dyv, cc_ref[...], nt, preferred_element_type=F32)
        _s5_scan_in_place(g_ref, a_ref, reverse=True)
        dcc_ref[...] = lax.dot_general(hs_ref[...].astype(BF16), dyv, tn, preferred_element_type=F32)
        gb = g_ref[...].astype(BF16)
        dbb_ref[...] = lax.dot_general(u_ref[...].astype(BF16), gb, tn, preferred_element_type=F32)
        du_ref[...] = lax.dot_general(gb, bb_ref[...], nt, preferred_element_type=F32) + sk_ref[...]
        dre = jnp.zeros((1, S5_W), F32)
        dim = jnp.zeros((1, S5_W), F32)
        for r0 in range(0, lp, S5_DA_ROWS):
            rows = min(S5_DA_ROWS, lp - r0)
            first = lax.broadcasted_iota(jnp.int32, (rows, 1), 0) == 0
            prev = hs_ref[r0 - 1:r0, :] if r0 else jnp.zeros((1, 2 * S5_W), F32)
            hr = jnp.where(first, prev[:, :S5_W], pltpu.roll(hs_ref[r0:r0 + rows, :S5_W], 1, 0))
            hi = jnp.where(first, prev[:, S5_W:], pltpu.roll(hs_ref[r0:r0 + rows, S5_W:], 1, 0))
            gr, gi = g_ref[r0:r0 + rows, :S5_W], g_ref[r0:r0 + rows, S5_W:]
            dre = dre + jnp.sum(gr * hr + gi * hi, axis=0, keepdims=True)
            dim = dim + jnp.sum(gi * hr - gr * hi, axis=0, keepdims=True)
        da_ref[:, :S5_W] = dre
        da_ref[:, S5_W:] = dim

    col_blk = pl.BlockSpec((lp, S5_GW), lambda j: (0, j))
    lane_blk = pl.BlockSpec((lp, 2 * S5_W), lambda j: (0, j))
    bb_blk = pl.BlockSpec((S5_GW, 2 * S5_W), lambda j: (j, 0))
    cc_blk = pl.BlockSpec((2 * S5_W, S5_GW), lambda j: (j, 0))
    a_blk = pl.BlockSpec((1, 2 * S5_W), lambda j: (0, j))
    return _pc(body, name=name, grid=(cfg.NB,),
               in_specs=[col_blk, lane_blk, col_blk, bb_blk, cc_blk, a_blk, col_blk],
               out_specs=[col_blk, bb_blk, cc_blk, a_blk],
               out_shape=[jax.ShapeDtypeStruct((lp, ds), F32), jax.ShapeDtypeStruct((ds, 2 * S5_W), F32),
                          jax.ShapeDtypeStruct((nl, S5_GW), F32), jax.ShapeDtypeStruct((1, nl), F32)],
               scratch_shapes=[pltpu.VMEM((lp, 2 * S5_W), F32)],
               compiler_params=pltpu.CompilerParams(dimension_semantics=("parallel",)))(dy, hs, z, bb_band, cc_band, a_l, du_skip)


def _conv_gate(pre, cw, cb):
    return cw[0:1] * pltpu.roll(pre, 2, 0) + cw[1:2] * pltpu.roll(pre, 1, 0) + cw[2:3] * pre + cb


def _conv_fwd(up, cw, cb, *, name):
    lp, fp2 = up.shape
    fp = fp2 // 2
    tc = _tile(fp, 256)
    nb = fp // tc

    def body(pre_ref, val_ref, cw_ref, cb_ref, o_ref):
        gate = _conv_gate(pre_ref[...], cw_ref[...], cb_ref[...])
        o_ref[...] = (jax.nn.silu(gate) * val_ref[...]).astype(BF16)

    return _pc(body, name=name, grid=(nb,),
               in_specs=[pl.BlockSpec((lp, tc), lambda j: (0, j)), pl.BlockSpec((lp, tc), lambda j: (0, nb + j)),
                         pl.BlockSpec((3, tc), lambda j: (0, j)), pl.BlockSpec((1, tc), lambda j: (0, j))],
               out_specs=pl.BlockSpec((lp, tc), lambda j: (0, j)),
               out_shape=jax.ShapeDtypeStruct((lp, fp), BF16),
               compiler_params=pltpu.CompilerParams(dimension_semantics=("parallel",)))(up, up, cw, cb)


def _conv_bwd(up, dact, cw, cb, *, name):
    lp, fp2 = up.shape
    fp = fp2 // 2
    tc = _tile(fp, 256)
    nb = fp // tc

    def body(pre_ref, val_ref, da_ref, cw_ref, cb_ref, dup_ref, dcw_ref, dcb_ref):
        pre, val, da, cwv = pre_ref[...], val_ref[...], da_ref[...].astype(F32), cw_ref[...]
        gate = _conv_gate(pre, cwv, cb_ref[...])
        sg = jax.nn.sigmoid(gate)
        dup_ref[1] = (da * (gate * sg)).astype(BF16)
        dgate = da * val * (sg * (1.0 + gate * (1.0 - sg)))
        dpre = cwv[2:3] * dgate + cwv[1:2] * pltpu.roll(dgate, lp - 1, 0) + cwv[0:1] * pltpu.roll(dgate, lp - 2, 0)
        dup_ref[0] = dpre.astype(BF16)
        dcb_ref[...] = jnp.sum(dgate, axis=0, keepdims=True)
        dcw_ref[0:1, :] = jnp.sum(dgate * pltpu.roll(pre, 2, 0), axis=0, keepdims=True)
        dcw_ref[1:2, :] = jnp.sum(dgate * pltpu.roll(pre, 1, 0), axis=0, keepdims=True)
        dcw_ref[2:3, :] = jnp.sum(dgate * pre, axis=0, keepdims=True)

    return _pc(body, name=name, grid=(nb,),
               in_specs=[pl.BlockSpec((lp, tc), lambda j: (0, j)), pl.BlockSpec((lp, tc), lambda j: (0, nb + j)),
                         pl.BlockSpec((lp, tc), lambda j: (0, j)),
                         pl.BlockSpec((3, tc), lambda j: (0, j)), pl.BlockSpec((1, tc), lambda j: (0, j))],
               out_specs=[pl.BlockSpec((2, lp, tc), lambda j: (0, 0, j)),
                          pl.BlockSpec((3, tc), lambda j: (0, j)), pl.BlockSpec((1, tc), lambda j: (0, j))],
               out_shape=[jax.ShapeDtypeStruct((2, lp, fp), BF16), jax.ShapeDtypeStruct((3, fp), F32),
                          jax.ShapeDtypeStruct((1, fp), F32)],
               compiler_params=pltpu.CompilerParams(dimension_semantics=("parallel",)))(up, up, dact, cw, cb)


def _key_limit(i, tq, lp):
    return min(lp, -(-((i + 1) * tq) // LANE) * LANE)


def _attn_mask(i, tq, nk):
    qrow = i * tq + lax.broadcasted_iota(jnp.int32, (tq, 1), 0)
    krow = lax.broadcasted_iota(jnp.int32, (1, nk), 1)
    return (krow >= PAD) & ((krow // CHUNK) <= (qrow // CHUNK)), qrow >= PAD


def _attn_scores(q, kn, kr, i, tq, scale):
    nt = (((1,), (1,)), ((), ()))
    s = lax.dot_general(q[:, :QK_NOPE], kn, nt, preferred_element_type=F32)
    s = s + lax.dot_general(q[:, QK_NOPE:], kr, nt, preferred_element_type=F32)
    mask, qvalid = _attn_mask(i, tq, kn.shape[0])
    return jnp.where(mask, s * scale, jnp.finfo(F32).min), qvalid


def _per_q_block(nq, fn):
    i = pl.program_id(1)
    for blk in range(nq):
        pl.when(i == blk)(functools.partial(fn, blk))


def _attn_fwd(qx, kv, kr, cfg, *, name):
    lp, h = cfg.LP, cfg.H
    tq = _tile(lp, 272, ROW_ALIGN)
    nq = lp // tq
    scale = 1.0 / math.sqrt(QK_NOPE + QK_ROPE)

    def body(q_ref, kn_ref, v_ref, kr_ref, o_ref, lse_ref):
        def block(blk):
            nk = _key_limit(blk, tq, lp)
            s, qvalid = _attn_scores(q_ref[...], kn_ref[:nk], kr_ref[:nk], blk, tq, scale)
            m = jnp.max(s, axis=-1, keepdims=True)
            p = jnp.exp(s - m)
            l = jnp.sum(p, axis=-1, keepdims=True)
            o = jnp.dot(p.astype(BF16), v_ref[:nk], preferred_element_type=F32) / l
            o_ref[...] = jnp.where(qvalid, o, 0.0)
            lse_ref[...] = m + jnp.log(l)

        _per_q_block(nq, block)

    return _pc(body, name=name, grid=(h, nq),
               in_specs=[pl.BlockSpec((tq, HEAD_SLOT), lambda hh, i: (i, hh)),
                         pl.BlockSpec((lp, QK_NOPE), lambda hh, i: (0, 2 * hh)),
                         pl.BlockSpec((lp, V_HEAD), lambda hh, i: (0, 2 * hh + 1)),
                         pl.BlockSpec((lp, LANE), lambda hh, i: (0, 0))],
               out_specs=[pl.BlockSpec((tq, V_HEAD), lambda hh, i: (i, hh)),
                          pl.BlockSpec((None, tq, 1), lambda hh, i: (hh, i, 0))],
               out_shape=[jax.ShapeDtypeStruct((lp, h * V_HEAD), F32), jax.ShapeDtypeStruct((h, lp, 1), F32)],
               compiler_params=pltpu.CompilerParams(dimension_semantics=("parallel", "parallel")))(qx, kv, kv, kr)


def _attn_bwd(qx, kv, kr, o, lse, do, cfg, *, name):
    lp, h = cfg.LP, cfg.H
    tq = _tile(lp, 272, ROW_ALIGN)
    nq = lp // tq
    scale = 1.0 / math.sqrt(QK_NOPE + QK_ROPE)
    tn_dims = (((0,), (0,)), ((), ()))

    def body(q_ref, kn_ref, v_ref, kr_ref, o_ref, lse_ref, do_ref, dq_ref, dkv_ref, dkr_ref, dkv_acc):
        hh, i = pl.program_id(0), pl.program_id(1)

        @pl.when(i == 0)
        def _():
            dkv_acc[...] = jnp.zeros_like(dkv_acc)

        @pl.when((i == 0) & (hh == 0))
        def _():
            dkr_ref[...] = jnp.zeros_like(dkr_ref)

        def block(blk):
            nk = _key_limit(blk, tq, lp)
            q, kn, v, krv = q_ref[...], kn_ref[:nk], v_ref[:nk], kr_ref[:nk]
            s, qvalid = _attn_scores(q, kn, krv, blk, tq, scale)
            dov = jnp.where(qvalid, do_ref[...], 0.0)
            p = jnp.exp(s - lse_ref[...])
            delta = jnp.sum(dov * o_ref[...], axis=-1, keepdims=True)
            dob = dov.astype(BF16)
            dp = lax.dot_general(dob, v, (((1,), (1,)), ((), ())), preferred_element_type=F32)
            ds = (p * (dp - delta) * scale).astype(BF16)
            dq_ref[:, :QK_NOPE] = jnp.dot(ds, kn, preferred_element_type=F32)
            dq_ref[:, QK_NOPE:] = jnp.dot(ds, krv, preferred_element_type=F32)
            dkv_acc[:nk, :QK_NOPE] += lax.dot_general(ds, q[:, :QK_NOPE], tn_dims, preferred_element_type=F32)
            dkv_acc[:nk, QK_NOPE:] += lax.dot_general(p.astype(BF16), dob, tn_dims, preferred_element_type=F32)
            dkr_ref[:nk, :] += lax.dot_general(ds, q[:, QK_NOPE:], tn_dims, preferred_element_type=F32)

        _per_q_block(nq, block)

        @pl.when(i == nq - 1)
        def _():
            dkv_ref[...] = dkv_acc[...].astype(BF16)

    return _pc(body, name=name, grid=(h, nq),
               in_specs=[pl.BlockSpec((tq, HEAD_SLOT), lambda hh, i: (i, hh)),
                         pl.BlockSpec((lp, QK_NOPE), lambda hh, i: (0, 2 * hh)),
                         pl.BlockSpec((lp, V_HEAD), lambda hh, i: (0, 2 * hh + 1)),
                         pl.BlockSpec((lp, LANE), lambda hh, i: (0, 0)),
                         pl.BlockSpec((tq, V_HEAD), lambda hh, i: (i, hh)),
                         pl.BlockSpec((None, tq, 1), lambda hh, i: (hh, i, 0)),
                         pl.BlockSpec((tq, V_HEAD), lambda hh, i: (i, hh))],
               out_specs=[pl.BlockSpec((tq, HEAD_SLOT), lambda hh, i: (i, hh)),
                          pl.BlockSpec((lp, QK_NOPE + V_HEAD), lambda hh, i: (0, hh)),
                          pl.BlockSpec((lp, LANE), lambda hh, i: (0, 0))],
               out_shape=[jax.ShapeDtypeStruct((lp, h * HEAD_SLOT), F32),
                          jax.ShapeDtypeStruct((lp, h * (QK_NOPE + V_HEAD)), BF16),
                          jax.ShapeDtypeStruct((lp, LANE), F32)],
               scratch_shapes=[pltpu.VMEM((lp, QK_NOPE + V_HEAD), F32)],
               compiler_params=pltpu.CompilerParams(dimension_semantics=("arbitrary", "arbitrary")))(qx, kv, kv, kr, o, lse, do)


def _rot_half(x):
    lane = lax.broadcasted_iota(jnp.int32, x.shape, 1)
    half = QK_ROPE // 2
    return jnp.where(lane < half, -pltpu.roll(x, LANE - half, 1), pltpu.roll(x, half, 1))


def _rope(x, cos, sin):
    return x * cos + _rot_half(x) * sin


def _unrope(dy, cos, sin):
    return dy * cos - _rot_half(dy * sin)


def _rope_heads(fn, h):
    def apply(rid, q, cos, sin):
        parts = []
        for hh in range(h):
            parts.append(q[:, hh * HEAD_SLOT: hh * HEAD_SLOT + QK_NOPE])
            parts.append(fn(q[:, hh * HEAD_SLOT + QK_NOPE: (hh + 1) * HEAD_SLOT], cos, sin))
        return jnp.concatenate(parts, axis=1)
    return apply


ANY = pl.BlockSpec(memory_space=pl.ANY)


def _place():
    x, y, c = lax.axis_index("x"), lax.axis_index("y"), lax.axis_index("c")
    chips = [(1 - x, y), (x, 1 - y), (1 - x, 1 - y)]
    return x, y, c, chips


def _rcopy(src, dst, send_sem, recv_sem, dev):
    return pltpu.make_async_remote_copy(src_ref=src, dst_ref=dst, send_sem=send_sem, recv_sem=recv_sem,
                                        device_id=dev, device_id_type=MESH)


def _place_shard(shard, dtype, *, name):
    r, cols = shard.shape
    tm = _row_tile(r, cols)
    nblk = r // tm
    me = (2 * lax.axis_index("x") + lax.axis_index("y")).astype(jnp.int32).reshape(1)

    def body(me_ref, s_ref, o_ref):
        o_ref[...] = s_ref[...].astype(dtype)

    return _pc(body, name=name,
               grid_spec=pltpu.PrefetchScalarGridSpec(
                   num_scalar_prefetch=1, grid=(nblk,),
                   in_specs=[pl.BlockSpec((tm, cols), lambda i, mr: (i, 0))],
                   out_specs=pl.BlockSpec((tm, cols), lambda i, mr: (mr[0] * nblk + i, 0))),
               out_shape=jax.ShapeDtypeStruct((4 * r, cols), dtype),
               compiler_params=pltpu.CompilerParams(dimension_semantics=("arbitrary",)))(me, shard)


def _allgather(fulls, *, name):
    n = len(fulls)

    def body(*refs):
        outs = refs[n:2 * n]
        send_sems, recv_sems = refs[2 * n:]
        x, y, c, chips = _place()
        sib = (x, y, 1 - c)
        me = 2 * x + y

        def rows(t, s, half):
            hrows = outs[t].shape[0] // 8
            return outs[t].at[pl.ds((2 * s + half) * hrows, hrows)]

        sent = []
        for t in range(n):
            for j, (cx, cy) in enumerate(chips):
                cp = _rcopy(rows(t, me, c), rows(t, me, c), send_sems.at[6 * t + j], recv_sems.at[6 * t + j], (cx, cy, c))
                cp.start()
                sent.append(cp)
        for t in range(n):
            for j, (cx, cy) in enumerate(chips):
                landed = rows(t, 2 * cx + cy, c)
                _rcopy(landed, landed, send_sems.at[6 * t + j], recv_sems.at[6 * t + j], (cx, cy, c)).wait_recv()
                cp = _rcopy(landed, landed, send_sems.at[6 * t + 3 + j], recv_sems.at[6 * t + 3 + j], sib)
                cp.start()
                sent.append(cp)
        for t in range(n):
            for j, (cx, cy) in enumerate(chips):
                other = rows(t, 2 * cx + cy, 1 - c)
                _rcopy(other, other, send_sems.at[6 * t + 3 + j], recv_sems.at[6 * t + 3 + j], sib).wait_recv()
        for cp in sent:
            cp.wait_send()

    return _pc(body, name=name, in_specs=[ANY] * n, out_specs=[ANY] * n,
               out_shape=[jax.ShapeDtypeStruct(f.shape, f.dtype) for f in fulls],
               input_output_aliases={t: t for t in range(n)},
               scratch_shapes=[pltpu.SemaphoreType.DMA((6 * n,)), pltpu.SemaphoreType.DMA((6 * n,))])(*fulls)


HBM = pl.BlockSpec(memory_space=pltpu.HBM)
SEM = pl.BlockSpec(memory_space=pltpu.SEMAPHORE)
EFFECT = pltpu.SideEffectType.DATAFLOW_SIDE_EFFECTING
TOKEN = jax.ShapeDtypeStruct((8, LANE), F32)


def _in_hbm(a):
    return pltpu.with_memory_space_constraint(a, pltpu.HBM)


def _half_rows(ref, s, half):
    hrows = ref.shape[0] // 8
    return ref.at[pl.ds((2 * s + half) * hrows, hrows)]


def _split_start(bufs, copies, n_copies, *, name, before=None):
    n = len(bufs)
    extra = [] if before is None else [before]

    def body(*refs):
        send_sems, recv_sems, token = refs[n + len(extra)], refs[n + len(extra) + 1], refs[-1]
        for k, (src, dst, dev) in enumerate(copies(refs[:n])):
            _rcopy(src, dst, send_sems.at[k], recv_sems.at[k], dev).start()
        token[...] = jnp.zeros_like(token)

    res = _pc(body, name=name, in_specs=[HBM] * n + [ANY] * len(extra),
              out_specs=[SEM, SEM] + [HBM] * n + [pl.BlockSpec(memory_space=pltpu.VMEM)],
              out_shape=[pltpu.SemaphoreType.DMA((n_copies,)), pltpu.SemaphoreType.DMA((n_copies,))]
              + [pltpu.HBM(b.shape, b.dtype) for b in bufs] + [TOKEN],
              input_output_aliases={t: 2 + t for t in range(n)},
              compiler_params=pltpu.CompilerParams(has_side_effects=EFFECT))(*[_in_hbm(b) for b in bufs], *extra)
    return res[0], res[1], list(res[2:2 + n]), res[-1]


def _split_wait(send_sems, recv_sems, bufs, copies, after, *, name):
    n = len(bufs)

    def body(*refs):
        send_ref, recv_ref = refs[n], refs[n + 1]
        for k, (src, dst, dev) in enumerate(copies(refs[:n])):
            cp = _rcopy(src, dst, send_ref.at[k], recv_ref.at[k], dev)
            cp.wait_send()
            cp.wait_recv()

    return _pc(body, name=name, in_specs=[HBM] * n + [SEM, SEM, ANY], out_specs=[HBM] * n,
               out_shape=[pltpu.HBM(b.shape, b.dtype) for b in bufs],
               input_output_aliases={t: t for t in range(n)},
               compiler_params=pltpu.CompilerParams(has_side_effects=EFFECT))(*bufs, send_sems, recv_sems, after)


def _allgather_ici_copies(refs):
    x, y, c, chips = _place()
    return [(_half_rows(r, 2 * x + y, c), _half_rows(r, 2 * x + y, c), (cx, cy, c)) for r in refs for cx, cy in chips]


def _rs_chips_copies(refs):
    x, y, c, chips = _place()
    n = len(refs) // 2
    return [(refs[t].at[2 * cx + cy], refs[n + t].at[j], (cx, cy, c)) for t in range(n) for j, (cx, cy) in enumerate(chips)]


def _rs_sibling_copies(refs):
    x, y, c, _ = _place()
    n = len(refs) // 2
    out = []
    for t in range(n):
        h = refs[t].shape[0] // 8
        out += [(refs[t].at[pl.ds((2 * s + 1 - c) * h, h)], refs[n + t].at[s], (x, y, 1 - c)) for s in range(4)]
    return out


def _allgather_forward(fulls, *, name):
    n = len(fulls)

    def body(*refs):
        outs = refs[n:2 * n]
        send_sems, recv_sems = refs[2 * n:]
        x, y, c, chips = _place()
        sent = []
        for t in range(n):
            for j, (cx, cy) in enumerate(chips):
                landed = _half_rows(outs[t], 2 * cx + cy, c)
                cp = _rcopy(landed, landed, send_sems.at[3 * t + j], recv_sems.at[3 * t + j], (x, y, 1 - c))
                cp.start()
                sent.append(cp)
        for t in range(n):
            for j, (cx, cy) in enumerate(chips):
                other = _half_rows(outs[t], 2 * cx + cy, 1 - c)
                _rcopy(other, other, send_sems.at[3 * t + j], recv_sems.at[3 * t + j], (x, y, 1 - c)).wait_recv()
        for cp in sent:
            cp.wait_send()

    return _pc(body, name=name, in_specs=[ANY] * n, out_specs=[ANY] * n,
               out_shape=[jax.ShapeDtypeStruct(f.shape, f.dtype) for f in fulls],
               input_output_aliases={t: t for t in range(n)},
               scratch_shapes=[pltpu.SemaphoreType.DMA((3 * n,)), pltpu.SemaphoreType.DMA((3 * n,))])(*fulls)


def _rs_sibling(grads, *, name):
    n = len(grads)

    def body(*refs):
        ins, outs = refs[:n], refs[n:2 * n]
        send_sems, recv_sems = refs[2 * n:]
        x, y, c, _ = _place()
        cps = []
        for t in range(n):
            h = ins[t].shape[0] // 8
            for s in range(4):
                cp = _rcopy(ins[t].at[pl.ds((2 * s + 1 - c) * h, h)], outs[t].at[s], send_sems.at[4 * t + s],
                            recv_sems.at[4 * t + s], (x, y, 1 - c))
                cp.start()
                cps.append(cp)
        for cp in cps:
            cp.wait()

    return _pc(body, name=name, in_specs=[ANY] * n, out_specs=[ANY] * n,
               out_shape=[jax.ShapeDtypeStruct((4, g.shape[0] // 8, g.shape[1]), g.dtype) for g in grads],
               scratch_shapes=[pltpu.SemaphoreType.DMA((4 * n,)), pltpu.SemaphoreType.DMA((4 * n,))])(*grads)


def _rs_chips(sends, *, name):
    n = len(sends)

    def body(*refs):
        s_refs, b_refs = refs[:n], refs[n:2 * n]
        send_sems, recv_sems = refs[2 * n:]
        x, y, c, chips = _place()
        cps = []
        for t in range(n):
            for j, (cx, cy) in enumerate(chips):
                cp = _rcopy(s_refs[t].at[2 * cx + cy], b_refs[t].at[j], send_sems.at[3 * t + j], recv_sems.at[3 * t + j],
                            (cx, cy, c))
                cp.start()
                cps.append(cp)
        for cp in cps:
            cp.wait()

    return _pc(body, name=name, in_specs=[ANY] * n, out_specs=[ANY] * n,
               out_shape=[jax.ShapeDtypeStruct((3,) + s.shape[1:], s.dtype) for s in sends],
               scratch_shapes=[pltpu.SemaphoreType.DMA((3 * n,)), pltpu.SemaphoreType.DMA((3 * n,))])(*sends)


def _rs_final(fulls, *, name):
    n = len(fulls)

    def body(*refs):
        outs = refs[n:2 * n]
        send_sems, recv_sems = refs[2 * n:]
        x, y, c, _ = _place()
        cps = []
        for t in range(n):
            cp = _rcopy(outs[t].at[c], outs[t].at[c], send_sems.at[t], recv_sems.at[t], (x, y, 1 - c))
            cp.start()
            cps.append(cp)
        for cp in cps:
            cp.wait()

    return _pc(body, name=name, in_specs=[ANY] * n, out_specs=[ANY] * n,
               out_shape=[jax.ShapeDtypeStruct(f.shape, f.dtype) for f in fulls],
               input_output_aliases={t: t for t in range(n)},
               scratch_shapes=[pltpu.SemaphoreType.DMA((n,)), pltpu.SemaphoreType.DMA((n,))])(*fulls)


def _add_halves(g, a, send_dtype, *, name):
    _, h, cols = a.shape
    th = _row_tile(h, cols)
    g4 = g.reshape(4, 2, h, cols)
    c = lax.axis_index("c").astype(jnp.int32).reshape(1)

    def body(c_ref, g_ref, a_ref, p_ref, s_ref):
        v = g_ref[...] + a_ref[...]
        p_ref[...] = v
        s_ref[...] = v.astype(send_dtype)

    return _pc(body, name=name,
               grid_spec=pltpu.PrefetchScalarGridSpec(
                   num_scalar_prefetch=1, grid=(4, h // th),
                   in_specs=[pl.BlockSpec((None, None, th, cols), lambda s, i, cr: (s, cr[0], i, 0)),
                             pl.BlockSpec((None, th, cols), lambda s, i, cr: (s, i, 0))],
                   out_specs=[pl.BlockSpec((None, th, cols), lambda s, i, cr: (s, i, 0))] * 2),
               out_shape=[jax.ShapeDtypeStruct(a.shape, F32), jax.ShapeDtypeStruct(a.shape, send_dtype)],
               compiler_params=pltpu.CompilerParams(dimension_semantics=("arbitrary", "arbitrary")))(c, g4, a)


def _add_chips(p, b, *, name):
    _, h, cols = p.shape
    th = _row_tile(h, cols)
    idx = jnp.stack([2 * lax.axis_index("x") + lax.axis_index("y"), lax.axis_index("c")]).astype(jnp.int32)

    def body(idx_ref, p_ref, b_ref, r_ref):
        r_ref[...] = ((p_ref[...] + b_ref[0].astype(F32)) + b_ref[1].astype(F32)) + b_ref[2].astype(F32)

    return _pc(body, name=name,
               grid_spec=pltpu.PrefetchScalarGridSpec(
                   num_scalar_prefetch=1, grid=(h // th,),
                   in_specs=[pl.BlockSpec((None, th, cols), lambda i, ir: (ir[0], i, 0)),
                             pl.BlockSpec((3, th, cols), lambda i, ir: (0, i, 0))],
                   out_specs=pl.BlockSpec((None, th, cols), lambda i, ir: (ir[1], i, 0))),
               out_shape=jax.ShapeDtypeStruct((2, h, cols), F32),
               compiler_params=pltpu.CompilerParams(dimension_semantics=("arbitrary",)))(idx, p, b)


def _add_halves_all(grads, recv, send_dtypes, tag):
    parts, sends = [], []
    for t, (g, a) in enumerate(zip(grads, recv)):
        p, s = _add_halves(g, a, send_dtypes[t], name=f"rs_add_halves_{tag}{t}")
        parts.append(p)
        sends.append(s)
    return parts, sends


def _rs_finish(parts, others, tag):
    halves = [_add_chips(p, b, name=f"rs_add_chips_{tag}{t}") for t, (p, b) in enumerate(zip(parts, others))]
    full = _rs_final(halves, name=f"rs_final_{tag}")
    return [f.reshape(-1, f.shape[-1]) for f in full]


def _s5_discretize(lam_re, lam_im, log_dt, b_re, b_im):
    lam = lax.complex(lam_re, lam_im)
    dt = jnp.exp(log_dt)[:, None]
    lam_bar = jnp.exp(lam * dt)
    b_bar = ((lam_bar - 1.0) / lam)[..., None] * lax.complex(b_re, b_im)
    return jnp.real(lam_bar), jnp.imag(lam_bar), jnp.real(b_bar), jnp.imag(b_bar)


def _lanes_from_gp(re, im, cfg):
    v = jnp.stack([re, im]).reshape(2, cfg.NB, GROUPS_PER_BLOCK, SSM_STATE)
    return jnp.transpose(v, (1, 0, 2, 3)).reshape(1, cfg.NL)


def _gp_from_lanes(v, cfg):
    v = jnp.transpose(v.reshape(cfg.NB, 2, GROUPS_PER_BLOCK, SSM_STATE), (1, 0, 2, 3)).reshape(2, cfg.G, SSM_STATE)
    return v[0], v[1]


def _bb_band(bb_re, bb_im, cfg):
    eye = jnp.eye(GROUPS_PER_BLOCK, dtype=F32)
    bb = jnp.stack([bb_re, bb_im]).reshape(2, cfg.NB, GROUPS_PER_BLOCK, SSM_STATE, SSM_GROUP)
    return jnp.einsum('rjgpc,gh->jgcrhp', bb, eye).reshape(cfg.DS, 2 * GROUPS_PER_BLOCK * SSM_STATE)


def _bb_from_band(m, cfg):
    eye = jnp.eye(GROUPS_PER_BLOCK, dtype=F32)
    m = m.reshape(cfg.NB, GROUPS_PER_BLOCK, SSM_GROUP, 2, GROUPS_PER_BLOCK, SSM_STATE)
    v = jnp.einsum('jgcrhp,gh->rjgpc', m, eye).reshape(2, cfg.G, SSM_STATE, SSM_GROUP)
    return v[0], v[1]


def _cc_band(c_re, c_im, cfg):
    eye = jnp.eye(GROUPS_PER_BLOCK, dtype=F32)
    cc = jnp.stack([c_re, -c_im]).reshape(2, cfg.NB, GROUPS_PER_BLOCK, SSM_GROUP, SSM_STATE)
    return jnp.einsum('rjgcp,gh->jrhpgc', cc, eye).reshape(cfg.NL, GROUPS_PER_BLOCK * SSM_GROUP)


def _cc_from_band(m, cfg):
    eye = jnp.eye(GROUPS_PER_BLOCK, dtype=F32)
    m = m.reshape(cfg.NB, 2, GROUPS_PER_BLOCK, SSM_STATE, GROUPS_PER_BLOCK, SSM_GROUP)
    v = jnp.einsum('jrhpgc,gh->rjgcp', m, eye).reshape(2, cfg.G, SSM_GROUP, SSM_STATE)
    return v[0], -v[1]


PACK_COLS = 512
PACK_ROW_ALIGN = 64


def _pack(arrs):
    flat = jnp.concatenate([a.reshape(-1).astype(F32) for a in arrs])
    unit = PACK_COLS * PACK_ROW_ALIGN
    total = -(-flat.shape[0] // unit) * unit
    return jnp.pad(flat, (0, total - flat.shape[0])).reshape(-1, PACK_COLS)


def _unpack(p, shapes):
    flat = p.reshape(-1)
    out, off = [], 0
    for shp in shapes:
        size = math.prod(shp)
        out.append(flat[off:off + size].reshape(shp))
        off += size
    return out


def _adamw(w, g, m, v, *, name, emit_grad=False):
    c1 = 1.0 / (1.0 - ADAM_B1 ** ADAM_STEP)
    c2 = 1.0 / (1.0 - ADAM_B2 ** ADAM_STEP)

    def fn(rid, wv, gv, mv, vv):
        mn = ADAM_B1 * mv + (1.0 - ADAM_B1) * gv
        vn = ADAM_B2 * vv + (1.0 - ADAM_B2) * (gv * gv)
        delta = -ADAM_LR * ((mn * c1) / (jnp.sqrt(vn * c2) + ADAM_EPS) + ADAM_WD * wv)
        return (gv, delta, mn, vn) if emit_grad else (delta, mn, vn)

    cols = w.shape[1]
    return _ew(fn, [w, g, m, v], [], [(cols, F32)] * (4 if emit_grad else 3), name=name)


def _to_comm_layout(name, w, cfg):
    w = w[0]
    if name == 'w_in':
        return jnp.pad(w, ((0, 0), (0, cfg.DINP - cfg.DIN)))
    if name == 'w_q_b':
        hs = w.shape[1] // (QK_NOPE + QK_ROPE)
        wt = w.T.reshape(hs, QK_NOPE + QK_ROPE, cfg.QL)
        return jnp.pad(wt, ((0, 0), (0, HEAD_SLOT - QK_NOPE - QK_ROPE), (0, 0))).reshape(hs * HEAD_SLOT, cfg.QL)
    if name == 'w_kv_b':
        return w.T
    if name == 'w_up':
        wt = w.T.reshape(2, cfg.F // 4, cfg.D)
        return jnp.pad(wt, ((0, 0), (0, cfg.FQ - cfg.F // 4), (0, 0))).reshape(2 * cfg.FQ, cfg.D)
    if name == 'w_down':
        return jnp.pad(w, ((0, cfg.FQ - cfg.F // 4), (0, 0)))
    return w


def _from_comm_layout(name, g, cfg):
    if name == 'w_in':
        g = g[:, :cfg.DIN]
    elif name == 'w_q_b':
        hs = g.shape[0] // HEAD_SLOT
        g = g.reshape(hs, HEAD_SLOT, cfg.QL)[:, :QK_NOPE + QK_ROPE].reshape(hs * (QK_NOPE + QK_ROPE), cfg.QL).T
    elif name == 'w_kv_b':
        g = g.T
    elif name == 'w_up':
        g = g.reshape(2, cfg.FQ, cfg.D)[:, :cfg.F // 4].reshape(cfg.F // 2, cfg.D).T
    elif name == 'w_down':
        g = g[:cfg.F // 4]
    return g[None]


def _ff_pad(v, cfg):
    k = v.shape[0]
    return jnp.pad(v.reshape(k, 4, cfg.F // 4), ((0, 0), (0, 0), (0, cfg.FQ - cfg.F // 4))).reshape(k, cfg.FP)


def _ff_unpad(v, cfg):
    k = v.shape[0]
    return v.reshape(k, 4, cfg.FQ)[:, :, :cfg.F // 4].reshape(k, cfg.F)


def _step(cfg, w, m, v, x, loss_target):
    lp, d, ds, nl = cfg.LP, cfg.D, cfg.DS, cfg.NL
    xi, yi = lax.axis_index("x"), lax.axis_index("y")
    me = 2 * xi + yi

    placed = [_place_shard(_to_comm_layout(n, w[n], cfg), BF16, name=f"place_{n}") for n in BIG]
    conv_w_shard = jnp.pad(w['conv_w'][0], ((0, ROW_ALIGN - 3), (0, cfg.FQ - cfg.F // 4)))
    placed += [_place_shard(w['meta_tokens'], F32, name="place_meta"), _place_shard(conv_w_shard, F32, name="place_conv_w")]
    w_in, meta_full = _allgather([placed[0], placed[7]], name="allgather_first")
    meta = jnp.transpose(meta_full.reshape(4, N_META, d // 4), (1, 0, 2)).reshape(N_META, d)
    conv_b = _ff_pad(w['conv_b'], cfg)
    mid = placed[1:5] + [placed[8]]
    mid_send, mid_recv, mid_flying, mid_token = _split_start(mid, _allgather_ici_copies, 3 * len(mid), before=meta_full,
                                                             name="allgather_mid_start")
    ffn_send, ffn_recv, ffn_flying, ffn_token = _split_start(placed[5:7], _allgather_ici_copies, 6, before=mid_token,
                                                             name="allgather_ffn_start")
    mix_norm = w['mix_norm'] + (mid_token[0:1, 0:1] + ffn_token[0:1, 0:1])

    pos = (jnp.arange(lp, dtype=jnp.int32) - PAD).astype(F32)
    inv_freq = 1.0 / (ROPE_BASE ** (jnp.arange(0, QK_ROPE, 2, dtype=F32) / QK_ROPE))
    ang = pos[:, None] * inv_freq[None, :]
    zpad = jnp.zeros((lp, LANE - QK_ROPE), F32)
    cos_t = jnp.concatenate([jnp.cos(ang), jnp.cos(ang), zpad], axis=1)
    sin_t = jnp.concatenate([jnp.sin(ang), jnp.sin(ang), zpad], axis=1)

    s5_in = (w['lam_re'][0], w['lam_im'][0], w['log_dt'][0], w['b_re'][0], w['b_im'][0])
    (a_re, a_im, bb_re, bb_im), s5_vjp = jax.vjp(_s5_discretize, *s5_in)
    a_l = _lanes_from_gp(a_re, a_im, cfg)
    bb_band = _bb_band(bb_re, bb_im, cfg).astype(BF16)
    cc_band = _cc_band(w['c_re'][0], w['c_im'][0], cfg).astype(BF16)
    d_skip, b_glu = w['d_skip'], w['b_glu']

    h0 = jnp.concatenate([jnp.zeros((PAD, d), F32), meta, x[0]], axis=0)
    xn = _rms_fwd(h0, mix_norm, name="rms_mix")
    z = _mm(xn, w_in, name="mm_in", tn=_tile(cfg.DINP, 640))
    u = (z, ds, 0)
    q_a = (z, cfg.QL, ds // cfg.QL)
    kv_a = (z, cfg.KVL, (ds + cfg.QL) // cfg.KVL)
    k_pe = (z, LANE, (ds + cfg.QL + cfg.KVL) // LANE)

    hs, yc = _s5_fwd(z, bb_band, cc_band, a_l, cfg, name="s5_fwd")

    def s5_y(ycv, uv, dk):
        return ycv + dk * uv

    gl = _ew(lambda rid, ycv, uv, dk: jax.nn.gelu(s5_y(ycv, uv, dk)), [yc, u], [d_skip], [(ds, BF16)], name="s5_gelu")[0]
    mid_landed = _split_wait(mid_send, mid_recv, mid_flying, _allgather_ici_copies, gl, name="allgather_mid_wait")
    w_glu, w_qt, w_kvt, w_out, conv_full = _allgather_forward(mid_landed, name="allgather_mid_forward")
    conv_w = jnp.transpose(conv_full.reshape(4, ROW_ALIGN, cfg.FQ)[:, :3], (1, 0, 2)).reshape(3, cfg.FP)
    tg = _mm(gl, w_glu, name="mm_glu")
    ya = _ew(lambda rid, ycv, uv, tv, dk, bg: jax.nn.gelu(s5_y(ycv, uv, dk)) * jax.nn.sigmoid(tv + bg),
             [yc, u, tg], [d_skip, b_glu], [(ds, F32)], name="s5_glu")[0]

    qn = _rms_fwd(q_a, w['q_a_norm'], name="rms_q")
    kvn = _rms_fwd(kv_a, w['kv_a_norm'], name="rms_kv")
    q_raw = _mm(qn, w_qt, tb=True, name="mm_q")
    qx = _ew(_rope_heads(_rope, cfg.H), [q_raw, cos_t, sin_t], [], [(cfg.H * HEAD_SLOT, BF16)], name="rope_q")[0]
    kv = _mm(kvn, w_kvt, tb=True, out_dtype=BF16, name="mm_kv")
    kr = _ew(lambda rid, kp, cs, sn: _rope(kp, cs, sn), [k_pe, cos_t, sin_t], [], [(LANE, BF16)], name="rope_k")[0]
    o, lse = _attn_fwd(qx, kv, kr, cfg, name="attn_fwd")

    def norm2(rid, yav, ov, gs, ga):
        return jnp.concatenate([_rms_parts(yav, gs)[0] * gs, _rms_parts(ov, ga)[0] * ga], axis=1)

    yn = _ew(norm2, [ya, o], [w['out_norm_ssm'], w['out_norm_attn']], [(cfg.DMIX, BF16)], name="rms_out")[0]
    h1 = _mm(yn, w_out, res=h0, name="mm_out")
    xn2 = _rms_fwd(h1, w['ffn_norm'], name="rms_ffn")
    ffn_landed = _split_wait(ffn_send, ffn_recv, ffn_flying, _allgather_ici_copies, xn2, name="allgather_ffn_wait")
    w_upt, w_down = _allgather_forward(ffn_landed, name="allgather_ffn_forward")
    up = _mm(xn2, w_upt, tb=True, name="mm_up")
    act = _conv_fwd(up, conv_w, conv_b, name="conv_fwd")
    h2 = _mm(act, w_down, res=h1, tm=_tile(lp, 544, ROW_ALIGN), name="mm_down")

    g_final = w['final_norm'].reshape(1, d)

    def head(rid, hv, tv, gv):
        xhat, r = _rms_parts(hv, gv)
        valid = rid >= PAD + N_META
        diff = jnp.where(valid, xhat * gv - tv, 0.0)
        dout = diff * (1.0 / d)
        dxhat = dout * gv
        dx = r * (dxhat - xhat * jnp.mean(dxhat * xhat, axis=-1, keepdims=True))
        return dx, dx, dout * xhat, 0.5 * diff * dout

    dh2, dh2_b, dg_final, loss_cols = _ew(head, [h2, (loss_target[0], d, 0, SKIP)], [g_final], [(d, F32), (d, BF16)], [d, d],
                                          tm=PAD + N_META, name="loss_head")
    loss = lax.psum(jnp.sum(loss_cols), ("x", "y", "c"))

    dact = _mm(dh2_b, w_down, tb=True, out_dtype=BF16, name="mm_dact")
    dw_down = _mm(act, dh2_b, ta=True, tn=d, tm=512, name="mm_dw_down")

    def sibling_start(g, tag):
        land = lax.empty((4, g.shape[0] // 8, g.shape[1]), F32)
        return _split_start([g, land], _rs_sibling_copies, 4, name=f"rs_sibling_{tag}_start")

    dn_send, dn_recv, dn_flying, dn_token = sibling_start(dw_down, "down")
    dup, dconv_w, dconv_b = _conv_bwd(up, dact, conv_w, conv_b + dn_token[0:1, 0:1], name="conv_bwd")
    tk_up, tm_up = _tile(cfg.FP, 1408), _tile(cfg.FP, 512)
    dw_upt = _mm(dup, xn2, ta=True, dims=(2 * cfg.FP, d, lp), tn=d, tm=tm_up, a_lead=True, name="mm_dw_up",
                 a_idx=lambda i, j, k: (i // (cfg.FP // tm_up), 0, i % (cfg.FP // tm_up)))
    up_send, up_recv, up_flying, up_token = sibling_start(dw_upt, "up")
    dxn2 = _mm(dup, w_upt, dims=(lp, d, 2 * cfg.FP), tk=tk_up, a_lead=True, name="mm_dxn2",
               a_idx=lambda i, j, k: (k // (cfg.FP // tk_up), i, k % (cfg.FP // tk_up)))
    dh1, dh1_b, dg_ffn = _rms_bwd(h1, w['ffn_norm'] + up_token[0:1, 0:1], dxn2, res=dh2, mask=True, with_bf16=True,
                                  name="rms_ffn_bwd")

    dyn = _mm(dh1_b, w_out, tb=True, name="mm_dyn")
    dw_out = _mm(yn, dh1_b, ta=True, tn=d, tm=512, name="mm_dw_out")
    up_done = _split_wait(up_send, up_recv, up_flying, _rs_sibling_copies, dw_out, name="rs_sibling_up_wait")
    dn_done = _split_wait(dn_send, dn_recv, dn_flying, _rs_sibling_copies, dw_out, name="rs_sibling_down_wait")
    early_parts, early_sends = _add_halves_all([up_done[0], dn_done[0]], [up_done[1], dn_done[1]], [BF16] * 2, "early")
    chip_lands = [lax.empty((3,) + s.shape[1:], s.dtype) for s in early_sends]
    ch_send, ch_recv, ch_flying, ch_token = _split_start(early_sends + chip_lands, _rs_chips_copies, 6,
                                                         name="rs_chips_early_start")
    dya, dg_ssm = _rms_bwd(ya, w['out_norm_ssm'] + ch_token[0:1, 0:1], (dyn, ds, 0), name="rms_ssm_bwd")
    do, dg_attn = _rms_bwd(o, w['out_norm_attn'], (dyn, cfg.DATTN, ds // cfg.DATTN), name="rms_attn_bwd")

    dqx, dkv, dkr = _attn_bwd(qx, kv, kr, o, lse, do, cfg, name="attn_bwd")
    dq_raw = _ew(_rope_heads(_unrope, cfg.H), [dqx, cos_t, sin_t], [], [(cfg.H * HEAD_SLOT, BF16)], name="unrope_q")[0]
    dk_pe = _ew(lambda rid, dk, cs, sn: _unrope(dk, cs, sn), [dkr, cos_t, sin_t], [], [(LANE, F32)], name="unrope_k")[0]
    dqn = _mm(dq_raw, w_qt, name="mm_dqn")
    dw_qt = _mm(dq_raw, qn, ta=True, tm=512, name="mm_dw_q")
    dkvn = _mm(dkv, w_kvt, name="mm_dkvn")
    dw_kvt = _mm(dkv, kvn, ta=True, tm=512, name="mm_dw_kv")
    dq_a, dg_q = _rms_bwd(q_a, w['q_a_norm'], dqn, name="rms_q_bwd")
    dkv_a, dg_kv = _rms_bwd(kv_a, w['kv_a_norm'], dkvn, name="rms_kv_bwd")

    def glu_bwd(rid, ycv, uv, tv, dyav, dk, bg):
        gelu = jax.nn.gelu(s5_y(ycv, uv, dk))
        sg = jax.nn.sigmoid(tv + bg)
        dt = dyav * gelu * sg * (1.0 - sg)
        return dt, dyav * sg, dt

    dt_b, dgl1, db_glu = _ew(glu_bwd, [yc, u, tg, dya], [d_skip, b_glu], [(ds, BF16), (ds, F32)], [ds], name="s5_glu_bwd")
    dgl = _mm(dt_b, w_glu, tb=True, res=dgl1, name="mm_dgl")
    dw_glu = _mm(gl, dt_b, ta=True, tm=512, name="mm_dw_glu")

    def gelu_bwd(rid, ycv, uv, dglv, dk):
        _, vjp = jax.vjp(jax.nn.gelu, s5_y(ycv, uv, dk))
        dy = vjp(dglv)[0]
        return dy, dy * dk, dy * uv

    dy_b, du_skip, dd_skip = _ew(gelu_bwd, [yc, u, dgl], [d_skip], [(ds, BF16), (ds, F32)], [ds], name="s5_gelu_bwd")
    du, dbb_band, dcc_band, da_l = _s5_bwd(dy_b, hs, z, bb_band, cc_band, a_l, du_skip, cfg, name="s5_bwd")

    dz = jnp.concatenate([du, dq_a, dkv_a, dk_pe], axis=1).astype(BF16)
    dxn = _mm(dz, w_in, tb=True, name="mm_dxn")
    dw_in = _mm(xn, dz, ta=True, tm=512, tn=_tile(cfg.DINP, 1024), name="mm_dw_in")
    def mix_bwd(rid, xv, dyv, resv, gv):
        dx, dg = _rms_bwd_block(xv, gv, dyv)
        dx = dx + resv
        return dx, dx, dg

    grad_x, dh0_head, dg_mix = _ew(mix_bwd, [h0, dxn, dh1], [mix_norm], [(d, F32, SKIP), (d, F32, FIRST)], [d],
                                   tm=PAD + N_META, name="rms_mix_bwd")
    grad_x = grad_x[None]

    da_re, da_im = _gp_from_lanes(da_l, cfg)
    dbb_re, dbb_im = _bb_from_band(dbb_band, cfg)
    dlam_re, dlam_im, dlog_dt, db_re, db_im = s5_vjp((da_re, da_im, dbb_re, dbb_im))
    dc_re, dc_im = _cc_from_band(dcc_band, cfg)
    local_small = {
        'meta_tokens': dh0_head[PAD:], 'mix_norm': dg_mix, 'lam_re': dlam_re, 'lam_im': dlam_im, 'log_dt': dlog_dt,
        'b_re': db_re, 'b_im': db_im, 'c_re': dc_re, 'c_im': dc_im, 'd_skip': dd_skip, 'b_glu': db_glu, 'q_a_norm': dg_q,
        'kv_a_norm': dg_kv, 'out_norm_ssm': dg_ssm, 'out_norm_attn': dg_attn, 'ffn_norm': dg_ffn,
        'conv_w': _ff_unpad(dconv_w, cfg), 'conv_b': _ff_unpad(dconv_b, cfg), 'final_norm': dg_final,
    }
    small_shapes = [local_small[n].shape for n in SMALL]

    rest_local = [dw_in, dw_glu, dw_qt, dw_kvt, dw_out, _pack([local_small[n] for n in SMALL])]
    rest_recv = _rs_sibling(rest_local, name="rs_sibling_rest")
    rest_parts, rest_sends = _add_halves_all(rest_local, rest_recv, [BF16] * 5 + [F32], "rest")
    ch_done = _split_wait(ch_send, ch_recv, ch_flying, _rs_chips_copies, rest_sends[0], name="rs_chips_early_wait")
    rest_others = _rs_chips(rest_sends, name="rs_chips_rest")
    red = _rs_finish(rest_parts + early_parts, list(rest_others) + list(ch_done[2:]), "grads")
    reduced = red[:5] + red[6:8] + [red[5]]
    small_full = _allgather([_place_shard(reduced[7], F32, name="place_small")], name="allgather_small")[0]
    small_sum = dict(zip(SMALL, _unpack(small_full, small_shapes)))

    padded_rows = ('w_down',)
    grads = {n: _from_comm_layout(n, g, cfg) for n, g in zip(BIG, reduced[:7]) if n not in padded_rows}
    for n in SMALL:
        g = small_sum[n]
        if n == 'meta_tokens':
            g = lax.dynamic_slice_in_dim(g, me * (d // 4), d // 4, axis=1)
        elif n == 'conv_w':
            g = lax.dynamic_slice_in_dim(g, me * (cfg.F // 4), cfg.F // 4, axis=1)[None]
        else:
            g = g.reshape(w[n].shape)
        grads[n] = g

    delta, new_m, new_v = {}, {}, {}
    for n, red in zip(BIG, reduced[:7]):
        shp = w[n].shape
        w2, m2, v2 = [t.reshape(shp[-2], shp[-1]) for t in (w[n], m[n], v[n])]
        if n in padded_rows:
            g2, dl, mn, vn = _adamw(w2, red, m2, v2, emit_grad=True, name=f"adamw_{n}")
            grads[n] = g2.reshape(shp)
        else:
            dl, mn, vn = _adamw(w2, grads[n].reshape(shp[-2], shp[-1]), m2, v2, name=f"adamw_{n}")
        delta[n], new_m[n], new_v[n] = dl.reshape(shp), mn.reshape(shp), vn.reshape(shp)
    shapes = [w[n].shape for n in SMALL]
    packs = [_pack([src[n] for n in SMALL]) for src in (w, grads, m, v)]
    for dst, p in zip((delta, new_m, new_v), _adamw(*packs, name="adamw_small")):
        dst.update(zip(SMALL, _unpack(p, shapes)))

    return (loss, grad_x, *[grads[n] for n in WEIGHTS], *[delta[n] for n in WEIGHTS],
            *[new_m[n] for n in WEIGHTS], *[new_v[n] for n in WEIGHTS])


def kernel(x, meta_tokens, mix_norm, w_in, lam_re, lam_im, log_dt, b_re, b_im, c_re, c_im, d_skip, w_glu, b_glu, q_a_norm, w_q_b, kv_a_norm, w_kv_b, out_norm_ssm, out_norm_attn, w_out, ffn_norm, w_up, conv_w, conv_b, w_down, final_norm, loss_target, m_meta_tokens, m_mix_norm, m_w_in, m_lam_re, m_lam_im, m_log_dt, m_b_re, m_b_im, m_c_re, m_c_im, m_d_skip, m_w_glu, m_b_glu, m_q_a_norm, m_w_q_b, m_kv_a_norm, m_w_kv_b, m_out_norm_ssm, m_out_norm_attn, m_w_out, m_ffn_norm, m_w_up, m_conv_w, m_conv_b, m_w_down, m_final_norm, v_meta_tokens, v_mix_norm, v_w_in, v_lam_re, v_lam_im, v_log_dt, v_b_re, v_b_im, v_c_re, v_c_im, v_d_skip, v_w_glu, v_b_glu, v_q_a_norm, v_w_q_b, v_kv_a_norm, v_w_kv_b, v_out_norm_ssm, v_out_norm_attn, v_w_out, v_ffn_norm, v_w_up, v_conv_w, v_conv_b, v_w_down, v_final_norm):
    args = dict(locals())
    w = {n: args[n] for n in WEIGHTS}
    m = {n: args["m_" + n] for n in WEIGHTS}
    v = {n: args["v_" + n] for n in WEIGHTS}
    return _step(PROD, w, m, v, x, loss_target)
```

```python
import functools
import math
from typing import NamedTuple

import jax
import jax.numpy as jnp
from jax import lax
from jax.experimental import pallas as pl
from jax.experimental.pallas import tpu as pltpu

F32, BF16 = jnp.float32, jnp.bfloat16
MESH = pl.DeviceIdType.MESH
LANE = 128
ROW_ALIGN = 16
N_META = 16
PAD = 112
CHUNK = 64
SSM_GROUP = 16
SSM_STATE = 64
GROUPS_PER_BLOCK = 8
QK_NOPE, QK_ROPE, V_HEAD = 128, 64, 128
HEAD_SLOT = 256
ROPE_BASE = 10000.0
EPS = 1e-6
ADAM_LR, ADAM_B1, ADAM_B2, ADAM_EPS, ADAM_WD, ADAM_STEP = 0.001, 0.9, 0.999, 1e-08, 0.01, 10
DT_F32_BLOCK_BYTES = 1 << 20
SKIP, FIRST = "skip", "first"


class Cfg(NamedTuple):
    D: int
    S: int
    DS: int
    H: int
    QL: int
    KVL: int
    F: int

    @property
    def LP(self):
        return PAD + N_META + self.S

    @property
    def G(self):
        return self.DS // SSM_GROUP

    @property
    def NB(self):
        return self.G // GROUPS_PER_BLOCK

    @property
    def NL(self):
        return 2 * self.G * SSM_STATE

    @property
    def DATTN(self):
        return self.H * V_HEAD

    @property
    def DMIX(self):
        return self.DS + self.DATTN

    @property
    def DIN(self):
        return self.DS + self.QL + self.KVL + QK_ROPE

    @property
    def DINP(self):
        return self.DS + self.QL + self.KVL + LANE

    @property
    def FQ(self):
        return -(-(self.F // 4) // LANE) * LANE

    @property
    def FP(self):
        return 4 * self.FQ


PROD = Cfg(D=2048, S=2048, DS=1024, H=8, QL=512, KVL=256, F=5504)

WEIGHTS = ['meta_tokens', 'mix_norm', 'w_in', 'lam_re', 'lam_im', 'log_dt', 'b_re', 'b_im', 'c_re', 'c_im', 'd_skip',
           'w_glu', 'b_glu', 'q_a_norm', 'w_q_b', 'kv_a_norm', 'w_kv_b', 'out_norm_ssm', 'out_norm_attn', 'w_out',
           'ffn_norm', 'w_up', 'conv_w', 'conv_b', 'w_down', 'final_norm']
BIG = ['w_in', 'w_glu', 'w_q_b', 'w_kv_b', 'w_out', 'w_up', 'w_down']
SMALL = [n for n in WEIGHTS if n not in BIG]


def _pc(body, **kw):
    return pl.pallas_call(body, **kw)


def _tile(n, target, align=LANE):
    best = None
    d = align
    while d <= min(n, target):
        if n % d == 0:
            best = d
        d += align
    return best if best is not None else n


def _row_tile(rows, cols):
    return _tile(rows, max(ROW_ALIGN, DT_F32_BLOCK_BYTES // (4 * cols)), ROW_ALIGN)


def _mm(a, b, *, name, ta=False, tb=False, tm=None, tn=512, tk=None, out_dtype=F32, res=None,
        a_idx=None, b_idx=None, dims=None, a_lead=False):
    if dims is None:
        m, k = (a.shape[1], a.shape[0]) if ta else a.shape
        n = b.shape[0] if tb else b.shape[1]
    else:
        m, n, k = dims
    tm = _tile(m, tm or m, LANE if ta else ROW_ALIGN)
    tn = _tile(n, tn)
    tk = _tile(k, tk or k, ROW_ALIGN if (ta and not tb) else LANE)
    nm, nn, nk = m // tm, n // tn, k // tk
    a_idx = a_idx or ((lambda i, j, kk: (kk, i)) if ta else (lambda i, j, kk: (i, kk)))
    b_idx = b_idx or ((lambda i, j, kk: (j, kk)) if tb else (lambda i, j, kk: (kk, j)))
    dn = (((0 if ta else 1,), (1 if tb else 0,)), ((), ()))

    def body(*refs):
        a_ref, b_ref = refs[0], refs[1]
        r_ref = refs[2] if res is not None else None
        o_ref = refs[3] if res is not None else refs[2]
        d = lax.dot_general(a_ref[...].astype(BF16), b_ref[...].astype(BF16), dn, preferred_element_type=F32)

        def finish(r):
            if r_ref is not None:
                r = r + r_ref[...].astype(F32)
            o_ref[...] = r.astype(out_dtype)

        if nk == 1:
            finish(d)
        else:
            acc = refs[-1]
            kk = pl.program_id(2)

            @pl.when(kk == 0)
            def _():
                acc[...] = d

            @pl.when(kk > 0)
            def _():
                acc[...] += d

            @pl.when(kk == nk - 1)
            def _():
                finish(acc[...])

    a_blk = ((None,) if a_lead else ()) + ((tk, tm) if ta else (tm, tk))
    in_specs = [pl.BlockSpec(a_blk, a_idx), pl.BlockSpec((tn, tk) if tb else (tk, tn), b_idx)]
    args = [a, b]
    if res is not None:
        in_specs.append(pl.BlockSpec((tm, tn), lambda i, j, kk: (i, j)))
        args.append(res)
    return _pc(body, name=name, grid=(nm, nn, nk), in_specs=in_specs,
               out_specs=pl.BlockSpec((tm, tn), lambda i, j, kk: (i, j)),
               out_shape=jax.ShapeDtypeStruct((m, n), out_dtype),
               scratch_shapes=[pltpu.VMEM((tm, tn), F32)] if nk > 1 else [],
               compiler_params=pltpu.CompilerParams(dimension_semantics=("parallel", "parallel", "arbitrary")))(*args)


def _ew(fn, ins, vecs, outs, sums=(), *, name, tm=None):
    ins = [x if isinstance(x, tuple) else (x, x.shape[1], 0) for x in ins]
    ins = [x if len(x) == 4 else x + (None,) for x in ins]
    outs = [o if len(o) == 3 else o + (None,) for o in outs]
    rows = ins[0][0].shape[0]
    cmax = max([c for _, c, _, _ in ins] + [c for c, _, _ in outs])
    tm = tm or _row_tile(rows, cmax)
    n_in, n_vec, n_out, n_sum = len(ins), len(vecs), len(outs), len(sums)

    def body(*refs):
        i = pl.program_id(0)
        rid = i * tm + lax.broadcasted_iota(jnp.int32, (tm, 1), 0)
        vals = [r[...] for r in refs[:n_in + n_vec]]
        res = fn(rid, *vals)
        res = res if isinstance(res, (tuple, list)) else (res,)
        o_refs = refs[n_in + n_vec:]
        for o_ref, r, (_, _, mode) in zip(o_refs[:n_out], res[:n_out], outs):
            if mode == FIRST:
                @pl.when(i == 0)
                def _():
                    o_ref[...] = r.astype(o_ref.dtype)
            else:
                o_ref[...] = r.astype(o_ref.dtype)
        for o_ref, r in zip(o_refs[n_out:], res[n_out:]):
            part = jnp.sum(r.astype(F32), axis=0, keepdims=True)

            @pl.when(i == 0)
            def _():
                o_ref[...] = part

            @pl.when(i > 0)
            def _():
                o_ref[...] += part

    def row_idx(mode):
        if mode == SKIP:
            return lambda i, cb=0: (jnp.maximum(i - 1, 0), cb)
        if mode == FIRST:
            return lambda i, cb=0: (0, cb)
        return lambda i, cb=0: (i, cb)

    in_specs = [pl.BlockSpec((tm, c), functools.partial(row_idx(mode), cb=cb)) for _, c, cb, mode in ins]
    in_specs += [pl.BlockSpec(v.shape, functools.partial(lambda i, nd: (0,) * nd, nd=v.ndim)) for v in vecs]
    out_specs = [pl.BlockSpec((tm, c), row_idx(mode)) for c, _, mode in outs]
    out_specs += [pl.BlockSpec((1, c), lambda i: (0, 0)) for c in sums]
    out_rows = {None: rows, SKIP: rows - tm, FIRST: tm}
    out_shape = [jax.ShapeDtypeStruct((out_rows[mode], c), dt) for c, dt, mode in outs]
    out_shape += [jax.ShapeDtypeStruct((1, c), F32) for c in sums]
    return _pc(body, name=name, grid=(rows // tm,), in_specs=in_specs, out_specs=out_specs, out_shape=out_shape,
               compiler_params=pltpu.CompilerParams(dimension_semantics=("arbitrary",)))(*[x[0] for x in ins], *vecs)


def _rms_parts(x, g):
    r = lax.rsqrt(jnp.mean(x * x, axis=-1, keepdims=True) + EPS)
    return x * r, r


def _rms_bwd_block(x, g, dy):
    xhat, r = _rms_parts(x, g)
    dxhat = dy * g
    dx = r * (dxhat - xhat * jnp.mean(dxhat * xhat, axis=-1, keepdims=True))
    return dx, dy * xhat


def _rms_fwd(x, g, *, name):
    c = x[1] if isinstance(x, tuple) else x.shape[1]
    return _ew(lambda rid, xv, gv: _rms_parts(xv.astype(F32), gv)[0] * gv, [x], [g], [(c, BF16)], name=name)[0]


def _rms_bwd(x, g, dy, *, name, res=None, mask=False, with_bf16=False):
    c = x[1] if isinstance(x, tuple) else x.shape[1]

    def fn(rid, xv, dyv, *rest):
        gv = rest[-1]
        dx, dg = _rms_bwd_block(xv.astype(F32), gv, dyv.astype(F32))
        if res is not None:
            dx = dx + rest[0]
        if mask:
            dx = jnp.where(rid >= PAD, dx, 0.0)
        return (dx, dx, dg) if with_bf16 else (dx, dg)

    ins = [x, dy] + ([res] if res is not None else [])
    outs = [(c, F32)] + ([(c, BF16)] if with_bf16 else [])
    return _ew(fn, ins, [g], outs, [c], name=name)


S5_W = GROUPS_PER_BLOCK * SSM_STATE
S5_GW = GROUPS_PER_BLOCK * SSM_GROUP
S5_UNROLL = 8
S5_DA_ROWS = 272


def _s5_scan_in_place(ref, a_ref, *, reverse):
    lp = ref.shape[0]
    ar = a_ref[:, :S5_W]
    ai = -a_ref[:, S5_W:] if reverse else a_ref[:, S5_W:]

    def step(n, carry):
        hr, hi = carry
        for q in range(S5_UNROLL):
            t = n * S5_UNROLL + q
            t = lp - 1 - t if reverse else t
            nr = ar * hr - ai * hi + ref[pl.ds(t, 1), :S5_W]
            ni = ar * hi + ai * hr + ref[pl.ds(t, 1), S5_W:]
            ref[pl.ds(t, 1), :S5_W] = nr
            ref[pl.ds(t, 1), S5_W:] = ni
            hr, hi = nr, ni
        return hr, hi

    z = jnp.zeros((1, S5_W), F32)
    lax.fori_loop(0, lp // S5_UNROLL, step, (z, z))


def _s5_fwd(z, bb_band, cc_band, a_l, cfg, *, name):
    lp, ds, nl = cfg.LP, cfg.DS, cfg.NL

    def body(u_ref, bb_ref, cc_ref, a_ref, hs_ref, y_ref):
        hs_ref[...] = jnp.dot(u_ref[...].astype(BF16), bb_ref[...], preferred_element_type=F32)
        _s5_scan_in_place(hs_ref, a_ref, reverse=False)
        y_ref[...] = jnp.dot(hs_ref[...].astype(BF16), cc_ref[...], preferred_element_type=F32)

    return _pc(body, name=name, grid=(cfg.NB,),
               in_specs=[pl.BlockSpec((lp, S5_GW), lambda j: (0, j)), pl.BlockSpec((S5_GW, 2 * S5_W), lambda j: (j, 0)),
                         pl.BlockSpec((2 * S5_W, S5_GW), lambda j: (j, 0)), pl.BlockSpec((1, 2 * S5_W), lambda j: (0, j))],
               out_specs=[pl.BlockSpec((lp, 2 * S5_W), lambda j: (0, j)), pl.BlockSpec((lp, S5_GW), lambda j: (0, j))],
               out_shape=[jax.ShapeDtypeStruct((lp, nl), F32), jax.ShapeDtypeStruct((lp, ds), F32)],
               compiler_params=pltpu.CompilerParams(dimension_semantics=("parallel",)))(z, bb_band, cc_band, a_l)


def _s5_bwd(dy, hs, z, bb_band, cc_band, a_l, du_skip, cfg, *, name):
    lp, ds, nl = cfg.LP, cfg.DS, cfg.NL
    nt = (((1,), (1,)), ((), ()))
    tn = (((0,), (0,)), ((), ()))

    def body(dy_ref, hs_ref, u_ref, bb_ref, cc_ref, a_ref, sk_ref, du_ref, dbb_ref, dcc_ref, da_ref, g_ref):
        dyv = dy_ref[...]
        g_ref[...] = lax.dot_general(dyv, cc_ref[...], nt, preferred_element_type=F32)
        _s5_scan_in_place(g_ref, a_ref, reverse=True)
        dcc_ref[...] = lax.dot_general(hs_ref[...].astype(BF16), dyv, tn, preferred_element_type=F32)
        gb = g_ref[...].astype(BF16)
        dbb_ref[...] = lax.dot_general(u_ref[...].astype(BF16), gb, tn, preferred_element_type=F32)
        du_ref[...] = lax.dot_general(gb, bb_ref[...], nt, preferred_element_type=F32) + sk_ref[...]
        dre = jnp.zeros((1, S5_W), F32)
        dim = jnp.zeros((1, S5_W), F32)
        for r0 in range(0, lp, S5_DA_ROWS):
            rows = min(S5_DA_ROWS, lp - r0)
            first = lax.broadcasted_iota(jnp.int32, (rows, 1), 0) == 0
            prev = hs_ref[r0 - 1:r0, :] if r0 else jnp.zeros((1, 2 * S5_W), F32)
            hr = jnp.where(first, prev[:, :S5_W], pltpu.roll(hs_ref[r0:r0 + rows, :S5_W], 1, 0))
            hi = jnp.where(first, prev[:, S5_W:], pltpu.roll(hs_ref[r0:r0 + rows, S5_W:], 1, 0))
            gr, gi = g_ref[r0:r0 + rows, :S5_W], g_ref[r0:r0 + rows, S5_W:]
            dre = dre + jnp.sum(gr * hr + gi * hi, axis=0, keepdims=True)
            dim = dim + jnp.sum(gi * hr - gr * hi, axis=0, keepdims=True)
        da_ref[:, :S5_W] = dre
        da_ref[:, S5_W:] = dim

    col_blk = pl.BlockSpec((lp, S5_GW), lambda j: (0, j))
    lane_blk = pl.BlockSpec((lp, 2 * S5_W), lambda j: (0, j))
    bb_blk = pl.BlockSpec((S5_GW, 2 * S5_W), lambda j: (j, 0))
    cc_blk = pl.BlockSpec((2 * S5_W, S5_GW), lambda j: (j, 0))
    a_blk = pl.BlockSpec((1, 2 * S5_W), lambda j: (0, j))
    return _pc(body, name=name, grid=(cfg.NB,),
               in_specs=[col_blk, lane_blk, col_blk, bb_blk, cc_blk, a_blk, col_blk],
               out_specs=[col_blk, bb_blk, cc_blk, a_blk],
               out_shape=[jax.ShapeDtypeStruct((lp, ds), F32), jax.ShapeDtypeStruct((ds, 2 * S5_W), F32),
                          jax.ShapeDtypeStruct((nl, S5_GW), F32), jax.ShapeDtypeStruct((1, nl), F32)],
               scratch_shapes=[pltpu.VMEM((lp, 2 * S5_W), F32)],
               compiler_params=pltpu.CompilerParams(dimension_semantics=("parallel",)))(dy, hs, z, bb_band, cc_band, a_l, du_skip)


def _conv_gate(pre, cw, cb):
    return cw[0:1] * pltpu.roll(pre, 2, 0) + cw[1:2] * pltpu.roll(pre, 1, 0) + cw[2:3] * pre + cb


def _conv_fwd(up, cw, cb, *, name):
    lp, fp2 = up.shape
    fp = fp2 // 2
    tc = _tile(fp, 256)
    nb = fp // tc

    def body(pre_ref, val_ref, cw_ref, cb_ref, o_ref):
        gate = _conv_gate(pre_ref[...], cw_ref[...], cb_ref[...])
        o_ref[...] = (jax.nn.silu(gate) * val_ref[...]).astype(BF16)

    return _pc(body, name=name, grid=(nb,),
               in_specs=[pl.BlockSpec((lp, tc), lambda j: (0, j)), pl.BlockSpec((lp, tc), lambda j: (0, nb + j)),
                         pl.BlockSpec((3, tc), lambda j: (0, j)), pl.BlockSpec((1, tc), lambda j: (0, j))],
               out_specs=pl.BlockSpec((lp, tc), lambda j: (0, j)),
               out_shape=jax.ShapeDtypeStruct((lp, fp), BF16),
               compiler_params=pltpu.CompilerParams(dimension_semantics=("parallel",)))(up, up, cw, cb)


def _conv_bwd(up, dact, cw, cb, *, name):
    lp, fp2 = up.shape
    fp = fp2 // 2
    tc = _tile(fp, 256)
    nb = fp // tc

    def body(pre_ref, val_ref, da_ref, cw_ref, cb_ref, dup_ref, dcw_ref, dcb_ref):
        pre, val, da, cwv = pre_ref[...], val_ref[...], da_ref[...].astype(F32), cw_ref[...]
        gate = _conv_gate(pre, cwv, cb_ref[...])
        sg = jax.nn.sigmoid(gate)
        dup_ref[1] = (da * (gate * sg)).astype(BF16)
        dgate = da * val * (sg * (1.0 + gate * (1.0 - sg)))
        dpre = cwv[2:3] * dgate + cwv[1:2] * pltpu.roll(dgate, lp - 1, 0) + cwv[0:1] * pltpu.roll(dgate, lp - 2, 0)
        dup_ref[0] = dpre.astype(BF16)
        dcb_ref[...] = jnp.sum(dgate, axis=0, keepdims=True)
        dcw_ref[0:1, :] = jnp.sum(dgate * pltpu.roll(pre, 2, 0), axis=0, keepdims=True)
        dcw_ref[1:2, :] = jnp.sum(dgate * pltpu.roll(pre, 1, 0), axis=0, keepdims=True)
        dcw_ref[2:3, :] = jnp.sum(dgate * pre, axis=0, keepdims=True)

    return _pc(body, name=name, grid=(nb,),
               in_specs=[pl.BlockSpec((lp, tc), lambda j: (0, j)), pl.BlockSpec((lp, tc), lambda j: (0, nb + j)),
                         pl.BlockSpec((lp, tc), lambda j: (0, j)),
                         pl.BlockSpec((3, tc), lambda j: (0, j)), pl.BlockSpec((1, tc), lambda j: (0, j))],
               out_specs=[pl.BlockSpec((2, lp, tc), lambda j: (0, 0, j)),
                          pl.BlockSpec((3, tc), lambda j: (0, j)), pl.BlockSpec((1, tc), lambda j: (0, j))],
               out_shape=[jax.ShapeDtypeStruct((2, lp, fp), BF16), jax.ShapeDtypeStruct((3, fp), F32),
                          jax.ShapeDtypeStruct((1, fp), F32)],
               compiler_params=pltpu.CompilerParams(dimension_semantics=("parallel",)))(up, up, dact, cw, cb)


def _key_limit(i, tq, lp):
    return min(lp, -(-((i + 1) * tq) // LANE) * LANE)


def _attn_mask(i, tq, nk):
    qrow = i * tq + lax.broadcasted_iota(jnp.int32, (tq, 1), 0)
    krow = lax.broadcasted_iota(jnp.int32, (1, nk), 1)
    return (krow >= PAD) & ((krow // CHUNK) <= (qrow // CHUNK)), qrow >= PAD


def _attn_scores(q, kn, kr, i, tq, scale):
    nt = (((1,), (1,)), ((), ()))
    s = lax.dot_general(q[:, :QK_NOPE], kn, nt, preferred_element_type=F32)
    s = s + lax.dot_general(q[:, QK_NOPE:], kr, nt, preferred_element_type=F32)
    mask, qvalid = _attn_mask(i, tq, kn.shape[0])
    return jnp.where(mask, s * scale, jnp.finfo(F32).min), qvalid


def _per_q_block(nq, fn):
    i = pl.program_id(1)
    for blk in range(nq):
        pl.when(i == blk)(functools.partial(fn, blk))


def _attn_fwd(qx, kv, kr, cfg, *, name):
    lp, h = cfg.LP, cfg.H
    tq = _tile(lp, 272, ROW_ALIGN)
    nq = lp // tq
    scale = 1.0 / math.sqrt(QK_NOPE + QK_ROPE)

    def body(q_ref, kn_ref, v_ref, kr_ref, o_ref, lse_ref):
        def block(blk):
            nk = _key_limit(blk, tq, lp)
            s, qvalid = _attn_scores(q_ref[...], kn_ref[:nk], kr_ref[:nk], blk, tq, scale)
            m = jnp.max(s, axis=-1, keepdims=True)
            p = jnp.exp(s - m)
            l = jnp.sum(p, axis=-1, keepdims=True)
            o = jnp.dot(p.astype(BF16), v_ref[:nk], preferred_element_type=F32) / l
            o_ref[...] = jnp.where(qvalid, o, 0.0)
            lse_ref[...] = m + jnp.log(l)

        _per_q_block(nq, block)

    return _pc(body, name=name, grid=(h, nq),
               in_specs=[pl.BlockSpec((tq, HEAD_SLOT), lambda hh, i: (i, hh)),
                         pl.BlockSpec((lp, QK_NOPE), lambda hh, i: (0, 2 * hh)),
                         pl.BlockSpec((lp, V_HEAD), lambda hh, i: (0, 2 * hh + 1)),
                         pl.BlockSpec((lp, LANE), lambda hh, i: (0, 0))],
               out_specs=[pl.BlockSpec((tq, V_HEAD), lambda hh, i: (i, hh)),
                          pl.BlockSpec((None, tq, 1), lambda hh, i: (hh, i, 0))],
               out_shape=[jax.ShapeDtypeStruct((lp, h * V_HEAD), F32), jax.ShapeDtypeStruct((h, lp, 1), F32)],
               compiler_params=pltpu.CompilerParams(dimension_semantics=("parallel", "parallel")))(qx, kv, kv, kr)


def _attn_bwd(qx, kv, kr, o, lse, do, cfg, *, name):
    lp, h = cfg.LP, cfg.H
    tq = _tile(lp, 272, ROW_ALIGN)
    nq = lp // tq
    scale = 1.0 / math.sqrt(QK_NOPE + QK_ROPE)
    tn_dims = (((0,), (0,)), ((), ()))

    def body(q_ref, kn_ref, v_ref, kr_ref, o_ref, lse_ref, do_ref, dq_ref, dkv_ref, dkr_ref, dkv_acc):
        hh, i = pl.program_id(0), pl.program_id(1)

        @pl.when(i == 0)
        def _():
            dkv_acc[...] = jnp.zeros_like(dkv_acc)

        @pl.when((i == 0) & (hh == 0))
        def _():
            dkr_ref[...] = jnp.zeros_like(dkr_ref)

        def block(blk):
            nk = _key_limit(blk, tq, lp)
            q, kn, v, krv = q_ref[...], kn_ref[:nk], v_ref[:nk], kr_ref[:nk]
            s, qvalid = _attn_scores(q, kn, krv, blk, tq, scale)
            dov = jnp.where(qvalid, do_ref[...], 0.0)
            p = jnp.exp(s - lse_ref[...])
            delta = jnp.sum(dov * o_ref[...], axis=-1, keepdims=True)
            dob = dov.astype(BF16)
            dp = lax.dot_general(dob, v, (((1,), (1,)), ((), ())), preferred_element_type=F32)
            ds = (p * (dp - delta) * scale).astype(BF16)
            dq_ref[:, :QK_NOPE] = jnp.dot(ds, kn, preferred_element_type=F32)
            dq_ref[:, QK_NOPE:] = jnp.dot(ds, krv, preferred_element_type=F32)
            dkv_acc[:nk, :QK_NOPE] += lax.dot_general(ds, q[:, :QK_NOPE], tn_dims, preferred_element_type=F32)
            dkv_acc[:nk, QK_NOPE:] += lax.dot_general(p.astype(BF16), dob, tn_dims, preferred_element_type=F32)
            dkr_ref[:nk, :] += lax.dot_general(ds, q[:, QK_NOPE:], tn_dims, preferred_element_type=F32)

        _per_q_block(nq, block)

        @pl.when(i == nq - 1)
        def _():
            dkv_ref[...] = dkv_acc[...].astype(BF16)

    return _pc(body, name=name, grid=(h, nq),
               in_specs=[pl.BlockSpec((tq, HEAD_SLOT), lambda hh, i: (i, hh)),
                         pl.BlockSpec((lp, QK_NOPE), lambda hh, i: (0, 2 * hh)),
                         pl.BlockSpec((lp, V_HEAD), lambda hh, i: (0, 2 * hh + 1)),
                         pl.BlockSpec((lp, LANE), lambda hh, i: (0, 0)),
                         pl.BlockSpec((tq, V_HEAD), lambda hh, i: (i, hh)),
                         pl.BlockSpec((None, tq, 1), lambda hh, i: (hh, i, 0)),
                         pl.BlockSpec((tq, V_HEAD), lambda hh, i: (i, hh))],
               out_specs=[pl.BlockSpec((tq, HEAD_SLOT), lambda hh, i: (i, hh)),
                          pl.BlockSpec((lp, QK_NOPE + V_HEAD), lambda hh, i: (0, hh)),
                          pl.BlockSpec((lp, LANE), lambda hh, i: (0, 0))],
               out_shape=[jax.ShapeDtypeStruct((lp, h * HEAD_SLOT), F32),
                          jax.ShapeDtypeStruct((lp, h * (QK_NOPE + V_HEAD)), BF16),
                          jax.ShapeDtypeStruct((lp, LANE), F32)],
               scratch_shapes=[pltpu.VMEM((lp, QK_NOPE + V_HEAD), F32)],
               compiler_params=pltpu.CompilerParams(dimension_semantics=("arbitrary", "arbitrary")))(qx, kv, kv, kr, o, lse, do)


def _rot_half(x):
    lane = lax.broadcasted_iota(jnp.int32, x.shape, 1)
    half = QK_ROPE // 2
    return jnp.where(lane < half, -pltpu.roll(x, LANE - half, 1), pltpu.roll(x, half, 1))


def _rope(x, cos, sin):
    return x * cos + _rot_half(x) * sin


def _unrope(dy, cos, sin):
    return dy * cos - _rot_half(dy * sin)


def _rope_heads(fn, h):
    def apply(rid, q, cos, sin):
        parts = []
        for hh in range(h):
            parts.append(q[:, hh * HEAD_SLOT: hh * HEAD_SLOT + QK_NOPE])
            parts.append(fn(q[:, hh * HEAD_SLOT + QK_NOPE: (hh + 1) * HEAD_SLOT], cos, sin))
        return jnp.concatenate(parts, axis=1)
    return apply


ANY = pl.BlockSpec(memory_space=pl.ANY)


def _place():
    x, y, c = lax.axis_index("x"), lax.axis_index("y"), lax.axis_index("c")
    chips = [(1 - x, y), (x, 1 - y), (1 - x, 1 - y)]
    return x, y, c, chips


def _rcopy(src, dst, send_sem, recv_sem, dev):
    return pltpu.make_async_remote_copy(src_ref=src, dst_ref=dst, send_sem=send_sem, recv_sem=recv_sem,
                                        device_id=dev, device_id_type=MESH)


def _place_shard(shard, dtype, *, name):
    r, cols = shard.shape
    tm = _row_tile(r, cols)
    nblk = r // tm
    me = (2 * lax.axis_index("x") + lax.axis_index("y")).astype(jnp.int32).reshape(1)

    def body(me_ref, s_ref, o_ref):
        o_ref[...] = s_ref[...].astype(dtype)

    return _pc(body, name=name,
               grid_spec=pltpu.PrefetchScalarGridSpec(
                   num_scalar_prefetch=1, grid=(nblk,),
                   in_specs=[pl.BlockSpec((tm, cols), lambda i, mr: (i, 0))],
                   out_specs=pl.BlockSpec((tm, cols), lambda i, mr: (mr[0] * nblk + i, 0))),
               out_shape=jax.ShapeDtypeStruct((4 * r, cols), dtype),
               compiler_params=pltpu.CompilerParams(dimension_semantics=("arbitrary",)))(me, shard)


def _allgather(fulls, *, name):
    n = len(fulls)

    def body(*refs):
        outs = refs[n:2 * n]
        send_sems, recv_sems = refs[2 * n:]
        x, y, c, chips = _place()
        sib = (x, y, 1 - c)
        me = 2 * x + y

        def rows(t, s, half):
            hrows = outs[t].shape[0] // 8
            return outs[t].at[pl.ds((2 * s + half) * hrows, hrows)]

        sent = []
        for t in range(n):
            for j, (cx, cy) in enumerate(chips):
                cp = _rcopy(rows(t, me, c), rows(t, me, c), send_sems.at[6 * t + j], recv_sems.at[6 * t + j], (cx, cy, c))
                cp.start()
                sent.append(cp)
        for t in range(n):
            for j, (cx, cy) in enumerate(chips):
                landed = rows(t, 2 * cx + cy, c)
                _rcopy(landed, landed, send_sems.at[6 * t + j], recv_sems.at[6 * t + j], (cx, cy, c)).wait_recv()
                cp = _rcopy(landed, landed, send_sems.at[6 * t + 3 + j], recv_sems.at[6 * t + 3 + j], sib)
                cp.start()
                sent.append(cp)
        for t in range(n):
            for j, (cx, cy) in enumerate(chips):
                other = rows(t, 2 * cx + cy, 1 - c)
                _rcopy(other, other, send_sems.at[6 * t + 3 + j], recv_sems.at[6 * t + 3 + j], sib).wait_recv()
        for cp in sent:
            cp.wait_send()

    return _pc(body, name=name, in_specs=[ANY] * n, out_specs=[ANY] * n,
               out_shape=[jax.ShapeDtypeStruct(f.shape, f.dtype) for f in fulls],
               input_output_aliases={t: t for t in range(n)},
               scratch_shapes=[pltpu.SemaphoreType.DMA((6 * n,)), pltpu.SemaphoreType.DMA((6 * n,))])(*fulls)


HBM = pl.BlockSpec(memory_space=pltpu.HBM)
SEM = pl.BlockSpec(memory_space=pltpu.SEMAPHORE)
EFFECT = pltpu.SideEffectType.DATAFLOW_SIDE_EFFECTING
TOKEN = jax.ShapeDtypeStruct((8, LANE), F32)


def _in_hbm(a):
    return pltpu.with_memory_space_constraint(a, pltpu.HBM)


def _half_rows(ref, s, half):
    hrows = ref.shape[0] // 8
    return ref.at[pl.ds((2 * s + half) * hrows, hrows)]


def _split_start(bufs, copies, n_copies, *, name, before=None):
    n = len(bufs)
    extra = [] if before is None else [before]

    def body(*refs):
        send_sems, recv_sems, token = refs[n + len(extra)], refs[n + len(extra) + 1], refs[-1]
        for k, (src, dst, dev) in enumerate(copies(refs[:n])):
            _rcopy(src, dst, send_sems.at[k], recv_sems.at[k], dev).start()
        token[...] = jnp.zeros_like(token)

    res = _pc(body, name=name, in_specs=[HBM] * n + [ANY] * len(extra),
              out_specs=[SEM, SEM] + [HBM] * n + [pl.BlockSpec(memory_space=pltpu.VMEM)],
              out_shape=[pltpu.SemaphoreType.DMA((n_copies,)), pltpu.SemaphoreType.DMA((n_copies,))]
              + [pltpu.HBM(b.shape, b.dtype) for b in bufs] + [TOKEN],
              input_output_aliases={t: 2 + t for t in range(n)},
              compiler_params=pltpu.CompilerParams(has_side_effects=EFFECT))(*[_in_hbm(b) for b in bufs], *extra)
    return res[0], res[1], list(res[2:2 + n]), res[-1]


def _split_wait(send_sems, recv_sems, bufs, copies, after, *, name):
    n = len(bufs)

    def body(*refs):
        send_ref, recv_ref = refs[n], refs[n + 1]
        for k, (src, dst, dev) in enumerate(copies(refs[:n])):
            cp = _rcopy(src, dst, send_ref.at[k], recv_ref.at[k], dev)
            cp.wait_send()
            cp.wait_recv()

    return _pc(body, name=name, in_specs=[HBM] * n + [SEM, SEM, ANY], out_specs=[HBM] * n,
               out_shape=[pltpu.HBM(b.shape, b.dtype) for b in bufs],
               input_output_aliases={t: t for t in range(n)},
               compiler_params=pltpu.CompilerParams(has_side_effects=EFFECT))(*bufs, send_sems, recv_sems, after)


def _allgather_ici_copies(refs):
    x, y, c, chips = _place()
    return [(_half_rows(r, 2 * x + y, c), _half_rows(r, 2 * x + y, c), (cx, cy, c)) for r in refs for cx, cy in chips]


def _rs_chips_copies(refs):
    x, y, c, chips = _place()
    n = len(refs) // 2
    return [(refs[t].at[2 * cx + cy], refs[n + t].at[j], (cx, cy, c)) for t in range(n) for j, (cx, cy) in enumerate(chips)]


def _rs_sibling_copies(refs):
    x, y, c, _ = _place()
    n = len(refs) // 2
    out = []
    for t in range(n):
        h = refs[t].shape[0] // 8
        out += [(refs[t].at[pl.ds((2 * s + 1 - c) * h, h)], refs[n + t].at[s], (x, y, 1 - c)) for s in range(4)]
    return out


def _allgather_forward(fulls, *, name):
    n = len(fulls)

    def body(*refs):
        outs = refs[n:2 * n]
        send_sems, recv_sems = refs[2 * n:]
        x, y, c, chips = _place()
        sent = []
        for t in range(n):
            for j, (cx, cy) in enumerate(chips):
                landed = _half_rows(outs[t], 2 * cx + cy, c)
                cp = _rcopy(landed, landed, send_sems.at[3 * t + j], recv_sems.at[3 * t + j], (x, y, 1 - c))
                cp.start()
                sent.append(cp)
        for t in range(n):
            for j, (cx, cy) in enumerate(chips):
                other = _half_rows(outs[t], 2 * cx + cy, 1 - c)
                _rcopy(other, other, send_sems.at[3 * t + j], recv_sems.at[3 * t + j], (x, y, 1 - c)).wait_recv()
        for cp in sent:
            cp.wait_send()

    return _pc(body, name=name, in_specs=[ANY] * n, out_specs=[ANY] * n,
               out_shape=[jax.ShapeDtypeStruct(f.shape, f.dtype) for f in fulls],
               input_output_aliases={t: t for t in range(n)},
               scratch_shapes=[pltpu.SemaphoreType.DMA((3 * n,)), pltpu.SemaphoreType.DMA((3 * n,))])(*fulls)


def _rs_sibling(grads, *, name):
    n = len(grads)

    def body(*refs):
        ins, outs = refs[:n], refs[n:2 * n]
        send_sems, recv_sems = refs[2 * n:]
        x, y, c, _ = _place()
        cps = []
        for t in range(n):
            h = ins[t].shape[0] // 8
            for s in range(4):
                cp = _rcopy(ins[t].at[pl.ds((2 * s + 1 - c) * h, h)], outs[t].at[s], send_sems.at[4 * t + s],
                            recv_sems.at[4 * t + s], (x, y, 1 - c))
                cp.start()
                cps.append(cp)
        for cp in cps:
            cp.wait()

    return _pc(body, name=name, in_specs=[ANY] * n, out_specs=[ANY] * n,
               out_shape=[jax.ShapeDtypeStruct((4, g.shape[0] // 8, g.shape[1]), g.dtype) for g in grads],
               scratch_shapes=[pltpu.SemaphoreType.DMA((4 * n,)), pltpu.SemaphoreType.DMA((4 * n,))])(*grads)


def _rs_chips(sends, *, name):
    n = len(sends)

    def body(*refs):
        s_refs, b_refs = refs[:n], refs[n:2 * n]
        send_sems, recv_sems = refs[2 * n:]
        x, y, c, chips = _place()
        cps = []
        for t in range(n):
            for j, (cx, cy) in enumerate(chips):
                cp = _rcopy(s_refs[t].at[2 * cx + cy], b_refs[t].at[j], send_sems.at[3 * t + j], recv_sems.at[3 * t + j],
                            (cx, cy, c))
                cp.start()
                cps.append(cp)
        for cp in cps:
            cp.wait()

    return _pc(body, name=name, in_specs=[ANY] * n, out_specs=[ANY] * n,
               out_shape=[jax.ShapeDtypeStruct((3,) + s.shape[1:], s.dtype) for s in sends],
               scratch_shapes=[pltpu.SemaphoreType.DMA((3 * n,)), pltpu.SemaphoreType.DMA((3 * n,))])(*sends)


def _rs_final(fulls, *, name):
    n = len(fulls)

    def body(*refs):
        outs = refs[n:2 * n]
        send_sems, recv_sems = refs[2 * n:]
        x, y, c, _ = _place()
        cps = []
        for t in range(n):
            cp = _rcopy(outs[t].at[c], outs[t].at[c], send_sems.at[t], recv_sems.at[t], (x, y, 1 - c))
            cp.start()
            cps.append(cp)
        for cp in cps:
            cp.wait()

    return _pc(body, name=name, in_specs=[ANY] * n, out_specs=[ANY] * n,
               out_shape=[jax.ShapeDtypeStruct(f.shape, f.dtype) for f in fulls],
               input_output_aliases={t: t for t in range(n)},
               scratch_shapes=[pltpu.SemaphoreType.DMA((n,)), pltpu.SemaphoreType.DMA((n,))])(*fulls)


def _add_halves(g, a, send_dtype, *, name):
    _, h, cols = a.shape
    th = _row_tile(h, cols)
    g4 = g.reshape(4, 2, h, cols)
    c = lax.axis_index("c").astype(jnp.int32).reshape(1)

    def body(c_ref, g_ref, a_ref, p_ref, s_ref):
        v = g_ref[...] + a_ref[...]
        p_ref[...] = v
        s_ref[...] = v.astype(send_dtype)

    return _pc(body, name=name,
               grid_spec=pltpu.PrefetchScalarGridSpec(
                   num_scalar_prefetch=1, grid=(4, h // th),
                   in_specs=[pl.BlockSpec((None, None, th, cols), lambda s, i, cr: (s, cr[0], i, 0)),
                             pl.BlockSpec((None, th, cols), lambda s, i, cr: (s, i, 0))],
                   out_specs=[pl.BlockSpec((None, th, cols), lambda s, i, cr: (s, i, 0))] * 2),
               out_shape=[jax.ShapeDtypeStruct(a.shape, F32), jax.ShapeDtypeStruct(a.shape, send_dtype)],
               compiler_params=pltpu.CompilerParams(dimension_semantics=("arbitrary", "arbitrary")))(c, g4, a)


def _add_chips(p, b, *, name, order=None):
    _, h, cols = p.shape
    th = _row_tile(h, cols)
    idx = jnp.stack([2 * lax.axis_index("x") + lax.axis_index("y"), lax.axis_index("c")]).astype(jnp.int32)
    extra = [] if order is None else [order]

    def body(idx_ref, p_ref, b_ref, *rest):
        r_ref = rest[-1]
        r_ref[...] = ((p_ref[...] + b_ref[0].astype(F32)) + b_ref[1].astype(F32)) + b_ref[2].astype(F32)

    return _pc(body, name=name,
               grid_spec=pltpu.PrefetchScalarGridSpec(
                   num_scalar_prefetch=1, grid=(h // th,),
                   in_specs=[pl.BlockSpec((None, th, cols), lambda i, ir: (ir[0], i, 0)),
                             pl.BlockSpec((3, th, cols), lambda i, ir: (0, i, 0))] + [ANY] * len(extra),
                   out_specs=pl.BlockSpec((None, th, cols), lambda i, ir: (ir[1], i, 0))),
               out_shape=jax.ShapeDtypeStruct((2, h, cols), F32),
               compiler_params=pltpu.CompilerParams(dimension_semantics=("arbitrary",)))(idx, p, b, *extra)


def _add_halves_all(grads, recv, send_dtypes, tag):
    parts, sends = [], []
    for t, (g, a) in enumerate(zip(grads, recv)):
        p, s = _add_halves(g, a, send_dtypes[t], name=f"rs_add_halves_{tag}{t}")
        parts.append(p)
        sends.append(s)
    return parts, sends


def _rs_finish(parts, others, tag, order=None):
    halves = [_add_chips(p, b, order=order, name=f"rs_add_chips_{tag}{t}") for t, (p, b) in enumerate(zip(parts, others))]
    full = _rs_final(halves, name=f"rs_final_{tag}")
    return [f.reshape(-1, f.shape[-1]) for f in full]


def _s5_discretize(lam_re, lam_im, log_dt, b_re, b_im):
    lam = lax.complex(lam_re, lam_im)
    dt = jnp.exp(log_dt)[:, None]
    lam_bar = jnp.exp(lam * dt)
    b_bar = ((lam_bar - 1.0) / lam)[..., None] * lax.complex(b_re, b_im)
    return jnp.real(lam_bar), jnp.imag(lam_bar), jnp.real(b_bar), jnp.imag(b_bar)


def _lanes_from_gp(re, im, cfg):
    v = jnp.stack([re, im]).reshape(2, cfg.NB, GROUPS_PER_BLOCK, SSM_STATE)
    return jnp.transpose(v, (1, 0, 2, 3)).reshape(1, cfg.NL)


def _gp_from_lanes(v, cfg):
    v = jnp.transpose(v.reshape(cfg.NB, 2, GROUPS_PER_BLOCK, SSM_STATE), (1, 0, 2, 3)).reshape(2, cfg.G, SSM_STATE)
    return v[0], v[1]


def _bb_band(bb_re, bb_im, cfg):
    eye = jnp.eye(GROUPS_PER_BLOCK, dtype=F32)
    bb = jnp.stack([bb_re, bb_im]).reshape(2, cfg.NB, GROUPS_PER_BLOCK, SSM_STATE, SSM_GROUP)
    return jnp.einsum('rjgpc,gh->jgcrhp', bb, eye).reshape(cfg.DS, 2 * GROUPS_PER_BLOCK * SSM_STATE)


def _bb_from_band(m, cfg):
    eye = jnp.eye(GROUPS_PER_BLOCK, dtype=F32)
    m = m.reshape(cfg.NB, GROUPS_PER_BLOCK, SSM_GROUP, 2, GROUPS_PER_BLOCK, SSM_STATE)
    v = jnp.einsum('jgcrhp,gh->rjgpc', m, eye).reshape(2, cfg.G, SSM_STATE, SSM_GROUP)
    return v[0], v[1]


def _cc_band(c_re, c_im, cfg):
    eye = jnp.eye(GROUPS_PER_BLOCK, dtype=F32)
    cc = jnp.stack([c_re, -c_im]).reshape(2, cfg.NB, GROUPS_PER_BLOCK, SSM_GROUP, SSM_STATE)
    return jnp.einsum('rjgcp,gh->jrhpgc', cc, eye).reshape(cfg.NL, GROUPS_PER_BLOCK * SSM_GROUP)


def _cc_from_band(m, cfg):
    eye = jnp.eye(GROUPS_PER_BLOCK, dtype=F32)
    m = m.reshape(cfg.NB, 2, GROUPS_PER_BLOCK, SSM_STATE, GROUPS_PER_BLOCK, SSM_GROUP)
    v = jnp.einsum('jrhpgc,gh->rjgcp', m, eye).reshape(2, cfg.G, SSM_GROUP, SSM_STATE)
    return v[0], -v[1]


PACK_COLS = 512
PACK_ROW_ALIGN = 64


def _pack(arrs):
    flat = jnp.concatenate([a.reshape(-1).astype(F32) for a in arrs])
    unit = PACK_COLS * PACK_ROW_ALIGN
    total = -(-flat.shape[0] // unit) * unit
    return jnp.pad(flat, (0, total - flat.shape[0])).reshape(-1, PACK_COLS)


def _unpack(p, shapes):
    flat = p.reshape(-1)
    out, off = [], 0
    for shp in shapes:
        size = math.prod(shp)
        out.append(flat[off:off + size].reshape(shp))
        off += size
    return out


def _adamw(w, g, m, v, *, name, emit_grad=False):
    c1 = 1.0 / (1.0 - ADAM_B1 ** ADAM_STEP)
    c2 = 1.0 / (1.0 - ADAM_B2 ** ADAM_STEP)

    def fn(rid, wv, gv, mv, vv):
        mn = ADAM_B1 * mv + (1.0 - ADAM_B1) * gv
        vn = ADAM_B2 * vv + (1.0 - ADAM_B2) * (gv * gv)
        delta = -ADAM_LR * ((mn * c1) / (jnp.sqrt(vn * c2) + ADAM_EPS) + ADAM_WD * wv)
        return (gv, delta, mn, vn) if emit_grad else (delta, mn, vn)

    cols = w.shape[1]
    return _ew(fn, [w, g, m, v], [], [(cols, F32)] * (4 if emit_grad else 3), name=name)


def _to_comm_layout(name, w, cfg):
    w = w[0]
    if name == 'w_in':
        return jnp.pad(w, ((0, 0), (0, cfg.DINP - cfg.DIN)))
    if name == 'w_q_b':
        hs = w.shape[1] // (QK_NOPE + QK_ROPE)
        wt = w.T.reshape(hs, QK_NOPE + QK_ROPE, cfg.QL)
        return jnp.pad(wt, ((0, 0), (0, HEAD_SLOT - QK_NOPE - QK_ROPE), (0, 0))).reshape(hs * HEAD_SLOT, cfg.QL)
    if name == 'w_kv_b':
        return w.T
    if name == 'w_up':
        wt = w.T.reshape(2, cfg.F // 4, cfg.D)
        return jnp.pad(wt, ((0, 0), (0, cfg.FQ - cfg.F // 4), (0, 0))).reshape(2 * cfg.FQ, cfg.D)
    if name == 'w_down':
        return jnp.pad(w, ((0, cfg.FQ - cfg.F // 4), (0, 0)))
    return w


def _from_comm_layout(name, g, cfg):
    if name == 'w_in':
        g = g[:, :cfg.DIN]
    elif name == 'w_q_b':
        hs = g.shape[0] // HEAD_SLOT
        g = g.reshape(hs, HEAD_SLOT, cfg.QL)[:, :QK_NOPE + QK_ROPE].reshape(hs * (QK_NOPE + QK_ROPE), cfg.QL).T
    elif name == 'w_kv_b':
        g = g.T
    elif name == 'w_up':
        g = g.reshape(2, cfg.FQ, cfg.D)[:, :cfg.F // 4].reshape(cfg.F // 2, cfg.D).T
    elif name == 'w_down':
        g = g[:cfg.F // 4]
    return g[None]


def _ff_pad(v, cfg):
    k = v.shape[0]
    return jnp.pad(v.reshape(k, 4, cfg.F // 4), ((0, 0), (0, 0), (0, cfg.FQ - cfg.F // 4))).reshape(k, cfg.FP)


def _ff_unpad(v, cfg):
    k = v.shape[0]
    return v.reshape(k, 4, cfg.FQ)[:, :, :cfg.F // 4].reshape(k, cfg.F)


def _step(cfg, w, m, v, x, loss_target):
    lp, d, ds, nl = cfg.LP, cfg.D, cfg.DS, cfg.NL
    xi, yi = lax.axis_index("x"), lax.axis_index("y")
    me = 2 * xi + yi

    placed = [_place_shard(_to_comm_layout(n, w[n], cfg), BF16, name=f"place_{n}") for n in BIG]
    conv_w_shard = jnp.pad(w['conv_w'][0], ((0, ROW_ALIGN - 3), (0, cfg.FQ - cfg.F // 4)))
    placed += [_place_shard(w['meta_tokens'], F32, name="place_meta"), _place_shard(conv_w_shard, F32, name="place_conv_w")]
    w_in, meta_full = _allgather([placed[0], placed[7]], name="allgather_first")
    meta = jnp.transpose(meta_full.reshape(4, N_META, d // 4), (1, 0, 2)).reshape(N_META, d)
    conv_b = _ff_pad(w['conv_b'], cfg)
    mid = placed[1:5] + [placed[8]]
    mid_send, mid_recv, mid_flying, mid_token = _split_start(mid, _allgather_ici_copies, 3 * len(mid), before=meta_full,
                                                             name="allgather_mid_start")
    ffn_send, ffn_recv, ffn_flying, ffn_token = _split_start(placed[5:7], _allgather_ici_copies, 6, before=mid_token,
                                                             name="allgather_ffn_start")
    mix_norm = w['mix_norm'] + (mid_token[0:1, 0:1] + ffn_token[0:1, 0:1])

    pos = (jnp.arange(lp, dtype=jnp.int32) - PAD).astype(F32)
    inv_freq = 1.0 / (ROPE_BASE ** (jnp.arange(0, QK_ROPE, 2, dtype=F32) / QK_ROPE))
    ang = pos[:, None] * inv_freq[None, :]
    zpad = jnp.zeros((lp, LANE - QK_ROPE), F32)
    cos_t = jnp.concatenate([jnp.cos(ang), jnp.cos(ang), zpad], axis=1)
    sin_t = jnp.concatenate([jnp.sin(ang), jnp.sin(ang), zpad], axis=1)

    s5_in = (w['lam_re'][0], w['lam_im'][0], w['log_dt'][0], w['b_re'][0], w['b_im'][0])
    (a_re, a_im, bb_re, bb_im), s5_vjp = jax.vjp(_s5_discretize, *s5_in)
    a_l = _lanes_from_gp(a_re, a_im, cfg)
    bb_band = _bb_band(bb_re, bb_im, cfg).astype(BF16)
    cc_band = _cc_band(w['c_re'][0], w['c_im'][0], cfg).astype(BF16)
    d_skip, b_glu = w['d_skip'], w['b_glu']

    h0 = jnp.concatenate([jnp.zeros((PAD, d), F32), meta, x[0]], axis=0)
    xn = _rms_fwd(h0, mix_norm, name="rms_mix")
    z = _mm(xn, w_in, name="mm_in", tn=_tile(cfg.DINP, 640))
    u = (z, ds, 0)
    q_a = (z, cfg.QL, ds // cfg.QL)
    kv_a = (z, cfg.KVL, (ds + cfg.QL) // cfg.KVL)
    k_pe = (z, LANE, (ds + cfg.QL + cfg.KVL) // LANE)

    hs, yc = _s5_fwd(z, bb_band, cc_band, a_l, cfg, name="s5_fwd")

    def s5_y(ycv, uv, dk):
        return ycv + dk * uv

    gl = _ew(lambda rid, ycv, uv, dk: jax.nn.gelu(s5_y(ycv, uv, dk)), [yc, u], [d_skip], [(ds, BF16)], name="s5_gelu")[0]
    mid_landed = _split_wait(mid_send, mid_recv, mid_flying, _allgather_ici_copies, gl, name="allgather_mid_wait")
    w_glu, w_qt, w_kvt, w_out, conv_full = _allgather_forward(mid_landed, name="allgather_mid_forward")
    conv_w = jnp.transpose(conv_full.reshape(4, ROW_ALIGN, cfg.FQ)[:, :3], (1, 0, 2)).reshape(3, cfg.FP)
    tg = _mm(gl, w_glu, name="mm_glu")
    ya = _ew(lambda rid, ycv, uv, tv, dk, bg: jax.nn.gelu(s5_y(ycv, uv, dk)) * jax.nn.sigmoid(tv + bg),
             [yc, u, tg], [d_skip, b_glu], [(ds, F32)], name="s5_glu")[0]

    qn = _rms_fwd(q_a, w['q_a_norm'], name="rms_q")
    kvn = _rms_fwd(kv_a, w['kv_a_norm'], name="rms_kv")
    q_raw = _mm(qn, w_qt, tb=True, name="mm_q")
    qx = _ew(_rope_heads(_rope, cfg.H), [q_raw, cos_t, sin_t], [], [(cfg.H * HEAD_SLOT, BF16)], name="rope_q")[0]
    kv = _mm(kvn, w_kvt, tb=True, out_dtype=BF16, name="mm_kv")
    kr = _ew(lambda rid, kp, cs, sn: _rope(kp, cs, sn), [k_pe, cos_t, sin_t], [], [(LANE, BF16)], name="rope_k")[0]
    o, lse = _attn_fwd(qx, kv, kr, cfg, name="attn_fwd")

    def norm2(rid, yav, ov, gs, ga):
        return jnp.concatenate([_rms_parts(yav, gs)[0] * gs, _rms_parts(ov, ga)[0] * ga], axis=1)

    yn = _ew(norm2, [ya, o], [w['out_norm_ssm'], w['out_norm_attn']], [(cfg.DMIX, BF16)], name="rms_out")[0]
    h1 = _mm(yn, w_out, res=h0, name="mm_out")
    xn2 = _rms_fwd(h1, w['ffn_norm'], name="rms_ffn")
    ffn_landed = _split_wait(ffn_send, ffn_recv, ffn_flying, _allgather_ici_copies, xn2, name="allgather_ffn_wait")
    w_upt, w_down = _allgather_forward(ffn_landed, name="allgather_ffn_forward")
    up = _mm(xn2, w_upt, tb=True, name="mm_up")
    act = _conv_fwd(up, conv_w, conv_b, name="conv_fwd")
    h2 = _mm(act, w_down, res=h1, tm=_tile(lp, 544, ROW_ALIGN), name="mm_down")

    g_final = w['final_norm'].reshape(1, d)

    def head(rid, hv, tv, gv):
        xhat, r = _rms_parts(hv, gv)
        valid = rid >= PAD + N_META
        diff = jnp.where(valid, xhat * gv - tv, 0.0)
        dout = diff * (1.0 / d)
        dxhat = dout * gv
        dx = r * (dxhat - xhat * jnp.mean(dxhat * xhat, axis=-1, keepdims=True))
        return dx, dx, dout * xhat, 0.5 * diff * dout

    dh2, dh2_b, dg_final, loss_cols = _ew(head, [h2, (loss_target[0], d, 0, SKIP)], [g_final], [(d, F32), (d, BF16)], [d, d],
                                          tm=PAD + N_META, name="loss_head")
    loss = lax.psum(jnp.sum(loss_cols), ("x", "y", "c"))

    dact = _mm(dh2_b, w_down, tb=True, out_dtype=BF16, name="mm_dact")
    dw_down = _mm(act, dh2_b, ta=True, tn=d, tm=512, name="mm_dw_down")

    def sibling_start(g, tag):
        land = lax.empty((4, g.shape[0] // 8, g.shape[1]), F32)
        return _split_start([g, land], _rs_sibling_copies, 4, name=f"rs_sibling_{tag}_start")

    dn_send, dn_recv, dn_flying, dn_token = sibling_start(dw_down, "down")
    dup, dconv_w, dconv_b = _conv_bwd(up, dact, conv_w, conv_b + dn_token[0:1, 0:1], name="conv_bwd")
    tk_up, tm_up = _tile(cfg.FP, 1408), _tile(cfg.FP, 512)
    dw_upt = _mm(dup, xn2, ta=True, dims=(2 * cfg.FP, d, lp), tn=d, tm=tm_up, a_lead=True, name="mm_dw_up",
                 a_idx=lambda i, j, k: (i // (cfg.FP // tm_up), 0, i % (cfg.FP // tm_up)))
    up_send, up_recv, up_flying, up_token = sibling_start(dw_upt, "up")
    dxn2 = _mm(dup, w_upt, dims=(lp, d, 2 * cfg.FP), tk=tk_up, a_lead=True, name="mm_dxn2",
               a_idx=lambda i, j, k: (k // (cfg.FP // tk_up), i, k % (cfg.FP // tk_up)))
    dh1, dh1_b, dg_ffn = _rms_bwd(h1, w['ffn_norm'] + up_token[0:1, 0:1], dxn2, res=dh2, mask=True, with_bf16=True,
                                  name="rms_ffn_bwd")

    dyn = _mm(dh1_b, w_out, tb=True, name="mm_dyn")
    dw_out = _mm(yn, dh1_b, ta=True, tn=d, tm=512, name="mm_dw_out")
    up_done = _split_wait(up_send, up_recv, up_flying, _rs_sibling_copies, dw_out, name="rs_sibling_up_wait")
    dn_done = _split_wait(dn_send, dn_recv, dn_flying, _rs_sibling_copies, dw_out, name="rs_sibling_down_wait")
    early_parts, early_sends = _add_halves_all([up_done[0], dn_done[0]], [up_done[1], dn_done[1]], [BF16] * 2, "early")
    chip_lands = [lax.empty((3,) + s.shape[1:], s.dtype) for s in early_sends]
    ch_send, ch_recv, ch_flying, ch_token = _split_start(early_sends + chip_lands, _rs_chips_copies, 6,
                                                         name="rs_chips_early_start")
    dya, dg_ssm = _rms_bwd(ya, w['out_norm_ssm'] + ch_token[0:1, 0:1], (dyn, ds, 0), name="rms_ssm_bwd")
    do, dg_attn = _rms_bwd(o, w['out_norm_attn'], (dyn, cfg.DATTN, ds // cfg.DATTN), name="rms_attn_bwd")

    dqx, dkv, dkr = _attn_bwd(qx, kv, kr, o, lse, do, cfg, name="attn_bwd")
    dq_raw = _ew(_rope_heads(_unrope, cfg.H), [dqx, cos_t, sin_t], [], [(cfg.H * HEAD_SLOT, BF16)], name="unrope_q")[0]
    dk_pe = _ew(lambda rid, dk, cs, sn: _unrope(dk, cs, sn), [dkr, cos_t, sin_t], [], [(LANE, F32)], name="unrope_k")[0]
    dqn = _mm(dq_raw, w_qt, name="mm_dqn")
    dw_qt = _mm(dq_raw, qn, ta=True, tm=512, name="mm_dw_q")
    dkvn = _mm(dkv, w_kvt, name="mm_dkvn")
    dw_kvt = _mm(dkv, kvn, ta=True, tm=512, name="mm_dw_kv")
    dq_a, dg_q = _rms_bwd(q_a, w['q_a_norm'], dqn, name="rms_q_bwd")
    dkv_a, dg_kv = _rms_bwd(kv_a, w['kv_a_norm'], dkvn, name="rms_kv_bwd")

    def glu_bwd(rid, ycv, uv, tv, dyav, dk, bg):
        gelu = jax.nn.gelu(s5_y(ycv, uv, dk))
        sg = jax.nn.sigmoid(tv + bg)
        dt = dyav * gelu * sg * (1.0 - sg)
        return dt, dyav * sg, dt

    dt_b, dgl1, db_glu = _ew(glu_bwd, [yc, u, tg, dya], [d_skip, b_glu], [(ds, BF16), (ds, F32)], [ds], name="s5_glu_bwd")
    dgl = _mm(dt_b, w_glu, tb=True, res=dgl1, name="mm_dgl")
    dw_glu = _mm(gl, dt_b, ta=True, tm=512, name="mm_dw_glu")

    def gelu_bwd(rid, ycv, uv, dglv, dk):
        _, vjp = jax.vjp(jax.nn.gelu, s5_y(ycv, uv, dk))
        dy = vjp(dglv)[0]
        return dy, dy * dk, dy * uv

    dy_b, du_skip, dd_skip = _ew(gelu_bwd, [yc, u, dgl], [d_skip], [(ds, BF16), (ds, F32)], [ds], name="s5_gelu_bwd")
    du, dbb_band, dcc_band, da_l = _s5_bwd(dy_b, hs, z, bb_band, cc_band, a_l, du_skip, cfg, name="s5_bwd")

    dz = jnp.concatenate([du, dq_a, dkv_a, dk_pe], axis=1).astype(BF16)
    dxn = _mm(dz, w_in, tb=True, name="mm_dxn")
    dw_in = _mm(xn, dz, ta=True, tm=512, tn=_tile(cfg.DINP, 1024), name="mm_dw_in")
    def mix_bwd(rid, xv, dyv, resv, gv):
        dx, dg = _rms_bwd_block(xv, gv, dyv)
        dx = dx + resv
        return dx, dx, dg

    grad_x, dh0_head, dg_mix = _ew(mix_bwd, [h0, dxn, dh1], [mix_norm], [(d, F32, SKIP), (d, F32, FIRST)], [d],
                                   tm=PAD + N_META, name="rms_mix_bwd")
    grad_x = grad_x[None]

    da_re, da_im = _gp_from_lanes(da_l, cfg)
    dbb_re, dbb_im = _bb_from_band(dbb_band, cfg)
    dlam_re, dlam_im, dlog_dt, db_re, db_im = s5_vjp((da_re, da_im, dbb_re, dbb_im))
    dc_re, dc_im = _cc_from_band(dcc_band, cfg)
    local_small = {
        'meta_tokens': dh0_head[PAD:], 'mix_norm': dg_mix, 'lam_re': dlam_re, 'lam_im': dlam_im, 'log_dt': dlog_dt,
        'b_re': db_re, 'b_im': db_im, 'c_re': dc_re, 'c_im': dc_im, 'd_skip': dd_skip, 'b_glu': db_glu, 'q_a_norm': dg_q,
        'kv_a_norm': dg_kv, 'out_norm_ssm': dg_ssm, 'out_norm_attn': dg_attn, 'ffn_norm': dg_ffn,
        'conv_w': _ff_unpad(dconv_w, cfg), 'conv_b': _ff_unpad(dconv_b, cfg), 'final_norm': dg_final,
    }
    small_shapes = [local_small[n].shape for n in SMALL]

    rest_local = [dw_in, dw_glu, dw_qt, dw_kvt, dw_out, _pack([local_small[n] for n in SMALL])]
    rest_recv = _rs_sibling(rest_local, name="rs_sibling_rest")
    rest_parts, rest_sends = _add_halves_all(rest_local, rest_recv, [BF16] * 5 + [F32], "rest")
    ch_done = _split_wait(ch_send, ch_recv, ch_flying, _rs_chips_copies, rest_sends[0], name="rs_chips_early_wait")
    rest_lands = [lax.empty((3,) + s.shape[1:], s.dtype) for s in rest_sends]
    rc_send, rc_recv, rc_flying, rc_token = _split_start(rest_sends + rest_lands, _rs_chips_copies, 3 * len(rest_sends),
                                                         before=ch_done[2], name="rs_chips_rest_start")
    red_up, red_down = _rs_finish(early_parts, ch_done[2:], "early", order=rc_token)

    delta, new_m, new_v, grads = {}, {}, {}, {}
    padded_rows = ('w_down',)

    def adamw_big(n, red):
        shp = w[n].shape
        w2, m2, v2 = [t.reshape(shp[-2], shp[-1]) for t in (w[n], m[n], v[n])]
        if n in padded_rows:
            g2, dl, mn, vn = _adamw(w2, red, m2, v2, emit_grad=True, name=f"adamw_{n}")
            grads[n] = g2.reshape(shp)
        else:
            grads[n] = _from_comm_layout(n, red, cfg)
            dl, mn, vn = _adamw(w2, grads[n].reshape(shp[-2], shp[-1]), m2, v2, name=f"adamw_{n}")
        delta[n], new_m[n], new_v[n] = dl.reshape(shp), mn.reshape(shp), vn.reshape(shp)

    adamw_big('w_up', red_up)
    adamw_big('w_down', red_down)
    rc_done = _split_wait(rc_send, rc_recv, rc_flying, _rs_chips_copies, delta['w_down'], name="rs_chips_rest_wait")
    red = _rs_finish(rest_parts, rc_done[len(rest_sends):], "rest")
    small_full = _allgather([_place_shard(red[5], F32, name="place_small")], name="allgather_small")[0]
    small_sum = dict(zip(SMALL, _unpack(small_full, small_shapes)))
    for n, r in zip(['w_in', 'w_glu', 'w_q_b', 'w_kv_b', 'w_out'], red[:5]):
        adamw_big(n, r)

    for n in SMALL:
        g = small_sum[n]
        if n == 'meta_tokens':
            g = lax.dynamic_slice_in_dim(g, me * (d // 4), d // 4, axis=1)
        elif n == 'conv_w':
            g = lax.dynamic_slice_in_dim(g, me * (cfg.F // 4), cfg.F // 4, axis=1)[None]
        else:
            g = g.reshape(w[n].shape)
        grads[n] = g

    shapes = [w[n].shape for n in SMALL]
    packs = [_pack([src[n] for n in SMALL]) for src in (w, grads, m, v)]
    for dst, p in zip((delta, new_m, new_v), _adamw(*packs, name="adamw_small")):
        dst.update(zip(SMALL, _unpack(p, shapes)))

    return (loss, grad_x, *[grads[n] for n in WEIGHTS], *[delta[n] for n in WEIGHTS],
            *[new_m[n] for n in WEIGHTS], *[new_v[n] for n in WEIGHTS])


def kernel(x, meta_tokens, mix_norm, w_in, lam_re, lam_im, log_dt, b_re, b_im, c_re, c_im, d_skip, w_glu, b_glu, q_a_norm, w_q_b, kv_a_norm, w_kv_b, out_norm_ssm, out_norm_attn, w_out, ffn_norm, w_up, conv_w, conv_b, w_down, final_norm, loss_target, m_meta_tokens, m_mix_norm, m_w_in, m_lam_re, m_lam_im, m_log_dt, m_b_re, m_b_im, m_c_re, m_c_im, m_d_skip, m_w_glu, m_b_glu, m_q_a_norm, m_w_q_b, m_kv_a_norm, m_w_kv_b, m_out_norm_ssm, m_out_norm_attn, m_w_out, m_ffn_norm, m_w_up, m_conv_w, m_conv_b, m_w_down, m_final_norm, v_meta_tokens, v_mix_norm, v_w_in, v_lam_re, v_lam_im, v_log_dt, v_b_re, v_b_im, v_c_re, v_c_im, v_d_skip, v_w_glu, v_b_glu, v_q_a_norm, v_w_q_b, v_kv_a_norm, v_w_kv_b, v_out_norm_ssm, v_out_norm_attn, v_w_out, v_ffn_norm, v_w_up, v_conv_w, v_conv_b, v_w_down, v_final_norm):
    args = dict(locals())
    w = {n: args[n] for n in WEIGHTS}
    m = {n: args["m_" + n] for n in WEIGHTS}
    v = {n: args["v_" + n] for n in WEIGHTS}
    return _step(PROD, w, m, v, x, loss_target)
```

```python
import functools
import math
from typing import NamedTuple

import jax
import jax.numpy as jnp
from jax import lax
from jax.experimental import pallas as pl
from jax.experimental.pallas import tpu as pltpu

F32, BF16 = jnp.float32, jnp.bfloat16
MESH = pl.DeviceIdType.MESH
LANE = 128
ROW_ALIGN = 16
N_META = 16
PAD = 112
CHUNK = 64
SSM_GROUP = 16
SSM_STATE = 64
GROUPS_PER_BLOCK = 8
QK_NOPE, QK_ROPE, V_HEAD = 128, 64, 128
HEAD_SLOT = 256
ROPE_BASE = 10000.0
EPS = 1e-6
ADAM_LR, ADAM_B1, ADAM_B2, ADAM_EPS, ADAM_WD, ADAM_STEP = 0.001, 0.9, 0.999, 1e-08, 0.01, 10
DT_F32_BLOCK_BYTES = 1 << 20
SKIP, FIRST = "skip", "first"


class Cfg(NamedTuple):
    D: int
    S: int
    DS: int
    H: int
    QL: int
    KVL: int
    F: int

    @property
    def LP(self):
        return PAD + N_META + self.S

    @property
    def G(self):
        return self.DS // SSM_GROUP

    @property
    def NB(self):
        return self.G // GROUPS_PER_BLOCK

    @property
    def NL(self):
        return 2 * self.G * SSM_STATE

    @property
    def DATTN(self):
        return self.H * V_HEAD

    @property
    def DMIX(self):
        return self.DS + self.DATTN

    @property
    def DIN(self):
        return self.DS + self.QL + self.KVL + QK_ROPE

    @property
    def DINP(self):
        return self.DS + self.QL + self.KVL + LANE

    @property
    def FQ(self):
        return -(-(self.F // 4) // LANE) * LANE

    @property
    def FP(self):
        return 4 * self.FQ


PROD = Cfg(D=2048, S=2048, DS=1024, H=8, QL=512, KVL=256, F=5504)

WEIGHTS = ['meta_tokens', 'mix_norm', 'w_in', 'lam_re', 'lam_im', 'log_dt', 'b_re', 'b_im', 'c_re', 'c_im', 'd_skip',
           'w_glu', 'b_glu', 'q_a_norm', 'w_q_b', 'kv_a_norm', 'w_kv_b', 'out_norm_ssm', 'out_norm_attn', 'w_out',
           'ffn_norm', 'w_up', 'conv_w', 'conv_b', 'w_down', 'final_norm']
BIG = ['w_in', 'w_glu', 'w_q_b', 'w_kv_b', 'w_out', 'w_up', 'w_down']
SMALL = [n for n in WEIGHTS if n not in BIG]


def _pc(body, **kw):
    return pl.pallas_call(body, **kw)


def _tile(n, target, align=LANE):
    best = None
    d = align
    while d <= min(n, target):
        if n % d == 0:
            best = d
        d += align
    return best if best is not None else n


def _row_tile(rows, cols):
    return _tile(rows, max(ROW_ALIGN, DT_F32_BLOCK_BYTES // (4 * cols)), ROW_ALIGN)


def _mm(a, b, *, name, ta=False, tb=False, tm=None, tn=512, tk=None, out_dtype=F32, res=None,
        a_idx=None, b_idx=None, dims=None, a_lead=False):
    if dims is None:
        m, k = (a.shape[1], a.shape[0]) if ta else a.shape
        n = b.shape[0] if tb else b.shape[1]
    else:
        m, n, k = dims
    tm = _tile(m, tm or m, LANE if ta else ROW_ALIGN)
    tn = _tile(n, tn)
    tk = _tile(k, tk or k, ROW_ALIGN if (ta and not tb) else LANE)
    nm, nn, nk = m // tm, n // tn, k // tk
    a_idx = a_idx or ((lambda i, j, kk: (kk, i)) if ta else (lambda i, j, kk: (i, kk)))
    b_idx = b_idx or ((lambda i, j, kk: (j, kk)) if tb else (lambda i, j, kk: (kk, j)))
    dn = (((0 if ta else 1,), (1 if tb else 0,)), ((), ()))

    def body(*refs):
        a_ref, b_ref = refs[0], refs[1]
        r_ref = refs[2] if res is not None else None
        o_ref = refs[3] if res is not None else refs[2]
        d = lax.dot_general(a_ref[...].astype(BF16), b_ref[...].astype(BF16), dn, preferred_element_type=F32)

        def finish(r):
            if r_ref is not None:
                r = r + r_ref[...].astype(F32)
            o_ref[...] = r.astype(out_dtype)

        if nk == 1:
            finish(d)
        else:
            acc = refs[-1]
            kk = pl.program_id(2)

            @pl.when(kk == 0)
            def _():
                acc[...] = d

            @pl.when(kk > 0)
            def _():
                acc[...] += d

            @pl.when(kk == nk - 1)
            def _():
                finish(acc[...])

    a_blk = ((None,) if a_lead else ()) + ((tk, tm) if ta else (tm, tk))
    in_specs = [pl.BlockSpec(a_blk, a_idx), pl.BlockSpec((tn, tk) if tb else (tk, tn), b_idx)]
    args = [a, b]
    if res is not None:
        in_specs.append(pl.BlockSpec((tm, tn), lambda i, j, kk: (i, j)))
        args.append(res)
    return _pc(body, name=name, grid=(nm, nn, nk), in_specs=in_specs,
               out_specs=pl.BlockSpec((tm, tn), lambda i, j, kk: (i, j)),
               out_shape=jax.ShapeDtypeStruct((m, n), out_dtype),
               scratch_shapes=[pltpu.VMEM((tm, tn), F32)] if nk > 1 else [],
               compiler_params=pltpu.CompilerParams(dimension_semantics=("parallel", "parallel", "arbitrary")))(*args)


def _ew(fn, ins, vecs, outs, sums=(), *, name, tm=None):
    ins = [x if isinstance(x, tuple) else (x, x.shape[1], 0) for x in ins]
    ins = [x if len(x) == 4 else x + (None,) for x in ins]
    outs = [o if len(o) == 3 else o + (None,) for o in outs]
    rows = ins[0][0].shape[0]
    cmax = max([c for _, c, _, _ in ins] + [c for c, _, _ in outs])
    tm = tm or _row_tile(rows, cmax)
    n_in, n_vec, n_out, n_sum = len(ins), len(vecs), len(outs), len(sums)

    def body(*refs):
        i = pl.program_id(0)
        rid = i * tm + lax.broadcasted_iota(jnp.int32, (tm, 1), 0)
        vals = [r[...] for r in refs[:n_in + n_vec]]
        res = fn(rid, *vals)
        res = res if isinstance(res, (tuple, list)) else (res,)
        o_refs = refs[n_in + n_vec:]
        for o_ref, r, (_, _, mode) in zip(o_refs[:n_out], res[:n_out], outs):
            if mode == FIRST:
                @pl.when(i == 0)
                def _():
                    o_ref[...] = r.astype(o_ref.dtype)
            else:
                o_ref[...] = r.astype(o_ref.dtype)
        for o_ref, r in zip(o_refs[n_out:], res[n_out:]):
            part = jnp.sum(r.astype(F32), axis=0, keepdims=True)

            @pl.when(i == 0)
            def _():
                o_ref[...] = part

            @pl.when(i > 0)
            def _():
                o_ref[...] += part

    def row_idx(mode):
        if mode == SKIP:
            return lambda i, cb=0: (jnp.maximum(i - 1, 0), cb)
        if mode == FIRST:
            return lambda i, cb=0: (0, cb)
        return lambda i, cb=0: (i, cb)

    in_specs = [pl.BlockSpec((tm, c), functools.partial(row_idx(mode), cb=cb)) for _, c, cb, mode in ins]
    in_specs += [pl.BlockSpec(v.shape, functools.partial(lambda i, nd: (0,) * nd, nd=v.ndim)) for v in vecs]
    out_specs = [pl.BlockSpec((tm, c), row_idx(mode)) for c, _, mode in outs]
    out_specs += [pl.BlockSpec((1, c), lambda i: (0, 0)) for c in sums]
    out_rows = {None: rows, SKIP: rows - tm, FIRST: tm}
    out_shape = [jax.ShapeDtypeStruct((out_rows[mode], c), dt) for c, dt, mode in outs]
    out_shape += [jax.ShapeDtypeStruct((1, c), F32) for c in sums]
    return _pc(body, name=name, grid=(rows // tm,), in_specs=in_specs, out_specs=out_specs, out_shape=out_shape,
               compiler_params=pltpu.CompilerParams(dimension_semantics=("arbitrary",)))(*[x[0] for x in ins], *vecs)


def _rms_parts(x, g):
    r = lax.rsqrt(jnp.mean(x * x, axis=-1, keepdims=True) + EPS)
    return x * r, r


def _rms_bwd_block(x, g, dy):
    xhat, r = _rms_parts(x, g)
    dxhat = dy * g
    dx = r * (dxhat - xhat * jnp.mean(dxhat * xhat, axis=-1, keepdims=True))
    return dx, dy * xhat


def _rms_fwd(x, g, *, name):
    c = x[1] if isinstance(x, tuple) else x.shape[1]
    return _ew(lambda rid, xv, gv: _rms_parts(xv.astype(F32), gv)[0] * gv, [x], [g], [(c, BF16)], name=name)[0]


def _rms_bwd(x, g, dy, *, name, res=None, mask=False, with_bf16=False):
    c = x[1] if isinstance(x, tuple) else x.shape[1]

    def fn(rid, xv, dyv, *rest):
        gv = rest[-1]
        dx, dg = _rms_bwd_block(xv.astype(F32), gv, dyv.astype(F32))
        if res is not None:
            dx = dx + rest[0]
        if mask:
            dx = jnp.where(rid >= PAD, dx, 0.0)
        return (dx, dx, dg) if with_bf16 else (dx, dg)

    ins = [x, dy] + ([res] if res is not None else [])
    outs = [(c, F32)] + ([(c, BF16)] if with_bf16 else [])
    return _ew(fn, ins, [g], outs, [c], name=name)


S5_W = GROUPS_PER_BLOCK * SSM_STATE
S5_GW = GROUPS_PER_BLOCK * SSM_GROUP
S5_UNROLL = 8
S5_DA_ROWS = 272


def _s5_scan_in_place(ref, pw_ref, *, reverse):
    lp = ref.shape[0]
    tile_rows = 8
    chunk = _tile(lp, S5_DA_ROWS, tile_rows)
    sub = lax.broadcasted_iota(jnp.int32, (chunk, 1), 0) % tile_rows

    def chunk_body(c, carry):
        rows = pl.ds(pl.multiple_of(c * chunk, tile_rows), chunk)
        xr, xi = ref[rows, :S5_W], ref[rows, S5_W:]
        for k in (1, 2, 4):
            row = tile_rows - k if reverse else k - 1
            mr, mi = pw_ref[row:row + 1, :S5_W], pw_ref[row:row + 1, S5_W:]
            shift = chunk - k if reverse else k
            sr, si = pltpu.roll(xr, shift, 0), pltpu.roll(xi, shift, 0)
            keep = (sub < tile_rows - k) if reverse else (sub >= k)
            xr, xi = xr + jnp.where(keep, mr * sr - mi * si, 0.0), xi + jnp.where(keep, mr * si + mi * sr, 0.0)
        ref[rows, :S5_W] = xr
        ref[rows, S5_W:] = xi
        return carry

    lax.fori_loop(0, lp // chunk, chunk_body, 0)

    pr, pi = pw_ref[:, :S5_W], pw_ref[:, S5_W:]
    ntile = lp // tile_rows
    unroll = 4

    def step(n, carry):
        cr, ci = carry
        for q in range(unroll):
            j = n * unroll + q
            j = ntile - 1 - j if reverse else j
            rows = pl.ds(pl.multiple_of(j * tile_rows, tile_rows), tile_rows)
            nr = ref[rows, :S5_W] + (pr * cr - pi * ci)
            ni = ref[rows, S5_W:] + (pr * ci + pi * cr)
            ref[rows, :S5_W] = nr
            ref[rows, S5_W:] = ni
            cr, ci = (nr[0:1], ni[0:1]) if reverse else (nr[tile_rows - 1:], ni[tile_rows - 1:])
        return cr, ci

    z = jnp.zeros((1, S5_W), F32)
    lax.fori_loop(0, ntile // unroll, step, (z, z))


def _s5_fwd(z, bb_band, cc_band, a_l, cfg, *, name):
    lp, ds, nl = cfg.LP, cfg.DS, cfg.NL

    def body(u_ref, bb_ref, cc_ref, a_ref, hs_ref, y_ref):
        hs_ref[...] = jnp.dot(u_ref[...].astype(BF16), bb_ref[...], preferred_element_type=F32)
        _s5_scan_in_place(hs_ref, a_ref, reverse=False)
        y_ref[...] = jnp.dot(hs_ref[...].astype(BF16), cc_ref[...], preferred_element_type=F32)

    return _pc(body, name=name, grid=(cfg.NB,),
               in_specs=[pl.BlockSpec((lp, S5_GW), lambda j: (0, j)), pl.BlockSpec((S5_GW, 2 * S5_W), lambda j: (j, 0)),
                         pl.BlockSpec((2 * S5_W, S5_GW), lambda j: (j, 0)), pl.BlockSpec((8, 2 * S5_W), lambda j: (0, j))],
               out_specs=[pl.BlockSpec((lp, 2 * S5_W), lambda j: (0, j)), pl.BlockSpec((lp, S5_GW), lambda j: (0, j))],
               out_shape=[jax.ShapeDtypeStruct((lp, nl), F32), jax.ShapeDtypeStruct((lp, ds), F32)],
               compiler_params=pltpu.CompilerParams(dimension_semantics=("parallel",)))(z, bb_band, cc_band, a_l)


def _s5_bwd(dy, hs, z, bb_band, cc_band, a_l, du_skip, cfg, *, name):
    lp, ds, nl = cfg.LP, cfg.DS, cfg.NL
    nt = (((1,), (1,)), ((), ()))
    tn = (((0,), (0,)), ((), ()))

    def body(dy_ref, hs_ref, u_ref, bb_ref, cc_ref, a_ref, sk_ref, du_ref, dbb_ref, dcc_ref, da_ref, g_ref):
        dyv = dy_ref[...]
        g_ref[...] = lax.dot_general(dyv, cc_ref[...], nt, preferred_element_type=F32)
        _s5_scan_in_place(g_ref, a_ref, reverse=True)
        dcc_ref[...] = lax.dot_general(hs_ref[...].astype(BF16), dyv, tn, preferred_element_type=F32)
        gb = g_ref[...].astype(BF16)
        dbb_ref[...] = lax.dot_general(u_ref[...].astype(BF16), gb, tn, preferred_element_type=F32)
        du_ref[...] = lax.dot_general(gb, bb_ref[...], nt, preferred_element_type=F32) + sk_ref[...]
        dre = jnp.zeros((1, S5_W), F32)
        dim = jnp.zeros((1, S5_W), F32)
        for r0 in range(0, lp, S5_DA_ROWS):
            rows = min(S5_DA_ROWS, lp - r0)
            first = lax.broadcasted_iota(jnp.int32, (rows, 1), 0) == 0
            prev = hs_ref[r0 - 1:r0, :] if r0 else jnp.zeros((1, 2 * S5_W), F32)
            hr = jnp.where(first, prev[:, :S5_W], pltpu.roll(hs_ref[r0:r0 + rows, :S5_W], 1, 0))
            hi = jnp.where(first, prev[:, S5_W:], pltpu.roll(hs_ref[r0:r0 + rows, S5_W:], 1, 0))
            gr, gi = g_ref[r0:r0 + rows, :S5_W], g_ref[r0:r0 + rows, S5_W:]
            dre = dre + jnp.sum(gr * hr + gi * hi, axis=0, keepdims=True)
            dim = dim + jnp.sum(gi * hr - gr * hi, axis=0, keepdims=True)
        da_ref[:, :S5_W] = dre
        da_ref[:, S5_W:] = dim

    col_blk = pl.BlockSpec((lp, S5_GW), lambda j: (0, j))
    lane_blk = pl.BlockSpec((lp, 2 * S5_W), lambda j: (0, j))
    bb_blk = pl.BlockSpec((S5_GW, 2 * S5_W), lambda j: (j, 0))
    cc_blk = pl.BlockSpec((2 * S5_W, S5_GW), lambda j: (j, 0))
    a_blk = pl.BlockSpec((1, 2 * S5_W), lambda j: (0, j))
    pw_blk = pl.BlockSpec((8, 2 * S5_W), lambda j: (0, j))
    return _pc(body, name=name, grid=(cfg.NB,),
               in_specs=[col_blk, lane_blk, col_blk, bb_blk, cc_blk, pw_blk, col_blk],
               out_specs=[col_blk, bb_blk, cc_blk, a_blk],
               out_shape=[jax.ShapeDtypeStruct((lp, ds), F32), jax.ShapeDtypeStruct((ds, 2 * S5_W), F32),
                          jax.ShapeDtypeStruct((nl, S5_GW), F32), jax.ShapeDtypeStruct((1, nl), F32)],
               scratch_shapes=[pltpu.VMEM((lp, 2 * S5_W), F32)],
               compiler_params=pltpu.CompilerParams(dimension_semantics=("parallel",)))(dy, hs, z, bb_band, cc_band, a_l, du_skip)


def _conv_gate(pre, cw, cb):
    return cw[0:1] * pltpu.roll(pre, 2, 0) + cw[1:2] * pltpu.roll(pre, 1, 0) + cw[2:3] * pre + cb


def _conv_fwd(up, cw, cb, *, name):
    lp, fp2 = up.shape
    fp = fp2 // 2
    tc = _tile(fp, 256)
    nb = fp // tc

    def body(pre_ref, val_ref, cw_ref, cb_ref, o_ref):
        gate = _conv_gate(pre_ref[...].astype(F32), cw_ref[...], cb_ref[...])
        o_ref[...] = (jax.nn.silu(gate) * val_ref[...].astype(F32)).astype(BF16)

    return _pc(body, name=name, grid=(nb,),
               in_specs=[pl.BlockSpec((lp, tc), lambda j: (0, j)), pl.BlockSpec((lp, tc), lambda j: (0, nb + j)),
                         pl.BlockSpec((3, tc), lambda j: (0, j)), pl.BlockSpec((1, tc), lambda j: (0, j))],
               out_specs=pl.BlockSpec((lp, tc), lambda j: (0, j)),
               out_shape=jax.ShapeDtypeStruct((lp, fp), BF16),
               compiler_params=pltpu.CompilerParams(dimension_semantics=("parallel",)))(up, up, cw, cb)


def _conv_bwd(up, dact, cw, cb, *, name):
    lp, fp2 = up.shape
    fp = fp2 // 2
    tc = _tile(fp, 256)
    nb = fp // tc

    def body(pre_ref, val_ref, da_ref, cw_ref, cb_ref, dup_ref, dcw_ref, dcb_ref):
        pre, val, da, cwv = pre_ref[...].astype(F32), val_ref[...].astype(F32), da_ref[...].astype(F32), cw_ref[...]
        gate = _conv_gate(pre, cwv, cb_ref[...])
        sg = jax.nn.sigmoid(gate)
        dup_ref[1] = (da * (gate * sg)).astype(BF16)
        dgate = da * val * (sg * (1.0 + gate * (1.0 - sg)))
        dpre = cwv[2:3] * dgate + cwv[1:2] * pltpu.roll(dgate, lp - 1, 0) + cwv[0:1] * pltpu.roll(dgate, lp - 2, 0)
        dup_ref[0] = dpre.astype(BF16)
        dcb_ref[...] = jnp.sum(dgate, axis=0, keepdims=True)
        dcw_ref[0:1, :] = jnp.sum(dgate * pltpu.roll(pre, 2, 0), axis=0, keepdims=True)
        dcw_ref[1:2, :] = jnp.sum(dgate * pltpu.roll(pre, 1, 0), axis=0, keepdims=True)
        dcw_ref[2:3, :] = jnp.sum(dgate * pre, axis=0, keepdims=True)

    return _pc(body, name=name, grid=(nb,),
               in_specs=[pl.BlockSpec((lp, tc), lambda j: (0, j)), pl.BlockSpec((lp, tc), lambda j: (0, nb + j)),
                         pl.BlockSpec((lp, tc), lambda j: (0, j)),
                         pl.BlockSpec((3, tc), lambda j: (0, j)), pl.BlockSpec((1, tc), lambda j: (0, j))],
               out_specs=[pl.BlockSpec((2, lp, tc), lambda j: (0, 0, j)),
                          pl.BlockSpec((3, tc), lambda j: (0, j)), pl.BlockSpec((1, tc), lambda j: (0, j))],
               out_shape=[jax.ShapeDtypeStruct((2, lp, fp), BF16), jax.ShapeDtypeStruct((3, fp), F32),
                          jax.ShapeDtypeStruct((1, fp), F32)],
               compiler_params=pltpu.CompilerParams(dimension_semantics=("parallel",)))(up, up, dact, cw, cb)


def _key_limit(i, tq, lp):
    return min(lp, -(-((i + 1) * tq) // LANE) * LANE)


def _attn_mask(i, tq, nk):
    qrow = i * tq + lax.broadcasted_iota(jnp.int32, (tq, 1), 0)
    krow = lax.broadcasted_iota(jnp.int32, (1, nk), 1)
    return (krow >= PAD) & ((krow // CHUNK) <= (qrow // CHUNK)), qrow >= PAD


def _attn_scores(q, kn, kr, i, tq, scale):
    nt = (((1,), (1,)), ((), ()))
    s = lax.dot_general(q[:, :QK_NOPE], kn, nt, preferred_element_type=F32)
    s = s + lax.dot_general(q[:, QK_NOPE:], kr, nt, preferred_element_type=F32)
    mask, qvalid = _attn_mask(i, tq, kn.shape[0])
    return jnp.where(mask, s * scale, jnp.finfo(F32).min), qvalid


def _per_q_block(nq, fn):
    i = pl.program_id(1)
    for blk in range(nq):
        pl.when(i == blk)(functools.partial(fn, blk))


def _attn_fwd(qx, kv, kr, cfg, *, name):
    lp, h = cfg.LP, cfg.H
    tq = _tile(lp, 272, ROW_ALIGN)
    nq = lp // tq
    scale = 1.0 / math.sqrt(QK_NOPE + QK_ROPE)

    def body(q_ref, kn_ref, v_ref, kr_ref, o_ref, lse_ref):
        def block(blk):
            nk = _key_limit(blk, tq, lp)
            s, qvalid = _attn_scores(q_ref[...], kn_ref[:nk], kr_ref[:nk], blk, tq, scale)
            m = jnp.max(s, axis=-1, keepdims=True)
            p = jnp.exp(s - m)
            l = jnp.sum(p, axis=-1, keepdims=True)
            o = jnp.dot(p.astype(BF16), v_ref[:nk], preferred_element_type=F32) / l
            o_ref[...] = jnp.where(qvalid, o, 0.0)
            lse_ref[...] = m + jnp.log(l)

        _per_q_block(nq, block)

    return _pc(body, name=name, grid=(h, nq),
               in_specs=[pl.BlockSpec((tq, HEAD_SLOT), lambda hh, i: (i, hh)),
                         pl.BlockSpec((lp, QK_NOPE), lambda hh, i: (0, 2 * hh)),
                         pl.BlockSpec((lp, V_HEAD), lambda hh, i: (0, 2 * hh + 1)),
                         pl.BlockSpec((lp, LANE), lambda hh, i: (0, 0))],
               out_specs=[pl.BlockSpec((tq, V_HEAD), lambda hh, i: (i, hh)),
                          pl.BlockSpec((None, tq, 1), lambda hh, i: (hh, i, 0))],
               out_shape=[jax.ShapeDtypeStruct((lp, h * V_HEAD), F32), jax.ShapeDtypeStruct((h, lp, 1), F32)],
               compiler_params=pltpu.CompilerParams(dimension_semantics=("parallel", "parallel")))(qx, kv, kv, kr)


def _attn_bwd(qx, kv, kr, o, lse, do, cfg, *, name):
    lp, h = cfg.LP, cfg.H
    tq = _tile(lp, 272, ROW_ALIGN)
    nq = lp // tq
    scale = 1.0 / math.sqrt(QK_NOPE + QK_ROPE)
    tn_dims = (((0,), (0,)), ((), ()))

    def body(q_ref, kn_ref, v_ref, kr_ref, o_ref, lse_ref, do_ref, dq_ref, dkv_ref, dkr_ref, dkv_acc):
        hh, i = pl.program_id(0), pl.program_id(1)

        @pl.when(i == 0)
        def _():
            dkv_acc[...] = jnp.zeros_like(dkv_acc)

        @pl.when((i == 0) & (hh == 0))
        def _():
            dkr_ref[...] = jnp.zeros_like(dkr_ref)

        def block(blk):
            nk = _key_limit(blk, tq, lp)
            q, kn, v, krv = q_ref[...], kn_ref[:nk], v_ref[:nk], kr_ref[:nk]
            s, qvalid = _attn_scores(q, kn, krv, blk, tq, scale)
            dov = jnp.where(qvalid, do_ref[...], 0.0)
            p = jnp.exp(s - lse_ref[...])
            delta = jnp.sum(dov * o_ref[...], axis=-1, keepdims=True)
            dob = dov.astype(BF16)
            dp = lax.dot_general(dob, v, (((1,), (1,)), ((), ())), preferred_element_type=F32)
            ds = (p * (dp - delta) * scale).astype(BF16)
            dq_ref[:, :QK_NOPE] = jnp.dot(ds, kn, preferred_element_type=F32)
            dq_ref[:, QK_NOPE:] = jnp.dot(ds, krv, preferred_element_type=F32)
            dkv_acc[:nk, :QK_NOPE] += lax.dot_general(ds, q[:, :QK_NOPE], tn_dims, preferred_element_type=F32)
            dkv_acc[:nk, QK_NOPE:] += lax.dot_general(p.astype(BF16), dob, tn_dims, preferred_element_type=F32)
            dkr_ref[:nk, :] += lax.dot_general(ds, q[:, QK_NOPE:], tn_dims, preferred_element_type=F32)

        _per_q_block(nq, block)

        @pl.when(i == nq - 1)
        def _():
            dkv_ref[...] = dkv_acc[...].astype(BF16)

    return _pc(body, name=name, grid=(h, nq),
               in_specs=[pl.BlockSpec((tq, HEAD_SLOT), lambda hh, i: (i, hh)),
                         pl.BlockSpec((lp, QK_NOPE), lambda hh, i: (0, 2 * hh)),
                         pl.BlockSpec((lp, V_HEAD), lambda hh, i: (0, 2 * hh + 1)),
                         pl.BlockSpec((lp, LANE), lambda hh, i: (0, 0)),
                         pl.BlockSpec((tq, V_HEAD), lambda hh, i: (i, hh)),
                         pl.BlockSpec((None, tq, 1), lambda hh, i: (hh, i, 0)),
                         pl.BlockSpec((tq, V_HEAD), lambda hh, i: (i, hh))],
               out_specs=[pl.BlockSpec((tq, HEAD_SLOT), lambda hh, i: (i, hh)),
                          pl.BlockSpec((lp, QK_NOPE + V_HEAD), lambda hh, i: (0, hh)),
                          pl.BlockSpec((lp, LANE), lambda hh, i: (0, 0))],
               out_shape=[jax.ShapeDtypeStruct((lp, h * HEAD_SLOT), F32),
                          jax.ShapeDtypeStruct((lp, h * (QK_NOPE + V_HEAD)), BF16),
                          jax.ShapeDtypeStruct((lp, LANE), F32)],
               scratch_shapes=[pltpu.VMEM((lp, QK_NOPE + V_HEAD), F32)],
               compiler_params=pltpu.CompilerParams(dimension_semantics=("arbitrary", "arbitrary")))(qx, kv, kv, kr, o, lse, do)


def _rot_half(x):
    lane = lax.broadcasted_iota(jnp.int32, x.shape, 1)
    half = QK_ROPE // 2
    return jnp.where(lane < half, -pltpu.roll(x, LANE - half, 1), pltpu.roll(x, half, 1))


def _rope(x, cos, sin):
    return x * cos + _rot_half(x) * sin


def _unrope(dy, cos, sin):
    return dy * cos - _rot_half(dy * sin)


def _rope_heads(fn, h):
    def apply(rid, q, cos, sin):
        parts = []
        for hh in range(h):
            parts.append(q[:, hh * HEAD_SLOT: hh * HEAD_SLOT + QK_NOPE])
            parts.append(fn(q[:, hh * HEAD_SLOT + QK_NOPE: (hh + 1) * HEAD_SLOT], cos, sin))
        return jnp.concatenate(parts, axis=1)
    return apply


ANY = pl.BlockSpec(memory_space=pl.ANY)


def _place():
    x, y, c = lax.axis_index("x"), lax.axis_index("y"), lax.axis_index("c")
    chips = [(1 - x, y), (x, 1 - y), (1 - x, 1 - y)]
    return x, y, c, chips


def _rcopy(src, dst, send_sem, recv_sem, dev):
    return pltpu.make_async_remote_copy(src_ref=src, dst_ref=dst, send_sem=send_sem, recv_sem=recv_sem,
                                        device_id=dev, device_id_type=MESH)


def _place_shard(shard, dtype, *, name):
    r, cols = shard.shape
    tm = _row_tile(r, cols)
    nblk = r // tm
    me = (2 * lax.axis_index("x") + lax.axis_index("y")).astype(jnp.int32).reshape(1)

    def body(me_ref, s_ref, o_ref):
        o_ref[...] = s_ref[...].astype(dtype)

    return _pc(body, name=name,
               grid_spec=pltpu.PrefetchScalarGridSpec(
                   num_scalar_prefetch=1, grid=(nblk,),
                   in_specs=[pl.BlockSpec((tm, cols), lambda i, mr: (i, 0))],
                   out_specs=pl.BlockSpec((tm, cols), lambda i, mr: (mr[0] * nblk + i, 0))),
               out_shape=jax.ShapeDtypeStruct((4 * r, cols), dtype),
               compiler_params=pltpu.CompilerParams(dimension_semantics=("arbitrary",)))(me, shard)


def _allgather(fulls, *, name):
    n = len(fulls)

    def body(*refs):
        outs = refs[n:2 * n]
        send_sems, recv_sems = refs[2 * n:]
        x, y, c, chips = _place()
        sib = (x, y, 1 - c)
        me = 2 * x + y

        def rows(t, s, half):
            hrows = outs[t].shape[0] // 8
            return outs[t].at[pl.ds((2 * s + half) * hrows, hrows)]

        sent = []
        for t in range(n):
            for j, (cx, cy) in enumerate(chips):
                cp = _rcopy(rows(t, me, c), rows(t, me, c), send_sems.at[6 * t + j], recv_sems.at[6 * t + j], (cx, cy, c))
                cp.start()
                sent.append(cp)
        for t in range(n):
            for j, (cx, cy) in enumerate(chips):
                landed = rows(t, 2 * cx + cy, c)
                _rcopy(landed, landed, send_sems.at[6 * t + j], recv_sems.at[6 * t + j], (cx, cy, c)).wait_recv()
                cp = _rcopy(landed, landed, send_sems.at[6 * t + 3 + j], recv_sems.at[6 * t + 3 + j], sib)
                cp.start()
                sent.append(cp)
        for t in range(n):
            for j, (cx, cy) in enumerate(chips):
                other = rows(t, 2 * cx + cy, 1 - c)
                _rcopy(other, other, send_sems.at[6 * t + 3 + j], recv_sems.at[6 * t + 3 + j], sib).wait_recv()
        for cp in sent:
            cp.wait_send()

    return _pc(body, name=name, in_specs=[ANY] * n, out_specs=[ANY] * n,
               out_shape=[jax.ShapeDtypeStruct(f.shape, f.dtype) for f in fulls],
               input_output_aliases={t: t for t in range(n)},
               scratch_shapes=[pltpu.SemaphoreType.DMA((6 * n,)), pltpu.SemaphoreType.DMA((6 * n,))])(*fulls)


HBM = pl.BlockSpec(memory_space=pltpu.HBM)
SEM = pl.BlockSpec(memory_space=pltpu.SEMAPHORE)
EFFECT = pltpu.SideEffectType.DATAFLOW_SIDE_EFFECTING
TOKEN = jax.ShapeDtypeStruct((8, LANE), F32)


def _in_hbm(a):
    return pltpu.with_memory_space_constraint(a, pltpu.HBM)


def _half_rows(ref, s, half):
    hrows = ref.shape[0] // 8
    return ref.at[pl.ds((2 * s + half) * hrows, hrows)]


def _split_start(bufs, copies, n_copies, *, name, before=None):
    n = len(bufs)
    extra = [] if before is None else [before]

    def body(*refs):
        send_sems, recv_sems, token = refs[n + len(extra)], refs[n + len(extra) + 1], refs[-1]
        for k, (src, dst, dev) in enumerate(copies(refs[:n])):
            _rcopy(src, dst, send_sems.at[k], recv_sems.at[k], dev).start()
        token[...] = jnp.zeros_like(token)

    res = _pc(body, name=name, in_specs=[HBM] * n + [ANY] * len(extra),
              out_specs=[SEM, SEM] + [HBM] * n + [pl.BlockSpec(memory_space=pltpu.VMEM)],
              out_shape=[pltpu.SemaphoreType.DMA((n_copies,)), pltpu.SemaphoreType.DMA((n_copies,))]
              + [pltpu.HBM(b.shape, b.dtype) for b in bufs] + [TOKEN],
              input_output_aliases={t: 2 + t for t in range(n)},
              compiler_params=pltpu.CompilerParams(has_side_effects=EFFECT))(*[_in_hbm(b) for b in bufs], *extra)
    return res[0], res[1], list(res[2:2 + n]), res[-1]


def _split_wait(send_sems, recv_sems, bufs, copies, after, *, name):
    n = len(bufs)

    def body(*refs):
        send_ref, recv_ref = refs[n], refs[n + 1]
        for k, (src, dst, dev) in enumerate(copies(refs[:n])):
            cp = _rcopy(src, dst, send_ref.at[k], recv_ref.at[k], dev)
            cp.wait_send()
            cp.wait_recv()

    return _pc(body, name=name, in_specs=[HBM] * n + [SEM, SEM, ANY], out_specs=[HBM] * n,
               out_shape=[pltpu.HBM(b.shape, b.dtype) for b in bufs],
               input_output_aliases={t: t for t in range(n)},
               compiler_params=pltpu.CompilerParams(has_side_effects=EFFECT))(*bufs, send_sems, recv_sems, after)


def _allgather_ici_copies(refs):
    x, y, c, chips = _place()
    return [(_half_rows(r, 2 * x + y, c), _half_rows(r, 2 * x + y, c), (cx, cy, c)) for r in refs for cx, cy in chips]


def _rs_chips_copies(refs):
    x, y, c, chips = _place()
    n = len(refs) // 2
    return [(refs[t].at[2 * cx + cy], refs[n + t].at[j], (cx, cy, c)) for t in range(n) for j, (cx, cy) in enumerate(chips)]


def _rs_sibling_copies(refs):
    x, y, c, _ = _place()
    n = len(refs) // 2
    out = []
    for t in range(n):
        h = refs[t].shape[0] // 8
        out += [(refs[t].at[pl.ds((2 * s + 1 - c) * h, h)], refs[n + t].at[s], (x, y, 1 - c)) for s in range(4)]
    return out


def _allgather_forward(fulls, *, name):
    n = len(fulls)

    def body(*refs):
        outs = refs[n:2 * n]
        send_sems, recv_sems = refs[2 * n:]
        x, y, c, chips = _place()
        sent = []
        for t in range(n):
            for j, (cx, cy) in enumerate(chips):
                landed = _half_rows(outs[t], 2 * cx + cy, c)
                cp = _rcopy(landed, landed, send_sems.at[3 * t + j], recv_sems.at[3 * t + j], (x, y, 1 - c))
                cp.start()
                sent.append(cp)
        for t in range(n):
            for j, (cx, cy) in enumerate(chips):
                other = _half_rows(outs[t], 2 * cx + cy, 1 - c)
                _rcopy(other, other, send_sems.at[3 * t + j], recv_sems.at[3 * t + j], (x, y, 1 - c)).wait_recv()
        for cp in sent:
            cp.wait_send()

    return _pc(body, name=name, in_specs=[ANY] * n, out_specs=[ANY] * n,
               out_shape=[jax.ShapeDtypeStruct(f.shape, f.dtype) for f in fulls],
               input_output_aliases={t: t for t in range(n)},
               scratch_shapes=[pltpu.SemaphoreType.DMA((3 * n,)), pltpu.SemaphoreType.DMA((3 * n,))])(*fulls)


def _rs_sibling(grads, *, name):
    n = len(grads)

    def body(*refs):
        ins, outs = refs[:n], refs[n:2 * n]
        send_sems, recv_sems = refs[2 * n:]
        x, y, c, _ = _place()
        cps = []
        for t in range(n):
            h = ins[t].shape[0] // 8
            for s in range(4):
                cp = _rcopy(ins[t].at[pl.ds((2 * s + 1 - c) * h, h)], outs[t].at[s], send_sems.at[4 * t + s],
                            recv_sems.at[4 * t + s], (x, y, 1 - c))
                cp.start()
                cps.append(cp)
        for cp in cps:
            cp.wait()

    return _pc(body, name=name, in_specs=[ANY] * n, out_specs=[ANY] * n,
               out_shape=[jax.ShapeDtypeStruct((4, g.shape[0] // 8, g.shape[1]), g.dtype) for g in grads],
               scratch_shapes=[pltpu.SemaphoreType.DMA((4 * n,)), pltpu.SemaphoreType.DMA((4 * n,))])(*grads)


def _rs_chips(sends, *, name):
    n = len(sends)

    def body(*refs):
        s_refs, b_refs = refs[:n], refs[n:2 * n]
        send_sems, recv_sems = refs[2 * n:]
        x, y, c, chips = _place()
        cps = []
        for t in range(n):
            for j, (cx, cy) in enumerate(chips):
                cp = _rcopy(s_refs[t].at[2 * cx + cy], b_refs[t].at[j], send_sems.at[3 * t + j], recv_sems.at[3 * t + j],
                            (cx, cy, c))
                cp.start()
                cps.append(cp)
        for cp in cps:
            cp.wait()

    return _pc(body, name=name, in_specs=[ANY] * n, out_specs=[ANY] * n,
               out_shape=[jax.ShapeDtypeStruct((3,) + s.shape[1:], s.dtype) for s in sends],
               scratch_shapes=[pltpu.SemaphoreType.DMA((3 * n,)), pltpu.SemaphoreType.DMA((3 * n,))])(*sends)


def _rs_final(fulls, *, name):
    n = len(fulls)

    def body(*refs):
        outs = refs[n:2 * n]
        send_sems, recv_sems = refs[2 * n:]
        x, y, c, _ = _place()
        cps = []
        for t in range(n):
            cp = _rcopy(outs[t].at[c], outs[t].at[c], send_sems.at[t], recv_sems.at[t], (x, y, 1 - c))
            cp.start()
            cps.append(cp)
        for cp in cps:
            cp.wait()

    return _pc(body, name=name, in_specs=[ANY] * n, out_specs=[ANY] * n,
               out_shape=[jax.ShapeDtypeStruct(f.shape, f.dtype) for f in fulls],
               input_output_aliases={t: t for t in range(n)},
               scratch_shapes=[pltpu.SemaphoreType.DMA((n,)), pltpu.SemaphoreType.DMA((n,))])(*fulls)


def _add_halves(g, a, send_dtype, *, name):
    _, h, cols = a.shape
    th = _row_tile(h, cols)
    g4 = g.reshape(4, 2, h, cols)
    c = lax.axis_index("c").astype(jnp.int32).reshape(1)

    def body(c_ref, g_ref, a_ref, p_ref, s_ref):
        v = g_ref[...] + a_ref[...]
        p_ref[...] = v
        s_ref[...] = v.astype(send_dtype)

    return _pc(body, name=name,
               grid_spec=pltpu.PrefetchScalarGridSpec(
                   num_scalar_prefetch=1, grid=(4, h // th),
                   in_specs=[pl.BlockSpec((None, None, th, cols), lambda s, i, cr: (s, cr[0], i, 0)),
                             pl.BlockSpec((None, th, cols), lambda s, i, cr: (s, i, 0))],
                   out_specs=[pl.BlockSpec((None, th, cols), lambda s, i, cr: (s, i, 0))] * 2),
               out_shape=[jax.ShapeDtypeStruct(a.shape, F32), jax.ShapeDtypeStruct(a.shape, send_dtype)],
               compiler_params=pltpu.CompilerParams(dimension_semantics=("arbitrary", "arbitrary")))(c, g4, a)


def _add_chips(p, b, *, name, order=None):
    _, h, cols = p.shape
    th = _row_tile(h, cols)
    idx = jnp.stack([2 * lax.axis_index("x") + lax.axis_index("y"), lax.axis_index("c")]).astype(jnp.int32)
    extra = [] if order is None else [order]

    def body(idx_ref, p_ref, b_ref, *rest):
        r_ref = rest[-1]
        r_ref[...] = ((p_ref[...] + b_ref[0].astype(F32)) + b_ref[1].astype(F32)) + b_ref[2].astype(F32)

    return _pc(body, name=name,
               grid_spec=pltpu.PrefetchScalarGridSpec(
                   num_scalar_prefetch=1, grid=(h // th,),
                   in_specs=[pl.BlockSpec((None, th, cols), lambda i, ir: (ir[0], i, 0)),
                             pl.BlockSpec((3, th, cols), lambda i, ir: (0, i, 0))] + [ANY] * len(extra),
                   out_specs=pl.BlockSpec((None, th, cols), lambda i, ir: (ir[1], i, 0))),
               out_shape=jax.ShapeDtypeStruct((2, h, cols), F32),
               compiler_params=pltpu.CompilerParams(dimension_semantics=("arbitrary",)))(idx, p, b, *extra)


def _add_halves_all(grads, recv, send_dtypes, tag):
    parts, sends = [], []
    for t, (g, a) in enumerate(zip(grads, recv)):
        p, s = _add_halves(g, a, send_dtypes[t], name=f"rs_add_halves_{tag}{t}")
        parts.append(p)
        sends.append(s)
    return parts, sends


def _rs_finish(parts, others, tag, order=None):
    halves = [_add_chips(p, b, order=order, name=f"rs_add_chips_{tag}{t}") for t, (p, b) in enumerate(zip(parts, others))]
    full = _rs_final(halves, name=f"rs_final_{tag}")
    return [f.reshape(-1, f.shape[-1]) for f in full]


def _s5_discretize(lam_re, lam_im, log_dt, b_re, b_im):
    lam = lax.complex(lam_re, lam_im)
    dt = jnp.exp(log_dt)[:, None]
    lam_bar = jnp.exp(lam * dt)
    b_bar = ((lam_bar - 1.0) / lam)[..., None] * lax.complex(b_re, b_im)
    return jnp.real(lam_bar), jnp.imag(lam_bar), jnp.real(b_bar), jnp.imag(b_bar)


def _lanes_from_gp(re, im, cfg):
    v = jnp.stack([re, im]).reshape(2, cfg.NB, GROUPS_PER_BLOCK, SSM_STATE)
    return jnp.transpose(v, (1, 0, 2, 3)).reshape(1, cfg.NL)


def _gp_from_lanes(v, cfg):
    v = jnp.transpose(v.reshape(cfg.NB, 2, GROUPS_PER_BLOCK, SSM_STATE), (1, 0, 2, 3)).reshape(2, cfg.G, SSM_STATE)
    return v[0], v[1]


def _bb_band(bb_re, bb_im, cfg):
    eye = jnp.eye(GROUPS_PER_BLOCK, dtype=F32)
    bb = jnp.stack([bb_re, bb_im]).reshape(2, cfg.NB, GROUPS_PER_BLOCK, SSM_STATE, SSM_GROUP)
    return jnp.einsum('rjgpc,gh->jgcrhp', bb, eye).reshape(cfg.DS, 2 * GROUPS_PER_BLOCK * SSM_STATE)


def _bb_from_band(m, cfg):
    eye = jnp.eye(GROUPS_PER_BLOCK, dtype=F32)
    m = m.reshape(cfg.NB, GROUPS_PER_BLOCK, SSM_GROUP, 2, GROUPS_PER_BLOCK, SSM_STATE)
    v = jnp.einsum('jgcrhp,gh->rjgpc', m, eye).reshape(2, cfg.G, SSM_STATE, SSM_GROUP)
    return v[0], v[1]


def _cc_band(c_re, c_im, cfg):
    eye = jnp.eye(GROUPS_PER_BLOCK, dtype=F32)
    cc = jnp.stack([c_re, -c_im]).reshape(2, cfg.NB, GROUPS_PER_BLOCK, SSM_GROUP, SSM_STATE)
    return jnp.einsum('rjgcp,gh->jrhpgc', cc, eye).reshape(cfg.NL, GROUPS_PER_BLOCK * SSM_GROUP)


def _cc_from_band(m, cfg):
    eye = jnp.eye(GROUPS_PER_BLOCK, dtype=F32)
    m = m.reshape(cfg.NB, 2, GROUPS_PER_BLOCK, SSM_STATE, GROUPS_PER_BLOCK, SSM_GROUP)
    v = jnp.einsum('jrhpgc,gh->rjgcp', m, eye).reshape(2, cfg.G, SSM_GROUP, SSM_STATE)
    return v[0], -v[1]


PACK_COLS = 512
PACK_ROW_ALIGN = 64


def _pack(arrs):
    flat = jnp.concatenate([a.reshape(-1).astype(F32) for a in arrs])
    unit = PACK_COLS * PACK_ROW_ALIGN
    total = -(-flat.shape[0] // unit) * unit
    return jnp.pad(flat, (0, total - flat.shape[0])).reshape(-1, PACK_COLS)


def _unpack(p, shapes):
    flat = p.reshape(-1)
    out, off = [], 0
    for shp in shapes:
        size = math.prod(shp)
        out.append(flat[off:off + size].reshape(shp))
        off += size
    return out


def _adamw(w, g, m, v, *, name, emit_grad=False):
    c1 = 1.0 / (1.0 - ADAM_B1 ** ADAM_STEP)
    c2 = 1.0 / (1.0 - ADAM_B2 ** ADAM_STEP)

    def fn(rid, wv, gv, mv, vv):
        mn = ADAM_B1 * mv + (1.0 - ADAM_B1) * gv
        vn = ADAM_B2 * vv + (1.0 - ADAM_B2) * (gv * gv)
        delta = -ADAM_LR * ((mn * c1) / (jnp.sqrt(vn * c2) + ADAM_EPS) + ADAM_WD * wv)
        return (gv, delta, mn, vn) if emit_grad else (delta, mn, vn)

    cols = w.shape[1]
    return _ew(fn, [w, g, m, v], [], [(cols, F32)] * (4 if emit_grad else 3), name=name)


def _to_comm_layout(name, w, cfg):
    w = w[0]
    if name == 'w_in':
        return jnp.pad(w, ((0, 0), (0, cfg.DINP - cfg.DIN)))
    if name == 'w_q_b':
        hs = w.shape[1] // (QK_NOPE + QK_ROPE)
        wt = w.T.reshape(hs, QK_NOPE + QK_ROPE, cfg.QL)
        return jnp.pad(wt, ((0, 0), (0, HEAD_SLOT - QK_NOPE - QK_ROPE), (0, 0))).reshape(hs * HEAD_SLOT, cfg.QL)
    if name == 'w_kv_b':
        return w.T
    if name == 'w_up':
        wt = w.T.reshape(2, cfg.F // 4, cfg.D)
        return jnp.pad(wt, ((0, 0), (0, cfg.FQ - cfg.F // 4), (0, 0))).reshape(2 * cfg.FQ, cfg.D)
    if name == 'w_down':
        return jnp.pad(w, ((0, cfg.FQ - cfg.F // 4), (0, 0)))
    return w


def _from_comm_layout(name, g, cfg):
    if name == 'w_in':
        g = g[:, :cfg.DIN]
    elif name == 'w_q_b':
        hs = g.shape[0] // HEAD_SLOT
        g = g.reshape(hs, HEAD_SLOT, cfg.QL)[:, :QK_NOPE + QK_ROPE].reshape(hs * (QK_NOPE + QK_ROPE), cfg.QL).T
    elif name == 'w_kv_b':
        g = g.T
    elif name == 'w_up':
        g = g.reshape(2, cfg.FQ, cfg.D)[:, :cfg.F // 4].reshape(cfg.F // 2, cfg.D).T
    elif name == 'w_down':
        g = g[:cfg.F // 4]
    return g[None]


def _ff_pad(v, cfg):
    k = v.shape[0]
    return jnp.pad(v.reshape(k, 4, cfg.F // 4), ((0, 0), (0, 0), (0, cfg.FQ - cfg.F // 4))).reshape(k, cfg.FP)


def _ff_unpad(v, cfg):
    k = v.shape[0]
    return v.reshape(k, 4, cfg.FQ)[:, :, :cfg.F // 4].reshape(k, cfg.F)


def _step(cfg, w, m, v, x, loss_target):
    lp, d, ds, nl = cfg.LP, cfg.D, cfg.DS, cfg.NL
    xi, yi = lax.axis_index("x"), lax.axis_index("y")
    me = 2 * xi + yi

    placed = [_place_shard(_to_comm_layout(n, w[n], cfg), BF16, name=f"place_{n}") for n in BIG]
    conv_w_shard = jnp.pad(w['conv_w'][0], ((0, ROW_ALIGN - 3), (0, cfg.FQ - cfg.F // 4)))
    placed += [_place_shard(w['meta_tokens'], F32, name="place_meta"), _place_shard(conv_w_shard, F32, name="place_conv_w")]
    w_in, meta_full = _allgather([placed[0], placed[7]], name="allgather_first")
    meta = jnp.transpose(meta_full.reshape(4, N_META, d // 4), (1, 0, 2)).reshape(N_META, d)
    conv_b = _ff_pad(w['conv_b'], cfg)
    mid = placed[1:5] + [placed[8]]
    mid_send, mid_recv, mid_flying, mid_token = _split_start(mid, _allgather_ici_copies, 3 * len(mid), before=meta_full,
                                                             name="allgather_mid_start")
    ffn_send, ffn_recv, ffn_flying, ffn_token = _split_start(placed[5:7], _allgather_ici_copies, 6, before=mid_token,
                                                             name="allgather_ffn_start")
    mix_norm = w['mix_norm'] + (mid_token[0:1, 0:1] + ffn_token[0:1, 0:1])

    pos = (jnp.arange(lp, dtype=jnp.int32) - PAD).astype(F32)
    inv_freq = 1.0 / (ROPE_BASE ** (jnp.arange(0, QK_ROPE, 2, dtype=F32) / QK_ROPE))
    ang = pos[:, None] * inv_freq[None, :]
    zpad = jnp.zeros((lp, LANE - QK_ROPE), F32)
    cos_t = jnp.concatenate([jnp.cos(ang), jnp.cos(ang), zpad], axis=1)
    sin_t = jnp.concatenate([jnp.sin(ang), jnp.sin(ang), zpad], axis=1)

    s5_in = (w['lam_re'][0], w['lam_im'][0], w['log_dt'][0], w['b_re'][0], w['b_im'][0])
    (a_re, a_im, bb_re, bb_im), s5_vjp = jax.vjp(_s5_discretize, *s5_in)
    lam_dt = lax.complex(s5_in[0], s5_in[1]) * jnp.exp(s5_in[2])[:, None]
    a_pow = jnp.exp(jnp.arange(1, 9, dtype=F32)[:, None, None] * lam_dt[None])
    pw_fwd = jnp.concatenate([_lanes_from_gp(jnp.real(a_pow[r]), jnp.imag(a_pow[r]), cfg) for r in range(8)], axis=0)
    pw_bwd = jnp.concatenate([_lanes_from_gp(jnp.real(a_pow[7 - r]), -jnp.imag(a_pow[7 - r]), cfg) for r in range(8)], axis=0)
    bb_band = _bb_band(bb_re, bb_im, cfg).astype(BF16)
    cc_band = _cc_band(w['c_re'][0], w['c_im'][0], cfg).astype(BF16)
    d_skip, b_glu = w['d_skip'], w['b_glu']

    h0 = jnp.concatenate([jnp.zeros((PAD, d), F32), meta, x[0]], axis=0)
    xn = _rms_fwd(h0, mix_norm, name="rms_mix")
    z = _mm(xn, w_in, name="mm_in", tn=_tile(cfg.DINP, 640))
    u = (z, ds, 0)
    q_a = (z, cfg.QL, ds // cfg.QL)
    kv_a = (z, cfg.KVL, (ds + cfg.QL) // cfg.KVL)
    k_pe = (z, LANE, (ds + cfg.QL + cfg.KVL) // LANE)

    hs, yc = _s5_fwd(z, bb_band, cc_band, pw_fwd, cfg, name="s5_fwd")

    def s5_y(ycv, uv, dk):
        return ycv + dk * uv

    gl = _ew(lambda rid, ycv, uv, dk: jax.nn.gelu(s5_y(ycv, uv, dk)), [yc, u], [d_skip], [(ds, BF16)], name="s5_gelu")[0]
    mid_landed = _split_wait(mid_send, mid_recv, mid_flying, _allgather_ici_copies, gl, name="allgather_mid_wait")
    w_glu, w_qt, w_kvt, w_out, conv_full = _allgather_forward(mid_landed, name="allgather_mid_forward")
    conv_w = jnp.transpose(conv_full.reshape(4, ROW_ALIGN, cfg.FQ)[:, :3], (1, 0, 2)).reshape(3, cfg.FP)
    tg = _mm(gl, w_glu, name="mm_glu")
    ya = _ew(lambda rid, ycv, uv, tv, dk, bg: jax.nn.gelu(s5_y(ycv, uv, dk)) * jax.nn.sigmoid(tv + bg),
             [yc, u, tg], [d_skip, b_glu], [(ds, F32)], name="s5_glu")[0]

    qn = _rms_fwd(q_a, w['q_a_norm'], name="rms_q")
    kvn = _rms_fwd(kv_a, w['kv_a_norm'], name="rms_kv")
    q_raw = _mm(qn, w_qt, tb=True, name="mm_q")
    qx = _ew(_rope_heads(_rope, cfg.H), [q_raw, cos_t, sin_t], [], [(cfg.H * HEAD_SLOT, BF16)], name="rope_q")[0]
    kv = _mm(kvn, w_kvt, tb=True, out_dtype=BF16, name="mm_kv")
    kr = _ew(lambda rid, kp, cs, sn: _rope(kp, cs, sn), [k_pe, cos_t, sin_t], [], [(LANE, BF16)], name="rope_k")[0]
    o, lse = _attn_fwd(qx, kv, kr, cfg, name="attn_fwd")

    def norm2(rid, yav, ov, gs, ga):
        return jnp.concatenate([_rms_parts(yav, gs)[0] * gs, _rms_parts(ov, ga)[0] * ga], axis=1)

    yn = _ew(norm2, [ya, o], [w['out_norm_ssm'], w['out_norm_attn']], [(cfg.DMIX, BF16)], name="rms_out")[0]
    h1 = _mm(yn, w_out, res=h0, name="mm_out")
    xn2 = _rms_fwd(h1, w['ffn_norm'], name="rms_ffn")
    ffn_landed = _split_wait(ffn_send, ffn_recv, ffn_flying, _allgather_ici_copies, xn2, name="allgather_ffn_wait")
    w_upt, w_down = _allgather_forward(ffn_landed, name="allgather_ffn_forward")
    up = _mm(xn2, w_upt, tb=True, out_dtype=BF16, name="mm_up")
    act = _conv_fwd(up, conv_w, conv_b, name="conv_fwd")
    h2 = _mm(act, w_down, res=h1, tm=_tile(lp, 544, ROW_ALIGN), name="mm_down")

    g_final = w['final_norm'].reshape(1, d)

    def head(rid, hv, tv, gv):
        xhat, r = _rms_parts(hv, gv)
        valid = rid >= PAD + N_META
        diff = jnp.where(valid, xhat * gv - tv, 0.0)
        dout = diff * (1.0 / d)
        dxhat = dout * gv
        dx = r * (dxhat - xhat * jnp.mean(dxhat * xhat, axis=-1, keepdims=True))
        return dx, dx, dout * xhat, 0.5 * diff * dout

    dh2, dh2_b, dg_final, loss_cols = _ew(head, [h2, (loss_target[0], d, 0, SKIP)], [g_final], [(d, F32), (d, BF16)], [d, d],
                                          tm=PAD + N_META, name="loss_head")
    loss = lax.psum(jnp.sum(loss_cols), ("x", "y", "c"))

    dact = _mm(dh2_b, w_down, tb=True, out_dtype=BF16, name="mm_dact")
    dw_down = _mm(act, dh2_b, ta=True, tn=d, tm=512, name="mm_dw_down")

    def sibling_start(g, tag):
        land = lax.empty((4, g.shape[0] // 8, g.shape[1]), F32)
        return _split_start([g, land], _rs_sibling_copies, 4, name=f"rs_sibling_{tag}_start")

    dn_send, dn_recv, dn_flying, dn_token = sibling_start(dw_down, "down")
    dup, dconv_w, dconv_b = _conv_bwd(up, dact, conv_w, conv_b + dn_token[0:1, 0:1], name="conv_bwd")
    tk_up, tm_up = _tile(cfg.FP, 1408), _tile(cfg.FP, 512)
    dw_upt = _mm(dup, xn2, ta=True, dims=(2 * cfg.FP, d, lp), tn=d, tm=tm_up, a_lead=True, name="mm_dw_up",
                 a_idx=lambda i, j, k: (i // (cfg.FP // tm_up), 0, i % (cfg.FP // tm_up)))
    up_send, up_recv, up_flying, up_token = sibling_start(dw_upt, "up")
    dxn2 = _mm(dup, w_upt, dims=(lp, d, 2 * cfg.FP), tk=tk_up, tn=1024, a_lead=True, name="mm_dxn2",
               a_idx=lambda i, j, k: (k // (cfg.FP // tk_up), i, k % (cfg.FP // tk_up)))
    dh1, dh1_b, dg_ffn = _rms_bwd(h1, w['ffn_norm'] + up_token[0:1, 0:1], dxn2, res=dh2, mask=True, with_bf16=True,
                                  name="rms_ffn_bwd")

    dyn = _mm(dh1_b, w_out, tb=True, name="mm_dyn")
    dw_out = _mm(yn, dh1_b, ta=True, tn=d, tm=512, name="mm_dw_out")
    up_done = _split_wait(up_send, up_recv, up_flying, _rs_sibling_copies, dw_out, name="rs_sibling_up_wait")
    dn_done = _split_wait(dn_send, dn_recv, dn_flying, _rs_sibling_copies, dw_out, name="rs_sibling_down_wait")
    early_parts, early_sends = _add_halves_all([up_done[0], dn_done[0]], [up_done[1], dn_done[1]], [BF16] * 2, "early")
    chip_lands = [lax.empty((3,) + s.shape[1:], s.dtype) for s in early_sends]
    ch_send, ch_recv, ch_flying, ch_token = _split_start(early_sends + chip_lands, _rs_chips_copies, 6,
                                                         name="rs_chips_early_start")
    dya, dg_ssm = _rms_bwd(ya, w['out_norm_ssm'] + ch_token[0:1, 0:1], (dyn, ds, 0), name="rms_ssm_bwd")
    do, dg_attn = _rms_bwd(o, w['out_norm_attn'], (dyn, cfg.DATTN, ds // cfg.DATTN), name="rms_attn_bwd")

    dqx, dkv, dkr = _attn_bwd(qx, kv, kr, o, lse, do, cfg, name="attn_bwd")
    dq_raw = _ew(_rope_heads(_unrope, cfg.H), [dqx, cos_t, sin_t], [], [(cfg.H * HEAD_SLOT, BF16)], name="unrope_q")[0]
    dk_pe = _ew(lambda rid, dk, cs, sn: _unrope(dk, cs, sn), [dkr, cos_t, sin_t], [], [(LANE, F32)], name="unrope_k")[0]
    dqn = _mm(dq_raw, w_qt, name="mm_dqn")
    dw_qt = _mm(dq_raw, qn, ta=True, tm=512, name="mm_dw_q")
    dkvn = _mm(dkv, w_kvt, name="mm_dkvn")
    dw_kvt = _mm(dkv, kvn, ta=True, tm=512, name="mm_dw_kv")
    dq_a, dg_q = _rms_bwd(q_a, w['q_a_norm'], dqn, name="rms_q_bwd")
    dkv_a, dg_kv = _rms_bwd(kv_a, w['kv_a_norm'], dkvn, name="rms_kv_bwd")

    def glu_bwd(rid, ycv, uv, tv, dyav, dk, bg):
        gelu = jax.nn.gelu(s5_y(ycv, uv, dk))
        sg = jax.nn.sigmoid(tv + bg)
        dt = dyav * gelu * sg * (1.0 - sg)
        return dt, dyav * sg, dt

    dt_b, dgl1, db_glu = _ew(glu_bwd, [yc, u, tg, dya], [d_skip, b_glu], [(ds, BF16), (ds, F32)], [ds], name="s5_glu_bwd")
    dgl = _mm(dt_b, w_glu, tb=True, res=dgl1, name="mm_dgl")
    dw_glu = _mm(gl, dt_b, ta=True, tm=512, name="mm_dw_glu")

    def gelu_bwd(rid, ycv, uv, dglv, dk):
        _, vjp = jax.vjp(jax.nn.gelu, s5_y(ycv, uv, dk))
        dy = vjp(dglv)[0]
        return dy, dy * dk, dy * uv

    dy_b, du_skip, dd_skip = _ew(gelu_bwd, [yc, u, dgl], [d_skip], [(ds, BF16), (ds, F32)], [ds], name="s5_gelu_bwd")
    du, dbb_band, dcc_band, da_l = _s5_bwd(dy_b, hs, z, bb_band, cc_band, pw_bwd, du_skip, cfg, name="s5_bwd")

    dz = jnp.concatenate([du, dq_a, dkv_a, dk_pe], axis=1).astype(BF16)
    dxn = _mm(dz, w_in, tb=True, name="mm_dxn")
    dw_in = _mm(xn, dz, ta=True, tm=512, tn=_tile(cfg.DINP, 1024), name="mm_dw_in")
    def mix_bwd(rid, xv, dyv, resv, gv):
        dx, dg = _rms_bwd_block(xv, gv, dyv)
        dx = dx + resv
        return dx, dx, dg

    grad_x, dh0_head, dg_mix = _ew(mix_bwd, [h0, dxn, dh1], [mix_norm], [(d, F32, SKIP), (d, F32, FIRST)], [d],
                                   tm=PAD + N_META, name="rms_mix_bwd")
    grad_x = grad_x[None]

    da_re, da_im = _gp_from_lanes(da_l, cfg)
    dbb_re, dbb_im = _bb_from_band(dbb_band, cfg)
    dlam_re, dlam_im, dlog_dt, db_re, db_im = s5_vjp((da_re, da_im, dbb_re, dbb_im))
    dc_re, dc_im = _cc_from_band(dcc_band, cfg)
    local_small = {
        'meta_tokens': dh0_head[PAD:], 'mix_norm': dg_mix, 'lam_re': dlam_re, 'lam_im': dlam_im, 'log_dt': dlog_dt,
        'b_re': db_re, 'b_im': db_im, 'c_re': dc_re, 'c_im': dc_im, 'd_skip': dd_skip, 'b_glu': db_glu, 'q_a_norm': dg_q,
        'kv_a_norm': dg_kv, 'out_norm_ssm': dg_ssm, 'out_norm_attn': dg_attn, 'ffn_norm': dg_ffn,
        'conv_w': _ff_unpad(dconv_w, cfg), 'conv_b': _ff_unpad(dconv_b, cfg), 'final_norm': dg_final,
    }
    small_shapes = [local_small[n].shape for n in SMALL]

    rest_local = [dw_in, dw_glu, dw_qt, dw_kvt, dw_out, _pack([local_small[n] for n in SMALL])]
    rest_recv = _rs_sibling(rest_local, name="rs_sibling_rest")
    rest_parts, rest_sends = _add_halves_all(rest_local, rest_recv, [BF16] * 5 + [F32], "rest")
    ch_done = _split_wait(ch_send, ch_recv, ch_flying, _rs_chips_copies, rest_sends[0], name="rs_chips_early_wait")
    rest_lands = [lax.empty((3,) + s.shape[1:], s.dtype) for s in rest_sends]
    rc_send, rc_recv, rc_flying, rc_token = _split_start(rest_sends + rest_lands, _rs_chips_copies, 3 * len(rest_sends),
                                                         before=ch_done[2], name="rs_chips_rest_start")
    red_up, red_down = _rs_finish(early_parts, ch_done[2:], "early", order=rc_token)

    delta, new_m, new_v, grads = {}, {}, {}, {}
    padded_rows = ('w_down',)

    def adamw_big(n, red):
        shp = w[n].shape
        w2, m2, v2 = [t.reshape(shp[-2], shp[-1]) for t in (w[n], m[n], v[n])]
        if n in padded_rows:
            g2, dl, mn, vn = _adamw(w2, red, m2, v2, emit_grad=True, name=f"adamw_{n}")
            grads[n] = g2.reshape(shp)
        else:
            grads[n] = _from_comm_layout(n, red, cfg)
            dl, mn, vn = _adamw(w2, grads[n].reshape(shp[-2], shp[-1]), m2, v2, name=f"adamw_{n}")
        delta[n], new_m[n], new_v[n] = dl.reshape(shp), mn.reshape(shp), vn.reshape(shp)

    adamw_big('w_up', red_up)
    adamw_big('w_down', red_down)
    rc_done = _split_wait(rc_send, rc_recv, rc_flying, _rs_chips_copies, delta['w_down'], name="rs_chips_rest_wait")
    red = _rs_finish(rest_parts, rc_done[len(rest_sends):], "rest")
    small_full = _allgather([_place_shard(red[5], F32, name="place_small")], name="allgather_small")[0]
    small_sum = dict(zip(SMALL, _unpack(small_full, small_shapes)))
    for n, r in zip(['w_in', 'w_glu', 'w_q_b', 'w_kv_b', 'w_out'], red[:5]):
        adamw_big(n, r)

    for n in SMALL:
        g = small_sum[n]
        if n == 'meta_tokens':
            g = lax.dynamic_slice_in_dim(g, me * (d // 4), d // 4, axis=1)
        elif n == 'conv_w':
            g = lax.dynamic_slice_in_dim(g, me * (cfg.F // 4), cfg.F // 4, axis=1)[None]
        else:
            g = g.reshape(w[n].shape)
        grads[n] = g

    shapes = [w[n].shape for n in SMALL]
    packs = [_pack([src[n] for n in SMALL]) for src in (w, grads, m, v)]
    for dst, p in zip((delta, new_m, new_v), _adamw(*packs, name="adamw_small")):
        dst.update(zip(SMALL, _unpack(p, shapes)))

    return (loss, grad_x, *[grads[n] for n in WEIGHTS], *[delta[n] for n in WEIGHTS],
            *[new_m[n] for n in WEIGHTS], *[new_v[n] for n in WEIGHTS])


def kernel(x, meta_tokens, mix_norm, w_in, lam_re, lam_im, log_dt, b_re, b_im, c_re, c_im, d_skip, w_glu, b_glu, q_a_norm, w_q_b, kv_a_norm, w_kv_b, out_norm_ssm, out_norm_attn, w_out, ffn_norm, w_up, conv_w, conv_b, w_down, final_norm, loss_target, m_meta_tokens, m_mix_norm, m_w_in, m_lam_re, m_lam_im, m_log_dt, m_b_re, m_b_im, m_c_re, m_c_im, m_d_skip, m_w_glu, m_b_glu, m_q_a_norm, m_w_q_b, m_kv_a_norm, m_w_kv_b, m_out_norm_ssm, m_out_norm_attn, m_w_out, m_ffn_norm, m_w_up, m_conv_w, m_conv_b, m_w_down, m_final_norm, v_meta_tokens, v_mix_norm, v_w_in, v_lam_re, v_lam_im, v_log_dt, v_b_re, v_b_im, v_c_re, v_c_im, v_d_skip, v_w_glu, v_b_glu, v_q_a_norm, v_w_q_b, v_kv_a_norm, v_w_kv_b, v_out_norm_ssm, v_out_norm_attn, v_w_out, v_ffn_norm, v_w_up, v_conv_w, v_conv_b, v_w_down, v_final_norm):
    args = dict(locals())
    w = {n: args[n] for n in WEIGHTS}
    m = {n: args["m_" + n] for n in WEIGHTS}
    v = {n: args["v_" + n] for n in WEIGHTS}
    return _step(PROD, w, m, v, x, loss_target)
```

```python
import functools
import math
from typing import NamedTuple

import jax
import jax.numpy as jnp
from jax import lax
from jax.experimental import pallas as pl
from jax.experimental.pallas import tpu as pltpu

F32, BF16 = jnp.float32, jnp.bfloat16
MESH = pl.DeviceIdType.MESH
LANE = 128
ROW_ALIGN = 16
N_META = 16
PAD = 112
CHUNK = 64
SSM_GROUP = 16
SSM_STATE = 64
GROUPS_PER_BLOCK = 8
QK_NOPE, QK_ROPE, V_HEAD = 128, 64, 128
HEAD_SLOT = 256
ROPE_BASE = 10000.0
EPS = 1e-6
ADAM_LR, ADAM_B1, ADAM_B2, ADAM_EPS, ADAM_WD, ADAM_STEP = 0.001, 0.9, 0.999, 1e-08, 0.01, 10
DT_F32_BLOCK_BYTES = 1 << 20
SKIP, FIRST = "skip", "first"


class Cfg(NamedTuple):
    D: int
    S: int
    DS: int
    H: int
    QL: int
    KVL: int
    F: int

    @property
    def LP(self):
        return PAD + N_META + self.S

    @property
    def G(self):
        return self.DS // SSM_GROUP

    @property
    def NB(self):
        return self.G // GROUPS_PER_BLOCK

    @property
    def NL(self):
        return 2 * self.G * SSM_STATE

    @property
    def DATTN(self):
        return self.H * V_HEAD

    @property
    def DMIX(self):
        return self.DS + self.DATTN

    @property
    def DIN(self):
        return self.DS + self.QL + self.KVL + QK_ROPE

    @property
    def DINP(self):
        return self.DS + self.QL + self.KVL + LANE

    @property
    def FQ(self):
        return -(-(self.F // 4) // LANE) * LANE

    @property
    def FP(self):
        return 4 * self.FQ


PROD = Cfg(D=2048, S=2048, DS=1024, H=8, QL=512, KVL=256, F=5504)

WEIGHTS = ['meta_tokens', 'mix_norm', 'w_in', 'lam_re', 'lam_im', 'log_dt', 'b_re', 'b_im', 'c_re', 'c_im', 'd_skip',
           'w_glu', 'b_glu', 'q_a_norm', 'w_q_b', 'kv_a_norm', 'w_kv_b', 'out_norm_ssm', 'out_norm_attn', 'w_out',
           'ffn_norm', 'w_up', 'conv_w', 'conv_b', 'w_down', 'final_norm']
BIG = ['w_in', 'w_glu', 'w_q_b', 'w_kv_b', 'w_out', 'w_up', 'w_down']
SMALL = [n for n in WEIGHTS if n not in BIG]


def _pc(body, **kw):
    return pl.pallas_call(body, **kw)


def _tile(n, target, align=LANE):
    best = None
    d = align
    while d <= min(n, target):
        if n % d == 0:
            best = d
        d += align
    return best if best is not None else n


def _row_tile(rows, cols):
    return _tile(rows, max(ROW_ALIGN, DT_F32_BLOCK_BYTES // (4 * cols)), ROW_ALIGN)


def _mm(a, b, *, name, ta=False, tb=False, tm=None, tn=512, tk=None, out_dtype=F32, res=None,
        a_idx=None, b_idx=None, dims=None, a_lead=False):
    if dims is None:
        m, k = (a.shape[1], a.shape[0]) if ta else a.shape
        n = b.shape[0] if tb else b.shape[1]
    else:
        m, n, k = dims
    tm = _tile(m, tm or m, LANE if ta else ROW_ALIGN)
    tn = _tile(n, tn)
    tk = _tile(k, tk or k, ROW_ALIGN if (ta and not tb) else LANE)
    nm, nn, nk = m // tm, n // tn, k // tk
    a_idx = a_idx or ((lambda i, j, kk: (kk, i)) if ta else (lambda i, j, kk: (i, kk)))
    b_idx = b_idx or ((lambda i, j, kk: (j, kk)) if tb else (lambda i, j, kk: (kk, j)))
    dn = (((0 if ta else 1,), (1 if tb else 0,)), ((), ()))

    def body(*refs):
        a_ref, b_ref = refs[0], refs[1]
        r_ref = refs[2] if res is not None else None
        o_ref = refs[3] if res is not None else refs[2]
        d = lax.dot_general(a_ref[...].astype(BF16), b_ref[...].astype(BF16), dn, preferred_element_type=F32)

        def finish(r):
            if r_ref is not None:
                r = r + r_ref[...].astype(F32)
            o_ref[...] = r.astype(out_dtype)

        if nk == 1:
            finish(d)
        else:
            acc = refs[-1]
            kk = pl.program_id(2)

            @pl.when(kk == 0)
            def _():
                acc[...] = d

            @pl.when(kk > 0)
            def _():
                acc[...] += d

            @pl.when(kk == nk - 1)
            def _():
                finish(acc[...])

    a_blk = ((None,) if a_lead else ()) + ((tk, tm) if ta else (tm, tk))
    in_specs = [pl.BlockSpec(a_blk, a_idx), pl.BlockSpec((tn, tk) if tb else (tk, tn), b_idx)]
    args = [a, b]
    if res is not None:
        in_specs.append(pl.BlockSpec((tm, tn), lambda i, j, kk: (i, j)))
        args.append(res)
    return _pc(body, name=name, grid=(nm, nn, nk), in_specs=in_specs,
               out_specs=pl.BlockSpec((tm, tn), lambda i, j, kk: (i, j)),
               out_shape=jax.ShapeDtypeStruct((m, n), out_dtype),
               scratch_shapes=[pltpu.VMEM((tm, tn), F32)] if nk > 1 else [],
               compiler_params=pltpu.CompilerParams(dimension_semantics=("parallel", "parallel", "arbitrary")))(*args)


def _ew(fn, ins, vecs, outs, sums=(), *, name, tm=None):
    ins = [x if isinstance(x, tuple) else (x, x.shape[1], 0) for x in ins]
    ins = [x if len(x) == 4 else x + (None,) for x in ins]
    outs = [o if len(o) == 3 else o + (None,) for o in outs]
    rows = ins[0][0].shape[0]
    cmax = max([c for _, c, _, _ in ins] + [c for c, _, _ in outs])
    tm = tm or _row_tile(rows, cmax)
    n_in, n_vec, n_out, n_sum = len(ins), len(vecs), len(outs), len(sums)

    def body(*refs):
        i = pl.program_id(0)
        rid = i * tm + lax.broadcasted_iota(jnp.int32, (tm, 1), 0)
        vals = [r[...] for r in refs[:n_in + n_vec]]
        res = fn(rid, *vals)
        res = res if isinstance(res, (tuple, list)) else (res,)
        o_refs = refs[n_in + n_vec:]
        for o_ref, r, (_, _, mode) in zip(o_refs[:n_out], res[:n_out], outs):
            if mode == FIRST:
                @pl.when(i == 0)
                def _():
                    o_ref[...] = r.astype(o_ref.dtype)
            else:
                o_ref[...] = r.astype(o_ref.dtype)
        for o_ref, r in zip(o_refs[n_out:], res[n_out:]):
            part = jnp.sum(r.astype(F32), axis=0, keepdims=True)

            @pl.when(i == 0)
            def _():
                o_ref[...] = part

            @pl.when(i > 0)
            def _():
                o_ref[...] += part

    def row_idx(mode):
        if mode == SKIP:
            return lambda i, cb=0: (jnp.maximum(i - 1, 0), cb)
        if mode == FIRST:
            return lambda i, cb=0: (0, cb)
        return lambda i, cb=0: (i, cb)

    in_specs = [pl.BlockSpec((tm, c), functools.partial(row_idx(mode), cb=cb)) for _, c, cb, mode in ins]
    in_specs += [pl.BlockSpec(v.shape, functools.partial(lambda i, nd: (0,) * nd, nd=v.ndim)) for v in vecs]
    out_specs = [pl.BlockSpec((tm, c), row_idx(mode)) for c, _, mode in outs]
    out_specs += [pl.BlockSpec((1, c), lambda i: (0, 0)) for c in sums]
    out_rows = {None: rows, SKIP: rows - tm, FIRST: tm}
    out_shape = [jax.ShapeDtypeStruct((out_rows[mode], c), dt) for c, dt, mode in outs]
    out_shape += [jax.ShapeDtypeStruct((1, c), F32) for c in sums]
    return _pc(body, name=name, grid=(rows // tm,), in_specs=in_specs, out_specs=out_specs, out_shape=out_shape,
               compiler_params=pltpu.CompilerParams(dimension_semantics=("arbitrary",)))(*[x[0] for x in ins], *vecs)


def _rms_parts(x, g):
    r = lax.rsqrt(jnp.mean(x * x, axis=-1, keepdims=True) + EPS)
    return x * r, r


def _rms_bwd_block(x, g, dy):
    xhat, r = _rms_parts(x, g)
    dxhat = dy * g
    dx = r * (dxhat - xhat * jnp.mean(dxhat * xhat, axis=-1, keepdims=True))
    return dx, dy * xhat


def _rms_fwd(x, g, *, name):
    c = x[1] if isinstance(x, tuple) else x.shape[1]
    return _ew(lambda rid, xv, gv: _rms_parts(xv.astype(F32), gv)[0] * gv, [x], [g], [(c, BF16)], name=name)[0]


def _rms_bwd(x, g, dy, *, name, res=None, mask=False, with_bf16=False):
    c = x[1] if isinstance(x, tuple) else x.shape[1]

    def fn(rid, xv, dyv, *rest):
        gv = rest[-1]
        dx, dg = _rms_bwd_block(xv.astype(F32), gv, dyv.astype(F32))
        if res is not None:
            dx = dx + rest[0]
        if mask:
            dx = jnp.where(rid >= PAD, dx, 0.0)
        return (dx, dx, dg) if with_bf16 else (dx, dg)

    ins = [x, dy] + ([res] if res is not None else [])
    outs = [(c, F32)] + ([(c, BF16)] if with_bf16 else [])
    return _ew(fn, ins, [g], outs, [c], name=name)


S5_W = GROUPS_PER_BLOCK * SSM_STATE
S5_GW = GROUPS_PER_BLOCK * SSM_GROUP
S5_UNROLL = 8
S5_DA_ROWS = 272


def _s5_scan_in_place(ref, pw_ref, *, reverse):
    lp = ref.shape[0]
    tile_rows = 8
    chunk = _tile(lp, S5_DA_ROWS, tile_rows)
    sub = lax.broadcasted_iota(jnp.int32, (chunk, 1), 0) % tile_rows

    def chunk_body(c, carry):
        rows = pl.ds(pl.multiple_of(c * chunk, tile_rows), chunk)
        xr, xi = ref[rows, :S5_W], ref[rows, S5_W:]
        for k in (1, 2, 4):
            row = tile_rows - k if reverse else k - 1
            mr, mi = pw_ref[row:row + 1, :S5_W], pw_ref[row:row + 1, S5_W:]
            shift = chunk - k if reverse else k
            sr, si = pltpu.roll(xr, shift, 0), pltpu.roll(xi, shift, 0)
            keep = (sub < tile_rows - k) if reverse else (sub >= k)
            xr, xi = xr + jnp.where(keep, mr * sr - mi * si, 0.0), xi + jnp.where(keep, mr * si + mi * sr, 0.0)
        ref[rows, :S5_W] = xr
        ref[rows, S5_W:] = xi
        return carry

    lax.fori_loop(0, lp // chunk, chunk_body, 0)

    pr, pi = pw_ref[:, :S5_W], pw_ref[:, S5_W:]
    ntile = lp // tile_rows
    unroll = 4

    def step(n, carry):
        cr, ci = carry
        for q in range(unroll):
            j = n * unroll + q
            j = ntile - 1 - j if reverse else j
            rows = pl.ds(pl.multiple_of(j * tile_rows, tile_rows), tile_rows)
            nr = ref[rows, :S5_W] + (pr * cr - pi * ci)
            ni = ref[rows, S5_W:] + (pr * ci + pi * cr)
            ref[rows, :S5_W] = nr
            ref[rows, S5_W:] = ni
            cr, ci = (nr[0:1], ni[0:1]) if reverse else (nr[tile_rows - 1:], ni[tile_rows - 1:])
        return cr, ci

    z = jnp.zeros((1, S5_W), F32)
    lax.fori_loop(0, ntile // unroll, step, (z, z))


def _s5_fwd(z, bb_band, cc_band, a_l, cfg, *, name):
    lp, ds, nl = cfg.LP, cfg.DS, cfg.NL

    def body(u_ref, bb_ref, cc_ref, a_ref, hs_ref, y_ref):
        hs_ref[...] = jnp.dot(u_ref[...].astype(BF16), bb_ref[...], preferred_element_type=F32)
        _s5_scan_in_place(hs_ref, a_ref, reverse=False)
        y_ref[...] = jnp.dot(hs_ref[...].astype(BF16), cc_ref[...], preferred_element_type=F32)

    return _pc(body, name=name, grid=(cfg.NB,),
               in_specs=[pl.BlockSpec((lp, S5_GW), lambda j: (0, j)), pl.BlockSpec((S5_GW, 2 * S5_W), lambda j: (j, 0)),
                         pl.BlockSpec((2 * S5_W, S5_GW), lambda j: (j, 0)), pl.BlockSpec((8, 2 * S5_W), lambda j: (0, j))],
               out_specs=[pl.BlockSpec((lp, 2 * S5_W), lambda j: (0, j)), pl.BlockSpec((lp, S5_GW), lambda j: (0, j))],
               out_shape=[jax.ShapeDtypeStruct((lp, nl), F32), jax.ShapeDtypeStruct((lp, ds), F32)],
               compiler_params=pltpu.CompilerParams(dimension_semantics=("parallel",)))(z, bb_band, cc_band, a_l)


def _s5_bwd(dy, hs, z, bb_band, cc_band, a_l, du_skip, cfg, *, name):
    lp, ds, nl = cfg.LP, cfg.DS, cfg.NL
    nt = (((1,), (1,)), ((), ()))
    tn = (((0,), (0,)), ((), ()))

    def body(dy_ref, hs_ref, u_ref, bb_ref, cc_ref, a_ref, sk_ref, du_ref, dbb_ref, dcc_ref, da_ref, g_ref):
        dyv = dy_ref[...]
        g_ref[...] = lax.dot_general(dyv, cc_ref[...], nt, preferred_element_type=F32)
        _s5_scan_in_place(g_ref, a_ref, reverse=True)
        dcc_ref[...] = lax.dot_general(hs_ref[...].astype(BF16), dyv, tn, preferred_element_type=F32)
        gb = g_ref[...].astype(BF16)
        dbb_ref[...] = lax.dot_general(u_ref[...].astype(BF16), gb, tn, preferred_element_type=F32)
        du_ref[...] = lax.dot_general(gb, bb_ref[...], nt, preferred_element_type=F32) + sk_ref[...]
        dre = jnp.zeros((1, S5_W), F32)
        dim = jnp.zeros((1, S5_W), F32)
        for r0 in range(0, lp, S5_DA_ROWS):
            rows = min(S5_DA_ROWS, lp - r0)
            first = lax.broadcasted_iota(jnp.int32, (rows, 1), 0) == 0
            prev = hs_ref[r0 - 1:r0, :] if r0 else jnp.zeros((1, 2 * S5_W), F32)
            hr = jnp.where(first, prev[:, :S5_W], pltpu.roll(hs_ref[r0:r0 + rows, :S5_W], 1, 0))
            hi = jnp.where(first, prev[:, S5_W:], pltpu.roll(hs_ref[r0:r0 + rows, S5_W:], 1, 0))
            gr, gi = g_ref[r0:r0 + rows, :S5_W], g_ref[r0:r0 + rows, S5_W:]
            dre = dre + jnp.sum(gr * hr + gi * hi, axis=0, keepdims=True)
            dim = dim + jnp.sum(gi * hr - gr * hi, axis=0, keepdims=True)
        da_ref[:, :S5_W] = dre
        da_ref[:, S5_W:] = dim

    col_blk = pl.BlockSpec((lp, S5_GW), lambda j: (0, j))
    lane_blk = pl.BlockSpec((lp, 2 * S5_W), lambda j: (0, j))
    bb_blk = pl.BlockSpec((S5_GW, 2 * S5_W), lambda j: (j, 0))
    cc_blk = pl.BlockSpec((2 * S5_W, S5_GW), lambda j: (j, 0))
    a_blk = pl.BlockSpec((1, 2 * S5_W), lambda j: (0, j))
    pw_blk = pl.BlockSpec((8, 2 * S5_W), lambda j: (0, j))
    return _pc(body, name=name, grid=(cfg.NB,),
               in_specs=[col_blk, lane_blk, col_blk, bb_blk, cc_blk, pw_blk, col_blk],
               out_specs=[col_blk, bb_blk, cc_blk, a_blk],
               out_shape=[jax.ShapeDtypeStruct((lp, ds), F32), jax.ShapeDtypeStruct((ds, 2 * S5_W), F32),
                          jax.ShapeDtypeStruct((nl, S5_GW), F32), jax.ShapeDtypeStruct((1, nl), F32)],
               scratch_shapes=[pltpu.VMEM((lp, 2 * S5_W), F32)],
               compiler_params=pltpu.CompilerParams(dimension_semantics=("parallel",)))(dy, hs, z, bb_band, cc_band, a_l, du_skip)


def _conv_gate(pre, cw, cb):
    return cw[0:1] * pltpu.roll(pre, 2, 0) + cw[1:2] * pltpu.roll(pre, 1, 0) + cw[2:3] * pre + cb


def _conv_fwd(up, cw, cb, *, name):
    lp, fp2 = up.shape
    fp = fp2 // 2
    tc = _tile(fp, 256)
    nb = fp // tc

    def body(pre_ref, val_ref, cw_ref, cb_ref, o_ref):
        gate = _conv_gate(pre_ref[...].astype(F32), cw_ref[...], cb_ref[...])
        o_ref[...] = (jax.nn.silu(gate) * val_ref[...].astype(F32)).astype(BF16)

    return _pc(body, name=name, grid=(nb,),
               in_specs=[pl.BlockSpec((lp, tc), lambda j: (0, j)), pl.BlockSpec((lp, tc), lambda j: (0, nb + j)),
                         pl.BlockSpec((3, tc), lambda j: (0, j)), pl.BlockSpec((1, tc), lambda j: (0, j))],
               out_specs=pl.BlockSpec((lp, tc), lambda j: (0, j)),
               out_shape=jax.ShapeDtypeStruct((lp, fp), BF16),
               compiler_params=pltpu.CompilerParams(dimension_semantics=("parallel",)))(up, up, cw, cb)


def _conv_bwd(up, dact, cw, cb, *, name):
    lp, fp2 = up.shape
    fp = fp2 // 2
    tc = _tile(fp, 256)
    nb = fp // tc

    def body(pre_ref, val_ref, da_ref, cw_ref, cb_ref, dup_ref, dcw_ref, dcb_ref):
        pre, val, da, cwv = pre_ref[...].astype(F32), val_ref[...].astype(F32), da_ref[...].astype(F32), cw_ref[...]
        gate = _conv_gate(pre, cwv, cb_ref[...])
        sg = jax.nn.sigmoid(gate)
        dup_ref[1] = (da * (gate * sg)).astype(BF16)
        dgate = da * val * (sg * (1.0 + gate * (1.0 - sg)))
        dpre = cwv[2:3] * dgate + cwv[1:2] * pltpu.roll(dgate, lp - 1, 0) + cwv[0:1] * pltpu.roll(dgate, lp - 2, 0)
        dup_ref[0] = dpre.astype(BF16)
        dcb_ref[...] = jnp.sum(dgate, axis=0, keepdims=True)
        dcw_ref[0:1, :] = jnp.sum(dgate * pltpu.roll(pre, 2, 0), axis=0, keepdims=True)
        dcw_ref[1:2, :] = jnp.sum(dgate * pltpu.roll(pre, 1, 0), axis=0, keepdims=True)
        dcw_ref[2:3, :] = jnp.sum(dgate * pre, axis=0, keepdims=True)

    return _pc(body, name=name, grid=(nb,),
               in_specs=[pl.BlockSpec((lp, tc), lambda j: (0, j)), pl.BlockSpec((lp, tc), lambda j: (0, nb + j)),
                         pl.BlockSpec((lp, tc), lambda j: (0, j)),
                         pl.BlockSpec((3, tc), lambda j: (0, j)), pl.BlockSpec((1, tc), lambda j: (0, j))],
               out_specs=[pl.BlockSpec((2, lp, tc), lambda j: (0, 0, j)),
                          pl.BlockSpec((3, tc), lambda j: (0, j)), pl.BlockSpec((1, tc), lambda j: (0, j))],
               out_shape=[jax.ShapeDtypeStruct((2, lp, fp), BF16), jax.ShapeDtypeStruct((3, fp), F32),
                          jax.ShapeDtypeStruct((1, fp), F32)],
               compiler_params=pltpu.CompilerParams(dimension_semantics=("parallel",)))(up, up, dact, cw, cb)


def _key_limit(i, tq, lp):
    return min(lp, -(-((i + 1) * tq) // LANE) * LANE)


def _attn_mask(i, tq, nk):
    qrow = i * tq + lax.broadcasted_iota(jnp.int32, (tq, 1), 0)
    krow = lax.broadcasted_iota(jnp.int32, (1, nk), 1)
    return (krow >= PAD) & ((krow // CHUNK) <= (qrow // CHUNK)), qrow >= PAD


def _attn_scores(q, kn, kr, i, tq, scale):
    nt = (((1,), (1,)), ((), ()))
    s = lax.dot_general(q[:, :QK_NOPE], kn, nt, preferred_element_type=F32)
    s = s + lax.dot_general(q[:, QK_NOPE:], kr, nt, preferred_element_type=F32)
    mask, qvalid = _attn_mask(i, tq, kn.shape[0])
    return jnp.where(mask, s * scale, jnp.finfo(F32).min), qvalid


def _per_q_block(nq, fn):
    i = pl.program_id(1)
    for blk in range(nq):
        pl.when(i == blk)(functools.partial(fn, blk))


def _attn_fwd(qx, kv, kr, cfg, *, name):
    lp, h = cfg.LP, cfg.H
    tq = _tile(lp, 272, ROW_ALIGN)
    nq = lp // tq
    scale = 1.0 / math.sqrt(QK_NOPE + QK_ROPE)

    def body(q_ref, kn_ref, v_ref, kr_ref, o_ref, lse_ref):
        def block(blk):
            nk = _key_limit(blk, tq, lp)
            s, qvalid = _attn_scores(q_ref[...], kn_ref[:nk], kr_ref[:nk], blk, tq, scale)
            m = jnp.max(s, axis=-1, keepdims=True)
            p = jnp.exp(s - m)
            l = jnp.sum(p, axis=-1, keepdims=True)
            o = jnp.dot(p.astype(BF16), v_ref[:nk], preferred_element_type=F32) / l
            o_ref[...] = jnp.where(qvalid, o, 0.0)
            lse_ref[...] = m + jnp.log(l)

        _per_q_block(nq, block)

    return _pc(body, name=name, grid=(h, nq),
               in_specs=[pl.BlockSpec((tq, HEAD_SLOT), lambda hh, i: (i, hh)),
                         pl.BlockSpec((lp, QK_NOPE), lambda hh, i: (0, 2 * hh)),
                         pl.BlockSpec((lp, V_HEAD), lambda hh, i: (0, 2 * hh + 1)),
                         pl.BlockSpec((lp, LANE), lambda hh, i: (0, 0))],
               out_specs=[pl.BlockSpec((tq, V_HEAD), lambda hh, i: (i, hh)),
                          pl.BlockSpec((None, tq, 1), lambda hh, i: (hh, i, 0))],
               out_shape=[jax.ShapeDtypeStruct((lp, h * V_HEAD), F32), jax.ShapeDtypeStruct((h, lp, 1), F32)],
               compiler_params=pltpu.CompilerParams(dimension_semantics=("parallel", "parallel")))(qx, kv, kv, kr)


def _attn_bwd(qx, kv, kr, o, lse, do, cfg, *, name):
    lp, h = cfg.LP, cfg.H
    tq = _tile(lp, 272, ROW_ALIGN)
    nq = lp // tq
    scale = 1.0 / math.sqrt(QK_NOPE + QK_ROPE)
    tn_dims = (((0,), (0,)), ((), ()))

    def body(q_ref, kn_ref, v_ref, kr_ref, o_ref, lse_ref, do_ref, dq_ref, dkv_ref, dkr_ref, dkv_acc):
        hh, i = pl.program_id(0), pl.program_id(1)

        @pl.when(i == 0)
        def _():
            dkv_acc[...] = jnp.zeros_like(dkv_acc)

        @pl.when((i == 0) & (hh == 0))
        def _():
            dkr_ref[...] = jnp.zeros_like(dkr_ref)

        def block(blk):
            nk = _key_limit(blk, tq, lp)
            q, kn, v, krv = q_ref[...], kn_ref[:nk], v_ref[:nk], kr_ref[:nk]
            s, qvalid = _attn_scores(q, kn, krv, blk, tq, scale)
            dov = jnp.where(qvalid, do_ref[...], 0.0)
            p = jnp.exp(s - lse_ref[...])
            delta = jnp.sum(dov * o_ref[...], axis=-1, keepdims=True)
            dob = dov.astype(BF16)
            dp = lax.dot_general(dob, v, (((1,), (1,)), ((), ())), preferred_element_type=F32)
            ds = (p * (dp - delta) * scale).astype(BF16)
            dq_ref[:, :QK_NOPE] = jnp.dot(ds, kn, preferred_element_type=F32)
            dq_ref[:, QK_NOPE:] = jnp.dot(ds, krv, preferred_element_type=F32)
            dkv_acc[:nk, :QK_NOPE] += lax.dot_general(ds, q[:, :QK_NOPE], tn_dims, preferred_element_type=F32)
            dkv_acc[:nk, QK_NOPE:] += lax.dot_general(p.astype(BF16), dob, tn_dims, preferred_element_type=F32)
            dkr_ref[:nk, :] += lax.dot_general(ds, q[:, QK_NOPE:], tn_dims, preferred_element_type=F32)

        _per_q_block(nq, block)

        @pl.when(i == nq - 1)
        def _():
            dkv_ref[...] = dkv_acc[...].astype(BF16)

    return _pc(body, name=name, grid=(h, nq),
               in_specs=[pl.BlockSpec((tq, HEAD_SLOT), lambda hh, i: (i, hh)),
                         pl.BlockSpec((lp, QK_NOPE), lambda hh, i: (0, 2 * hh)),
                         pl.BlockSpec((lp, V_HEAD), lambda hh, i: (0, 2 * hh + 1)),
                         pl.BlockSpec((lp, LANE), lambda hh, i: (0, 0)),
                         pl.BlockSpec((tq, V_HEAD), lambda hh, i: (i, hh)),
                         pl.BlockSpec((None, tq, 1), lambda hh, i: (hh, i, 0)),
                         pl.BlockSpec((tq, V_HEAD), lambda hh, i: (i, hh))],
               out_specs=[pl.BlockSpec((tq, HEAD_SLOT), lambda hh, i: (i, hh)),
                          pl.BlockSpec((lp, QK_NOPE + V_HEAD), lambda hh, i: (0, hh)),
                          pl.BlockSpec((lp, LANE), lambda hh, i: (0, 0))],
               out_shape=[jax.ShapeDtypeStruct((lp, h * HEAD_SLOT), F32),
                          jax.ShapeDtypeStruct((lp, h * (QK_NOPE + V_HEAD)), BF16),
                          jax.ShapeDtypeStruct((lp, LANE), F32)],
               scratch_shapes=[pltpu.VMEM((lp, QK_NOPE + V_HEAD), F32)],
               compiler_params=pltpu.CompilerParams(dimension_semantics=("arbitrary", "arbitrary")))(qx, kv, kv, kr, o, lse, do)


def _rot_half(x):
    lane = lax.broadcasted_iota(jnp.int32, x.shape, 1)
    half = QK_ROPE // 2
    return jnp.where(lane < half, -pltpu.roll(x, LANE - half, 1), pltpu.roll(x, half, 1))


def _rope(x, cos, sin):
    return x * cos + _rot_half(x) * sin


def _unrope(dy, cos, sin):
    return dy * cos - _rot_half(dy * sin)


def _rope_heads(fn, h):
    def apply(rid, q, cos, sin):
        parts = []
        for hh in range(h):
            parts.append(q[:, hh * HEAD_SLOT: hh * HEAD_SLOT + QK_NOPE])
            parts.append(fn(q[:, hh * HEAD_SLOT + QK_NOPE: (hh + 1) * HEAD_SLOT], cos, sin))
        return jnp.concatenate(parts, axis=1)
    return apply


ANY = pl.BlockSpec(memory_space=pl.ANY)


def _place():
    x, y, c = lax.axis_index("x"), lax.axis_index("y"), lax.axis_index("c")
    chips = [(1 - x, y), (x, 1 - y), (1 - x, 1 - y)]
    return x, y, c, chips


def _rcopy(src, dst, send_sem, recv_sem, dev):
    return pltpu.make_async_remote_copy(src_ref=src, dst_ref=dst, send_sem=send_sem, recv_sem=recv_sem,
                                        device_id=dev, device_id_type=MESH)


def _place_shard(shard, dtype, *, name):
    r, cols = shard.shape
    tm = _row_tile(r, cols)
    nblk = r // tm
    me = (2 * lax.axis_index("x") + lax.axis_index("y")).astype(jnp.int32).reshape(1)

    def body(me_ref, s_ref, o_ref):
        o_ref[...] = s_ref[...].astype(dtype)

    return _pc(body, name=name,
               grid_spec=pltpu.PrefetchScalarGridSpec(
                   num_scalar_prefetch=1, grid=(nblk,),
                   in_specs=[pl.BlockSpec((tm, cols), lambda i, mr: (i, 0))],
                   out_specs=pl.BlockSpec((tm, cols), lambda i, mr: (mr[0] * nblk + i, 0))),
               out_shape=jax.ShapeDtypeStruct((4 * r, cols), dtype),
               compiler_params=pltpu.CompilerParams(dimension_semantics=("arbitrary",)))(me, shard)


def _allgather(fulls, *, name):
    n = len(fulls)

    def body(*refs):
        outs = refs[n:2 * n]
        send_sems, recv_sems = refs[2 * n:]
        x, y, c, chips = _place()
        sib = (x, y, 1 - c)
        me = 2 * x + y

        def rows(t, s, half):
            hrows = outs[t].shape[0] // 8
            return outs[t].at[pl.ds((2 * s + half) * hrows, hrows)]

        sent = []
        for t in range(n):
            for j, (cx, cy) in enumerate(chips):
                cp = _rcopy(rows(t, me, c), rows(t, me, c), send_sems.at[6 * t + j], recv_sems.at[6 * t + j], (cx, cy, c))
                cp.start()
                sent.append(cp)
        for t in range(n):
            for j, (cx, cy) in enumerate(chips):
                landed = rows(t, 2 * cx + cy, c)
                _rcopy(landed, landed, send_sems.at[6 * t + j], recv_sems.at[6 * t + j], (cx, cy, c)).wait_recv()
                cp = _rcopy(landed, landed, send_sems.at[6 * t + 3 + j], recv_sems.at[6 * t + 3 + j], sib)
                cp.start()
                sent.append(cp)
        for t in range(n):
            for j, (cx, cy) in enumerate(chips):
                other = rows(t, 2 * cx + cy, 1 - c)
                _rcopy(other, other, send_sems.at[6 * t + 3 + j], recv_sems.at[6 * t + 3 + j], sib).wait_recv()
        for cp in sent:
            cp.wait_send()

    return _pc(body, name=name, in_specs=[ANY] * n, out_specs=[ANY] * n,
               out_shape=[jax.ShapeDtypeStruct(f.shape, f.dtype) for f in fulls],
               input_output_aliases={t: t for t in range(n)},
               scratch_shapes=[pltpu.SemaphoreType.DMA((6 * n,)), pltpu.SemaphoreType.DMA((6 * n,))])(*fulls)


HBM = pl.BlockSpec(memory_space=pltpu.HBM)
SEM = pl.BlockSpec(memory_space=pltpu.SEMAPHORE)
EFFECT = pltpu.SideEffectType.DATAFLOW_SIDE_EFFECTING
TOKEN = jax.ShapeDtypeStruct((8, LANE), F32)


def _in_hbm(a):
    return pltpu.with_memory_space_constraint(a, pltpu.HBM)


def _half_rows(ref, s, half):
    hrows = ref.shape[0] // 8
    return ref.at[pl.ds((2 * s + half) * hrows, hrows)]


def _split_start(bufs, copies, n_copies, *, name, before=None):
    n = len(bufs)
    extra = [] if before is None else [before]

    def body(*refs):
        send_sems, recv_sems, token = refs[n + len(extra)], refs[n + len(extra) + 1], refs[-1]
        for k, (src, dst, dev) in enumerate(copies(refs[:n])):
            _rcopy(src, dst, send_sems.at[k], recv_sems.at[k], dev).start()
        token[...] = jnp.zeros_like(token)

    res = _pc(body, name=name, in_specs=[HBM] * n + [ANY] * len(extra),
              out_specs=[SEM, SEM] + [HBM] * n + [pl.BlockSpec(memory_space=pltpu.VMEM)],
              out_shape=[pltpu.SemaphoreType.DMA((n_copies,)), pltpu.SemaphoreType.DMA((n_copies,))]
              + [pltpu.HBM(b.shape, b.dtype) for b in bufs] + [TOKEN],
              input_output_aliases={t: 2 + t for t in range(n)},
              compiler_params=pltpu.CompilerParams(has_side_effects=EFFECT))(*[_in_hbm(b) for b in bufs], *extra)
    return res[0], res[1], list(res[2:2 + n]), res[-1]


def _split_wait(send_sems, recv_sems, bufs, copies, after, *, name):
    n = len(bufs)

    def body(*refs):
        send_ref, recv_ref = refs[n], refs[n + 1]
        for k, (src, dst, dev) in enumerate(copies(refs[:n])):
            cp = _rcopy(src, dst, send_ref.at[k], recv_ref.at[k], dev)
            cp.wait_send()
            cp.wait_recv()

    return _pc(body, name=name, in_specs=[HBM] * n + [SEM, SEM, ANY], out_specs=[HBM] * n,
               out_shape=[pltpu.HBM(b.shape, b.dtype) for b in bufs],
               input_output_aliases={t: t for t in range(n)},
               compiler_params=pltpu.CompilerParams(has_side_effects=EFFECT))(*bufs, send_sems, recv_sems, after)


def _allgather_ici_copies(refs):
    x, y, c, chips = _place()
    return [(_half_rows(r, 2 * x + y, c), _half_rows(r, 2 * x + y, c), (cx, cy, c)) for r in refs for cx, cy in chips]


def _rs_chips_copies(refs):
    x, y, c, chips = _place()
    n = len(refs) // 2
    return [(refs[t].at[2 * cx + cy], refs[n + t].at[j], (cx, cy, c)) for t in range(n) for j, (cx, cy) in enumerate(chips)]


def _rs_final_copies(refs):
    x, y, c, _ = _place()
    return [(r.at[c], r.at[c], (x, y, 1 - c)) for r in refs]


def _rs_sibling_copies(refs):
    x, y, c, _ = _place()
    n = len(refs) // 2
    out = []
    for t in range(n):
        h = refs[t].shape[0] // 8
        out += [(refs[t].at[pl.ds((2 * s + 1 - c) * h, h)], refs[n + t].at[s], (x, y, 1 - c)) for s in range(4)]
    return out


def _allgather_forward(fulls, *, name):
    n = len(fulls)

    def body(*refs):
        outs = refs[n:2 * n]
        send_sems, recv_sems = refs[2 * n:]
        x, y, c, chips = _place()
        sent = []
        for t in range(n):
            for j, (cx, cy) in enumerate(chips):
                landed = _half_rows(outs[t], 2 * cx + cy, c)
                cp = _rcopy(landed, landed, send_sems.at[3 * t + j], recv_sems.at[3 * t + j], (x, y, 1 - c))
                cp.start()
                sent.append(cp)
        for t in range(n):
            for j, (cx, cy) in enumerate(chips):
                other = _half_rows(outs[t], 2 * cx + cy, 1 - c)
                _rcopy(other, other, send_sems.at[3 * t + j], recv_sems.at[3 * t + j], (x, y, 1 - c)).wait_recv()
        for cp in sent:
            cp.wait_send()

    return _pc(body, name=name, in_specs=[ANY] * n, out_specs=[ANY] * n,
               out_shape=[jax.ShapeDtypeStruct(f.shape, f.dtype) for f in fulls],
               input_output_aliases={t: t for t in range(n)},
               scratch_shapes=[pltpu.SemaphoreType.DMA((3 * n,)), pltpu.SemaphoreType.DMA((3 * n,))])(*fulls)


def _rs_sibling(grads, *, name):
    n = len(grads)

    def body(*refs):
        ins, outs = refs[:n], refs[n:2 * n]
        send_sems, recv_sems = refs[2 * n:]
        x, y, c, _ = _place()
        cps = []
        for t in range(n):
            h = ins[t].shape[0] // 8
            for s in range(4):
                cp = _rcopy(ins[t].at[pl.ds((2 * s + 1 - c) * h, h)], outs[t].at[s], send_sems.at[4 * t + s],
                            recv_sems.at[4 * t + s], (x, y, 1 - c))
                cp.start()
                cps.append(cp)
        for cp in cps:
            cp.wait()

    return _pc(body, name=name, in_specs=[ANY] * n, out_specs=[ANY] * n,
               out_shape=[jax.ShapeDtypeStruct((4, g.shape[0] // 8, g.shape[1]), g.dtype) for g in grads],
               scratch_shapes=[pltpu.SemaphoreType.DMA((4 * n,)), pltpu.SemaphoreType.DMA((4 * n,))])(*grads)


def _rs_chips(sends, *, name):
    n = len(sends)

    def body(*refs):
        s_refs, b_refs = refs[:n], refs[n:2 * n]
        send_sems, recv_sems = refs[2 * n:]
        x, y, c, chips = _place()
        cps = []
        for t in range(n):
            for j, (cx, cy) in enumerate(chips):
                cp = _rcopy(s_refs[t].at[2 * cx + cy], b_refs[t].at[j], send_sems.at[3 * t + j], recv_sems.at[3 * t + j],
                            (cx, cy, c))
                cp.start()
                cps.append(cp)
        for cp in cps:
            cp.wait()

    return _pc(body, name=name, in_specs=[ANY] * n, out_specs=[ANY] * n,
               out_shape=[jax.ShapeDtypeStruct((3,) + s.shape[1:], s.dtype) for s in sends],
               scratch_shapes=[pltpu.SemaphoreType.DMA((3 * n,)), pltpu.SemaphoreType.DMA((3 * n,))])(*sends)


def _rs_final(fulls, *, name):
    n = len(fulls)

    def body(*refs):
        outs = refs[n:2 * n]
        send_sems, recv_sems = refs[2 * n:]
        x, y, c, _ = _place()
        cps = []
        for t in range(n):
            cp = _rcopy(outs[t].at[c], outs[t].at[c], send_sems.at[t], recv_sems.at[t], (x, y, 1 - c))
            cp.start()
            cps.append(cp)
        for cp in cps:
            cp.wait()

    return _pc(body, name=name, in_specs=[ANY] * n, out_specs=[ANY] * n,
               out_shape=[jax.ShapeDtypeStruct(f.shape, f.dtype) for f in fulls],
               input_output_aliases={t: t for t in range(n)},
               scratch_shapes=[pltpu.SemaphoreType.DMA((n,)), pltpu.SemaphoreType.DMA((n,))])(*fulls)


def _add_halves(g, a, send_dtype, *, name):
    _, h, cols = a.shape
    th = _row_tile(h, cols)
    g4 = g.reshape(4, 2, h, cols)
    c = lax.axis_index("c").astype(jnp.int32).reshape(1)

    def body(c_ref, g_ref, a_ref, p_ref, s_ref):
        v = g_ref[...].astype(F32) + a_ref[...].astype(F32)
        p_ref[...] = v
        s_ref[...] = v.astype(send_dtype)

    return _pc(body, name=name,
               grid_spec=pltpu.PrefetchScalarGridSpec(
                   num_scalar_prefetch=1, grid=(4, h // th),
                   in_specs=[pl.BlockSpec((None, None, th, cols), lambda s, i, cr: (s, cr[0], i, 0)),
                             pl.BlockSpec((None, th, cols), lambda s, i, cr: (s, i, 0))],
                   out_specs=[pl.BlockSpec((None, th, cols), lambda s, i, cr: (s, i, 0))] * 2),
               out_shape=[jax.ShapeDtypeStruct(a.shape, F32), jax.ShapeDtypeStruct(a.shape, send_dtype)],
               compiler_params=pltpu.CompilerParams(dimension_semantics=("arbitrary", "arbitrary")))(c, g4, a)


def _add_chips(p, b, *, name, order=None):
    _, h, cols = p.shape
    th = _row_tile(h, cols)
    idx = jnp.stack([2 * lax.axis_index("x") + lax.axis_index("y"), lax.axis_index("c")]).astype(jnp.int32)
    extra = [] if order is None else [order]

    def body(idx_ref, p_ref, b_ref, *rest):
        r_ref = rest[-1]
        r_ref[...] = ((p_ref[...] + b_ref[0].astype(F32)) + b_ref[1].astype(F32)) + b_ref[2].astype(F32)

    return _pc(body, name=name,
               grid_spec=pltpu.PrefetchScalarGridSpec(
                   num_scalar_prefetch=1, grid=(h // th,),
                   in_specs=[pl.BlockSpec((None, th, cols), lambda i, ir: (ir[0], i, 0)),
                             pl.BlockSpec((3, th, cols), lambda i, ir: (0, i, 0))] + [ANY] * len(extra),
                   out_specs=pl.BlockSpec((None, th, cols), lambda i, ir: (ir[1], i, 0))),
               out_shape=jax.ShapeDtypeStruct((2, h, cols), F32),
               compiler_params=pltpu.CompilerParams(dimension_semantics=("arbitrary",)))(idx, p, b, *extra)


def _add_halves_all(grads, recv, send_dtypes, tag):
    parts, sends = [], []
    for t, (g, a) in enumerate(zip(grads, recv)):
        p, s = _add_halves(g, a, send_dtypes[t], name=f"rs_add_halves_{tag}{t}")
        parts.append(p)
        sends.append(s)
    return parts, sends


def _rs_finish(parts, others, tag, order=None):
    halves = [_add_chips(p, b, order=order, name=f"rs_add_chips_{tag}{t}") for t, (p, b) in enumerate(zip(parts, others))]
    full = _rs_final(halves, name=f"rs_final_{tag}")
    return [f.reshape(-1, f.shape[-1]) for f in full]


def _s5_discretize(lam_re, lam_im, log_dt, b_re, b_im):
    lam = lax.complex(lam_re, lam_im)
    dt = jnp.exp(log_dt)[:, None]
    lam_bar = jnp.exp(lam * dt)
    b_bar = ((lam_bar - 1.0) / lam)[..., None] * lax.complex(b_re, b_im)
    return jnp.real(lam_bar), jnp.imag(lam_bar), jnp.real(b_bar), jnp.imag(b_bar)


def _lanes_from_gp(re, im, cfg):
    v = jnp.stack([re, im]).reshape(2, cfg.NB, GROUPS_PER_BLOCK, SSM_STATE)
    return jnp.transpose(v, (1, 0, 2, 3)).reshape(1, cfg.NL)


def _gp_from_lanes(v, cfg):
    v = jnp.transpose(v.reshape(cfg.NB, 2, GROUPS_PER_BLOCK, SSM_STATE), (1, 0, 2, 3)).reshape(2, cfg.G, SSM_STATE)
    return v[0], v[1]


def _bb_band(bb_re, bb_im, cfg):
    eye = jnp.eye(GROUPS_PER_BLOCK, dtype=F32)
    bb = jnp.stack([bb_re, bb_im]).reshape(2, cfg.NB, GROUPS_PER_BLOCK, SSM_STATE, SSM_GROUP)
    return jnp.einsum('rjgpc,gh->jgcrhp', bb, eye).reshape(cfg.DS, 2 * GROUPS_PER_BLOCK * SSM_STATE)


def _bb_from_band(m, cfg):
    eye = jnp.eye(GROUPS_PER_BLOCK, dtype=F32)
    m = m.reshape(cfg.NB, GROUPS_PER_BLOCK, SSM_GROUP, 2, GROUPS_PER_BLOCK, SSM_STATE)
    v = jnp.einsum('jgcrhp,gh->rjgpc', m, eye).reshape(2, cfg.G, SSM_STATE, SSM_GROUP)
    return v[0], v[1]


def _cc_band(c_re, c_im, cfg):
    eye = jnp.eye(GROUPS_PER_BLOCK, dtype=F32)
    cc = jnp.stack([c_re, -c_im]).reshape(2, cfg.NB, GROUPS_PER_BLOCK, SSM_GROUP, SSM_STATE)
    return jnp.einsum('rjgcp,gh->jrhpgc', cc, eye).reshape(cfg.NL, GROUPS_PER_BLOCK * SSM_GROUP)


def _cc_from_band(m, cfg):
    eye = jnp.eye(GROUPS_PER_BLOCK, dtype=F32)
    m = m.reshape(cfg.NB, 2, GROUPS_PER_BLOCK, SSM_STATE, GROUPS_PER_BLOCK, SSM_GROUP)
    v = jnp.einsum('jrhpgc,gh->rjgcp', m, eye).reshape(2, cfg.G, SSM_GROUP, SSM_STATE)
    return v[0], -v[1]


PACK_COLS = 512
PACK_ROW_ALIGN = 64


def _pack(arrs):
    flat = jnp.concatenate([a.reshape(-1).astype(F32) for a in arrs])
    unit = PACK_COLS * PACK_ROW_ALIGN
    total = -(-flat.shape[0] // unit) * unit
    return jnp.pad(flat, (0, total - flat.shape[0])).reshape(-1, PACK_COLS)


def _unpack(p, shapes):
    flat = p.reshape(-1)
    out, off = [], 0
    for shp in shapes:
        size = math.prod(shp)
        out.append(flat[off:off + size].reshape(shp))
        off += size
    return out


def _adamw(w, g, m, v, *, name, emit_grad=False):
    c1 = 1.0 / (1.0 - ADAM_B1 ** ADAM_STEP)
    c2 = 1.0 / (1.0 - ADAM_B2 ** ADAM_STEP)

    rows, cols = w.shape
    tc = _tile(cols, 512)
    tm = _tile(rows, max(8, 3 * DT_F32_BLOCK_BYTES // (8 * tc)), 8)
    n_out = 4 if emit_grad else 3

    def body(w_ref, g_ref, m_ref, v_ref, *o_refs):
        gv = g_ref[...]
        mn = ADAM_B1 * m_ref[...] + (1.0 - ADAM_B1) * gv
        vn = ADAM_B2 * v_ref[...] + (1.0 - ADAM_B2) * (gv * gv)
        delta = -ADAM_LR * ((mn * c1) / (jnp.sqrt(vn * c2) + ADAM_EPS) + ADAM_WD * w_ref[...])
        for o_ref, val in zip(o_refs, ((gv, delta, mn, vn) if emit_grad else (delta, mn, vn))):
            o_ref[...] = val

    blk = pl.BlockSpec((tm, tc), lambda i, j: (i, j))
    return _pc(body, name=name, grid=(rows // tm, cols // tc), in_specs=[blk] * 4, out_specs=[blk] * n_out,
               out_shape=[jax.ShapeDtypeStruct((rows, cols), F32)] * n_out,
               compiler_params=pltpu.CompilerParams(dimension_semantics=("parallel", "parallel")))(w, g, m, v)


def _to_comm_layout(name, w, cfg):
    w = w[0]
    if name == 'w_in':
        return jnp.pad(w, ((0, 0), (0, cfg.DINP - cfg.DIN)))
    if name == 'w_q_b':
        hs = w.shape[1] // (QK_NOPE + QK_ROPE)
        wt = w.T.reshape(hs, QK_NOPE + QK_ROPE, cfg.QL)
        return jnp.pad(wt, ((0, 0), (0, HEAD_SLOT - QK_NOPE - QK_ROPE), (0, 0))).reshape(hs * HEAD_SLOT, cfg.QL)
    if name == 'w_kv_b':
        return w.T
    if name == 'w_up':
        wt = w.T.reshape(2, cfg.F // 4, cfg.D)
        return jnp.pad(wt, ((0, 0), (0, cfg.FQ - cfg.F // 4), (0, 0))).reshape(2 * cfg.FQ, cfg.D)
    if name == 'w_down':
        return jnp.pad(w, ((0, cfg.FQ - cfg.F // 4), (0, 0)))
    return w


def _from_comm_layout(name, g, cfg):
    if name == 'w_in':
        g = g[:, :cfg.DIN]
    elif name == 'w_q_b':
        hs = g.shape[0] // HEAD_SLOT
        g = g.reshape(hs, HEAD_SLOT, cfg.QL)[:, :QK_NOPE + QK_ROPE].reshape(hs * (QK_NOPE + QK_ROPE), cfg.QL).T
    elif name == 'w_kv_b':
        g = g.T
    elif name == 'w_up':
        g = g.reshape(2, cfg.FQ, cfg.D)[:, :cfg.F // 4].reshape(cfg.F // 2, cfg.D).T
    elif name == 'w_down':
        g = g[:cfg.F // 4]
    return g[None]


def _ff_pad(v, cfg):
    k = v.shape[0]
    return jnp.pad(v.reshape(k, 4, cfg.F // 4), ((0, 0), (0, 0), (0, cfg.FQ - cfg.F // 4))).reshape(k, cfg.FP)


def _ff_unpad(v, cfg):
    k = v.shape[0]
    return v.reshape(k, 4, cfg.FQ)[:, :, :cfg.F // 4].reshape(k, cfg.F)


def _step(cfg, w, m, v, x, loss_target):
    lp, d, ds, nl = cfg.LP, cfg.D, cfg.DS, cfg.NL
    xi, yi = lax.axis_index("x"), lax.axis_index("y")
    me = 2 * xi + yi

    def place(n):
        return _place_shard(_to_comm_layout(n, w[n], cfg), BF16, name=f"place_{n}")

    first = [place('w_in'), _place_shard(w['meta_tokens'], F32, name="place_meta")]
    f_send, f_recv, f_flying, f_token = _split_start(first, _allgather_ici_copies, 6, name="allgather_first_start")
    conv_w_shard = jnp.pad(w['conv_w'][0], ((0, ROW_ALIGN - 3), (0, cfg.FQ - cfg.F // 4))) + f_token[0:1, 0:1]
    placed = [None] + [place(n) for n in BIG[1:]] + [None, _place_shard(conv_w_shard, F32, name="place_conv_w")]
    f_landed = _split_wait(f_send, f_recv, f_flying, _allgather_ici_copies, placed[8], name="allgather_first_wait")
    w_in, meta_full = _allgather_forward(f_landed, name="allgather_first_forward")
    meta = jnp.transpose(meta_full.reshape(4, N_META, d // 4), (1, 0, 2)).reshape(N_META, d)
    conv_b = _ff_pad(w['conv_b'], cfg)
    mid = placed[1:5] + [placed[8]]
    mid_send, mid_recv, mid_flying, mid_token = _split_start(mid, _allgather_ici_copies, 3 * len(mid), before=meta_full,
                                                             name="allgather_mid_start")
    ffn_send, ffn_recv, ffn_flying, ffn_token = _split_start(placed[5:7], _allgather_ici_copies, 6, before=mid_token,
                                                             name="allgather_ffn_start")
    mix_norm = w['mix_norm'] + (mid_token[0:1, 0:1] + ffn_token[0:1, 0:1])

    pos = (jnp.arange(lp, dtype=jnp.int32) - PAD).astype(F32)
    inv_freq = 1.0 / (ROPE_BASE ** (jnp.arange(0, QK_ROPE, 2, dtype=F32) / QK_ROPE))
    ang = pos[:, None] * inv_freq[None, :]
    zpad = jnp.zeros((lp, LANE - QK_ROPE), F32)
    cos_t = jnp.concatenate([jnp.cos(ang), jnp.cos(ang), zpad], axis=1)
    sin_t = jnp.concatenate([jnp.sin(ang), jnp.sin(ang), zpad], axis=1)

    s5_in = (w['lam_re'][0], w['lam_im'][0], w['log_dt'][0], w['b_re'][0], w['b_im'][0])
    (a_re, a_im, bb_re, bb_im), s5_vjp = jax.vjp(_s5_discretize, *s5_in)
    lam_dt = lax.complex(s5_in[0], s5_in[1]) * jnp.exp(s5_in[2])[:, None]
    a_pow = jnp.exp(jnp.arange(1, 9, dtype=F32)[:, None, None] * lam_dt[None])
    pw_fwd = jnp.concatenate([_lanes_from_gp(jnp.real(a_pow[r]), jnp.imag(a_pow[r]), cfg) for r in range(8)], axis=0)
    pw_bwd = jnp.concatenate([_lanes_from_gp(jnp.real(a_pow[7 - r]), -jnp.imag(a_pow[7 - r]), cfg) for r in range(8)], axis=0)
    bb_band = _bb_band(bb_re, bb_im, cfg).astype(BF16)
    cc_band = _cc_band(w['c_re'][0], w['c_im'][0], cfg).astype(BF16)
    d_skip, b_glu = w['d_skip'], w['b_glu']

    h0 = jnp.concatenate([jnp.zeros((PAD, d), F32), meta, x[0]], axis=0)
    xn = _rms_fwd(h0, mix_norm, name="rms_mix")
    z = _mm(xn, w_in, name="mm_in", tn=_tile(cfg.DINP, 640))
    u = (z, ds, 0)
    q_a = (z, cfg.QL, ds // cfg.QL)
    kv_a = (z, cfg.KVL, (ds + cfg.QL) // cfg.KVL)
    k_pe = (z, LANE, (ds + cfg.QL + cfg.KVL) // LANE)

    hs, yc = _s5_fwd(z, bb_band, cc_band, pw_fwd, cfg, name="s5_fwd")

    def s5_y(ycv, uv, dk):
        return ycv + dk * uv

    gl = _ew(lambda rid, ycv, uv, dk: jax.nn.gelu(s5_y(ycv, uv, dk)), [yc, u], [d_skip], [(ds, BF16)], name="s5_gelu")[0]
    mid_landed = _split_wait(mid_send, mid_recv, mid_flying, _allgather_ici_copies, gl, name="allgather_mid_wait")
    w_glu, w_qt, w_kvt, w_out, conv_full = _allgather_forward(mid_landed, name="allgather_mid_forward")
    conv_w = jnp.transpose(conv_full.reshape(4, ROW_ALIGN, cfg.FQ)[:, :3], (1, 0, 2)).reshape(3, cfg.FP)
    tg = _mm(gl, w_glu, name="mm_glu")
    ya = _ew(lambda rid, ycv, uv, tv, dk, bg: jax.nn.gelu(s5_y(ycv, uv, dk)) * jax.nn.sigmoid(tv + bg),
             [yc, u, tg], [d_skip, b_glu], [(ds, F32)], name="s5_glu")[0]

    qn = _rms_fwd(q_a, w['q_a_norm'], name="rms_q")
    kvn = _rms_fwd(kv_a, w['kv_a_norm'], name="rms_kv")
    q_raw = _mm(qn, w_qt, tb=True, name="mm_q")
    qx = _ew(_rope_heads(_rope, cfg.H), [q_raw, cos_t, sin_t], [], [(cfg.H * HEAD_SLOT, BF16)], name="rope_q")[0]
    kv = _mm(kvn, w_kvt, tb=True, out_dtype=BF16, name="mm_kv")
    kr = _ew(lambda rid, kp, cs, sn: _rope(kp, cs, sn), [k_pe, cos_t, sin_t], [], [(LANE, BF16)], name="rope_k")[0]
    o, lse = _attn_fwd(qx, kv, kr, cfg, name="attn_fwd")

    def norm2(rid, yav, ov, gs, ga):
        return jnp.concatenate([_rms_parts(yav, gs)[0] * gs, _rms_parts(ov, ga)[0] * ga], axis=1)

    yn = _ew(norm2, [ya, o], [w['out_norm_ssm'], w['out_norm_attn']], [(cfg.DMIX, BF16)], name="rms_out")[0]
    h1 = _mm(yn, w_out, res=h0, name="mm_out")
    xn2 = _rms_fwd(h1, w['ffn_norm'], name="rms_ffn")
    ffn_landed = _split_wait(ffn_send, ffn_recv, ffn_flying, _allgather_ici_copies, xn2, name="allgather_ffn_wait")
    w_upt, w_down = _allgather_forward(ffn_landed, name="allgather_ffn_forward")
    up = _mm(xn2, w_upt, tb=True, out_dtype=BF16, name="mm_up")
    act = _conv_fwd(up, conv_w, conv_b, name="conv_fwd")
    h2 = _mm(act, w_down, res=h1, tm=_tile(lp, 544, ROW_ALIGN), name="mm_down")

    g_final = w['final_norm'].reshape(1, d)

    def head(rid, hv, tv, gv):
        xhat, r = _rms_parts(hv, gv)
        valid = rid >= PAD + N_META
        diff = jnp.where(valid, xhat * gv - tv, 0.0)
        dout = diff * (1.0 / d)
        dxhat = dout * gv
        dx = r * (dxhat - xhat * jnp.mean(dxhat * xhat, axis=-1, keepdims=True))
        return dx, dx, dout * xhat, 0.5 * diff * dout

    dh2, dh2_b, dg_final, loss_cols = _ew(head, [h2, (loss_target[0], d, 0, SKIP)], [g_final], [(d, F32), (d, BF16)], [d, d],
                                          tm=PAD + N_META, name="loss_head")
    loss = lax.psum(jnp.sum(loss_cols), ("x", "y", "c"))

    dact = _mm(dh2_b, w_down, tb=True, out_dtype=BF16, name="mm_dact")
    dw_down = _mm(act, dh2_b, ta=True, tn=d, tm=512, out_dtype=BF16, name="mm_dw_down")

    def sibling_start(g, tag):
        land = lax.empty((4, g.shape[0] // 8, g.shape[1]), g.dtype)
        return _split_start([g, land], _rs_sibling_copies, 4, name=f"rs_sibling_{tag}_start")

    dn_send, dn_recv, dn_flying, dn_token = sibling_start(dw_down, "down")
    dup, dconv_w, dconv_b = _conv_bwd(up, dact, conv_w, conv_b + dn_token[0:1, 0:1], name="conv_bwd")
    tk_up, tm_up = _tile(cfg.FP, 1408), _tile(cfg.FP, 512)
    dw_upt = _mm(dup, xn2, ta=True, dims=(2 * cfg.FP, d, lp), tn=d, tm=tm_up, a_lead=True, out_dtype=BF16, name="mm_dw_up",
                 a_idx=lambda i, j, k: (i // (cfg.FP // tm_up), 0, i % (cfg.FP // tm_up)))
    up_send, up_recv, up_flying, up_token = sibling_start(dw_upt, "up")
    dxn2 = _mm(dup, w_upt, dims=(lp, d, 2 * cfg.FP), tk=tk_up, tn=1024, a_lead=True, name="mm_dxn2",
               a_idx=lambda i, j, k: (k // (cfg.FP // tk_up), i, k % (cfg.FP // tk_up)))
    dh1, dh1_b, dg_ffn = _rms_bwd(h1, w['ffn_norm'] + up_token[0:1, 0:1], dxn2, res=dh2, mask=True, with_bf16=True,
                                  name="rms_ffn_bwd")

    dyn = _mm(dh1_b, w_out, tb=True, name="mm_dyn")
    dw_out = _mm(yn, dh1_b, ta=True, tn=d, tm=512, name="mm_dw_out")
    up_done = _split_wait(up_send, up_recv, up_flying, _rs_sibling_copies, dw_out, name="rs_sibling_up_wait")
    dn_done = _split_wait(dn_send, dn_recv, dn_flying, _rs_sibling_copies, dw_out, name="rs_sibling_down_wait")
    early_parts, early_sends = _add_halves_all([up_done[0], dn_done[0]], [up_done[1], dn_done[1]], [BF16] * 2, "early")
    chip_lands = [lax.empty((3,) + s.shape[1:], s.dtype) for s in early_sends]
    ch_send, ch_recv, ch_flying, ch_token = _split_start(early_sends + chip_lands, _rs_chips_copies, 6,
                                                         name="rs_chips_early_start")
    dya, dg_ssm = _rms_bwd(ya, w['out_norm_ssm'] + ch_token[0:1, 0:1], (dyn, ds, 0), name="rms_ssm_bwd")
    do, dg_attn = _rms_bwd(o, w['out_norm_attn'], (dyn, cfg.DATTN, ds // cfg.DATTN), name="rms_attn_bwd")

    dqx, dkv, dkr = _attn_bwd(qx, kv, kr, o, lse, do, cfg, name="attn_bwd")
    dq_raw = _ew(_rope_heads(_unrope, cfg.H), [dqx, cos_t, sin_t], [], [(cfg.H * HEAD_SLOT, BF16)], name="unrope_q")[0]
    dk_pe = _ew(lambda rid, dk, cs, sn: _unrope(dk, cs, sn), [dkr, cos_t, sin_t], [], [(LANE, F32)], name="unrope_k")[0]
    dqn = _mm(dq_raw, w_qt, name="mm_dqn")
    dw_qt = _mm(dq_raw, qn, ta=True, tm=512, name="mm_dw_q")
    dkvn = _mm(dkv, w_kvt, name="mm_dkvn")
    dw_kvt = _mm(dkv, kvn, ta=True, tm=512, name="mm_dw_kv")
    dq_a, dg_q = _rms_bwd(q_a, w['q_a_norm'], dqn, name="rms_q_bwd")
    dkv_a, dg_kv = _rms_bwd(kv_a, w['kv_a_norm'], dkvn, name="rms_kv_bwd")

    def glu_bwd(rid, ycv, uv, tv, dyav, dk, bg):
        gelu = jax.nn.gelu(s5_y(ycv, uv, dk))
        sg = jax.nn.sigmoid(tv + bg)
        dt = dyav * gelu * sg * (1.0 - sg)
        return dt, dyav * sg, dt

    dt_b, dgl1, db_glu = _ew(glu_bwd, [yc, u, tg, dya], [d_skip, b_glu], [(ds, BF16), (ds, F32)], [ds], name="s5_glu_bwd")
    dgl = _mm(dt_b, w_glu, tb=True, res=dgl1, name="mm_dgl")
    dw_glu = _mm(gl, dt_b, ta=True, tm=512, name="mm_dw_glu")

    def gelu_bwd(rid, ycv, uv, dglv, dk):
        _, vjp = jax.vjp(jax.nn.gelu, s5_y(ycv, uv, dk))
        dy = vjp(dglv)[0]
        return dy, dy * dk, dy * uv

    dy_b, du_skip, dd_skip = _ew(gelu_bwd, [yc, u, dgl], [d_skip], [(ds, BF16), (ds, F32)], [ds], name="s5_gelu_bwd")
    du, dbb_band, dcc_band, da_l = _s5_bwd(dy_b, hs, z, bb_band, cc_band, pw_bwd, du_skip, cfg, name="s5_bwd")

    dz = jnp.concatenate([du, dq_a, dkv_a, dk_pe], axis=1).astype(BF16)
    dxn = _mm(dz, w_in, tb=True, name="mm_dxn")
    dw_in = _mm(xn, dz, ta=True, tm=512, tn=_tile(cfg.DINP, 1024), name="mm_dw_in")
    def mix_bwd(rid, xv, dyv, resv, gv):
        dx, dg = _rms_bwd_block(xv, gv, dyv)
        dx = dx + resv
        return dx, dx, dg

    grad_x, dh0_head, dg_mix = _ew(mix_bwd, [h0, dxn, dh1], [mix_norm], [(d, F32, SKIP), (d, F32, FIRST)], [d],
                                   tm=PAD + N_META, name="rms_mix_bwd")
    grad_x = grad_x[None]

    da_re, da_im = _gp_from_lanes(da_l, cfg)
    dbb_re, dbb_im = _bb_from_band(dbb_band, cfg)
    dlam_re, dlam_im, dlog_dt, db_re, db_im = s5_vjp((da_re, da_im, dbb_re, dbb_im))
    dc_re, dc_im = _cc_from_band(dcc_band, cfg)
    local_small = {
        'meta_tokens': dh0_head[PAD:], 'mix_norm': dg_mix, 'lam_re': dlam_re, 'lam_im': dlam_im, 'log_dt': dlog_dt,
        'b_re': db_re, 'b_im': db_im, 'c_re': dc_re, 'c_im': dc_im, 'd_skip': dd_skip, 'b_glu': db_glu, 'q_a_norm': dg_q,
        'kv_a_norm': dg_kv, 'out_norm_ssm': dg_ssm, 'out_norm_attn': dg_attn, 'ffn_norm': dg_ffn,
        'conv_w': _ff_unpad(dconv_w, cfg), 'conv_b': _ff_unpad(dconv_b, cfg), 'final_norm': dg_final,
    }
    small_shapes = [local_small[n].shape for n in SMALL]

    small_pack = _pack([local_small[n] for n in SMALL])
    ch_done = _split_wait(ch_send, ch_recv, ch_flying, _rs_chips_copies, small_pack, name="rs_chips_early_wait")
    early_halves = [_add_chips(p, b, name=f"rs_add_chips_early{t}") for t, (p, b) in enumerate(zip(early_parts, ch_done[2:]))]
    fe_send, fe_recv, fe_flying, fe_token = _split_start(early_halves, _rs_final_copies, len(early_halves),
                                                         name="rs_final_early_start")
    rest_local = [dw_in, dw_glu, dw_qt, dw_kvt, dw_out, small_pack + fe_token[0:1, 0:1]]
    rest_recv = _rs_sibling(rest_local, name="rs_sibling_rest")
    rest_parts, rest_sends = _add_halves_all(rest_local, rest_recv, [BF16] * 5 + [F32], "rest")
    rest_lands = [lax.empty((3,) + s.shape[1:], s.dtype) for s in rest_sends]
    rc_send, rc_recv, rc_flying, rc_token = _split_start(rest_sends + rest_lands, _rs_chips_copies, 3 * len(rest_sends),
                                                         name="rs_chips_rest_start")
    fe_done = _split_wait(fe_send, fe_recv, fe_flying, _rs_final_copies, rc_token, name="rs_final_early_wait")
    red_up, red_down = [f.reshape(-1, f.shape[-1]) for f in fe_done]

    delta, new_m, new_v, grads = {}, {}, {}, {}
    padded_rows = ('w_down',)

    def adamw_big(n, red):
        shp = w[n].shape
        w2, m2, v2 = [t.reshape(shp[-2], shp[-1]) for t in (w[n], m[n], v[n])]
        if n in padded_rows:
            g2, dl, mn, vn = _adamw(w2, red, m2, v2, emit_grad=True, name=f"adamw_{n}")
            grads[n] = g2.reshape(shp)
        else:
            grads[n] = _from_comm_layout(n, red, cfg)
            dl, mn, vn = _adamw(w2, grads[n].reshape(shp[-2], shp[-1]), m2, v2, name=f"adamw_{n}")
        delta[n], new_m[n], new_v[n] = dl.reshape(shp), mn.reshape(shp), vn.reshape(shp)

    adamw_big('w_up', red_up)
    adamw_big('w_down', red_down)
    rc_done = _split_wait(rc_send, rc_recv, rc_flying, _rs_chips_copies, delta['w_down'], name="rs_chips_rest_wait")
    red = _rs_finish(rest_parts, rc_done[len(rest_sends):], "rest")
    small_full = _allgather([_place_shard(red[5], F32, name="place_small")], name="allgather_small")[0]
    small_sum = dict(zip(SMALL, _unpack(small_full, small_shapes)))
    for n, r in zip(['w_in', 'w_glu', 'w_q_b', 'w_kv_b', 'w_out'], red[:5]):
        adamw_big(n, r)

    for n in SMALL:
        g = small_sum[n]
        if n == 'meta_tokens':
            g = lax.dynamic_slice_in_dim(g, me * (d // 4), d // 4, axis=1)
        elif n == 'conv_w':
            g = lax.dynamic_slice_in_dim(g, me * (cfg.F // 4), cfg.F // 4, axis=1)[None]
        else:
            g = g.reshape(w[n].shape)
        grads[n] = g

    shapes = [w[n].shape for n in SMALL]
    packs = [_pack([src[n] for n in SMALL]) for src in (w, grads, m, v)]
    for dst, p in zip((delta, new_m, new_v), _adamw(*packs, name="adamw_small")):
        dst.update(zip(SMALL, _unpack(p, shapes)))

    return (loss, grad_x, *[grads[n] for n in WEIGHTS], *[delta[n] for n in WEIGHTS],
            *[new_m[n] for n in WEIGHTS], *[new_v[n] for n in WEIGHTS])


def kernel(x, meta_tokens, mix_norm, w_in, lam_re, lam_im, log_dt, b_re, b_im, c_re, c_im, d_skip, w_glu, b_glu, q_a_norm, w_q_b, kv_a_norm, w_kv_b, out_norm_ssm, out_norm_attn, w_out, ffn_norm, w_up, conv_w, conv_b, w_down, final_norm, loss_target, m_meta_tokens, m_mix_norm, m_w_in, m_lam_re, m_lam_im, m_log_dt, m_b_re, m_b_im, m_c_re, m_c_im, m_d_skip, m_w_glu, m_b_glu, m_q_a_norm, m_w_q_b, m_kv_a_norm, m_w_kv_b, m_out_norm_ssm, m_out_norm_attn, m_w_out, m_ffn_norm, m_w_up, m_conv_w, m_conv_b, m_w_down, m_final_norm, v_meta_tokens, v_mix_norm, v_w_in, v_lam_re, v_lam_im, v_log_dt, v_b_re, v_b_im, v_c_re, v_c_im, v_d_skip, v_w_glu, v_b_glu, v_q_a_norm, v_w_q_b, v_kv_a_norm, v_w_kv_b, v_out_norm_ssm, v_out_norm_attn, v_w_out, v_ffn_norm, v_w_up, v_conv_w, v_conv_b, v_w_down, v_final_norm):
    args = dict(locals())
    w = {n: args[n] for n in WEIGHTS}
    m = {n: args["m_" + n] for n in WEIGHTS}
    v = {n: args["v_" + n] for n in WEIGHTS}
    return _step(PROD, w, m, v, x, loss_target)
```

```python
import functools
import math
from typing import NamedTuple

import jax
import jax.numpy as jnp
from jax import lax
from jax.experimental import pallas as pl
from jax.experimental.pallas import tpu as pltpu

F32, BF16 = jnp.float32, jnp.bfloat16
MESH = pl.DeviceIdType.MESH
LANE = 128
ROW_ALIGN = 16
N_META = 16
PAD = 112
CHUNK = 64
SSM_GROUP = 16
SSM_STATE = 64
GROUPS_PER_BLOCK = 8
QK_NOPE, QK_ROPE, V_HEAD = 128, 64, 128
HEAD_SLOT = 256
ROPE_BASE = 10000.0
EPS = 1e-6
ADAM_LR, ADAM_B1, ADAM_B2, ADAM_EPS, ADAM_WD, ADAM_STEP = 0.001, 0.9, 0.999, 1e-08, 0.01, 10
DT_F32_BLOCK_BYTES = 1 << 20
SKIP, FIRST = "skip", "first"


class Cfg(NamedTuple):
    D: int
    S: int
    DS: int
    H: int
    QL: int
    KVL: int
    F: int

    @property
    def LP(self):
        return PAD + N_META + self.S

    @property
    def G(self):
        return self.DS // SSM_GROUP

    @property
    def NB(self):
        return self.G // GROUPS_PER_BLOCK

    @property
    def NL(self):
        return 2 * self.G * SSM_STATE

    @property
    def DATTN(self):
        return self.H * V_HEAD

    @property
    def DMIX(self):
        return self.DS + self.DATTN

    @property
    def DIN(self):
        return self.DS + self.QL + self.KVL + QK_ROPE

    @property
    def DINP(self):
        return self.DS + self.QL + self.KVL + LANE

    @property
    def FQ(self):
        return -(-(self.F // 4) // LANE) * LANE

    @property
    def FP(self):
        return 4 * self.FQ


PROD = Cfg(D=2048, S=2048, DS=1024, H=8, QL=512, KVL=256, F=5504)

WEIGHTS = ['meta_tokens', 'mix_norm', 'w_in', 'lam_re', 'lam_im', 'log_dt', 'b_re', 'b_im', 'c_re', 'c_im', 'd_skip',
           'w_glu', 'b_glu', 'q_a_norm', 'w_q_b', 'kv_a_norm', 'w_kv_b', 'out_norm_ssm', 'out_norm_attn', 'w_out',
           'ffn_norm', 'w_up', 'conv_w', 'conv_b', 'w_down', 'final_norm']
BIG = ['w_in', 'w_glu', 'w_q_b', 'w_kv_b', 'w_out', 'w_up', 'w_down']
SMALL = [n for n in WEIGHTS if n not in BIG]


def _pc(body, **kw):
    return pl.pallas_call(body, **kw)


def _tile(n, target, align=LANE):
    best = None
    d = align
    while d <= min(n, target):
        if n % d == 0:
            best = d
        d += align
    return best if best is not None else n


def _row_tile(rows, cols):
    return _tile(rows, max(ROW_ALIGN, DT_F32_BLOCK_BYTES // (4 * cols)), ROW_ALIGN)


def _mm(a, b, *, name, ta=False, tb=False, tm=None, tn=512, tk=None, out_dtype=F32, res=None,
        a_idx=None, b_idx=None, dims=None, a_lead=False):
    if dims is None:
        m, k = (a.shape[1], a.shape[0]) if ta else a.shape
        n = b.shape[0] if tb else b.shape[1]
    else:
        m, n, k = dims
    tm = _tile(m, tm or m, LANE if ta else ROW_ALIGN)
    tn = _tile(n, tn)
    tk = _tile(k, tk or k, ROW_ALIGN if (ta and not tb) else LANE)
    nm, nn, nk = m // tm, n // tn, k // tk
    a_idx = a_idx or ((lambda i, j, kk: (kk, i)) if ta else (lambda i, j, kk: (i, kk)))
    b_idx = b_idx or ((lambda i, j, kk: (j, kk)) if tb else (lambda i, j, kk: (kk, j)))
    dn = (((0 if ta else 1,), (1 if tb else 0,)), ((), ()))

    def body(*refs):
        a_ref, b_ref = refs[0], refs[1]
        r_ref = refs[2] if res is not None else None
        o_ref = refs[3] if res is not None else refs[2]
        d = lax.dot_general(a_ref[...].astype(BF16), b_ref[...].astype(BF16), dn, preferred_element_type=F32)

        def finish(r):
            if r_ref is not None:
                r = r + r_ref[...].astype(F32)
            o_ref[...] = r.astype(out_dtype)

        if nk == 1:
            finish(d)
        else:
            acc = refs[-1]
            kk = pl.program_id(2)

            @pl.when(kk == 0)
            def _():
                acc[...] = d

            @pl.when(kk > 0)
            def _():
                acc[...] += d

            @pl.when(kk == nk - 1)
            def _():
                finish(acc[...])

    a_blk = ((None,) if a_lead else ()) + ((tk, tm) if ta else (tm, tk))
    in_specs = [pl.BlockSpec(a_blk, a_idx), pl.BlockSpec((tn, tk) if tb else (tk, tn), b_idx)]
    args = [a, b]
    if res is not None:
        in_specs.append(pl.BlockSpec((tm, tn), lambda i, j, kk: (i, j)))
        args.append(res)
    return _pc(body, name=name, grid=(nm, nn, nk), in_specs=in_specs,
               out_specs=pl.BlockSpec((tm, tn), lambda i, j, kk: (i, j)),
               out_shape=jax.ShapeDtypeStruct((m, n), out_dtype),
               scratch_shapes=[pltpu.VMEM((tm, tn), F32)] if nk > 1 else [],
               compiler_params=pltpu.CompilerParams(dimension_semantics=("parallel", "parallel", "arbitrary")))(*args)


def _ew(fn, ins, vecs, outs, sums=(), *, name, tm=None):
    ins = [x if isinstance(x, tuple) else (x, x.shape[1], 0) for x in ins]
    ins = [x if len(x) == 4 else x + (None,) for x in ins]
    outs = [o if len(o) == 3 else o + (None,) for o in outs]
    rows = ins[0][0].shape[0]
    cmax = max([c for _, c, _, _ in ins] + [c for c, _, _ in outs])
    tm = tm or _row_tile(rows, cmax)
    n_in, n_vec, n_out, n_sum = len(ins), len(vecs), len(outs), len(sums)

    def body(*refs):
        i = pl.program_id(0)
        rid = i * tm + lax.broadcasted_iota(jnp.int32, (tm, 1), 0)
        vals = [r[...] for r in refs[:n_in + n_vec]]
        res = fn(rid, *vals)
        res = res if isinstance(res, (tuple, list)) else (res,)
        o_refs = refs[n_in + n_vec:]
        for o_ref, r, (_, _, mode) in zip(o_refs[:n_out], res[:n_out], outs):
            if mode == FIRST:
                @pl.when(i == 0)
                def _():
                    o_ref[...] = r.astype(o_ref.dtype)
            else:
                o_ref[...] = r.astype(o_ref.dtype)
        for o_ref, r in zip(o_refs[n_out:], res[n_out:]):
            part = jnp.sum(r.astype(F32), axis=0, keepdims=True)

            @pl.when(i == 0)
            def _():
                o_ref[...] = part

            @pl.when(i > 0)
            def _():
                o_ref[...] += part

    def row_idx(mode):
        if mode == SKIP:
            return lambda i, cb=0: (jnp.maximum(i - 1, 0), cb)
        if mode == FIRST:
            return lambda i, cb=0: (0, cb)
        return lambda i, cb=0: (i, cb)

    in_specs = [pl.BlockSpec((tm, c), functools.partial(row_idx(mode), cb=cb)) for _, c, cb, mode in ins]
    in_specs += [pl.BlockSpec(v.shape, functools.partial(lambda i, nd: (0,) * nd, nd=v.ndim)) for v in vecs]
    out_specs = [pl.BlockSpec((tm, c), row_idx(mode)) for c, _, mode in outs]
    out_specs += [pl.BlockSpec((1, c), lambda i: (0, 0)) for c in sums]
    out_rows = {None: rows, SKIP: rows - tm, FIRST: tm}
    out_shape = [jax.ShapeDtypeStruct((out_rows[mode], c), dt) for c, dt, mode in outs]
    out_shape += [jax.ShapeDtypeStruct((1, c), F32) for c in sums]
    return _pc(body, name=name, grid=(rows // tm,), in_specs=in_specs, out_specs=out_specs, out_shape=out_shape,
               compiler_params=pltpu.CompilerParams(dimension_semantics=("arbitrary",)))(*[x[0] for x in ins], *vecs)


def _rms_parts(x, g):
    r = lax.rsqrt(jnp.mean(x * x, axis=-1, keepdims=True) + EPS)
    return x * r, r


def _rms_bwd_block(x, g, dy):
    xhat, r = _rms_parts(x, g)
    dxhat = dy * g
    dx = r * (dxhat - xhat * jnp.mean(dxhat * xhat, axis=-1, keepdims=True))
    return dx, dy * xhat


def _rms_fwd(x, g, *, name):
    c = x[1] if isinstance(x, tuple) else x.shape[1]
    return _ew(lambda rid, xv, gv: _rms_parts(xv.astype(F32), gv)[0] * gv, [x], [g], [(c, BF16)], name=name)[0]


def _rms_bwd(x, g, dy, *, name, res=None, mask=False, with_bf16=False):
    c = x[1] if isinstance(x, tuple) else x.shape[1]

    def fn(rid, xv, dyv, *rest):
        gv = rest[-1]
        dx, dg = _rms_bwd_block(xv.astype(F32), gv, dyv.astype(F32))
        if res is not None:
            dx = dx + rest[0]
        if mask:
            dx = jnp.where(rid >= PAD, dx, 0.0)
        return (dx, dx, dg) if with_bf16 else (dx, dg)

    ins = [x, dy] + ([res] if res is not None else [])
    outs = [(c, F32)] + ([(c, BF16)] if with_bf16 else [])
    return _ew(fn, ins, [g], outs, [c], name=name)


S5_W = GROUPS_PER_BLOCK * SSM_STATE
S5_GW = GROUPS_PER_BLOCK * SSM_GROUP
S5_UNROLL = 8
S5_DA_ROWS = 272


def _s5_scan_in_place(ref, pw_ref, *, reverse):
    lp = ref.shape[0]
    tile_rows = 8
    chunk = _tile(lp, S5_DA_ROWS, tile_rows)
    sub = lax.broadcasted_iota(jnp.int32, (chunk, 1), 0) % tile_rows

    def chunk_body(c, carry):
        rows = pl.ds(pl.multiple_of(c * chunk, tile_rows), chunk)
        xr, xi = ref[rows, :S5_W], ref[rows, S5_W:]
        for k in (1, 2, 4):
            row = tile_rows - k if reverse else k - 1
            mr, mi = pw_ref[row:row + 1, :S5_W], pw_ref[row:row + 1, S5_W:]
            shift = chunk - k if reverse else k
            sr, si = pltpu.roll(xr, shift, 0), pltpu.roll(xi, shift, 0)
            keep = (sub < tile_rows - k) if reverse else (sub >= k)
            xr, xi = xr + jnp.where(keep, mr * sr - mi * si, 0.0), xi + jnp.where(keep, mr * si + mi * sr, 0.0)
        ref[rows, :S5_W] = xr
        ref[rows, S5_W:] = xi
        return carry

    lax.fori_loop(0, lp // chunk, chunk_body, 0)

    pr, pi = pw_ref[:, :S5_W], pw_ref[:, S5_W:]
    ntile = lp // tile_rows
    unroll = 4

    def step(n, carry):
        cr, ci = carry
        for q in range(unroll):
            j = n * unroll + q
            j = ntile - 1 - j if reverse else j
            rows = pl.ds(pl.multiple_of(j * tile_rows, tile_rows), tile_rows)
            nr = ref[rows, :S5_W] + (pr * cr - pi * ci)
            ni = ref[rows, S5_W:] + (pr * ci + pi * cr)
            ref[rows, :S5_W] = nr
            ref[rows, S5_W:] = ni
            cr, ci = (nr[0:1], ni[0:1]) if reverse else (nr[tile_rows - 1:], ni[tile_rows - 1:])
        return cr, ci

    z = jnp.zeros((1, S5_W), F32)
    lax.fori_loop(0, ntile // unroll, step, (z, z))


def _s5_fwd(z, bb_band, cc_band, a_l, cfg, *, name):
    lp, ds, nl = cfg.LP, cfg.DS, cfg.NL

    def body(u_ref, bb_ref, cc_ref, a_ref, hs_ref, y_ref):
        hs_ref[...] = jnp.dot(u_ref[...].astype(BF16), bb_ref[...], preferred_element_type=F32)
        _s5_scan_in_place(hs_ref, a_ref, reverse=False)
        y_ref[...] = jnp.dot(hs_ref[...].astype(BF16), cc_ref[...], preferred_element_type=F32)

    return _pc(body, name=name, grid=(cfg.NB,),
               in_specs=[pl.BlockSpec((lp, S5_GW), lambda j: (0, j)), pl.BlockSpec((S5_GW, 2 * S5_W), lambda j: (j, 0)),
                         pl.BlockSpec((2 * S5_W, S5_GW), lambda j: (j, 0)), pl.BlockSpec((8, 2 * S5_W), lambda j: (0, j))],
               out_specs=[pl.BlockSpec((lp, 2 * S5_W), lambda j: (0, j)), pl.BlockSpec((lp, S5_GW), lambda j: (0, j))],
               out_shape=[jax.ShapeDtypeStruct((lp, nl), F32), jax.ShapeDtypeStruct((lp, ds), F32)],
               compiler_params=pltpu.CompilerParams(dimension_semantics=("parallel",)))(z, bb_band, cc_band, a_l)


def _s5_bwd(dy, hs, z, bb_band, cc_band, a_l, du_skip, cfg, *, name):
    lp, ds, nl = cfg.LP, cfg.DS, cfg.NL
    nt = (((1,), (1,)), ((), ()))
    tn = (((0,), (0,)), ((), ()))

    def body(dy_ref, hs_ref, u_ref, bb_ref, cc_ref, a_ref, sk_ref, du_ref, dbb_ref, dcc_ref, da_ref, g_ref):
        dyv = dy_ref[...]
        g_ref[...] = lax.dot_general(dyv, cc_ref[...], nt, preferred_element_type=F32)
        _s5_scan_in_place(g_ref, a_ref, reverse=True)
        dcc_ref[...] = lax.dot_general(hs_ref[...].astype(BF16), dyv, tn, preferred_element_type=F32)
        gb = g_ref[...].astype(BF16)
        dbb_ref[...] = lax.dot_general(u_ref[...].astype(BF16), gb, tn, preferred_element_type=F32)
        du_ref[...] = lax.dot_general(gb, bb_ref[...], nt, preferred_element_type=F32) + sk_ref[...]
        dre = jnp.zeros((1, S5_W), F32)
        dim = jnp.zeros((1, S5_W), F32)
        for r0 in range(0, lp, S5_DA_ROWS):
            rows = min(S5_DA_ROWS, lp - r0)
            first = lax.broadcasted_iota(jnp.int32, (rows, 1), 0) == 0
            prev = hs_ref[r0 - 1:r0, :] if r0 else jnp.zeros((1, 2 * S5_W), F32)
            hr = jnp.where(first, prev[:, :S5_W], pltpu.roll(hs_ref[r0:r0 + rows, :S5_W], 1, 0))
            hi = jnp.where(first, prev[:, S5_W:], pltpu.roll(hs_ref[r0:r0 + rows, S5_W:], 1, 0))
            gr, gi = g_ref[r0:r0 + rows, :S5_W], g_ref[r0:r0 + rows, S5_W:]
            dre = dre + jnp.sum(gr * hr + gi * hi, axis=0, keepdims=True)
            dim = dim + jnp.sum(gi * hr - gr * hi, axis=0, keepdims=True)
        da_ref[:, :S5_W] = dre
        da_ref[:, S5_W:] = dim

    col_blk = pl.BlockSpec((lp, S5_GW), lambda j: (0, j))
    lane_blk = pl.BlockSpec((lp, 2 * S5_W), lambda j: (0, j))
    bb_blk = pl.BlockSpec((S5_GW, 2 * S5_W), lambda j: (j, 0))
    cc_blk = pl.BlockSpec((2 * S5_W, S5_GW), lambda j: (j, 0))
    a_blk = pl.BlockSpec((1, 2 * S5_W), lambda j: (0, j))
    pw_blk = pl.BlockSpec((8, 2 * S5_W), lambda j: (0, j))
    return _pc(body, name=name, grid=(cfg.NB,),
               in_specs=[col_blk, lane_blk, col_blk, bb_blk, cc_blk, pw_blk, col_blk],
               out_specs=[col_blk, bb_blk, cc_blk, a_blk],
               out_shape=[jax.ShapeDtypeStruct((lp, ds), F32), jax.ShapeDtypeStruct((ds, 2 * S5_W), F32),
                          jax.ShapeDtypeStruct((nl, S5_GW), F32), jax.ShapeDtypeStruct((1, nl), F32)],
               scratch_shapes=[pltpu.VMEM((lp, 2 * S5_W), F32)],
               compiler_params=pltpu.CompilerParams(dimension_semantics=("parallel",)))(dy, hs, z, bb_band, cc_band, a_l, du_skip)


def _conv_gate(pre, cw, cb):
    return cw[0:1] * pltpu.roll(pre, 2, 0) + cw[1:2] * pltpu.roll(pre, 1, 0) + cw[2:3] * pre + cb


def _conv_fwd(up, cw, cb, *, name):
    lp, fp2 = up.shape
    fp = fp2 // 2
    tc = _tile(fp, 256)
    nb = fp // tc

    def body(pre_ref, val_ref, cw_ref, cb_ref, o_ref):
        gate = _conv_gate(pre_ref[...].astype(F32), cw_ref[...], cb_ref[...])
        o_ref[...] = (jax.nn.silu(gate) * val_ref[...].astype(F32)).astype(BF16)

    return _pc(body, name=name, grid=(nb,),
               in_specs=[pl.BlockSpec((lp, tc), lambda j: (0, j)), pl.BlockSpec((lp, tc), lambda j: (0, nb + j)),
                         pl.BlockSpec((3, tc), lambda j: (0, j)), pl.BlockSpec((1, tc), lambda j: (0, j))],
               out_specs=pl.BlockSpec((lp, tc), lambda j: (0, j)),
               out_shape=jax.ShapeDtypeStruct((lp, fp), BF16),
               compiler_params=pltpu.CompilerParams(dimension_semantics=("parallel",)))(up, up, cw, cb)


def _conv_bwd(up, dact, cw, cb, *, name):
    lp, fp2 = up.shape
    fp = fp2 // 2
    tc = _tile(fp, 256)
    nb = fp // tc

    def body(pre_ref, val_ref, da_ref, cw_ref, cb_ref, dup_ref, dcw_ref, dcb_ref):
        pre, val, da, cwv = pre_ref[...].astype(F32), val_ref[...].astype(F32), da_ref[...].astype(F32), cw_ref[...]
        gate = _conv_gate(pre, cwv, cb_ref[...])
        sg = jax.nn.sigmoid(gate)
        dup_ref[1] = (da * (gate * sg)).astype(BF16)
        dgate = da * val * (sg * (1.0 + gate * (1.0 - sg)))
        dpre = cwv[2:3] * dgate + cwv[1:2] * pltpu.roll(dgate, lp - 1, 0) + cwv[0:1] * pltpu.roll(dgate, lp - 2, 0)
        dup_ref[0] = dpre.astype(BF16)
        dcb_ref[...] = jnp.sum(dgate, axis=0, keepdims=True)
        dcw_ref[0:1, :] = jnp.sum(dgate * pltpu.roll(pre, 2, 0), axis=0, keepdims=True)
        dcw_ref[1:2, :] = jnp.sum(dgate * pltpu.roll(pre, 1, 0), axis=0, keepdims=True)
        dcw_ref[2:3, :] = jnp.sum(dgate * pre, axis=0, keepdims=True)

    return _pc(body, name=name, grid=(nb,),
               in_specs=[pl.BlockSpec((lp, tc), lambda j: (0, j)), pl.BlockSpec((lp, tc), lambda j: (0, nb + j)),
                         pl.BlockSpec((lp, tc), lambda j: (0, j)),
                         pl.BlockSpec((3, tc), lambda j: (0, j)), pl.BlockSpec((1, tc), lambda j: (0, j))],
               out_specs=[pl.BlockSpec((2, lp, tc), lambda j: (0, 0, j)),
                          pl.BlockSpec((3, tc), lambda j: (0, j)), pl.BlockSpec((1, tc), lambda j: (0, j))],
               out_shape=[jax.ShapeDtypeStruct((2, lp, fp), BF16), jax.ShapeDtypeStruct((3, fp), F32),
                          jax.ShapeDtypeStruct((1, fp), F32)],
               compiler_params=pltpu.CompilerParams(dimension_semantics=("parallel",)))(up, up, dact, cw, cb)


def _key_limit(i, tq, lp):
    return min(lp, -(-((i + 1) * tq) // LANE) * LANE)


def _attn_mask(i, tq, nk):
    qrow = i * tq + lax.broadcasted_iota(jnp.int32, (tq, 1), 0)
    krow = lax.broadcasted_iota(jnp.int32, (1, nk), 1)
    return (krow >= PAD) & ((krow // CHUNK) <= (qrow // CHUNK)), qrow >= PAD


def _attn_scores(q, kn, kr, i, tq, scale):
    nt = (((1,), (1,)), ((), ()))
    s = lax.dot_general(q[:, :QK_NOPE], kn, nt, preferred_element_type=F32)
    s = s + lax.dot_general(q[:, QK_NOPE:], kr, nt, preferred_element_type=F32)
    mask, qvalid = _attn_mask(i, tq, kn.shape[0])
    return jnp.where(mask, s * scale, jnp.finfo(F32).min), qvalid


def _per_q_block(nq, fn):
    i = pl.program_id(1)
    for blk in range(nq):
        pl.when(i == blk)(functools.partial(fn, blk))


def _attn_fwd(qx, kv, kr, cfg, *, name):
    lp, h = cfg.LP, cfg.H
    tq = _tile(lp, 272, ROW_ALIGN)
    nq = lp // tq
    scale = 1.0 / math.sqrt(QK_NOPE + QK_ROPE)

    def body(q_ref, kn_ref, v_ref, kr_ref, o_ref, lse_ref):
        def block(blk):
            nk = _key_limit(blk, tq, lp)
            s, qvalid = _attn_scores(q_ref[...], kn_ref[:nk], kr_ref[:nk], blk, tq, scale)
            m = jnp.max(s, axis=-1, keepdims=True)
            p = jnp.exp(s - m)
            l = jnp.sum(p, axis=-1, keepdims=True)
            o = jnp.dot(p.astype(BF16), v_ref[:nk], preferred_element_type=F32) / l
            o_ref[...] = jnp.where(qvalid, o, 0.0)
            lse_ref[...] = m + jnp.log(l)

        _per_q_block(nq, block)

    return _pc(body, name=name, grid=(h, nq),
               in_specs=[pl.BlockSpec((tq, HEAD_SLOT), lambda hh, i: (i, hh)),
                         pl.BlockSpec((lp, QK_NOPE), lambda hh, i: (0, 2 * hh)),
                         pl.BlockSpec((lp, V_HEAD), lambda hh, i: (0, 2 * hh + 1)),
                         pl.BlockSpec((lp, LANE), lambda hh, i: (0, 0))],
               out_specs=[pl.BlockSpec((tq, V_HEAD), lambda hh, i: (i, hh)),
                          pl.BlockSpec((None, tq, 1), lambda hh, i: (hh, i, 0))],
               out_shape=[jax.ShapeDtypeStruct((lp, h * V_HEAD), F32), jax.ShapeDtypeStruct((h, lp, 1), F32)],
               compiler_params=pltpu.CompilerParams(dimension_semantics=("parallel", "parallel")))(qx, kv, kv, kr)


def _attn_bwd(qx, kv, kr, o, lse, do, cfg, *, name):
    lp, h = cfg.LP, cfg.H
    tq = _tile(lp, 272, ROW_ALIGN)
    nq = lp // tq
    scale = 1.0 / math.sqrt(QK_NOPE + QK_ROPE)
    tn_dims = (((0,), (0,)), ((), ()))

    def body(q_ref, kn_ref, v_ref, kr_ref, o_ref, lse_ref, do_ref, dq_ref, dkv_ref, dkr_ref, dkv_acc):
        hh, i = pl.program_id(0), pl.program_id(1)

        @pl.when(i == 0)
        def _():
            dkv_acc[...] = jnp.zeros_like(dkv_acc)

        @pl.when((i == 0) & (hh == 0))
        def _():
            dkr_ref[...] = jnp.zeros_like(dkr_ref)

        def block(blk):
            nk = _key_limit(blk, tq, lp)
            q, kn, v, krv = q_ref[...], kn_ref[:nk], v_ref[:nk], kr_ref[:nk]
            s, qvalid = _attn_scores(q, kn, krv, blk, tq, scale)
            dov = jnp.where(qvalid, do_ref[...], 0.0)
            p = jnp.exp(s - lse_ref[...])
            delta = jnp.sum(dov * o_ref[...], axis=-1, keepdims=True)
            dob = dov.astype(BF16)
            dp = lax.dot_general(dob, v, (((1,), (1,)), ((), ())), preferred_element_type=F32)
            ds = (p * (dp - delta) * scale).astype(BF16)
            dq_ref[:, :QK_NOPE] = jnp.dot(ds, kn, preferred_element_type=F32)
            dq_ref[:, QK_NOPE:] = jnp.dot(ds, krv, preferred_element_type=F32)
            dkv_acc[:nk, :QK_NOPE] += lax.dot_general(ds, q[:, :QK_NOPE], tn_dims, preferred_element_type=F32)
            dkv_acc[:nk, QK_NOPE:] += lax.dot_general(p.astype(BF16), dob, tn_dims, preferred_element_type=F32)
            dkr_ref[:nk, :] += lax.dot_general(ds, q[:, QK_NOPE:], tn_dims, preferred_element_type=F32)

        _per_q_block(nq, block)

        @pl.when(i == nq - 1)
        def _():
            dkv_ref[...] = dkv_acc[...].astype(BF16)

    return _pc(body, name=name, grid=(h, nq),
               in_specs=[pl.BlockSpec((tq, HEAD_SLOT), lambda hh, i: (i, hh)),
                         pl.BlockSpec((lp, QK_NOPE), lambda hh, i: (0, 2 * hh)),
                         pl.BlockSpec((lp, V_HEAD), lambda hh, i: (0, 2 * hh + 1)),
                         pl.BlockSpec((lp, LANE), lambda hh, i: (0, 0)),
                         pl.BlockSpec((tq, V_HEAD), lambda hh, i: (i, hh)),
                         pl.BlockSpec((None, tq, 1), lambda hh, i: (hh, i, 0)),
                         pl.BlockSpec((tq, V_HEAD), lambda hh, i: (i, hh))],
               out_specs=[pl.BlockSpec((tq, HEAD_SLOT), lambda hh, i: (i, hh)),
                          pl.BlockSpec((lp, QK_NOPE + V_HEAD), lambda hh, i: (0, hh)),
                          pl.BlockSpec((lp, LANE), lambda hh, i: (0, 0))],
               out_shape=[jax.ShapeDtypeStruct((lp, h * HEAD_SLOT), F32),
                          jax.ShapeDtypeStruct((lp, h * (QK_NOPE + V_HEAD)), BF16),
                          jax.ShapeDtypeStruct((lp, LANE), F32)],
               scratch_shapes=[pltpu.VMEM((lp, QK_NOPE + V_HEAD), F32)],
               compiler_params=pltpu.CompilerParams(dimension_semantics=("arbitrary", "arbitrary")))(qx, kv, kv, kr, o, lse, do)


def _rot_half(x):
    lane = lax.broadcasted_iota(jnp.int32, x.shape, 1)
    half = QK_ROPE // 2
    return jnp.where(lane < half, -pltpu.roll(x, LANE - half, 1), pltpu.roll(x, half, 1))


def _rope(x, cos, sin):
    return x * cos + _rot_half(x) * sin


def _unrope(dy, cos, sin):
    return dy * cos - _rot_half(dy * sin)


def _rope_heads(fn, h):
    def apply(rid, q, cos, sin):
        parts = []
        for hh in range(h):
            parts.append(q[:, hh * HEAD_SLOT: hh * HEAD_SLOT + QK_NOPE])
            parts.append(fn(q[:, hh * HEAD_SLOT + QK_NOPE: (hh + 1) * HEAD_SLOT], cos, sin))
        return jnp.concatenate(parts, axis=1)
    return apply


ANY = pl.BlockSpec(memory_space=pl.ANY)


def _place():
    x, y, c = lax.axis_index("x"), lax.axis_index("y"), lax.axis_index("c")
    chips = [(1 - x, y), (x, 1 - y), (1 - x, 1 - y)]
    return x, y, c, chips


def _rcopy(src, dst, send_sem, recv_sem, dev):
    return pltpu.make_async_remote_copy(src_ref=src, dst_ref=dst, send_sem=send_sem, recv_sem=recv_sem,
                                        device_id=dev, device_id_type=MESH)


def _place_shard(shard, dtype, *, name):
    r, cols = shard.shape
    tm = _row_tile(r, cols)
    nblk = r // tm
    me = (2 * lax.axis_index("x") + lax.axis_index("y")).astype(jnp.int32).reshape(1)

    def body(me_ref, s_ref, o_ref):
        o_ref[...] = s_ref[...].astype(dtype)

    return _pc(body, name=name,
               grid_spec=pltpu.PrefetchScalarGridSpec(
                   num_scalar_prefetch=1, grid=(nblk,),
                   in_specs=[pl.BlockSpec((tm, cols), lambda i, mr: (i, 0))],
                   out_specs=pl.BlockSpec((tm, cols), lambda i, mr: (mr[0] * nblk + i, 0))),
               out_shape=jax.ShapeDtypeStruct((4 * r, cols), dtype),
               compiler_params=pltpu.CompilerParams(dimension_semantics=("arbitrary",)))(me, shard)


def _allgather(fulls, *, name):
    n = len(fulls)

    def body(*refs):
        outs = refs[n:2 * n]
        send_sems, recv_sems = refs[2 * n:]
        x, y, c, chips = _place()
        sib = (x, y, 1 - c)
        me = 2 * x + y

        def rows(t, s, half):
            hrows = outs[t].shape[0] // 8
            return outs[t].at[pl.ds((2 * s + half) * hrows, hrows)]

        sent = []
        for t in range(n):
            for j, (cx, cy) in enumerate(chips):
                cp = _rcopy(rows(t, me, c), rows(t, me, c), send_sems.at[6 * t + j], recv_sems.at[6 * t + j], (cx, cy, c))
                cp.start()
                sent.append(cp)
        for t in range(n):
            for j, (cx, cy) in enumerate(chips):
                landed = rows(t, 2 * cx + cy, c)
                _rcopy(landed, landed, send_sems.at[6 * t + j], recv_sems.at[6 * t + j], (cx, cy, c)).wait_recv()
                cp = _rcopy(landed, landed, send_sems.at[6 * t + 3 + j], recv_sems.at[6 * t + 3 + j], sib)
                cp.start()
                sent.append(cp)
        for t in range(n):
            for j, (cx, cy) in enumerate(chips):
                other = rows(t, 2 * cx + cy, 1 - c)
                _rcopy(other, other, send_sems.at[6 * t + 3 + j], recv_sems.at[6 * t + 3 + j], sib).wait_recv()
        for cp in sent:
            cp.wait_send()

    return _pc(body, name=name, in_specs=[ANY] * n, out_specs=[ANY] * n,
               out_shape=[jax.ShapeDtypeStruct(f.shape, f.dtype) for f in fulls],
               input_output_aliases={t: t for t in range(n)},
               scratch_shapes=[pltpu.SemaphoreType.DMA((6 * n,)), pltpu.SemaphoreType.DMA((6 * n,))])(*fulls)


HBM = pl.BlockSpec(memory_space=pltpu.HBM)
SEM = pl.BlockSpec(memory_space=pltpu.SEMAPHORE)
EFFECT = pltpu.SideEffectType.DATAFLOW_SIDE_EFFECTING
TOKEN = jax.ShapeDtypeStruct((8, LANE), F32)


def _in_hbm(a):
    return pltpu.with_memory_space_constraint(a, pltpu.HBM)


def _half_rows(ref, s, half):
    hrows = ref.shape[0] // 8
    return ref.at[pl.ds((2 * s + half) * hrows, hrows)]


def _split_start(bufs, copies, n_copies, *, name, before=None):
    n = len(bufs)
    extra = [] if before is None else [before]

    def body(*refs):
        send_sems, recv_sems, token = refs[n + len(extra)], refs[n + len(extra) + 1], refs[-1]
        for k, (src, dst, dev) in enumerate(copies(refs[:n])):
            _rcopy(src, dst, send_sems.at[k], recv_sems.at[k], dev).start()
        token[...] = jnp.zeros_like(token)

    res = _pc(body, name=name, in_specs=[HBM] * n + [ANY] * len(extra),
              out_specs=[SEM, SEM] + [HBM] * n + [pl.BlockSpec(memory_space=pltpu.VMEM)],
              out_shape=[pltpu.SemaphoreType.DMA((n_copies,)), pltpu.SemaphoreType.DMA((n_copies,))]
              + [pltpu.HBM(b.shape, b.dtype) for b in bufs] + [TOKEN],
              input_output_aliases={t: 2 + t for t in range(n)},
              compiler_params=pltpu.CompilerParams(has_side_effects=EFFECT))(*[_in_hbm(b) for b in bufs], *extra)
    return res[0], res[1], list(res[2:2 + n]), res[-1]


def _split_wait(send_sems, recv_sems, bufs, copies, after, *, name):
    n = len(bufs)

    def body(*refs):
        send_ref, recv_ref = refs[n], refs[n + 1]
        for k, (src, dst, dev) in enumerate(copies(refs[:n])):
            cp = _rcopy(src, dst, send_ref.at[k], recv_ref.at[k], dev)
            cp.wait_send()
            cp.wait_recv()

    return _pc(body, name=name, in_specs=[HBM] * n + [SEM, SEM, ANY], out_specs=[HBM] * n,
               out_shape=[pltpu.HBM(b.shape, b.dtype) for b in bufs],
               input_output_aliases={t: t for t in range(n)},
               compiler_params=pltpu.CompilerParams(has_side_effects=EFFECT))(*bufs, send_sems, recv_sems, after)


def _allgather_ici_copies(refs):
    x, y, c, chips = _place()
    return [(_half_rows(r, 2 * x + y, c), _half_rows(r, 2 * x + y, c), (cx, cy, c)) for r in refs for cx, cy in chips]


def _rs_chips_copies(refs):
    x, y, c, chips = _place()
    n = len(refs) // 2
    return [(refs[t].at[2 * cx + cy], refs[n + t].at[j], (cx, cy, c)) for t in range(n) for j, (cx, cy) in enumerate(chips)]


def _rs_sibling_copies(refs):
    x, y, c, _ = _place()
    n = len(refs) // 2
    out = []
    for t in range(n):
        h = refs[t].shape[0] // 8
        out += [(refs[t].at[pl.ds((2 * s + 1 - c) * h, h)], refs[n + t].at[s], (x, y, 1 - c)) for s in range(4)]
    return out


def _allgather_forward(fulls, *, name):
    n = len(fulls)

    def body(*refs):
        outs = refs[n:2 * n]
        send_sems, recv_sems = refs[2 * n:]
        x, y, c, chips = _place()
        sent = []
        for t in range(n):
            for j, (cx, cy) in enumerate(chips):
                landed = _half_rows(outs[t], 2 * cx + cy, c)
                cp = _rcopy(landed, landed, send_sems.at[3 * t + j], recv_sems.at[3 * t + j], (x, y, 1 - c))
                cp.start()
                sent.append(cp)
        for t in range(n):
            for j, (cx, cy) in enumerate(chips):
                other = _half_rows(outs[t], 2 * cx + cy, 1 - c)
                _rcopy(other, other, send_sems.at[3 * t + j], recv_sems.at[3 * t + j], (x, y, 1 - c)).wait_recv()
        for cp in sent:
            cp.wait_send()

    return _pc(body, name=name, in_specs=[ANY] * n, out_specs=[ANY] * n,
               out_shape=[jax.ShapeDtypeStruct(f.shape, f.dtype) for f in fulls],
               input_output_aliases={t: t for t in range(n)},
               scratch_shapes=[pltpu.SemaphoreType.DMA((3 * n,)), pltpu.SemaphoreType.DMA((3 * n,))])(*fulls)


def _rs_sibling(grads, *, name):
    n = len(grads)

    def body(*refs):
        ins, outs = refs[:n], refs[n:2 * n]
        send_sems, recv_sems = refs[2 * n:]
        x, y, c, _ = _place()
        cps = []
        for t in range(n):
            h = ins[t].shape[0] // 8
            for s in range(4):
                cp = _rcopy(ins[t].at[pl.ds((2 * s + 1 - c) * h, h)], outs[t].at[s], send_sems.at[4 * t + s],
                            recv_sems.at[4 * t + s], (x, y, 1 - c))
                cp.start()
                cps.append(cp)
        for cp in cps:
            cp.wait()

    return _pc(body, name=name, in_specs=[ANY] * n, out_specs=[ANY] * n,
               out_shape=[jax.ShapeDtypeStruct((4, g.shape[0] // 8, g.shape[1]), g.dtype) for g in grads],
               scratch_shapes=[pltpu.SemaphoreType.DMA((4 * n,)), pltpu.SemaphoreType.DMA((4 * n,))])(*grads)


def _rs_chips(sends, *, name):
    n = len(sends)

    def body(*refs):
        s_refs, b_refs = refs[:n], refs[n:2 * n]
        send_sems, recv_sems = refs[2 * n:]
        x, y, c, chips = _place()
        cps = []
        for t in range(n):
            for j, (cx, cy) in enumerate(chips):
                cp = _rcopy(s_refs[t].at[2 * cx + cy], b_refs[t].at[j], send_sems.at[3 * t + j], recv_sems.at[3 * t + j],
                            (cx, cy, c))
                cp.start()
                cps.append(cp)
        for cp in cps:
            cp.wait()

    return _pc(body, name=name, in_specs=[ANY] * n, out_specs=[ANY] * n,
               out_shape=[jax.ShapeDtypeStruct((3,) + s.shape[1:], s.dtype) for s in sends],
               scratch_shapes=[pltpu.SemaphoreType.DMA((3 * n,)), pltpu.SemaphoreType.DMA((3 * n,))])(*sends)


def _rs_final(fulls, *, name):
    n = len(fulls)

    def body(*refs):
        outs = refs[n:2 * n]
        send_sems, recv_sems = refs[2 * n:]
        x, y, c, _ = _place()
        cps = []
        for t in range(n):
            cp = _rcopy(outs[t].at[c], outs[t].at[c], send_sems.at[t], recv_sems.at[t], (x, y, 1 - c))
            cp.start()
            cps.append(cp)
        for cp in cps:
            cp.wait()

    return _pc(body, name=name, in_specs=[ANY] * n, out_specs=[ANY] * n,
               out_shape=[jax.ShapeDtypeStruct(f.shape, f.dtype) for f in fulls],
               input_output_aliases={t: t for t in range(n)},
               scratch_shapes=[pltpu.SemaphoreType.DMA((n,)), pltpu.SemaphoreType.DMA((n,))])(*fulls)


def _add_halves(g, a, send_dtype, *, name):
    _, h, cols = a.shape
    th = _row_tile(h, cols)
    g4 = g.reshape(4, 2, h, cols)
    c = lax.axis_index("c").astype(jnp.int32).reshape(1)

    def body(c_ref, g_ref, a_ref, p_ref, s_ref):
        v = g_ref[...].astype(F32) + a_ref[...].astype(F32)
        p_ref[...] = v
        s_ref[...] = v.astype(send_dtype)

    return _pc(body, name=name,
               grid_spec=pltpu.PrefetchScalarGridSpec(
                   num_scalar_prefetch=1, grid=(4, h // th),
                   in_specs=[pl.BlockSpec((None, None, th, cols), lambda s, i, cr: (s, cr[0], i, 0)),
                             pl.BlockSpec((None, th, cols), lambda s, i, cr: (s, i, 0))],
                   out_specs=[pl.BlockSpec((None, th, cols), lambda s, i, cr: (s, i, 0))] * 2),
               out_shape=[jax.ShapeDtypeStruct(a.shape, F32), jax.ShapeDtypeStruct(a.shape, send_dtype)],
               compiler_params=pltpu.CompilerParams(dimension_semantics=("arbitrary", "arbitrary")))(c, g4, a)


def _add_chips(p, b, *, name, order=None):
    _, h, cols = p.shape
    th = _row_tile(h, cols)
    idx = jnp.stack([2 * lax.axis_index("x") + lax.axis_index("y"), lax.axis_index("c")]).astype(jnp.int32)
    extra = [] if order is None else [order]

    def body(idx_ref, p_ref, b_ref, *rest):
        r_ref = rest[-1]
        r_ref[...] = ((p_ref[...] + b_ref[0].astype(F32)) + b_ref[1].astype(F32)) + b_ref[2].astype(F32)

    return _pc(body, name=name,
               grid_spec=pltpu.PrefetchScalarGridSpec(
                   num_scalar_prefetch=1, grid=(h // th,),
                   in_specs=[pl.BlockSpec((None, th, cols), lambda i, ir: (ir[0], i, 0)),
                             pl.BlockSpec((3, th, cols), lambda i, ir: (0, i, 0))] + [ANY] * len(extra),
                   out_specs=pl.BlockSpec((None, th, cols), lambda i, ir: (ir[1], i, 0))),
               out_shape=jax.ShapeDtypeStruct((2, h, cols), F32),
               compiler_params=pltpu.CompilerParams(dimension_semantics=("arbitrary",)))(idx, p, b, *extra)


def _add_halves_all(grads, recv, send_dtypes, tag):
    parts, sends = [], []
    for t, (g, a) in enumerate(zip(grads, recv)):
        p, s = _add_halves(g, a, send_dtypes[t], name=f"rs_add_halves_{tag}{t}")
        parts.append(p)
        sends.append(s)
    return parts, sends


def _rs_finish(parts, others, tag, order=None):
    halves = [_add_chips(p, b, order=order, name=f"rs_add_chips_{tag}{t}") for t, (p, b) in enumerate(zip(parts, others))]
    full = _rs_final(halves, name=f"rs_final_{tag}")
    return [f.reshape(-1, f.shape[-1]) for f in full]


def _s5_discretize(lam_re, lam_im, log_dt, b_re, b_im):
    lam = lax.complex(lam_re, lam_im)
    dt = jnp.exp(log_dt)[:, None]
    lam_bar = jnp.exp(lam * dt)
    b_bar = ((lam_bar - 1.0) / lam)[..., None] * lax.complex(b_re, b_im)
    return jnp.real(lam_bar), jnp.imag(lam_bar), jnp.real(b_bar), jnp.imag(b_bar)


def _lanes_from_gp(re, im, cfg):
    v = jnp.stack([re, im]).reshape(2, cfg.NB, GROUPS_PER_BLOCK, SSM_STATE)
    return jnp.transpose(v, (1, 0, 2, 3)).reshape(1, cfg.NL)


def _gp_from_lanes(v, cfg):
    v = jnp.transpose(v.reshape(cfg.NB, 2, GROUPS_PER_BLOCK, SSM_STATE), (1, 0, 2, 3)).reshape(2, cfg.G, SSM_STATE)
    return v[0], v[1]


def _bb_band(bb_re, bb_im, cfg):
    eye = jnp.eye(GROUPS_PER_BLOCK, dtype=F32)
    bb = jnp.stack([bb_re, bb_im]).reshape(2, cfg.NB, GROUPS_PER_BLOCK, SSM_STATE, SSM_GROUP)
    return jnp.einsum('rjgpc,gh->jgcrhp', bb, eye).reshape(cfg.DS, 2 * GROUPS_PER_BLOCK * SSM_STATE)


def _bb_from_band(m, cfg):
    eye = jnp.eye(GROUPS_PER_BLOCK, dtype=F32)
    m = m.reshape(cfg.NB, GROUPS_PER_BLOCK, SSM_GROUP, 2, GROUPS_PER_BLOCK, SSM_STATE)
    v = jnp.einsum('jgcrhp,gh->rjgpc', m, eye).reshape(2, cfg.G, SSM_STATE, SSM_GROUP)
    return v[0], v[1]


def _cc_band(c_re, c_im, cfg):
    eye = jnp.eye(GROUPS_PER_BLOCK, dtype=F32)
    cc = jnp.stack([c_re, -c_im]).reshape(2, cfg.NB, GROUPS_PER_BLOCK, SSM_GROUP, SSM_STATE)
    return jnp.einsum('rjgcp,gh->jrhpgc', cc, eye).reshape(cfg.NL, GROUPS_PER_BLOCK * SSM_GROUP)


def _cc_from_band(m, cfg):
    eye = jnp.eye(GROUPS_PER_BLOCK, dtype=F32)
    m = m.reshape(cfg.NB, 2, GROUPS_PER_BLOCK, SSM_STATE, GROUPS_PER_BLOCK, SSM_GROUP)
    v = jnp.einsum('jrhpgc,gh->rjgcp', m, eye).reshape(2, cfg.G, SSM_GROUP, SSM_STATE)
    return v[0], -v[1]


PACK_COLS = 512
PACK_ROW_ALIGN = 64


def _pack(arrs):
    flat = jnp.concatenate([a.reshape(-1).astype(F32) for a in arrs])
    unit = PACK_COLS * PACK_ROW_ALIGN
    total = -(-flat.shape[0] // unit) * unit
    return jnp.pad(flat, (0, total - flat.shape[0])).reshape(-1, PACK_COLS)


def _unpack(p, shapes):
    flat = p.reshape(-1)
    out, off = [], 0
    for shp in shapes:
        size = math.prod(shp)
        out.append(flat[off:off + size].reshape(shp))
        off += size
    return out


def _adamw(w, g, m, v, *, name, emit_grad=False):
    c1 = 1.0 / (1.0 - ADAM_B1 ** ADAM_STEP)
    c2 = 1.0 / (1.0 - ADAM_B2 ** ADAM_STEP)

    rows, cols = w.shape
    tc = _tile(cols, 512)
    tm = _tile(rows, max(8, 3 * DT_F32_BLOCK_BYTES // (8 * tc)), 8)
    n_out = 4 if emit_grad else 3

    def body(w_ref, g_ref, m_ref, v_ref, *o_refs):
        gv = g_ref[...]
        mn = ADAM_B1 * m_ref[...] + (1.0 - ADAM_B1) * gv
        vn = ADAM_B2 * v_ref[...] + (1.0 - ADAM_B2) * (gv * gv)
        delta = -ADAM_LR * ((mn * c1) / (jnp.sqrt(vn * c2) + ADAM_EPS) + ADAM_WD * w_ref[...])
        for o_ref, val in zip(o_refs, ((gv, delta, mn, vn) if emit_grad else (delta, mn, vn))):
            o_ref[...] = val

    blk = pl.BlockSpec((tm, tc), lambda i, j: (i, j))
    return _pc(body, name=name, grid=(rows // tm, cols // tc), in_specs=[blk] * 4, out_specs=[blk] * n_out,
               out_shape=[jax.ShapeDtypeStruct((rows, cols), F32)] * n_out,
               compiler_params=pltpu.CompilerParams(dimension_semantics=("parallel", "parallel")))(w, g, m, v)


def _to_comm_layout(name, w, cfg):
    w = w[0]
    if name == 'w_in':
        return jnp.pad(w, ((0, 0), (0, cfg.DINP - cfg.DIN)))
    if name == 'w_q_b':
        hs = w.shape[1] // (QK_NOPE + QK_ROPE)
        wt = w.T.reshape(hs, QK_NOPE + QK_ROPE, cfg.QL)
        return jnp.pad(wt, ((0, 0), (0, HEAD_SLOT - QK_NOPE - QK_ROPE), (0, 0))).reshape(hs * HEAD_SLOT, cfg.QL)
    if name == 'w_kv_b':
        return w.T
    if name == 'w_up':
        wt = w.T.reshape(2, cfg.F // 4, cfg.D)
        return jnp.pad(wt, ((0, 0), (0, cfg.FQ - cfg.F // 4), (0, 0))).reshape(2 * cfg.FQ, cfg.D)
    if name == 'w_down':
        return jnp.pad(w, ((0, cfg.FQ - cfg.F // 4), (0, 0)))
    return w


def _from_comm_layout(name, g, cfg):
    if name == 'w_in':
        g = g[:, :cfg.DIN]
    elif name == 'w_q_b':
        hs = g.shape[0] // HEAD_SLOT
        g = g.reshape(hs, HEAD_SLOT, cfg.QL)[:, :QK_NOPE + QK_ROPE].reshape(hs * (QK_NOPE + QK_ROPE), cfg.QL).T
    elif name == 'w_kv_b':
        g = g.T
    elif name == 'w_up':
        g = g.reshape(2, cfg.FQ, cfg.D)[:, :cfg.F // 4].reshape(cfg.F // 2, cfg.D).T
    elif name == 'w_down':
        g = g[:cfg.F // 4]
    return g[None]


def _ff_pad(v, cfg):
    k = v.shape[0]
    return jnp.pad(v.reshape(k, 4, cfg.F // 4), ((0, 0), (0, 0), (0, cfg.FQ - cfg.F // 4))).reshape(k, cfg.FP)


def _ff_unpad(v, cfg):
    k = v.shape[0]
    return v.reshape(k, 4, cfg.FQ)[:, :, :cfg.F // 4].reshape(k, cfg.F)


def _step(cfg, w, m, v, x, loss_target):
    lp, d, ds, nl = cfg.LP, cfg.D, cfg.DS, cfg.NL
    xi, yi = lax.axis_index("x"), lax.axis_index("y")
    me = 2 * xi + yi

    def place(n):
        return _place_shard(_to_comm_layout(n, w[n], cfg), BF16, name=f"place_{n}")

    first = [place('w_in'), _place_shard(w['meta_tokens'], F32, name="place_meta")]
    f_send, f_recv, f_flying, f_token = _split_start(first, _allgather_ici_copies, 6, name="allgather_first_start")
    conv_w_shard = jnp.pad(w['conv_w'][0], ((0, ROW_ALIGN - 3), (0, cfg.FQ - cfg.F // 4))) + f_token[0:1, 0:1]
    placed = [None] + [place(n) for n in BIG[1:]] + [None, _place_shard(conv_w_shard, F32, name="place_conv_w")]
    f_landed = _split_wait(f_send, f_recv, f_flying, _allgather_ici_copies, placed[6], name="allgather_first_wait")
    w_in, meta_full = _allgather_forward(f_landed, name="allgather_first_forward")
    meta = jnp.transpose(meta_full.reshape(4, N_META, d // 4), (1, 0, 2)).reshape(N_META, d)
    conv_b = _ff_pad(w['conv_b'], cfg)
    mid = placed[1:5] + [placed[8]]
    mid_send, mid_recv, mid_flying, mid_token = _split_start(mid, _allgather_ici_copies, 3 * len(mid), before=meta_full,
                                                             name="allgather_mid_start")
    ffn_send, ffn_recv, ffn_flying, ffn_token = _split_start(placed[5:7], _allgather_ici_copies, 6, before=mid_token,
                                                             name="allgather_ffn_start")
    mix_norm = w['mix_norm'] + (mid_token[0:1, 0:1] + ffn_token[0:1, 0:1])

    pos = (jnp.arange(lp, dtype=jnp.int32) - PAD).astype(F32)
    inv_freq = 1.0 / (ROPE_BASE ** (jnp.arange(0, QK_ROPE, 2, dtype=F32) / QK_ROPE))
    ang = pos[:, None] * inv_freq[None, :]
    zpad = jnp.zeros((lp, LANE - QK_ROPE), F32)
    cos_t = jnp.concatenate([jnp.cos(ang), jnp.cos(ang), zpad], axis=1)
    sin_t = jnp.concatenate([jnp.sin(ang), jnp.sin(ang), zpad], axis=1)

    s5_in = (w['lam_re'][0], w['lam_im'][0], w['log_dt'][0], w['b_re'][0], w['b_im'][0])
    (a_re, a_im, bb_re, bb_im), s5_vjp = jax.vjp(_s5_discretize, *s5_in)
    lam_dt = lax.complex(s5_in[0], s5_in[1]) * jnp.exp(s5_in[2])[:, None]
    a_pow = jnp.exp(jnp.arange(1, 9, dtype=F32)[:, None, None] * lam_dt[None])
    pw_fwd = jnp.concatenate([_lanes_from_gp(jnp.real(a_pow[r]), jnp.imag(a_pow[r]), cfg) for r in range(8)], axis=0)
    pw_bwd = jnp.concatenate([_lanes_from_gp(jnp.real(a_pow[7 - r]), -jnp.imag(a_pow[7 - r]), cfg) for r in range(8)], axis=0)
    bb_band = _bb_band(bb_re, bb_im, cfg).astype(BF16)
    cc_band = _cc_band(w['c_re'][0], w['c_im'][0], cfg).astype(BF16)
    d_skip, b_glu = w['d_skip'], w['b_glu']

    h0 = jnp.concatenate([jnp.zeros((PAD, d), F32), meta, x[0]], axis=0)
    xn = _rms_fwd(h0, mix_norm, name="rms_mix")
    z = _mm(xn, w_in, name="mm_in", tn=_tile(cfg.DINP, 640))
    u = (z, ds, 0)
    q_a = (z, cfg.QL, ds // cfg.QL)
    kv_a = (z, cfg.KVL, (ds + cfg.QL) // cfg.KVL)
    k_pe = (z, LANE, (ds + cfg.QL + cfg.KVL) // LANE)

    hs, yc = _s5_fwd(z, bb_band, cc_band, pw_fwd, cfg, name="s5_fwd")

    def s5_y(ycv, uv, dk):
        return ycv + dk * uv

    gl = _ew(lambda rid, ycv, uv, dk: jax.nn.gelu(s5_y(ycv, uv, dk)), [yc, u], [d_skip], [(ds, BF16)], name="s5_gelu")[0]
    mid_landed = _split_wait(mid_send, mid_recv, mid_flying, _allgather_ici_copies, gl, name="allgather_mid_wait")
    w_glu, w_qt, w_kvt, w_out, conv_full = _allgather_forward(mid_landed, name="allgather_mid_forward")
    conv_w = jnp.transpose(conv_full.reshape(4, ROW_ALIGN, cfg.FQ)[:, :3], (1, 0, 2)).reshape(3, cfg.FP)
    tg = _mm(gl, w_glu, name="mm_glu")
    ya = _ew(lambda rid, ycv, uv, tv, dk, bg: jax.nn.gelu(s5_y(ycv, uv, dk)) * jax.nn.sigmoid(tv + bg),
             [yc, u, tg], [d_skip, b_glu], [(ds, F32)], name="s5_glu")[0]

    qn = _rms_fwd(q_a, w['q_a_norm'], name="rms_q")
    kvn = _rms_fwd(kv_a, w['kv_a_norm'], name="rms_kv")
    q_raw = _mm(qn, w_qt, tb=True, name="mm_q")
    qx = _ew(_rope_heads(_rope, cfg.H), [q_raw, cos_t, sin_t], [], [(cfg.H * HEAD_SLOT, BF16)], name="rope_q")[0]
    kv = _mm(kvn, w_kvt, tb=True, out_dtype=BF16, name="mm_kv")
    kr = _ew(lambda rid, kp, cs, sn: _rope(kp, cs, sn), [k_pe, cos_t, sin_t], [], [(LANE, BF16)], name="rope_k")[0]
    o, lse = _attn_fwd(qx, kv, kr, cfg, name="attn_fwd")

    def norm2(rid, yav, ov, gs, ga):
        return jnp.concatenate([_rms_parts(yav, gs)[0] * gs, _rms_parts(ov, ga)[0] * ga], axis=1)

    yn = _ew(norm2, [ya, o], [w['out_norm_ssm'], w['out_norm_attn']], [(cfg.DMIX, BF16)], name="rms_out")[0]
    h1 = _mm(yn, w_out, res=h0, name="mm_out")
    xn2 = _rms_fwd(h1, w['ffn_norm'], name="rms_ffn")
    ffn_landed = _split_wait(ffn_send, ffn_recv, ffn_flying, _allgather_ici_copies, xn2, name="allgather_ffn_wait")
    w_upt, w_down = _allgather_forward(ffn_landed, name="allgather_ffn_forward")
    up = _mm(xn2, w_upt, tb=True, out_dtype=BF16, name="mm_up")
    act = _conv_fwd(up, conv_w, conv_b, name="conv_fwd")
    h2 = _mm(act, w_down, res=h1, tm=_tile(lp, 544, ROW_ALIGN), name="mm_down")

    g_final = w['final_norm'].reshape(1, d)

    def head(rid, hv, tv, gv):
        xhat, r = _rms_parts(hv, gv)
        valid = rid >= PAD + N_META
        diff = jnp.where(valid, xhat * gv - tv, 0.0)
        dout = diff * (1.0 / d)
        dxhat = dout * gv
        dx = r * (dxhat - xhat * jnp.mean(dxhat * xhat, axis=-1, keepdims=True))
        return dx, dx, dout * xhat, 0.5 * diff * dout

    dh2, dh2_b, dg_final, loss_cols = _ew(head, [h2, (loss_target[0], d, 0, SKIP)], [g_final], [(d, F32), (d, BF16)], [d, d],
                                          tm=PAD + N_META, name="loss_head")
    loss = lax.psum(jnp.sum(loss_cols), ("x", "y", "c"))

    dact = _mm(dh2_b, w_down, tb=True, out_dtype=BF16, name="mm_dact")
    dw_down = _mm(act, dh2_b, ta=True, tn=d, tm=512, out_dtype=BF16, name="mm_dw_down")

    def sibling_start(g, tag):
        land = lax.empty((4, g.shape[0] // 8, g.shape[1]), g.dtype)
        return _split_start([g, land], _rs_sibling_copies, 4, name=f"rs_sibling_{tag}_start")

    dn_send, dn_recv, dn_flying, dn_token = sibling_start(dw_down, "down")
    dup, dconv_w, dconv_b = _conv_bwd(up, dact, conv_w, conv_b + dn_token[0:1, 0:1], name="conv_bwd")
    tk_up, tm_up = _tile(cfg.FP, 1408), _tile(cfg.FP, 512)
    dw_upt = _mm(dup, xn2, ta=True, dims=(2 * cfg.FP, d, lp), tn=d, tm=tm_up, a_lead=True, out_dtype=BF16, name="mm_dw_up",
                 a_idx=lambda i, j, k: (i // (cfg.FP // tm_up), 0, i % (cfg.FP // tm_up)))
    up_send, up_recv, up_flying, up_token = sibling_start(dw_upt, "up")
    dxn2 = _mm(dup, w_upt, dims=(lp, d, 2 * cfg.FP), tk=tk_up, tn=1024, a_lead=True, name="mm_dxn2",
               a_idx=lambda i, j, k: (k // (cfg.FP // tk_up), i, k % (cfg.FP // tk_up)))
    dh1, dh1_b, dg_ffn = _rms_bwd(h1, w['ffn_norm'] + up_token[0:1, 0:1], dxn2, res=dh2, mask=True, with_bf16=True,
                                  name="rms_ffn_bwd")

    dyn = _mm(dh1_b, w_out, tb=True, name="mm_dyn")
    dw_out = _mm(yn, dh1_b, ta=True, tn=d, tm=512, name="mm_dw_out")
    up_done = _split_wait(up_send, up_recv, up_flying, _rs_sibling_copies, dw_out, name="rs_sibling_up_wait")
    dn_done = _split_wait(dn_send, dn_recv, dn_flying, _rs_sibling_copies, dw_out, name="rs_sibling_down_wait")
    early_parts, early_sends = _add_halves_all([up_done[0], dn_done[0]], [up_done[1], dn_done[1]], [BF16] * 2, "early")
    chip_lands = [lax.empty((3,) + s.shape[1:], s.dtype) for s in early_sends]
    ch_send, ch_recv, ch_flying, ch_token = _split_start(early_sends + chip_lands, _rs_chips_copies, 6,
                                                         name="rs_chips_early_start")
    dya, dg_ssm = _rms_bwd(ya, w['out_norm_ssm'] + ch_token[0:1, 0:1], (dyn, ds, 0), name="rms_ssm_bwd")
    do, dg_attn = _rms_bwd(o, w['out_norm_attn'], (dyn, cfg.DATTN, ds // cfg.DATTN), name="rms_attn_bwd")

    dqx, dkv, dkr = _attn_bwd(qx, kv, kr, o, lse, do, cfg, name="attn_bwd")
    dq_raw = _ew(_rope_heads(_unrope, cfg.H), [dqx, cos_t, sin_t], [], [(cfg.H * HEAD_SLOT, BF16)], name="unrope_q")[0]
    dk_pe = _ew(lambda rid, dk, cs, sn: _unrope(dk, cs, sn), [dkr, cos_t, sin_t], [], [(LANE, F32)], name="unrope_k")[0]
    dqn = _mm(dq_raw, w_qt, name="mm_dqn")
    dw_qt = _mm(dq_raw, qn, ta=True, tm=512, name="mm_dw_q")
    dkvn = _mm(dkv, w_kvt, name="mm_dkvn")
    dw_kvt = _mm(dkv, kvn, ta=True, tm=512, name="mm_dw_kv")
    dq_a, dg_q = _rms_bwd(q_a, w['q_a_norm'], dqn, name="rms_q_bwd")
    dkv_a, dg_kv = _rms_bwd(kv_a, w['kv_a_norm'], dkvn, name="rms_kv_bwd")

    def glu_bwd(rid, ycv, uv, tv, dyav, dk, bg):
        gelu = jax.nn.gelu(s5_y(ycv, uv, dk))
        sg = jax.nn.sigmoid(tv + bg)
        dt = dyav * gelu * sg * (1.0 - sg)
        return dt, dyav * sg, dt

    dt_b, dgl1, db_glu = _ew(glu_bwd, [yc, u, tg, dya], [d_skip, b_glu], [(ds, BF16), (ds, F32)], [ds], name="s5_glu_bwd")
    dgl = _mm(dt_b, w_glu, tb=True, res=dgl1, name="mm_dgl")
    dw_glu = _mm(gl, dt_b, ta=True, tm=512, name="mm_dw_glu")

    def gelu_bwd(rid, ycv, uv, dglv, dk):
        _, vjp = jax.vjp(jax.nn.gelu, s5_y(ycv, uv, dk))
        dy = vjp(dglv)[0]
        return dy, dy * dk, dy * uv

    mid_grads = [dw_out, dw_glu, dw_qt, dw_kvt]
    mid_lands = [lax.empty((4, g.shape[0] // 8, g.shape[1]), g.dtype) for g in mid_grads]
    ms_send, ms_recv, ms_flying, ms_token = _split_start(mid_grads + mid_lands, _rs_sibling_copies, 4 * len(mid_grads),
                                                         name="rs_sibling_mid_start")
    dy_b, du_skip, dd_skip = _ew(gelu_bwd, [yc, u, dgl], [d_skip + ms_token[0:1, 0:1]], [(ds, BF16), (ds, F32)], [ds],
                                 name="s5_gelu_bwd")
    ms_done = _split_wait(ms_send, ms_recv, ms_flying, _rs_sibling_copies, dy_b, name="rs_sibling_mid_wait")
    mid_parts, mid_sends = _add_halves_all(ms_done[:4], ms_done[4:], [BF16] * 4, "mid")
    mid_chip_lands = [lax.empty((3,) + s.shape[1:], s.dtype) for s in mid_sends]
    mc_send, mc_recv, mc_flying, mc_token = _split_start(mid_sends + mid_chip_lands, _rs_chips_copies, 3 * len(mid_sends),
                                                         name="rs_chips_mid_start")
    du, dbb_band, dcc_band, da_l = _s5_bwd(dy_b, hs, z, bb_band, cc_band, pw_bwd + mc_token[0:1, 0:1], du_skip, cfg,
                                           name="s5_bwd")

    dz = jnp.concatenate([du, dq_a, dkv_a, dk_pe], axis=1).astype(BF16)
    dxn = _mm(dz, w_in, tb=True, name="mm_dxn")
    dw_in = _mm(xn, dz, ta=True, tm=512, tn=_tile(cfg.DINP, 1024), name="mm_dw_in")
    def mix_bwd(rid, xv, dyv, resv, gv):
        dx, dg = _rms_bwd_block(xv, gv, dyv)
        dx = dx + resv
        return dx, dx, dg

    grad_x, dh0_head, dg_mix = _ew(mix_bwd, [h0, dxn, dh1], [mix_norm], [(d, F32, SKIP), (d, F32, FIRST)], [d],
                                   tm=PAD + N_META, name="rms_mix_bwd")
    grad_x = grad_x[None]

    da_re, da_im = _gp_from_lanes(da_l, cfg)
    dbb_re, dbb_im = _bb_from_band(dbb_band, cfg)
    dlam_re, dlam_im, dlog_dt, db_re, db_im = s5_vjp((da_re, da_im, dbb_re, dbb_im))
    dc_re, dc_im = _cc_from_band(dcc_band, cfg)
    local_small = {
        'meta_tokens': dh0_head[PAD:], 'mix_norm': dg_mix, 'lam_re': dlam_re, 'lam_im': dlam_im, 'log_dt': dlog_dt,
        'b_re': db_re, 'b_im': db_im, 'c_re': dc_re, 'c_im': dc_im, 'd_skip': dd_skip, 'b_glu': db_glu, 'q_a_norm': dg_q,
        'kv_a_norm': dg_kv, 'out_norm_ssm': dg_ssm, 'out_norm_attn': dg_attn, 'ffn_norm': dg_ffn,
        'conv_w': _ff_unpad(dconv_w, cfg), 'conv_b': _ff_unpad(dconv_b, cfg), 'final_norm': dg_final,
    }
    small_shapes = [local_small[n].shape for n in SMALL]

    small_pack = _pack([local_small[n] for n in SMALL])
    end_local = [dw_in, small_pack]
    end_recv = _rs_sibling(end_local, name="rs_sibling_end")
    end_parts, end_sends = _add_halves_all(end_local, end_recv, [BF16, F32], "end")
    end_lands = [lax.empty((3,) + s.shape[1:], s.dtype) for s in end_sends]
    ec_send, ec_recv, ec_flying, ec_token = _split_start(end_sends + end_lands, _rs_chips_copies, 3 * len(end_sends),
                                                         name="rs_chips_end_start")
    ch_done = _split_wait(ch_send, ch_recv, ch_flying, _rs_chips_copies, ec_token, name="rs_chips_early_wait")
    red_up, red_down = _rs_finish(early_parts, ch_done[2:], "early")

    delta, new_m, new_v, grads = {}, {}, {}, {}
    padded_rows = ('w_down',)

    def adamw_big(n, red):
        shp = w[n].shape
        w2, m2, v2 = [t.reshape(shp[-2], shp[-1]) for t in (w[n], m[n], v[n])]
        if n in padded_rows:
            g2, dl, mn, vn = _adamw(w2, red, m2, v2, emit_grad=True, name=f"adamw_{n}")
            grads[n] = g2.reshape(shp)
        else:
            grads[n] = _from_comm_layout(n, red, cfg)
            dl, mn, vn = _adamw(w2, grads[n].reshape(shp[-2], shp[-1]), m2, v2, name=f"adamw_{n}")
        delta[n], new_m[n], new_v[n] = dl.reshape(shp), mn.reshape(shp), vn.reshape(shp)

    adamw_big('w_up', red_up)
    adamw_big('w_down', red_down)
    mc_done = _split_wait(mc_send, mc_recv, mc_flying, _rs_chips_copies, delta['w_down'], name="rs_chips_mid_wait")
    ec_done = _split_wait(ec_send, ec_recv, ec_flying, _rs_chips_copies, mc_done[0], name="rs_chips_end_wait")
    red = _rs_finish(mid_parts + end_parts, list(mc_done[len(mid_sends):]) + list(ec_done[len(end_sends):]), "rest")
    small_full = _allgather([_place_shard(red[5], F32, name="place_small")], name="allgather_small")[0]
    small_sum = dict(zip(SMALL, _unpack(small_full, small_shapes)))
    for n, r in zip(['w_out', 'w_glu', 'w_q_b', 'w_kv_b', 'w_in'], red[:5]):
        adamw_big(n, r)

    for n in SMALL:
        g = small_sum[n]
        if n == 'meta_tokens':
            g = lax.dynamic_slice_in_dim(g, me * (d // 4), d // 4, axis=1)
        elif n == 'conv_w':
            g = lax.dynamic_slice_in_dim(g, me * (cfg.F // 4), cfg.F // 4, axis=1)[None]
        else:
            g = g.reshape(w[n].shape)
        grads[n] = g

    shapes = [w[n].shape for n in SMALL]
    packs = [_pack([src[n] for n in SMALL]) for src in (w, grads, m, v)]
    for dst, p in zip((delta, new_m, new_v), _adamw(*packs, name="adamw_small")):
        dst.update(zip(SMALL, _unpack(p, shapes)))

    return (loss, grad_x, *[grads[n] for n in WEIGHTS], *[delta[n] for n in WEIGHTS],
            *[new_m[n] for n in WEIGHTS], *[new_v[n] for n in WEIGHTS])


def kernel(x, meta_tokens, mix_norm, w_in, lam_re, lam_im, log_dt, b_re, b_im, c_re, c_im, d_skip, w_glu, b_glu, q_a_norm, w_q_b, kv_a_norm, w_kv_b, out_norm_ssm, out_norm_attn, w_out, ffn_norm, w_up, conv_w, conv_b, w_down, final_norm, loss_target, m_meta_tokens, m_mix_norm, m_w_in, m_lam_re, m_lam_im, m_log_dt, m_b_re, m_b_im, m_c_re, m_c_im, m_d_skip, m_w_glu, m_b_glu, m_q_a_norm, m_w_q_b, m_kv_a_norm, m_w_kv_b, m_out_norm_ssm, m_out_norm_attn, m_w_out, m_ffn_norm, m_w_up, m_conv_w, m_conv_b, m_w_down, m_final_norm, v_meta_tokens, v_mix_norm, v_w_in, v_lam_re, v_lam_im, v_log_dt, v_b_re, v_b_im, v_c_re, v_c_im, v_d_skip, v_w_glu, v_b_glu, v_q_a_norm, v_w_q_b, v_kv_a_norm, v_w_kv_b, v_out_norm_ssm, v_out_norm_attn, v_w_out, v_ffn_norm, v_w_up, v_conv_w, v_conv_b, v_w_down, v_final_norm):
    args = dict(locals())
    w = {n: args[n] for n in WEIGHTS}
    m = {n: args["m_" + n] for n in WEIGHTS}
    v = {n: args["v_" + n] for n in WEIGHTS}
    return _step(PROD, w, m, v, x, loss_target)
```

```python
import functools
import math
from typing import NamedTuple

import jax
import jax.numpy as jnp
from jax import lax
from jax.experimental import pallas as pl
from jax.experimental.pallas import tpu as pltpu

F32, BF16 = jnp.float32, jnp.bfloat16
MESH = pl.DeviceIdType.MESH
LANE = 128
ROW_ALIGN = 16
N_META = 16
PAD = 112
CHUNK = 64
SSM_GROUP = 16
SSM_STATE = 64
GROUPS_PER_BLOCK = 8
QK_NOPE, QK_ROPE, V_HEAD = 128, 64, 128
HEAD_SLOT = 256
ROPE_BASE = 10000.0
EPS = 1e-6
ADAM_LR, ADAM_B1, ADAM_B2, ADAM_EPS, ADAM_WD, ADAM_STEP = 0.001, 0.9, 0.999, 1e-08, 0.01, 10
DT_F32_BLOCK_BYTES = 1 << 20
SKIP, FIRST = "skip", "first"


class Cfg(NamedTuple):
    D: int
    S: int
    DS: int
    H: int
    QL: int
    KVL: int
    F: int

    @property
    def LP(self):
        return PAD + N_META + self.S

    @property
    def G(self):
        return self.DS // SSM_GROUP

    @property
    def NB(self):
        return self.G // GROUPS_PER_BLOCK

    @property
    def NL(self):
        return 2 * self.G * SSM_STATE

    @property
    def DATTN(self):
        return self.H * V_HEAD

    @property
    def DMIX(self):
        return self.DS + self.DATTN

    @property
    def DIN(self):
        return self.DS + self.QL + self.KVL + QK_ROPE

    @property
    def DINP(self):
        return self.DS + self.QL + self.KVL + LANE

    @property
    def FQ(self):
        return -(-(self.F // 4) // LANE) * LANE

    @property
    def FP(self):
        return 4 * self.FQ


PROD = Cfg(D=2048, S=2048, DS=1024, H=8, QL=512, KVL=256, F=5504)

WEIGHTS = ['meta_tokens', 'mix_norm', 'w_in', 'lam_re', 'lam_im', 'log_dt', 'b_re', 'b_im', 'c_re', 'c_im', 'd_skip',
           'w_glu', 'b_glu', 'q_a_norm', 'w_q_b', 'kv_a_norm', 'w_kv_b', 'out_norm_ssm', 'out_norm_attn', 'w_out',
           'ffn_norm', 'w_up', 'conv_w', 'conv_b', 'w_down', 'final_norm']
BIG = ['w_in', 'w_glu', 'w_q_b', 'w_kv_b', 'w_out', 'w_up', 'w_down']
SMALL = [n for n in WEIGHTS if n not in BIG]


def _pc(body, **kw):
    return pl.pallas_call(body, **kw)


def _tile(n, target, align=LANE):
    best = None
    d = align
    while d <= min(n, target):
        if n % d == 0:
            best = d
        d += align
    return best if best is not None else n


def _row_tile(rows, cols):
    return _tile(rows, max(ROW_ALIGN, DT_F32_BLOCK_BYTES // (4 * cols)), ROW_ALIGN)


def _mm(a, b, *, name, ta=False, tb=False, tm=None, tn=512, tk=None, out_dtype=F32, res=None,
        a_idx=None, b_idx=None, dims=None, a_lead=False):
    if dims is None:
        m, k = (a.shape[1], a.shape[0]) if ta else a.shape
        n = b.shape[0] if tb else b.shape[1]
    else:
        m, n, k = dims
    tm = _tile(m, tm or m, LANE if ta else ROW_ALIGN)
    tn = _tile(n, tn)
    tk = _tile(k, tk or k, ROW_ALIGN if (ta and not tb) else LANE)
    nm, nn, nk = m // tm, n // tn, k // tk
    a_idx = a_idx or ((lambda i, j, kk: (kk, i)) if ta else (lambda i, j, kk: (i, kk)))
    b_idx = b_idx or ((lambda i, j, kk: (j, kk)) if tb else (lambda i, j, kk: (kk, j)))
    dn = (((0 if ta else 1,), (1 if tb else 0,)), ((), ()))

    def body(*refs):
        a_ref, b_ref = refs[0], refs[1]
        r_ref = refs[2] if res is not None else None
        o_ref = refs[3] if res is not None else refs[2]
        d = lax.dot_general(a_ref[...].astype(BF16), b_ref[...].astype(BF16), dn, preferred_element_type=F32)

        def finish(r):
            if r_ref is not None:
                r = r + r_ref[...].astype(F32)
            o_ref[...] = r.astype(out_dtype)

        if nk == 1:
            finish(d)
        else:
            acc = refs[-1]
            kk = pl.program_id(2)

            @pl.when(kk == 0)
            def _():
                acc[...] = d

            @pl.when(kk > 0)
            def _():
                acc[...] += d

            @pl.when(kk == nk - 1)
            def _():
                finish(acc[...])

    a_blk = ((None,) if a_lead else ()) + ((tk, tm) if ta else (tm, tk))
    in_specs = [pl.BlockSpec(a_blk, a_idx), pl.BlockSpec((tn, tk) if tb else (tk, tn), b_idx)]
    args = [a, b]
    if res is not None:
        in_specs.append(pl.BlockSpec((tm, tn), lambda i, j, kk: (i, j)))
        args.append(res)
    return _pc(body, name=name, grid=(nm, nn, nk), in_specs=in_specs,
               out_specs=pl.BlockSpec((tm, tn), lambda i, j, kk: (i, j)),
               out_shape=jax.ShapeDtypeStruct((m, n), out_dtype),
               scratch_shapes=[pltpu.VMEM((tm, tn), F32)] if nk > 1 else [],
               compiler_params=pltpu.CompilerParams(dimension_semantics=("parallel", "parallel", "arbitrary")))(*args)


def _ew(fn, ins, vecs, outs, sums=(), *, name, tm=None):
    ins = [x if isinstance(x, tuple) else (x, x.shape[1], 0) for x in ins]
    ins = [x if len(x) == 4 else x + (None,) for x in ins]
    outs = [o if len(o) == 3 else o + (None,) for o in outs]
    rows = ins[0][0].shape[0]
    cmax = max([c for _, c, _, _ in ins] + [c for c, _, _ in outs])
    tm = tm or _row_tile(rows, cmax)
    n_in, n_vec, n_out, n_sum = len(ins), len(vecs), len(outs), len(sums)

    def body(*refs):
        i = pl.program_id(0)
        rid = i * tm + lax.broadcasted_iota(jnp.int32, (tm, 1), 0)
        vals = [r[...] for r in refs[:n_in + n_vec]]
        res = fn(rid, *vals)
        res = res if isinstance(res, (tuple, list)) else (res,)
        o_refs = refs[n_in + n_vec:]
        for o_ref, r, (_, _, mode) in zip(o_refs[:n_out], res[:n_out], outs):
            if mode == FIRST:
                @pl.when(i == 0)
                def _():
                    o_ref[...] = r.astype(o_ref.dtype)
            else:
                o_ref[...] = r.astype(o_ref.dtype)
        for o_ref, r in zip(o_refs[n_out:], res[n_out:]):
            part = jnp.sum(r.astype(F32), axis=0, keepdims=True)

            @pl.when(i == 0)
            def _():
                o_ref[...] = part

            @pl.when(i > 0)
            def _():
                o_ref[...] += part

    def row_idx(mode):
        if mode == SKIP:
            return lambda i, cb=0: (jnp.maximum(i - 1, 0), cb)
        if mode == FIRST:
            return lambda i, cb=0: (0, cb)
        return lambda i, cb=0: (i, cb)

    in_specs = [pl.BlockSpec((tm, c), functools.partial(row_idx(mode), cb=cb)) for _, c, cb, mode in ins]
    in_specs += [pl.BlockSpec(v.shape, functools.partial(lambda i, nd: (0,) * nd, nd=v.ndim)) for v in vecs]
    out_specs = [pl.BlockSpec((tm, c), row_idx(mode)) for c, _, mode in outs]
    out_specs += [pl.BlockSpec((1, c), lambda i: (0, 0)) for c in sums]
    out_rows = {None: rows, SKIP: rows - tm, FIRST: tm}
    out_shape = [jax.ShapeDtypeStruct((out_rows[mode], c), dt) for c, dt, mode in outs]
    out_shape += [jax.ShapeDtypeStruct((1, c), F32) for c in sums]
    return _pc(body, name=name, grid=(rows // tm,), in_specs=in_specs, out_specs=out_specs, out_shape=out_shape,
               compiler_params=pltpu.CompilerParams(dimension_semantics=("arbitrary",)))(*[x[0] for x in ins], *vecs)


def _rms_parts(x, g):
    r = lax.rsqrt(jnp.mean(x * x, axis=-1, keepdims=True) + EPS)
    return x * r, r


def _rms_bwd_block(x, g, dy):
    xhat, r = _rms_parts(x, g)
    dxhat = dy * g
    dx = r * (dxhat - xhat * jnp.mean(dxhat * xhat, axis=-1, keepdims=True))
    return dx, dy * xhat


def _rms_fwd(x, g, *, name):
    c = x[1] if isinstance(x, tuple) else x.shape[1]
    return _ew(lambda rid, xv, gv: _rms_parts(xv.astype(F32), gv)[0] * gv, [x], [g], [(c, BF16)], name=name)[0]


def _rms_bwd(x, g, dy, *, name, res=None, mask=False, with_bf16=False):
    c = x[1] if isinstance(x, tuple) else x.shape[1]

    def fn(rid, xv, dyv, *rest):
        gv = rest[-1]
        dx, dg = _rms_bwd_block(xv.astype(F32), gv, dyv.astype(F32))
        if res is not None:
            dx = dx + rest[0]
        if mask:
            dx = jnp.where(rid >= PAD, dx, 0.0)
        return (dx, dx, dg) if with_bf16 else (dx, dg)

    ins = [x, dy] + ([res] if res is not None else [])
    outs = [(c, F32)] + ([(c, BF16)] if with_bf16 else [])
    return _ew(fn, ins, [g], outs, [c], name=name)


S5_W = GROUPS_PER_BLOCK * SSM_STATE
S5_GW = GROUPS_PER_BLOCK * SSM_GROUP
S5_UNROLL = 8
S5_DA_ROWS = 272


def _s5_scan_in_place(ref, pw_ref, *, reverse):
    lp = ref.shape[0]
    tile_rows = 8
    chunk = _tile(lp, S5_DA_ROWS, tile_rows)
    tiles = chunk // tile_rows

    def chunk_body(c, carry):
        rows = pl.ds(pl.multiple_of(c * chunk, tile_rows), chunk)
        xr, xi = ref[rows, :S5_W], ref[rows, S5_W:]
        for level, k in enumerate((1, 2, 4)):
            base = tile_rows * (1 + level)
            mr, mi = pw_ref[base:base + tile_rows, :S5_W][None], pw_ref[base:base + tile_rows, S5_W:][None]
            shift = chunk - k if reverse else k
            sr = pltpu.roll(xr, shift, 0).reshape(tiles, tile_rows, S5_W)
            si = pltpu.roll(xi, shift, 0).reshape(tiles, tile_rows, S5_W)
            xr = xr + (mr * sr - mi * si).reshape(chunk, S5_W)
            xi = xi + (mr * si + mi * sr).reshape(chunk, S5_W)
        ref[rows, :S5_W] = xr
        ref[rows, S5_W:] = xi
        return carry

    lax.fori_loop(0, lp // chunk, chunk_body, 0)

    pr, pi = pw_ref[0:tile_rows, :S5_W], pw_ref[0:tile_rows, S5_W:]
    ntile = lp // tile_rows
    unroll = 4

    def step(n, carry):
        cr, ci = carry
        for q in range(unroll):
            j = n * unroll + q
            j = ntile - 1 - j if reverse else j
            rows = pl.ds(pl.multiple_of(j * tile_rows, tile_rows), tile_rows)
            nr = ref[rows, :S5_W] + (pr * cr - pi * ci)
            ni = ref[rows, S5_W:] + (pr * ci + pi * cr)
            ref[rows, :S5_W] = nr
            ref[rows, S5_W:] = ni
            cr, ci = (nr[0:1], ni[0:1]) if reverse else (nr[tile_rows - 1:], ni[tile_rows - 1:])
        return cr, ci

    z = jnp.zeros((1, S5_W), F32)
    lax.fori_loop(0, ntile // unroll, step, (z, z))


def _s5_fwd(z, bb_band, cc_band, a_l, cfg, *, name):
    lp, ds, nl = cfg.LP, cfg.DS, cfg.NL

    def body(u_ref, bb_ref, cc_ref, a_ref, hs_ref, y_ref):
        hs_ref[...] = jnp.dot(u_ref[...].astype(BF16), bb_ref[...], preferred_element_type=F32)
        _s5_scan_in_place(hs_ref, a_ref, reverse=False)
        y_ref[...] = jnp.dot(hs_ref[...].astype(BF16), cc_ref[...], preferred_element_type=F32)

    return _pc(body, name=name, grid=(cfg.NB,),
               in_specs=[pl.BlockSpec((lp, S5_GW), lambda j: (0, j)), pl.BlockSpec((S5_GW, 2 * S5_W), lambda j: (j, 0)),
                         pl.BlockSpec((2 * S5_W, S5_GW), lambda j: (j, 0)), pl.BlockSpec((32, 2 * S5_W), lambda j: (0, j))],
               out_specs=[pl.BlockSpec((lp, 2 * S5_W), lambda j: (0, j)), pl.BlockSpec((lp, S5_GW), lambda j: (0, j))],
               out_shape=[jax.ShapeDtypeStruct((lp, nl), F32), jax.ShapeDtypeStruct((lp, ds), F32)],
               compiler_params=pltpu.CompilerParams(dimension_semantics=("parallel",)))(z, bb_band, cc_band, a_l)


def _s5_bwd(dy, hs, z, bb_band, cc_band, a_l, du_skip, cfg, *, name):
    lp, ds, nl = cfg.LP, cfg.DS, cfg.NL
    nt = (((1,), (1,)), ((), ()))
    tn = (((0,), (0,)), ((), ()))

    def body(dy_ref, hs_ref, u_ref, bb_ref, cc_ref, a_ref, sk_ref, du_ref, dbb_ref, dcc_ref, da_ref, g_ref):
        dyv = dy_ref[...]
        g_ref[...] = lax.dot_general(dyv, cc_ref[...], nt, preferred_element_type=F32)
        _s5_scan_in_place(g_ref, a_ref, reverse=True)
        dcc_ref[...] = lax.dot_general(hs_ref[...].astype(BF16), dyv, tn, preferred_element_type=F32)
        gb = g_ref[...].astype(BF16)
        dbb_ref[...] = lax.dot_general(u_ref[...].astype(BF16), gb, tn, preferred_element_type=F32)
        du_ref[...] = lax.dot_general(gb, bb_ref[...], nt, preferred_element_type=F32) + sk_ref[...]
        dre = jnp.zeros((1, S5_W), F32)
        dim = jnp.zeros((1, S5_W), F32)
        for r0 in range(0, lp, S5_DA_ROWS):
            rows = min(S5_DA_ROWS, lp - r0)
            first = lax.broadcasted_iota(jnp.int32, (rows, 1), 0) == 0
            prev = hs_ref[r0 - 1:r0, :] if r0 else jnp.zeros((1, 2 * S5_W), F32)
            hr = jnp.where(first, prev[:, :S5_W], pltpu.roll(hs_ref[r0:r0 + rows, :S5_W], 1, 0))
            hi = jnp.where(first, prev[:, S5_W:], pltpu.roll(hs_ref[r0:r0 + rows, S5_W:], 1, 0))
            gr, gi = g_ref[r0:r0 + rows, :S5_W], g_ref[r0:r0 + rows, S5_W:]
            dre = dre + jnp.sum(gr * hr + gi * hi, axis=0, keepdims=True)
            dim = dim + jnp.sum(gi * hr - gr * hi, axis=0, keepdims=True)
        da_ref[:, :S5_W] = dre
        da_ref[:, S5_W:] = dim

    col_blk = pl.BlockSpec((lp, S5_GW), lambda j: (0, j))
    lane_blk = pl.BlockSpec((lp, 2 * S5_W), lambda j: (0, j))
    bb_blk = pl.BlockSpec((S5_GW, 2 * S5_W), lambda j: (j, 0))
    cc_blk = pl.BlockSpec((2 * S5_W, S5_GW), lambda j: (j, 0))
    a_blk = pl.BlockSpec((1, 2 * S5_W), lambda j: (0, j))
    pw_blk = pl.BlockSpec((32, 2 * S5_W), lambda j: (0, j))
    return _pc(body, name=name, grid=(cfg.NB,),
               in_specs=[col_blk, lane_blk, col_blk, bb_blk, cc_blk, pw_blk, col_blk],
               out_specs=[col_blk, bb_blk, cc_blk, a_blk],
               out_shape=[jax.ShapeDtypeStruct((lp, ds), F32), jax.ShapeDtypeStruct((ds, 2 * S5_W), F32),
                          jax.ShapeDtypeStruct((nl, S5_GW), F32), jax.ShapeDtypeStruct((1, nl), F32)],
               scratch_shapes=[pltpu.VMEM((lp, 2 * S5_W), F32)],
               compiler_params=pltpu.CompilerParams(dimension_semantics=("parallel",)))(dy, hs, z, bb_band, cc_band, a_l, du_skip)


def _conv_gate(pre, cw, cb):
    return cw[0:1] * pltpu.roll(pre, 2, 0) + cw[1:2] * pltpu.roll(pre, 1, 0) + cw[2:3] * pre + cb


def _conv_fwd(up, cw, cb, *, name):
    lp, fp2 = up.shape
    fp = fp2 // 2
    tc = _tile(fp, 256)
    nb = fp // tc

    def body(pre_ref, val_ref, cw_ref, cb_ref, o_ref):
        gate = _conv_gate(pre_ref[...].astype(F32), cw_ref[...], cb_ref[...])
        o_ref[...] = (jax.nn.silu(gate) * val_ref[...].astype(F32)).astype(BF16)

    return _pc(body, name=name, grid=(nb,),
               in_specs=[pl.BlockSpec((lp, tc), lambda j: (0, j)), pl.BlockSpec((lp, tc), lambda j: (0, nb + j)),
                         pl.BlockSpec((3, tc), lambda j: (0, j)), pl.BlockSpec((1, tc), lambda j: (0, j))],
               out_specs=pl.BlockSpec((lp, tc), lambda j: (0, j)),
               out_shape=jax.ShapeDtypeStruct((lp, fp), BF16),
               compiler_params=pltpu.CompilerParams(dimension_semantics=("parallel",)))(up, up, cw, cb)


def _conv_bwd(up, dact, cw, cb, *, name):
    lp, fp2 = up.shape
    fp = fp2 // 2
    tc = _tile(fp, 256)
    nb = fp // tc

    def body(pre_ref, val_ref, da_ref, cw_ref, cb_ref, dup_ref, dcw_ref, dcb_ref):
        pre, val, da, cwv = pre_ref[...].astype(F32), val_ref[...].astype(F32), da_ref[...].astype(F32), cw_ref[...]
        gate = _conv_gate(pre, cwv, cb_ref[...])
        sg = jax.nn.sigmoid(gate)
        dup_ref[1] = (da * (gate * sg)).astype(BF16)
        dgate = da * val * (sg * (1.0 + gate * (1.0 - sg)))
        dpre = cwv[2:3] * dgate + cwv[1:2] * pltpu.roll(dgate, lp - 1, 0) + cwv[0:1] * pltpu.roll(dgate, lp - 2, 0)
        dup_ref[0] = dpre.astype(BF16)
        dcb_ref[...] = jnp.sum(dgate, axis=0, keepdims=True)
        dcw_ref[0:1, :] = jnp.sum(dgate * pltpu.roll(pre, 2, 0), axis=0, keepdims=True)
        dcw_ref[1:2, :] = jnp.sum(dgate * pltpu.roll(pre, 1, 0), axis=0, keepdims=True)
        dcw_ref[2:3, :] = jnp.sum(dgate * pre, axis=0, keepdims=True)

    return _pc(body, name=name, grid=(nb,),
               in_specs=[pl.BlockSpec((lp, tc), lambda j: (0, j)), pl.BlockSpec((lp, tc), lambda j: (0, nb + j)),
                         pl.BlockSpec((lp, tc), lambda j: (0, j)),
                         pl.BlockSpec((3, tc), lambda j: (0, j)), pl.BlockSpec((1, tc), lambda j: (0, j))],
               out_specs=[pl.BlockSpec((2, lp, tc), lambda j: (0, 0, j)),
                          pl.BlockSpec((3, tc), lambda j: (0, j)), pl.BlockSpec((1, tc), lambda j: (0, j))],
               out_shape=[jax.ShapeDtypeStruct((2, lp, fp), BF16), jax.ShapeDtypeStruct((3, fp), F32),
                          jax.ShapeDtypeStruct((1, fp), F32)],
               compiler_params=pltpu.CompilerParams(dimension_semantics=("parallel",)))(up, up, dact, cw, cb)


def _key_limit(i, tq, lp):
    return min(lp, -(-((i + 1) * tq) // LANE) * LANE)


def _attn_mask(i, tq, nk):
    qrow = i * tq + lax.broadcasted_iota(jnp.int32, (tq, 1), 0)
    krow = lax.broadcasted_iota(jnp.int32, (1, nk), 1)
    return (krow >= PAD) & ((krow // CHUNK) <= (qrow // CHUNK)), qrow >= PAD


def _attn_scores(q, kn, kr, i, tq, scale):
    nt = (((1,), (1,)), ((), ()))
    s = lax.dot_general(q[:, :QK_NOPE], kn, nt, preferred_element_type=F32)
    s = s + lax.dot_general(q[:, QK_NOPE:], kr, nt, preferred_element_type=F32)
    mask, qvalid = _attn_mask(i, tq, kn.shape[0])
    return jnp.where(mask, s * scale, jnp.finfo(F32).min), qvalid


def _per_q_block(nq, fn):
    i = pl.program_id(1)
    for blk in range(nq):
        pl.when(i == blk)(functools.partial(fn, blk))


def _attn_fwd(qx, kv, kr, cfg, *, name):
    lp, h = cfg.LP, cfg.H
    tq = _tile(lp, 272, ROW_ALIGN)
    nq = lp // tq
    scale = 1.0 / math.sqrt(QK_NOPE + QK_ROPE)

    def body(q_ref, kn_ref, v_ref, kr_ref, o_ref, lse_ref):
        def block(blk):
            nk = _key_limit(blk, tq, lp)
            s, qvalid = _attn_scores(q_ref[...], kn_ref[:nk], kr_ref[:nk], blk, tq, scale)
            m = jnp.max(s, axis=-1, keepdims=True)
            p = jnp.exp(s - m)
            l = jnp.sum(p, axis=-1, keepdims=True)
            o = jnp.dot(p.astype(BF16), v_ref[:nk], preferred_element_type=F32) / l
            o_ref[...] = jnp.where(qvalid, o, 0.0)
            lse_ref[...] = m + jnp.log(l)

        _per_q_block(nq, block)

    return _pc(body, name=name, grid=(h, nq),
               in_specs=[pl.BlockSpec((tq, HEAD_SLOT), lambda hh, i: (i, hh)),
                         pl.BlockSpec((lp, QK_NOPE), lambda hh, i: (0, 2 * hh)),
                         pl.BlockSpec((lp, V_HEAD), lambda hh, i: (0, 2 * hh + 1)),
                         pl.BlockSpec((lp, LANE), lambda hh, i: (0, 0))],
               out_specs=[pl.BlockSpec((tq, V_HEAD), lambda hh, i: (i, hh)),
                          pl.BlockSpec((None, tq, 1), lambda hh, i: (hh, i, 0))],
               out_shape=[jax.ShapeDtypeStruct((lp, h * V_HEAD), F32), jax.ShapeDtypeStruct((h, lp, 1), F32)],
               compiler_params=pltpu.CompilerParams(dimension_semantics=("parallel", "parallel")))(qx, kv, kv, kr)


def _attn_bwd(qx, kv, kr, o, lse, do, cfg, *, name):
    lp, h = cfg.LP, cfg.H
    tq = _tile(lp, 272, ROW_ALIGN)
    nq = lp // tq
    scale = 1.0 / math.sqrt(QK_NOPE + QK_ROPE)
    tn_dims = (((0,), (0,)), ((), ()))

    def body(q_ref, kn_ref, v_ref, kr_ref, o_ref, lse_ref, do_ref, dq_ref, dkv_ref, dkr_ref, dkv_acc):
        hh, i = pl.program_id(0), pl.program_id(1)

        @pl.when(i == 0)
        def _():
            dkv_acc[...] = jnp.zeros_like(dkv_acc)

        @pl.when((i == 0) & (hh == 0))
        def _():
            dkr_ref[...] = jnp.zeros_like(dkr_ref)

        def block(blk):
            nk = _key_limit(blk, tq, lp)
            q, kn, v, krv = q_ref[...], kn_ref[:nk], v_ref[:nk], kr_ref[:nk]
            s, qvalid = _attn_scores(q, kn, krv, blk, tq, scale)
            dov = jnp.where(qvalid, do_ref[...], 0.0)
            p = jnp.exp(s - lse_ref[...])
            delta = jnp.sum(dov * o_ref[...], axis=-1, keepdims=True)
            dob = dov.astype(BF16)
            dp = lax.dot_general(dob, v, (((1,), (1,)), ((), ())), preferred_element_type=F32)
            ds = (p * (dp - delta) * scale).astype(BF16)
            dq_ref[:, :QK_NOPE] = jnp.dot(ds, kn, preferred_element_type=F32)
            dq_ref[:, QK_NOPE:] = jnp.dot(ds, krv, preferred_element_type=F32)
            dkv_acc[:nk, :QK_NOPE] += lax.dot_general(ds, q[:, :QK_NOPE], tn_dims, preferred_element_type=F32)
            dkv_acc[:nk, QK_NOPE:] += lax.dot_general(p.astype(BF16), dob, tn_dims, preferred_element_type=F32)
            dkr_ref[:nk, :] += lax.dot_general(ds, q[:, QK_NOPE:], tn_dims, preferred_element_type=F32)

        _per_q_block(nq, block)

        @pl.when(i == nq - 1)
        def _():
            dkv_ref[...] = dkv_acc[...].astype(BF16)

    return _pc(body, name=name, grid=(h, nq),
               in_specs=[pl.BlockSpec((tq, HEAD_SLOT), lambda hh, i: (i, hh)),
                         pl.BlockSpec((lp, QK_NOPE), lambda hh, i: (0, 2 * hh)),
                         pl.BlockSpec((lp, V_HEAD), lambda hh, i: (0, 2 * hh + 1)),
                         pl.BlockSpec((lp, LANE), lambda hh, i: (0, 0)),
                         pl.BlockSpec((tq, V_HEAD), lambda hh, i: (i, hh)),
                         pl.BlockSpec((None, tq, 1), lambda hh, i: (hh, i, 0)),
                         pl.BlockSpec((tq, V_HEAD), lambda hh, i: (i, hh))],
               out_specs=[pl.BlockSpec((tq, HEAD_SLOT), lambda hh, i: (i, hh)),
                          pl.BlockSpec((lp, QK_NOPE + V_HEAD), lambda hh, i: (0, hh)),
                          pl.BlockSpec((lp, LANE), lambda hh, i: (0, 0))],
               out_shape=[jax.ShapeDtypeStruct((lp, h * HEAD_SLOT), F32),
                          jax.ShapeDtypeStruct((lp, h * (QK_NOPE + V_HEAD)), BF16),
                          jax.ShapeDtypeStruct((lp, LANE), F32)],
               scratch_shapes=[pltpu.VMEM((lp, QK_NOPE + V_HEAD), F32)],
               compiler_params=pltpu.CompilerParams(dimension_semantics=("arbitrary", "arbitrary")))(qx, kv, kv, kr, o, lse, do)


def _rot_half(x):
    lane = lax.broadcasted_iota(jnp.int32, x.shape, 1)
    half = QK_ROPE // 2
    return jnp.where(lane < half, -pltpu.roll(x, LANE - half, 1), pltpu.roll(x, half, 1))


def _rope(x, cos, sin):
    return x * cos + _rot_half(x) * sin


def _unrope(dy, cos, sin):
    return dy * cos - _rot_half(dy * sin)


def _rope_heads(fn, h):
    def apply(rid, q, cos, sin):
        parts = []
        for hh in range(h):
            parts.append(q[:, hh * HEAD_SLOT: hh * HEAD_SLOT + QK_NOPE])
            parts.append(fn(q[:, hh * HEAD_SLOT + QK_NOPE: (hh + 1) * HEAD_SLOT], cos, sin))
        return jnp.concatenate(parts, axis=1)
    return apply


ANY = pl.BlockSpec(memory_space=pl.ANY)


def _place():
    x, y, c = lax.axis_index("x"), lax.axis_index("y"), lax.axis_index("c")
    chips = [(1 - x, y), (x, 1 - y), (1 - x, 1 - y)]
    return x, y, c, chips


def _rcopy(src, dst, send_sem, recv_sem, dev):
    return pltpu.make_async_remote_copy(src_ref=src, dst_ref=dst, send_sem=send_sem, recv_sem=recv_sem,
                                        device_id=dev, device_id_type=MESH)


def _place_shard(shard, dtype, *, name, order=None):
    r, cols = shard.shape
    tm = _row_tile(r, cols)
    nblk = r // tm
    me = (2 * lax.axis_index("x") + lax.axis_index("y")).astype(jnp.int32).reshape(1)
    extra = [] if order is None else [order]

    def body(me_ref, s_ref, *rest):
        rest[-1][...] = s_ref[...].astype(dtype)

    return _pc(body, name=name,
               grid_spec=pltpu.PrefetchScalarGridSpec(
                   num_scalar_prefetch=1, grid=(nblk,),
                   in_specs=[pl.BlockSpec((tm, cols), lambda i, mr: (i, 0))] + [ANY] * len(extra),
                   out_specs=pl.BlockSpec((tm, cols), lambda i, mr: (mr[0] * nblk + i, 0))),
               out_shape=jax.ShapeDtypeStruct((4 * r, cols), dtype),
               compiler_params=pltpu.CompilerParams(dimension_semantics=("arbitrary",)))(me, shard, *extra)


def _allgather(fulls, *, name):
    n = len(fulls)

    def body(*refs):
        outs = refs[n:2 * n]
        send_sems, recv_sems = refs[2 * n:]
        x, y, c, chips = _place()
        sib = (x, y, 1 - c)
        me = 2 * x + y

        def rows(t, s, half):
            hrows = outs[t].shape[0] // 8
            return outs[t].at[pl.ds((2 * s + half) * hrows, hrows)]

        sent = []
        for t in range(n):
            for j, (cx, cy) in enumerate(chips):
                cp = _rcopy(rows(t, me, c), rows(t, me, c), send_sems.at[6 * t + j], recv_sems.at[6 * t + j], (cx, cy, c))
                cp.start()
                sent.append(cp)
        for t in range(n):
            for j, (cx, cy) in enumerate(chips):
                landed = rows(t, 2 * cx + cy, c)
                _rcopy(landed, landed, send_sems.at[6 * t + j], recv_sems.at[6 * t + j], (cx, cy, c)).wait_recv()
                cp = _rcopy(landed, landed, send_sems.at[6 * t + 3 + j], recv_sems.at[6 * t + 3 + j], sib)
                cp.start()
                sent.append(cp)
        for t in range(n):
            for j, (cx, cy) in enumerate(chips):
                other = rows(t, 2 * cx + cy, 1 - c)
                _rcopy(other, other, send_sems.at[6 * t + 3 + j], recv_sems.at[6 * t + 3 + j], sib).wait_recv()
        for cp in sent:
            cp.wait_send()

    return _pc(body, name=name, in_specs=[ANY] * n, out_specs=[ANY] * n,
               out_shape=[jax.ShapeDtypeStruct(f.shape, f.dtype) for f in fulls],
               input_output_aliases={t: t for t in range(n)},
               scratch_shapes=[pltpu.SemaphoreType.DMA((6 * n,)), pltpu.SemaphoreType.DMA((6 * n,))])(*fulls)


HBM = pl.BlockSpec(memory_space=pltpu.HBM)
SEM = pl.BlockSpec(memory_space=pltpu.SEMAPHORE)
EFFECT = pltpu.SideEffectType.DATAFLOW_SIDE_EFFECTING
TOKEN = jax.ShapeDtypeStruct((8, LANE), F32)


def _in_hbm(a):
    return pltpu.with_memory_space_constraint(a, pltpu.HBM)


def _half_rows(ref, s, half):
    hrows = ref.shape[0] // 8
    return ref.at[pl.ds((2 * s + half) * hrows, hrows)]


def _split_start(bufs, copies, n_copies, *, name, before=None):
    n = len(bufs)
    extra = [] if before is None else [before]

    def body(*refs):
        send_sems, recv_sems, token = refs[n + len(extra)], refs[n + len(extra) + 1], refs[-1]
        for k, (src, dst, dev) in enumerate(copies(refs[:n])):
            _rcopy(src, dst, send_sems.at[k], recv_sems.at[k], dev).start()
        token[...] = jnp.zeros_like(token)

    res = _pc(body, name=name, in_specs=[HBM] * n + [ANY] * len(extra),
              out_specs=[SEM, SEM] + [HBM] * n + [pl.BlockSpec(memory_space=pltpu.VMEM)],
              out_shape=[pltpu.SemaphoreType.DMA((n_copies,)), pltpu.SemaphoreType.DMA((n_copies,))]
              + [pltpu.HBM(b.shape, b.dtype) for b in bufs] + [TOKEN],
              input_output_aliases={t: 2 + t for t in range(n)},
              compiler_params=pltpu.CompilerParams(has_side_effects=EFFECT))(*[_in_hbm(b) for b in bufs], *extra)
    return res[0], res[1], list(res[2:2 + n]), res[-1]


def _split_wait(send_sems, recv_sems, bufs, copies, after, *, name):
    n = len(bufs)

    def body(*refs):
        send_ref, recv_ref = refs[n], refs[n + 1]
        for k, (src, dst, dev) in enumerate(copies(refs[:n])):
            cp = _rcopy(src, dst, send_ref.at[k], recv_ref.at[k], dev)
            cp.wait_send()
            cp.wait_recv()

    return _pc(body, name=name, in_specs=[HBM] * n + [SEM, SEM, ANY], out_specs=[HBM] * n,
               out_shape=[pltpu.HBM(b.shape, b.dtype) for b in bufs],
               input_output_aliases={t: t for t in range(n)},
               compiler_params=pltpu.CompilerParams(has_side_effects=EFFECT))(*bufs, send_sems, recv_sems, after)


def _allgather_ici_copies(refs):
    x, y, c, chips = _place()
    return [(_half_rows(r, 2 * x + y, c), _half_rows(r, 2 * x + y, c), (cx, cy, c)) for r in refs for cx, cy in chips]


def _rs_chips_copies(refs):
    x, y, c, chips = _place()
    n = len(refs) // 2
    return [(refs[t].at[2 * cx + cy], refs[n + t].at[j], (cx, cy, c)) for t in range(n) for j, (cx, cy) in enumerate(chips)]


def _rs_sibling_copies(refs):
    x, y, c, _ = _place()
    n = len(refs) // 2
    out = []
    for t in range(n):
        h = refs[t].shape[0] // 8
        out += [(refs[t].at[pl.ds((2 * s + 1 - c) * h, h)], refs[n + t].at[s], (x, y, 1 - c)) for s in range(4)]
    return out


def _allgather_forward(fulls, *, name):
    n = len(fulls)

    def body(*refs):
        outs = refs[n:2 * n]
        send_sems, recv_sems = refs[2 * n:]
        x, y, c, chips = _place()
        sent = []
        for t in range(n):
            for j, (cx, cy) in enumerate(chips):
                landed = _half_rows(outs[t], 2 * cx + cy, c)
                cp = _rcopy(landed, landed, send_sems.at[3 * t + j], recv_sems.at[3 * t + j], (x, y, 1 - c))
                cp.start()
                sent.append(cp)
        for t in range(n):
            for j, (cx, cy) in enumerate(chips):
                other = _half_rows(outs[t], 2 * cx + cy, 1 - c)
                _rcopy(other, other, send_sems.at[3 * t + j], recv_sems.at[3 * t + j], (x, y, 1 - c)).wait_recv()
        for cp in sent:
            cp.wait_send()

    return _pc(body, name=name, in_specs=[ANY] * n, out_specs=[ANY] * n,
               out_shape=[jax.ShapeDtypeStruct(f.shape, f.dtype) for f in fulls],
               input_output_aliases={t: t for t in range(n)},
               scratch_shapes=[pltpu.SemaphoreType.DMA((3 * n,)), pltpu.SemaphoreType.DMA((3 * n,))])(*fulls)


def _rs_sibling(grads, *, name):
    n = len(grads)

    def body(*refs):
        ins, outs = refs[:n], refs[n:2 * n]
        send_sems, recv_sems = refs[2 * n:]
        x, y, c, _ = _place()
        cps = []
        for t in range(n):
            h = ins[t].shape[0] // 8
            for s in range(4):
                cp = _rcopy(ins[t].at[pl.ds((2 * s + 1 - c) * h, h)], outs[t].at[s], send_sems.at[4 * t + s],
                            recv_sems.at[4 * t + s], (x, y, 1 - c))
                cp.start()
                cps.append(cp)
        for cp in cps:
            cp.wait()

    return _pc(body, name=name, in_specs=[ANY] * n, out_specs=[ANY] * n,
               out_shape=[jax.ShapeDtypeStruct((4, g.shape[0] // 8, g.shape[1]), g.dtype) for g in grads],
               scratch_shapes=[pltpu.SemaphoreType.DMA((4 * n,)), pltpu.SemaphoreType.DMA((4 * n,))])(*grads)


def _rs_chips(sends, *, name):
    n = len(sends)

    def body(*refs):
        s_refs, b_refs = refs[:n], refs[n:2 * n]
        send_sems, recv_sems = refs[2 * n:]
        x, y, c, chips = _place()
        cps = []
        for t in range(n):
            for j, (cx, cy) in enumerate(chips):
                cp = _rcopy(s_refs[t].at[2 * cx + cy], b_refs[t].at[j], send_sems.at[3 * t + j], recv_sems.at[3 * t + j],
                            (cx, cy, c))
                cp.start()
                cps.append(cp)
        for cp in cps:
            cp.wait()

    return _pc(body, name=name, in_specs=[ANY] * n, out_specs=[ANY] * n,
               out_shape=[jax.ShapeDtypeStruct((3,) + s.shape[1:], s.dtype) for s in sends],
               scratch_shapes=[pltpu.SemaphoreType.DMA((3 * n,)), pltpu.SemaphoreType.DMA((3 * n,))])(*sends)


def _rs_final(fulls, *, name):
    n = len(fulls)

    def body(*refs):
        outs = refs[n:2 * n]
        send_sems, recv_sems = refs[2 * n:]
        x, y, c, _ = _place()
        cps = []
        for t in range(n):
            cp = _rcopy(outs[t].at[c], outs[t].at[c], send_sems.at[t], recv_sems.at[t], (x, y, 1 - c))
            cp.start()
            cps.append(cp)
        for cp in cps:
            cp.wait()

    return _pc(body, name=name, in_specs=[ANY] * n, out_specs=[ANY] * n,
               out_shape=[jax.ShapeDtypeStruct(f.shape, f.dtype) for f in fulls],
               input_output_aliases={t: t for t in range(n)},
               scratch_shapes=[pltpu.SemaphoreType.DMA((n,)), pltpu.SemaphoreType.DMA((n,))])(*fulls)


def _add_halves(g, a, send_dtype, *, name):
    _, h, cols = a.shape
    th = _row_tile(h, cols)
    g4 = g.reshape(4, 2, h, cols)
    idx = jnp.stack([lax.axis_index("c"), 2 * lax.axis_index("x") + lax.axis_index("y")]).astype(jnp.int32)

    def shard(k, ir):
        return (ir[1] + 1 + k) % 4

    def body(idx_ref, g_ref, a_ref, p_ref, s_ref):
        v = g_ref[...].astype(F32) + a_ref[...].astype(F32)
        s_ref[...] = v.astype(send_dtype)

        @pl.when(pl.program_id(1) == 3)
        def _():
            p_ref[...] = v

    return _pc(body, name=name,
               grid_spec=pltpu.PrefetchScalarGridSpec(
                   num_scalar_prefetch=1, grid=(h // th, 4),
                   in_specs=[pl.BlockSpec((None, None, th, cols), lambda i, k, ir: (shard(k, ir), ir[0], i, 0)),
                             pl.BlockSpec((None, th, cols), lambda i, k, ir: (shard(k, ir), i, 0))],
                   out_specs=[pl.BlockSpec((th, cols), lambda i, k, ir: (i, 0)),
                              pl.BlockSpec((None, th, cols), lambda i, k, ir: (shard(k, ir), i, 0))]),
               out_shape=[jax.ShapeDtypeStruct((h, cols), F32), jax.ShapeDtypeStruct(a.shape, send_dtype)],
               compiler_params=pltpu.CompilerParams(dimension_semantics=("arbitrary", "arbitrary")))(idx, g4, a)


def _add_chips(p, b, *, name, order=None):
    h, cols = p.shape
    th = _row_tile(h, cols)
    idx = lax.axis_index("c").astype(jnp.int32).reshape(1)
    extra = [] if order is None else [order]

    def body(idx_ref, p_ref, b_ref, *rest):
        r_ref = rest[-1]
        r_ref[...] = ((p_ref[...] + b_ref[0].astype(F32)) + b_ref[1].astype(F32)) + b_ref[2].astype(F32)

    return _pc(body, name=name,
               grid_spec=pltpu.PrefetchScalarGridSpec(
                   num_scalar_prefetch=1, grid=(h // th,),
                   in_specs=[pl.BlockSpec((th, cols), lambda i, ir: (i, 0)),
                             pl.BlockSpec((3, th, cols), lambda i, ir: (0, i, 0))] + [ANY] * len(extra),
                   out_specs=pl.BlockSpec((None, th, cols), lambda i, ir: (ir[0], i, 0))),
               out_shape=jax.ShapeDtypeStruct((2, h, cols), F32),
               compiler_params=pltpu.CompilerParams(dimension_semantics=("arbitrary",)))(idx, p, b, *extra)


def _add_halves_all(grads, recv, send_dtypes, tag):
    parts, sends = [], []
    for t, (g, a) in enumerate(zip(grads, recv)):
        p, s = _add_halves(g, a, send_dtypes[t], name=f"rs_add_halves_{tag}{t}")
        parts.append(p)
        sends.append(s)
    return parts, sends


def _rs_finish(parts, others, tag, order=None):
    halves = [_add_chips(p, b, order=order, name=f"rs_add_chips_{tag}{t}") for t, (p, b) in enumerate(zip(parts, others))]
    full = _rs_final(halves, name=f"rs_final_{tag}")
    return [f.reshape(-1, f.shape[-1]) for f in full]


def _s5_discretize(lam_re, lam_im, log_dt, b_re, b_im):
    lam = lax.complex(lam_re, lam_im)
    dt = jnp.exp(log_dt)[:, None]
    lam_bar = jnp.exp(lam * dt)
    b_bar = ((lam_bar - 1.0) / lam)[..., None] * lax.complex(b_re, b_im)
    return jnp.real(lam_bar), jnp.imag(lam_bar), jnp.real(b_bar), jnp.imag(b_bar)


def _lanes_from_gp(re, im, cfg):
    v = jnp.stack([re, im]).reshape(2, cfg.NB, GROUPS_PER_BLOCK, SSM_STATE)
    return jnp.transpose(v, (1, 0, 2, 3)).reshape(1, cfg.NL)


def _gp_from_lanes(v, cfg):
    v = jnp.transpose(v.reshape(cfg.NB, 2, GROUPS_PER_BLOCK, SSM_STATE), (1, 0, 2, 3)).reshape(2, cfg.G, SSM_STATE)
    return v[0], v[1]


def _bb_band(bb_re, bb_im, cfg):
    eye = jnp.eye(GROUPS_PER_BLOCK, dtype=F32)
    bb = jnp.stack([bb_re, bb_im]).reshape(2, cfg.NB, GROUPS_PER_BLOCK, SSM_STATE, SSM_GROUP)
    return jnp.einsum('rjgpc,gh->jgcrhp', bb, eye).reshape(cfg.DS, 2 * GROUPS_PER_BLOCK * SSM_STATE)


def _bb_from_band(m, cfg):
    eye = jnp.eye(GROUPS_PER_BLOCK, dtype=F32)
    m = m.reshape(cfg.NB, GROUPS_PER_BLOCK, SSM_GROUP, 2, GROUPS_PER_BLOCK, SSM_STATE)
    v = jnp.einsum('jgcrhp,gh->rjgpc', m, eye).reshape(2, cfg.G, SSM_STATE, SSM_GROUP)
    return v[0], v[1]


def _cc_band(c_re, c_im, cfg):
    eye = jnp.eye(GROUPS_PER_BLOCK, dtype=F32)
    cc = jnp.stack([c_re, -c_im]).reshape(2, cfg.NB, GROUPS_PER_BLOCK, SSM_GROUP, SSM_STATE)
    return jnp.einsum('rjgcp,gh->jrhpgc', cc, eye).reshape(cfg.NL, GROUPS_PER_BLOCK * SSM_GROUP)


def _cc_from_band(m, cfg):
    eye = jnp.eye(GROUPS_PER_BLOCK, dtype=F32)
    m = m.reshape(cfg.NB, 2, GROUPS_PER_BLOCK, SSM_STATE, GROUPS_PER_BLOCK, SSM_GROUP)
    v = jnp.einsum('jrhpgc,gh->rjgcp', m, eye).reshape(2, cfg.G, SSM_GROUP, SSM_STATE)
    return v[0], -v[1]


PACK_COLS = 512
PACK_ROW_ALIGN = 64


def _pack(arrs):
    flat = jnp.concatenate([a.reshape(-1).astype(F32) for a in arrs])
    unit = PACK_COLS * PACK_ROW_ALIGN
    total = -(-flat.shape[0] // unit) * unit
    return jnp.pad(flat, (0, total - flat.shape[0])).reshape(-1, PACK_COLS)


def _unpack(p, shapes):
    flat = p.reshape(-1)
    out, off = [], 0
    for shp in shapes:
        size = math.prod(shp)
        out.append(flat[off:off + size].reshape(shp))
        off += size
    return out


def _adamw(w, g, m, v, *, name, emit_grad=False):
    c1 = 1.0 / (1.0 - ADAM_B1 ** ADAM_STEP)
    c2 = 1.0 / (1.0 - ADAM_B2 ** ADAM_STEP)

    rows, cols = w.shape
    tc = _tile(cols, 512)
    tm = _tile(rows, max(8, 3 * DT_F32_BLOCK_BYTES // (8 * tc)), 8)
    n_out = 4 if emit_grad else 3

    def body(w_ref, g_ref, m_ref, v_ref, *o_refs):
        gv = g_ref[...]
        mn = ADAM_B1 * m_ref[...] + (1.0 - ADAM_B1) * gv
        vn = ADAM_B2 * v_ref[...] + (1.0 - ADAM_B2) * (gv * gv)
        delta = -ADAM_LR * ((mn * c1) / (jnp.sqrt(vn * c2) + ADAM_EPS) + ADAM_WD * w_ref[...])
        for o_ref, val in zip(o_refs, ((gv, delta, mn, vn) if emit_grad else (delta, mn, vn))):
            o_ref[...] = val

    blk = pl.BlockSpec((tm, tc), lambda i, j: (i, j))
    return _pc(body, name=name, grid=(rows // tm, cols // tc), in_specs=[blk] * 4, out_specs=[blk] * n_out,
               out_shape=[jax.ShapeDtypeStruct((rows, cols), F32)] * n_out,
               compiler_params=pltpu.CompilerParams(dimension_semantics=("parallel", "parallel")))(w, g, m, v)


def _to_comm_layout(name, w, cfg):
    w = w[0]
    if name == 'w_in':
        return jnp.pad(w, ((0, 0), (0, cfg.DINP - cfg.DIN)))
    if name == 'w_q_b':
        hs = w.shape[1] // (QK_NOPE + QK_ROPE)
        wt = w.T.reshape(hs, QK_NOPE + QK_ROPE, cfg.QL)
        return jnp.pad(wt, ((0, 0), (0, HEAD_SLOT - QK_NOPE - QK_ROPE), (0, 0))).reshape(hs * HEAD_SLOT, cfg.QL)
    if name == 'w_kv_b':
        return w.T
    if name == 'w_up':
        wt = w.T.reshape(2, cfg.F // 4, cfg.D)
        return jnp.pad(wt, ((0, 0), (0, cfg.FQ - cfg.F // 4), (0, 0))).reshape(2 * cfg.FQ, cfg.D)
    if name == 'w_down':
        return jnp.pad(w, ((0, cfg.FQ - cfg.F // 4), (0, 0)))
    return w


def _from_comm_layout(name, g, cfg):
    if name == 'w_in':
        g = g[:, :cfg.DIN]
    elif name == 'w_q_b':
        hs = g.shape[0] // HEAD_SLOT
        g = g.reshape(hs, HEAD_SLOT, cfg.QL)[:, :QK_NOPE + QK_ROPE].reshape(hs * (QK_NOPE + QK_ROPE), cfg.QL).T
    elif name == 'w_kv_b':
        g = g.T
    elif name == 'w_up':
        g = g.reshape(2, cfg.FQ, cfg.D)[:, :cfg.F // 4].reshape(cfg.F // 2, cfg.D).T
    elif name == 'w_down':
        g = g[:cfg.F // 4]
    return g[None]


def _ff_pad(v, cfg):
    k = v.shape[0]
    return jnp.pad(v.reshape(k, 4, cfg.F // 4), ((0, 0), (0, 0), (0, cfg.FQ - cfg.F // 4))).reshape(k, cfg.FP)


def _ff_unpad(v, cfg):
    k = v.shape[0]
    return v.reshape(k, 4, cfg.FQ)[:, :, :cfg.F // 4].reshape(k, cfg.F)


def _step(cfg, w, m, v, x, loss_target):
    lp, d, ds, nl = cfg.LP, cfg.D, cfg.DS, cfg.NL
    xi, yi = lax.axis_index("x"), lax.axis_index("y")
    me = 2 * xi + yi

    def place(n, order=None):
        return _place_shard(_to_comm_layout(n, w[n], cfg), BF16, order=order, name=f"place_{n}")

    first = [place('w_in'), _place_shard(w['meta_tokens'], F32, name="place_meta")]
    f_send, f_recv, f_flying, f_token = _split_start(first, _allgather_ici_copies, 6, name="allgather_first_start")
    conv_w_shard = jnp.pad(w['conv_w'][0], ((0, ROW_ALIGN - 3), (0, cfg.FQ - cfg.F // 4)))
    placed = [None] + [place(n, f_token) for n in BIG[1:]]
    placed += [None, _place_shard(conv_w_shard, F32, order=f_token, name="place_conv_w")]
    f_landed = _split_wait(f_send, f_recv, f_flying, _allgather_ici_copies, placed[6], name="allgather_first_wait")
    w_in, meta_full = _allgather_forward(f_landed, name="allgather_first_forward")
    meta = jnp.transpose(meta_full.reshape(4, N_META, d // 4), (1, 0, 2)).reshape(N_META, d)
    conv_b = _ff_pad(w['conv_b'], cfg)
    mid = placed[1:5] + [placed[8]]
    mid_send, mid_recv, mid_flying, mid_token = _split_start(mid, _allgather_ici_copies, 3 * len(mid), before=meta_full,
                                                             name="allgather_mid_start")
    ffn_send, ffn_recv, ffn_flying, ffn_token = _split_start(placed[5:7], _allgather_ici_copies, 6, before=mid_token,
                                                             name="allgather_ffn_start")
    mix_norm = w['mix_norm'] + (mid_token[0:1, 0:1] + ffn_token[0:1, 0:1])

    pos = (jnp.arange(lp, dtype=jnp.int32) - PAD).astype(F32)
    inv_freq = 1.0 / (ROPE_BASE ** (jnp.arange(0, QK_ROPE, 2, dtype=F32) / QK_ROPE))
    ang = pos[:, None] * inv_freq[None, :]
    zpad = jnp.zeros((lp, LANE - QK_ROPE), F32)
    cos_t = jnp.concatenate([jnp.cos(ang), jnp.cos(ang), zpad], axis=1)
    sin_t = jnp.concatenate([jnp.sin(ang), jnp.sin(ang), zpad], axis=1)

    s5_in = (w['lam_re'][0], w['lam_im'][0], w['log_dt'][0], w['b_re'][0], w['b_im'][0])
    (a_re, a_im, bb_re, bb_im), s5_vjp = jax.vjp(_s5_discretize, *s5_in)
    lam_dt = lax.complex(s5_in[0], s5_in[1]) * jnp.exp(s5_in[2])[:, None]
    a_pow = jnp.exp(jnp.arange(1, 9, dtype=F32)[:, None, None] * lam_dt[None])
    r8 = jnp.arange(8)
    step_f = jnp.stack([jnp.where((r8 >= k)[:, None, None], a_pow[k - 1][None], 0.0) for k in (1, 2, 4)]).reshape(24, cfg.G, -1)
    step_b = jnp.stack([jnp.where((r8 < 8 - k)[:, None, None], a_pow[k - 1][None], 0.0) for k in (1, 2, 4)]).reshape(24, cfg.G, -1)
    rows_f = jnp.concatenate([a_pow, step_f])
    rows_b = jnp.conj(jnp.concatenate([a_pow[::-1], step_b]))

    def lane_rows(t):
        v = jnp.stack([jnp.real(t), jnp.imag(t)], axis=1).reshape(t.shape[0], 2, cfg.NB, GROUPS_PER_BLOCK, SSM_STATE)
        return jnp.transpose(v, (0, 2, 1, 3, 4)).reshape(t.shape[0], cfg.NL)

    pw_fwd, pw_bwd = lane_rows(rows_f), lane_rows(rows_b)
    bb_band = _bb_band(bb_re, bb_im, cfg).astype(BF16)
    cc_band = _cc_band(w['c_re'][0], w['c_im'][0], cfg).astype(BF16)
    d_skip, b_glu = w['d_skip'], w['b_glu']

    h0 = jnp.concatenate([jnp.zeros((PAD, d), F32), meta, x[0]], axis=0)
    xn = _rms_fwd(h0, mix_norm, name="rms_mix")
    z = _mm(xn, w_in, name="mm_in", tn=_tile(cfg.DINP, 640))
    u = (z, ds, 0)
    q_a = (z, cfg.QL, ds // cfg.QL)
    kv_a = (z, cfg.KVL, (ds + cfg.QL) // cfg.KVL)
    k_pe = (z, LANE, (ds + cfg.QL + cfg.KVL) // LANE)

    hs, yc = _s5_fwd(z, bb_band, cc_band, pw_fwd, cfg, name="s5_fwd")

    def s5_y(ycv, uv, dk):
        return ycv + dk * uv

    gl = _ew(lambda rid, ycv, uv, dk: jax.nn.gelu(s5_y(ycv, uv, dk)), [yc, u], [d_skip], [(ds, BF16)], name="s5_gelu")[0]
    mid_landed = _split_wait(mid_send, mid_recv, mid_flying, _allgather_ici_copies, gl, name="allgather_mid_wait")
    w_glu, w_qt, w_kvt, w_out, conv_full = _allgather_forward(mid_landed, name="allgather_mid_forward")
    conv_w = jnp.transpose(conv_full.reshape(4, ROW_ALIGN, cfg.FQ)[:, :3], (1, 0, 2)).reshape(3, cfg.FP)
    tg = _mm(gl, w_glu, name="mm_glu")
    ya = _ew(lambda rid, ycv, uv, tv, dk, bg: jax.nn.gelu(s5_y(ycv, uv, dk)) * jax.nn.sigmoid(tv + bg),
             [yc, u, tg], [d_skip, b_glu], [(ds, F32)], name="s5_glu")[0]

    qn = _rms_fwd(q_a, w['q_a_norm'], name="rms_q")
    kvn = _rms_fwd(kv_a, w['kv_a_norm'], name="rms_kv")
    q_raw = _mm(qn, w_qt, tb=True, name="mm_q")
    qx = _ew(_rope_heads(_rope, cfg.H), [q_raw, cos_t, sin_t], [], [(cfg.H * HEAD_SLOT, BF16)], name="rope_q")[0]
    kv = _mm(kvn, w_kvt, tb=True, out_dtype=BF16, name="mm_kv")
    kr = _ew(lambda rid, kp, cs, sn: _rope(kp, cs, sn), [k_pe, cos_t, sin_t], [], [(LANE, BF16)], name="rope_k")[0]
    o, lse = _attn_fwd(qx, kv, kr, cfg, name="attn_fwd")

    def norm2(rid, yav, ov, gs, ga):
        return jnp.concatenate([_rms_parts(yav, gs)[0] * gs, _rms_parts(ov, ga)[0] * ga], axis=1)

    yn = _ew(norm2, [ya, o], [w['out_norm_ssm'], w['out_norm_attn']], [(cfg.DMIX, BF16)], name="rms_out")[0]
    h1 = _mm(yn, w_out, res=h0, name="mm_out")
    xn2 = _rms_fwd(h1, w['ffn_norm'], name="rms_ffn")
    ffn_landed = _split_wait(ffn_send, ffn_recv, ffn_flying, _allgather_ici_copies, xn2, name="allgather_ffn_wait")
    w_upt, w_down = _allgather_forward(ffn_landed, name="allgather_ffn_forward")
    up = _mm(xn2, w_upt, tb=True, out_dtype=BF16, name="mm_up")
    act = _conv_fwd(up, conv_w, conv_b, name="conv_fwd")
    h2 = _mm(act, w_down, res=h1, tm=_tile(lp, 544, ROW_ALIGN), name="mm_down")

    g_final = w['final_norm'].reshape(1, d)

    def head(rid, hv, tv, gv):
        xhat, r = _rms_parts(hv, gv)
        valid = rid >= PAD + N_META
        diff = jnp.where(valid, xhat * gv - tv, 0.0)
        dout = diff * (1.0 / d)
        dxhat = dout * gv
        dx = r * (dxhat - xhat * jnp.mean(dxhat * xhat, axis=-1, keepdims=True))
        return dx, dx, dout * xhat, 0.5 * diff * dout

    dh2, dh2_b, dg_final, loss_cols = _ew(head, [h2, (loss_target[0], d, 0, SKIP)], [g_final], [(d, F32), (d, BF16)], [d, d],
                                          tm=PAD + N_META, name="loss_head")
    loss = lax.psum(jnp.sum(loss_cols), ("x", "y", "c"))

    dact = _mm(dh2_b, w_down, tb=True, out_dtype=BF16, name="mm_dact")
    dw_down = _mm(act, dh2_b, ta=True, tn=d, tm=512, out_dtype=BF16, name="mm_dw_down")

    def sibling_start(g, tag):
        land = lax.empty((4, g.shape[0] // 8, g.shape[1]), g.dtype)
        return _split_start([g, land], _rs_sibling_copies, 4, name=f"rs_sibling_{tag}_start")

    dn_send, dn_recv, dn_flying, dn_token = sibling_start(dw_down, "down")
    dup, dconv_w, dconv_b = _conv_bwd(up, dact, conv_w, conv_b + dn_token[0:1, 0:1], name="conv_bwd")
    tk_up, tm_up = _tile(cfg.FP, 1408), _tile(cfg.FP, 512)
    dw_upt = _mm(dup, xn2, ta=True, dims=(2 * cfg.FP, d, lp), tn=d, tm=tm_up, a_lead=True, out_dtype=BF16, name="mm_dw_up",
                 a_idx=lambda i, j, k: (i // (cfg.FP // tm_up), 0, i % (cfg.FP // tm_up)))
    up_send, up_recv, up_flying, up_token = sibling_start(dw_upt, "up")
    dxn2 = _mm(dup, w_upt, dims=(lp, d, 2 * cfg.FP), tk=tk_up, tn=1024, a_lead=True, name="mm_dxn2",
               a_idx=lambda i, j, k: (k // (cfg.FP // tk_up), i, k % (cfg.FP // tk_up)))
    dh1, dh1_b, dg_ffn = _rms_bwd(h1, w['ffn_norm'] + up_token[0:1, 0:1], dxn2, res=dh2, mask=True, with_bf16=True,
                                  name="rms_ffn_bwd")

    dyn = _mm(dh1_b, w_out, tb=True, name="mm_dyn")
    dw_out = _mm(yn, dh1_b, ta=True, tn=d, tm=512, name="mm_dw_out")
    up_done = _split_wait(up_send, up_recv, up_flying, _rs_sibling_copies, dw_out, name="rs_sibling_up_wait")
    dn_done = _split_wait(dn_send, dn_recv, dn_flying, _rs_sibling_copies, dw_out, name="rs_sibling_down_wait")
    early_parts, early_sends = _add_halves_all([up_done[0], dn_done[0]], [up_done[1], dn_done[1]], [BF16] * 2, "early")
    chip_lands = [lax.empty((3,) + s.shape[1:], s.dtype) for s in early_sends]
    ch_send, ch_recv, ch_flying, ch_token = _split_start(early_sends + chip_lands, _rs_chips_copies, 6,
                                                         name="rs_chips_early_start")
    dya, dg_ssm = _rms_bwd(ya, w['out_norm_ssm'] + ch_token[0:1, 0:1], (dyn, ds, 0), name="rms_ssm_bwd")
    do, dg_attn = _rms_bwd(o, w['out_norm_attn'], (dyn, cfg.DATTN, ds // cfg.DATTN), name="rms_attn_bwd")

    dqx, dkv, dkr = _attn_bwd(qx, kv, kr, o, lse, do, cfg, name="attn_bwd")
    dq_raw = _ew(_rope_heads(_unrope, cfg.H), [dqx, cos_t, sin_t], [], [(cfg.H * HEAD_SLOT, BF16)], name="unrope_q")[0]
    dk_pe = _ew(lambda rid, dk, cs, sn: _unrope(dk, cs, sn), [dkr, cos_t, sin_t], [], [(LANE, F32)], name="unrope_k")[0]
    dqn = _mm(dq_raw, w_qt, name="mm_dqn")
    dw_qt = _mm(dq_raw, qn, ta=True, tm=512, name="mm_dw_q")
    dkvn = _mm(dkv, w_kvt, name="mm_dkvn")
    dw_kvt = _mm(dkv, kvn, ta=True, tm=512, name="mm_dw_kv")
    dq_a, dg_q = _rms_bwd(q_a, w['q_a_norm'], dqn, name="rms_q_bwd")
    dkv_a, dg_kv = _rms_bwd(kv_a, w['kv_a_norm'], dkvn, name="rms_kv_bwd")

    def glu_bwd(rid, ycv, uv, tv, dyav, dk, bg):
        gelu = jax.nn.gelu(s5_y(ycv, uv, dk))
        sg = jax.nn.sigmoid(tv + bg)
        dt = dyav * gelu * sg * (1.0 - sg)
        return dt, dyav * sg, dt

    dt_b, dgl1, db_glu = _ew(glu_bwd, [yc, u, tg, dya], [d_skip, b_glu], [(ds, BF16), (ds, F32)], [ds], name="s5_glu_bwd")
    dgl = _mm(dt_b, w_glu, tb=True, res=dgl1, name="mm_dgl")
    dw_glu = _mm(gl, dt_b, ta=True, tm=512, name="mm_dw_glu")

    def gelu_bwd(rid, ycv, uv, dglv, dk):
        _, vjp = jax.vjp(jax.nn.gelu, s5_y(ycv, uv, dk))
        dy = vjp(dglv)[0]
        return dy, dy * dk, dy * uv

    mid_grads = [dw_out, dw_glu, dw_qt, dw_kvt]
    mid_lands = [lax.empty((4, g.shape[0] // 8, g.shape[1]), g.dtype) for g in mid_grads]
    ms_send, ms_recv, ms_flying, ms_token = _split_start(mid_grads + mid_lands, _rs_sibling_copies, 4 * len(mid_grads),
                                                         name="rs_sibling_mid_start")
    dy_b, du_skip, dd_skip = _ew(gelu_bwd, [yc, u, dgl], [d_skip + ms_token[0:1, 0:1]], [(ds, BF16), (ds, F32)], [ds],
                                 name="s5_gelu_bwd")
    ms_done = _split_wait(ms_send, ms_recv, ms_flying, _rs_sibling_copies, dy_b, name="rs_sibling_mid_wait")
    mid_parts, mid_sends = _add_halves_all(ms_done[:4], ms_done[4:], [BF16] * 4, "mid")
    mid_chip_lands = [lax.empty((3,) + s.shape[1:], s.dtype) for s in mid_sends]
    mc_send, mc_recv, mc_flying, mc_token = _split_start(mid_sends + mid_chip_lands, _rs_chips_copies, 3 * len(mid_sends),
                                                         name="rs_chips_mid_start")
    du, dbb_band, dcc_band, da_l = _s5_bwd(dy_b, hs, z, bb_band, cc_band, pw_bwd + mc_token[0:1, 0:1], du_skip, cfg,
                                           name="s5_bwd")

    dz = jnp.concatenate([du, dq_a, dkv_a, dk_pe], axis=1).astype(BF16)
    dxn = _mm(dz, w_in, tb=True, name="mm_dxn")
    dw_in = _mm(xn, dz, ta=True, tm=512, tn=_tile(cfg.DINP, 1024), name="mm_dw_in")
    def mix_bwd(rid, xv, dyv, resv, gv):
        dx, dg = _rms_bwd_block(xv, gv, dyv)
        dx = dx + resv
        return dx, dx, dg

    grad_x, dh0_head, dg_mix = _ew(mix_bwd, [h0, dxn, dh1], [mix_norm], [(d, F32, SKIP), (d, F32, FIRST)], [d],
                                   tm=PAD + N_META, name="rms_mix_bwd")
    grad_x = grad_x[None]

    da_re, da_im = _gp_from_lanes(da_l, cfg)
    dbb_re, dbb_im = _bb_from_band(dbb_band, cfg)
    dlam_re, dlam_im, dlog_dt, db_re, db_im = s5_vjp((da_re, da_im, dbb_re, dbb_im))
    dc_re, dc_im = _cc_from_band(dcc_band, cfg)
    local_small = {
        'meta_tokens': dh0_head[PAD:], 'mix_norm': dg_mix, 'lam_re': dlam_re, 'lam_im': dlam_im, 'log_dt': dlog_dt,
        'b_re': db_re, 'b_im': db_im, 'c_re': dc_re, 'c_im': dc_im, 'd_skip': dd_skip, 'b_glu': db_glu, 'q_a_norm': dg_q,
        'kv_a_norm': dg_kv, 'out_norm_ssm': dg_ssm, 'out_norm_attn': dg_attn, 'ffn_norm': dg_ffn,
        'conv_w': _ff_unpad(dconv_w, cfg), 'conv_b': _ff_unpad(dconv_b, cfg), 'final_norm': dg_final,
    }
    small_shapes = [local_small[n].shape for n in SMALL]

    small_pack = _pack([local_small[n] for n in SMALL])
    end_local = [dw_in, small_pack]
    end_recv = _rs_sibling(end_local, name="rs_sibling_end")
    end_parts, end_sends = _add_halves_all(end_local, end_recv, [BF16, F32], "end")
    end_lands = [lax.empty((3,) + s.shape[1:], s.dtype) for s in end_sends]
    ec_send, ec_recv, ec_flying, ec_token = _split_start(end_sends + end_lands, _rs_chips_copies, 3 * len(end_sends),
                                                         name="rs_chips_end_start")
    ch_done = _split_wait(ch_send, ch_recv, ch_flying, _rs_chips_copies, ec_token, name="rs_chips_early_wait")
    red_up, red_down = _rs_finish(early_parts, ch_done[2:], "early")

    delta, new_m, new_v, grads = {}, {}, {}, {}
    padded_rows = ('w_down',)

    def adamw_big(n, red):
        shp = w[n].shape
        w2, m2, v2 = [t.reshape(shp[-2], shp[-1]) for t in (w[n], m[n], v[n])]
        if n in padded_rows:
            g2, dl, mn, vn = _adamw(w2, red, m2, v2, emit_grad=True, name=f"adamw_{n}")
            grads[n] = g2.reshape(shp)
        else:
            grads[n] = _from_comm_layout(n, red, cfg)
            dl, mn, vn = _adamw(w2, grads[n].reshape(shp[-2], shp[-1]), m2, v2, name=f"adamw_{n}")
        delta[n], new_m[n], new_v[n] = dl.reshape(shp), mn.reshape(shp), vn.reshape(shp)

    adamw_big('w_up', red_up)
    adamw_big('w_down', red_down)
    mc_done = _split_wait(mc_send, mc_recv, mc_flying, _rs_chips_copies, delta['w_down'], name="rs_chips_mid_wait")
    ec_done = _split_wait(ec_send, ec_recv, ec_flying, _rs_chips_copies, mc_done[0], name="rs_chips_end_wait")
    red = _rs_finish(mid_parts + end_parts, list(mc_done[len(mid_sends):]) + list(ec_done[len(end_sends):]), "rest")
    small_full = _allgather([_place_shard(red[5], F32, name="place_small")], name="allgather_small")[0]
    small_sum = dict(zip(SMALL, _unpack(small_full, small_shapes)))
    for n, r in zip(['w_out', 'w_glu', 'w_q_b', 'w_kv_b', 'w_in'], red[:5]):
        adamw_big(n, r)

    for n in SMALL:
        g = small_sum[n]
        if n == 'meta_tokens':
            g = lax.dynamic_slice_in_dim(g, me * (d // 4), d // 4, axis=1)
        elif n == 'conv_w':
            g = lax.dynamic_slice_in_dim(g, me * (cfg.F // 4), cfg.F // 4, axis=1)[None]
        else:
            g = g.reshape(w[n].shape)
        grads[n] = g

    shapes = [w[n].shape for n in SMALL]
    packs = [_pack([src[n] for n in SMALL]) for src in (w, grads, m, v)]
    for dst, p in zip((delta, new_m, new_v), _adamw(*packs, name="adamw_small")):
        dst.update(zip(SMALL, _unpack(p, shapes)))

    return (loss, grad_x, *[grads[n] for n in WEIGHTS], *[delta[n] for n in WEIGHTS],
            *[new_m[n] for n in WEIGHTS], *[new_v[n] for n in WEIGHTS])


def kernel(x, meta_tokens, mix_norm, w_in, lam_re, lam_im, log_dt, b_re, b_im, c_re, c_im, d_skip, w_glu, b_glu, q_a_norm, w_q_b, kv_a_norm, w_kv_b, out_norm_ssm, out_norm_attn, w_out, ffn_norm, w_up, conv_w, conv_b, w_down, final_norm, loss_target, m_meta_tokens, m_mix_norm, m_w_in, m_lam_re, m_lam_im, m_log_dt, m_b_re, m_b_im, m_c_re, m_c_im, m_d_skip, m_w_glu, m_b_glu, m_q_a_norm, m_w_q_b, m_kv_a_norm, m_w_kv_b, m_out_norm_ssm, m_out_norm_attn, m_w_out, m_ffn_norm, m_w_up, m_conv_w, m_conv_b, m_w_down, m_final_norm, v_meta_tokens, v_mix_norm, v_w_in, v_lam_re, v_lam_im, v_log_dt, v_b_re, v_b_im, v_c_re, v_c_im, v_d_skip, v_w_glu, v_b_glu, v_q_a_norm, v_w_q_b, v_kv_a_norm, v_w_kv_b, v_out_norm_ssm, v_out_norm_attn, v_w_out, v_ffn_norm, v_w_up, v_conv_w, v_conv_b, v_w_down, v_final_norm):
    args = dict(locals())
    w = {n: args[n] for n in WEIGHTS}
    m = {n: args["m_" + n] for n in WEIGHTS}
    v = {n: args["v_" + n] for n in WEIGHTS}
    return _step(PROD, w, m, v, x, loss_target)
```

```python
import functools
import math
from typing import NamedTuple

import jax
import jax.numpy as jnp
from jax import lax
from jax.experimental import pallas as pl
from jax.experimental.pallas import tpu as pltpu

F32, BF16 = jnp.float32, jnp.bfloat16
MESH = pl.DeviceIdType.MESH
LANE = 128
ROW_ALIGN = 16
N_META = 16
PAD = 112
CHUNK = 64
SSM_GROUP = 16
SSM_STATE = 64
GROUPS_PER_BLOCK = 8
QK_NOPE, QK_ROPE, V_HEAD = 128, 64, 128
HEAD_SLOT = 256
ROPE_BASE = 10000.0
EPS = 1e-6
ADAM_LR, ADAM_B1, ADAM_B2, ADAM_EPS, ADAM_WD, ADAM_STEP = 0.001, 0.9, 0.999, 1e-08, 0.01, 10
DT_F32_BLOCK_BYTES = 1 << 20
SKIP, FIRST = "skip", "first"


class Cfg(NamedTuple):
    D: int
    S: int
    DS: int
    H: int
    QL: int
    KVL: int
    F: int

    @property
    def LP(self):
        return PAD + N_META + self.S

    @property
    def G(self):
        return self.DS // SSM_GROUP

    @property
    def NB(self):
        return self.G // GROUPS_PER_BLOCK

    @property
    def NL(self):
        return 2 * self.G * SSM_STATE

    @property
    def DATTN(self):
        return self.H * V_HEAD

    @property
    def DMIX(self):
        return self.DS + self.DATTN

    @property
    def DIN(self):
        return self.DS + self.QL + self.KVL + QK_ROPE

    @property
    def DINP(self):
        return self.DS + self.QL + self.KVL + LANE

    @property
    def FQ(self):
        return -(-(self.F // 4) // LANE) * LANE

    @property
    def FP(self):
        return 4 * self.FQ


PROD = Cfg(D=2048, S=2048, DS=1024, H=8, QL=512, KVL=256, F=5504)

WEIGHTS = ['meta_tokens', 'mix_norm', 'w_in', 'lam_re', 'lam_im', 'log_dt', 'b_re', 'b_im', 'c_re', 'c_im', 'd_skip',
           'w_glu', 'b_glu', 'q_a_norm', 'w_q_b', 'kv_a_norm', 'w_kv_b', 'out_norm_ssm', 'out_norm_attn', 'w_out',
           'ffn_norm', 'w_up', 'conv_w', 'conv_b', 'w_down', 'final_norm']
BIG = ['w_in', 'w_glu', 'w_q_b', 'w_kv_b', 'w_out', 'w_up', 'w_down']
SMALL = [n for n in WEIGHTS if n not in BIG]


def _pc(body, **kw):
    return pl.pallas_call(body, **kw)


def _tile(n, target, align=LANE):
    best = None
    d = align
    while d <= min(n, target):
        if n % d == 0:
            best = d
        d += align
    return best if best is not None else n


def _row_tile(rows, cols):
    return _tile(rows, max(ROW_ALIGN, DT_F32_BLOCK_BYTES // (4 * cols)), ROW_ALIGN)


def _mm(a, b, *, name, ta=False, tb=False, tm=None, tn=512, tk=None, out_dtype=F32, res=None,
        a_idx=None, b_idx=None, dims=None, a_lead=False):
    if dims is None:
        m, k = (a.shape[1], a.shape[0]) if ta else a.shape
        n = b.shape[0] if tb else b.shape[1]
    else:
        m, n, k = dims
    tm = _tile(m, tm or m, LANE if ta else ROW_ALIGN)
    tn = _tile(n, tn)
    tk = _tile(k, tk or k, ROW_ALIGN if (ta and not tb) else LANE)
    nm, nn, nk = m // tm, n // tn, k // tk
    a_idx = a_idx or ((lambda i, j, kk: (kk, i)) if ta else (lambda i, j, kk: (i, kk)))
    b_idx = b_idx or ((lambda i, j, kk: (j, kk)) if tb else (lambda i, j, kk: (kk, j)))
    dn = (((0 if ta else 1,), (1 if tb else 0,)), ((), ()))

    def body(*refs):
        a_ref, b_ref = refs[0], refs[1]
        r_ref = refs[2] if res is not None else None
        o_ref = refs[3] if res is not None else refs[2]
        d = lax.dot_general(a_ref[...].astype(BF16), b_ref[...].astype(BF16), dn, preferred_element_type=F32)

        def finish(r):
            if r_ref is not None:
                r = r + r_ref[...].astype(F32)
            o_ref[...] = r.astype(out_dtype)

        if nk == 1:
            finish(d)
        else:
            acc = refs[-1]
            kk = pl.program_id(2)

            @pl.when(kk == 0)
            def _():
                acc[...] = d

            @pl.when(kk > 0)
            def _():
                acc[...] += d

            @pl.when(kk == nk - 1)
            def _():
                finish(acc[...])

    a_blk = ((None,) if a_lead else ()) + ((tk, tm) if ta else (tm, tk))
    in_specs = [pl.BlockSpec(a_blk, a_idx), pl.BlockSpec((tn, tk) if tb else (tk, tn), b_idx)]
    args = [a, b]
    if res is not None:
        in_specs.append(pl.BlockSpec((tm, tn), lambda i, j, kk: (i, j)))
        args.append(res)
    return _pc(body, name=name, grid=(nm, nn, nk), in_specs=in_specs,
               out_specs=pl.BlockSpec((tm, tn), lambda i, j, kk: (i, j)),
               out_shape=jax.ShapeDtypeStruct((m, n), out_dtype),
               scratch_shapes=[pltpu.VMEM((tm, tn), F32)] if nk > 1 else [],
               compiler_params=pltpu.CompilerParams(dimension_semantics=("parallel", "parallel", "arbitrary")))(*args)


def _ew(fn, ins, vecs, outs, sums=(), *, name, tm=None):
    ins = [x if isinstance(x, tuple) else (x, x.shape[1], 0) for x in ins]
    ins = [x if len(x) == 4 else x + (None,) for x in ins]
    outs = [o if len(o) == 3 else o + (None,) for o in outs]
    rows = ins[0][0].shape[0]
    cmax = max([c for _, c, _, _ in ins] + [c for c, _, _ in outs])
    tm = tm or _row_tile(rows, cmax)
    n_in, n_vec, n_out, n_sum = len(ins), len(vecs), len(outs), len(sums)

    def body(*refs):
        i = pl.program_id(0)
        rid = i * tm + lax.broadcasted_iota(jnp.int32, (tm, 1), 0)
        vals = [r[...] for r in refs[:n_in + n_vec]]
        res = fn(rid, *vals)
        res = res if isinstance(res, (tuple, list)) else (res,)
        o_refs = refs[n_in + n_vec:]
        for o_ref, r, (_, _, mode) in zip(o_refs[:n_out], res[:n_out], outs):
            if mode == FIRST:
                @pl.when(i == 0)
                def _():
                    o_ref[...] = r.astype(o_ref.dtype)
            else:
                o_ref[...] = r.astype(o_ref.dtype)
        for o_ref, r in zip(o_refs[n_out:], res[n_out:]):
            part = jnp.sum(r.astype(F32), axis=0, keepdims=True)

            @pl.when(i == 0)
            def _():
                o_ref[...] = part

            @pl.when(i > 0)
            def _():
                o_ref[...] += part

    def row_idx(mode):
        if mode == SKIP:
            return lambda i, cb=0: (jnp.maximum(i - 1, 0), cb)
        if mode == FIRST:
            return lambda i, cb=0: (0, cb)
        return lambda i, cb=0: (i, cb)

    in_specs = [pl.BlockSpec((tm, c), functools.partial(row_idx(mode), cb=cb)) for _, c, cb, mode in ins]
    in_specs += [pl.BlockSpec(v.shape, functools.partial(lambda i, nd: (0,) * nd, nd=v.ndim)) for v in vecs]
    out_specs = [pl.BlockSpec((tm, c), row_idx(mode)) for c, _, mode in outs]
    out_specs += [pl.BlockSpec((1, c), lambda i: (0, 0)) for c in sums]
    out_rows = {None: rows, SKIP: rows - tm, FIRST: tm}
    out_shape = [jax.ShapeDtypeStruct((out_rows[mode], c), dt) for c, dt, mode in outs]
    out_shape += [jax.ShapeDtypeStruct((1, c), F32) for c in sums]
    return _pc(body, name=name, grid=(rows // tm,), in_specs=in_specs, out_specs=out_specs, out_shape=out_shape,
               compiler_params=pltpu.CompilerParams(dimension_semantics=("arbitrary",)))(*[x[0] for x in ins], *vecs)


def _rms_parts(x, g):
    r = lax.rsqrt(jnp.mean(x * x, axis=-1, keepdims=True) + EPS)
    return x * r, r


def _rms_bwd_block(x, g, dy):
    xhat, r = _rms_parts(x, g)
    dxhat = dy * g
    dx = r * (dxhat - xhat * jnp.mean(dxhat * xhat, axis=-1, keepdims=True))
    return dx, dy * xhat


def _rms_fwd(x, g, *, name):
    c = x[1] if isinstance(x, tuple) else x.shape[1]
    return _ew(lambda rid, xv, gv: _rms_parts(xv.astype(F32), gv)[0] * gv, [x], [g], [(c, BF16)], name=name)[0]


def _rms_bwd(x, g, dy, *, name, res=None, mask=False, with_bf16=False):
    c = x[1] if isinstance(x, tuple) else x.shape[1]

    def fn(rid, xv, dyv, *rest):
        gv = rest[-1]
        dx, dg = _rms_bwd_block(xv.astype(F32), gv, dyv.astype(F32))
        if res is not None:
            dx = dx + rest[0]
        if mask:
            dx = jnp.where(rid >= PAD, dx, 0.0)
        return (dx, dx, dg) if with_bf16 else (dx, dg)

    ins = [x, dy] + ([res] if res is not None else [])
    outs = [(c, F32)] + ([(c, BF16)] if with_bf16 else [])
    return _ew(fn, ins, [g], outs, [c], name=name)


S5_W = GROUPS_PER_BLOCK * SSM_STATE
S5_GW = GROUPS_PER_BLOCK * SSM_GROUP
S5_UNROLL = 8
S5_DA_ROWS = 272


def _s5_scan_in_place(ref, pw_ref, *, reverse):
    lp = ref.shape[0]
    tile_rows = 8
    chunk = _tile(lp, S5_DA_ROWS, tile_rows)
    tiles = chunk // tile_rows

    def chunk_body(c, carry):
        rows = pl.ds(pl.multiple_of(c * chunk, tile_rows), chunk)
        xr, xi = ref[rows, :S5_W], ref[rows, S5_W:]
        for level, k in enumerate((1, 2, 4)):
            base = tile_rows * (1 + level)
            mr, mi = pw_ref[base:base + tile_rows, :S5_W][None], pw_ref[base:base + tile_rows, S5_W:][None]
            shift = chunk - k if reverse else k
            sr = pltpu.roll(xr, shift, 0).reshape(tiles, tile_rows, S5_W)
            si = pltpu.roll(xi, shift, 0).reshape(tiles, tile_rows, S5_W)
            xr = xr + (mr * sr - mi * si).reshape(chunk, S5_W)
            xi = xi + (mr * si + mi * sr).reshape(chunk, S5_W)
        ref[rows, :S5_W] = xr
        ref[rows, S5_W:] = xi
        return carry

    lax.fori_loop(0, lp // chunk, chunk_body, 0)

    pr, pi = pw_ref[0:tile_rows, :S5_W], pw_ref[0:tile_rows, S5_W:]
    ntile = lp // tile_rows
    unroll = 4

    def step(n, carry):
        cr, ci = carry
        for q in range(unroll):
            j = n * unroll + q
            j = ntile - 1 - j if reverse else j
            rows = pl.ds(pl.multiple_of(j * tile_rows, tile_rows), tile_rows)
            nr = ref[rows, :S5_W] + (pr * cr - pi * ci)
            ni = ref[rows, S5_W:] + (pr * ci + pi * cr)
            ref[rows, :S5_W] = nr
            ref[rows, S5_W:] = ni
            cr, ci = (nr[0:1], ni[0:1]) if reverse else (nr[tile_rows - 1:], ni[tile_rows - 1:])
        return cr, ci

    z = jnp.zeros((1, S5_W), F32)
    lax.fori_loop(0, ntile // unroll, step, (z, z))


def _s5_fwd(z, bb_band, cc_band, a_l, cfg, *, name):
    lp, ds, nl = cfg.LP, cfg.DS, cfg.NL

    def body(u_ref, bb_ref, cc_ref, a_ref, hs_ref, y_ref):
        hs_ref[...] = jnp.dot(u_ref[...].astype(BF16), bb_ref[...], preferred_element_type=F32)
        _s5_scan_in_place(hs_ref, a_ref, reverse=False)
        y_ref[...] = jnp.dot(hs_ref[...].astype(BF16), cc_ref[...], preferred_element_type=F32)

    return _pc(body, name=name, grid=(cfg.NB,),
               in_specs=[pl.BlockSpec((lp, S5_GW), lambda j: (0, j)), pl.BlockSpec((S5_GW, 2 * S5_W), lambda j: (j, 0)),
                         pl.BlockSpec((2 * S5_W, S5_GW), lambda j: (j, 0)), pl.BlockSpec((32, 2 * S5_W), lambda j: (0, j))],
               out_specs=[pl.BlockSpec((lp, 2 * S5_W), lambda j: (0, j)), pl.BlockSpec((lp, S5_GW), lambda j: (0, j))],
               out_shape=[jax.ShapeDtypeStruct((lp, nl), F32), jax.ShapeDtypeStruct((lp, ds), F32)],
               compiler_params=pltpu.CompilerParams(dimension_semantics=("parallel",)))(z, bb_band, cc_band, a_l)


def _s5_bwd(dy, hs, z, bb_band, cc_band, a_l, du_skip, cfg, *, name):
    lp, ds, nl = cfg.LP, cfg.DS, cfg.NL
    nt = (((1,), (1,)), ((), ()))
    tn = (((0,), (0,)), ((), ()))

    def body(dy_ref, hs_ref, u_ref, bb_ref, cc_ref, a_ref, sk_ref, du_ref, dbb_ref, dcc_ref, da_ref, g_ref):
        dyv = dy_ref[...]
        g_ref[...] = lax.dot_general(dyv, cc_ref[...], nt, preferred_element_type=F32)
        _s5_scan_in_place(g_ref, a_ref, reverse=True)
        dcc_ref[...] = lax.dot_general(hs_ref[...].astype(BF16), dyv, tn, preferred_element_type=F32)
        gb = g_ref[...].astype(BF16)
        dbb_ref[...] = lax.dot_general(u_ref[...].astype(BF16), gb, tn, preferred_element_type=F32)
        du_ref[...] = lax.dot_general(gb, bb_ref[...], nt, preferred_element_type=F32) + sk_ref[...]
        dre = jnp.zeros((1, S5_W), F32)
        dim = jnp.zeros((1, S5_W), F32)
        for r0 in range(0, lp, S5_DA_ROWS):
            rows = min(S5_DA_ROWS, lp - r0)
            first = lax.broadcasted_iota(jnp.int32, (rows, 1), 0) == 0
            prev = hs_ref[r0 - 1:r0, :] if r0 else jnp.zeros((1, 2 * S5_W), F32)
            hr = jnp.where(first, prev[:, :S5_W], pltpu.roll(hs_ref[r0:r0 + rows, :S5_W], 1, 0))
            hi = jnp.where(first, prev[:, S5_W:], pltpu.roll(hs_ref[r0:r0 + rows, S5_W:], 1, 0))
            gr, gi = g_ref[r0:r0 + rows, :S5_W], g_ref[r0:r0 + rows, S5_W:]
            dre = dre + jnp.sum(gr * hr + gi * hi, axis=0, keepdims=True)
            dim = dim + jnp.sum(gi * hr - gr * hi, axis=0, keepdims=True)
        da_ref[:, :S5_W] = dre
        da_ref[:, S5_W:] = dim

    col_blk = pl.BlockSpec((lp, S5_GW), lambda j: (0, j))
    lane_blk = pl.BlockSpec((lp, 2 * S5_W), lambda j: (0, j))
    bb_blk = pl.BlockSpec((S5_GW, 2 * S5_W), lambda j: (j, 0))
    cc_blk = pl.BlockSpec((2 * S5_W, S5_GW), lambda j: (j, 0))
    a_blk = pl.BlockSpec((1, 2 * S5_W), lambda j: (0, j))
    pw_blk = pl.BlockSpec((32, 2 * S5_W), lambda j: (0, j))
    return _pc(body, name=name, grid=(cfg.NB,),
               in_specs=[col_blk, lane_blk, col_blk, bb_blk, cc_blk, pw_blk, col_blk],
               out_specs=[col_blk, bb_blk, cc_blk, a_blk],
               out_shape=[jax.ShapeDtypeStruct((lp, ds), F32), jax.ShapeDtypeStruct((ds, 2 * S5_W), F32),
                          jax.ShapeDtypeStruct((nl, S5_GW), F32), jax.ShapeDtypeStruct((1, nl), F32)],
               scratch_shapes=[pltpu.VMEM((lp, 2 * S5_W), F32)],
               compiler_params=pltpu.CompilerParams(dimension_semantics=("parallel",)))(dy, hs, z, bb_band, cc_band, a_l, du_skip)


def _conv_gate(pre, cw, cb):
    return cw[0:1] * pltpu.roll(pre, 2, 0) + cw[1:2] * pltpu.roll(pre, 1, 0) + cw[2:3] * pre + cb


def _conv_fwd(up, cw, cb, *, name):
    lp, fp2 = up.shape
    fp = fp2 // 2
    tc = _tile(fp, 256)
    nb = fp // tc

    def body(pre_ref, val_ref, cw_ref, cb_ref, o_ref):
        gate = _conv_gate(pre_ref[...].astype(F32), cw_ref[...], cb_ref[...])
        o_ref[...] = (jax.nn.silu(gate) * val_ref[...].astype(F32)).astype(BF16)

    return _pc(body, name=name, grid=(nb,),
               in_specs=[pl.BlockSpec((lp, tc), lambda j: (0, j)), pl.BlockSpec((lp, tc), lambda j: (0, nb + j)),
                         pl.BlockSpec((3, tc), lambda j: (0, j)), pl.BlockSpec((1, tc), lambda j: (0, j))],
               out_specs=pl.BlockSpec((lp, tc), lambda j: (0, j)),
               out_shape=jax.ShapeDtypeStruct((lp, fp), BF16),
               compiler_params=pltpu.CompilerParams(dimension_semantics=("parallel",)))(up, up, cw, cb)


def _conv_bwd(up, dact, cw, cb, *, name):
    lp, fp2 = up.shape
    fp = fp2 // 2
    tc = _tile(fp, 256)
    nb = fp // tc

    def body(pre_ref, val_ref, da_ref, cw_ref, cb_ref, dup_ref, dcw_ref, dcb_ref):
        pre, val, da, cwv = pre_ref[...].astype(F32), val_ref[...].astype(F32), da_ref[...].astype(F32), cw_ref[...]
        gate = _conv_gate(pre, cwv, cb_ref[...])
        sg = jax.nn.sigmoid(gate)
        dup_ref[1] = (da * (gate * sg)).astype(BF16)
        dgate = da * val * (sg * (1.0 + gate * (1.0 - sg)))
        dpre = cwv[2:3] * dgate + cwv[1:2] * pltpu.roll(dgate, lp - 1, 0) + cwv[0:1] * pltpu.roll(dgate, lp - 2, 0)
        dup_ref[0] = dpre.astype(BF16)
        dcb_ref[...] = jnp.sum(dgate, axis=0, keepdims=True)
        dcw_ref[0:1, :] = jnp.sum(dgate * pltpu.roll(pre, 2, 0), axis=0, keepdims=True)
        dcw_ref[1:2, :] = jnp.sum(dgate * pltpu.roll(pre, 1, 0), axis=0, keepdims=True)
        dcw_ref[2:3, :] = jnp.sum(dgate * pre, axis=0, keepdims=True)

    return _pc(body, name=name, grid=(nb,),
               in_specs=[pl.BlockSpec((lp, tc), lambda j: (0, j)), pl.BlockSpec((lp, tc), lambda j: (0, nb + j)),
                         pl.BlockSpec((lp, tc), lambda j: (0, j)),
                         pl.BlockSpec((3, tc), lambda j: (0, j)), pl.BlockSpec((1, tc), lambda j: (0, j))],
               out_specs=[pl.BlockSpec((2, lp, tc), lambda j: (0, 0, j)),
                          pl.BlockSpec((3, tc), lambda j: (0, j)), pl.BlockSpec((1, tc), lambda j: (0, j))],
               out_shape=[jax.ShapeDtypeStruct((2, lp, fp), BF16), jax.ShapeDtypeStruct((3, fp), F32),
                          jax.ShapeDtypeStruct((1, fp), F32)],
               compiler_params=pltpu.CompilerParams(dimension_semantics=("parallel",)))(up, up, dact, cw, cb)


def _key_limit(i, tq, lp):
    return min(lp, -(-((i + 1) * tq) // LANE) * LANE)


def _attn_mask(i, tq, nk):
    qrow = i * tq + lax.broadcasted_iota(jnp.int32, (tq, 1), 0)
    krow = lax.broadcasted_iota(jnp.int32, (1, nk), 1)
    return (krow >= PAD) & ((krow // CHUNK) <= (qrow // CHUNK)), qrow >= PAD


def _attn_scores(q, kn, kr, i, tq, scale):
    nt = (((1,), (1,)), ((), ()))
    s = lax.dot_general(q[:, :QK_NOPE], kn, nt, preferred_element_type=F32)
    s = s + lax.dot_general(q[:, QK_NOPE:], kr, nt, preferred_element_type=F32)
    mask, qvalid = _attn_mask(i, tq, kn.shape[0])
    return jnp.where(mask, s * scale, jnp.finfo(F32).min), qvalid


def _per_q_block(nq, fn):
    i = pl.program_id(1)
    for blk in range(nq):
        pl.when(i == blk)(functools.partial(fn, blk))


def _attn_fwd(qx, kv, kr, cfg, *, name):
    lp, h = cfg.LP, cfg.H
    tq = _tile(lp, 272, ROW_ALIGN)
    nq = lp // tq
    scale = 1.0 / math.sqrt(QK_NOPE + QK_ROPE)

    def body(q_ref, kn_ref, v_ref, kr_ref, o_ref, lse_ref):
        def block(blk):
            nk = _key_limit(blk, tq, lp)
            s, qvalid = _attn_scores(q_ref[...], kn_ref[:nk], kr_ref[:nk], blk, tq, scale)
            m = jnp.max(s, axis=-1, keepdims=True)
            p = jnp.exp(s - m)
            l = jnp.sum(p, axis=-1, keepdims=True)
            o = jnp.dot(p.astype(BF16), v_ref[:nk], preferred_element_type=F32) / l
            o_ref[...] = jnp.where(qvalid, o, 0.0)
            lse_ref[...] = m + jnp.log(l)

        _per_q_block(nq, block)

    return _pc(body, name=name, grid=(h, nq),
               in_specs=[pl.BlockSpec((tq, HEAD_SLOT), lambda hh, i: (i, hh)),
                         pl.BlockSpec((lp, QK_NOPE), lambda hh, i: (0, 2 * hh)),
                         pl.BlockSpec((lp, V_HEAD), lambda hh, i: (0, 2 * hh + 1)),
                         pl.BlockSpec((lp, LANE), lambda hh, i: (0, 0))],
               out_specs=[pl.BlockSpec((tq, V_HEAD), lambda hh, i: (i, hh)),
                          pl.BlockSpec((None, tq, 1), lambda hh, i: (hh, i, 0))],
               out_shape=[jax.ShapeDtypeStruct((lp, h * V_HEAD), F32), jax.ShapeDtypeStruct((h, lp, 1), F32)],
               compiler_params=pltpu.CompilerParams(dimension_semantics=("parallel", "parallel")))(qx, kv, kv, kr)


def _attn_bwd(qx, kv, kr, o, lse, do, cfg, *, name):
    lp, h = cfg.LP, cfg.H
    tq = _tile(lp, 272, ROW_ALIGN)
    nq = lp // tq
    scale = 1.0 / math.sqrt(QK_NOPE + QK_ROPE)
    tn_dims = (((0,), (0,)), ((), ()))

    def body(q_ref, kn_ref, v_ref, kr_ref, o_ref, lse_ref, do_ref, dq_ref, dkv_ref, dkr_ref, dkv_acc):
        hh, i = pl.program_id(0), pl.program_id(1)

        @pl.when(i == 0)
        def _():
            dkv_acc[...] = jnp.zeros_like(dkv_acc)

        @pl.when((i == 0) & (hh == 0))
        def _():
            dkr_ref[...] = jnp.zeros_like(dkr_ref)

        def block(blk):
            nk = _key_limit(blk, tq, lp)
            q, kn, v, krv = q_ref[...], kn_ref[:nk], v_ref[:nk], kr_ref[:nk]
            s, qvalid = _attn_scores(q, kn, krv, blk, tq, scale)
            dov = jnp.where(qvalid, do_ref[...], 0.0)
            p = jnp.exp(s - lse_ref[...])
            delta = jnp.sum(dov * o_ref[...], axis=-1, keepdims=True)
            dob = dov.astype(BF16)
            dp = lax.dot_general(dob, v, (((1,), (1,)), ((), ())), preferred_element_type=F32)
            ds = (p * (dp - delta) * scale).astype(BF16)
            dq_ref[:, :QK_NOPE] = jnp.dot(ds, kn, preferred_element_type=F32)
            dq_ref[:, QK_NOPE:] = jnp.dot(ds, krv, preferred_element_type=F32)
            dkv_acc[:nk, :QK_NOPE] += lax.dot_general(ds, q[:, :QK_NOPE], tn_dims, preferred_element_type=F32)
            dkv_acc[:nk, QK_NOPE:] += lax.dot_general(p.astype(BF16), dob, tn_dims, preferred_element_type=F32)
            dkr_ref[:nk, :] += lax.dot_general(ds, q[:, QK_NOPE:], tn_dims, preferred_element_type=F32)

        _per_q_block(nq, block)

        @pl.when(i == nq - 1)
        def _():
            dkv_ref[...] = dkv_acc[...].astype(BF16)

    return _pc(body, name=name, grid=(h, nq),
               in_specs=[pl.BlockSpec((tq, HEAD_SLOT), lambda hh, i: (i, hh)),
                         pl.BlockSpec((lp, QK_NOPE), lambda hh, i: (0, 2 * hh)),
                         pl.BlockSpec((lp, V_HEAD), lambda hh, i: (0, 2 * hh + 1)),
                         pl.BlockSpec((lp, LANE), lambda hh, i: (0, 0)),
                         pl.BlockSpec((tq, V_HEAD), lambda hh, i: (i, hh)),
                         pl.BlockSpec((None, tq, 1), lambda hh, i: (hh, i, 0)),
                         pl.BlockSpec((tq, V_HEAD), lambda hh, i: (i, hh))],
               out_specs=[pl.BlockSpec((tq, HEAD_SLOT), lambda hh, i: (i, hh)),
                          pl.BlockSpec((lp, QK_NOPE + V_HEAD), lambda hh, i: (0, hh)),
                          pl.BlockSpec((lp, LANE), lambda hh, i: (0, 0))],
               out_shape=[jax.ShapeDtypeStruct((lp, h * HEAD_SLOT), F32),
                          jax.ShapeDtypeStruct((lp, h * (QK_NOPE + V_HEAD)), BF16),
                          jax.ShapeDtypeStruct((lp, LANE), F32)],
               scratch_shapes=[pltpu.VMEM((lp, QK_NOPE + V_HEAD), F32)],
               compiler_params=pltpu.CompilerParams(dimension_semantics=("arbitrary", "arbitrary")))(qx, kv, kv, kr, o, lse, do)


def _rot_half(x):
    lane = lax.broadcasted_iota(jnp.int32, x.shape, 1)
    half = QK_ROPE // 2
    return jnp.where(lane < half, -pltpu.roll(x, LANE - half, 1), pltpu.roll(x, half, 1))


def _rope(x, cos, sin):
    return x * cos + _rot_half(x) * sin


def _unrope(dy, cos, sin):
    return dy * cos - _rot_half(dy * sin)


def _rope_heads(fn, h):
    def apply(rid, q, cos, sin):
        parts = []
        for hh in range(h):
            parts.append(q[:, hh * HEAD_SLOT: hh * HEAD_SLOT + QK_NOPE])
            parts.append(fn(q[:, hh * HEAD_SLOT + QK_NOPE: (hh + 1) * HEAD_SLOT], cos, sin))
        return jnp.concatenate(parts, axis=1)
    return apply


ANY = pl.BlockSpec(memory_space=pl.ANY)


def _place():
    x, y, c = lax.axis_index("x"), lax.axis_index("y"), lax.axis_index("c")
    chips = [(1 - x, y), (x, 1 - y), (1 - x, 1 - y)]
    return x, y, c, chips


def _rcopy(src, dst, send_sem, recv_sem, dev):
    return pltpu.make_async_remote_copy(src_ref=src, dst_ref=dst, send_sem=send_sem, recv_sem=recv_sem,
                                        device_id=dev, device_id_type=MESH)


def _place_shard(shard, dtype, *, name, order=None):
    r, cols = shard.shape
    tm = _row_tile(r, cols)
    nblk = r // tm
    me = (2 * lax.axis_index("x") + lax.axis_index("y")).astype(jnp.int32).reshape(1)
    extra = [] if order is None else [order]

    def body(me_ref, s_ref, *rest):
        rest[-1][...] = s_ref[...].astype(dtype)

    return _pc(body, name=name,
               grid_spec=pltpu.PrefetchScalarGridSpec(
                   num_scalar_prefetch=1, grid=(nblk,),
                   in_specs=[pl.BlockSpec((tm, cols), lambda i, mr: (i, 0))] + [ANY] * len(extra),
                   out_specs=pl.BlockSpec((tm, cols), lambda i, mr: (mr[0] * nblk + i, 0))),
               out_shape=jax.ShapeDtypeStruct((4 * r, cols), dtype),
               compiler_params=pltpu.CompilerParams(dimension_semantics=("arbitrary",)))(me, shard, *extra)


def _allgather(fulls, *, name):
    n = len(fulls)

    def body(*refs):
        outs = refs[n:2 * n]
        send_sems, recv_sems = refs[2 * n:]
        x, y, c, chips = _place()
        sib = (x, y, 1 - c)
        me = 2 * x + y

        def rows(t, s, half):
            hrows = outs[t].shape[0] // 8
            return outs[t].at[pl.ds((2 * s + half) * hrows, hrows)]

        sent = []
        for t in range(n):
            for j, (cx, cy) in enumerate(chips):
                cp = _rcopy(rows(t, me, c), rows(t, me, c), send_sems.at[6 * t + j], recv_sems.at[6 * t + j], (cx, cy, c))
                cp.start()
                sent.append(cp)
        for t in range(n):
            for j, (cx, cy) in enumerate(chips):
                landed = rows(t, 2 * cx + cy, c)
                _rcopy(landed, landed, send_sems.at[6 * t + j], recv_sems.at[6 * t + j], (cx, cy, c)).wait_recv()
                cp = _rcopy(landed, landed, send_sems.at[6 * t + 3 + j], recv_sems.at[6 * t + 3 + j], sib)
                cp.start()
                sent.append(cp)
        for t in range(n):
            for j, (cx, cy) in enumerate(chips):
                other = rows(t, 2 * cx + cy, 1 - c)
                _rcopy(other, other, send_sems.at[6 * t + 3 + j], recv_sems.at[6 * t + 3 + j], sib).wait_recv()
        for cp in sent:
            cp.wait_send()

    return _pc(body, name=name, in_specs=[ANY] * n, out_specs=[ANY] * n,
               out_shape=[jax.ShapeDtypeStruct(f.shape, f.dtype) for f in fulls],
               input_output_aliases={t: t for t in range(n)},
               scratch_shapes=[pltpu.SemaphoreType.DMA((6 * n,)), pltpu.SemaphoreType.DMA((6 * n,))])(*fulls)


HBM = pl.BlockSpec(memory_space=pltpu.HBM)
SEM = pl.BlockSpec(memory_space=pltpu.SEMAPHORE)
EFFECT = pltpu.SideEffectType.DATAFLOW_SIDE_EFFECTING
TOKEN = jax.ShapeDtypeStruct((8, LANE), F32)


def _in_hbm(a):
    return pltpu.with_memory_space_constraint(a, pltpu.HBM)


def _half_rows(ref, s, half):
    hrows = ref.shape[0] // 8
    return ref.at[pl.ds((2 * s + half) * hrows, hrows)]


def _split_start(bufs, copies, n_copies, *, name, before=None):
    n = len(bufs)
    extra = [] if before is None else [before]

    def body(*refs):
        send_sems, recv_sems, token = refs[n + len(extra)], refs[n + len(extra) + 1], refs[-1]
        for k, (src, dst, dev) in enumerate(copies(refs[:n])):
            _rcopy(src, dst, send_sems.at[k], recv_sems.at[k], dev).start()
        token[...] = jnp.zeros_like(token)

    res = _pc(body, name=name, in_specs=[HBM] * n + [ANY] * len(extra),
              out_specs=[SEM, SEM] + [HBM] * n + [pl.BlockSpec(memory_space=pltpu.VMEM)],
              out_shape=[pltpu.SemaphoreType.DMA((n_copies,)), pltpu.SemaphoreType.DMA((n_copies,))]
              + [pltpu.HBM(b.shape, b.dtype) for b in bufs] + [TOKEN],
              input_output_aliases={t: 2 + t for t in range(n)},
              compiler_params=pltpu.CompilerParams(has_side_effects=EFFECT))(*[_in_hbm(b) for b in bufs], *extra)
    return res[0], res[1], list(res[2:2 + n]), res[-1]


def _split_wait(send_sems, recv_sems, bufs, copies, after, *, name):
    n = len(bufs)

    def body(*refs):
        send_ref, recv_ref = refs[n], refs[n + 1]
        for k, (src, dst, dev) in enumerate(copies(refs[:n])):
            cp = _rcopy(src, dst, send_ref.at[k], recv_ref.at[k], dev)
            cp.wait_send()
            cp.wait_recv()

    return _pc(body, name=name, in_specs=[HBM] * n + [SEM, SEM, ANY], out_specs=[HBM] * n,
               out_shape=[pltpu.HBM(b.shape, b.dtype) for b in bufs],
               input_output_aliases={t: t for t in range(n)},
               compiler_params=pltpu.CompilerParams(has_side_effects=EFFECT))(*bufs, send_sems, recv_sems, after)


def _allgather_ici_copies(refs):
    x, y, c, chips = _place()
    return [(_half_rows(r, 2 * x + y, c), _half_rows(r, 2 * x + y, c), (cx, cy, c)) for r in refs for cx, cy in chips]


def _rs_chips_copies(refs):
    x, y, c, chips = _place()
    n = len(refs) // 2
    return [(refs[t].at[2 * cx + cy], refs[n + t].at[j], (cx, cy, c)) for t in range(n) for j, (cx, cy) in enumerate(chips)]


def _rs_sibling_copies(refs):
    x, y, c, _ = _place()
    n = len(refs) // 2
    out = []
    for t in range(n):
        h = refs[t].shape[0] // 8
        out += [(refs[t].at[pl.ds((2 * s + 1 - c) * h, h)], refs[n + t].at[s], (x, y, 1 - c)) for s in range(4)]
    return out


def _allgather_forward(fulls, *, name):
    n = len(fulls)

    def body(*refs):
        outs = refs[n:2 * n]
        send_sems, recv_sems = refs[2 * n:]
        x, y, c, chips = _place()
        sent = []
        for t in range(n):
            for j, (cx, cy) in enumerate(chips):
                landed = _half_rows(outs[t], 2 * cx + cy, c)
                cp = _rcopy(landed, landed, send_sems.at[3 * t + j], recv_sems.at[3 * t + j], (x, y, 1 - c))
                cp.start()
                sent.append(cp)
        for t in range(n):
            for j, (cx, cy) in enumerate(chips):
                other = _half_rows(outs[t], 2 * cx + cy, 1 - c)
                _rcopy(other, other, send_sems.at[3 * t + j], recv_sems.at[3 * t + j], (x, y, 1 - c)).wait_recv()
        for cp in sent:
            cp.wait_send()

    return _pc(body, name=name, in_specs=[ANY] * n, out_specs=[ANY] * n,
               out_shape=[jax.ShapeDtypeStruct(f.shape, f.dtype) for f in fulls],
               input_output_aliases={t: t for t in range(n)},
               scratch_shapes=[pltpu.SemaphoreType.DMA((3 * n,)), pltpu.SemaphoreType.DMA((3 * n,))])(*fulls)


def _rs_sibling(grads, *, name):
    n = len(grads)

    def body(*refs):
        ins, outs = refs[:n], refs[n:2 * n]
        send_sems, recv_sems = refs[2 * n:]
        x, y, c, _ = _place()
        cps = []
        for t in range(n):
            h = ins[t].shape[0] // 8
            for s in range(4):
                cp = _rcopy(ins[t].at[pl.ds((2 * s + 1 - c) * h, h)], outs[t].at[s], send_sems.at[4 * t + s],
                            recv_sems.at[4 * t + s], (x, y, 1 - c))
                cp.start()
                cps.append(cp)
        for cp in cps:
            cp.wait()

    return _pc(body, name=name, in_specs=[ANY] * n, out_specs=[ANY] * n,
               out_shape=[jax.ShapeDtypeStruct((4, g.shape[0] // 8, g.shape[1]), g.dtype) for g in grads],
               scratch_shapes=[pltpu.SemaphoreType.DMA((4 * n,)), pltpu.SemaphoreType.DMA((4 * n,))])(*grads)


def _rs_chips(sends, *, name):
    n = len(sends)

    def body(*refs):
        s_refs, b_refs = refs[:n], refs[n:2 * n]
        send_sems, recv_sems = refs[2 * n:]
        x, y, c, chips = _place()
        cps = []
        for t in range(n):
            for j, (cx, cy) in enumerate(chips):
                cp = _rcopy(s_refs[t].at[2 * cx + cy], b_refs[t].at[j], send_sems.at[3 * t + j], recv_sems.at[3 * t + j],
                            (cx, cy, c))
                cp.start()
                cps.append(cp)
        for cp in cps:
            cp.wait()

    return _pc(body, name=name, in_specs=[ANY] * n, out_specs=[ANY] * n,
               out_shape=[jax.ShapeDtypeStruct((3,) + s.shape[1:], s.dtype) for s in sends],
               scratch_shapes=[pltpu.SemaphoreType.DMA((3 * n,)), pltpu.SemaphoreType.DMA((3 * n,))])(*sends)


def _rs_final(fulls, *, name):
    n = len(fulls)

    def body(*refs):
        outs = refs[n:2 * n]
        send_sems, recv_sems = refs[2 * n:]
        x, y, c, _ = _place()
        cps = []
        for t in range(n):
            cp = _rcopy(outs[t].at[c], outs[t].at[c], send_sems.at[t], recv_sems.at[t], (x, y, 1 - c))
            cp.start()
            cps.append(cp)
        for cp in cps:
            cp.wait()

    return _pc(body, name=name, in_specs=[ANY] * n, out_specs=[ANY] * n,
               out_shape=[jax.ShapeDtypeStruct(f.shape, f.dtype) for f in fulls],
               input_output_aliases={t: t for t in range(n)},
               scratch_shapes=[pltpu.SemaphoreType.DMA((n,)), pltpu.SemaphoreType.DMA((n,))])(*fulls)


def _add_halves(g, a, send_dtype, *, name):
    _, h, cols = a.shape
    th = _row_tile(h, cols)
    g4 = g.reshape(4, 2, h, cols)
    idx = jnp.stack([lax.axis_index("c"), 2 * lax.axis_index("x") + lax.axis_index("y")]).astype(jnp.int32)

    def shard(k, ir):
        return (ir[1] + 1 + k) % 4

    def body(idx_ref, g_ref, a_ref, p_ref, s_ref):
        v = g_ref[...].astype(F32) + a_ref[...].astype(F32)
        s_ref[...] = v.astype(send_dtype)

        @pl.when(pl.program_id(1) == 3)
        def _():
            p_ref[...] = v

    return _pc(body, name=name,
               grid_spec=pltpu.PrefetchScalarGridSpec(
                   num_scalar_prefetch=1, grid=(h // th, 4),
                   in_specs=[pl.BlockSpec((None, None, th, cols), lambda i, k, ir: (shard(k, ir), ir[0], i, 0)),
                             pl.BlockSpec((None, th, cols), lambda i, k, ir: (shard(k, ir), i, 0))],
                   out_specs=[pl.BlockSpec((th, cols), lambda i, k, ir: (i, 0)),
                              pl.BlockSpec((None, th, cols), lambda i, k, ir: (shard(k, ir), i, 0))]),
               out_shape=[jax.ShapeDtypeStruct((h, cols), F32), jax.ShapeDtypeStruct(a.shape, send_dtype)],
               compiler_params=pltpu.CompilerParams(dimension_semantics=("arbitrary", "arbitrary")))(idx, g4, a)


def _add_chips(p, b, *, name, order=None):
    h, cols = p.shape
    th = _row_tile(h, cols)
    idx = lax.axis_index("c").astype(jnp.int32).reshape(1)
    extra = [] if order is None else [order]

    def body(idx_ref, p_ref, b_ref, *rest):
        r_ref = rest[-1]
        r_ref[...] = ((p_ref[...] + b_ref[0].astype(F32)) + b_ref[1].astype(F32)) + b_ref[2].astype(F32)

    return _pc(body, name=name,
               grid_spec=pltpu.PrefetchScalarGridSpec(
                   num_scalar_prefetch=1, grid=(h // th,),
                   in_specs=[pl.BlockSpec((th, cols), lambda i, ir: (i, 0)),
                             pl.BlockSpec((3, th, cols), lambda i, ir: (0, i, 0))] + [ANY] * len(extra),
                   out_specs=pl.BlockSpec((None, th, cols), lambda i, ir: (ir[0], i, 0))),
               out_shape=jax.ShapeDtypeStruct((2, h, cols), F32),
               compiler_params=pltpu.CompilerParams(dimension_semantics=("arbitrary",)))(idx, p, b, *extra)


def _add_halves_all(grads, recv, send_dtypes, tag):
    parts, sends = [], []
    for t, (g, a) in enumerate(zip(grads, recv)):
        p, s = _add_halves(g, a, send_dtypes[t], name=f"rs_add_halves_{tag}{t}")
        parts.append(p)
        sends.append(s)
    return parts, sends


def _rs_finish(parts, others, tag, order=None):
    halves = [_add_chips(p, b, order=order, name=f"rs_add_chips_{tag}{t}") for t, (p, b) in enumerate(zip(parts, others))]
    full = _rs_final(halves, name=f"rs_final_{tag}")
    return [f.reshape(-1, f.shape[-1]) for f in full]


def _s5_discretize(lam_re, lam_im, log_dt, b_re, b_im):
    lam = lax.complex(lam_re, lam_im)
    dt = jnp.exp(log_dt)[:, None]
    lam_bar = jnp.exp(lam * dt)
    b_bar = ((lam_bar - 1.0) / lam)[..., None] * lax.complex(b_re, b_im)
    return jnp.real(lam_bar), jnp.imag(lam_bar), jnp.real(b_bar), jnp.imag(b_bar)


def _lanes_from_gp(re, im, cfg):
    v = jnp.stack([re, im]).reshape(2, cfg.NB, GROUPS_PER_BLOCK, SSM_STATE)
    return jnp.transpose(v, (1, 0, 2, 3)).reshape(1, cfg.NL)


def _gp_from_lanes(v, cfg):
    v = jnp.transpose(v.reshape(cfg.NB, 2, GROUPS_PER_BLOCK, SSM_STATE), (1, 0, 2, 3)).reshape(2, cfg.G, SSM_STATE)
    return v[0], v[1]


def _bb_band(bb_re, bb_im, cfg):
    eye = jnp.eye(GROUPS_PER_BLOCK, dtype=F32)
    bb = jnp.stack([bb_re, bb_im]).reshape(2, cfg.NB, GROUPS_PER_BLOCK, SSM_STATE, SSM_GROUP)
    return jnp.einsum('rjgpc,gh->jgcrhp', bb, eye).reshape(cfg.DS, 2 * GROUPS_PER_BLOCK * SSM_STATE)


def _bb_from_band(m, cfg):
    eye = jnp.eye(GROUPS_PER_BLOCK, dtype=F32)
    m = m.reshape(cfg.NB, GROUPS_PER_BLOCK, SSM_GROUP, 2, GROUPS_PER_BLOCK, SSM_STATE)
    v = jnp.einsum('jgcrhp,gh->rjgpc', m, eye).reshape(2, cfg.G, SSM_STATE, SSM_GROUP)
    return v[0], v[1]


def _cc_band(c_re, c_im, cfg):
    eye = jnp.eye(GROUPS_PER_BLOCK, dtype=F32)
    cc = jnp.stack([c_re, -c_im]).reshape(2, cfg.NB, GROUPS_PER_BLOCK, SSM_GROUP, SSM_STATE)
    return jnp.einsum('rjgcp,gh->jrhpgc', cc, eye).reshape(cfg.NL, GROUPS_PER_BLOCK * SSM_GROUP)


def _cc_from_band(m, cfg):
    eye = jnp.eye(GROUPS_PER_BLOCK, dtype=F32)
    m = m.reshape(cfg.NB, 2, GROUPS_PER_BLOCK, SSM_STATE, GROUPS_PER_BLOCK, SSM_GROUP)
    v = jnp.einsum('jrhpgc,gh->rjgcp', m, eye).reshape(2, cfg.G, SSM_GROUP, SSM_STATE)
    return v[0], -v[1]


PACK_COLS = 512
PACK_ROW_ALIGN = 64


def _pack(arrs):
    flat = jnp.concatenate([a.reshape(-1).astype(F32) for a in arrs])
    unit = PACK_COLS * PACK_ROW_ALIGN
    total = -(-flat.shape[0] // unit) * unit
    return jnp.pad(flat, (0, total - flat.shape[0])).reshape(-1, PACK_COLS)


def _unpack(p, shapes):
    flat = p.reshape(-1)
    out, off = [], 0
    for shp in shapes:
        size = math.prod(shp)
        out.append(flat[off:off + size].reshape(shp))
        off += size
    return out


def _adamw(w, g, m, v, *, name, emit_grad=False):
    c1 = 1.0 / (1.0 - ADAM_B1 ** ADAM_STEP)
    c2 = 1.0 / (1.0 - ADAM_B2 ** ADAM_STEP)

    if w.ndim == 2:
        outs = _adamw(w[None], g[None], m[None], v[None], name=name, emit_grad=emit_grad)
        return [o[0] for o in outs]
    lead, rows, cols = w.shape
    tc = _tile(cols, 512)
    tm = _tile(rows, max(8, 3 * DT_F32_BLOCK_BYTES // (8 * tc)), 8)
    n_out = 4 if emit_grad else 3

    def body(w_ref, g_ref, m_ref, v_ref, *o_refs):
        gv = g_ref[...]
        mn = ADAM_B1 * m_ref[...] + (1.0 - ADAM_B1) * gv
        vn = ADAM_B2 * v_ref[...] + (1.0 - ADAM_B2) * (gv * gv)
        delta = -ADAM_LR * ((mn * c1) / (jnp.sqrt(vn * c2) + ADAM_EPS) + ADAM_WD * w_ref[...])
        for o_ref, val in zip(o_refs, ((gv, delta, mn, vn) if emit_grad else (delta, mn, vn))):
            o_ref[...] = val

    blk = pl.BlockSpec((None, tm, tc), lambda n, i, j: (n, i, j))
    return _pc(body, name=name, grid=(lead, rows // tm, cols // tc), in_specs=[blk] * 4, out_specs=[blk] * n_out,
               out_shape=[jax.ShapeDtypeStruct((lead, rows, cols), F32)] * n_out,
               compiler_params=pltpu.CompilerParams(dimension_semantics=("parallel", "parallel", "parallel")))(w, g, m, v)


def _to_comm_layout(name, w, cfg):
    w = w[0]
    if name == 'w_in':
        return jnp.pad(w, ((0, 0), (0, cfg.DINP - cfg.DIN)))
    if name == 'w_q_b':
        hs = w.shape[1] // (QK_NOPE + QK_ROPE)
        wt = w.T.reshape(hs, QK_NOPE + QK_ROPE, cfg.QL)
        return jnp.pad(wt, ((0, 0), (0, HEAD_SLOT - QK_NOPE - QK_ROPE), (0, 0))).reshape(hs * HEAD_SLOT, cfg.QL)
    if name == 'w_kv_b':
        return w.T
    if name == 'w_up':
        wt = w.T.reshape(2, cfg.F // 4, cfg.D)
        return jnp.pad(wt, ((0, 0), (0, cfg.FQ - cfg.F // 4), (0, 0))).reshape(2 * cfg.FQ, cfg.D)
    if name == 'w_down':
        return jnp.pad(w, ((0, cfg.FQ - cfg.F // 4), (0, 0)))
    return w


def _from_comm_layout(name, g, cfg):
    if name == 'w_in':
        g = g[:, :cfg.DIN]
    elif name == 'w_q_b':
        hs = g.shape[0] // HEAD_SLOT
        g = g.reshape(hs, HEAD_SLOT, cfg.QL)[:, :QK_NOPE + QK_ROPE].reshape(hs * (QK_NOPE + QK_ROPE), cfg.QL).T
    elif name == 'w_kv_b':
        g = g.T
    elif name == 'w_up':
        g = g.reshape(2, cfg.FQ, cfg.D)[:, :cfg.F // 4].reshape(cfg.F // 2, cfg.D).T
    elif name == 'w_down':
        g = g[:cfg.F // 4]
    return g[None]


def _ff_pad(v, cfg):
    k = v.shape[0]
    return jnp.pad(v.reshape(k, 4, cfg.F // 4), ((0, 0), (0, 0), (0, cfg.FQ - cfg.F // 4))).reshape(k, cfg.FP)


def _ff_unpad(v, cfg):
    k = v.shape[0]
    return v.reshape(k, 4, cfg.FQ)[:, :, :cfg.F // 4].reshape(k, cfg.F)


def _step(cfg, w, m, v, x, loss_target):
    lp, d, ds, nl = cfg.LP, cfg.D, cfg.DS, cfg.NL
    xi, yi = lax.axis_index("x"), lax.axis_index("y")
    me = 2 * xi + yi

    def place(n, order=None):
        return _place_shard(_to_comm_layout(n, w[n], cfg), BF16, order=order, name=f"place_{n}")

    first = [place('w_in'), _place_shard(w['meta_tokens'], F32, name="place_meta")]
    f_send, f_recv, f_flying, f_token = _split_start(first, _allgather_ici_copies, 6, name="allgather_first_start")
    conv_w_shard = jnp.pad(w['conv_w'][0], ((0, ROW_ALIGN - 3), (0, cfg.FQ - cfg.F // 4)))
    placed = [None] + [place(n, f_token) for n in BIG[1:]]
    placed += [None, _place_shard(conv_w_shard, F32, order=f_token, name="place_conv_w")]
    f_landed = _split_wait(f_send, f_recv, f_flying, _allgather_ici_copies, placed[6], name="allgather_first_wait")
    w_in, meta_full = _allgather_forward(f_landed, name="allgather_first_forward")
    meta = jnp.transpose(meta_full.reshape(4, N_META, d // 4), (1, 0, 2)).reshape(N_META, d)
    conv_b = _ff_pad(w['conv_b'], cfg)
    mid = placed[1:5] + [placed[8]]
    mid_send, mid_recv, mid_flying, mid_token = _split_start(mid, _allgather_ici_copies, 3 * len(mid), before=meta_full,
                                                             name="allgather_mid_start")
    ffn_send, ffn_recv, ffn_flying, ffn_token = _split_start(placed[5:7], _allgather_ici_copies, 6, before=mid_token,
                                                             name="allgather_ffn_start")
    mix_norm = w['mix_norm'] + (mid_token[0:1, 0:1] + ffn_token[0:1, 0:1])

    pos = (jnp.arange(lp, dtype=jnp.int32) - PAD).astype(F32)
    inv_freq = 1.0 / (ROPE_BASE ** (jnp.arange(0, QK_ROPE, 2, dtype=F32) / QK_ROPE))
    ang = pos[:, None] * inv_freq[None, :]
    zpad = jnp.zeros((lp, LANE - QK_ROPE), F32)
    cos_t = jnp.concatenate([jnp.cos(ang), jnp.cos(ang), zpad], axis=1)
    sin_t = jnp.concatenate([jnp.sin(ang), jnp.sin(ang), zpad], axis=1)

    s5_in = (w['lam_re'][0], w['lam_im'][0], w['log_dt'][0], w['b_re'][0], w['b_im'][0])
    (a_re, a_im, bb_re, bb_im), s5_vjp = jax.vjp(_s5_discretize, *s5_in)
    lam_dt = lax.complex(s5_in[0], s5_in[1]) * jnp.exp(s5_in[2])[:, None]
    a_pow = jnp.exp(jnp.arange(1, 9, dtype=F32)[:, None, None] * lam_dt[None])
    r8 = jnp.arange(8)
    step_f = jnp.stack([jnp.where((r8 >= k)[:, None, None], a_pow[k - 1][None], 0.0) for k in (1, 2, 4)]).reshape(24, cfg.G, -1)
    step_b = jnp.stack([jnp.where((r8 < 8 - k)[:, None, None], a_pow[k - 1][None], 0.0) for k in (1, 2, 4)]).reshape(24, cfg.G, -1)
    rows_f = jnp.concatenate([a_pow, step_f])
    rows_b = jnp.conj(jnp.concatenate([a_pow[::-1], step_b]))

    def lane_rows(t):
        v = jnp.stack([jnp.real(t), jnp.imag(t)], axis=1).reshape(t.shape[0], 2, cfg.NB, GROUPS_PER_BLOCK, SSM_STATE)
        return jnp.transpose(v, (0, 2, 1, 3, 4)).reshape(t.shape[0], cfg.NL)

    pw_fwd, pw_bwd = lane_rows(rows_f), lane_rows(rows_b)
    bb_band = _bb_band(bb_re, bb_im, cfg).astype(BF16)
    cc_band = _cc_band(w['c_re'][0], w['c_im'][0], cfg).astype(BF16)
    d_skip, b_glu = w['d_skip'], w['b_glu']

    h0 = jnp.concatenate([jnp.zeros((PAD, d), F32), meta, x[0]], axis=0)
    xn = _rms_fwd(h0, mix_norm, name="rms_mix")
    z = _mm(xn, w_in, name="mm_in", tn=_tile(cfg.DINP, 640))
    u = (z, ds, 0)
    q_a = (z, cfg.QL, ds // cfg.QL)
    kv_a = (z, cfg.KVL, (ds + cfg.QL) // cfg.KVL)
    k_pe = (z, LANE, (ds + cfg.QL + cfg.KVL) // LANE)

    hs, yc = _s5_fwd(z, bb_band, cc_band, pw_fwd, cfg, name="s5_fwd")

    def s5_y(ycv, uv, dk):
        return ycv + dk * uv

    gl = _ew(lambda rid, ycv, uv, dk: jax.nn.gelu(s5_y(ycv, uv, dk)), [yc, u], [d_skip], [(ds, BF16)], name="s5_gelu")[0]
    mid_landed = _split_wait(mid_send, mid_recv, mid_flying, _allgather_ici_copies, gl, name="allgather_mid_wait")
    w_glu, w_qt, w_kvt, w_out, conv_full = _allgather_forward(mid_landed, name="allgather_mid_forward")
    conv_w = jnp.transpose(conv_full.reshape(4, ROW_ALIGN, cfg.FQ)[:, :3], (1, 0, 2)).reshape(3, cfg.FP)
    tg = _mm(gl, w_glu, name="mm_glu")
    ya = _ew(lambda rid, ycv, uv, tv, dk, bg: jax.nn.gelu(s5_y(ycv, uv, dk)) * jax.nn.sigmoid(tv + bg),
             [yc, u, tg], [d_skip, b_glu], [(ds, F32)], name="s5_glu")[0]

    qn = _rms_fwd(q_a, w['q_a_norm'], name="rms_q")
    kvn = _rms_fwd(kv_a, w['kv_a_norm'], name="rms_kv")
    q_raw = _mm(qn, w_qt, tb=True, name="mm_q")
    qx = _ew(_rope_heads(_rope, cfg.H), [q_raw, cos_t, sin_t], [], [(cfg.H * HEAD_SLOT, BF16)], name="rope_q")[0]
    kv = _mm(kvn, w_kvt, tb=True, out_dtype=BF16, name="mm_kv")
    kr = _ew(lambda rid, kp, cs, sn: _rope(kp, cs, sn), [k_pe, cos_t, sin_t], [], [(LANE, BF16)], name="rope_k")[0]
    o, lse = _attn_fwd(qx, kv, kr, cfg, name="attn_fwd")

    def norm2(rid, yav, ov, gs, ga):
        return jnp.concatenate([_rms_parts(yav, gs)[0] * gs, _rms_parts(ov, ga)[0] * ga], axis=1)

    yn = _ew(norm2, [ya, o], [w['out_norm_ssm'], w['out_norm_attn']], [(cfg.DMIX, BF16)], name="rms_out")[0]
    h1 = _mm(yn, w_out, res=h0, name="mm_out")
    xn2 = _rms_fwd(h1, w['ffn_norm'], name="rms_ffn")
    ffn_landed = _split_wait(ffn_send, ffn_recv, ffn_flying, _allgather_ici_copies, xn2, name="allgather_ffn_wait")
    w_upt, w_down = _allgather_forward(ffn_landed, name="allgather_ffn_forward")
    up = _mm(xn2, w_upt, tb=True, out_dtype=BF16, name="mm_up")
    act = _conv_fwd(up, conv_w, conv_b, name="conv_fwd")
    h2 = _mm(act, w_down, res=h1, tm=_tile(lp, 544, ROW_ALIGN), name="mm_down")

    g_final = w['final_norm'].reshape(1, d)

    def head(rid, hv, tv, gv):
        xhat, r = _rms_parts(hv, gv)
        valid = rid >= PAD + N_META
        diff = jnp.where(valid, xhat * gv - tv, 0.0)
        dout = diff * (1.0 / d)
        dxhat = dout * gv
        dx = r * (dxhat - xhat * jnp.mean(dxhat * xhat, axis=-1, keepdims=True))
        return dx, dx, dout * xhat, 0.5 * diff * dout

    dh2, dh2_b, dg_final, loss_cols = _ew(head, [h2, (loss_target[0], d, 0, SKIP)], [g_final], [(d, F32), (d, BF16)], [d, d],
                                          tm=PAD + N_META, name="loss_head")
    loss = lax.psum(jnp.sum(loss_cols), ("x", "y", "c"))

    dact = _mm(dh2_b, w_down, tb=True, out_dtype=BF16, name="mm_dact")
    dw_down = _mm(act, dh2_b, ta=True, tn=d, tm=512, out_dtype=BF16, name="mm_dw_down")

    def sibling_start(g, tag):
        land = lax.empty((4, g.shape[0] // 8, g.shape[1]), g.dtype)
        return _split_start([g, land], _rs_sibling_copies, 4, name=f"rs_sibling_{tag}_start")

    dn_send, dn_recv, dn_flying, dn_token = sibling_start(dw_down, "down")
    dup, dconv_w, dconv_b = _conv_bwd(up, dact, conv_w, conv_b + dn_token[0:1, 0:1], name="conv_bwd")
    tk_up, tm_up = _tile(cfg.FP, 1408), _tile(cfg.FP, 512)
    dw_upt = _mm(dup, xn2, ta=True, dims=(2 * cfg.FP, d, lp), tn=d, tm=tm_up, a_lead=True, out_dtype=BF16, name="mm_dw_up",
                 a_idx=lambda i, j, k: (i // (cfg.FP // tm_up), 0, i % (cfg.FP // tm_up)))
    up_send, up_recv, up_flying, up_token = sibling_start(dw_upt, "up")
    dxn2 = _mm(dup, w_upt, dims=(lp, d, 2 * cfg.FP), tk=tk_up, tn=1024, a_lead=True, name="mm_dxn2",
               a_idx=lambda i, j, k: (k // (cfg.FP // tk_up), i, k % (cfg.FP // tk_up)))
    dh1, dh1_b, dg_ffn = _rms_bwd(h1, w['ffn_norm'] + up_token[0:1, 0:1], dxn2, res=dh2, mask=True, with_bf16=True,
                                  name="rms_ffn_bwd")

    dyn = _mm(dh1_b, w_out, tb=True, name="mm_dyn")
    dw_out = _mm(yn, dh1_b, ta=True, tn=d, tm=512, name="mm_dw_out")
    up_done = _split_wait(up_send, up_recv, up_flying, _rs_sibling_copies, dw_out, name="rs_sibling_up_wait")
    dn_done = _split_wait(dn_send, dn_recv, dn_flying, _rs_sibling_copies, dw_out, name="rs_sibling_down_wait")
    early_parts, early_sends = _add_halves_all([up_done[0], dn_done[0]], [up_done[1], dn_done[1]], [BF16] * 2, "early")
    chip_lands = [lax.empty((3,) + s.shape[1:], s.dtype) for s in early_sends]
    ch_send, ch_recv, ch_flying, ch_token = _split_start(early_sends + chip_lands, _rs_chips_copies, 6,
                                                         name="rs_chips_early_start")
    dya, dg_ssm = _rms_bwd(ya, w['out_norm_ssm'] + ch_token[0:1, 0:1], (dyn, ds, 0), name="rms_ssm_bwd")
    do, dg_attn = _rms_bwd(o, w['out_norm_attn'], (dyn, cfg.DATTN, ds // cfg.DATTN), name="rms_attn_bwd")

    dqx, dkv, dkr = _attn_bwd(qx, kv, kr, o, lse, do, cfg, name="attn_bwd")
    dq_raw = _ew(_rope_heads(_unrope, cfg.H), [dqx, cos_t, sin_t], [], [(cfg.H * HEAD_SLOT, BF16)], name="unrope_q")[0]
    dk_pe = _ew(lambda rid, dk, cs, sn: _unrope(dk, cs, sn), [dkr, cos_t, sin_t], [], [(LANE, F32)], name="unrope_k")[0]
    dqn = _mm(dq_raw, w_qt, name="mm_dqn")
    dw_qt = _mm(dq_raw, qn, ta=True, tm=512, name="mm_dw_q")
    dkvn = _mm(dkv, w_kvt, name="mm_dkvn")
    dw_kvt = _mm(dkv, kvn, ta=True, tm=512, name="mm_dw_kv")
    dq_a, dg_q = _rms_bwd(q_a, w['q_a_norm'], dqn, name="rms_q_bwd")
    dkv_a, dg_kv = _rms_bwd(kv_a, w['kv_a_norm'], dkvn, name="rms_kv_bwd")

    def glu_bwd(rid, ycv, uv, tv, dyav, dk, bg):
        gelu = jax.nn.gelu(s5_y(ycv, uv, dk))
        sg = jax.nn.sigmoid(tv + bg)
        dt = dyav * gelu * sg * (1.0 - sg)
        return dt, dyav * sg, dt

    dt_b, dgl1, db_glu = _ew(glu_bwd, [yc, u, tg, dya], [d_skip, b_glu], [(ds, BF16), (ds, F32)], [ds], name="s5_glu_bwd")
    dgl = _mm(dt_b, w_glu, tb=True, res=dgl1, name="mm_dgl")
    dw_glu = _mm(gl, dt_b, ta=True, tm=512, name="mm_dw_glu")

    def gelu_bwd(rid, ycv, uv, dglv, dk):
        _, vjp = jax.vjp(jax.nn.gelu, s5_y(ycv, uv, dk))
        dy = vjp(dglv)[0]
        return dy, dy * dk, dy * uv

    mid_grads = [dw_out, dw_glu, dw_qt, dw_kvt]
    mid_lands = [lax.empty((4, g.shape[0] // 8, g.shape[1]), g.dtype) for g in mid_grads]
    ms_send, ms_recv, ms_flying, ms_token = _split_start(mid_grads + mid_lands, _rs_sibling_copies, 4 * len(mid_grads),
                                                         name="rs_sibling_mid_start")
    dy_b, du_skip, dd_skip = _ew(gelu_bwd, [yc, u, dgl], [d_skip + ms_token[0:1, 0:1]], [(ds, BF16), (ds, F32)], [ds],
                                 name="s5_gelu_bwd")
    ms_done = _split_wait(ms_send, ms_recv, ms_flying, _rs_sibling_copies, dy_b, name="rs_sibling_mid_wait")
    mid_parts, mid_sends = _add_halves_all(ms_done[:4], ms_done[4:], [BF16] * 4, "mid")
    mid_chip_lands = [lax.empty((3,) + s.shape[1:], s.dtype) for s in mid_sends]
    mc_send, mc_recv, mc_flying, mc_token = _split_start(mid_sends + mid_chip_lands, _rs_chips_copies, 3 * len(mid_sends),
                                                         name="rs_chips_mid_start")
    du, dbb_band, dcc_band, da_l = _s5_bwd(dy_b, hs, z, bb_band, cc_band, pw_bwd + mc_token[0:1, 0:1], du_skip, cfg,
                                           name="s5_bwd")

    dz = jnp.concatenate([du, dq_a, dkv_a, dk_pe], axis=1).astype(BF16)
    dxn = _mm(dz, w_in, tb=True, name="mm_dxn")
    dw_in = _mm(xn, dz, ta=True, tm=512, tn=_tile(cfg.DINP, 1024), name="mm_dw_in")
    def mix_bwd(rid, xv, dyv, resv, gv):
        dx, dg = _rms_bwd_block(xv, gv, dyv)
        dx = dx + resv
        return dx, dx, dg

    grad_x, dh0_head, dg_mix = _ew(mix_bwd, [h0, dxn, dh1], [mix_norm], [(d, F32, SKIP), (d, F32, FIRST)], [d],
                                   tm=PAD + N_META, name="rms_mix_bwd")
    grad_x = grad_x[None]

    da_re, da_im = _gp_from_lanes(da_l, cfg)
    dbb_re, dbb_im = _bb_from_band(dbb_band, cfg)
    dlam_re, dlam_im, dlog_dt, db_re, db_im = s5_vjp((da_re, da_im, dbb_re, dbb_im))
    dc_re, dc_im = _cc_from_band(dcc_band, cfg)
    local_small = {
        'meta_tokens': dh0_head[PAD:], 'mix_norm': dg_mix, 'lam_re': dlam_re, 'lam_im': dlam_im, 'log_dt': dlog_dt,
        'b_re': db_re, 'b_im': db_im, 'c_re': dc_re, 'c_im': dc_im, 'd_skip': dd_skip, 'b_glu': db_glu, 'q_a_norm': dg_q,
        'kv_a_norm': dg_kv, 'out_norm_ssm': dg_ssm, 'out_norm_attn': dg_attn, 'ffn_norm': dg_ffn,
        'conv_w': _ff_unpad(dconv_w, cfg), 'conv_b': _ff_unpad(dconv_b, cfg), 'final_norm': dg_final,
    }
    small_shapes = [local_small[n].shape for n in SMALL]

    small_pack = _pack([local_small[n] for n in SMALL])
    end_local = [dw_in, small_pack]
    end_recv = _rs_sibling(end_local, name="rs_sibling_end")
    end_parts, end_sends = _add_halves_all(end_local, end_recv, [BF16, F32], "end")
    end_lands = [lax.empty((3,) + s.shape[1:], s.dtype) for s in end_sends]
    ec_send, ec_recv, ec_flying, ec_token = _split_start(end_sends + end_lands, _rs_chips_copies, 3 * len(end_sends),
                                                         name="rs_chips_end_start")
    ch_done = _split_wait(ch_send, ch_recv, ch_flying, _rs_chips_copies, ec_token, name="rs_chips_early_wait")
    red_up, red_down = _rs_finish(early_parts, ch_done[2:], "early")

    delta, new_m, new_v, grads = {}, {}, {}, {}
    padded_rows = ('w_down',)

    def adamw_big(n, red):
        shp = w[n].shape
        w2, m2, v2 = [t.reshape(shp[-2], shp[-1]) for t in (w[n], m[n], v[n])]
        if n in padded_rows:
            g2, dl, mn, vn = _adamw(w2, red, m2, v2, emit_grad=True, name=f"adamw_{n}")
            grads[n] = g2.reshape(shp)
        else:
            grads[n] = _from_comm_layout(n, red, cfg)
            dl, mn, vn = _adamw(w2, grads[n].reshape(shp[-2], shp[-1]), m2, v2, name=f"adamw_{n}")
        delta[n], new_m[n], new_v[n] = dl.reshape(shp), mn.reshape(shp), vn.reshape(shp)

    def adamw_up(red):
        q = cfg.F // 4
        wt, mt, vt = [jnp.transpose(t[0]).reshape(2, q, d) for t in (w['w_up'], m['w_up'], v['w_up'])]
        outs = _adamw(wt, red.reshape(2, cfg.FQ, d), mt, vt, emit_grad=True, name="adamw_w_up")
        grads['w_up'], delta['w_up'], new_m['w_up'], new_v['w_up'] = [jnp.transpose(t.reshape(2 * q, d))[None] for t in outs]

    adamw_up(red_up)
    adamw_big('w_down', red_down)
    mc_done = _split_wait(mc_send, mc_recv, mc_flying, _rs_chips_copies, delta['w_down'], name="rs_chips_mid_wait")
    ec_done = _split_wait(ec_send, ec_recv, ec_flying, _rs_chips_copies, mc_done[0], name="rs_chips_end_wait")
    red = _rs_finish(mid_parts + end_parts, list(mc_done[len(mid_sends):]) + list(ec_done[len(end_sends):]), "rest")
    small_full = _allgather([_place_shard(red[5], F32, name="place_small")], name="allgather_small")[0]
    small_sum = dict(zip(SMALL, _unpack(small_full, small_shapes)))
    for n, r in zip(['w_out', 'w_glu', 'w_q_b', 'w_kv_b'], red[:4]):
        adamw_big(n, r)
    in_t = [jnp.transpose(t[0]) for t in (w['w_in'], m['w_in'], v['w_in'])]
    outs = _adamw(in_t[0], jnp.transpose(red[4][:, :cfg.DIN]), in_t[1], in_t[2], emit_grad=True, name="adamw_w_in")
    grads['w_in'], delta['w_in'], new_m['w_in'], new_v['w_in'] = [jnp.transpose(t)[None] for t in outs]

    for n in SMALL:
        g = small_sum[n]
        if n == 'meta_tokens':
            g = lax.dynamic_slice_in_dim(g, me * (d // 4), d // 4, axis=1)
        elif n == 'conv_w':
            g = lax.dynamic_slice_in_dim(g, me * (cfg.F // 4), cfg.F // 4, axis=1)[None]
        else:
            g = g.reshape(w[n].shape)
        grads[n] = g

    shapes = [w[n].shape for n in SMALL]
    packs = [_pack([src[n] for n in SMALL]) for src in (w, grads, m, v)]
    for dst, p in zip((delta, new_m, new_v), _adamw(*packs, name="adamw_small")):
        dst.update(zip(SMALL, _unpack(p, shapes)))

    return (loss, grad_x, *[grads[n] for n in WEIGHTS], *[delta[n] for n in WEIGHTS],
            *[new_m[n] for n in WEIGHTS], *[new_v[n] for n in WEIGHTS])


def kernel(x, meta_tokens, mix_norm, w_in, lam_re, lam_im, log_dt, b_re, b_im, c_re, c_im, d_skip, w_glu, b_glu, q_a_norm, w_q_b, kv_a_norm, w_kv_b, out_norm_ssm, out_norm_attn, w_out, ffn_norm, w_up, conv_w, conv_b, w_down, final_norm, loss_target, m_meta_tokens, m_mix_norm, m_w_in, m_lam_re, m_lam_im, m_log_dt, m_b_re, m_b_im, m_c_re, m_c_im, m_d_skip, m_w_glu, m_b_glu, m_q_a_norm, m_w_q_b, m_kv_a_norm, m_w_kv_b, m_out_norm_ssm, m_out_norm_attn, m_w_out, m_ffn_norm, m_w_up, m_conv_w, m_conv_b, m_w_down, m_final_norm, v_meta_tokens, v_mix_norm, v_w_in, v_lam_re, v_lam_im, v_log_dt, v_b_re, v_b_im, v_c_re, v_c_im, v_d_skip, v_w_glu, v_b_glu, v_q_a_norm, v_w_q_b, v_kv_a_norm, v_w_kv_b, v_out_norm_ssm, v_out_norm_attn, v_w_out, v_ffn_norm, v_w_up, v_conv_w, v_conv_b, v_w_down, v_final_norm):
    args = dict(locals())
    w = {n: args[n] for n in WEIGHTS}
    m = {n: args["m_" + n] for n in WEIGHTS}
    v = {n: args["v_" + n] for n in WEIGHTS}
    return _step(PROD, w, m, v, x, loss_target)
```

```python
import functools
import math
from typing import NamedTuple

import jax
import jax.numpy as jnp
from jax import lax
from jax.experimental import pallas as pl
from jax.experimental.pallas import tpu as pltpu

F32, BF16 = jnp.float32, jnp.bfloat16
MESH = pl.DeviceIdType.MESH
LANE = 128
ROW_ALIGN = 16
N_META = 16
PAD = 112
CHUNK = 64
SSM_GROUP = 16
SSM_STATE = 64
GROUPS_PER_BLOCK = 8
QK_NOPE, QK_ROPE, V_HEAD = 128, 64, 128
HEAD_SLOT = 256
ROPE_BASE = 10000.0
EPS = 1e-6
ADAM_LR, ADAM_B1, ADAM_B2, ADAM_EPS, ADAM_WD, ADAM_STEP = 0.001, 0.9, 0.999, 1e-08, 0.01, 10
DT_F32_BLOCK_BYTES = 1 << 20
SKIP, FIRST = "skip", "first"


class Cfg(NamedTuple):
    D: int
    S: int
    DS: int
    H: int
    QL: int
    KVL: int
    F: int

    @property
    def LP(self):
        return PAD + N_META + self.S

    @property
    def G(self):
        return self.DS // SSM_GROUP

    @property
    def NB(self):
        return self.G // GROUPS_PER_BLOCK

    @property
    def NL(self):
        return 2 * self.G * SSM_STATE

    @property
    def DATTN(self):
        return self.H * V_HEAD

    @property
    def DMIX(self):
        return self.DS + self.DATTN

    @property
    def DIN(self):
        return self.DS + self.QL + self.KVL + QK_ROPE

    @property
    def DINP(self):
        return self.DS + self.QL + self.KVL + LANE

    @property
    def FQ(self):
        return -(-(self.F // 4) // LANE) * LANE

    @property
    def FP(self):
        return 4 * self.FQ


PROD = Cfg(D=2048, S=2048, DS=1024, H=8, QL=512, KVL=256, F=5504)

WEIGHTS = ['meta_tokens', 'mix_norm', 'w_in', 'lam_re', 'lam_im', 'log_dt', 'b_re', 'b_im', 'c_re', 'c_im', 'd_skip',
           'w_glu', 'b_glu', 'q_a_norm', 'w_q_b', 'kv_a_norm', 'w_kv_b', 'out_norm_ssm', 'out_norm_attn', 'w_out',
           'ffn_norm', 'w_up', 'conv_w', 'conv_b', 'w_down', 'final_norm']
BIG = ['w_in', 'w_glu', 'w_q_b', 'w_kv_b', 'w_out', 'w_up', 'w_down']
SMALL = [n for n in WEIGHTS if n not in BIG]


def _pc(body, **kw):
    return pl.pallas_call(body, **kw)


def _tile(n, target, align=LANE):
    best = None
    d = align
    while d <= min(n, target):
        if n % d == 0:
            best = d
        d += align
    return best if best is not None else n


def _row_tile(rows, cols):
    return _tile(rows, max(ROW_ALIGN, DT_F32_BLOCK_BYTES // (4 * cols)), ROW_ALIGN)


def _mm(a, b, *, name, ta=False, tb=False, tm=None, tn=512, tk=None, out_dtype=F32, res=None,
        a_idx=None, b_idx=None, dims=None, a_lead=False):
    if dims is None:
        m, k = (a.shape[1], a.shape[0]) if ta else a.shape
        n = b.shape[0] if tb else b.shape[1]
    else:
        m, n, k = dims
    tm = _tile(m, tm or m, LANE if ta else ROW_ALIGN)
    tn = _tile(n, tn)
    tk = _tile(k, tk or k, ROW_ALIGN if (ta and not tb) else LANE)
    nm, nn, nk = m // tm, n // tn, k // tk
    a_idx = a_idx or ((lambda i, j, kk: (kk, i)) if ta else (lambda i, j, kk: (i, kk)))
    b_idx = b_idx or ((lambda i, j, kk: (j, kk)) if tb else (lambda i, j, kk: (kk, j)))
    dn = (((0 if ta else 1,), (1 if tb else 0,)), ((), ()))

    def body(*refs):
        a_ref, b_ref = refs[0], refs[1]
        r_ref = refs[2] if res is not None else None
        o_ref = refs[3] if res is not None else refs[2]
        d = lax.dot_general(a_ref[...].astype(BF16), b_ref[...].astype(BF16), dn, preferred_element_type=F32)

        def finish(r):
            if r_ref is not None:
                r = r + r_ref[...].astype(F32)
            o_ref[...] = r.astype(out_dtype)

        if nk == 1:
            finish(d)
        else:
            acc = refs[-1]
            kk = pl.program_id(2)

            @pl.when(kk == 0)
            def _():
                acc[...] = d

            @pl.when(kk > 0)
            def _():
                acc[...] += d

            @pl.when(kk == nk - 1)
            def _():
                finish(acc[...])

    a_blk = ((None,) if a_lead else ()) + ((tk, tm) if ta else (tm, tk))
    in_specs = [pl.BlockSpec(a_blk, a_idx), pl.BlockSpec((tn, tk) if tb else (tk, tn), b_idx)]
    args = [a, b]
    if res is not None:
        in_specs.append(pl.BlockSpec((tm, tn), lambda i, j, kk: (i, j)))
        args.append(res)
    return _pc(body, name=name, grid=(nm, nn, nk), in_specs=in_specs,
               out_specs=pl.BlockSpec((tm, tn), lambda i, j, kk: (i, j)),
               out_shape=jax.ShapeDtypeStruct((m, n), out_dtype),
               scratch_shapes=[pltpu.VMEM((tm, tn), F32)] if nk > 1 else [],
               compiler_params=pltpu.CompilerParams(dimension_semantics=("parallel", "parallel", "arbitrary")))(*args)


def _ew(fn, ins, vecs, outs, sums=(), *, name, tm=None):
    ins = [x if isinstance(x, tuple) else (x, x.shape[1], 0) for x in ins]
    ins = [x if len(x) == 4 else x + (None,) for x in ins]
    outs = [o if len(o) == 3 else o + (None,) for o in outs]
    rows = ins[0][0].shape[0]
    cmax = max([c for _, c, _, _ in ins] + [c for c, _, _ in outs])
    tm = tm or _row_tile(rows, cmax)
    n_in, n_vec, n_out, n_sum = len(ins), len(vecs), len(outs), len(sums)

    def body(*refs):
        i = pl.program_id(0)
        rid = i * tm + lax.broadcasted_iota(jnp.int32, (tm, 1), 0)
        vals = [r[...] for r in refs[:n_in + n_vec]]
        res = fn(rid, *vals)
        res = res if isinstance(res, (tuple, list)) else (res,)
        o_refs = refs[n_in + n_vec:]
        for o_ref, r, (_, _, mode) in zip(o_refs[:n_out], res[:n_out], outs):
            if mode == FIRST:
                @pl.when(i == 0)
                def _():
                    o_ref[...] = r.astype(o_ref.dtype)
            else:
                o_ref[...] = r.astype(o_ref.dtype)
        for o_ref, r in zip(o_refs[n_out:], res[n_out:]):
            part = jnp.sum(r.astype(F32), axis=0, keepdims=True)

            @pl.when(i == 0)
            def _():
                o_ref[...] = part

            @pl.when(i > 0)
            def _():
                o_ref[...] += part

    def row_idx(mode):
        if mode == SKIP:
            return lambda i, cb=0: (jnp.maximum(i - 1, 0), cb)
        if mode == FIRST:
            return lambda i, cb=0: (0, cb)
        return lambda i, cb=0: (i, cb)

    in_specs = [pl.BlockSpec((tm, c), functools.partial(row_idx(mode), cb=cb)) for _, c, cb, mode in ins]
    in_specs += [pl.BlockSpec(v.shape, functools.partial(lambda i, nd: (0,) * nd, nd=v.ndim)) for v in vecs]
    out_specs = [pl.BlockSpec((tm, c), row_idx(mode)) for c, _, mode in outs]
    out_specs += [pl.BlockSpec((1, c), lambda i: (0, 0)) for c in sums]
    out_rows = {None: rows, SKIP: rows - tm, FIRST: tm}
    out_shape = [jax.ShapeDtypeStruct((out_rows[mode], c), dt) for c, dt, mode in outs]
    out_shape += [jax.ShapeDtypeStruct((1, c), F32) for c in sums]
    return _pc(body, name=name, grid=(rows // tm,), in_specs=in_specs, out_specs=out_specs, out_shape=out_shape,
               compiler_params=pltpu.CompilerParams(dimension_semantics=("arbitrary",)))(*[x[0] for x in ins], *vecs)


def _rms_parts(x, g):
    r = lax.rsqrt(jnp.mean(x * x, axis=-1, keepdims=True) + EPS)
    return x * r, r


def _rms_bwd_block(x, g, dy):
    xhat, r = _rms_parts(x, g)
    dxhat = dy * g
    dx = r * (dxhat - xhat * jnp.mean(dxhat * xhat, axis=-1, keepdims=True))
    return dx, dy * xhat


def _rms_fwd(x, g, *, name):
    c = x[1] if isinstance(x, tuple) else x.shape[1]
    return _ew(lambda rid, xv, gv: _rms_parts(xv.astype(F32), gv)[0] * gv, [x], [g], [(c, BF16)], name=name)[0]


def _rms_bwd(x, g, dy, *, name, res=None, mask=False, with_bf16=False):
    c = x[1] if isinstance(x, tuple) else x.shape[1]

    def fn(rid, xv, dyv, *rest):
        gv = rest[-1]
        dx, dg = _rms_bwd_block(xv.astype(F32), gv, dyv.astype(F32))
        if res is not None:
            dx = dx + rest[0]
        if mask:
            dx = jnp.where(rid >= PAD, dx, 0.0)
        return (dx, dx, dg) if with_bf16 else (dx, dg)

    ins = [x, dy] + ([res] if res is not None else [])
    outs = [(c, F32)] + ([(c, BF16)] if with_bf16 else [])
    return _ew(fn, ins, [g], outs, [c], name=name)


S5_W = GROUPS_PER_BLOCK * SSM_STATE
S5_GW = GROUPS_PER_BLOCK * SSM_GROUP
S5_UNROLL = 8
S5_DA_ROWS = 272


def _s5_scan_in_place(ref, pw_ref, *, reverse):
    lp = ref.shape[0]
    tile_rows = 8
    chunk = _tile(lp, S5_DA_ROWS, tile_rows)
    tiles = chunk // tile_rows

    def chunk_body(c, carry):
        rows = pl.ds(pl.multiple_of(c * chunk, tile_rows), chunk)
        xr, xi = ref[rows, :S5_W], ref[rows, S5_W:]
        for level, k in enumerate((1, 2, 4)):
            base = tile_rows * (1 + level)
            mr, mi = pw_ref[base:base + tile_rows, :S5_W][None], pw_ref[base:base + tile_rows, S5_W:][None]
            shift = chunk - k if reverse else k
            sr = pltpu.roll(xr, shift, 0).reshape(tiles, tile_rows, S5_W)
            si = pltpu.roll(xi, shift, 0).reshape(tiles, tile_rows, S5_W)
            xr = xr + (mr * sr - mi * si).reshape(chunk, S5_W)
            xi = xi + (mr * si + mi * sr).reshape(chunk, S5_W)
        ref[rows, :S5_W] = xr
        ref[rows, S5_W:] = xi
        return carry

    lax.fori_loop(0, lp // chunk, chunk_body, 0)

    pr, pi = pw_ref[0:tile_rows, :S5_W], pw_ref[0:tile_rows, S5_W:]
    ntile = lp // tile_rows
    unroll = 4

    def step(n, carry):
        cr, ci = carry
        for q in range(unroll):
            j = n * unroll + q
            j = ntile - 1 - j if reverse else j
            rows = pl.ds(pl.multiple_of(j * tile_rows, tile_rows), tile_rows)
            nr = ref[rows, :S5_W] + (pr * cr - pi * ci)
            ni = ref[rows, S5_W:] + (pr * ci + pi * cr)
            ref[rows, :S5_W] = nr
            ref[rows, S5_W:] = ni
            cr, ci = (nr[0:1], ni[0:1]) if reverse else (nr[tile_rows - 1:], ni[tile_rows - 1:])
        return cr, ci

    z = jnp.zeros((1, S5_W), F32)
    lax.fori_loop(0, ntile // unroll, step, (z, z))


def _s5_fwd(z, bb_band, cc_band, a_l, cfg, *, name):
    lp, ds, nl = cfg.LP, cfg.DS, cfg.NL

    def body(u_ref, bb_ref, cc_ref, a_ref, hs_ref, y_ref):
        hs_ref[...] = jnp.dot(u_ref[...].astype(BF16), bb_ref[...], preferred_element_type=F32)
        _s5_scan_in_place(hs_ref, a_ref, reverse=False)
        y_ref[...] = jnp.dot(hs_ref[...].astype(BF16), cc_ref[...], preferred_element_type=F32)

    return _pc(body, name=name, grid=(cfg.NB,),
               in_specs=[pl.BlockSpec((lp, S5_GW), lambda j: (0, j)), pl.BlockSpec((S5_GW, 2 * S5_W), lambda j: (j, 0)),
                         pl.BlockSpec((2 * S5_W, S5_GW), lambda j: (j, 0)), pl.BlockSpec((32, 2 * S5_W), lambda j: (0, j))],
               out_specs=[pl.BlockSpec((lp, 2 * S5_W), lambda j: (0, j)), pl.BlockSpec((lp, S5_GW), lambda j: (0, j))],
               out_shape=[jax.ShapeDtypeStruct((lp, nl), F32), jax.ShapeDtypeStruct((lp, ds), F32)],
               compiler_params=pltpu.CompilerParams(dimension_semantics=("parallel",)))(z, bb_band, cc_band, a_l)


def _s5_bwd(dy, hs, z, bb_band, cc_band, a_l, du_skip, cfg, *, name):
    lp, ds, nl = cfg.LP, cfg.DS, cfg.NL
    nt = (((1,), (1,)), ((), ()))
    tn = (((0,), (0,)), ((), ()))

    def body(dy_ref, hs_ref, u_ref, bb_ref, cc_ref, a_ref, sk_ref, du_ref, dbb_ref, dcc_ref, da_ref, g_ref):
        dyv = dy_ref[...]
        g_ref[...] = lax.dot_general(dyv, cc_ref[...], nt, preferred_element_type=F32)
        _s5_scan_in_place(g_ref, a_ref, reverse=True)
        dcc_ref[...] = lax.dot_general(hs_ref[...].astype(BF16), dyv, tn, preferred_element_type=F32)
        gb = g_ref[...].astype(BF16)
        dbb_ref[...] = lax.dot_general(u_ref[...].astype(BF16), gb, tn, preferred_element_type=F32)
        du_ref[...] = lax.dot_general(gb, bb_ref[...], nt, preferred_element_type=F32) + sk_ref[...]
        dre = jnp.zeros((1, S5_W), F32)
        dim = jnp.zeros((1, S5_W), F32)
        for r0 in range(0, lp, S5_DA_ROWS):
            rows = min(S5_DA_ROWS, lp - r0)
            first = lax.broadcasted_iota(jnp.int32, (rows, 1), 0) == 0
            prev = hs_ref[r0 - 1:r0, :] if r0 else jnp.zeros((1, 2 * S5_W), F32)
            hr = jnp.where(first, prev[:, :S5_W], pltpu.roll(hs_ref[r0:r0 + rows, :S5_W], 1, 0))
            hi = jnp.where(first, prev[:, S5_W:], pltpu.roll(hs_ref[r0:r0 + rows, S5_W:], 1, 0))
            gr, gi = g_ref[r0:r0 + rows, :S5_W], g_ref[r0:r0 + rows, S5_W:]
            dre = dre + jnp.sum(gr * hr + gi * hi, axis=0, keepdims=True)
            dim = dim + jnp.sum(gi * hr - gr * hi, axis=0, keepdims=True)
        da_ref[:, :S5_W] = dre
        da_ref[:, S5_W:] = dim

    col_blk = pl.BlockSpec((lp, S5_GW), lambda j: (0, j))
    lane_blk = pl.BlockSpec((lp, 2 * S5_W), lambda j: (0, j))
    bb_blk = pl.BlockSpec((S5_GW, 2 * S5_W), lambda j: (j, 0))
    cc_blk = pl.BlockSpec((2 * S5_W, S5_GW), lambda j: (j, 0))
    a_blk = pl.BlockSpec((1, 2 * S5_W), lambda j: (0, j))
    pw_blk = pl.BlockSpec((32, 2 * S5_W), lambda j: (0, j))
    return _pc(body, name=name, grid=(cfg.NB,),
               in_specs=[col_blk, lane_blk, col_blk, bb_blk, cc_blk, pw_blk, col_blk],
               out_specs=[col_blk, bb_blk, cc_blk, a_blk],
               out_shape=[jax.ShapeDtypeStruct((lp, ds), F32), jax.ShapeDtypeStruct((ds, 2 * S5_W), F32),
                          jax.ShapeDtypeStruct((nl, S5_GW), F32), jax.ShapeDtypeStruct((1, nl), F32)],
               scratch_shapes=[pltpu.VMEM((lp, 2 * S5_W), F32)],
               compiler_params=pltpu.CompilerParams(dimension_semantics=("parallel",)))(dy, hs, z, bb_band, cc_band, a_l, du_skip)


def _conv_gate(pre, cw, cb):
    return cw[0:1] * pltpu.roll(pre, 2, 0) + cw[1:2] * pltpu.roll(pre, 1, 0) + cw[2:3] * pre + cb


def _ffn_up(xn2, w_upt, cw, cb, *, name):
    lp, d = xn2.shape
    fp = w_upt.shape[0] // 2
    tc = _tile(fp, 256)
    nb = fp // tc

    def body(x_ref, wg_ref, wv_ref, cw_ref, cb_ref, up_ref, act_ref):
        wcat = jnp.concatenate([wg_ref[...], wv_ref[...]], axis=0)
        r = lax.dot_general(x_ref[...], wcat, (((1,), (1,)), ((), ())), preferred_element_type=F32)
        pre, val = r[:, :tc].astype(BF16), r[:, tc:].astype(BF16)
        up_ref[0] = pre
        up_ref[1] = val
        gate = _conv_gate(pre.astype(F32), cw_ref[...], cb_ref[...])
        act_ref[...] = (jax.nn.silu(gate) * val.astype(F32)).astype(BF16)

    return _pc(body, name=name, grid=(nb,),
               in_specs=[pl.BlockSpec((lp, d), lambda j: (0, 0)), pl.BlockSpec((tc, d), lambda j: (j, 0)),
                         pl.BlockSpec((tc, d), lambda j: (nb + j, 0)),
                         pl.BlockSpec((3, tc), lambda j: (0, j)), pl.BlockSpec((1, tc), lambda j: (0, j))],
               out_specs=[pl.BlockSpec((2, lp, tc), lambda j: (0, 0, j)), pl.BlockSpec((lp, tc), lambda j: (0, j))],
               out_shape=[jax.ShapeDtypeStruct((2, lp, fp), BF16), jax.ShapeDtypeStruct((lp, fp), BF16)],
               compiler_params=pltpu.CompilerParams(dimension_semantics=("parallel",)))(xn2, w_upt, w_upt, cw, cb)


def _ffn_dact(dh2, w_down, up, cw, cb, *, name):
    lp, d = dh2.shape
    fp = w_down.shape[0]
    tc = _tile(fp, 256)
    nb = fp // tc

    def body(dh_ref, wd_ref, up_ref, cw_ref, cb_ref, dup_ref, dcw_ref, dcb_ref):
        da = lax.dot_general(dh_ref[...], wd_ref[...], (((1,), (1,)), ((), ())), preferred_element_type=F32)
        pre, val, cwv = up_ref[0].astype(F32), up_ref[1].astype(F32), cw_ref[...]
        gate = _conv_gate(pre, cwv, cb_ref[...])
        sg = jax.nn.sigmoid(gate)
        dup_ref[1] = (da * (gate * sg)).astype(BF16)
        dgate = da * val * (sg * (1.0 + gate * (1.0 - sg)))
        dpre = cwv[2:3] * dgate + cwv[1:2] * pltpu.roll(dgate, lp - 1, 0) + cwv[0:1] * pltpu.roll(dgate, lp - 2, 0)
        dup_ref[0] = dpre.astype(BF16)
        dcb_ref[...] = jnp.sum(dgate, axis=0, keepdims=True)
        dcw_ref[0:1, :] = jnp.sum(dgate * pltpu.roll(pre, 2, 0), axis=0, keepdims=True)
        dcw_ref[1:2, :] = jnp.sum(dgate * pltpu.roll(pre, 1, 0), axis=0, keepdims=True)
        dcw_ref[2:3, :] = jnp.sum(dgate * pre, axis=0, keepdims=True)

    return _pc(body, name=name, grid=(nb,),
               in_specs=[pl.BlockSpec((lp, d), lambda j: (0, 0)), pl.BlockSpec((tc, d), lambda j: (j, 0)),
                         pl.BlockSpec((2, lp, tc), lambda j: (0, 0, j)),
                         pl.BlockSpec((3, tc), lambda j: (0, j)), pl.BlockSpec((1, tc), lambda j: (0, j))],
               out_specs=[pl.BlockSpec((2, lp, tc), lambda j: (0, 0, j)),
                          pl.BlockSpec((3, tc), lambda j: (0, j)), pl.BlockSpec((1, tc), lambda j: (0, j))],
               out_shape=[jax.ShapeDtypeStruct((2, lp, fp), BF16), jax.ShapeDtypeStruct((3, fp), F32),
                          jax.ShapeDtypeStruct((1, fp), F32)],
               compiler_params=pltpu.CompilerParams(dimension_semantics=("parallel",)))(dh2, w_down, up, cw, cb)


def _key_limit(i, tq, lp):
    return min(lp, -(-((i + 1) * tq) // LANE) * LANE)


def _attn_mask(i, tq, nk):
    qrow = i * tq + lax.broadcasted_iota(jnp.int32, (tq, 1), 0)
    krow = lax.broadcasted_iota(jnp.int32, (1, nk), 1)
    return (krow >= PAD) & ((krow // CHUNK) <= (qrow // CHUNK)), qrow >= PAD


def _attn_scores(q, kn, kr, i, tq, scale):
    nt = (((1,), (1,)), ((), ()))
    s = lax.dot_general(q[:, :QK_NOPE], kn, nt, preferred_element_type=F32)
    s = s + lax.dot_general(q[:, QK_NOPE:], kr, nt, preferred_element_type=F32)
    mask, qvalid = _attn_mask(i, tq, kn.shape[0])
    return jnp.where(mask, s * scale, jnp.finfo(F32).min), qvalid


def _per_q_block(nq, fn):
    i = pl.program_id(1)
    for blk in range(nq):
        pl.when(i == blk)(functools.partial(fn, blk))


def _attn_fwd(qx, kv, kr, cfg, *, name):
    lp, h = cfg.LP, cfg.H
    tq = _tile(lp, 272, ROW_ALIGN)
    nq = lp // tq
    scale = 1.0 / math.sqrt(QK_NOPE + QK_ROPE)

    def body(q_ref, kn_ref, v_ref, kr_ref, o_ref, lse_ref):
        def block(blk):
            nk = _key_limit(blk, tq, lp)
            s, qvalid = _attn_scores(q_ref[...], kn_ref[:nk], kr_ref[:nk], blk, tq, scale)
            m = jnp.max(s, axis=-1, keepdims=True)
            p = jnp.exp(s - m)
            l = jnp.sum(p, axis=-1, keepdims=True)
            o = jnp.dot(p.astype(BF16), v_ref[:nk], preferred_element_type=F32) / l
            o_ref[...] = jnp.where(qvalid, o, 0.0)
            lse_ref[...] = m + jnp.log(l)

        _per_q_block(nq, block)

    return _pc(body, name=name, grid=(h, nq),
               in_specs=[pl.BlockSpec((tq, HEAD_SLOT), lambda hh, i: (i, hh)),
                         pl.BlockSpec((lp, QK_NOPE), lambda hh, i: (0, 2 * hh)),
                         pl.BlockSpec((lp, V_HEAD), lambda hh, i: (0, 2 * hh + 1)),
                         pl.BlockSpec((lp, LANE), lambda hh, i: (0, 0))],
               out_specs=[pl.BlockSpec((tq, V_HEAD), lambda hh, i: (i, hh)),
                          pl.BlockSpec((None, tq, 1), lambda hh, i: (hh, i, 0))],
               out_shape=[jax.ShapeDtypeStruct((lp, h * V_HEAD), F32), jax.ShapeDtypeStruct((h, lp, 1), F32)],
               compiler_params=pltpu.CompilerParams(dimension_semantics=("parallel", "parallel")))(qx, kv, kv, kr)


def _attn_bwd(qx, kv, kr, o, lse, do, cfg, *, name):
    lp, h = cfg.LP, cfg.H
    tq = _tile(lp, 272, ROW_ALIGN)
    nq = lp // tq
    scale = 1.0 / math.sqrt(QK_NOPE + QK_ROPE)
    tn_dims = (((0,), (0,)), ((), ()))

    def body(q_ref, kn_ref, v_ref, kr_ref, o_ref, lse_ref, do_ref, dq_ref, dkv_ref, dkr_ref, dkv_acc):
        hh, i = pl.program_id(0), pl.program_id(1)

        @pl.when(i == 0)
        def _():
            dkv_acc[...] = jnp.zeros_like(dkv_acc)

        @pl.when((i == 0) & (hh == 0))
        def _():
            dkr_ref[...] = jnp.zeros_like(dkr_ref)

        def block(blk):
            nk = _key_limit(blk, tq, lp)
            q, kn, v, krv = q_ref[...], kn_ref[:nk], v_ref[:nk], kr_ref[:nk]
            s, qvalid = _attn_scores(q, kn, krv, blk, tq, scale)
            dov = jnp.where(qvalid, do_ref[...], 0.0)
            p = jnp.exp(s - lse_ref[...])
            delta = jnp.sum(dov * o_ref[...], axis=-1, keepdims=True)
            dob = dov.astype(BF16)
            dp = lax.dot_general(dob, v, (((1,), (1,)), ((), ())), preferred_element_type=F32)
            ds = (p * (dp - delta) * scale).astype(BF16)
            dq_ref[:, :QK_NOPE] = jnp.dot(ds, kn, preferred_element_type=F32)
            dq_ref[:, QK_NOPE:] = jnp.dot(ds, krv, preferred_element_type=F32)
            dkv_acc[:nk, :QK_NOPE] += lax.dot_general(ds, q[:, :QK_NOPE], tn_dims, preferred_element_type=F32)
            dkv_acc[:nk, QK_NOPE:] += lax.dot_general(p.astype(BF16), dob, tn_dims, preferred_element_type=F32)
            dkr_ref[:nk, :] += lax.dot_general(ds, q[:, QK_NOPE:], tn_dims, preferred_element_type=F32)

        _per_q_block(nq, block)

        @pl.when(i == nq - 1)
        def _():
            dkv_ref[...] = dkv_acc[...].astype(BF16)

    return _pc(body, name=name, grid=(h, nq),
               in_specs=[pl.BlockSpec((tq, HEAD_SLOT), lambda hh, i: (i, hh)),
                         pl.BlockSpec((lp, QK_NOPE), lambda hh, i: (0, 2 * hh)),
                         pl.BlockSpec((lp, V_HEAD), lambda hh, i: (0, 2 * hh + 1)),
                         pl.BlockSpec((lp, LANE), lambda hh, i: (0, 0)),
                         pl.BlockSpec((tq, V_HEAD), lambda hh, i: (i, hh)),
                         pl.BlockSpec((None, tq, 1), lambda hh, i: (hh, i, 0)),
                         pl.BlockSpec((tq, V_HEAD), lambda hh, i: (i, hh))],
               out_specs=[pl.BlockSpec((tq, HEAD_SLOT), lambda hh, i: (i, hh)),
                          pl.BlockSpec((lp, QK_NOPE + V_HEAD), lambda hh, i: (0, hh)),
                          pl.BlockSpec((lp, LANE), lambda hh, i: (0, 0))],
               out_shape=[jax.ShapeDtypeStruct((lp, h * HEAD_SLOT), F32),
                          jax.ShapeDtypeStruct((lp, h * (QK_NOPE + V_HEAD)), BF16),
                          jax.ShapeDtypeStruct((lp, LANE), F32)],
               scratch_shapes=[pltpu.VMEM((lp, QK_NOPE + V_HEAD), F32)],
               compiler_params=pltpu.CompilerParams(dimension_semantics=("arbitrary", "arbitrary")))(qx, kv, kv, kr, o, lse, do)


def _rot_half(x):
    lane = lax.broadcasted_iota(jnp.int32, x.shape, 1)
    half = QK_ROPE // 2
    return jnp.where(lane < half, -pltpu.roll(x, LANE - half, 1), pltpu.roll(x, half, 1))


def _rope(x, cos, sin):
    return x * cos + _rot_half(x) * sin


def _unrope(dy, cos, sin):
    return dy * cos - _rot_half(dy * sin)


def _rope_heads(fn, h):
    def apply(rid, q, cos, sin):
        parts = []
        for hh in range(h):
            parts.append(q[:, hh * HEAD_SLOT: hh * HEAD_SLOT + QK_NOPE])
            parts.append(fn(q[:, hh * HEAD_SLOT + QK_NOPE: (hh + 1) * HEAD_SLOT], cos, sin))
        return jnp.concatenate(parts, axis=1)
    return apply


ANY = pl.BlockSpec(memory_space=pl.ANY)


def _place():
    x, y, c = lax.axis_index("x"), lax.axis_index("y"), lax.axis_index("c")
    chips = [(1 - x, y), (x, 1 - y), (1 - x, 1 - y)]
    return x, y, c, chips


def _rcopy(src, dst, send_sem, recv_sem, dev):
    return pltpu.make_async_remote_copy(src_ref=src, dst_ref=dst, send_sem=send_sem, recv_sem=recv_sem,
                                        device_id=dev, device_id_type=MESH)


def _place_shard(shard, dtype, *, name, order=None):
    r, cols = shard.shape
    tm = _row_tile(r, cols)
    nblk = r // tm
    me = (2 * lax.axis_index("x") + lax.axis_index("y")).astype(jnp.int32).reshape(1)
    extra = [] if order is None else [order]

    def body(me_ref, s_ref, *rest):
        rest[-1][...] = s_ref[...].astype(dtype)

    return _pc(body, name=name,
               grid_spec=pltpu.PrefetchScalarGridSpec(
                   num_scalar_prefetch=1, grid=(nblk,),
                   in_specs=[pl.BlockSpec((tm, cols), lambda i, mr: (i, 0))] + [ANY] * len(extra),
                   out_specs=pl.BlockSpec((tm, cols), lambda i, mr: (mr[0] * nblk + i, 0))),
               out_shape=jax.ShapeDtypeStruct((4 * r, cols), dtype),
               compiler_params=pltpu.CompilerParams(dimension_semantics=("arbitrary",)))(me, shard, *extra)


def _allgather(fulls, *, name):
    n = len(fulls)

    def body(*refs):
        outs = refs[n:2 * n]
        send_sems, recv_sems = refs[2 * n:]
        x, y, c, chips = _place()
        sib = (x, y, 1 - c)
        me = 2 * x + y

        def rows(t, s, half):
            hrows = outs[t].shape[0] // 8
            return outs[t].at[pl.ds((2 * s + half) * hrows, hrows)]

        sent = []
        for t in range(n):
            for j, (cx, cy) in enumerate(chips):
                cp = _rcopy(rows(t, me, c), rows(t, me, c), send_sems.at[6 * t + j], recv_sems.at[6 * t + j], (cx, cy, c))
                cp.start()
                sent.append(cp)
        for t in range(n):
            for j, (cx, cy) in enumerate(chips):
                landed = rows(t, 2 * cx + cy, c)
                _rcopy(landed, landed, send_sems.at[6 * t + j], recv_sems.at[6 * t + j], (cx, cy, c)).wait_recv()
                cp = _rcopy(landed, landed, send_sems.at[6 * t + 3 + j], recv_sems.at[6 * t + 3 + j], sib)
                cp.start()
                sent.append(cp)
        for t in range(n):
            for j, (cx, cy) in enumerate(chips):
                other = rows(t, 2 * cx + cy, 1 - c)
                _rcopy(other, other, send_sems.at[6 * t + 3 + j], recv_sems.at[6 * t + 3 + j], sib).wait_recv()
        for cp in sent:
            cp.wait_send()

    return _pc(body, name=name, in_specs=[ANY] * n, out_specs=[ANY] * n,
               out_shape=[jax.ShapeDtypeStruct(f.shape, f.dtype) for f in fulls],
               input_output_aliases={t: t for t in range(n)},
               scratch_shapes=[pltpu.SemaphoreType.DMA((6 * n,)), pltpu.SemaphoreType.DMA((6 * n,))])(*fulls)


HBM = pl.BlockSpec(memory_space=pltpu.HBM)
SEM = pl.BlockSpec(memory_space=pltpu.SEMAPHORE)
EFFECT = pltpu.SideEffectType.DATAFLOW_SIDE_EFFECTING
TOKEN = jax.ShapeDtypeStruct((8, LANE), F32)


def _in_hbm(a):
    return pltpu.with_memory_space_constraint(a, pltpu.HBM)


def _half_rows(ref, s, half):
    hrows = ref.shape[0] // 8
    return ref.at[pl.ds((2 * s + half) * hrows, hrows)]


def _split_start(bufs, copies, n_copies, *, name, before=None):
    n = len(bufs)
    extra = [] if before is None else [before]

    def body(*refs):
        send_sems, recv_sems, token = refs[n + len(extra)], refs[n + len(extra) + 1], refs[-1]
        for k, (src, dst, dev) in enumerate(copies(refs[:n])):
            _rcopy(src, dst, send_sems.at[k], recv_sems.at[k], dev).start()
        token[...] = jnp.zeros_like(token)

    res = _pc(body, name=name, in_specs=[HBM] * n + [ANY] * len(extra),
              out_specs=[SEM, SEM] + [HBM] * n + [pl.BlockSpec(memory_space=pltpu.VMEM)],
              out_shape=[pltpu.SemaphoreType.DMA((n_copies,)), pltpu.SemaphoreType.DMA((n_copies,))]
              + [pltpu.HBM(b.shape, b.dtype) for b in bufs] + [TOKEN],
              input_output_aliases={t: 2 + t for t in range(n)},
              compiler_params=pltpu.CompilerParams(has_side_effects=EFFECT))(*[_in_hbm(b) for b in bufs], *extra)
    return res[0], res[1], list(res[2:2 + n]), res[-1]


def _split_wait(send_sems, recv_sems, bufs, copies, after, *, name):
    n = len(bufs)

    def body(*refs):
        send_ref, recv_ref = refs[n], refs[n + 1]
        for k, (src, dst, dev) in enumerate(copies(refs[:n])):
            cp = _rcopy(src, dst, send_ref.at[k], recv_ref.at[k], dev)
            cp.wait_send()
            cp.wait_recv()

    return _pc(body, name=name, in_specs=[HBM] * n + [SEM, SEM, ANY], out_specs=[HBM] * n,
               out_shape=[pltpu.HBM(b.shape, b.dtype) for b in bufs],
               input_output_aliases={t: t for t in range(n)},
               compiler_params=pltpu.CompilerParams(has_side_effects=EFFECT))(*bufs, send_sems, recv_sems, after)


def _allgather_ici_copies(refs):
    x, y, c, chips = _place()
    return [(_half_rows(r, 2 * x + y, c), _half_rows(r, 2 * x + y, c), (cx, cy, c)) for r in refs for cx, cy in chips]


def _rs_chips_copies(refs):
    x, y, c, chips = _place()
    n = len(refs) // 2
    return [(refs[t].at[2 * cx + cy], refs[n + t].at[j], (cx, cy, c)) for t in range(n) for j, (cx, cy) in enumerate(chips)]


def _rs_sibling_copies(refs):
    x, y, c, _ = _place()
    n = len(refs) // 2
    out = []
    for t in range(n):
        h = refs[t].shape[0] // 8
        out += [(refs[t].at[pl.ds((2 * s + 1 - c) * h, h)], refs[n + t].at[s], (x, y, 1 - c)) for s in range(4)]
    return out


def _allgather_forward(fulls, *, name):
    n = len(fulls)

    def body(*refs):
        outs = refs[n:2 * n]
        send_sems, recv_sems = refs[2 * n:]
        x, y, c, chips = _place()
        sent = []
        for t in range(n):
            for j, (cx, cy) in enumerate(chips):
                landed = _half_rows(outs[t], 2 * cx + cy, c)
                cp = _rcopy(landed, landed, send_sems.at[3 * t + j], recv_sems.at[3 * t + j], (x, y, 1 - c))
                cp.start()
                sent.append(cp)
        for t in range(n):
            for j, (cx, cy) in enumerate(chips):
                other = _half_rows(outs[t], 2 * cx + cy, 1 - c)
                _rcopy(other, other, send_sems.at[3 * t + j], recv_sems.at[3 * t + j], (x, y, 1 - c)).wait_recv()
        for cp in sent:
            cp.wait_send()

    return _pc(body, name=name, in_specs=[ANY] * n, out_specs=[ANY] * n,
               out_shape=[jax.ShapeDtypeStruct(f.shape, f.dtype) for f in fulls],
               input_output_aliases={t: t for t in range(n)},
               scratch_shapes=[pltpu.SemaphoreType.DMA((3 * n,)), pltpu.SemaphoreType.DMA((3 * n,))])(*fulls)


def _rs_sibling(grads, *, name):
    n = len(grads)

    def body(*refs):
        ins, outs = refs[:n], refs[n:2 * n]
        send_sems, recv_sems = refs[2 * n:]
        x, y, c, _ = _place()
        cps = []
        for t in range(n):
            h = ins[t].shape[0] // 8
            for s in range(4):
                cp = _rcopy(ins[t].at[pl.ds((2 * s + 1 - c) * h, h)], outs[t].at[s], send_sems.at[4 * t + s],
                            recv_sems.at[4 * t + s], (x, y, 1 - c))
                cp.start()
                cps.append(cp)
        for cp in cps:
            cp.wait()

    return _pc(body, name=name, in_specs=[ANY] * n, out_specs=[ANY] * n,
               out_shape=[jax.ShapeDtypeStruct((4, g.shape[0] // 8, g.shape[1]), g.dtype) for g in grads],
               scratch_shapes=[pltpu.SemaphoreType.DMA((4 * n,)), pltpu.SemaphoreType.DMA((4 * n,))])(*grads)


def _rs_chips(sends, *, name):
    n = len(sends)

    def body(*refs):
        s_refs, b_refs = refs[:n], refs[n:2 * n]
        send_sems, recv_sems = refs[2 * n:]
        x, y, c, chips = _place()
        cps = []
        for t in range(n):
            for j, (cx, cy) in enumerate(chips):
                cp = _rcopy(s_refs[t].at[2 * cx + cy], b_refs[t].at[j], send_sems.at[3 * t + j], recv_sems.at[3 * t + j],
                            (cx, cy, c))
                cp.start()
                cps.append(cp)
        for cp in cps:
            cp.wait()

    return _pc(body, name=name, in_specs=[ANY] * n, out_specs=[ANY] * n,
               out_shape=[jax.ShapeDtypeStruct((3,) + s.shape[1:], s.dtype) for s in sends],
               scratch_shapes=[pltpu.SemaphoreType.DMA((3 * n,)), pltpu.SemaphoreType.DMA((3 * n,))])(*sends)


def _rs_final(fulls, *, name):
    n = len(fulls)

    def body(*refs):
        outs = refs[n:2 * n]
        send_sems, recv_sems = refs[2 * n:]
        x, y, c, _ = _place()
        cps = []
        for t in range(n):
            cp = _rcopy(outs[t].at[c], outs[t].at[c], send_sems.at[t], recv_sems.at[t], (x, y, 1 - c))
            cp.start()
            cps.append(cp)
        for cp in cps:
            cp.wait()

    return _pc(body, name=name, in_specs=[ANY] * n, out_specs=[ANY] * n,
               out_shape=[jax.ShapeDtypeStruct(f.shape, f.dtype) for f in fulls],
               input_output_aliases={t: t for t in range(n)},
               scratch_shapes=[pltpu.SemaphoreType.DMA((n,)), pltpu.SemaphoreType.DMA((n,))])(*fulls)


def _add_halves(g, a, send_dtype, *, name):
    _, h, cols = a.shape
    th = _row_tile(h, cols)
    g4 = g.reshape(4, 2, h, cols)
    idx = jnp.stack([lax.axis_index("c"), 2 * lax.axis_index("x") + lax.axis_index("y")]).astype(jnp.int32)

    def shard(k, ir):
        return (ir[1] + 1 + k) % 4

    def body(idx_ref, g_ref, a_ref, p_ref, s_ref):
        v = g_ref[...].astype(F32) + a_ref[...].astype(F32)
        s_ref[...] = v.astype(send_dtype)

        @pl.when(pl.program_id(1) == 3)
        def _():
            p_ref[...] = v

    return _pc(body, name=name,
               grid_spec=pltpu.PrefetchScalarGridSpec(
                   num_scalar_prefetch=1, grid=(h // th, 4),
                   in_specs=[pl.BlockSpec((None, None, th, cols), lambda i, k, ir: (shard(k, ir), ir[0], i, 0)),
                             pl.BlockSpec((None, th, cols), lambda i, k, ir: (shard(k, ir), i, 0))],
                   out_specs=[pl.BlockSpec((th, cols), lambda i, k, ir: (i, 0)),
                              pl.BlockSpec((None, th, cols), lambda i, k, ir: (shard(k, ir), i, 0))]),
               out_shape=[jax.ShapeDtypeStruct((h, cols), F32), jax.ShapeDtypeStruct(a.shape, send_dtype)],
               compiler_params=pltpu.CompilerParams(dimension_semantics=("arbitrary", "arbitrary")))(idx, g4, a)


def _add_chips(p, b, *, name, order=None):
    h, cols = p.shape
    th = _row_tile(h, cols)
    idx = lax.axis_index("c").astype(jnp.int32).reshape(1)
    extra = [] if order is None else [order]

    def body(idx_ref, p_ref, b_ref, *rest):
        r_ref = rest[-1]
        r_ref[...] = ((p_ref[...] + b_ref[0].astype(F32)) + b_ref[1].astype(F32)) + b_ref[2].astype(F32)

    return _pc(body, name=name,
               grid_spec=pltpu.PrefetchScalarGridSpec(
                   num_scalar_prefetch=1, grid=(h // th,),
                   in_specs=[pl.BlockSpec((th, cols), lambda i, ir: (i, 0)),
                             pl.BlockSpec((3, th, cols), lambda i, ir: (0, i, 0))] + [ANY] * len(extra),
                   out_specs=pl.BlockSpec((None, th, cols), lambda i, ir: (ir[0], i, 0))),
               out_shape=jax.ShapeDtypeStruct((2, h, cols), F32),
               compiler_params=pltpu.CompilerParams(dimension_semantics=("arbitrary",)))(idx, p, b, *extra)


def _add_halves_all(grads, recv, send_dtypes, tag):
    parts, sends = [], []
    for t, (g, a) in enumerate(zip(grads, recv)):
        p, s = _add_halves(g, a, send_dtypes[t], name=f"rs_add_halves_{tag}{t}")
        parts.append(p)
        sends.append(s)
    return parts, sends


def _rs_finish(parts, others, tag, order=None):
    halves = [_add_chips(p, b, order=order, name=f"rs_add_chips_{tag}{t}") for t, (p, b) in enumerate(zip(parts, others))]
    full = _rs_final(halves, name=f"rs_final_{tag}")
    return [f.reshape(-1, f.shape[-1]) for f in full]


def _s5_discretize(lam_re, lam_im, log_dt, b_re, b_im):
    lam = lax.complex(lam_re, lam_im)
    dt = jnp.exp(log_dt)[:, None]
    lam_bar = jnp.exp(lam * dt)
    b_bar = ((lam_bar - 1.0) / lam)[..., None] * lax.complex(b_re, b_im)
    return jnp.real(lam_bar), jnp.imag(lam_bar), jnp.real(b_bar), jnp.imag(b_bar)


def _lanes_from_gp(re, im, cfg):
    v = jnp.stack([re, im]).reshape(2, cfg.NB, GROUPS_PER_BLOCK, SSM_STATE)
    return jnp.transpose(v, (1, 0, 2, 3)).reshape(1, cfg.NL)


def _gp_from_lanes(v, cfg):
    v = jnp.transpose(v.reshape(cfg.NB, 2, GROUPS_PER_BLOCK, SSM_STATE), (1, 0, 2, 3)).reshape(2, cfg.G, SSM_STATE)
    return v[0], v[1]


def _bb_band(bb_re, bb_im, cfg):
    eye = jnp.eye(GROUPS_PER_BLOCK, dtype=F32)
    bb = jnp.stack([bb_re, bb_im]).reshape(2, cfg.NB, GROUPS_PER_BLOCK, SSM_STATE, SSM_GROUP)
    return jnp.einsum('rjgpc,gh->jgcrhp', bb, eye).reshape(cfg.DS, 2 * GROUPS_PER_BLOCK * SSM_STATE)


def _bb_from_band(m, cfg):
    eye = jnp.eye(GROUPS_PER_BLOCK, dtype=F32)
    m = m.reshape(cfg.NB, GROUPS_PER_BLOCK, SSM_GROUP, 2, GROUPS_PER_BLOCK, SSM_STATE)
    v = jnp.einsum('jgcrhp,gh->rjgpc', m, eye).reshape(2, cfg.G, SSM_STATE, SSM_GROUP)
    return v[0], v[1]


def _cc_band(c_re, c_im, cfg):
    eye = jnp.eye(GROUPS_PER_BLOCK, dtype=F32)
    cc = jnp.stack([c_re, -c_im]).reshape(2, cfg.NB, GROUPS_PER_BLOCK, SSM_GROUP, SSM_STATE)
    return jnp.einsum('rjgcp,gh->jrhpgc', cc, eye).reshape(cfg.NL, GROUPS_PER_BLOCK * SSM_GROUP)


def _cc_from_band(m, cfg):
    eye = jnp.eye(GROUPS_PER_BLOCK, dtype=F32)
    m = m.reshape(cfg.NB, 2, GROUPS_PER_BLOCK, SSM_STATE, GROUPS_PER_BLOCK, SSM_GROUP)
    v = jnp.einsum('jrhpgc,gh->rjgcp', m, eye).reshape(2, cfg.G, SSM_GROUP, SSM_STATE)
    return v[0], -v[1]


PACK_COLS = 512
PACK_ROW_ALIGN = 64


def _pack(arrs):
    flat = jnp.concatenate([a.reshape(-1).astype(F32) for a in arrs])
    unit = PACK_COLS * PACK_ROW_ALIGN
    total = -(-flat.shape[0] // unit) * unit
    return jnp.pad(flat, (0, total - flat.shape[0])).reshape(-1, PACK_COLS)


def _unpack(p, shapes):
    flat = p.reshape(-1)
    out, off = [], 0
    for shp in shapes:
        size = math.prod(shp)
        out.append(flat[off:off + size].reshape(shp))
        off += size
    return out


def _adamw(w, g, m, v, *, name, emit_grad=False):
    c1 = 1.0 / (1.0 - ADAM_B1 ** ADAM_STEP)
    c2 = 1.0 / (1.0 - ADAM_B2 ** ADAM_STEP)

    if w.ndim == 2:
        outs = _adamw(w[None], g[None], m[None], v[None], name=name, emit_grad=emit_grad)
        return [o[0] for o in outs]
    lead, rows, cols = w.shape
    tc = _tile(cols, 512)
    tm = _tile(rows, max(8, 3 * DT_F32_BLOCK_BYTES // (8 * tc)), 8)
    n_out = 4 if emit_grad else 3

    def body(w_ref, g_ref, m_ref, v_ref, *o_refs):
        gv = g_ref[...]
        mn = ADAM_B1 * m_ref[...] + (1.0 - ADAM_B1) * gv
        vn = ADAM_B2 * v_ref[...] + (1.0 - ADAM_B2) * (gv * gv)
        delta = -ADAM_LR * ((mn * c1) / (jnp.sqrt(vn * c2) + ADAM_EPS) + ADAM_WD * w_ref[...])
        for o_ref, val in zip(o_refs, ((gv, delta, mn, vn) if emit_grad else (delta, mn, vn))):
            o_ref[...] = val

    blk = pl.BlockSpec((None, tm, tc), lambda n, i, j: (n, i, j))
    return _pc(body, name=name, grid=(lead, rows // tm, cols // tc), in_specs=[blk] * 4, out_specs=[blk] * n_out,
               out_shape=[jax.ShapeDtypeStruct((lead, rows, cols), F32)] * n_out,
               compiler_params=pltpu.CompilerParams(dimension_semantics=("parallel", "parallel", "parallel")))(w, g, m, v)


def _to_comm_layout(name, w, cfg):
    w = w[0]
    if name == 'w_in':
        return jnp.pad(w, ((0, 0), (0, cfg.DINP - cfg.DIN)))
    if name == 'w_q_b':
        hs = w.shape[1] // (QK_NOPE + QK_ROPE)
        wt = w.T.reshape(hs, QK_NOPE + QK_ROPE, cfg.QL)
        return jnp.pad(wt, ((0, 0), (0, HEAD_SLOT - QK_NOPE - QK_ROPE), (0, 0))).reshape(hs * HEAD_SLOT, cfg.QL)
    if name == 'w_kv_b':
        return w.T
    if name == 'w_up':
        wt = w.T.reshape(2, cfg.F // 4, cfg.D)
        return jnp.pad(wt, ((0, 0), (0, cfg.FQ - cfg.F // 4), (0, 0))).reshape(2 * cfg.FQ, cfg.D)
    if name == 'w_down':
        return jnp.pad(w, ((0, cfg.FQ - cfg.F // 4), (0, 0)))
    return w


def _from_comm_layout(name, g, cfg):
    if name == 'w_in':
        g = g[:, :cfg.DIN]
    elif name == 'w_q_b':
        hs = g.shape[0] // HEAD_SLOT
        g = g.reshape(hs, HEAD_SLOT, cfg.QL)[:, :QK_NOPE + QK_ROPE].reshape(hs * (QK_NOPE + QK_ROPE), cfg.QL).T
    elif name == 'w_kv_b':
        g = g.T
    elif name == 'w_up':
        g = g.reshape(2, cfg.FQ, cfg.D)[:, :cfg.F // 4].reshape(cfg.F // 2, cfg.D).T
    elif name == 'w_down':
        g = g[:cfg.F // 4]
    return g[None]


def _ff_pad(v, cfg):
    k = v.shape[0]
    return jnp.pad(v.reshape(k, 4, cfg.F // 4), ((0, 0), (0, 0), (0, cfg.FQ - cfg.F // 4))).reshape(k, cfg.FP)


def _ff_unpad(v, cfg):
    k = v.shape[0]
    return v.reshape(k, 4, cfg.FQ)[:, :, :cfg.F // 4].reshape(k, cfg.F)


def _step(cfg, w, m, v, x, loss_target):
    lp, d, ds, nl = cfg.LP, cfg.D, cfg.DS, cfg.NL
    xi, yi = lax.axis_index("x"), lax.axis_index("y")
    me = 2 * xi + yi

    def place(n, order=None):
        return _place_shard(_to_comm_layout(n, w[n], cfg), BF16, order=order, name=f"place_{n}")

    first = [place('w_in'), _place_shard(w['meta_tokens'], F32, name="place_meta")]
    f_send, f_recv, f_flying, f_token = _split_start(first, _allgather_ici_copies, 6, name="allgather_first_start")
    conv_w_shard = jnp.pad(w['conv_w'][0], ((0, ROW_ALIGN - 3), (0, cfg.FQ - cfg.F // 4)))
    placed = [None] + [place(n, f_token) for n in BIG[1:]]
    placed += [None, _place_shard(conv_w_shard, F32, order=f_token, name="place_conv_w")]
    f_landed = _split_wait(f_send, f_recv, f_flying, _allgather_ici_copies, placed[6], name="allgather_first_wait")
    w_in, meta_full = _allgather_forward(f_landed, name="allgather_first_forward")
    meta = jnp.transpose(meta_full.reshape(4, N_META, d // 4), (1, 0, 2)).reshape(N_META, d)
    conv_b = _ff_pad(w['conv_b'], cfg)
    mid = placed[1:5] + [placed[8]]
    mid_send, mid_recv, mid_flying, mid_token = _split_start(mid, _allgather_ici_copies, 3 * len(mid), before=meta_full,
                                                             name="allgather_mid_start")
    ffn_send, ffn_recv, ffn_flying, ffn_token = _split_start(placed[5:7], _allgather_ici_copies, 6, before=mid_token,
                                                             name="allgather_ffn_start")
    mix_norm = w['mix_norm'] + (mid_token[0:1, 0:1] + ffn_token[0:1, 0:1])

    pos = (jnp.arange(lp, dtype=jnp.int32) - PAD).astype(F32)
    inv_freq = 1.0 / (ROPE_BASE ** (jnp.arange(0, QK_ROPE, 2, dtype=F32) / QK_ROPE))
    ang = pos[:, None] * inv_freq[None, :]
    zpad = jnp.zeros((lp, LANE - QK_ROPE), F32)
    cos_t = jnp.concatenate([jnp.cos(ang), jnp.cos(ang), zpad], axis=1)
    sin_t = jnp.concatenate([jnp.sin(ang), jnp.sin(ang), zpad], axis=1)

    s5_in = (w['lam_re'][0], w['lam_im'][0], w['log_dt'][0], w['b_re'][0], w['b_im'][0])
    (a_re, a_im, bb_re, bb_im), s5_vjp = jax.vjp(_s5_discretize, *s5_in)
    lam_dt = lax.complex(s5_in[0], s5_in[1]) * jnp.exp(s5_in[2])[:, None]
    a_pow = jnp.exp(jnp.arange(1, 9, dtype=F32)[:, None, None] * lam_dt[None])
    r8 = jnp.arange(8)
    step_f = jnp.stack([jnp.where((r8 >= k)[:, None, None], a_pow[k - 1][None], 0.0) for k in (1, 2, 4)]).reshape(24, cfg.G, -1)
    step_b = jnp.stack([jnp.where((r8 < 8 - k)[:, None, None], a_pow[k - 1][None], 0.0) for k in (1, 2, 4)]).reshape(24, cfg.G, -1)
    rows_f = jnp.concatenate([a_pow, step_f])
    rows_b = jnp.conj(jnp.concatenate([a_pow[::-1], step_b]))

    def lane_rows(t):
        v = jnp.stack([jnp.real(t), jnp.imag(t)], axis=1).reshape(t.shape[0], 2, cfg.NB, GROUPS_PER_BLOCK, SSM_STATE)
        return jnp.transpose(v, (0, 2, 1, 3, 4)).reshape(t.shape[0], cfg.NL)

    pw_fwd, pw_bwd = lane_rows(rows_f), lane_rows(rows_b)
    bb_band = _bb_band(bb_re, bb_im, cfg).astype(BF16)
    cc_band = _cc_band(w['c_re'][0], w['c_im'][0], cfg).astype(BF16)
    d_skip, b_glu = w['d_skip'], w['b_glu']

    h0 = jnp.concatenate([jnp.zeros((PAD, d), F32), meta, x[0]], axis=0)
    xn = _rms_fwd(h0, mix_norm, name="rms_mix")
    z = _mm(xn, w_in, name="mm_in", tn=_tile(cfg.DINP, 640))
    u = (z, ds, 0)
    q_a = (z, cfg.QL, ds // cfg.QL)
    kv_a = (z, cfg.KVL, (ds + cfg.QL) // cfg.KVL)
    k_pe = (z, LANE, (ds + cfg.QL + cfg.KVL) // LANE)

    hs, yc = _s5_fwd(z, bb_band, cc_band, pw_fwd, cfg, name="s5_fwd")

    def s5_y(ycv, uv, dk):
        return ycv + dk * uv

    gl = _ew(lambda rid, ycv, uv, dk: jax.nn.gelu(s5_y(ycv, uv, dk)), [yc, u], [d_skip], [(ds, BF16)], name="s5_gelu")[0]
    mid_landed = _split_wait(mid_send, mid_recv, mid_flying, _allgather_ici_copies, gl, name="allgather_mid_wait")
    w_glu, w_qt, w_kvt, w_out, conv_full = _allgather_forward(mid_landed, name="allgather_mid_forward")
    conv_w = jnp.transpose(conv_full.reshape(4, ROW_ALIGN, cfg.FQ)[:, :3], (1, 0, 2)).reshape(3, cfg.FP)
    tg = _mm(gl, w_glu, name="mm_glu")
    ya = _ew(lambda rid, ycv, uv, tv, dk, bg: jax.nn.gelu(s5_y(ycv, uv, dk)) * jax.nn.sigmoid(tv + bg),
             [yc, u, tg], [d_skip, b_glu], [(ds, F32)], name="s5_glu")[0]

    qn = _rms_fwd(q_a, w['q_a_norm'], name="rms_q")
    kvn = _rms_fwd(kv_a, w['kv_a_norm'], name="rms_kv")
    q_raw = _mm(qn, w_qt, tb=True, name="mm_q")
    qx = _ew(_rope_heads(_rope, cfg.H), [q_raw, cos_t, sin_t], [], [(cfg.H * HEAD_SLOT, BF16)], name="rope_q")[0]
    kv = _mm(kvn, w_kvt, tb=True, out_dtype=BF16, name="mm_kv")
    kr = _ew(lambda rid, kp, cs, sn: _rope(kp, cs, sn), [k_pe, cos_t, sin_t], [], [(LANE, BF16)], name="rope_k")[0]
    o, lse = _attn_fwd(qx, kv, kr, cfg, name="attn_fwd")

    def norm2(rid, yav, ov, gs, ga):
        return jnp.concatenate([_rms_parts(yav, gs)[0] * gs, _rms_parts(ov, ga)[0] * ga], axis=1)

    yn = _ew(norm2, [ya, o], [w['out_norm_ssm'], w['out_norm_attn']], [(cfg.DMIX, BF16)], name="rms_out")[0]
    h1 = _mm(yn, w_out, res=h0, name="mm_out")
    xn2 = _rms_fwd(h1, w['ffn_norm'], name="rms_ffn")
    ffn_landed = _split_wait(ffn_send, ffn_recv, ffn_flying, _allgather_ici_copies, xn2, name="allgather_ffn_wait")
    w_upt, w_down = _allgather_forward(ffn_landed, name="allgather_ffn_forward")
    up, act = _ffn_up(xn2, w_upt, conv_w, conv_b, name="ffn_up")
    h2 = _mm(act, w_down, res=h1, tm=_tile(lp, 544, ROW_ALIGN), name="mm_down")

    g_final = w['final_norm'].reshape(1, d)

    def head(rid, hv, tv, gv):
        xhat, r = _rms_parts(hv, gv)
        valid = rid >= PAD + N_META
        diff = jnp.where(valid, xhat * gv - tv, 0.0)
        dout = diff * (1.0 / d)
        dxhat = dout * gv
        dx = r * (dxhat - xhat * jnp.mean(dxhat * xhat, axis=-1, keepdims=True))
        return dx, dx, dout * xhat, 0.5 * diff * dout

    dh2, dh2_b, dg_final, loss_cols = _ew(head, [h2, (loss_target[0], d, 0, SKIP)], [g_final], [(d, F32), (d, BF16)], [d, d],
                                          tm=PAD + N_META, name="loss_head")
    loss = lax.psum(jnp.sum(loss_cols), ("x", "y", "c"))

    dw_down = _mm(act, dh2_b, ta=True, tn=d, tm=512, out_dtype=BF16, name="mm_dw_down")

    def sibling_start(g, tag):
        land = lax.empty((4, g.shape[0] // 8, g.shape[1]), g.dtype)
        return _split_start([g, land], _rs_sibling_copies, 4, name=f"rs_sibling_{tag}_start")

    dn_send, dn_recv, dn_flying, dn_token = sibling_start(dw_down, "down")
    dup, dconv_w, dconv_b = _ffn_dact(dh2_b, w_down, up, conv_w, conv_b + dn_token[0:1, 0:1], name="ffn_dact")
    tk_up, tm_up = _tile(cfg.FP, 1408), _tile(cfg.FP, 512)
    dw_upt = _mm(dup, xn2, ta=True, dims=(2 * cfg.FP, d, lp), tn=d, tm=tm_up, a_lead=True, out_dtype=BF16, name="mm_dw_up",
                 a_idx=lambda i, j, k: (i // (cfg.FP // tm_up), 0, i % (cfg.FP // tm_up)))
    up_send, up_recv, up_flying, up_token = sibling_start(dw_upt, "up")
    dxn2 = _mm(dup, w_upt, dims=(lp, d, 2 * cfg.FP), tk=tk_up, tn=1024, a_lead=True, name="mm_dxn2",
               a_idx=lambda i, j, k: (k // (cfg.FP // tk_up), i, k % (cfg.FP // tk_up)))
    dh1, dh1_b, dg_ffn = _rms_bwd(h1, w['ffn_norm'] + up_token[0:1, 0:1], dxn2, res=dh2, mask=True, with_bf16=True,
                                  name="rms_ffn_bwd")

    dyn = _mm(dh1_b, w_out, tb=True, name="mm_dyn")
    dw_out = _mm(yn, dh1_b, ta=True, tn=d, tm=512, name="mm_dw_out")
    up_done = _split_wait(up_send, up_recv, up_flying, _rs_sibling_copies, dw_out, name="rs_sibling_up_wait")
    dn_done = _split_wait(dn_send, dn_recv, dn_flying, _rs_sibling_copies, dw_out, name="rs_sibling_down_wait")
    early_parts, early_sends = _add_halves_all([up_done[0], dn_done[0]], [up_done[1], dn_done[1]], [BF16] * 2, "early")
    chip_lands = [lax.empty((3,) + s.shape[1:], s.dtype) for s in early_sends]
    ch_send, ch_recv, ch_flying, ch_token = _split_start(early_sends + chip_lands, _rs_chips_copies, 6,
                                                         name="rs_chips_early_start")
    dya, dg_ssm = _rms_bwd(ya, w['out_norm_ssm'] + ch_token[0:1, 0:1], (dyn, ds, 0), name="rms_ssm_bwd")
    do, dg_attn = _rms_bwd(o, w['out_norm_attn'], (dyn, cfg.DATTN, ds // cfg.DATTN), name="rms_attn_bwd")

    dqx, dkv, dkr = _attn_bwd(qx, kv, kr, o, lse, do, cfg, name="attn_bwd")
    dq_raw = _ew(_rope_heads(_unrope, cfg.H), [dqx, cos_t, sin_t], [], [(cfg.H * HEAD_SLOT, BF16)], name="unrope_q")[0]
    dk_pe = _ew(lambda rid, dk, cs, sn: _unrope(dk, cs, sn), [dkr, cos_t, sin_t], [], [(LANE, F32)], name="unrope_k")[0]
    dqn = _mm(dq_raw, w_qt, name="mm_dqn")
    dw_qt = _mm(dq_raw, qn, ta=True, tm=512, name="mm_dw_q")
    dkvn = _mm(dkv, w_kvt, name="mm_dkvn")
    dw_kvt = _mm(dkv, kvn, ta=True, tm=512, name="mm_dw_kv")
    dq_a, dg_q = _rms_bwd(q_a, w['q_a_norm'], dqn, name="rms_q_bwd")
    dkv_a, dg_kv = _rms_bwd(kv_a, w['kv_a_norm'], dkvn, name="rms_kv_bwd")

    def glu_bwd(rid, ycv, uv, tv, dyav, dk, bg):
        gelu = jax.nn.gelu(s5_y(ycv, uv, dk))
        sg = jax.nn.sigmoid(tv + bg)
        dt = dyav * gelu * sg * (1.0 - sg)
        return dt, dyav * sg, dt

    dt_b, dgl1, db_glu = _ew(glu_bwd, [yc, u, tg, dya], [d_skip, b_glu], [(ds, BF16), (ds, F32)], [ds], name="s5_glu_bwd")
    dgl = _mm(dt_b, w_glu, tb=True, res=dgl1, name="mm_dgl")
    dw_glu = _mm(gl, dt_b, ta=True, tm=512, name="mm_dw_glu")

    def gelu_bwd(rid, ycv, uv, dglv, dk):
        _, vjp = jax.vjp(jax.nn.gelu, s5_y(ycv, uv, dk))
        dy = vjp(dglv)[0]
        return dy, dy * dk, dy * uv

    mid_grads = [dw_out, dw_glu, dw_qt, dw_kvt]
    mid_lands = [lax.empty((4, g.shape[0] // 8, g.shape[1]), g.dtype) for g in mid_grads]
    ms_send, ms_recv, ms_flying, ms_token = _split_start(mid_grads + mid_lands, _rs_sibling_copies, 4 * len(mid_grads),
                                                         name="rs_sibling_mid_start")
    dy_b, du_skip, dd_skip = _ew(gelu_bwd, [yc, u, dgl], [d_skip + ms_token[0:1, 0:1]], [(ds, BF16), (ds, F32)], [ds],
                                 name="s5_gelu_bwd")
    ms_done = _split_wait(ms_send, ms_recv, ms_flying, _rs_sibling_copies, dy_b, name="rs_sibling_mid_wait")
    mid_parts, mid_sends = _add_halves_all(ms_done[:4], ms_done[4:], [BF16] * 4, "mid")
    mid_chip_lands = [lax.empty((3,) + s.shape[1:], s.dtype) for s in mid_sends]
    mc_send, mc_recv, mc_flying, mc_token = _split_start(mid_sends + mid_chip_lands, _rs_chips_copies, 3 * len(mid_sends),
                                                         name="rs_chips_mid_start")
    du, dbb_band, dcc_band, da_l = _s5_bwd(dy_b, hs, z, bb_band, cc_band, pw_bwd + mc_token[0:1, 0:1], du_skip, cfg,
                                           name="s5_bwd")

    dz = jnp.concatenate([du, dq_a, dkv_a, dk_pe], axis=1).astype(BF16)
    dxn = _mm(dz, w_in, tb=True, name="mm_dxn")
    dw_in = _mm(xn, dz, ta=True, tm=512, tn=_tile(cfg.DINP, 1024), name="mm_dw_in")
    def mix_bwd(rid, xv, dyv, resv, gv):
        dx, dg = _rms_bwd_block(xv, gv, dyv)
        dx = dx + resv
        return dx, dx, dg

    grad_x, dh0_head, dg_mix = _ew(mix_bwd, [h0, dxn, dh1], [mix_norm], [(d, F32, SKIP), (d, F32, FIRST)], [d],
                                   tm=PAD + N_META, name="rms_mix_bwd")
    grad_x = grad_x[None]

    da_re, da_im = _gp_from_lanes(da_l, cfg)
    dbb_re, dbb_im = _bb_from_band(dbb_band, cfg)
    dlam_re, dlam_im, dlog_dt, db_re, db_im = s5_vjp((da_re, da_im, dbb_re, dbb_im))
    dc_re, dc_im = _cc_from_band(dcc_band, cfg)
    local_small = {
        'meta_tokens': dh0_head[PAD:], 'mix_norm': dg_mix, 'lam_re': dlam_re, 'lam_im': dlam_im, 'log_dt': dlog_dt,
        'b_re': db_re, 'b_im': db_im, 'c_re': dc_re, 'c_im': dc_im, 'd_skip': dd_skip, 'b_glu': db_glu, 'q_a_norm': dg_q,
        'kv_a_norm': dg_kv, 'out_norm_ssm': dg_ssm, 'out_norm_attn': dg_attn, 'ffn_norm': dg_ffn,
        'conv_w': _ff_unpad(dconv_w, cfg), 'conv_b': _ff_unpad(dconv_b, cfg), 'final_norm': dg_final,
    }
    small_shapes = [local_small[n].shape for n in SMALL]

    small_pack = _pack([local_small[n] for n in SMALL])
    end_local = [dw_in, small_pack]
    end_recv = _rs_sibling(end_local, name="rs_sibling_end")
    end_parts, end_sends = _add_halves_all(end_local, end_recv, [BF16, F32], "end")
    end_lands = [lax.empty((3,) + s.shape[1:], s.dtype) for s in end_sends]
    ec_send, ec_recv, ec_flying, ec_token = _split_start(end_sends + end_lands, _rs_chips_copies, 3 * len(end_sends),
                                                         name="rs_chips_end_start")
    ch_done = _split_wait(ch_send, ch_recv, ch_flying, _rs_chips_copies, ec_token, name="rs_chips_early_wait")
    red_up, red_down = _rs_finish(early_parts, ch_done[2:], "early")

    delta, new_m, new_v, grads = {}, {}, {}, {}
    padded_rows = ('w_down',)

    def adamw_big(n, red):
        shp = w[n].shape
        w2, m2, v2 = [t.reshape(shp[-2], shp[-1]) for t in (w[n], m[n], v[n])]
        if n in padded_rows:
            g2, dl, mn, vn = _adamw(w2, red, m2, v2, emit_grad=True, name=f"adamw_{n}")
            grads[n] = g2.reshape(shp)
        else:
            grads[n] = _from_comm_layout(n, red, cfg)
            dl, mn, vn = _adamw(w2, grads[n].reshape(shp[-2], shp[-1]), m2, v2, name=f"adamw_{n}")
        delta[n], new_m[n], new_v[n] = dl.reshape(shp), mn.reshape(shp), vn.reshape(shp)

    def adamw_up(red):
        q = cfg.F // 4
        wt, mt, vt = [jnp.transpose(t[0]).reshape(2, q, d) for t in (w['w_up'], m['w_up'], v['w_up'])]
        outs = _adamw(wt, red.reshape(2, cfg.FQ, d), mt, vt, emit_grad=True, name="adamw_w_up")
        grads['w_up'], delta['w_up'], new_m['w_up'], new_v['w_up'] = [jnp.transpose(t.reshape(2 * q, d))[None] for t in outs]

    adamw_up(red_up)
    adamw_big('w_down', red_down)
    mc_done = _split_wait(mc_send, mc_recv, mc_flying, _rs_chips_copies, delta['w_down'], name="rs_chips_mid_wait")
    ec_done = _split_wait(ec_send, ec_recv, ec_flying, _rs_chips_copies, mc_done[0], name="rs_chips_end_wait")
    red = _rs_finish(mid_parts + end_parts, list(mc_done[len(mid_sends):]) + list(ec_done[len(end_sends):]), "rest")
    small_full = _allgather([_place_shard(red[5], F32, name="place_small")], name="allgather_small")[0]
    small_sum = dict(zip(SMALL, _unpack(small_full, small_shapes)))
    for n, r in zip(['w_out', 'w_glu', 'w_q_b', 'w_kv_b'], red[:4]):
        adamw_big(n, r)
    in_t = [jnp.transpose(t[0]) for t in (w['w_in'], m['w_in'], v['w_in'])]
    outs = _adamw(in_t[0], jnp.transpose(red[4][:, :cfg.DIN]), in_t[1], in_t[2], emit_grad=True, name="adamw_w_in")
    grads['w_in'], delta['w_in'], new_m['w_in'], new_v['w_in'] = [jnp.transpose(t)[None] for t in outs]

    for n in SMALL:
        g = small_sum[n]
        if n == 'meta_tokens':
            g = lax.dynamic_slice_in_dim(g, me * (d // 4), d // 4, axis=1)
        elif n == 'conv_w':
            g = lax.dynamic_slice_in_dim(g, me * (cfg.F // 4), cfg.F // 4, axis=1)[None]
        else:
            g = g.reshape(w[n].shape)
        grads[n] = g

    shapes = [w[n].shape for n in SMALL]
    packs = [_pack([src[n] for n in SMALL]) for src in (w, grads, m, v)]
    for dst, p in zip((delta, new_m, new_v), _adamw(*packs, name="adamw_small")):
        dst.update(zip(SMALL, _unpack(p, shapes)))

    return (loss, grad_x, *[grads[n] for n in WEIGHTS], *[delta[n] for n in WEIGHTS],
            *[new_m[n] for n in WEIGHTS], *[new_v[n] for n in WEIGHTS])


def kernel(x, meta_tokens, mix_norm, w_in, lam_re, lam_im, log_dt, b_re, b_im, c_re, c_im, d_skip, w_glu, b_glu, q_a_norm, w_q_b, kv_a_norm, w_kv_b, out_norm_ssm, out_norm_attn, w_out, ffn_norm, w_up, conv_w, conv_b, w_down, final_norm, loss_target, m_meta_tokens, m_mix_norm, m_w_in, m_lam_re, m_lam_im, m_log_dt, m_b_re, m_b_im, m_c_re, m_c_im, m_d_skip, m_w_glu, m_b_glu, m_q_a_norm, m_w_q_b, m_kv_a_norm, m_w_kv_b, m_out_norm_ssm, m_out_norm_attn, m_w_out, m_ffn_norm, m_w_up, m_conv_w, m_conv_b, m_w_down, m_final_norm, v_meta_tokens, v_mix_norm, v_w_in, v_lam_re, v_lam_im, v_log_dt, v_b_re, v_b_im, v_c_re, v_c_im, v_d_skip, v_w_glu, v_b_glu, v_q_a_norm, v_w_q_b, v_kv_a_norm, v_w_kv_b, v_out_norm_ssm, v_out_norm_attn, v_w_out, v_ffn_norm, v_w_up, v_conv_w, v_conv_b, v_w_down, v_final_norm):
    args = dict(locals())
    w = {n: args[n] for n in WEIGHTS}
    m = {n: args["m_" + n] for n in WEIGHTS}
    v = {n: args["v_" + n] for n in WEIGHTS}
    return _step(PROD, w, m, v, x, loss_target)
```

```python
import functools
import math
from typing import NamedTuple

import jax
import jax.numpy as jnp
from jax import lax
from jax.experimental import pallas as pl
from jax.experimental.pallas import tpu as pltpu

F32, BF16 = jnp.float32, jnp.bfloat16
MESH = pl.DeviceIdType.MESH
LANE = 128
ROW_ALIGN = 16
N_META = 16
PAD = 112
CHUNK = 64
SSM_GROUP = 16
SSM_STATE = 64
GROUPS_PER_BLOCK = 8
QK_NOPE, QK_ROPE, V_HEAD = 128, 64, 128
HEAD_SLOT = 256
ROPE_BASE = 10000.0
EPS = 1e-6
ADAM_LR, ADAM_B1, ADAM_B2, ADAM_EPS, ADAM_WD, ADAM_STEP = 0.001, 0.9, 0.999, 1e-08, 0.01, 10
DT_F32_BLOCK_BYTES = 9 << 18
ADAMW_BLOCK_BYTES = 3 << 19
PLACE_BLOCK_BYTES = 6 << 20
SKIP, FIRST = "skip", "first"


class Cfg(NamedTuple):
    D: int
    S: int
    DS: int
    H: int
    QL: int
    KVL: int
    F: int

    @property
    def LP(self):
        return PAD + N_META + self.S

    @property
    def G(self):
        return self.DS // SSM_GROUP

    @property
    def NB(self):
        return self.G // GROUPS_PER_BLOCK

    @property
    def NL(self):
        return 2 * self.G * SSM_STATE

    @property
    def DATTN(self):
        return self.H * V_HEAD

    @property
    def DMIX(self):
        return self.DS + self.DATTN

    @property
    def DIN(self):
        return self.DS + self.QL + self.KVL + QK_ROPE

    @property
    def DINP(self):
        return self.DS + self.QL + self.KVL + LANE

    @property
    def FQ(self):
        return -(-(self.F // 4) // LANE) * LANE

    @property
    def FP(self):
        return 4 * self.FQ


PROD = Cfg(D=2048, S=2048, DS=1024, H=8, QL=512, KVL=256, F=5504)

WEIGHTS = ['meta_tokens', 'mix_norm', 'w_in', 'lam_re', 'lam_im', 'log_dt', 'b_re', 'b_im', 'c_re', 'c_im', 'd_skip',
           'w_glu', 'b_glu', 'q_a_norm', 'w_q_b', 'kv_a_norm', 'w_kv_b', 'out_norm_ssm', 'out_norm_attn', 'w_out',
           'ffn_norm', 'w_up', 'conv_w', 'conv_b', 'w_down', 'final_norm']
BIG = ['w_in', 'w_glu', 'w_q_b', 'w_kv_b', 'w_out', 'w_up', 'w_down']
SMALL = [n for n in WEIGHTS if n not in BIG]


def _pc(body, **kw):
    return pl.pallas_call(body, **kw)


def _tile(n, target, align=LANE):
    best = None
    d = align
    while d <= min(n, target):
        if n % d == 0:
            best = d
        d += align
    return best if best is not None else n


def _row_tile(rows, cols):
    return _tile(rows, max(ROW_ALIGN, DT_F32_BLOCK_BYTES // (4 * cols)), ROW_ALIGN)


def _mm(a, b, *, name, ta=False, tb=False, tm=None, tn=512, tk=None, out_dtype=F32, res=None,
        a_idx=None, b_idx=None, dims=None, a_lead=False):
    if dims is None:
        m, k = (a.shape[1], a.shape[0]) if ta else a.shape
        n = b.shape[0] if tb else b.shape[1]
    else:
        m, n, k = dims
    tm = _tile(m, tm or m, LANE if ta else ROW_ALIGN)
    tn = _tile(n, tn)
    tk = _tile(k, tk or k, ROW_ALIGN if (ta and not tb) else LANE)
    nm, nn, nk = m // tm, n // tn, k // tk
    a_idx = a_idx or ((lambda i, j, kk: (kk, i)) if ta else (lambda i, j, kk: (i, kk)))
    b_idx = b_idx or ((lambda i, j, kk: (j, kk)) if tb else (lambda i, j, kk: (kk, j)))
    dn = (((0 if ta else 1,), (1 if tb else 0,)), ((), ()))

    def body(*refs):
        a_ref, b_ref = refs[0], refs[1]
        r_ref = refs[2] if res is not None else None
        o_ref = refs[3] if res is not None else refs[2]
        d = lax.dot_general(a_ref[...].astype(BF16), b_ref[...].astype(BF16), dn, preferred_element_type=F32)

        def finish(r):
            if r_ref is not None:
                r = r + r_ref[...].astype(F32)
            o_ref[...] = r.astype(out_dtype)

        if nk == 1:
            finish(d)
        else:
            acc = refs[-1]
            kk = pl.program_id(2)

            @pl.when(kk == 0)
            def _():
                acc[...] = d

            @pl.when(kk > 0)
            def _():
                acc[...] += d

            @pl.when(kk == nk - 1)
            def _():
                finish(acc[...])

    a_blk = ((None,) if a_lead else ()) + ((tk, tm) if ta else (tm, tk))
    in_specs = [pl.BlockSpec(a_blk, a_idx), pl.BlockSpec((tn, tk) if tb else (tk, tn), b_idx)]
    args = [a, b]
    if res is not None:
        in_specs.append(pl.BlockSpec((tm, tn), lambda i, j, kk: (i, j)))
        args.append(res)
    return _pc(body, name=name, grid=(nm, nn, nk), in_specs=in_specs,
               out_specs=pl.BlockSpec((tm, tn), lambda i, j, kk: (i, j)),
               out_shape=jax.ShapeDtypeStruct((m, n), out_dtype),
               scratch_shapes=[pltpu.VMEM((tm, tn), F32)] if nk > 1 else [],
               compiler_params=pltpu.CompilerParams(dimension_semantics=("parallel", "parallel", "arbitrary")))(*args)


def _ew(fn, ins, vecs, outs, sums=(), *, name, tm=None):
    ins = [x if isinstance(x, tuple) else (x, x.shape[1], 0) for x in ins]
    ins = [x if len(x) == 4 else x + (None,) for x in ins]
    outs = [o if len(o) == 3 else o + (None,) for o in outs]
    rows = ins[0][0].shape[0]
    cmax = max([c for _, c, _, _ in ins] + [c for c, _, _ in outs])
    tm = tm or _row_tile(rows, cmax)
    n_in, n_vec, n_out, n_sum = len(ins), len(vecs), len(outs), len(sums)

    def body(*refs):
        i = pl.program_id(0)
        rid = i * tm + lax.broadcasted_iota(jnp.int32, (tm, 1), 0)
        vals = [r[...] for r in refs[:n_in + n_vec]]
        res = fn(rid, *vals)
        res = res if isinstance(res, (tuple, list)) else (res,)
        o_refs = refs[n_in + n_vec:]
        for o_ref, r, (_, _, mode) in zip(o_refs[:n_out], res[:n_out], outs):
            if mode == FIRST:
                @pl.when(i == 0)
                def _():
                    o_ref[...] = r.astype(o_ref.dtype)
            else:
                o_ref[...] = r.astype(o_ref.dtype)
        for o_ref, r in zip(o_refs[n_out:], res[n_out:]):
            part = jnp.sum(r.astype(F32), axis=0, keepdims=True)

            @pl.when(i == 0)
            def _():
                o_ref[...] = part

            @pl.when(i > 0)
            def _():
                o_ref[...] += part

    def row_idx(mode):
        if mode == SKIP:
            return lambda i, cb=0: (jnp.maximum(i - 1, 0), cb)
        if mode == FIRST:
            return lambda i, cb=0: (0, cb)
        return lambda i, cb=0: (i, cb)

    in_specs = [pl.BlockSpec((tm, c), functools.partial(row_idx(mode), cb=cb)) for _, c, cb, mode in ins]
    in_specs += [pl.BlockSpec(v.shape, functools.partial(lambda i, nd: (0,) * nd, nd=v.ndim)) for v in vecs]
    out_specs = [pl.BlockSpec((tm, c), row_idx(mode)) for c, _, mode in outs]
    out_specs += [pl.BlockSpec((1, c), lambda i: (0, 0)) for c in sums]
    out_rows = {None: rows, SKIP: rows - tm, FIRST: tm}
    out_shape = [jax.ShapeDtypeStruct((out_rows[mode], c), dt) for c, dt, mode in outs]
    out_shape += [jax.ShapeDtypeStruct((1, c), F32) for c in sums]
    return _pc(body, name=name, grid=(rows // tm,), in_specs=in_specs, out_specs=out_specs, out_shape=out_shape,
               compiler_params=pltpu.CompilerParams(dimension_semantics=("arbitrary",)))(*[x[0] for x in ins], *vecs)


def _rms_parts(x, g):
    r = lax.rsqrt(jnp.mean(x * x, axis=-1, keepdims=True) + EPS)
    return x * r, r


def _rms_bwd_block(x, g, dy):
    xhat, r = _rms_parts(x, g)
    dxhat = dy * g
    dx = r * (dxhat - xhat * jnp.mean(dxhat * xhat, axis=-1, keepdims=True))
    return dx, dy * xhat


def _rms_fwd(x, g, *, name):
    c = x[1] if isinstance(x, tuple) else x.shape[1]
    return _ew(lambda rid, xv, gv: _rms_parts(xv.astype(F32), gv)[0] * gv, [x], [g], [(c, BF16)], name=name)[0]


def _rms_bwd(x, g, dy, *, name, res=None, mask=False, with_bf16=False):
    c = x[1] if isinstance(x, tuple) else x.shape[1]

    def fn(rid, xv, dyv, *rest):
        gv = rest[-1]
        dx, dg = _rms_bwd_block(xv.astype(F32), gv, dyv.astype(F32))
        if res is not None:
            dx = dx + rest[0]
        if mask:
            dx = jnp.where(rid >= PAD, dx, 0.0)
        return (dx, dx, dg) if with_bf16 else (dx, dg)

    ins = [x, dy] + ([res] if res is not None else [])
    outs = [(c, F32)] + ([(c, BF16)] if with_bf16 else [])
    return _ew(fn, ins, [g], outs, [c], name=name)


S5_W = GROUPS_PER_BLOCK * SSM_STATE
S5_GW = GROUPS_PER_BLOCK * SSM_GROUP
S5_UNROLL = 8
S5_DA_ROWS = 272


def _s5_scan_in_place(ref, pw_ref, *, reverse):
    lp = ref.shape[0]
    tile_rows = 8
    chunk = _tile(lp, S5_DA_ROWS, tile_rows)
    tiles = chunk // tile_rows

    def chunk_body(c, carry):
        rows = pl.ds(pl.multiple_of(c * chunk, tile_rows), chunk)
        xr, xi = ref[rows, :S5_W], ref[rows, S5_W:]
        for level, k in enumerate((1, 2, 4)):
            base = tile_rows * (1 + level)
            mr, mi = pw_ref[base:base + tile_rows, :S5_W][None], pw_ref[base:base + tile_rows, S5_W:][None]
            shift = chunk - k if reverse else k
            sr = pltpu.roll(xr, shift, 0).reshape(tiles, tile_rows, S5_W)
            si = pltpu.roll(xi, shift, 0).reshape(tiles, tile_rows, S5_W)
            xr = xr + (mr * sr - mi * si).reshape(chunk, S5_W)
            xi = xi + (mr * si + mi * sr).reshape(chunk, S5_W)
        ref[rows, :S5_W] = xr
        ref[rows, S5_W:] = xi
        return carry

    lax.fori_loop(0, lp // chunk, chunk_body, 0)

    pr, pi = pw_ref[0:tile_rows, :S5_W], pw_ref[0:tile_rows, S5_W:]
    ntile = lp // tile_rows
    unroll = 4

    def step(n, carry):
        cr, ci = carry
        for q in range(unroll):
            j = n * unroll + q
            j = ntile - 1 - j if reverse else j
            rows = pl.ds(pl.multiple_of(j * tile_rows, tile_rows), tile_rows)
            nr = ref[rows, :S5_W] + (pr * cr - pi * ci)
            ni = ref[rows, S5_W:] + (pr * ci + pi * cr)
            ref[rows, :S5_W] = nr
            ref[rows, S5_W:] = ni
            cr, ci = (nr[0:1], ni[0:1]) if reverse else (nr[tile_rows - 1:], ni[tile_rows - 1:])
        return cr, ci

    z = jnp.zeros((1, S5_W), F32)
    lax.fori_loop(0, ntile // unroll, step, (z, z))


def _s5_fwd(z, bb_band, cc_band, a_l, cfg, *, name):
    lp, ds, nl = cfg.LP, cfg.DS, cfg.NL

    def body(u_ref, bb_ref, cc_ref, a_ref, hs_ref, y_ref):
        hs_ref[...] = jnp.dot(u_ref[...].astype(BF16), bb_ref[...], preferred_element_type=F32)
        _s5_scan_in_place(hs_ref, a_ref, reverse=False)
        y_ref[...] = jnp.dot(hs_ref[...].astype(BF16), cc_ref[...], preferred_element_type=F32)

    return _pc(body, name=name, grid=(cfg.NB,),
               in_specs=[pl.BlockSpec((lp, S5_GW), lambda j: (0, j)), pl.BlockSpec((S5_GW, 2 * S5_W), lambda j: (j, 0)),
                         pl.BlockSpec((2 * S5_W, S5_GW), lambda j: (j, 0)), pl.BlockSpec((32, 2 * S5_W), lambda j: (0, j))],
               out_specs=[pl.BlockSpec((lp, 2 * S5_W), lambda j: (0, j)), pl.BlockSpec((lp, S5_GW), lambda j: (0, j))],
               out_shape=[jax.ShapeDtypeStruct((lp, nl), F32), jax.ShapeDtypeStruct((lp, ds), F32)],
               compiler_params=pltpu.CompilerParams(dimension_semantics=("parallel",)))(z, bb_band, cc_band, a_l)


def _s5_bwd(dy, hs, z, bb_band, cc_band, a_l, du_skip, cfg, *, name):
    lp, ds, nl = cfg.LP, cfg.DS, cfg.NL
    nt = (((1,), (1,)), ((), ()))
    tn = (((0,), (0,)), ((), ()))

    def body(dy_ref, hs_ref, u_ref, bb_ref, cc_ref, a_ref, sk_ref, du_ref, dbb_ref, dcc_ref, da_ref, g_ref):
        dyv = dy_ref[...]
        g_ref[...] = lax.dot_general(dyv, cc_ref[...], nt, preferred_element_type=F32)
        _s5_scan_in_place(g_ref, a_ref, reverse=True)
        dcc_ref[...] = lax.dot_general(hs_ref[...].astype(BF16), dyv, tn, preferred_element_type=F32)
        gb = g_ref[...].astype(BF16)
        dbb_ref[...] = lax.dot_general(u_ref[...].astype(BF16), gb, tn, preferred_element_type=F32)
        du_ref[...] = lax.dot_general(gb, bb_ref[...], nt, preferred_element_type=F32) + sk_ref[...]
        dre = jnp.zeros((1, S5_W), F32)
        dim = jnp.zeros((1, S5_W), F32)
        for r0 in range(0, lp, S5_DA_ROWS):
            rows = min(S5_DA_ROWS, lp - r0)
            first = lax.broadcasted_iota(jnp.int32, (rows, 1), 0) == 0
            prev = hs_ref[r0 - 1:r0, :] if r0 else jnp.zeros((1, 2 * S5_W), F32)
            hr = jnp.where(first, prev[:, :S5_W], pltpu.roll(hs_ref[r0:r0 + rows, :S5_W], 1, 0))
            hi = jnp.where(first, prev[:, S5_W:], pltpu.roll(hs_ref[r0:r0 + rows, S5_W:], 1, 0))
            gr, gi = g_ref[r0:r0 + rows, :S5_W], g_ref[r0:r0 + rows, S5_W:]
            dre = dre + jnp.sum(gr * hr + gi * hi, axis=0, keepdims=True)
            dim = dim + jnp.sum(gi * hr - gr * hi, axis=0, keepdims=True)
        da_ref[:, :S5_W] = dre
        da_ref[:, S5_W:] = dim

    col_blk = pl.BlockSpec((lp, S5_GW), lambda j: (0, j))
    lane_blk = pl.BlockSpec((lp, 2 * S5_W), lambda j: (0, j))
    bb_blk = pl.BlockSpec((S5_GW, 2 * S5_W), lambda j: (j, 0))
    cc_blk = pl.BlockSpec((2 * S5_W, S5_GW), lambda j: (j, 0))
    a_blk = pl.BlockSpec((1, 2 * S5_W), lambda j: (0, j))
    pw_blk = pl.BlockSpec((32, 2 * S5_W), lambda j: (0, j))
    return _pc(body, name=name, grid=(cfg.NB,),
               in_specs=[col_blk, lane_blk, col_blk, bb_blk, cc_blk, pw_blk, col_blk],
               out_specs=[col_blk, bb_blk, cc_blk, a_blk],
               out_shape=[jax.ShapeDtypeStruct((lp, ds), F32), jax.ShapeDtypeStruct((ds, 2 * S5_W), F32),
                          jax.ShapeDtypeStruct((nl, S5_GW), F32), jax.ShapeDtypeStruct((1, nl), F32)],
               scratch_shapes=[pltpu.VMEM((lp, 2 * S5_W), F32)],
               compiler_params=pltpu.CompilerParams(dimension_semantics=("parallel",)))(dy, hs, z, bb_band, cc_band, a_l, du_skip)


def _conv_gate(pre, cw, cb):
    return cw[0:1] * pltpu.roll(pre, 2, 0) + cw[1:2] * pltpu.roll(pre, 1, 0) + cw[2:3] * pre + cb


def _ffn_up(xn2, w_upt, cw, cb, *, name):
    lp, d = xn2.shape
    fp = w_upt.shape[0] // 2
    tc = _tile(fp, 256)
    nb = fp // tc

    def body(x_ref, wg_ref, wv_ref, cw_ref, cb_ref, up_ref, act_ref):
        wcat = jnp.concatenate([wg_ref[...], wv_ref[...]], axis=0)
        r = lax.dot_general(x_ref[...], wcat, (((1,), (1,)), ((), ())), preferred_element_type=F32)
        pre, val = r[:, :tc].astype(BF16), r[:, tc:].astype(BF16)
        up_ref[0] = pre
        up_ref[1] = val
        gate = _conv_gate(pre.astype(F32), cw_ref[...], cb_ref[...])
        act_ref[...] = (jax.nn.silu(gate) * val.astype(F32)).astype(BF16)

    return _pc(body, name=name, grid=(nb,),
               in_specs=[pl.BlockSpec((lp, d), lambda j: (0, 0)), pl.BlockSpec((tc, d), lambda j: (j, 0)),
                         pl.BlockSpec((tc, d), lambda j: (nb + j, 0)),
                         pl.BlockSpec((3, tc), lambda j: (0, j)), pl.BlockSpec((1, tc), lambda j: (0, j))],
               out_specs=[pl.BlockSpec((2, lp, tc), lambda j: (0, 0, j)), pl.BlockSpec((lp, tc), lambda j: (0, j))],
               out_shape=[jax.ShapeDtypeStruct((2, lp, fp), BF16), jax.ShapeDtypeStruct((lp, fp), BF16)],
               compiler_params=pltpu.CompilerParams(dimension_semantics=("parallel",)))(xn2, w_upt, w_upt, cw, cb)


def _ffn_dact(dh2, w_down, up, cw, cb, *, name):
    lp, d = dh2.shape
    fp = w_down.shape[0]
    tc = _tile(fp, 256)
    nb = fp // tc

    def body(dh_ref, wd_ref, up_ref, cw_ref, cb_ref, dup_ref, dcw_ref, dcb_ref):
        da = lax.dot_general(dh_ref[...], wd_ref[...], (((1,), (1,)), ((), ())), preferred_element_type=F32)
        pre, val, cwv = up_ref[0].astype(F32), up_ref[1].astype(F32), cw_ref[...]
        gate = _conv_gate(pre, cwv, cb_ref[...])
        sg = jax.nn.sigmoid(gate)
        dup_ref[1] = (da * (gate * sg)).astype(BF16)
        dgate = da * val * (sg * (1.0 + gate * (1.0 - sg)))
        dpre = cwv[2:3] * dgate + cwv[1:2] * pltpu.roll(dgate, lp - 1, 0) + cwv[0:1] * pltpu.roll(dgate, lp - 2, 0)
        dup_ref[0] = dpre.astype(BF16)
        dcb_ref[...] = jnp.sum(dgate, axis=0, keepdims=True)
        dcw_ref[0:1, :] = jnp.sum(dgate * pltpu.roll(pre, 2, 0), axis=0, keepdims=True)
        dcw_ref[1:2, :] = jnp.sum(dgate * pltpu.roll(pre, 1, 0), axis=0, keepdims=True)
        dcw_ref[2:3, :] = jnp.sum(dgate * pre, axis=0, keepdims=True)

    return _pc(body, name=name, grid=(nb,),
               in_specs=[pl.BlockSpec((lp, d), lambda j: (0, 0)), pl.BlockSpec((tc, d), lambda j: (j, 0)),
                         pl.BlockSpec((2, lp, tc), lambda j: (0, 0, j)),
                         pl.BlockSpec((3, tc), lambda j: (0, j)), pl.BlockSpec((1, tc), lambda j: (0, j))],
               out_specs=[pl.BlockSpec((2, lp, tc), lambda j: (0, 0, j)),
                          pl.BlockSpec((3, tc), lambda j: (0, j)), pl.BlockSpec((1, tc), lambda j: (0, j))],
               out_shape=[jax.ShapeDtypeStruct((2, lp, fp), BF16), jax.ShapeDtypeStruct((3, fp), F32),
                          jax.ShapeDtypeStruct((1, fp), F32)],
               compiler_params=pltpu.CompilerParams(dimension_semantics=("parallel",)))(dh2, w_down, up, cw, cb)


def _key_limit(i, tq, lp):
    return min(lp, -(-((i + 1) * tq) // LANE) * LANE)


def _attn_mask(i, tq, nk):
    qrow = i * tq + lax.broadcasted_iota(jnp.int32, (tq, 1), 0)
    krow = lax.broadcasted_iota(jnp.int32, (1, nk), 1)
    return (krow >= PAD) & ((krow // CHUNK) <= (qrow // CHUNK)), qrow >= PAD


def _attn_scores(q, kn, kr, i, tq, scale):
    nt = (((1,), (1,)), ((), ()))
    s = lax.dot_general(q[:, :QK_NOPE], kn, nt, preferred_element_type=F32)
    s = s + lax.dot_general(q[:, QK_NOPE:], kr, nt, preferred_element_type=F32)
    mask, qvalid = _attn_mask(i, tq, kn.shape[0])
    return jnp.where(mask, s * scale, jnp.finfo(F32).min), qvalid


def _per_q_block(nq, fn):
    i = pl.program_id(1)
    for blk in range(nq):
        pl.when(i == blk)(functools.partial(fn, blk))


def _attn_fwd(qx, kv, kr, cfg, *, name):
    lp, h = cfg.LP, cfg.H
    tq = _tile(lp, 272, ROW_ALIGN)
    nq = lp // tq
    scale = 1.0 / math.sqrt(QK_NOPE + QK_ROPE)

    def body(q_ref, kn_ref, v_ref, kr_ref, o_ref, lse_ref):
        def block(blk):
            nk = _key_limit(blk, tq, lp)
            s, qvalid = _attn_scores(q_ref[...], kn_ref[:nk], kr_ref[:nk], blk, tq, scale)
            m = jnp.max(s, axis=-1, keepdims=True)
            p = jnp.exp(s - m)
            l = jnp.sum(p, axis=-1, keepdims=True)
            o = jnp.dot(p.astype(BF16), v_ref[:nk], preferred_element_type=F32) / l
            o_ref[...] = jnp.where(qvalid, o, 0.0)
            lse_ref[...] = m + jnp.log(l)

        _per_q_block(nq, block)

    return _pc(body, name=name, grid=(h, nq),
               in_specs=[pl.BlockSpec((tq, HEAD_SLOT), lambda hh, i: (i, hh)),
                         pl.BlockSpec((lp, QK_NOPE), lambda hh, i: (0, 2 * hh)),
                         pl.BlockSpec((lp, V_HEAD), lambda hh, i: (0, 2 * hh + 1)),
                         pl.BlockSpec((lp, LANE), lambda hh, i: (0, 0))],
               out_specs=[pl.BlockSpec((tq, V_HEAD), lambda hh, i: (i, hh)),
                          pl.BlockSpec((None, tq, 1), lambda hh, i: (hh, i, 0))],
               out_shape=[jax.ShapeDtypeStruct((lp, h * V_HEAD), F32), jax.ShapeDtypeStruct((h, lp, 1), F32)],
               compiler_params=pltpu.CompilerParams(dimension_semantics=("parallel", "parallel")))(qx, kv, kv, kr)


def _attn_bwd(qx, kv, kr, o, lse, do, cfg, *, name):
    lp, h = cfg.LP, cfg.H
    tq = _tile(lp, 272, ROW_ALIGN)
    nq = lp // tq
    scale = 1.0 / math.sqrt(QK_NOPE + QK_ROPE)
    tn_dims = (((0,), (0,)), ((), ()))

    def body(q_ref, kn_ref, v_ref, kr_ref, o_ref, lse_ref, do_ref, dq_ref, dkv_ref, dkr_ref, dkv_acc):
        hh, i = pl.program_id(0), pl.program_id(1)

        @pl.when(i == 0)
        def _():
            dkv_acc[...] = jnp.zeros_like(dkv_acc)

        @pl.when((i == 0) & (hh == 0))
        def _():
            dkr_ref[...] = jnp.zeros_like(dkr_ref)

        def block(blk):
            nk = _key_limit(blk, tq, lp)
            q, kn, v, krv = q_ref[...], kn_ref[:nk], v_ref[:nk], kr_ref[:nk]
            s, qvalid = _attn_scores(q, kn, krv, blk, tq, scale)
            dov = jnp.where(qvalid, do_ref[...], 0.0)
            p = jnp.exp(s - lse_ref[...])
            delta = jnp.sum(dov * o_ref[...], axis=-1, keepdims=True)
            dob = dov.astype(BF16)
            dp = lax.dot_general(dob, v, (((1,), (1,)), ((), ())), preferred_element_type=F32)
            ds = (p * (dp - delta) * scale).astype(BF16)
            dq_ref[:, :QK_NOPE] = jnp.dot(ds, kn, preferred_element_type=F32)
            dq_ref[:, QK_NOPE:] = jnp.dot(ds, krv, preferred_element_type=F32)
            dkv_acc[:nk, :QK_NOPE] += lax.dot_general(ds, q[:, :QK_NOPE], tn_dims, preferred_element_type=F32)
            dkv_acc[:nk, QK_NOPE:] += lax.dot_general(p.astype(BF16), dob, tn_dims, preferred_element_type=F32)
            dkr_ref[:nk, :] += lax.dot_general(ds, q[:, QK_NOPE:], tn_dims, preferred_element_type=F32)

        _per_q_block(nq, block)

        @pl.when(i == nq - 1)
        def _():
            dkv_ref[...] = dkv_acc[...].astype(BF16)

    return _pc(body, name=name, grid=(h, nq),
               in_specs=[pl.BlockSpec((tq, HEAD_SLOT), lambda hh, i: (i, hh)),
                         pl.BlockSpec((lp, QK_NOPE), lambda hh, i: (0, 2 * hh)),
                         pl.BlockSpec((lp, V_HEAD), lambda hh, i: (0, 2 * hh + 1)),
                         pl.BlockSpec((lp, LANE), lambda hh, i: (0, 0)),
                         pl.BlockSpec((tq, V_HEAD), lambda hh, i: (i, hh)),
                         pl.BlockSpec((None, tq, 1), lambda hh, i: (hh, i, 0)),
                         pl.BlockSpec((tq, V_HEAD), lambda hh, i: (i, hh))],
               out_specs=[pl.BlockSpec((tq, HEAD_SLOT), lambda hh, i: (i, hh)),
                          pl.BlockSpec((lp, QK_NOPE + V_HEAD), lambda hh, i: (0, hh)),
                          pl.BlockSpec((lp, LANE), lambda hh, i: (0, 0))],
               out_shape=[jax.ShapeDtypeStruct((lp, h * HEAD_SLOT), F32),
                          jax.ShapeDtypeStruct((lp, h * (QK_NOPE + V_HEAD)), BF16),
                          jax.ShapeDtypeStruct((lp, LANE), F32)],
               scratch_shapes=[pltpu.VMEM((lp, QK_NOPE + V_HEAD), F32)],
               compiler_params=pltpu.CompilerParams(dimension_semantics=("arbitrary", "arbitrary")))(qx, kv, kv, kr, o, lse, do)


def _rot_half(x):
    lane = lax.broadcasted_iota(jnp.int32, x.shape, 1)
    half = QK_ROPE // 2
    return jnp.where(lane < half, -pltpu.roll(x, LANE - half, 1), pltpu.roll(x, half, 1))


def _rope(x, cos, sin):
    return x * cos + _rot_half(x) * sin


def _unrope(dy, cos, sin):
    return dy * cos - _rot_half(dy * sin)


def _rope_heads(fn, h):
    def apply(rid, q, cos, sin):
        parts = []
        for hh in range(h):
            parts.append(q[:, hh * HEAD_SLOT: hh * HEAD_SLOT + QK_NOPE])
            parts.append(fn(q[:, hh * HEAD_SLOT + QK_NOPE: (hh + 1) * HEAD_SLOT], cos, sin))
        return jnp.concatenate(parts, axis=1)
    return apply


ANY = pl.BlockSpec(memory_space=pl.ANY)


def _place():
    x, y, c = lax.axis_index("x"), lax.axis_index("y"), lax.axis_index("c")
    chips = [(1 - x, y), (x, 1 - y), (1 - x, 1 - y)]
    return x, y, c, chips


def _rcopy(src, dst, send_sem, recv_sem, dev):
    return pltpu.make_async_remote_copy(src_ref=src, dst_ref=dst, send_sem=send_sem, recv_sem=recv_sem,
                                        device_id=dev, device_id_type=MESH)


def _place_shard(shard, dtype, *, name, order=None, rows_to=None):
    shard = shard if shard.ndim == 3 else shard[None]
    n, r, cols = shard.shape
    rp = rows_to or r
    tm = _tile(r, max(ROW_ALIGN, PLACE_BLOCK_BYTES // (4 * cols)), ROW_ALIGN)
    me = (2 * lax.axis_index("x") + lax.axis_index("y")).astype(jnp.int32).reshape(1)
    extra = [] if order is None else [order]

    def body(me_ref, s_ref, *rest):
        rest[-1][...] = s_ref[...].astype(dtype)

    full = _pc(body, name=name,
               grid_spec=pltpu.PrefetchScalarGridSpec(
                   num_scalar_prefetch=1, grid=(n, r // tm),
                   in_specs=[pl.BlockSpec((None, tm, cols), lambda q, i, mr: (q, i, 0))] + [ANY] * len(extra),
                   out_specs=pl.BlockSpec((None, tm, cols), lambda q, i, mr: (mr[0] * n + q, i, 0))),
               out_shape=jax.ShapeDtypeStruct((4 * n, rp, cols), dtype),
               compiler_params=pltpu.CompilerParams(dimension_semantics=("arbitrary", "arbitrary")))(me, shard, *extra)
    if rp > r:
        pad = rp - r
        assert r % pad == 0

        def zero(me_ref, f_ref, o_ref):
            o_ref[...] = jnp.zeros_like(o_ref)

        full = _pc(zero, name=name + "_pad",
                   grid_spec=pltpu.PrefetchScalarGridSpec(
                       num_scalar_prefetch=1, grid=(n,), in_specs=[ANY],
                       out_specs=pl.BlockSpec((None, pad, cols), lambda q, mr: (mr[0] * n + q, r // pad, 0))),
                   out_shape=jax.ShapeDtypeStruct(full.shape, dtype), input_output_aliases={1: 0},
                   compiler_params=pltpu.CompilerParams(dimension_semantics=("arbitrary",)))(me, full)
    return full.reshape(4 * n * rp, cols)


def _allgather(fulls, *, name):
    n = len(fulls)

    def body(*refs):
        outs = refs[n:2 * n]
        send_sems, recv_sems = refs[2 * n:]
        x, y, c, chips = _place()
        sib = (x, y, 1 - c)
        me = 2 * x + y

        def rows(t, s, half):
            hrows = outs[t].shape[0] // 8
            return outs[t].at[pl.ds((2 * s + half) * hrows, hrows)]

        sent = []
        for t in range(n):
            for j, (cx, cy) in enumerate(chips):
                cp = _rcopy(rows(t, me, c), rows(t, me, c), send_sems.at[6 * t + j], recv_sems.at[6 * t + j], (cx, cy, c))
                cp.start()
                sent.append(cp)
        for t in range(n):
            for j, (cx, cy) in enumerate(chips):
                landed = rows(t, 2 * cx + cy, c)
                _rcopy(landed, landed, send_sems.at[6 * t + j], recv_sems.at[6 * t + j], (cx, cy, c)).wait_recv()
                cp = _rcopy(landed, landed, send_sems.at[6 * t + 3 + j], recv_sems.at[6 * t + 3 + j], sib)
                cp.start()
                sent.append(cp)
        for t in range(n):
            for j, (cx, cy) in enumerate(chips):
                other = rows(t, 2 * cx + cy, 1 - c)
                _rcopy(other, other, send_sems.at[6 * t + 3 + j], recv_sems.at[6 * t + 3 + j], sib).wait_recv()
        for cp in sent:
            cp.wait_send()

    return _pc(body, name=name, in_specs=[ANY] * n, out_specs=[ANY] * n,
               out_shape=[jax.ShapeDtypeStruct(f.shape, f.dtype) for f in fulls],
               input_output_aliases={t: t for t in range(n)},
               scratch_shapes=[pltpu.SemaphoreType.DMA((6 * n,)), pltpu.SemaphoreType.DMA((6 * n,))])(*fulls)


HBM = pl.BlockSpec(memory_space=pltpu.HBM)
SEM = pl.BlockSpec(memory_space=pltpu.SEMAPHORE)
EFFECT = pltpu.SideEffectType.DATAFLOW_SIDE_EFFECTING
TOKEN = jax.ShapeDtypeStruct((8, LANE), F32)


def _in_hbm(a):
    return pltpu.with_memory_space_constraint(a, pltpu.HBM)


def _half_rows(ref, s, half):
    hrows = ref.shape[0] // 8
    return ref.at[pl.ds((2 * s + half) * hrows, hrows)]


def _split_start(bufs, copies, n_copies, *, name, before=None):
    n = len(bufs)
    extra = [] if before is None else [before]

    def body(*refs):
        send_sems, recv_sems, token = refs[n + len(extra)], refs[n + len(extra) + 1], refs[-1]
        for k, (src, dst, dev) in enumerate(copies(refs[:n])):
            _rcopy(src, dst, send_sems.at[k], recv_sems.at[k], dev).start()
        token[...] = jnp.zeros_like(token)

    res = _pc(body, name=name, in_specs=[HBM] * n + [ANY] * len(extra),
              out_specs=[SEM, SEM] + [HBM] * n + [pl.BlockSpec(memory_space=pltpu.VMEM)],
              out_shape=[pltpu.SemaphoreType.DMA((n_copies,)), pltpu.SemaphoreType.DMA((n_copies,))]
              + [pltpu.HBM(b.shape, b.dtype) for b in bufs] + [TOKEN],
              input_output_aliases={t: 2 + t for t in range(n)},
              compiler_params=pltpu.CompilerParams(has_side_effects=EFFECT))(*[_in_hbm(b) for b in bufs], *extra)
    return res[0], res[1], list(res[2:2 + n]), res[-1]


def _split_wait(send_sems, recv_sems, bufs, copies, after, *, name):
    n = len(bufs)

    def body(*refs):
        send_ref, recv_ref = refs[n], refs[n + 1]
        for k, (src, dst, dev) in enumerate(copies(refs[:n])):
            cp = _rcopy(src, dst, send_ref.at[k], recv_ref.at[k], dev)
            cp.wait_send()
            cp.wait_recv()

    return _pc(body, name=name, in_specs=[HBM] * n + [SEM, SEM, ANY], out_specs=[HBM] * n,
               out_shape=[pltpu.HBM(b.shape, b.dtype) for b in bufs],
               input_output_aliases={t: t for t in range(n)},
               compiler_params=pltpu.CompilerParams(has_side_effects=EFFECT))(*bufs, send_sems, recv_sems, after)


def _allgather_ici_copies(refs):
    x, y, c, chips = _place()
    return [(_half_rows(r, 2 * x + y, c), _half_rows(r, 2 * x + y, c), (cx, cy, c)) for r in refs for cx, cy in chips]


def _rs_chips_copies(refs):
    x, y, c, chips = _place()
    n = len(refs) // 2
    return [(refs[t].at[2 * cx + cy], refs[n + t].at[j], (cx, cy, c)) for t in range(n) for j, (cx, cy) in enumerate(chips)]


def _rs_sibling_copies(refs):
    x, y, c, _ = _place()
    n = len(refs) // 2
    out = []
    for t in range(n):
        h = refs[t].shape[0] // 8
        out += [(refs[t].at[pl.ds((2 * s + 1 - c) * h, h)], refs[n + t].at[s], (x, y, 1 - c)) for s in range(4)]
    return out


def _allgather_forward(fulls, *, name):
    n = len(fulls)

    def body(*refs):
        outs = refs[n:2 * n]
        send_sems, recv_sems = refs[2 * n:]
        x, y, c, chips = _place()
        sent = []
        for t in range(n):
            for j, (cx, cy) in enumerate(chips):
                landed = _half_rows(outs[t], 2 * cx + cy, c)
                cp = _rcopy(landed, landed, send_sems.at[3 * t + j], recv_sems.at[3 * t + j], (x, y, 1 - c))
                cp.start()
                sent.append(cp)
        for t in range(n):
            for j, (cx, cy) in enumerate(chips):
                other = _half_rows(outs[t], 2 * cx + cy, 1 - c)
                _rcopy(other, other, send_sems.at[3 * t + j], recv_sems.at[3 * t + j], (x, y, 1 - c)).wait_recv()
        for cp in sent:
            cp.wait_send()

    return _pc(body, name=name, in_specs=[ANY] * n, out_specs=[ANY] * n,
               out_shape=[jax.ShapeDtypeStruct(f.shape, f.dtype) for f in fulls],
               input_output_aliases={t: t for t in range(n)},
               scratch_shapes=[pltpu.SemaphoreType.DMA((3 * n,)), pltpu.SemaphoreType.DMA((3 * n,))])(*fulls)


def _rs_sibling(grads, *, name):
    n = len(grads)

    def body(*refs):
        ins, outs = refs[:n], refs[n:2 * n]
        send_sems, recv_sems = refs[2 * n:]
        x, y, c, _ = _place()
        cps = []
        for t in range(n):
            h = ins[t].shape[0] // 8
            for s in range(4):
                cp = _rcopy(ins[t].at[pl.ds((2 * s + 1 - c) * h, h)], outs[t].at[s], send_sems.at[4 * t + s],
                            recv_sems.at[4 * t + s], (x, y, 1 - c))
                cp.start()
                cps.append(cp)
        for cp in cps:
            cp.wait()

    return _pc(body, name=name, in_specs=[ANY] * n, out_specs=[ANY] * n,
               out_shape=[jax.ShapeDtypeStruct((4, g.shape[0] // 8, g.shape[1]), g.dtype) for g in grads],
               scratch_shapes=[pltpu.SemaphoreType.DMA((4 * n,)), pltpu.SemaphoreType.DMA((4 * n,))])(*grads)


def _rs_chips(sends, *, name):
    n = len(sends)

    def body(*refs):
        s_refs, b_refs = refs[:n], refs[n:2 * n]
        send_sems, recv_sems = refs[2 * n:]
        x, y, c, chips = _place()
        cps = []
        for t in range(n):
            for j, (cx, cy) in enumerate(chips):
                cp = _rcopy(s_refs[t].at[2 * cx + cy], b_refs[t].at[j], send_sems.at[3 * t + j], recv_sems.at[3 * t + j],
                            (cx, cy, c))
                cp.start()
                cps.append(cp)
        for cp in cps:
            cp.wait()

    return _pc(body, name=name, in_specs=[ANY] * n, out_specs=[ANY] * n,
               out_shape=[jax.ShapeDtypeStruct((3,) + s.shape[1:], s.dtype) for s in sends],
               scratch_shapes=[pltpu.SemaphoreType.DMA((3 * n,)), pltpu.SemaphoreType.DMA((3 * n,))])(*sends)


def _rs_final(fulls, *, name):
    n = len(fulls)

    def body(*refs):
        outs = refs[n:2 * n]
        send_sems, recv_sems = refs[2 * n:]
        x, y, c, _ = _place()
        cps = []
        for t in range(n):
            cp = _rcopy(outs[t].at[c], outs[t].at[c], send_sems.at[t], recv_sems.at[t], (x, y, 1 - c))
            cp.start()
            cps.append(cp)
        for cp in cps:
            cp.wait()

    return _pc(body, name=name, in_specs=[ANY] * n, out_specs=[ANY] * n,
               out_shape=[jax.ShapeDtypeStruct(f.shape, f.dtype) for f in fulls],
               input_output_aliases={t: t for t in range(n)},
               scratch_shapes=[pltpu.SemaphoreType.DMA((n,)), pltpu.SemaphoreType.DMA((n,))])(*fulls)


def _add_halves(g, a, send_dtype, *, name):
    _, h, cols = a.shape
    th = _row_tile(h, cols)
    g4 = g.reshape(4, 2, h, cols)
    idx = jnp.stack([lax.axis_index("c"), 2 * lax.axis_index("x") + lax.axis_index("y")]).astype(jnp.int32)

    def shard(k, ir):
        return (ir[1] + 1 + k) % 4

    def body(idx_ref, g_ref, a_ref, p_ref, s_ref):
        v = g_ref[...].astype(F32) + a_ref[...].astype(F32)
        s_ref[...] = v.astype(send_dtype)

        @pl.when(pl.program_id(1) == 3)
        def _():
            p_ref[...] = v

    return _pc(body, name=name,
               grid_spec=pltpu.PrefetchScalarGridSpec(
                   num_scalar_prefetch=1, grid=(h // th, 4),
                   in_specs=[pl.BlockSpec((None, None, th, cols), lambda i, k, ir: (shard(k, ir), ir[0], i, 0)),
                             pl.BlockSpec((None, th, cols), lambda i, k, ir: (shard(k, ir), i, 0))],
                   out_specs=[pl.BlockSpec((th, cols), lambda i, k, ir: (i, 0)),
                              pl.BlockSpec((None, th, cols), lambda i, k, ir: (shard(k, ir), i, 0))]),
               out_shape=[jax.ShapeDtypeStruct((h, cols), F32), jax.ShapeDtypeStruct(a.shape, send_dtype)],
               compiler_params=pltpu.CompilerParams(dimension_semantics=("arbitrary", "arbitrary")))(idx, g4, a)


def _add_chips(p, b, *, name, order=None):
    h, cols = p.shape
    th = _row_tile(h, cols)
    idx = lax.axis_index("c").astype(jnp.int32).reshape(1)
    extra = [] if order is None else [order]

    def body(idx_ref, p_ref, b_ref, *rest):
        r_ref = rest[-1]
        r_ref[...] = ((p_ref[...] + b_ref[0].astype(F32)) + b_ref[1].astype(F32)) + b_ref[2].astype(F32)

    return _pc(body, name=name,
               grid_spec=pltpu.PrefetchScalarGridSpec(
                   num_scalar_prefetch=1, grid=(h // th,),
                   in_specs=[pl.BlockSpec((th, cols), lambda i, ir: (i, 0)),
                             pl.BlockSpec((3, th, cols), lambda i, ir: (0, i, 0))] + [ANY] * len(extra),
                   out_specs=pl.BlockSpec((None, th, cols), lambda i, ir: (ir[0], i, 0))),
               out_shape=jax.ShapeDtypeStruct((2, h, cols), F32),
               compiler_params=pltpu.CompilerParams(dimension_semantics=("arbitrary",)))(idx, p, b, *extra)


def _add_halves_all(grads, recv, send_dtypes, tag):
    parts, sends = [], []
    for t, (g, a) in enumerate(zip(grads, recv)):
        p, s = _add_halves(g, a, send_dtypes[t], name=f"rs_add_halves_{tag}{t}")
        parts.append(p)
        sends.append(s)
    return parts, sends


def _rs_finish(parts, others, tag, order=None):
    halves = [_add_chips(p, b, order=order, name=f"rs_add_chips_{tag}{t}") for t, (p, b) in enumerate(zip(parts, others))]
    full = _rs_final(halves, name=f"rs_final_{tag}")
    return [f.reshape(-1, f.shape[-1]) for f in full]


def _s5_discretize(lam_re, lam_im, log_dt, b_re, b_im):
    lam = lax.complex(lam_re, lam_im)
    dt = jnp.exp(log_dt)[:, None]
    lam_bar = jnp.exp(lam * dt)
    b_bar = ((lam_bar - 1.0) / lam)[..., None] * lax.complex(b_re, b_im)
    return jnp.real(lam_bar), jnp.imag(lam_bar), jnp.real(b_bar), jnp.imag(b_bar)


def _lanes_from_gp(re, im, cfg):
    v = jnp.stack([re, im]).reshape(2, cfg.NB, GROUPS_PER_BLOCK, SSM_STATE)
    return jnp.transpose(v, (1, 0, 2, 3)).reshape(1, cfg.NL)


def _gp_from_lanes(v, cfg):
    v = jnp.transpose(v.reshape(cfg.NB, 2, GROUPS_PER_BLOCK, SSM_STATE), (1, 0, 2, 3)).reshape(2, cfg.G, SSM_STATE)
    return v[0], v[1]


def _bb_band(bb_re, bb_im, cfg):
    eye = jnp.eye(GROUPS_PER_BLOCK, dtype=F32)
    bb = jnp.stack([bb_re, bb_im]).reshape(2, cfg.NB, GROUPS_PER_BLOCK, SSM_STATE, SSM_GROUP)
    return jnp.einsum('rjgpc,gh->jgcrhp', bb, eye).reshape(cfg.DS, 2 * GROUPS_PER_BLOCK * SSM_STATE)


def _bb_from_band(m, cfg):
    eye = jnp.eye(GROUPS_PER_BLOCK, dtype=F32)
    m = m.reshape(cfg.NB, GROUPS_PER_BLOCK, SSM_GROUP, 2, GROUPS_PER_BLOCK, SSM_STATE)
    v = jnp.einsum('jgcrhp,gh->rjgpc', m, eye).reshape(2, cfg.G, SSM_STATE, SSM_GROUP)
    return v[0], v[1]


def _cc_band(c_re, c_im, cfg):
    eye = jnp.eye(GROUPS_PER_BLOCK, dtype=F32)
    cc = jnp.stack([c_re, -c_im]).reshape(2, cfg.NB, GROUPS_PER_BLOCK, SSM_GROUP, SSM_STATE)
    return jnp.einsum('rjgcp,gh->jrhpgc', cc, eye).reshape(cfg.NL, GROUPS_PER_BLOCK * SSM_GROUP)


def _cc_from_band(m, cfg):
    eye = jnp.eye(GROUPS_PER_BLOCK, dtype=F32)
    m = m.reshape(cfg.NB, 2, GROUPS_PER_BLOCK, SSM_STATE, GROUPS_PER_BLOCK, SSM_GROUP)
    v = jnp.einsum('jrhpgc,gh->rjgcp', m, eye).reshape(2, cfg.G, SSM_GROUP, SSM_STATE)
    return v[0], -v[1]


PACK_COLS = 512
PACK_ROW_ALIGN = 64


def _pack(arrs):
    flat = jnp.concatenate([a.reshape(-1).astype(F32) for a in arrs])
    unit = PACK_COLS * PACK_ROW_ALIGN
    total = -(-flat.shape[0] // unit) * unit
    return jnp.pad(flat, (0, total - flat.shape[0])).reshape(-1, PACK_COLS)


def _unpack(p, shapes):
    flat = p.reshape(-1)
    out, off = [], 0
    for shp in shapes:
        size = math.prod(shp)
        out.append(flat[off:off + size].reshape(shp))
        off += size
    return out


def _adamw(w, g, m, v, *, name, emit_grad=False):
    c1 = 1.0 / (1.0 - ADAM_B1 ** ADAM_STEP)
    c2 = 1.0 / (1.0 - ADAM_B2 ** ADAM_STEP)

    if w.ndim == 2:
        outs = _adamw(w[None], g[None], m[None], v[None], name=name, emit_grad=emit_grad)
        return [o[0] for o in outs]
    lead, rows, cols = w.shape
    tc = _tile(cols, 512)
    tm = _tile(rows, max(8, ADAMW_BLOCK_BYTES // (4 * tc)), 8)
    n_out = 4 if emit_grad else 3

    def body(w_ref, g_ref, m_ref, v_ref, *o_refs):
        gv = g_ref[...]
        mn = ADAM_B1 * m_ref[...] + (1.0 - ADAM_B1) * gv
        vn = ADAM_B2 * v_ref[...] + (1.0 - ADAM_B2) * (gv * gv)
        delta = -ADAM_LR * ((mn * c1) / (jnp.sqrt(vn * c2) + ADAM_EPS) + ADAM_WD * w_ref[...])
        for o_ref, val in zip(o_refs, ((gv, delta, mn, vn) if emit_grad else (delta, mn, vn))):
            o_ref[...] = val

    blk = pl.BlockSpec((None, tm, tc), lambda n, i, j: (n, i, j))
    return _pc(body, name=name, grid=(lead, rows // tm, cols // tc), in_specs=[blk] * 4, out_specs=[blk] * n_out,
               out_shape=[jax.ShapeDtypeStruct((lead, rows, cols), F32)] * n_out,
               compiler_params=pltpu.CompilerParams(dimension_semantics=("parallel", "parallel", "parallel")))(w, g, m, v)


def _to_comm_layout(name, w, cfg):
    w = w[0]
    if name == 'w_in':
        return jnp.pad(w, ((0, 0), (0, cfg.DINP - cfg.DIN)))
    if name == 'w_q_b':
        hs = w.shape[1] // (QK_NOPE + QK_ROPE)
        wt = w.T.reshape(hs, QK_NOPE + QK_ROPE, cfg.QL)
        return jnp.pad(wt, ((0, 0), (0, HEAD_SLOT - QK_NOPE - QK_ROPE), (0, 0))).reshape(hs * HEAD_SLOT, cfg.QL)
    if name == 'w_kv_b':
        return w.T
    if name == 'w_up':
        return w.T.reshape(2, cfg.F // 4, cfg.D)
    return w


def _from_comm_layout(name, g, cfg):
    if name == 'w_in':
        g = g[:, :cfg.DIN]
    elif name == 'w_q_b':
        hs = g.shape[0] // HEAD_SLOT
        g = g.reshape(hs, HEAD_SLOT, cfg.QL)[:, :QK_NOPE + QK_ROPE].reshape(hs * (QK_NOPE + QK_ROPE), cfg.QL).T
    elif name == 'w_kv_b':
        g = g.T
    elif name == 'w_up':
        g = g.reshape(2, cfg.FQ, cfg.D)[:, :cfg.F // 4].reshape(cfg.F // 2, cfg.D).T
    elif name == 'w_down':
        g = g[:cfg.F // 4]
    return g[None]


def _ff_pad(v, cfg):
    k = v.shape[0]
    return jnp.pad(v.reshape(k, 4, cfg.F // 4), ((0, 0), (0, 0), (0, cfg.FQ - cfg.F // 4))).reshape(k, cfg.FP)


def _ff_unpad(v, cfg):
    k = v.shape[0]
    return v.reshape(k, 4, cfg.FQ)[:, :, :cfg.F // 4].reshape(k, cfg.F)


def _step(cfg, w, m, v, x, loss_target):
    lp, d, ds, nl = cfg.LP, cfg.D, cfg.DS, cfg.NL
    xi, yi = lax.axis_index("x"), lax.axis_index("y")
    me = 2 * xi + yi

    def place(n, order=None):
        rows_to = cfg.FQ if n in ('w_up', 'w_down') else None
        return _place_shard(_to_comm_layout(n, w[n], cfg), BF16, order=order, rows_to=rows_to, name=f"place_{n}")

    first = [place('w_in'), _place_shard(w['meta_tokens'], F32, name="place_meta")]
    f_send, f_recv, f_flying, f_token = _split_start(first, _allgather_ici_copies, 6, name="allgather_first_start")
    conv_w_shard = jnp.pad(w['conv_w'][0], ((0, ROW_ALIGN - 3), (0, cfg.FQ - cfg.F // 4)))
    placed = [None] + [place(n, f_token) for n in BIG[1:]]
    placed += [None, _place_shard(conv_w_shard, F32, order=f_token, name="place_conv_w")]
    f_landed = _split_wait(f_send, f_recv, f_flying, _allgather_ici_copies, placed[6], name="allgather_first_wait")
    w_in, meta_full = _allgather_forward(f_landed, name="allgather_first_forward")
    meta = jnp.transpose(meta_full.reshape(4, N_META, d // 4), (1, 0, 2)).reshape(N_META, d)
    conv_b = _ff_pad(w['conv_b'], cfg)
    mid = placed[1:5] + [placed[8]]
    mid_send, mid_recv, mid_flying, mid_token = _split_start(mid, _allgather_ici_copies, 3 * len(mid), before=meta_full,
                                                             name="allgather_mid_start")
    ffn_send, ffn_recv, ffn_flying, ffn_token = _split_start(placed[5:7], _allgather_ici_copies, 6, before=mid_token,
                                                             name="allgather_ffn_start")
    mix_norm = w['mix_norm'] + (mid_token[0:1, 0:1] + ffn_token[0:1, 0:1])

    pos = (jnp.arange(lp, dtype=jnp.int32) - PAD).astype(F32)
    inv_freq = 1.0 / (ROPE_BASE ** (jnp.arange(0, QK_ROPE, 2, dtype=F32) / QK_ROPE))
    ang = pos[:, None] * inv_freq[None, :]
    zpad = jnp.zeros((lp, LANE - QK_ROPE), F32)
    cos_t = jnp.concatenate([jnp.cos(ang), jnp.cos(ang), zpad], axis=1)
    sin_t = jnp.concatenate([jnp.sin(ang), jnp.sin(ang), zpad], axis=1)

    s5_in = (w['lam_re'][0], w['lam_im'][0], w['log_dt'][0], w['b_re'][0], w['b_im'][0])
    (a_re, a_im, bb_re, bb_im), s5_vjp = jax.vjp(_s5_discretize, *s5_in)
    lam_dt = lax.complex(s5_in[0], s5_in[1]) * jnp.exp(s5_in[2])[:, None]
    a_pow = jnp.exp(jnp.arange(1, 9, dtype=F32)[:, None, None] * lam_dt[None])
    r8 = jnp.arange(8)
    step_f = jnp.stack([jnp.where((r8 >= k)[:, None, None], a_pow[k - 1][None], 0.0) for k in (1, 2, 4)]).reshape(24, cfg.G, -1)
    step_b = jnp.stack([jnp.where((r8 < 8 - k)[:, None, None], a_pow[k - 1][None], 0.0) for k in (1, 2, 4)]).reshape(24, cfg.G, -1)
    rows_f = jnp.concatenate([a_pow, step_f])
    rows_b = jnp.conj(jnp.concatenate([a_pow[::-1], step_b]))

    def lane_rows(t):
        v = jnp.stack([jnp.real(t), jnp.imag(t)], axis=1).reshape(t.shape[0], 2, cfg.NB, GROUPS_PER_BLOCK, SSM_STATE)
        return jnp.transpose(v, (0, 2, 1, 3, 4)).reshape(t.shape[0], cfg.NL)

    pw_fwd, pw_bwd = lane_rows(rows_f), lane_rows(rows_b)
    bb_band = _bb_band(bb_re, bb_im, cfg).astype(BF16)
    cc_band = _cc_band(w['c_re'][0], w['c_im'][0], cfg).astype(BF16)
    d_skip, b_glu = w['d_skip'], w['b_glu']

    h0 = jnp.concatenate([jnp.zeros((PAD, d), F32), meta, x[0]], axis=0)
    xn = _rms_fwd(h0, mix_norm, name="rms_mix")
    z = _mm(xn, w_in, name="mm_in", tn=_tile(cfg.DINP, 640))
    u = (z, ds, 0)
    q_a = (z, cfg.QL, ds // cfg.QL)
    kv_a = (z, cfg.KVL, (ds + cfg.QL) // cfg.KVL)
    k_pe = (z, LANE, (ds + cfg.QL + cfg.KVL) // LANE)

    hs, yc = _s5_fwd(z, bb_band, cc_band, pw_fwd, cfg, name="s5_fwd")

    def s5_y(ycv, uv, dk):
        return ycv + dk * uv

    gl = _ew(lambda rid, ycv, uv, dk: jax.nn.gelu(s5_y(ycv, uv, dk)), [yc, u], [d_skip], [(ds, BF16)], name="s5_gelu")[0]
    mid_landed = _split_wait(mid_send, mid_recv, mid_flying, _allgather_ici_copies, gl, name="allgather_mid_wait")
    w_glu, w_qt, w_kvt, w_out, conv_full = _allgather_forward(mid_landed, name="allgather_mid_forward")
    conv_w = jnp.transpose(conv_full.reshape(4, ROW_ALIGN, cfg.FQ)[:, :3], (1, 0, 2)).reshape(3, cfg.FP)
    tg = _mm(gl, w_glu, name="mm_glu")
    ya = _ew(lambda rid, ycv, uv, tv, dk, bg: jax.nn.gelu(s5_y(ycv, uv, dk)) * jax.nn.sigmoid(tv + bg),
             [yc, u, tg], [d_skip, b_glu], [(ds, F32)], name="s5_glu")[0]

    qn = _rms_fwd(q_a, w['q_a_norm'], name="rms_q")
    kvn = _rms_fwd(kv_a, w['kv_a_norm'], name="rms_kv")
    q_raw = _mm(qn, w_qt, tb=True, name="mm_q")
    qx = _ew(_rope_heads(_rope, cfg.H), [q_raw, cos_t, sin_t], [], [(cfg.H * HEAD_SLOT, BF16)], name="rope_q")[0]
    kv = _mm(kvn, w_kvt, tb=True, out_dtype=BF16, name="mm_kv")
    kr = _ew(lambda rid, kp, cs, sn: _rope(kp, cs, sn), [k_pe, cos_t, sin_t], [], [(LANE, BF16)], name="rope_k")[0]
    o, lse = _attn_fwd(qx, kv, kr, cfg, name="attn_fwd")

    def norm2(rid, yav, ov, gs, ga):
        return jnp.concatenate([_rms_parts(yav, gs)[0] * gs, _rms_parts(ov, ga)[0] * ga], axis=1)

    yn = _ew(norm2, [ya, o], [w['out_norm_ssm'], w['out_norm_attn']], [(cfg.DMIX, BF16)], name="rms_out")[0]
    h1 = _mm(yn, w_out, res=h0, name="mm_out")
    xn2 = _rms_fwd(h1, w['ffn_norm'], name="rms_ffn")
    ffn_landed = _split_wait(ffn_send, ffn_recv, ffn_flying, _allgather_ici_copies, xn2, name="allgather_ffn_wait")
    w_upt, w_down = _allgather_forward(ffn_landed, name="allgather_ffn_forward")
    up, act = _ffn_up(xn2, w_upt, conv_w, conv_b, name="ffn_up")
    h2 = _mm(act, w_down, res=h1, tm=_tile(lp, 544, ROW_ALIGN), name="mm_down")

    g_final = w['final_norm'].reshape(1, d)

    def head(rid, hv, tv, gv):
        xhat, r = _rms_parts(hv, gv)
        valid = rid >= PAD + N_META
        diff = jnp.where(valid, xhat * gv - tv, 0.0)
        dout = diff * (1.0 / d)
        dxhat = dout * gv
        dx = r * (dxhat - xhat * jnp.mean(dxhat * xhat, axis=-1, keepdims=True))
        return dx, dx, dout * xhat, 0.5 * diff * dout

    dh2, dh2_b, dg_final, loss_cols = _ew(head, [h2, (loss_target[0], d, 0, SKIP)], [g_final], [(d, F32), (d, BF16)], [d, d],
                                          tm=PAD + N_META, name="loss_head")
    loss = lax.psum(jnp.sum(loss_cols), ("x", "y", "c"))

    dw_down = _mm(act, dh2_b, ta=True, tn=d, tm=512, out_dtype=BF16, name="mm_dw_down")

    def sibling_start(g, tag):
        land = lax.empty((4, g.shape[0] // 8, g.shape[1]), g.dtype)
        return _split_start([g, land], _rs_sibling_copies, 4, name=f"rs_sibling_{tag}_start")

    dn_send, dn_recv, dn_flying, dn_token = sibling_start(dw_down, "down")
    dup, dconv_w, dconv_b = _ffn_dact(dh2_b, w_down, up, conv_w, conv_b + dn_token[0:1, 0:1], name="ffn_dact")
    tk_up, tm_up = _tile(cfg.FP, 1408), _tile(cfg.FP, 512)
    dw_upt = _mm(dup, xn2, ta=True, dims=(2 * cfg.FP, d, lp), tn=d, tm=tm_up, a_lead=True, out_dtype=BF16, name="mm_dw_up",
                 a_idx=lambda i, j, k: (i // (cfg.FP // tm_up), 0, i % (cfg.FP // tm_up)))
    up_send, up_recv, up_flying, up_token = sibling_start(dw_upt, "up")
    dxn2 = _mm(dup, w_upt, dims=(lp, d, 2 * cfg.FP), tk=tk_up, tn=1024, a_lead=True, name="mm_dxn2",
               a_idx=lambda i, j, k: (k // (cfg.FP // tk_up), i, k % (cfg.FP // tk_up)))
    dh1, dh1_b, dg_ffn = _rms_bwd(h1, w['ffn_norm'] + up_token[0:1, 0:1], dxn2, res=dh2, mask=True, with_bf16=True,
                                  name="rms_ffn_bwd")

    dyn = _mm(dh1_b, w_out, tb=True, name="mm_dyn")
    dw_out = _mm(yn, dh1_b, ta=True, tn=d, tm=512, name="mm_dw_out")
    up_done = _split_wait(up_send, up_recv, up_flying, _rs_sibling_copies, dw_out, name="rs_sibling_up_wait")
    dn_done = _split_wait(dn_send, dn_recv, dn_flying, _rs_sibling_copies, dw_out, name="rs_sibling_down_wait")
    early_parts, early_sends = _add_halves_all([up_done[0], dn_done[0]], [up_done[1], dn_done[1]], [BF16] * 2, "early")
    chip_lands = [lax.empty((3,) + s.shape[1:], s.dtype) for s in early_sends]
    ch_send, ch_recv, ch_flying, ch_token = _split_start(early_sends + chip_lands, _rs_chips_copies, 6,
                                                         name="rs_chips_early_start")
    dya, dg_ssm = _rms_bwd(ya, w['out_norm_ssm'] + ch_token[0:1, 0:1], (dyn, ds, 0), name="rms_ssm_bwd")
    do, dg_attn = _rms_bwd(o, w['out_norm_attn'], (dyn, cfg.DATTN, ds // cfg.DATTN), name="rms_attn_bwd")

    dqx, dkv, dkr = _attn_bwd(qx, kv, kr, o, lse, do, cfg, name="attn_bwd")
    dq_raw = _ew(_rope_heads(_unrope, cfg.H), [dqx, cos_t, sin_t], [], [(cfg.H * HEAD_SLOT, BF16)], name="unrope_q")[0]
    dk_pe = _ew(lambda rid, dk, cs, sn: _unrope(dk, cs, sn), [dkr, cos_t, sin_t], [], [(LANE, F32)], name="unrope_k")[0]
    dqn = _mm(dq_raw, w_qt, name="mm_dqn")
    dw_qt = _mm(dq_raw, qn, ta=True, tm=512, name="mm_dw_q")
    dkvn = _mm(dkv, w_kvt, name="mm_dkvn")
    dw_kvt = _mm(dkv, kvn, ta=True, tm=512, name="mm_dw_kv")
    dq_a, dg_q = _rms_bwd(q_a, w['q_a_norm'], dqn, name="rms_q_bwd")
    dkv_a, dg_kv = _rms_bwd(kv_a, w['kv_a_norm'], dkvn, name="rms_kv_bwd")

    def glu_bwd(rid, ycv, uv, tv, dyav, dk, bg):
        gelu = jax.nn.gelu(s5_y(ycv, uv, dk))
        sg = jax.nn.sigmoid(tv + bg)
        dt = dyav * gelu * sg * (1.0 - sg)
        return dt, dyav * sg, dt

    dt_b, dgl1, db_glu = _ew(glu_bwd, [yc, u, tg, dya], [d_skip, b_glu], [(ds, BF16), (ds, F32)], [ds], name="s5_glu_bwd")
    dgl = _mm(dt_b, w_glu, tb=True, res=dgl1, name="mm_dgl")
    dw_glu = _mm(gl, dt_b, ta=True, tm=512, name="mm_dw_glu")

    def gelu_bwd(rid, ycv, uv, dglv, dk):
        _, vjp = jax.vjp(jax.nn.gelu, s5_y(ycv, uv, dk))
        dy = vjp(dglv)[0]
        return dy, dy * dk, dy * uv

    mid_grads = [dw_out, dw_glu, dw_qt, dw_kvt]
    mid_lands = [lax.empty((4, g.shape[0] // 8, g.shape[1]), g.dtype) for g in mid_grads]
    ms_send, ms_recv, ms_flying, ms_token = _split_start(mid_grads + mid_lands, _rs_sibling_copies, 4 * len(mid_grads),
                                                         name="rs_sibling_mid_start")
    dy_b, du_skip, dd_skip = _ew(gelu_bwd, [yc, u, dgl], [d_skip + ms_token[0:1, 0:1]], [(ds, BF16), (ds, F32)], [ds],
                                 name="s5_gelu_bwd")
    ms_done = _split_wait(ms_send, ms_recv, ms_flying, _rs_sibling_copies, dy_b, name="rs_sibling_mid_wait")
    mid_parts, mid_sends = _add_halves_all(ms_done[:4], ms_done[4:], [BF16] * 4, "mid")
    mid_chip_lands = [lax.empty((3,) + s.shape[1:], s.dtype) for s in mid_sends]
    mc_send, mc_recv, mc_flying, mc_token = _split_start(mid_sends + mid_chip_lands, _rs_chips_copies, 3 * len(mid_sends),
                                                         name="rs_chips_mid_start")
    du, dbb_band, dcc_band, da_l = _s5_bwd(dy_b, hs, z, bb_band, cc_band, pw_bwd + mc_token[0:1, 0:1], du_skip, cfg,
                                           name="s5_bwd")

    dz = jnp.concatenate([du, dq_a, dkv_a, dk_pe], axis=1).astype(BF16)
    dxn = _mm(dz, w_in, tb=True, name="mm_dxn")
    dw_in = _mm(xn, dz, ta=True, tm=512, tn=_tile(cfg.DINP, 1024), name="mm_dw_in")
    def mix_bwd(rid, xv, dyv, resv, gv):
        dx, dg = _rms_bwd_block(xv, gv, dyv)
        dx = dx + resv
        return dx, dx, dg

    grad_x, dh0_head, dg_mix = _ew(mix_bwd, [h0, dxn, dh1], [mix_norm], [(d, F32, SKIP), (d, F32, FIRST)], [d],
                                   tm=PAD + N_META, name="rms_mix_bwd")
    grad_x = grad_x[None]

    da_re, da_im = _gp_from_lanes(da_l, cfg)
    dbb_re, dbb_im = _bb_from_band(dbb_band, cfg)
    dlam_re, dlam_im, dlog_dt, db_re, db_im = s5_vjp((da_re, da_im, dbb_re, dbb_im))
    dc_re, dc_im = _cc_from_band(dcc_band, cfg)
    local_small = {
        'meta_tokens': dh0_head[PAD:], 'mix_norm': dg_mix, 'lam_re': dlam_re, 'lam_im': dlam_im, 'log_dt': dlog_dt,
        'b_re': db_re, 'b_im': db_im, 'c_re': dc_re, 'c_im': dc_im, 'd_skip': dd_skip, 'b_glu': db_glu, 'q_a_norm': dg_q,
        'kv_a_norm': dg_kv, 'out_norm_ssm': dg_ssm, 'out_norm_attn': dg_attn, 'ffn_norm': dg_ffn,
        'conv_w': _ff_unpad(dconv_w, cfg), 'conv_b': _ff_unpad(dconv_b, cfg), 'final_norm': dg_final,
    }
    small_shapes = [local_small[n].shape for n in SMALL]

    small_pack = _pack([local_small[n] for n in SMALL])
    end_local = [dw_in, small_pack]
    end_recv = _rs_sibling(end_local, name="rs_sibling_end")
    end_parts, end_sends = _add_halves_all(end_local, end_recv, [BF16, F32], "end")
    end_lands = [lax.empty((3,) + s.shape[1:], s.dtype) for s in end_sends]
    ec_send, ec_recv, ec_flying, ec_token = _split_start(end_sends + end_lands, _rs_chips_copies, 3 * len(end_sends),
                                                         name="rs_chips_end_start")
    ch_done = _split_wait(ch_send, ch_recv, ch_flying, _rs_chips_copies, ec_token, name="rs_chips_early_wait")
    red_up, red_down = _rs_finish(early_parts, ch_done[2:], "early")

    delta, new_m, new_v, grads = {}, {}, {}, {}
    padded_rows = ('w_down',)

    def adamw_big(n, red):
        shp = w[n].shape
        w2, m2, v2 = [t.reshape(shp[-2], shp[-1]) for t in (w[n], m[n], v[n])]
        if n in padded_rows:
            g2, dl, mn, vn = _adamw(w2, red, m2, v2, emit_grad=True, name=f"adamw_{n}")
            grads[n] = g2.reshape(shp)
        else:
            grads[n] = _from_comm_layout(n, red, cfg)
            dl, mn, vn = _adamw(w2, grads[n].reshape(shp[-2], shp[-1]), m2, v2, name=f"adamw_{n}")
        delta[n], new_m[n], new_v[n] = dl.reshape(shp), mn.reshape(shp), vn.reshape(shp)

    def adamw_up(red):
        q = cfg.F // 4
        wt, mt, vt = [jnp.transpose(t[0]).reshape(2, q, d) for t in (w['w_up'], m['w_up'], v['w_up'])]
        outs = _adamw(wt, red.reshape(2, cfg.FQ, d), mt, vt, emit_grad=True, name="adamw_w_up")
        grads['w_up'], delta['w_up'], new_m['w_up'], new_v['w_up'] = [jnp.transpose(t.reshape(2 * q, d))[None] for t in outs]

    adamw_up(red_up)
    adamw_big('w_down', red_down)
    mc_done = _split_wait(mc_send, mc_recv, mc_flying, _rs_chips_copies, delta['w_down'], name="rs_chips_mid_wait")
    ec_done = _split_wait(ec_send, ec_recv, ec_flying, _rs_chips_copies, mc_done[0], name="rs_chips_end_wait")
    red = _rs_finish(mid_parts + end_parts, list(mc_done[len(mid_sends):]) + list(ec_done[len(end_sends):]), "rest")
    small_full = _allgather([_place_shard(red[5], F32, name="place_small")], name="allgather_small")[0]
    small_sum = dict(zip(SMALL, _unpack(small_full, small_shapes)))
    for n, r in zip(['w_out', 'w_glu', 'w_q_b', 'w_kv_b'], red[:4]):
        adamw_big(n, r)
    in_t = [jnp.transpose(t[0]) for t in (w['w_in'], m['w_in'], v['w_in'])]
    outs = _adamw(in_t[0], jnp.transpose(red[4][:, :cfg.DIN]), in_t[1], in_t[2], emit_grad=True, name="adamw_w_in")
    grads['w_in'], delta['w_in'], new_m['w_in'], new_v['w_in'] = [jnp.transpose(t)[None] for t in outs]

    for n in SMALL:
        g = small_sum[n]
        if n == 'meta_tokens':
            g = lax.dynamic_slice_in_dim(g, me * (d // 4), d // 4, axis=1)
        elif n == 'conv_w':
            g = lax.dynamic_slice_in_dim(g, me * (cfg.F // 4), cfg.F // 4, axis=1)[None]
        else:
            g = g.reshape(w[n].shape)
        grads[n] = g

    shapes = [w[n].shape for n in SMALL]
    packs = [_pack([src[n] for n in SMALL]) for src in (w, grads, m, v)]
    for dst, p in zip((delta, new_m, new_v), _adamw(*packs, name="adamw_small")):
        dst.update(zip(SMALL, _unpack(p, shapes)))

    return (loss, grad_x, *[grads[n] for n in WEIGHTS], *[delta[n] for n in WEIGHTS],
            *[new_m[n] for n in WEIGHTS], *[new_v[n] for n in WEIGHTS])


def kernel(x, meta_tokens, mix_norm, w_in, lam_re, lam_im, log_dt, b_re, b_im, c_re, c_im, d_skip, w_glu, b_glu, q_a_norm, w_q_b, kv_a_norm, w_kv_b, out_norm_ssm, out_norm_attn, w_out, ffn_norm, w_up, conv_w, conv_b, w_down, final_norm, loss_target, m_meta_tokens, m_mix_norm, m_w_in, m_lam_re, m_lam_im, m_log_dt, m_b_re, m_b_im, m_c_re, m_c_im, m_d_skip, m_w_glu, m_b_glu, m_q_a_norm, m_w_q_b, m_kv_a_norm, m_w_kv_b, m_out_norm_ssm, m_out_norm_attn, m_w_out, m_ffn_norm, m_w_up, m_conv_w, m_conv_b, m_w_down, m_final_norm, v_meta_tokens, v_mix_norm, v_w_in, v_lam_re, v_lam_im, v_log_dt, v_b_re, v_b_im, v_c_re, v_c_im, v_d_skip, v_w_glu, v_b_glu, v_q_a_norm, v_w_q_b, v_kv_a_norm, v_w_kv_b, v_out_norm_ssm, v_out_norm_attn, v_w_out, v_ffn_norm, v_w_up, v_conv_w, v_conv_b, v_w_down, v_final_norm):
    args = dict(locals())
    w = {n: args[n] for n in WEIGHTS}
    m = {n: args["m_" + n] for n in WEIGHTS}
    v = {n: args["v_" + n] for n in WEIGHTS}
    return _step(PROD, w, m, v, x, loss_target)
```

```python
import functools
import math
from typing import NamedTuple

import jax
import jax.numpy as jnp
from jax import lax
from jax.experimental import pallas as pl
from jax.experimental.pallas import tpu as pltpu

F32, BF16 = jnp.float32, jnp.bfloat16
MESH = pl.DeviceIdType.MESH
LANE = 128
ROW_ALIGN = 16
N_META = 16
PAD = 112
CHUNK = 64
SSM_GROUP = 16
SSM_STATE = 64
GROUPS_PER_BLOCK = 8
QK_NOPE, QK_ROPE, V_HEAD = 128, 64, 128
HEAD_SLOT = 256
ROPE_BASE = 10000.0
EPS = 1e-6
ADAM_LR, ADAM_B1, ADAM_B2, ADAM_EPS, ADAM_WD, ADAM_STEP = 0.001, 0.9, 0.999, 1e-08, 0.01, 10
DT_F32_BLOCK_BYTES = 9 << 18
ADAMW_BLOCK_BYTES = 3 << 19
PLACE_BLOCK_BYTES = 6 << 20
SKIP, FIRST = "skip", "first"


class Cfg(NamedTuple):
    D: int
    S: int
    DS: int
    H: int
    QL: int
    KVL: int
    F: int

    @property
    def LP(self):
        return PAD + N_META + self.S

    @property
    def G(self):
        return self.DS // SSM_GROUP

    @property
    def NB(self):
        return self.G // GROUPS_PER_BLOCK

    @property
    def NL(self):
        return 2 * self.G * SSM_STATE

    @property
    def DATTN(self):
        return self.H * V_HEAD

    @property
    def DMIX(self):
        return self.DS + self.DATTN

    @property
    def DIN(self):
        return self.DS + self.QL + self.KVL + QK_ROPE

    @property
    def DINP(self):
        return self.DS + self.QL + self.KVL + LANE

    @property
    def FQ(self):
        return -(-(self.F // 4) // LANE) * LANE

    @property
    def FP(self):
        return 4 * self.FQ


PROD = Cfg(D=2048, S=2048, DS=1024, H=8, QL=512, KVL=256, F=5504)

WEIGHTS = ['meta_tokens', 'mix_norm', 'w_in', 'lam_re', 'lam_im', 'log_dt', 'b_re', 'b_im', 'c_re', 'c_im', 'd_skip',
           'w_glu', 'b_glu', 'q_a_norm', 'w_q_b', 'kv_a_norm', 'w_kv_b', 'out_norm_ssm', 'out_norm_attn', 'w_out',
           'ffn_norm', 'w_up', 'conv_w', 'conv_b', 'w_down', 'final_norm']
BIG = ['w_in', 'w_glu', 'w_q_b', 'w_kv_b', 'w_out', 'w_up', 'w_down']
SMALL = [n for n in WEIGHTS if n not in BIG]


def _pc(body, **kw):
    return pl.pallas_call(body, **kw)


def _tile(n, target, align=LANE):
    best = None
    d = align
    while d <= min(n, target):
        if n % d == 0:
            best = d
        d += align
    return best if best is not None else n


def _row_tile(rows, cols):
    return _tile(rows, max(ROW_ALIGN, DT_F32_BLOCK_BYTES // (4 * cols)), ROW_ALIGN)


def _mm(a, b, *, name, ta=False, tb=False, tm=None, tn=512, tk=None, out_dtype=F32, res=None,
        a_idx=None, b_idx=None, dims=None, a_lead=False):
    if dims is None:
        m, k = (a.shape[1], a.shape[0]) if ta else a.shape
        n = b.shape[0] if tb else b.shape[1]
    else:
        m, n, k = dims
    tm = _tile(m, tm or m, LANE if ta else ROW_ALIGN)
    tn = _tile(n, tn)
    tk = _tile(k, tk or k, ROW_ALIGN if (ta and not tb) else LANE)
    nm, nn, nk = m // tm, n // tn, k // tk
    a_idx = a_idx or ((lambda i, j, kk: (kk, i)) if ta else (lambda i, j, kk: (i, kk)))
    b_idx = b_idx or ((lambda i, j, kk: (j, kk)) if tb else (lambda i, j, kk: (kk, j)))
    dn = (((0 if ta else 1,), (1 if tb else 0,)), ((), ()))

    def body(*refs):
        a_ref, b_ref = refs[0], refs[1]
        r_ref = refs[2] if res is not None else None
        o_ref = refs[3] if res is not None else refs[2]
        d = lax.dot_general(a_ref[...].astype(BF16), b_ref[...].astype(BF16), dn, preferred_element_type=F32)

        def finish(r):
            if r_ref is not None:
                r = r + r_ref[...].astype(F32)
            o_ref[...] = r.astype(out_dtype)

        if nk == 1:
            finish(d)
        else:
            acc = refs[-1]
            kk = pl.program_id(2)

            @pl.when(kk == 0)
            def _():
                acc[...] = d

            @pl.when(kk > 0)
            def _():
                acc[...] += d

            @pl.when(kk == nk - 1)
            def _():
                finish(acc[...])

    a_blk = ((None,) if a_lead else ()) + ((tk, tm) if ta else (tm, tk))
    in_specs = [pl.BlockSpec(a_blk, a_idx), pl.BlockSpec((tn, tk) if tb else (tk, tn), b_idx)]
    args = [a, b]
    if res is not None:
        in_specs.append(pl.BlockSpec((tm, tn), lambda i, j, kk: (i, j)))
        args.append(res)
    return _pc(body, name=name, grid=(nm, nn, nk), in_specs=in_specs,
               out_specs=pl.BlockSpec((tm, tn), lambda i, j, kk: (i, j)),
               out_shape=jax.ShapeDtypeStruct((m, n), out_dtype),
               scratch_shapes=[pltpu.VMEM((tm, tn), F32)] if nk > 1 else [],
               compiler_params=pltpu.CompilerParams(dimension_semantics=("parallel", "parallel", "arbitrary")))(*args)


def _ew(fn, ins, vecs, outs, sums=(), *, name, tm=None):
    ins = [x if isinstance(x, tuple) else (x, x.shape[1], 0) for x in ins]
    ins = [x if len(x) == 4 else x + (None,) for x in ins]
    outs = [o if len(o) == 3 else o + (None,) for o in outs]
    rows = ins[0][0].shape[0]
    cmax = max([c for _, c, _, _ in ins] + [c for c, _, _ in outs])
    tm = tm or _row_tile(rows, cmax)
    n_in, n_vec, n_out, n_sum = len(ins), len(vecs), len(outs), len(sums)

    def body(*refs):
        i = pl.program_id(0)
        rid = i * tm + lax.broadcasted_iota(jnp.int32, (tm, 1), 0)
        vals = [r[...] for r in refs[:n_in + n_vec]]
        res = fn(rid, *vals)
        res = res if isinstance(res, (tuple, list)) else (res,)
        o_refs = refs[n_in + n_vec:]
        for o_ref, r, (_, _, mode) in zip(o_refs[:n_out], res[:n_out], outs):
            if mode == FIRST:
                @pl.when(i == 0)
                def _():
                    o_ref[...] = r.astype(o_ref.dtype)
            else:
                o_ref[...] = r.astype(o_ref.dtype)
        for o_ref, r in zip(o_refs[n_out:], res[n_out:]):
            part = jnp.sum(r.astype(F32), axis=0, keepdims=True)

            @pl.when(i == 0)
            def _():
                o_ref[...] = part

            @pl.when(i > 0)
            def _():
                o_ref[...] += part

    def row_idx(mode):
        if mode == SKIP:
            return lambda i, cb=0: (jnp.maximum(i - 1, 0), cb)
        if mode == FIRST:
            return lambda i, cb=0: (0, cb)
        return lambda i, cb=0: (i, cb)

    in_specs = [pl.BlockSpec((tm, c), functools.partial(row_idx(mode), cb=cb)) for _, c, cb, mode in ins]
    in_specs += [pl.BlockSpec(v.shape, functools.partial(lambda i, nd: (0,) * nd, nd=v.ndim)) for v in vecs]
    out_specs = [pl.BlockSpec((tm, c), row_idx(mode)) for c, _, mode in outs]
    out_specs += [pl.BlockSpec((1, c), lambda i: (0, 0)) for c in sums]
    out_rows = {None: rows, SKIP: rows - tm, FIRST: tm}
    out_shape = [jax.ShapeDtypeStruct((out_rows[mode], c), dt) for c, dt, mode in outs]
    out_shape += [jax.ShapeDtypeStruct((1, c), F32) for c in sums]
    return _pc(body, name=name, grid=(rows // tm,), in_specs=in_specs, out_specs=out_specs, out_shape=out_shape,
               compiler_params=pltpu.CompilerParams(dimension_semantics=("arbitrary",)))(*[x[0] for x in ins], *vecs)


def _rms_parts(x, g):
    r = lax.rsqrt(jnp.mean(x * x, axis=-1, keepdims=True) + EPS)
    return x * r, r


def _rms_bwd_block(x, g, dy):
    xhat, r = _rms_parts(x, g)
    dxhat = dy * g
    dx = r * (dxhat - xhat * jnp.mean(dxhat * xhat, axis=-1, keepdims=True))
    return dx, dy * xhat


def _rms_fwd(x, g, *, name):
    c = x[1] if isinstance(x, tuple) else x.shape[1]
    return _ew(lambda rid, xv, gv: _rms_parts(xv.astype(F32), gv)[0] * gv, [x], [g], [(c, BF16)], name=name)[0]


def _rms_bwd(x, g, dy, *, name, res=None, mask=False, with_bf16=False):
    c = x[1] if isinstance(x, tuple) else x.shape[1]

    def fn(rid, xv, dyv, *rest):
        gv = rest[-1]
        dx, dg = _rms_bwd_block(xv.astype(F32), gv, dyv.astype(F32))
        if res is not None:
            dx = dx + rest[0]
        if mask:
            dx = jnp.where(rid >= PAD, dx, 0.0)
        return (dx, dx, dg) if with_bf16 else (dx, dg)

    ins = [x, dy] + ([res] if res is not None else [])
    outs = [(c, F32)] + ([(c, BF16)] if with_bf16 else [])
    return _ew(fn, ins, [g], outs, [c], name=name)


S5_W = GROUPS_PER_BLOCK * SSM_STATE
S5_GW = GROUPS_PER_BLOCK * SSM_GROUP
S5_UNROLL = 8
S5_DA_ROWS = 272


def _s5_scan_in_place(ref, pw_ref, *, reverse):
    lp = ref.shape[0]
    tile_rows = 8
    chunk = _tile(lp, S5_DA_ROWS, tile_rows)
    tiles = chunk // tile_rows

    def chunk_body(c, carry):
        rows = pl.ds(pl.multiple_of(c * chunk, tile_rows), chunk)
        xr, xi = ref[rows, :S5_W], ref[rows, S5_W:]
        for level, k in enumerate((1, 2, 4)):
            base = tile_rows * (1 + level)
            mr, mi = pw_ref[base:base + tile_rows, :S5_W][None], pw_ref[base:base + tile_rows, S5_W:][None]
            shift = chunk - k if reverse else k
            sr = pltpu.roll(xr, shift, 0).reshape(tiles, tile_rows, S5_W)
            si = pltpu.roll(xi, shift, 0).reshape(tiles, tile_rows, S5_W)
            xr = xr + (mr * sr - mi * si).reshape(chunk, S5_W)
            xi = xi + (mr * si + mi * sr).reshape(chunk, S5_W)
        ref[rows, :S5_W] = xr
        ref[rows, S5_W:] = xi
        return carry

    lax.fori_loop(0, lp // chunk, chunk_body, 0)

    pr, pi = pw_ref[0:tile_rows, :S5_W], pw_ref[0:tile_rows, S5_W:]
    ntile = lp // tile_rows
    unroll = 4

    def step(n, carry):
        cr, ci = carry
        for q in range(unroll):
            j = n * unroll + q
            j = ntile - 1 - j if reverse else j
            rows = pl.ds(pl.multiple_of(j * tile_rows, tile_rows), tile_rows)
            nr = ref[rows, :S5_W] + (pr * cr - pi * ci)
            ni = ref[rows, S5_W:] + (pr * ci + pi * cr)
            ref[rows, :S5_W] = nr
            ref[rows, S5_W:] = ni
            cr, ci = (nr[0:1], ni[0:1]) if reverse else (nr[tile_rows - 1:], ni[tile_rows - 1:])
        return cr, ci

    z = jnp.zeros((1, S5_W), F32)
    lax.fori_loop(0, ntile // unroll, step, (z, z))


def _s5_fwd(z, bb_band, cc_band, a_l, cfg, *, name):
    lp, ds, nl = cfg.LP, cfg.DS, cfg.NL

    def body(u_ref, bb_ref, cc_ref, a_ref, hs_ref, y_ref):
        hs_ref[...] = jnp.dot(u_ref[...].astype(BF16), bb_ref[...], preferred_element_type=F32)
        _s5_scan_in_place(hs_ref, a_ref, reverse=False)
        y_ref[...] = jnp.dot(hs_ref[...].astype(BF16), cc_ref[...], preferred_element_type=F32)

    return _pc(body, name=name, grid=(cfg.NB,),
               in_specs=[pl.BlockSpec((lp, S5_GW), lambda j: (0, j)), pl.BlockSpec((S5_GW, 2 * S5_W), lambda j: (j, 0)),
                         pl.BlockSpec((2 * S5_W, S5_GW), lambda j: (j, 0)), pl.BlockSpec((32, 2 * S5_W), lambda j: (0, j))],
               out_specs=[pl.BlockSpec((lp, 2 * S5_W), lambda j: (0, j)), pl.BlockSpec((lp, S5_GW), lambda j: (0, j))],
               out_shape=[jax.ShapeDtypeStruct((lp, nl), F32), jax.ShapeDtypeStruct((lp, ds), F32)],
               compiler_params=pltpu.CompilerParams(dimension_semantics=("parallel",)))(z, bb_band, cc_band, a_l)


def _s5_bwd(dy, hs, z, bb_band, cc_band, a_l, du_skip, cfg, *, name):
    lp, ds, nl = cfg.LP, cfg.DS, cfg.NL
    nt = (((1,), (1,)), ((), ()))
    tn = (((0,), (0,)), ((), ()))

    def body(dy_ref, hs_ref, u_ref, bb_ref, cc_ref, a_ref, sk_ref, du_ref, dbb_ref, dcc_ref, da_ref, g_ref):
        dyv = dy_ref[...]
        g_ref[...] = lax.dot_general(dyv, cc_ref[...], nt, preferred_element_type=F32)
        _s5_scan_in_place(g_ref, a_ref, reverse=True)
        dcc_ref[...] = lax.dot_general(hs_ref[...].astype(BF16), dyv, tn, preferred_element_type=F32)
        gb = g_ref[...].astype(BF16)
        dbb_ref[...] = lax.dot_general(u_ref[...].astype(BF16), gb, tn, preferred_element_type=F32)
        du_ref[...] = lax.dot_general(gb, bb_ref[...], nt, preferred_element_type=F32) + sk_ref[...]
        dre = jnp.zeros((1, S5_W), F32)
        dim = jnp.zeros((1, S5_W), F32)
        for r0 in range(0, lp, S5_DA_ROWS):
            rows = min(S5_DA_ROWS, lp - r0)
            first = lax.broadcasted_iota(jnp.int32, (rows, 1), 0) == 0
            prev = hs_ref[r0 - 1:r0, :] if r0 else jnp.zeros((1, 2 * S5_W), F32)
            hr = jnp.where(first, prev[:, :S5_W], pltpu.roll(hs_ref[r0:r0 + rows, :S5_W], 1, 0))
            hi = jnp.where(first, prev[:, S5_W:], pltpu.roll(hs_ref[r0:r0 + rows, S5_W:], 1, 0))
            gr, gi = g_ref[r0:r0 + rows, :S5_W], g_ref[r0:r0 + rows, S5_W:]
            dre = dre + jnp.sum(gr * hr + gi * hi, axis=0, keepdims=True)
            dim = dim + jnp.sum(gi * hr - gr * hi, axis=0, keepdims=True)
        da_ref[:, :S5_W] = dre
        da_ref[:, S5_W:] = dim

    col_blk = pl.BlockSpec((lp, S5_GW), lambda j: (0, j))
    lane_blk = pl.BlockSpec((lp, 2 * S5_W), lambda j: (0, j))
    bb_blk = pl.BlockSpec((S5_GW, 2 * S5_W), lambda j: (j, 0))
    cc_blk = pl.BlockSpec((2 * S5_W, S5_GW), lambda j: (j, 0))
    a_blk = pl.BlockSpec((1, 2 * S5_W), lambda j: (0, j))
    pw_blk = pl.BlockSpec((32, 2 * S5_W), lambda j: (0, j))
    return _pc(body, name=name, grid=(cfg.NB,),
               in_specs=[col_blk, lane_blk, col_blk, bb_blk, cc_blk, pw_blk, col_blk],
               out_specs=[col_blk, bb_blk, cc_blk, a_blk],
               out_shape=[jax.ShapeDtypeStruct((lp, ds), F32), jax.ShapeDtypeStruct((ds, 2 * S5_W), F32),
                          jax.ShapeDtypeStruct((nl, S5_GW), F32), jax.ShapeDtypeStruct((1, nl), F32)],
               scratch_shapes=[pltpu.VMEM((lp, 2 * S5_W), F32)],
               compiler_params=pltpu.CompilerParams(dimension_semantics=("parallel",)))(dy, hs, z, bb_band, cc_band, a_l, du_skip)


def _conv_gate(pre, cw, cb):
    return cw[0:1] * pltpu.roll(pre, 2, 0) + cw[1:2] * pltpu.roll(pre, 1, 0) + cw[2:3] * pre + cb


def _ffn_up(xn2, w_upt, cw, cb, *, name):
    lp, d = xn2.shape
    fp = w_upt.shape[0] // 2
    tc = _tile(fp, 256)
    nb = fp // tc

    def body(x_ref, wg_ref, wv_ref, cw_ref, cb_ref, up_ref, act_ref):
        wcat = jnp.concatenate([wg_ref[...], wv_ref[...]], axis=0)
        r = lax.dot_general(x_ref[...], wcat, (((1,), (1,)), ((), ())), preferred_element_type=F32)
        pre, val = r[:, :tc].astype(BF16), r[:, tc:].astype(BF16)
        up_ref[0] = pre
        up_ref[1] = val
        gate = _conv_gate(pre.astype(F32), cw_ref[...], cb_ref[...])
        act_ref[...] = (jax.nn.silu(gate) * val.astype(F32)).astype(BF16)

    return _pc(body, name=name, grid=(nb,),
               in_specs=[pl.BlockSpec((lp, d), lambda j: (0, 0)), pl.BlockSpec((tc, d), lambda j: (j, 0)),
                         pl.BlockSpec((tc, d), lambda j: (nb + j, 0)),
                         pl.BlockSpec((3, tc), lambda j: (0, j)), pl.BlockSpec((1, tc), lambda j: (0, j))],
               out_specs=[pl.BlockSpec((2, lp, tc), lambda j: (0, 0, j)), pl.BlockSpec((lp, tc), lambda j: (0, j))],
               out_shape=[jax.ShapeDtypeStruct((2, lp, fp), BF16), jax.ShapeDtypeStruct((lp, fp), BF16)],
               compiler_params=pltpu.CompilerParams(dimension_semantics=("parallel",)))(xn2, w_upt, w_upt, cw, cb)


def _ffn_dact(dh2, w_down, up, cw, cb, *, name):
    lp, d = dh2.shape
    fp = w_down.shape[0]
    tc = _tile(fp, 256)
    nb = fp // tc

    def body(dh_ref, wd_ref, up_ref, cw_ref, cb_ref, dup_ref, dcw_ref, dcb_ref):
        da = lax.dot_general(dh_ref[...], wd_ref[...], (((1,), (1,)), ((), ())), preferred_element_type=F32)
        pre, val, cwv = up_ref[0].astype(F32), up_ref[1].astype(F32), cw_ref[...]
        gate = _conv_gate(pre, cwv, cb_ref[...])
        sg = jax.nn.sigmoid(gate)
        dup_ref[1] = (da * (gate * sg)).astype(BF16)
        dgate = da * val * (sg * (1.0 + gate * (1.0 - sg)))
        dpre = cwv[2:3] * dgate + cwv[1:2] * pltpu.roll(dgate, lp - 1, 0) + cwv[0:1] * pltpu.roll(dgate, lp - 2, 0)
        dup_ref[0] = dpre.astype(BF16)
        dcb_ref[...] = jnp.sum(dgate, axis=0, keepdims=True)
        dcw_ref[0:1, :] = jnp.sum(dgate * pltpu.roll(pre, 2, 0), axis=0, keepdims=True)
        dcw_ref[1:2, :] = jnp.sum(dgate * pltpu.roll(pre, 1, 0), axis=0, keepdims=True)
        dcw_ref[2:3, :] = jnp.sum(dgate * pre, axis=0, keepdims=True)

    return _pc(body, name=name, grid=(nb,),
               in_specs=[pl.BlockSpec((lp, d), lambda j: (0, 0)), pl.BlockSpec((tc, d), lambda j: (j, 0)),
                         pl.BlockSpec((2, lp, tc), lambda j: (0, 0, j)),
                         pl.BlockSpec((3, tc), lambda j: (0, j)), pl.BlockSpec((1, tc), lambda j: (0, j))],
               out_specs=[pl.BlockSpec((2, lp, tc), lambda j: (0, 0, j)),
                          pl.BlockSpec((3, tc), lambda j: (0, j)), pl.BlockSpec((1, tc), lambda j: (0, j))],
               out_shape=[jax.ShapeDtypeStruct((2, lp, fp), BF16), jax.ShapeDtypeStruct((3, fp), F32),
                          jax.ShapeDtypeStruct((1, fp), F32)],
               compiler_params=pltpu.CompilerParams(dimension_semantics=("parallel",)))(dh2, w_down, up, cw, cb)


def _key_limit(i, tq, lp):
    return min(lp, -(-((i + 1) * tq) // LANE) * LANE)


def _attn_mask(i, tq, nk):
    qrow = i * tq + lax.broadcasted_iota(jnp.int32, (tq, 1), 0)
    krow = lax.broadcasted_iota(jnp.int32, (1, nk), 1)
    return (krow >= PAD) & ((krow // CHUNK) <= (qrow // CHUNK)), qrow >= PAD


def _attn_scores(q, kn, kr, i, tq, scale):
    nt = (((1,), (1,)), ((), ()))
    s = lax.dot_general(q[:, :QK_NOPE], kn, nt, preferred_element_type=F32)
    s = s + lax.dot_general(q[:, QK_NOPE:], kr, nt, preferred_element_type=F32)
    mask, qvalid = _attn_mask(i, tq, kn.shape[0])
    return jnp.where(mask, s * scale, jnp.finfo(F32).min), qvalid


def _per_q_block(nq, fn):
    i = pl.program_id(1)
    for blk in range(nq):
        pl.when(i == blk)(functools.partial(fn, blk))


def _attn_fwd(qx, kv, kr, cfg, *, name):
    lp, h = cfg.LP, cfg.H
    tq = _tile(lp, 272, ROW_ALIGN)
    nq = lp // tq
    scale = 1.0 / math.sqrt(QK_NOPE + QK_ROPE)

    def body(q_ref, kn_ref, v_ref, kr_ref, o_ref, lse_ref):
        def block(blk):
            nk = _key_limit(blk, tq, lp)
            s, qvalid = _attn_scores(q_ref[...], kn_ref[:nk], kr_ref[:nk], blk, tq, scale)
            m = jnp.max(s, axis=-1, keepdims=True)
            p = jnp.exp(s - m)
            l = jnp.sum(p, axis=-1, keepdims=True)
            o = jnp.dot(p.astype(BF16), v_ref[:nk], preferred_element_type=F32) / l
            o_ref[...] = jnp.where(qvalid, o, 0.0)
            lse_ref[...] = m + jnp.log(l)

        _per_q_block(nq, block)

    return _pc(body, name=name, grid=(h, nq),
               in_specs=[pl.BlockSpec((tq, HEAD_SLOT), lambda hh, i: (i, hh)),
                         pl.BlockSpec((lp, QK_NOPE), lambda hh, i: (0, 2 * hh)),
                         pl.BlockSpec((lp, V_HEAD), lambda hh, i: (0, 2 * hh + 1)),
                         pl.BlockSpec((lp, LANE), lambda hh, i: (0, 0))],
               out_specs=[pl.BlockSpec((tq, V_HEAD), lambda hh, i: (i, hh)),
                          pl.BlockSpec((None, tq, 1), lambda hh, i: (hh, i, 0))],
               out_shape=[jax.ShapeDtypeStruct((lp, h * V_HEAD), F32), jax.ShapeDtypeStruct((h, lp, 1), F32)],
               compiler_params=pltpu.CompilerParams(dimension_semantics=("parallel", "parallel")))(qx, kv, kv, kr)


def _attn_bwd(qx, kv, kr, o, lse, do, cfg, *, name):
    lp, h = cfg.LP, cfg.H
    tq = _tile(lp, 272, ROW_ALIGN)
    nq = lp // tq
    scale = 1.0 / math.sqrt(QK_NOPE + QK_ROPE)
    tn_dims = (((0,), (0,)), ((), ()))

    def body(q_ref, kn_ref, v_ref, kr_ref, o_ref, lse_ref, do_ref, dq_ref, dkv_ref, dkr_ref, dkv_acc):
        hh, i = pl.program_id(0), pl.program_id(1)

        @pl.when(i == 0)
        def _():
            dkv_acc[...] = jnp.zeros_like(dkv_acc)

        @pl.when((i == 0) & (hh == 0))
        def _():
            dkr_ref[...] = jnp.zeros_like(dkr_ref)

        def block(blk):
            nk = _key_limit(blk, tq, lp)
            q, kn, v, krv = q_ref[...], kn_ref[:nk], v_ref[:nk], kr_ref[:nk]
            s, qvalid = _attn_scores(q, kn, krv, blk, tq, scale)
            dov = jnp.where(qvalid, do_ref[...], 0.0)
            p = jnp.exp(s - lse_ref[...])
            delta = jnp.sum(dov * o_ref[...], axis=-1, keepdims=True)
            dob = dov.astype(BF16)
            dp = lax.dot_general(dob, v, (((1,), (1,)), ((), ())), preferred_element_type=F32)
            ds = (p * (dp - delta) * scale).astype(BF16)
            dq_ref[:, :QK_NOPE] = jnp.dot(ds, kn, preferred_element_type=F32)
            dq_ref[:, QK_NOPE:] = jnp.dot(ds, krv, preferred_element_type=F32)
            dkv_acc[:nk, :QK_NOPE] += lax.dot_general(ds, q[:, :QK_NOPE], tn_dims, preferred_element_type=F32)
            dkv_acc[:nk, QK_NOPE:] += lax.dot_general(p.astype(BF16), dob, tn_dims, preferred_element_type=F32)
            dkr_ref[:nk, :] += lax.dot_general(ds, q[:, QK_NOPE:], tn_dims, preferred_element_type=F32)

        _per_q_block(nq, block)

        @pl.when(i == nq - 1)
        def _():
            dkv_ref[...] = dkv_acc[...].astype(BF16)

    return _pc(body, name=name, grid=(h, nq),
               in_specs=[pl.BlockSpec((tq, HEAD_SLOT), lambda hh, i: (i, hh)),
                         pl.BlockSpec((lp, QK_NOPE), lambda hh, i: (0, 2 * hh)),
                         pl.BlockSpec((lp, V_HEAD), lambda hh, i: (0, 2 * hh + 1)),
                         pl.BlockSpec((lp, LANE), lambda hh, i: (0, 0)),
                         pl.BlockSpec((tq, V_HEAD), lambda hh, i: (i, hh)),
                         pl.BlockSpec((None, tq, 1), lambda hh, i: (hh, i, 0)),
                         pl.BlockSpec((tq, V_HEAD), lambda hh, i: (i, hh))],
               out_specs=[pl.BlockSpec((tq, HEAD_SLOT), lambda hh, i: (i, hh)),
                          pl.BlockSpec((lp, QK_NOPE + V_HEAD), lambda hh, i: (0, hh)),
                          pl.BlockSpec((lp, LANE), lambda hh, i: (0, 0))],
               out_shape=[jax.ShapeDtypeStruct((lp, h * HEAD_SLOT), F32),
                          jax.ShapeDtypeStruct((lp, h * (QK_NOPE + V_HEAD)), BF16),
                          jax.ShapeDtypeStruct((lp, LANE), F32)],
               scratch_shapes=[pltpu.VMEM((lp, QK_NOPE + V_HEAD), F32)],
               compiler_params=pltpu.CompilerParams(dimension_semantics=("arbitrary", "arbitrary")))(qx, kv, kv, kr, o, lse, do)


def _rot_half(x):
    lane = lax.broadcasted_iota(jnp.int32, x.shape, 1)
    half = QK_ROPE // 2
    return jnp.where(lane < half, -pltpu.roll(x, LANE - half, 1), pltpu.roll(x, half, 1))


def _rope(x, cos, sin):
    return x * cos + _rot_half(x) * sin


def _unrope(dy, cos, sin):
    return dy * cos - _rot_half(dy * sin)


def _rope_heads(fn, h):
    def apply(rid, q, cos, sin):
        parts = []
        for hh in range(h):
            parts.append(q[:, hh * HEAD_SLOT: hh * HEAD_SLOT + QK_NOPE])
            parts.append(fn(q[:, hh * HEAD_SLOT + QK_NOPE: (hh + 1) * HEAD_SLOT], cos, sin))
        return jnp.concatenate(parts, axis=1)
    return apply


ANY = pl.BlockSpec(memory_space=pl.ANY)


def _place():
    x, y, c = lax.axis_index("x"), lax.axis_index("y"), lax.axis_index("c")
    chips = [(1 - x, y), (x, 1 - y), (1 - x, 1 - y)]
    return x, y, c, chips


def _rcopy(src, dst, send_sem, recv_sem, dev):
    return pltpu.make_async_remote_copy(src_ref=src, dst_ref=dst, send_sem=send_sem, recv_sem=recv_sem,
                                        device_id=dev, device_id_type=MESH)


def _place_shard(shard, dtype, *, name, order=None, rows_to=None):
    shard = shard if shard.ndim == 3 else shard[None]
    n, r, cols = shard.shape
    rp = rows_to or r
    tm = _tile(r, max(ROW_ALIGN, PLACE_BLOCK_BYTES // (4 * cols)), ROW_ALIGN)
    me = (2 * lax.axis_index("x") + lax.axis_index("y")).astype(jnp.int32).reshape(1)
    extra = [] if order is None else [order]

    def body(me_ref, s_ref, *rest):
        rest[-1][...] = s_ref[...].astype(dtype)

    full = _pc(body, name=name,
               grid_spec=pltpu.PrefetchScalarGridSpec(
                   num_scalar_prefetch=1, grid=(n, r // tm),
                   in_specs=[pl.BlockSpec((None, tm, cols), lambda q, i, mr: (q, i, 0))] + [ANY] * len(extra),
                   out_specs=pl.BlockSpec((None, tm, cols), lambda q, i, mr: (mr[0] * n + q, i, 0))),
               out_shape=jax.ShapeDtypeStruct((4 * n, rp, cols), dtype),
               compiler_params=pltpu.CompilerParams(dimension_semantics=("arbitrary", "arbitrary")))(me, shard, *extra)
    if rp > r:
        pad = rp - r
        assert r % pad == 0

        def zero(me_ref, f_ref, o_ref):
            o_ref[...] = jnp.zeros_like(o_ref)

        full = _pc(zero, name=name + "_pad",
                   grid_spec=pltpu.PrefetchScalarGridSpec(
                       num_scalar_prefetch=1, grid=(n,), in_specs=[ANY],
                       out_specs=pl.BlockSpec((None, pad, cols), lambda q, mr: (mr[0] * n + q, r // pad, 0))),
                   out_shape=jax.ShapeDtypeStruct(full.shape, dtype), input_output_aliases={1: 0},
                   compiler_params=pltpu.CompilerParams(dimension_semantics=("arbitrary",)))(me, full)
    return full.reshape(4 * n * rp, cols)


def _allgather(fulls, *, name):
    n = len(fulls)

    def body(*refs):
        outs = refs[n:2 * n]
        send_sems, recv_sems = refs[2 * n:]
        x, y, c, chips = _place()
        sib = (x, y, 1 - c)
        me = 2 * x + y

        def rows(t, s, half):
            hrows = outs[t].shape[0] // 8
            return outs[t].at[pl.ds((2 * s + half) * hrows, hrows)]

        sent = []
        for t in range(n):
            for j, (cx, cy) in enumerate(chips):
                cp = _rcopy(rows(t, me, c), rows(t, me, c), send_sems.at[6 * t + j], recv_sems.at[6 * t + j], (cx, cy, c))
                cp.start()
                sent.append(cp)
        for t in range(n):
            for j, (cx, cy) in enumerate(chips):
                landed = rows(t, 2 * cx + cy, c)
                _rcopy(landed, landed, send_sems.at[6 * t + j], recv_sems.at[6 * t + j], (cx, cy, c)).wait_recv()
                cp = _rcopy(landed, landed, send_sems.at[6 * t + 3 + j], recv_sems.at[6 * t + 3 + j], sib)
                cp.start()
                sent.append(cp)
        for t in range(n):
            for j, (cx, cy) in enumerate(chips):
                other = rows(t, 2 * cx + cy, 1 - c)
                _rcopy(other, other, send_sems.at[6 * t + 3 + j], recv_sems.at[6 * t + 3 + j], sib).wait_recv()
        for cp in sent:
            cp.wait_send()

    return _pc(body, name=name, in_specs=[ANY] * n, out_specs=[ANY] * n,
               out_shape=[jax.ShapeDtypeStruct(f.shape, f.dtype) for f in fulls],
               input_output_aliases={t: t for t in range(n)},
               scratch_shapes=[pltpu.SemaphoreType.DMA((6 * n,)), pltpu.SemaphoreType.DMA((6 * n,))])(*fulls)


HBM = pl.BlockSpec(memory_space=pltpu.HBM)
SEM = pl.BlockSpec(memory_space=pltpu.SEMAPHORE)
EFFECT = pltpu.SideEffectType.DATAFLOW_SIDE_EFFECTING
TOKEN = jax.ShapeDtypeStruct((8, LANE), F32)


def _in_hbm(a):
    return pltpu.with_memory_space_constraint(a, pltpu.HBM)


def _half_rows(ref, s, half):
    hrows = ref.shape[0] // 8
    return ref.at[pl.ds((2 * s + half) * hrows, hrows)]


def _split_start(bufs, copies, n_copies, *, name, before=None):
    n = len(bufs)
    extra = [] if before is None else [before]

    def body(*refs):
        send_sems, recv_sems, token = refs[n + len(extra)], refs[n + len(extra) + 1], refs[-1]
        for k, (src, dst, dev) in enumerate(copies(refs[:n])):
            _rcopy(src, dst, send_sems.at[k], recv_sems.at[k], dev).start()
        token[...] = jnp.zeros_like(token)

    res = _pc(body, name=name, in_specs=[HBM] * n + [ANY] * len(extra),
              out_specs=[SEM, SEM] + [HBM] * n + [pl.BlockSpec(memory_space=pltpu.VMEM)],
              out_shape=[pltpu.SemaphoreType.DMA((n_copies,)), pltpu.SemaphoreType.DMA((n_copies,))]
              + [pltpu.HBM(b.shape, b.dtype) for b in bufs] + [TOKEN],
              input_output_aliases={t: 2 + t for t in range(n)},
              compiler_params=pltpu.CompilerParams(has_side_effects=EFFECT))(*[_in_hbm(b) for b in bufs], *extra)
    return res[0], res[1], list(res[2:2 + n]), res[-1]


def _split_wait(send_sems, recv_sems, bufs, copies, after, *, name):
    n = len(bufs)

    def body(*refs):
        send_ref, recv_ref = refs[n], refs[n + 1]
        for k, (src, dst, dev) in enumerate(copies(refs[:n])):
            cp = _rcopy(src, dst, send_ref.at[k], recv_ref.at[k], dev)
            cp.wait_send()
            cp.wait_recv()

    return _pc(body, name=name, in_specs=[HBM] * n + [SEM, SEM, ANY], out_specs=[HBM] * n,
               out_shape=[pltpu.HBM(b.shape, b.dtype) for b in bufs],
               input_output_aliases={t: t for t in range(n)},
               compiler_params=pltpu.CompilerParams(has_side_effects=EFFECT))(*bufs, send_sems, recv_sems, after)


def _allgather_ici_copies(refs):
    x, y, c, chips = _place()
    return [(_half_rows(r, 2 * x + y, c), _half_rows(r, 2 * x + y, c), (cx, cy, c)) for r in refs for cx, cy in chips]


def _rs_chips_copies(refs):
    x, y, c, chips = _place()
    n = len(refs) // 2
    return [(refs[t].at[2 * cx + cy], refs[n + t].at[j], (cx, cy, c)) for t in range(n) for j, (cx, cy) in enumerate(chips)]


def _allgather_forward_copies(refs):
    x, y, c, chips = _place()
    return [(_half_rows(r, 2 * cx + cy, c), _half_rows(r, 2 * cx + cy, c), (x, y, 1 - c)) for r in refs for cx, cy in chips]


def _rs_final_copies(refs):
    x, y, c, _ = _place()
    return [(r.at[c], r.at[c], (x, y, 1 - c)) for r in refs]


def _rs_sibling_copies(refs):
    x, y, c, _ = _place()
    n = len(refs) // 2
    out = []
    for t in range(n):
        h = refs[t].shape[0] // 8
        out += [(refs[t].at[pl.ds((2 * s + 1 - c) * h, h)], refs[n + t].at[s], (x, y, 1 - c)) for s in range(4)]
    return out


def _allgather_forward(fulls, *, name):
    n = len(fulls)

    def body(*refs):
        outs = refs[n:2 * n]
        send_sems, recv_sems = refs[2 * n:]
        x, y, c, chips = _place()
        sent = []
        for t in range(n):
            for j, (cx, cy) in enumerate(chips):
                landed = _half_rows(outs[t], 2 * cx + cy, c)
                cp = _rcopy(landed, landed, send_sems.at[3 * t + j], recv_sems.at[3 * t + j], (x, y, 1 - c))
                cp.start()
                sent.append(cp)
        for t in range(n):
            for j, (cx, cy) in enumerate(chips):
                other = _half_rows(outs[t], 2 * cx + cy, 1 - c)
                _rcopy(other, other, send_sems.at[3 * t + j], recv_sems.at[3 * t + j], (x, y, 1 - c)).wait_recv()
        for cp in sent:
            cp.wait_send()

    return _pc(body, name=name, in_specs=[ANY] * n, out_specs=[ANY] * n,
               out_shape=[jax.ShapeDtypeStruct(f.shape, f.dtype) for f in fulls],
               input_output_aliases={t: t for t in range(n)},
               scratch_shapes=[pltpu.SemaphoreType.DMA((3 * n,)), pltpu.SemaphoreType.DMA((3 * n,))])(*fulls)


def _rs_sibling(grads, *, name):
    n = len(grads)

    def body(*refs):
        ins, outs = refs[:n], refs[n:2 * n]
        send_sems, recv_sems = refs[2 * n:]
        x, y, c, _ = _place()
        cps = []
        for t in range(n):
            h = ins[t].shape[0] // 8
            for s in range(4):
                cp = _rcopy(ins[t].at[pl.ds((2 * s + 1 - c) * h, h)], outs[t].at[s], send_sems.at[4 * t + s],
                            recv_sems.at[4 * t + s], (x, y, 1 - c))
                cp.start()
                cps.append(cp)
        for cp in cps:
            cp.wait()

    return _pc(body, name=name, in_specs=[ANY] * n, out_specs=[ANY] * n,
               out_shape=[jax.ShapeDtypeStruct((4, g.shape[0] // 8, g.shape[1]), g.dtype) for g in grads],
               scratch_shapes=[pltpu.SemaphoreType.DMA((4 * n,)), pltpu.SemaphoreType.DMA((4 * n,))])(*grads)


def _rs_chips(sends, *, name):
    n = len(sends)

    def body(*refs):
        s_refs, b_refs = refs[:n], refs[n:2 * n]
        send_sems, recv_sems = refs[2 * n:]
        x, y, c, chips = _place()
        cps = []
        for t in range(n):
            for j, (cx, cy) in enumerate(chips):
                cp = _rcopy(s_refs[t].at[2 * cx + cy], b_refs[t].at[j], send_sems.at[3 * t + j], recv_sems.at[3 * t + j],
                            (cx, cy, c))
                cp.start()
                cps.append(cp)
        for cp in cps:
            cp.wait()

    return _pc(body, name=name, in_specs=[ANY] * n, out_specs=[ANY] * n,
               out_shape=[jax.ShapeDtypeStruct((3,) + s.shape[1:], s.dtype) for s in sends],
               scratch_shapes=[pltpu.SemaphoreType.DMA((3 * n,)), pltpu.SemaphoreType.DMA((3 * n,))])(*sends)


def _rs_final(fulls, *, name):
    n = len(fulls)

    def body(*refs):
        outs = refs[n:2 * n]
        send_sems, recv_sems = refs[2 * n:]
        x, y, c, _ = _place()
        cps = []
        for t in range(n):
            cp = _rcopy(outs[t].at[c], outs[t].at[c], send_sems.at[t], recv_sems.at[t], (x, y, 1 - c))
            cp.start()
            cps.append(cp)
        for cp in cps:
            cp.wait()

    return _pc(body, name=name, in_specs=[ANY] * n, out_specs=[ANY] * n,
               out_shape=[jax.ShapeDtypeStruct(f.shape, f.dtype) for f in fulls],
               input_output_aliases={t: t for t in range(n)},
               scratch_shapes=[pltpu.SemaphoreType.DMA((n,)), pltpu.SemaphoreType.DMA((n,))])(*fulls)


def _add_halves(g, a, send_dtype, *, name):
    _, h, cols = a.shape
    th = _row_tile(h, cols)
    g4 = g.reshape(4, 2, h, cols)
    idx = jnp.stack([lax.axis_index("c"), 2 * lax.axis_index("x") + lax.axis_index("y")]).astype(jnp.int32)

    def shard(k, ir):
        return (ir[1] + 1 + k) % 4

    def body(idx_ref, g_ref, a_ref, p_ref, s_ref):
        v = g_ref[...].astype(F32) + a_ref[...].astype(F32)
        s_ref[...] = v.astype(send_dtype)

        @pl.when(pl.program_id(1) == 3)
        def _():
            p_ref[...] = v

    return _pc(body, name=name,
               grid_spec=pltpu.PrefetchScalarGridSpec(
                   num_scalar_prefetch=1, grid=(h // th, 4),
                   in_specs=[pl.BlockSpec((None, None, th, cols), lambda i, k, ir: (shard(k, ir), ir[0], i, 0)),
                             pl.BlockSpec((None, th, cols), lambda i, k, ir: (shard(k, ir), i, 0))],
                   out_specs=[pl.BlockSpec((th, cols), lambda i, k, ir: (i, 0)),
                              pl.BlockSpec((None, th, cols), lambda i, k, ir: (shard(k, ir), i, 0))]),
               out_shape=[jax.ShapeDtypeStruct((h, cols), F32), jax.ShapeDtypeStruct(a.shape, send_dtype)],
               compiler_params=pltpu.CompilerParams(dimension_semantics=("arbitrary", "arbitrary")))(idx, g4, a)


def _add_chips(p, b, *, name, order=None):
    h, cols = p.shape
    th = _row_tile(h, cols)
    idx = lax.axis_index("c").astype(jnp.int32).reshape(1)
    extra = [] if order is None else [order]

    def body(idx_ref, p_ref, b_ref, *rest):
        r_ref = rest[-1]
        r_ref[...] = ((p_ref[...] + b_ref[0].astype(F32)) + b_ref[1].astype(F32)) + b_ref[2].astype(F32)

    return _pc(body, name=name,
               grid_spec=pltpu.PrefetchScalarGridSpec(
                   num_scalar_prefetch=1, grid=(h // th,),
                   in_specs=[pl.BlockSpec((th, cols), lambda i, ir: (i, 0)),
                             pl.BlockSpec((3, th, cols), lambda i, ir: (0, i, 0))] + [ANY] * len(extra),
                   out_specs=pl.BlockSpec((None, th, cols), lambda i, ir: (ir[0], i, 0))),
               out_shape=jax.ShapeDtypeStruct((2, h, cols), F32),
               compiler_params=pltpu.CompilerParams(dimension_semantics=("arbitrary",)))(idx, p, b, *extra)


def _add_halves_all(grads, recv, send_dtypes, tag):
    parts, sends = [], []
    for t, (g, a) in enumerate(zip(grads, recv)):
        p, s = _add_halves(g, a, send_dtypes[t], name=f"rs_add_halves_{tag}{t}")
        parts.append(p)
        sends.append(s)
    return parts, sends


def _rs_finish(parts, others, tag, order=None):
    halves = [_add_chips(p, b, order=order, name=f"rs_add_chips_{tag}{t}") for t, (p, b) in enumerate(zip(parts, others))]
    full = _rs_final(halves, name=f"rs_final_{tag}")
    return [f.reshape(-1, f.shape[-1]) for f in full]


def _s5_discretize(lam_re, lam_im, log_dt, b_re, b_im):
    lam = lax.complex(lam_re, lam_im)
    dt = jnp.exp(log_dt)[:, None]
    lam_bar = jnp.exp(lam * dt)
    b_bar = ((lam_bar - 1.0) / lam)[..., None] * lax.complex(b_re, b_im)
    return jnp.real(lam_bar), jnp.imag(lam_bar), jnp.real(b_bar), jnp.imag(b_bar)


def _lanes_from_gp(re, im, cfg):
    v = jnp.stack([re, im]).reshape(2, cfg.NB, GROUPS_PER_BLOCK, SSM_STATE)
    return jnp.transpose(v, (1, 0, 2, 3)).reshape(1, cfg.NL)


def _gp_from_lanes(v, cfg):
    v = jnp.transpose(v.reshape(cfg.NB, 2, GROUPS_PER_BLOCK, SSM_STATE), (1, 0, 2, 3)).reshape(2, cfg.G, SSM_STATE)
    return v[0], v[1]


def _bb_band(bb_re, bb_im, cfg):
    eye = jnp.eye(GROUPS_PER_BLOCK, dtype=F32)
    bb = jnp.stack([bb_re, bb_im]).reshape(2, cfg.NB, GROUPS_PER_BLOCK, SSM_STATE, SSM_GROUP)
    return jnp.einsum('rjgpc,gh->jgcrhp', bb, eye).reshape(cfg.DS, 2 * GROUPS_PER_BLOCK * SSM_STATE)


def _bb_from_band(m, cfg):
    eye = jnp.eye(GROUPS_PER_BLOCK, dtype=F32)
    m = m.reshape(cfg.NB, GROUPS_PER_BLOCK, SSM_GROUP, 2, GROUPS_PER_BLOCK, SSM_STATE)
    v = jnp.einsum('jgcrhp,gh->rjgpc', m, eye).reshape(2, cfg.G, SSM_STATE, SSM_GROUP)
    return v[0], v[1]


def _cc_band(c_re, c_im, cfg):
    eye = jnp.eye(GROUPS_PER_BLOCK, dtype=F32)
    cc = jnp.stack([c_re, -c_im]).reshape(2, cfg.NB, GROUPS_PER_BLOCK, SSM_GROUP, SSM_STATE)
    return jnp.einsum('rjgcp,gh->jrhpgc', cc, eye).reshape(cfg.NL, GROUPS_PER_BLOCK * SSM_GROUP)


def _cc_from_band(m, cfg):
    eye = jnp.eye(GROUPS_PER_BLOCK, dtype=F32)
    m = m.reshape(cfg.NB, 2, GROUPS_PER_BLOCK, SSM_STATE, GROUPS_PER_BLOCK, SSM_GROUP)
    v = jnp.einsum('jrhpgc,gh->rjgcp', m, eye).reshape(2, cfg.G, SSM_GROUP, SSM_STATE)
    return v[0], -v[1]


PACK_COLS = 512
PACK_ROW_ALIGN = 64


def _pack(arrs):
    flat = jnp.concatenate([a.reshape(-1).astype(F32) for a in arrs])
    unit = PACK_COLS * PACK_ROW_ALIGN
    total = -(-flat.shape[0] // unit) * unit
    return jnp.pad(flat, (0, total - flat.shape[0])).reshape(-1, PACK_COLS)


def _unpack(p, shapes):
    flat = p.reshape(-1)
    out, off = [], 0
    for shp in shapes:
        size = math.prod(shp)
        out.append(flat[off:off + size].reshape(shp))
        off += size
    return out


def _adamw(w, g, m, v, *, name, emit_grad=False):
    c1 = 1.0 / (1.0 - ADAM_B1 ** ADAM_STEP)
    c2 = 1.0 / (1.0 - ADAM_B2 ** ADAM_STEP)

    if w.ndim == 2:
        outs = _adamw(w[None], g[None], m[None], v[None], name=name, emit_grad=emit_grad)
        return [o[0] for o in outs]
    lead, rows, cols = w.shape
    tc = _tile(cols, 512)
    tm = _tile(rows, max(8, ADAMW_BLOCK_BYTES // (4 * tc)), 8)
    n_out = 4 if emit_grad else 3

    def body(w_ref, g_ref, m_ref, v_ref, *o_refs):
        gv = g_ref[...]
        mn = ADAM_B1 * m_ref[...] + (1.0 - ADAM_B1) * gv
        vn = ADAM_B2 * v_ref[...] + (1.0 - ADAM_B2) * (gv * gv)
        delta = -ADAM_LR * ((mn * c1) / (jnp.sqrt(vn * c2) + ADAM_EPS) + ADAM_WD * w_ref[...])
        for o_ref, val in zip(o_refs, ((gv, delta, mn, vn) if emit_grad else (delta, mn, vn))):
            o_ref[...] = val

    blk = pl.BlockSpec((None, tm, tc), lambda n, i, j: (n, i, j))
    return _pc(body, name=name, grid=(lead, rows // tm, cols // tc), in_specs=[blk] * 4, out_specs=[blk] * n_out,
               out_shape=[jax.ShapeDtypeStruct((lead, rows, cols), F32)] * n_out,
               compiler_params=pltpu.CompilerParams(dimension_semantics=("parallel", "parallel", "parallel")))(w, g, m, v)


def _to_comm_layout(name, w, cfg):
    w = w[0]
    if name == 'w_in':
        return jnp.pad(w, ((0, 0), (0, cfg.DINP - cfg.DIN)))
    if name == 'w_q_b':
        hs = w.shape[1] // (QK_NOPE + QK_ROPE)
        wt = w.T.reshape(hs, QK_NOPE + QK_ROPE, cfg.QL)
        return jnp.pad(wt, ((0, 0), (0, HEAD_SLOT - QK_NOPE - QK_ROPE), (0, 0))).reshape(hs * HEAD_SLOT, cfg.QL)
    if name == 'w_kv_b':
        return w.T
    if name == 'w_up':
        return w.T.reshape(2, cfg.F // 4, cfg.D)
    return w


def _from_comm_layout(name, g, cfg):
    if name == 'w_in':
        g = g[:, :cfg.DIN]
    elif name == 'w_q_b':
        hs = g.shape[0] // HEAD_SLOT
        g = g.reshape(hs, HEAD_SLOT, cfg.QL)[:, :QK_NOPE + QK_ROPE].reshape(hs * (QK_NOPE + QK_ROPE), cfg.QL).T
    elif name == 'w_kv_b':
        g = g.T
    elif name == 'w_up':
        g = g.reshape(2, cfg.FQ, cfg.D)[:, :cfg.F // 4].reshape(cfg.F // 2, cfg.D).T
    elif name == 'w_down':
        g = g[:cfg.F // 4]
    return g[None]


def _ff_pad(v, cfg):
    k = v.shape[0]
    return jnp.pad(v.reshape(k, 4, cfg.F // 4), ((0, 0), (0, 0), (0, cfg.FQ - cfg.F // 4))).reshape(k, cfg.FP)


def _ff_unpad(v, cfg):
    k = v.shape[0]
    return v.reshape(k, 4, cfg.FQ)[:, :, :cfg.F // 4].reshape(k, cfg.F)


def _step(cfg, w, m, v, x, loss_target):
    lp, d, ds, nl = cfg.LP, cfg.D, cfg.DS, cfg.NL
    xi, yi = lax.axis_index("x"), lax.axis_index("y")
    me = 2 * xi + yi

    def place(n, order=None):
        rows_to = cfg.FQ if n in ('w_up', 'w_down') else None
        return _place_shard(_to_comm_layout(n, w[n], cfg), BF16, order=order, rows_to=rows_to, name=f"place_{n}")

    first = [place('w_in'), _place_shard(w['meta_tokens'], F32, name="place_meta")]
    f_send, f_recv, f_flying, f_token = _split_start(first, _allgather_ici_copies, 6, name="allgather_first_start")
    conv_w_shard = jnp.pad(w['conv_w'][0], ((0, ROW_ALIGN - 3), (0, cfg.FQ - cfg.F // 4)))
    placed = [None] + [place(n, f_token) for n in BIG[1:]]
    placed += [None, _place_shard(conv_w_shard, F32, order=f_token, name="place_conv_w")]
    f_landed = _split_wait(f_send, f_recv, f_flying, _allgather_ici_copies, placed[6], name="allgather_first_wait")
    w_in, meta_full = _allgather_forward(f_landed, name="allgather_first_forward")
    meta = jnp.transpose(meta_full.reshape(4, N_META, d // 4), (1, 0, 2)).reshape(N_META, d)
    conv_b = _ff_pad(w['conv_b'], cfg)
    mid = placed[1:5] + [placed[8]]
    mid_send, mid_recv, mid_flying, mid_token = _split_start(mid, _allgather_ici_copies, 3 * len(mid), before=meta_full,
                                                             name="allgather_mid_start")
    ffn_send, ffn_recv, ffn_flying, ffn_token = _split_start(placed[5:7], _allgather_ici_copies, 6, before=mid_token,
                                                             name="allgather_ffn_start")
    mix_norm = w['mix_norm'] + (mid_token[0:1, 0:1] + ffn_token[0:1, 0:1])

    pos = (jnp.arange(lp, dtype=jnp.int32) - PAD).astype(F32)
    inv_freq = 1.0 / (ROPE_BASE ** (jnp.arange(0, QK_ROPE, 2, dtype=F32) / QK_ROPE))
    ang = pos[:, None] * inv_freq[None, :]
    zpad = jnp.zeros((lp, LANE - QK_ROPE), F32)
    cos_t = jnp.concatenate([jnp.cos(ang), jnp.cos(ang), zpad], axis=1)
    sin_t = jnp.concatenate([jnp.sin(ang), jnp.sin(ang), zpad], axis=1)

    s5_in = (w['lam_re'][0], w['lam_im'][0], w['log_dt'][0], w['b_re'][0], w['b_im'][0])
    (a_re, a_im, bb_re, bb_im), s5_vjp = jax.vjp(_s5_discretize, *s5_in)
    lam_dt = lax.complex(s5_in[0], s5_in[1]) * jnp.exp(s5_in[2])[:, None]
    a_pow = jnp.exp(jnp.arange(1, 9, dtype=F32)[:, None, None] * lam_dt[None])
    r8 = jnp.arange(8)
    step_f = jnp.stack([jnp.where((r8 >= k)[:, None, None], a_pow[k - 1][None], 0.0) for k in (1, 2, 4)]).reshape(24, cfg.G, -1)
    step_b = jnp.stack([jnp.where((r8 < 8 - k)[:, None, None], a_pow[k - 1][None], 0.0) for k in (1, 2, 4)]).reshape(24, cfg.G, -1)
    rows_f = jnp.concatenate([a_pow, step_f])
    rows_b = jnp.conj(jnp.concatenate([a_pow[::-1], step_b]))

    def lane_rows(t):
        v = jnp.stack([jnp.real(t), jnp.imag(t)], axis=1).reshape(t.shape[0], 2, cfg.NB, GROUPS_PER_BLOCK, SSM_STATE)
        return jnp.transpose(v, (0, 2, 1, 3, 4)).reshape(t.shape[0], cfg.NL)

    pw_fwd, pw_bwd = lane_rows(rows_f), lane_rows(rows_b)
    bb_band = _bb_band(bb_re, bb_im, cfg).astype(BF16)
    cc_band = _cc_band(w['c_re'][0], w['c_im'][0], cfg).astype(BF16)
    d_skip, b_glu = w['d_skip'], w['b_glu']

    h0 = jnp.concatenate([jnp.zeros((PAD, d), F32), meta, x[0]], axis=0)
    xn = _rms_fwd(h0, mix_norm, name="rms_mix")
    z = _mm(xn, w_in, name="mm_in", tn=_tile(cfg.DINP, 640))
    u = (z, ds, 0)
    q_a = (z, cfg.QL, ds // cfg.QL)
    kv_a = (z, cfg.KVL, (ds + cfg.QL) // cfg.KVL)
    k_pe = (z, LANE, (ds + cfg.QL + cfg.KVL) // LANE)

    hs, yc = _s5_fwd(z, bb_band, cc_band, pw_fwd, cfg, name="s5_fwd")

    def s5_y(ycv, uv, dk):
        return ycv + dk * uv

    gl = _ew(lambda rid, ycv, uv, dk: jax.nn.gelu(s5_y(ycv, uv, dk)), [yc, u], [d_skip], [(ds, BF16)], name="s5_gelu")[0]
    mid_landed = _split_wait(mid_send, mid_recv, mid_flying, _allgather_ici_copies, gl, name="allgather_mid_wait")
    w_glu, w_qt, w_kvt, w_out, conv_full = _allgather_forward(mid_landed, name="allgather_mid_forward")
    conv_w = jnp.transpose(conv_full.reshape(4, ROW_ALIGN, cfg.FQ)[:, :3], (1, 0, 2)).reshape(3, cfg.FP)
    tg = _mm(gl, w_glu, name="mm_glu")
    ya = _ew(lambda rid, ycv, uv, tv, dk, bg: jax.nn.gelu(s5_y(ycv, uv, dk)) * jax.nn.sigmoid(tv + bg),
             [yc, u, tg], [d_skip, b_glu], [(ds, F32)], name="s5_glu")[0]

    qn = _rms_fwd(q_a, w['q_a_norm'], name="rms_q")
    kvn = _rms_fwd(kv_a, w['kv_a_norm'], name="rms_kv")
    q_raw = _mm(qn, w_qt, tb=True, name="mm_q")
    qx = _ew(_rope_heads(_rope, cfg.H), [q_raw, cos_t, sin_t], [], [(cfg.H * HEAD_SLOT, BF16)], name="rope_q")[0]
    kv = _mm(kvn, w_kvt, tb=True, out_dtype=BF16, name="mm_kv")
    kr = _ew(lambda rid, kp, cs, sn: _rope(kp, cs, sn), [k_pe, cos_t, sin_t], [], [(LANE, BF16)], name="rope_k")[0]
    o, lse = _attn_fwd(qx, kv, kr, cfg, name="attn_fwd")

    def norm2(rid, yav, ov, gs, ga):
        return jnp.concatenate([_rms_parts(yav, gs)[0] * gs, _rms_parts(ov, ga)[0] * ga], axis=1)

    ffn_landed = _split_wait(ffn_send, ffn_recv, ffn_flying, _allgather_ici_copies, o, name="allgather_ffn_wait")
    ff_send, ff_recv, ff_flying, ff_token = _split_start(ffn_landed, _allgather_forward_copies, 6,
                                                         name="allgather_ffn_forward_start")
    yn = _ew(norm2, [ya, o], [w['out_norm_ssm'] + ff_token[0:1, 0:1], w['out_norm_attn']], [(cfg.DMIX, BF16)],
             name="rms_out")[0]
    h1 = _mm(yn, w_out, res=h0, name="mm_out")
    xn2 = _rms_fwd(h1, w['ffn_norm'], name="rms_ffn")
    w_upt, w_down = _split_wait(ff_send, ff_recv, ff_flying, _allgather_forward_copies, xn2,
                                name="allgather_ffn_forward_wait")
    up, act = _ffn_up(xn2, w_upt, conv_w, conv_b, name="ffn_up")
    h2 = _mm(act, w_down, res=h1, tm=_tile(lp, 544, ROW_ALIGN), name="mm_down")

    g_final = w['final_norm'].reshape(1, d)

    def head(rid, hv, tv, gv):
        xhat, r = _rms_parts(hv, gv)
        valid = rid >= PAD + N_META
        diff = jnp.where(valid, xhat * gv - tv, 0.0)
        dout = diff * (1.0 / d)
        dxhat = dout * gv
        dx = r * (dxhat - xhat * jnp.mean(dxhat * xhat, axis=-1, keepdims=True))
        return dx, dx, dout * xhat, 0.5 * diff * dout

    dh2, dh2_b, dg_final, loss_cols = _ew(head, [h2, (loss_target[0], d, 0, SKIP)], [g_final], [(d, F32), (d, BF16)], [d, d],
                                          tm=PAD + N_META, name="loss_head")
    loss = lax.psum(jnp.sum(loss_cols), ("x", "y", "c"))

    dw_down = _mm(act, dh2_b, ta=True, tn=d, tm=512, out_dtype=BF16, name="mm_dw_down")

    def sibling_start(g, tag):
        land = lax.empty((4, g.shape[0] // 8, g.shape[1]), g.dtype)
        return _split_start([g, land], _rs_sibling_copies, 4, name=f"rs_sibling_{tag}_start")

    dn_send, dn_recv, dn_flying, dn_token = sibling_start(dw_down, "down")
    dup, dconv_w, dconv_b = _ffn_dact(dh2_b, w_down, up, conv_w, conv_b + dn_token[0:1, 0:1], name="ffn_dact")
    tk_up, tm_up = _tile(cfg.FP, 1408), _tile(cfg.FP, 512)
    dw_upt = _mm(dup, xn2, ta=True, dims=(2 * cfg.FP, d, lp), tn=d, tm=tm_up, a_lead=True, out_dtype=BF16, name="mm_dw_up",
                 a_idx=lambda i, j, k: (i // (cfg.FP // tm_up), 0, i % (cfg.FP // tm_up)))
    up_send, up_recv, up_flying, up_token = sibling_start(dw_upt, "up")
    dxn2 = _mm(dup, w_upt, dims=(lp, d, 2 * cfg.FP), tk=tk_up, tn=1024, a_lead=True, name="mm_dxn2",
               a_idx=lambda i, j, k: (k // (cfg.FP // tk_up), i, k % (cfg.FP // tk_up)))
    dh1, dh1_b, dg_ffn = _rms_bwd(h1, w['ffn_norm'] + up_token[0:1, 0:1], dxn2, res=dh2, mask=True, with_bf16=True,
                                  name="rms_ffn_bwd")

    dyn = _mm(dh1_b, w_out, tb=True, name="mm_dyn")
    dw_out = _mm(yn, dh1_b, ta=True, tn=d, tm=512, name="mm_dw_out")
    up_done = _split_wait(up_send, up_recv, up_flying, _rs_sibling_copies, dw_out, name="rs_sibling_up_wait")
    dn_done = _split_wait(dn_send, dn_recv, dn_flying, _rs_sibling_copies, dw_out, name="rs_sibling_down_wait")
    early_parts, early_sends = _add_halves_all([up_done[0], dn_done[0]], [up_done[1], dn_done[1]], [BF16] * 2, "early")
    chip_lands = [lax.empty((3,) + s.shape[1:], s.dtype) for s in early_sends]
    ch_send, ch_recv, ch_flying, ch_token = _split_start(early_sends + chip_lands, _rs_chips_copies, 6,
                                                         name="rs_chips_early_start")
    dya, dg_ssm = _rms_bwd(ya, w['out_norm_ssm'] + ch_token[0:1, 0:1], (dyn, ds, 0), name="rms_ssm_bwd")
    do, dg_attn = _rms_bwd(o, w['out_norm_attn'], (dyn, cfg.DATTN, ds // cfg.DATTN), name="rms_attn_bwd")

    dqx, dkv, dkr = _attn_bwd(qx, kv, kr, o, lse, do, cfg, name="attn_bwd")
    dq_raw = _ew(_rope_heads(_unrope, cfg.H), [dqx, cos_t, sin_t], [], [(cfg.H * HEAD_SLOT, BF16)], name="unrope_q")[0]
    dk_pe = _ew(lambda rid, dk, cs, sn: _unrope(dk, cs, sn), [dkr, cos_t, sin_t], [], [(LANE, F32)], name="unrope_k")[0]
    dqn = _mm(dq_raw, w_qt, name="mm_dqn")
    dw_qt = _mm(dq_raw, qn, ta=True, tm=512, name="mm_dw_q")
    dkvn = _mm(dkv, w_kvt, name="mm_dkvn")
    dw_kvt = _mm(dkv, kvn, ta=True, tm=512, name="mm_dw_kv")
    dq_a, dg_q = _rms_bwd(q_a, w['q_a_norm'], dqn, name="rms_q_bwd")
    dkv_a, dg_kv = _rms_bwd(kv_a, w['kv_a_norm'], dkvn, name="rms_kv_bwd")

    def glu_bwd(rid, ycv, uv, tv, dyav, dk, bg):
        gelu = jax.nn.gelu(s5_y(ycv, uv, dk))
        sg = jax.nn.sigmoid(tv + bg)
        dt = dyav * gelu * sg * (1.0 - sg)
        return dt, dyav * sg, dt

    dt_b, dgl1, db_glu = _ew(glu_bwd, [yc, u, tg, dya], [d_skip, b_glu], [(ds, BF16), (ds, F32)], [ds], name="s5_glu_bwd")
    dgl = _mm(dt_b, w_glu, tb=True, res=dgl1, name="mm_dgl")
    dw_glu = _mm(gl, dt_b, ta=True, tm=512, name="mm_dw_glu")

    def gelu_bwd(rid, ycv, uv, dglv, dk):
        _, vjp = jax.vjp(jax.nn.gelu, s5_y(ycv, uv, dk))
        dy = vjp(dglv)[0]
        return dy, dy * dk, dy * uv

    mid_grads = [dw_out, dw_glu, dw_qt, dw_kvt]
    mid_lands = [lax.empty((4, g.shape[0] // 8, g.shape[1]), g.dtype) for g in mid_grads]
    ms_send, ms_recv, ms_flying, ms_token = _split_start(mid_grads + mid_lands, _rs_sibling_copies, 4 * len(mid_grads),
                                                         name="rs_sibling_mid_start")
    dy_b, du_skip, dd_skip = _ew(gelu_bwd, [yc, u, dgl], [d_skip + ms_token[0:1, 0:1]], [(ds, BF16), (ds, F32)], [ds],
                                 name="s5_gelu_bwd")
    ms_done = _split_wait(ms_send, ms_recv, ms_flying, _rs_sibling_copies, dy_b, name="rs_sibling_mid_wait")
    mid_parts, mid_sends = _add_halves_all(ms_done[:4], ms_done[4:], [BF16] * 4, "mid")
    mid_chip_lands = [lax.empty((3,) + s.shape[1:], s.dtype) for s in mid_sends]
    mc_send, mc_recv, mc_flying, mc_token = _split_start(mid_sends + mid_chip_lands, _rs_chips_copies, 3 * len(mid_sends),
                                                         name="rs_chips_mid_start")
    du, dbb_band, dcc_band, da_l = _s5_bwd(dy_b, hs, z, bb_band, cc_band, pw_bwd + mc_token[0:1, 0:1], du_skip, cfg,
                                           name="s5_bwd")

    dz = jnp.concatenate([du, dq_a, dkv_a, dk_pe], axis=1).astype(BF16)
    dxn = _mm(dz, w_in, tb=True, name="mm_dxn")
    dw_in = _mm(xn, dz, ta=True, tm=512, tn=_tile(cfg.DINP, 1024), name="mm_dw_in")
    def mix_bwd(rid, xv, dyv, resv, gv):
        dx, dg = _rms_bwd_block(xv, gv, dyv)
        dx = dx + resv
        return dx, dx, dg

    grad_x, dh0_head, dg_mix = _ew(mix_bwd, [h0, dxn, dh1], [mix_norm], [(d, F32, SKIP), (d, F32, FIRST)], [d],
                                   tm=PAD + N_META, name="rms_mix_bwd")
    grad_x = grad_x[None]

    da_re, da_im = _gp_from_lanes(da_l, cfg)
    dbb_re, dbb_im = _bb_from_band(dbb_band, cfg)
    dlam_re, dlam_im, dlog_dt, db_re, db_im = s5_vjp((da_re, da_im, dbb_re, dbb_im))
    dc_re, dc_im = _cc_from_band(dcc_band, cfg)
    local_small = {
        'meta_tokens': dh0_head[PAD:], 'mix_norm': dg_mix, 'lam_re': dlam_re, 'lam_im': dlam_im, 'log_dt': dlog_dt,
        'b_re': db_re, 'b_im': db_im, 'c_re': dc_re, 'c_im': dc_im, 'd_skip': dd_skip, 'b_glu': db_glu, 'q_a_norm': dg_q,
        'kv_a_norm': dg_kv, 'out_norm_ssm': dg_ssm, 'out_norm_attn': dg_attn, 'ffn_norm': dg_ffn,
        'conv_w': _ff_unpad(dconv_w, cfg), 'conv_b': _ff_unpad(dconv_b, cfg), 'final_norm': dg_final,
    }
    small_shapes = [local_small[n].shape for n in SMALL]

    small_pack = _pack([local_small[n] for n in SMALL])
    ch_done = _split_wait(ch_send, ch_recv, ch_flying, _rs_chips_copies, small_pack, name="rs_chips_early_wait")
    early_halves = [_add_chips(p, b, name=f"rs_add_chips_early{t}") for t, (p, b) in enumerate(zip(early_parts, ch_done[2:]))]
    fe_send, fe_recv, fe_flying, fe_token = _split_start(early_halves, _rs_final_copies, 2, name="rs_final_early_start")
    end_local = [dw_in, small_pack + fe_token[0:1, 0:1]]
    end_recv = _rs_sibling(end_local, name="rs_sibling_end")
    end_parts, end_sends = _add_halves_all(end_local, end_recv, [BF16, F32], "end")
    end_lands = [lax.empty((3,) + s.shape[1:], s.dtype) for s in end_sends]
    ec_send, ec_recv, ec_flying, ec_token = _split_start(end_sends + end_lands, _rs_chips_copies, 3 * len(end_sends),
                                                         name="rs_chips_end_start")
    fe_done = _split_wait(fe_send, fe_recv, fe_flying, _rs_final_copies, ec_token, name="rs_final_early_wait")
    red_up, red_down = [f.reshape(-1, f.shape[-1]) for f in fe_done]

    delta, new_m, new_v, grads = {}, {}, {}, {}
    padded_rows = ('w_down',)

    def adamw_big(n, red):
        shp = w[n].shape
        w2, m2, v2 = [t.reshape(shp[-2], shp[-1]) for t in (w[n], m[n], v[n])]
        if n in padded_rows:
            g2, dl, mn, vn = _adamw(w2, red, m2, v2, emit_grad=True, name=f"adamw_{n}")
            grads[n] = g2.reshape(shp)
        else:
            grads[n] = _from_comm_layout(n, red, cfg)
            dl, mn, vn = _adamw(w2, grads[n].reshape(shp[-2], shp[-1]), m2, v2, name=f"adamw_{n}")
        delta[n], new_m[n], new_v[n] = dl.reshape(shp), mn.reshape(shp), vn.reshape(shp)

    def adamw_up(red):
        q = cfg.F // 4
        wt, mt, vt = [jnp.transpose(t[0]).reshape(2, q, d) for t in (w['w_up'], m['w_up'], v['w_up'])]
        outs = _adamw(wt, red.reshape(2, cfg.FQ, d), mt, vt, emit_grad=True, name="adamw_w_up")
        grads['w_up'], delta['w_up'], new_m['w_up'], new_v['w_up'] = [jnp.transpose(t.reshape(2 * q, d))[None] for t in outs]

    adamw_up(red_up)
    adamw_big('w_down', red_down)
    mc_done = _split_wait(mc_send, mc_recv, mc_flying, _rs_chips_copies, delta['w_down'], name="rs_chips_mid_wait")
    ec_done = _split_wait(ec_send, ec_recv, ec_flying, _rs_chips_copies, mc_done[0], name="rs_chips_end_wait")
    red = _rs_finish(mid_parts + end_parts, list(mc_done[len(mid_sends):]) + list(ec_done[len(end_sends):]), "rest")
    small_full = _allgather([_place_shard(red[5], F32, name="place_small")], name="allgather_small")[0]
    small_sum = dict(zip(SMALL, _unpack(small_full, small_shapes)))
    for n, r in zip(['w_out', 'w_glu', 'w_q_b', 'w_kv_b'], red[:4]):
        adamw_big(n, r)
    in_t = [jnp.transpose(t[0]) for t in (w['w_in'], m['w_in'], v['w_in'])]
    outs = _adamw(in_t[0], jnp.transpose(red[4][:, :cfg.DIN]), in_t[1], in_t[2], emit_grad=True, name="adamw_w_in")
    grads['w_in'], delta['w_in'], new_m['w_in'], new_v['w_in'] = [jnp.transpose(t)[None] for t in outs]

    for n in SMALL:
        g = small_sum[n]
        if n == 'meta_tokens':
            g = lax.dynamic_slice_in_dim(g, me * (d // 4), d // 4, axis=1)
        elif n == 'conv_w':
            g = lax.dynamic_slice_in_dim(g, me * (cfg.F // 4), cfg.F // 4, axis=1)[None]
        else:
            g = g.reshape(w[n].shape)
        grads[n] = g

    shapes = [w[n].shape for n in SMALL]
    packs = [_pack([src[n] for n in SMALL]) for src in (w, grads, m, v)]
    for dst, p in zip((delta, new_m, new_v), _adamw(*packs, name="adamw_small")):
        dst.update(zip(SMALL, _unpack(p, shapes)))

    return (loss, grad_x, *[grads[n] for n in WEIGHTS], *[delta[n] for n in WEIGHTS],
            *[new_m[n] for n in WEIGHTS], *[new_v[n] for n in WEIGHTS])


def kernel(x, meta_tokens, mix_norm, w_in, lam_re, lam_im, log_dt, b_re, b_im, c_re, c_im, d_skip, w_glu, b_glu, q_a_norm, w_q_b, kv_a_norm, w_kv_b, out_norm_ssm, out_norm_attn, w_out, ffn_norm, w_up, conv_w, conv_b, w_down, final_norm, loss_target, m_meta_tokens, m_mix_norm, m_w_in, m_lam_re, m_lam_im, m_log_dt, m_b_re, m_b_im, m_c_re, m_c_im, m_d_skip, m_w_glu, m_b_glu, m_q_a_norm, m_w_q_b, m_kv_a_norm, m_w_kv_b, m_out_norm_ssm, m_out_norm_attn, m_w_out, m_ffn_norm, m_w_up, m_conv_w, m_conv_b, m_w_down, m_final_norm, v_meta_tokens, v_mix_norm, v_w_in, v_lam_re, v_lam_im, v_log_dt, v_b_re, v_b_im, v_c_re, v_c_im, v_d_skip, v_w_glu, v_b_glu, v_q_a_norm, v_w_q_b, v_kv_a_norm, v_w_kv_b, v_out_norm_ssm, v_out_norm_attn, v_w_out, v_ffn_norm, v_w_up, v_conv_w, v_conv_b, v_w_down, v_final_norm):
    args = dict(locals())
    w = {n: args[n] for n in WEIGHTS}
    m = {n: args["m_" + n] for n in WEIGHTS}
    v = {n: args["v_" + n] for n in WEIGHTS}
    return _step(PROD, w, m, v, x, loss_target)
```

```python
import functools
import math
from typing import NamedTuple

import jax
import jax.numpy as jnp
from jax import lax
from jax.experimental import pallas as pl
from jax.experimental.pallas import tpu as pltpu

F32, BF16 = jnp.float32, jnp.bfloat16
MESH = pl.DeviceIdType.MESH
LANE = 128
ROW_ALIGN = 16
N_META = 16
PAD = 112
CHUNK = 64
SSM_GROUP = 16
SSM_STATE = 64
GROUPS_PER_BLOCK = 8
QK_NOPE, QK_ROPE, V_HEAD = 128, 64, 128
HEAD_SLOT = 256
ROPE_BASE = 10000.0
EPS = 1e-6
ADAM_LR, ADAM_B1, ADAM_B2, ADAM_EPS, ADAM_WD, ADAM_STEP = 0.001, 0.9, 0.999, 1e-08, 0.01, 10
DT_F32_BLOCK_BYTES = 9 << 18
ADAMW_BLOCK_BYTES = 3 << 19
PLACE_BLOCK_BYTES = 6 << 20
SKIP, FIRST = "skip", "first"


class Cfg(NamedTuple):
    D: int
    S: int
    DS: int
    H: int
    QL: int
    KVL: int
    F: int

    @property
    def LP(self):
        return PAD + N_META + self.S

    @property
    def G(self):
        return self.DS // SSM_GROUP

    @property
    def NB(self):
        return self.G // GROUPS_PER_BLOCK

    @property
    def NL(self):
        return 2 * self.G * SSM_STATE

    @property
    def DATTN(self):
        return self.H * V_HEAD

    @property
    def DMIX(self):
        return self.DS + self.DATTN

    @property
    def DIN(self):
        return self.DS + self.QL + self.KVL + QK_ROPE

    @property
    def DINP(self):
        return self.DS + self.QL + self.KVL + LANE

    @property
    def FQ(self):
        return -(-(self.F // 4) // LANE) * LANE

    @property
    def FP(self):
        return 4 * self.FQ


PROD = Cfg(D=2048, S=2048, DS=1024, H=8, QL=512, KVL=256, F=5504)

WEIGHTS = ['meta_tokens', 'mix_norm', 'w_in', 'lam_re', 'lam_im', 'log_dt', 'b_re', 'b_im', 'c_re', 'c_im', 'd_skip',
           'w_glu', 'b_glu', 'q_a_norm', 'w_q_b', 'kv_a_norm', 'w_kv_b', 'out_norm_ssm', 'out_norm_attn', 'w_out',
           'ffn_norm', 'w_up', 'conv_w', 'conv_b', 'w_down', 'final_norm']
BIG = ['w_in', 'w_glu', 'w_q_b', 'w_kv_b', 'w_out', 'w_up', 'w_down']
SMALL = [n for n in WEIGHTS if n not in BIG]


def _pc(body, **kw):
    return pl.pallas_call(body, **kw)


def _tile(n, target, align=LANE):
    best = None
    d = align
    while d <= min(n, target):
        if n % d == 0:
            best = d
        d += align
    return best if best is not None else n


def _row_tile(rows, cols):
    return _tile(rows, max(ROW_ALIGN, DT_F32_BLOCK_BYTES // (4 * cols)), ROW_ALIGN)


def _mm(a, b, *, name, ta=False, tb=False, tm=None, tn=512, tk=None, out_dtype=F32, res=None,
        a_idx=None, b_idx=None, dims=None, a_lead=False):
    if dims is None:
        m, k = (a.shape[1], a.shape[0]) if ta else a.shape
        n = b.shape[0] if tb else b.shape[1]
    else:
        m, n, k = dims
    tm = _tile(m, tm or m, LANE if ta else ROW_ALIGN)
    tn = _tile(n, tn)
    tk = _tile(k, tk or k, ROW_ALIGN if (ta and not tb) else LANE)
    nm, nn, nk = m // tm, n // tn, k // tk
    a_idx = a_idx or ((lambda i, j, kk: (kk, i)) if ta else (lambda i, j, kk: (i, kk)))
    b_idx = b_idx or ((lambda i, j, kk: (j, kk)) if tb else (lambda i, j, kk: (kk, j)))
    dn = (((0 if ta else 1,), (1 if tb else 0,)), ((), ()))

    def body(*refs):
        a_ref, b_ref = refs[0], refs[1]
        r_ref = refs[2] if res is not None else None
        o_ref = refs[3] if res is not None else refs[2]
        d = lax.dot_general(a_ref[...].astype(BF16), b_ref[...].astype(BF16), dn, preferred_element_type=F32)

        def finish(r):
            if r_ref is not None:
                r = r + r_ref[...].astype(F32)
            o_ref[...] = r.astype(out_dtype)

        if nk == 1:
            finish(d)
        else:
            acc = refs[-1]
            kk = pl.program_id(2)

            @pl.when(kk == 0)
            def _():
                acc[...] = d

            @pl.when(kk > 0)
            def _():
                acc[...] += d

            @pl.when(kk == nk - 1)
            def _():
                finish(acc[...])

    a_blk = ((None,) if a_lead else ()) + ((tk, tm) if ta else (tm, tk))
    in_specs = [pl.BlockSpec(a_blk, a_idx), pl.BlockSpec((tn, tk) if tb else (tk, tn), b_idx)]
    args = [a, b]
    if res is not None:
        in_specs.append(pl.BlockSpec((tm, tn), lambda i, j, kk: (i, j)))
        args.append(res)
    return _pc(body, name=name, grid=(nm, nn, nk), in_specs=in_specs,
               out_specs=pl.BlockSpec((tm, tn), lambda i, j, kk: (i, j)),
               out_shape=jax.ShapeDtypeStruct((m, n), out_dtype),
               scratch_shapes=[pltpu.VMEM((tm, tn), F32)] if nk > 1 else [],
               compiler_params=pltpu.CompilerParams(dimension_semantics=("parallel", "parallel", "arbitrary")))(*args)


def _ew(fn, ins, vecs, outs, sums=(), *, name, tm=None):
    ins = [x if isinstance(x, tuple) else (x, x.shape[1], 0) for x in ins]
    ins = [x if len(x) == 4 else x + (None,) for x in ins]
    outs = [o if len(o) == 3 else o + (None,) for o in outs]
    rows = ins[0][0].shape[0]
    cmax = max([c for _, c, _, _ in ins] + [c for c, _, _ in outs])
    tm = tm or _row_tile(rows, cmax)
    n_in, n_vec, n_out, n_sum = len(ins), len(vecs), len(outs), len(sums)

    def body(*refs):
        i = pl.program_id(0)
        rid = i * tm + lax.broadcasted_iota(jnp.int32, (tm, 1), 0)
        vals = [r[...] for r in refs[:n_in + n_vec]]
        res = fn(rid, *vals)
        res = res if isinstance(res, (tuple, list)) else (res,)
        o_refs = refs[n_in + n_vec:]
        for o_ref, r, (_, _, mode) in zip(o_refs[:n_out], res[:n_out], outs):
            if mode == FIRST:
                @pl.when(i == 0)
                def _():
                    o_ref[...] = r.astype(o_ref.dtype)
            else:
                o_ref[...] = r.astype(o_ref.dtype)
        for o_ref, r in zip(o_refs[n_out:], res[n_out:]):
            part = jnp.sum(r.astype(F32), axis=0, keepdims=True)

            @pl.when(i == 0)
            def _():
                o_ref[...] = part

            @pl.when(i > 0)
            def _():
                o_ref[...] += part

    def row_idx(mode):
        if mode == SKIP:
            return lambda i, cb=0: (jnp.maximum(i - 1, 0), cb)
        if mode == FIRST:
            return lambda i, cb=0: (0, cb)
        return lambda i, cb=0: (i, cb)

    in_specs = [pl.BlockSpec((tm, c), functools.partial(row_idx(mode), cb=cb)) for _, c, cb, mode in ins]
    in_specs += [pl.BlockSpec(v.shape, functools.partial(lambda i, nd: (0,) * nd, nd=v.ndim)) for v in vecs]
    out_specs = [pl.BlockSpec((tm, c), row_idx(mode)) for c, _, mode in outs]
    out_specs += [pl.BlockSpec((1, c), lambda i: (0, 0)) for c in sums]
    out_rows = {None: rows, SKIP: rows - tm, FIRST: tm}
    out_shape = [jax.ShapeDtypeStruct((out_rows[mode], c), dt) for c, dt, mode in outs]
    out_shape += [jax.ShapeDtypeStruct((1, c), F32) for c in sums]
    return _pc(body, name=name, grid=(rows // tm,), in_specs=in_specs, out_specs=out_specs, out_shape=out_shape,
               compiler_params=pltpu.CompilerParams(dimension_semantics=("arbitrary",)))(*[x[0] for x in ins], *vecs)


def _rms_parts(x, g):
    r = lax.rsqrt(jnp.mean(x * x, axis=-1, keepdims=True) + EPS)
    return x * r, r


def _rms_bwd_block(x, g, dy):
    xhat, r = _rms_parts(x, g)
    dxhat = dy * g
    dx = r * (dxhat - xhat * jnp.mean(dxhat * xhat, axis=-1, keepdims=True))
    return dx, dy * xhat


def _rms_fwd(x, g, *, name):
    c = x[1] if isinstance(x, tuple) else x.shape[1]
    return _ew(lambda rid, xv, gv: _rms_parts(xv.astype(F32), gv)[0] * gv, [x], [g], [(c, BF16)], name=name)[0]


def _rms_bwd(x, g, dy, *, name, res=None, mask=False, with_bf16=False):
    c = x[1] if isinstance(x, tuple) else x.shape[1]

    def fn(rid, xv, dyv, *rest):
        gv = rest[-1]
        dx, dg = _rms_bwd_block(xv.astype(F32), gv, dyv.astype(F32))
        if res is not None:
            dx = dx + rest[0]
        if mask:
            dx = jnp.where(rid >= PAD, dx, 0.0)
        return (dx, dx, dg) if with_bf16 else (dx, dg)

    ins = [x, dy] + ([res] if res is not None else [])
    outs = [(c, F32)] + ([(c, BF16)] if with_bf16 else [])
    return _ew(fn, ins, [g], outs, [c], name=name)


S5_W = GROUPS_PER_BLOCK * SSM_STATE
S5_GW = GROUPS_PER_BLOCK * SSM_GROUP
S5_UNROLL = 8
S5_DA_ROWS = 272


def _s5_scan_in_place(ref, pw_ref, *, reverse):
    lp = ref.shape[0]
    tile_rows = 8
    chunk = _tile(lp, S5_DA_ROWS, tile_rows)
    tiles = chunk // tile_rows

    def chunk_body(c, carry):
        rows = pl.ds(pl.multiple_of(c * chunk, tile_rows), chunk)
        xr, xi = ref[rows, :S5_W], ref[rows, S5_W:]
        for level, k in enumerate((1, 2, 4)):
            base = tile_rows * (1 + level)
            mr, mi = pw_ref[base:base + tile_rows, :S5_W][None], pw_ref[base:base + tile_rows, S5_W:][None]
            shift = chunk - k if reverse else k
            sr = pltpu.roll(xr, shift, 0).reshape(tiles, tile_rows, S5_W)
            si = pltpu.roll(xi, shift, 0).reshape(tiles, tile_rows, S5_W)
            xr = xr + (mr * sr - mi * si).reshape(chunk, S5_W)
            xi = xi + (mr * si + mi * sr).reshape(chunk, S5_W)
        ref[rows, :S5_W] = xr
        ref[rows, S5_W:] = xi
        return carry

    lax.fori_loop(0, lp // chunk, chunk_body, 0)

    pr, pi = pw_ref[0:tile_rows, :S5_W], pw_ref[0:tile_rows, S5_W:]
    ntile = lp // tile_rows
    unroll = 4

    def step(n, carry):
        cr, ci = carry
        for q in range(unroll):
            j = n * unroll + q
            j = ntile - 1 - j if reverse else j
            rows = pl.ds(pl.multiple_of(j * tile_rows, tile_rows), tile_rows)
            nr = ref[rows, :S5_W] + (pr * cr - pi * ci)
            ni = ref[rows, S5_W:] + (pr * ci + pi * cr)
            ref[rows, :S5_W] = nr
            ref[rows, S5_W:] = ni
            cr, ci = (nr[0:1], ni[0:1]) if reverse else (nr[tile_rows - 1:], ni[tile_rows - 1:])
        return cr, ci

    z = jnp.zeros((1, S5_W), F32)
    lax.fori_loop(0, ntile // unroll, step, (z, z))


def _s5_fwd(z, bb_band, cc_band, a_l, cfg, *, name):
    lp, ds, nl = cfg.LP, cfg.DS, cfg.NL

    def body(u_ref, bb_ref, cc_ref, a_ref, hs_ref, y_ref):
        hs_ref[...] = jnp.dot(u_ref[...].astype(BF16), bb_ref[...], preferred_element_type=F32)
        _s5_scan_in_place(hs_ref, a_ref, reverse=False)
        y_ref[...] = jnp.dot(hs_ref[...].astype(BF16), cc_ref[...], preferred_element_type=F32)

    return _pc(body, name=name, grid=(cfg.NB,),
               in_specs=[pl.BlockSpec((lp, S5_GW), lambda j: (0, j)), pl.BlockSpec((S5_GW, 2 * S5_W), lambda j: (j, 0)),
                         pl.BlockSpec((2 * S5_W, S5_GW), lambda j: (j, 0)), pl.BlockSpec((32, 2 * S5_W), lambda j: (0, j))],
               out_specs=[pl.BlockSpec((lp, 2 * S5_W), lambda j: (0, j)), pl.BlockSpec((lp, S5_GW), lambda j: (0, j))],
               out_shape=[jax.ShapeDtypeStruct((lp, nl), F32), jax.ShapeDtypeStruct((lp, ds), F32)],
               compiler_params=pltpu.CompilerParams(dimension_semantics=("parallel",)))(z, bb_band, cc_band, a_l)


def _s5_bwd(dy, hs, z, bb_band, cc_band, a_l, du_skip, cfg, *, name):
    lp, ds, nl = cfg.LP, cfg.DS, cfg.NL
    nt = (((1,), (1,)), ((), ()))
    tn = (((0,), (0,)), ((), ()))

    def body(dy_ref, hs_ref, u_ref, bb_ref, cc_ref, a_ref, sk_ref, du_ref, dbb_ref, dcc_ref, da_ref, g_ref):
        dyv = dy_ref[...]
        g_ref[...] = lax.dot_general(dyv, cc_ref[...], nt, preferred_element_type=F32)
        _s5_scan_in_place(g_ref, a_ref, reverse=True)
        dcc_ref[...] = lax.dot_general(hs_ref[...].astype(BF16), dyv, tn, preferred_element_type=F32)
        gb = g_ref[...].astype(BF16)
        dbb_ref[...] = lax.dot_general(u_ref[...].astype(BF16), gb, tn, preferred_element_type=F32)
        du_ref[...] = lax.dot_general(gb, bb_ref[...], nt, preferred_element_type=F32) + sk_ref[...]
        dre = jnp.zeros((1, S5_W), F32)
        dim = jnp.zeros((1, S5_W), F32)
        for r0 in range(0, lp, S5_DA_ROWS):
            rows = min(S5_DA_ROWS, lp - r0)
            first = lax.broadcasted_iota(jnp.int32, (rows, 1), 0) == 0
            prev = hs_ref[r0 - 1:r0, :] if r0 else jnp.zeros((1, 2 * S5_W), F32)
            hr = jnp.where(first, prev[:, :S5_W], pltpu.roll(hs_ref[r0:r0 + rows, :S5_W], 1, 0))
            hi = jnp.where(first, prev[:, S5_W:], pltpu.roll(hs_ref[r0:r0 + rows, S5_W:], 1, 0))
            gr, gi = g_ref[r0:r0 + rows, :S5_W], g_ref[r0:r0 + rows, S5_W:]
            dre = dre + jnp.sum(gr * hr + gi * hi, axis=0, keepdims=True)
            dim = dim + jnp.sum(gi * hr - gr * hi, axis=0, keepdims=True)
        da_ref[:, :S5_W] = dre
        da_ref[:, S5_W:] = dim

    col_blk = pl.BlockSpec((lp, S5_GW), lambda j: (0, j))
    lane_blk = pl.BlockSpec((lp, 2 * S5_W), lambda j: (0, j))
    bb_blk = pl.BlockSpec((S5_GW, 2 * S5_W), lambda j: (j, 0))
    cc_blk = pl.BlockSpec((2 * S5_W, S5_GW), lambda j: (j, 0))
    a_blk = pl.BlockSpec((1, 2 * S5_W), lambda j: (0, j))
    pw_blk = pl.BlockSpec((32, 2 * S5_W), lambda j: (0, j))
    return _pc(body, name=name, grid=(cfg.NB,),
               in_specs=[col_blk, lane_blk, col_blk, bb_blk, cc_blk, pw_blk, col_blk],
               out_specs=[col_blk, bb_blk, cc_blk, a_blk],
               out_shape=[jax.ShapeDtypeStruct((lp, ds), F32), jax.ShapeDtypeStruct((ds, 2 * S5_W), F32),
                          jax.ShapeDtypeStruct((nl, S5_GW), F32), jax.ShapeDtypeStruct((1, nl), F32)],
               scratch_shapes=[pltpu.VMEM((lp, 2 * S5_W), F32)],
               compiler_params=pltpu.CompilerParams(dimension_semantics=("parallel",)))(dy, hs, z, bb_band, cc_band, a_l, du_skip)


def _conv_gate(pre, cw, cb):
    return cw[0:1] * pltpu.roll(pre, 2, 0) + cw[1:2] * pltpu.roll(pre, 1, 0) + cw[2:3] * pre + cb


def _ffn_up(xn2, w_upt, cw, cb, *, name):
    lp, d = xn2.shape
    fp = w_upt.shape[0] // 2
    tc = _tile(fp, 256)
    nb = fp // tc

    def body(x_ref, wg_ref, wv_ref, cw_ref, cb_ref, up_ref, act_ref):
        wcat = jnp.concatenate([wg_ref[...], wv_ref[...]], axis=0)
        r = lax.dot_general(x_ref[...], wcat, (((1,), (1,)), ((), ())), preferred_element_type=F32)
        pre, val = r[:, :tc].astype(BF16), r[:, tc:].astype(BF16)
        up_ref[0] = pre
        up_ref[1] = val
        gate = _conv_gate(pre.astype(F32), cw_ref[...], cb_ref[...])
        act_ref[...] = (jax.nn.silu(gate) * val.astype(F32)).astype(BF16)

    return _pc(body, name=name, grid=(nb,),
               in_specs=[pl.BlockSpec((lp, d), lambda j: (0, 0)), pl.BlockSpec((tc, d), lambda j: (j, 0)),
                         pl.BlockSpec((tc, d), lambda j: (nb + j, 0)),
                         pl.BlockSpec((3, tc), lambda j: (0, j)), pl.BlockSpec((1, tc), lambda j: (0, j))],
               out_specs=[pl.BlockSpec((2, lp, tc), lambda j: (0, 0, j)), pl.BlockSpec((lp, tc), lambda j: (0, j))],
               out_shape=[jax.ShapeDtypeStruct((2, lp, fp), BF16), jax.ShapeDtypeStruct((lp, fp), BF16)],
               compiler_params=pltpu.CompilerParams(dimension_semantics=("parallel",)))(xn2, w_upt, w_upt, cw, cb)


def _ffn_dact(dh2, w_down, up, cw, cb, *, name):
    lp, d = dh2.shape
    fp = w_down.shape[0]
    tc = _tile(fp, 256)
    nb = fp // tc

    def body(dh_ref, wd_ref, up_ref, cw_ref, cb_ref, dup_ref, dcw_ref, dcb_ref):
        da = lax.dot_general(dh_ref[...], wd_ref[...], (((1,), (1,)), ((), ())), preferred_element_type=F32)
        pre, val, cwv = up_ref[0].astype(F32), up_ref[1].astype(F32), cw_ref[...]
        gate = _conv_gate(pre, cwv, cb_ref[...])
        sg = jax.nn.sigmoid(gate)
        dup_ref[1] = (da * (gate * sg)).astype(BF16)
        dgate = da * val * (sg * (1.0 + gate * (1.0 - sg)))
        dpre = cwv[2:3] * dgate + cwv[1:2] * pltpu.roll(dgate, lp - 1, 0) + cwv[0:1] * pltpu.roll(dgate, lp - 2, 0)
        dup_ref[0] = dpre.astype(BF16)
        dcb_ref[...] = jnp.sum(dgate, axis=0, keepdims=True)
        dcw_ref[0:1, :] = jnp.sum(dgate * pltpu.roll(pre, 2, 0), axis=0, keepdims=True)
        dcw_ref[1:2, :] = jnp.sum(dgate * pltpu.roll(pre, 1, 0), axis=0, keepdims=True)
        dcw_ref[2:3, :] = jnp.sum(dgate * pre, axis=0, keepdims=True)

    return _pc(body, name=name, grid=(nb,),
               in_specs=[pl.BlockSpec((lp, d), lambda j: (0, 0)), pl.BlockSpec((tc, d), lambda j: (j, 0)),
                         pl.BlockSpec((2, lp, tc), lambda j: (0, 0, j)),
                         pl.BlockSpec((3, tc), lambda j: (0, j)), pl.BlockSpec((1, tc), lambda j: (0, j))],
               out_specs=[pl.BlockSpec((2, lp, tc), lambda j: (0, 0, j)),
                          pl.BlockSpec((3, tc), lambda j: (0, j)), pl.BlockSpec((1, tc), lambda j: (0, j))],
               out_shape=[jax.ShapeDtypeStruct((2, lp, fp), BF16), jax.ShapeDtypeStruct((3, fp), F32),
                          jax.ShapeDtypeStruct((1, fp), F32)],
               compiler_params=pltpu.CompilerParams(dimension_semantics=("parallel",)))(dh2, w_down, up, cw, cb)


def _key_limit(i, tq, lp):
    return min(lp, -(-((i + 1) * tq) // LANE) * LANE)


def _attn_mask(i, tq, nk):
    qrow = i * tq + lax.broadcasted_iota(jnp.int32, (tq, 1), 0)
    krow = lax.broadcasted_iota(jnp.int32, (1, nk), 1)
    return (krow >= PAD) & ((krow // CHUNK) <= (qrow // CHUNK)), qrow >= PAD


def _attn_scores(q, kn, kr, i, tq, scale):
    nt = (((1,), (1,)), ((), ()))
    s = lax.dot_general(q[:, :QK_NOPE], kn, nt, preferred_element_type=F32)
    s = s + lax.dot_general(q[:, QK_NOPE:], kr, nt, preferred_element_type=F32)
    mask, qvalid = _attn_mask(i, tq, kn.shape[0])
    return jnp.where(mask, s * scale, jnp.finfo(F32).min), qvalid


def _per_q_block(nq, fn):
    i = pl.program_id(1)
    for blk in range(nq):
        pl.when(i == blk)(functools.partial(fn, blk))


def _attn_fwd(qx, kv, kr, cfg, *, name):
    lp, h = cfg.LP, cfg.H
    tq = _tile(lp, 272, ROW_ALIGN)
    nq = lp // tq
    scale = 1.0 / math.sqrt(QK_NOPE + QK_ROPE)

    def body(q_ref, kn_ref, v_ref, kr_ref, o_ref, lse_ref):
        def block(blk):
            nk = _key_limit(blk, tq, lp)
            s, qvalid = _attn_scores(q_ref[...], kn_ref[:nk], kr_ref[:nk], blk, tq, scale)
            m = jnp.max(s, axis=-1, keepdims=True)
            p = jnp.exp(s - m)
            l = jnp.sum(p, axis=-1, keepdims=True)
            o = jnp.dot(p.astype(BF16), v_ref[:nk], preferred_element_type=F32) / l
            o_ref[...] = jnp.where(qvalid, o, 0.0)
            lse_ref[...] = m + jnp.log(l)

        _per_q_block(nq, block)

    return _pc(body, name=name, grid=(h, nq),
               in_specs=[pl.BlockSpec((tq, HEAD_SLOT), lambda hh, i: (i, hh)),
                         pl.BlockSpec((lp, QK_NOPE), lambda hh, i: (0, 2 * hh)),
                         pl.BlockSpec((lp, V_HEAD), lambda hh, i: (0, 2 * hh + 1)),
                         pl.BlockSpec((lp, LANE), lambda hh, i: (0, 0))],
               out_specs=[pl.BlockSpec((tq, V_HEAD), lambda hh, i: (i, hh)),
                          pl.BlockSpec((None, tq, 1), lambda hh, i: (hh, i, 0))],
               out_shape=[jax.ShapeDtypeStruct((lp, h * V_HEAD), F32), jax.ShapeDtypeStruct((h, lp, 1), F32)],
               compiler_params=pltpu.CompilerParams(dimension_semantics=("parallel", "parallel")))(qx, kv, kv, kr)


def _attn_bwd(qx, kv, kr, o, lse, do, cfg, *, name):
    lp, h = cfg.LP, cfg.H
    tq = _tile(lp, 272, ROW_ALIGN)
    nq = lp // tq
    scale = 1.0 / math.sqrt(QK_NOPE + QK_ROPE)
    tn_dims = (((0,), (0,)), ((), ()))

    def body(q_ref, kn_ref, v_ref, kr_ref, o_ref, lse_ref, do_ref, dq_ref, dkv_ref, dkr_ref, dkv_acc):
        hh, i = pl.program_id(0), pl.program_id(1)

        @pl.when(i == 0)
        def _():
            dkv_acc[...] = jnp.zeros_like(dkv_acc)

        @pl.when((i == 0) & (hh == 0))
        def _():
            dkr_ref[...] = jnp.zeros_like(dkr_ref)

        def block(blk):
            nk = _key_limit(blk, tq, lp)
            q, kn, v, krv = q_ref[...], kn_ref[:nk], v_ref[:nk], kr_ref[:nk]
            s, qvalid = _attn_scores(q, kn, krv, blk, tq, scale)
            dov = jnp.where(qvalid, do_ref[...], 0.0)
            p = jnp.exp(s - lse_ref[...])
            delta = jnp.sum(dov * o_ref[...], axis=-1, keepdims=True)
            dob = dov.astype(BF16)
            dp = lax.dot_general(dob, v, (((1,), (1,)), ((), ())), preferred_element_type=F32)
            ds = (p * (dp - delta) * scale).astype(BF16)
            dq_ref[:, :QK_NOPE] = jnp.dot(ds, kn, preferred_element_type=F32)
            dq_ref[:, QK_NOPE:] = jnp.dot(ds, krv, preferred_element_type=F32)
            dkv_acc[:nk, :QK_NOPE] += lax.dot_general(ds, q[:, :QK_NOPE], tn_dims, preferred_element_type=F32)
            dkv_acc[:nk, QK_NOPE:] += lax.dot_general(p.astype(BF16), dob, tn_dims, preferred_element_type=F32)
            dkr_ref[:nk, :] += lax.dot_general(ds, q[:, QK_NOPE:], tn_dims, preferred_element_type=F32)

        _per_q_block(nq, block)

        @pl.when(i == nq - 1)
        def _():
            dkv_ref[...] = dkv_acc[...].astype(BF16)

    return _pc(body, name=name, grid=(h, nq),
               in_specs=[pl.BlockSpec((tq, HEAD_SLOT), lambda hh, i: (i, hh)),
                         pl.BlockSpec((lp, QK_NOPE), lambda hh, i: (0, 2 * hh)),
                         pl.BlockSpec((lp, V_HEAD), lambda hh, i: (0, 2 * hh + 1)),
                         pl.BlockSpec((lp, LANE), lambda hh, i: (0, 0)),
                         pl.BlockSpec((tq, V_HEAD), lambda hh, i: (i, hh)),
                         pl.BlockSpec((None, tq, 1), lambda hh, i: (hh, i, 0)),
                         pl.BlockSpec((tq, V_HEAD), lambda hh, i: (i, hh))],
               out_specs=[pl.BlockSpec((tq, HEAD_SLOT), lambda hh, i: (i, hh)),
                          pl.BlockSpec((lp, QK_NOPE + V_HEAD), lambda hh, i: (0, hh)),
                          pl.BlockSpec((lp, LANE), lambda hh, i: (0, 0))],
               out_shape=[jax.ShapeDtypeStruct((lp, h * HEAD_SLOT), F32),
                          jax.ShapeDtypeStruct((lp, h * (QK_NOPE + V_HEAD)), BF16),
                          jax.ShapeDtypeStruct((lp, LANE), F32)],
               scratch_shapes=[pltpu.VMEM((lp, QK_NOPE + V_HEAD), F32)],
               compiler_params=pltpu.CompilerParams(dimension_semantics=("arbitrary", "arbitrary")))(qx, kv, kv, kr, o, lse, do)


def _rot_half(x):
    lane = lax.broadcasted_iota(jnp.int32, x.shape, 1)
    half = QK_ROPE // 2
    return jnp.where(lane < half, -pltpu.roll(x, LANE - half, 1), pltpu.roll(x, half, 1))


def _rope(x, cos, sin):
    return x * cos + _rot_half(x) * sin


def _unrope(dy, cos, sin):
    return dy * cos - _rot_half(dy * sin)


def _rope_heads(fn, h):
    def apply(rid, q, cos, sin):
        parts = []
        for hh in range(h):
            parts.append(q[:, hh * HEAD_SLOT: hh * HEAD_SLOT + QK_NOPE])
            parts.append(fn(q[:, hh * HEAD_SLOT + QK_NOPE: (hh + 1) * HEAD_SLOT], cos, sin))
        return jnp.concatenate(parts, axis=1)
    return apply


ANY = pl.BlockSpec(memory_space=pl.ANY)


def _place():
    x, y, c = lax.axis_index("x"), lax.axis_index("y"), lax.axis_index("c")
    chips = [(1 - x, y), (x, 1 - y), (1 - x, 1 - y)]
    return x, y, c, chips


def _rcopy(src, dst, send_sem, recv_sem, dev):
    return pltpu.make_async_remote_copy(src_ref=src, dst_ref=dst, send_sem=send_sem, recv_sem=recv_sem,
                                        device_id=dev, device_id_type=MESH)


def _place_shard(shard, dtype, *, name, order=None, rows_to=None):
    shard = shard if shard.ndim == 3 else shard[None]
    n, r, cols = shard.shape
    rp = rows_to or r
    tm = _tile(r, max(ROW_ALIGN, PLACE_BLOCK_BYTES // (4 * cols)), ROW_ALIGN)
    me = (2 * lax.axis_index("x") + lax.axis_index("y")).astype(jnp.int32).reshape(1)
    extra = [] if order is None else [order]

    def body(me_ref, s_ref, *rest):
        rest[-1][...] = s_ref[...].astype(dtype)

    full = _pc(body, name=name,
               grid_spec=pltpu.PrefetchScalarGridSpec(
                   num_scalar_prefetch=1, grid=(n, r // tm),
                   in_specs=[pl.BlockSpec((None, tm, cols), lambda q, i, mr: (q, i, 0))] + [ANY] * len(extra),
                   out_specs=pl.BlockSpec((None, tm, cols), lambda q, i, mr: (mr[0] * n + q, i, 0))),
               out_shape=jax.ShapeDtypeStruct((4 * n, rp, cols), dtype),
               compiler_params=pltpu.CompilerParams(dimension_semantics=("arbitrary", "arbitrary")))(me, shard, *extra)
    if rp > r:
        pad = rp - r
        assert r % pad == 0

        def zero(me_ref, f_ref, o_ref):
            o_ref[...] = jnp.zeros_like(o_ref)

        full = _pc(zero, name=name + "_pad",
                   grid_spec=pltpu.PrefetchScalarGridSpec(
                       num_scalar_prefetch=1, grid=(n,), in_specs=[ANY],
                       out_specs=pl.BlockSpec((None, pad, cols), lambda q, mr: (mr[0] * n + q, r // pad, 0))),
                   out_shape=jax.ShapeDtypeStruct(full.shape, dtype), input_output_aliases={1: 0},
                   compiler_params=pltpu.CompilerParams(dimension_semantics=("arbitrary",)))(me, full)
    return full.reshape(4 * n * rp, cols)


def _allgather(fulls, *, name):
    n = len(fulls)

    def body(*refs):
        outs = refs[n:2 * n]
        send_sems, recv_sems = refs[2 * n:]
        x, y, c, chips = _place()
        sib = (x, y, 1 - c)
        me = 2 * x + y

        def rows(t, s, half):
            hrows = outs[t].shape[0] // 8
            return outs[t].at[pl.ds((2 * s + half) * hrows, hrows)]

        sent = []
        for t in range(n):
            for j, (cx, cy) in enumerate(chips):
                cp = _rcopy(rows(t, me, c), rows(t, me, c), send_sems.at[6 * t + j], recv_sems.at[6 * t + j], (cx, cy, c))
                cp.start()
                sent.append(cp)
        for t in range(n):
            for j, (cx, cy) in enumerate(chips):
                landed = rows(t, 2 * cx + cy, c)
                _rcopy(landed, landed, send_sems.at[6 * t + j], recv_sems.at[6 * t + j], (cx, cy, c)).wait_recv()
                cp = _rcopy(landed, landed, send_sems.at[6 * t + 3 + j], recv_sems.at[6 * t + 3 + j], sib)
                cp.start()
                sent.append(cp)
        for t in range(n):
            for j, (cx, cy) in enumerate(chips):
                other = rows(t, 2 * cx + cy, 1 - c)
                _rcopy(other, other, send_sems.at[6 * t + 3 + j], recv_sems.at[6 * t + 3 + j], sib).wait_recv()
        for cp in sent:
            cp.wait_send()

    return _pc(body, name=name, in_specs=[ANY] * n, out_specs=[ANY] * n,
               out_shape=[jax.ShapeDtypeStruct(f.shape, f.dtype) for f in fulls],
               input_output_aliases={t: t for t in range(n)},
               scratch_shapes=[pltpu.SemaphoreType.DMA((6 * n,)), pltpu.SemaphoreType.DMA((6 * n,))])(*fulls)


HBM = pl.BlockSpec(memory_space=pltpu.HBM)
SEM = pl.BlockSpec(memory_space=pltpu.SEMAPHORE)
EFFECT = pltpu.SideEffectType.DATAFLOW_SIDE_EFFECTING
TOKEN = jax.ShapeDtypeStruct((8, LANE), F32)


def _in_hbm(a):
    return pltpu.with_memory_space_constraint(a, pltpu.HBM)


def _half_rows(ref, s, half):
    hrows = ref.shape[0] // 8
    return ref.at[pl.ds((2 * s + half) * hrows, hrows)]


def _split_start(bufs, copies, n_copies, *, name, before=None):
    n = len(bufs)
    extra = [] if before is None else [before]

    def body(*refs):
        send_sems, recv_sems, token = refs[n + len(extra)], refs[n + len(extra) + 1], refs[-1]
        for k, (src, dst, dev) in enumerate(copies(refs[:n])):
            _rcopy(src, dst, send_sems.at[k], recv_sems.at[k], dev).start()
        token[...] = jnp.zeros_like(token)

    res = _pc(body, name=name, in_specs=[HBM] * n + [ANY] * len(extra),
              out_specs=[SEM, SEM] + [HBM] * n + [pl.BlockSpec(memory_space=pltpu.VMEM)],
              out_shape=[pltpu.SemaphoreType.DMA((n_copies,)), pltpu.SemaphoreType.DMA((n_copies,))]
              + [pltpu.HBM(b.shape, b.dtype) for b in bufs] + [TOKEN],
              input_output_aliases={t: 2 + t for t in range(n)},
              compiler_params=pltpu.CompilerParams(has_side_effects=EFFECT))(*[_in_hbm(b) for b in bufs], *extra)
    return res[0], res[1], list(res[2:2 + n]), res[-1]


def _split_wait(send_sems, recv_sems, bufs, copies, after, *, name):
    n = len(bufs)
    after = list(after) if isinstance(after, (list, tuple)) else [after]

    def body(*refs):
        send_ref, recv_ref = refs[n], refs[n + 1]
        for k, (src, dst, dev) in enumerate(copies(refs[:n])):
            cp = _rcopy(src, dst, send_ref.at[k], recv_ref.at[k], dev)
            cp.wait_send()
            cp.wait_recv()

    return _pc(body, name=name, in_specs=[HBM] * n + [SEM, SEM] + [ANY] * len(after), out_specs=[HBM] * n,
               out_shape=[pltpu.HBM(b.shape, b.dtype) for b in bufs],
               input_output_aliases={t: t for t in range(n)},
               compiler_params=pltpu.CompilerParams(has_side_effects=EFFECT))(*bufs, send_sems, recv_sems, *after)


def _allgather_ici_copies(refs):
    x, y, c, chips = _place()
    return [(_half_rows(r, 2 * x + y, c), _half_rows(r, 2 * x + y, c), (cx, cy, c)) for r in refs for cx, cy in chips]


def _rs_chips_copies(refs):
    x, y, c, chips = _place()
    n = len(refs) // 2
    return [(refs[t].at[2 * cx + cy], refs[n + t].at[j], (cx, cy, c)) for t in range(n) for j, (cx, cy) in enumerate(chips)]


def _allgather_forward_copies(refs):
    x, y, c, chips = _place()
    return [(_half_rows(r, 2 * cx + cy, c), _half_rows(r, 2 * cx + cy, c), (x, y, 1 - c)) for r in refs for cx, cy in chips]


def _rs_final_copies(refs):
    x, y, c, _ = _place()
    return [(r.at[c], r.at[c], (x, y, 1 - c)) for r in refs]


def _rs_sibling_copies(refs):
    x, y, c, _ = _place()
    n = len(refs) // 2
    out = []
    for t in range(n):
        h = refs[t].shape[0] // 8
        out += [(refs[t].at[pl.ds((2 * s + 1 - c) * h, h)], refs[n + t].at[s], (x, y, 1 - c)) for s in range(4)]
    return out


def _allgather_forward(fulls, *, name):
    n = len(fulls)

    def body(*refs):
        outs = refs[n:2 * n]
        send_sems, recv_sems = refs[2 * n:]
        x, y, c, chips = _place()
        sent = []
        for t in range(n):
            for j, (cx, cy) in enumerate(chips):
                landed = _half_rows(outs[t], 2 * cx + cy, c)
                cp = _rcopy(landed, landed, send_sems.at[3 * t + j], recv_sems.at[3 * t + j], (x, y, 1 - c))
                cp.start()
                sent.append(cp)
        for t in range(n):
            for j, (cx, cy) in enumerate(chips):
                other = _half_rows(outs[t], 2 * cx + cy, 1 - c)
                _rcopy(other, other, send_sems.at[3 * t + j], recv_sems.at[3 * t + j], (x, y, 1 - c)).wait_recv()
        for cp in sent:
            cp.wait_send()

    return _pc(body, name=name, in_specs=[ANY] * n, out_specs=[ANY] * n,
               out_shape=[jax.ShapeDtypeStruct(f.shape, f.dtype) for f in fulls],
               input_output_aliases={t: t for t in range(n)},
               scratch_shapes=[pltpu.SemaphoreType.DMA((3 * n,)), pltpu.SemaphoreType.DMA((3 * n,))])(*fulls)


def _rs_sibling(grads, *, name):
    n = len(grads)

    def body(*refs):
        ins, outs = refs[:n], refs[n:2 * n]
        send_sems, recv_sems = refs[2 * n:]
        x, y, c, _ = _place()
        cps = []
        for t in range(n):
            h = ins[t].shape[0] // 8
            for s in range(4):
                cp = _rcopy(ins[t].at[pl.ds((2 * s + 1 - c) * h, h)], outs[t].at[s], send_sems.at[4 * t + s],
                            recv_sems.at[4 * t + s], (x, y, 1 - c))
                cp.start()
                cps.append(cp)
        for cp in cps:
            cp.wait()

    return _pc(body, name=name, in_specs=[ANY] * n, out_specs=[ANY] * n,
               out_shape=[jax.ShapeDtypeStruct((4, g.shape[0] // 8, g.shape[1]), g.dtype) for g in grads],
               scratch_shapes=[pltpu.SemaphoreType.DMA((4 * n,)), pltpu.SemaphoreType.DMA((4 * n,))])(*grads)


def _rs_chips(sends, *, name):
    n = len(sends)

    def body(*refs):
        s_refs, b_refs = refs[:n], refs[n:2 * n]
        send_sems, recv_sems = refs[2 * n:]
        x, y, c, chips = _place()
        cps = []
        for t in range(n):
            for j, (cx, cy) in enumerate(chips):
                cp = _rcopy(s_refs[t].at[2 * cx + cy], b_refs[t].at[j], send_sems.at[3 * t + j], recv_sems.at[3 * t + j],
                            (cx, cy, c))
                cp.start()
                cps.append(cp)
        for cp in cps:
            cp.wait()

    return _pc(body, name=name, in_specs=[ANY] * n, out_specs=[ANY] * n,
               out_shape=[jax.ShapeDtypeStruct((3,) + s.shape[1:], s.dtype) for s in sends],
               scratch_shapes=[pltpu.SemaphoreType.DMA((3 * n,)), pltpu.SemaphoreType.DMA((3 * n,))])(*sends)


def _rs_final(fulls, *, name):
    n = len(fulls)

    def body(*refs):
        outs = refs[n:2 * n]
        send_sems, recv_sems = refs[2 * n:]
        x, y, c, _ = _place()
        cps = []
        for t in range(n):
            cp = _rcopy(outs[t].at[c], outs[t].at[c], send_sems.at[t], recv_sems.at[t], (x, y, 1 - c))
            cp.start()
            cps.append(cp)
        for cp in cps:
            cp.wait()

    return _pc(body, name=name, in_specs=[ANY] * n, out_specs=[ANY] * n,
               out_shape=[jax.ShapeDtypeStruct(f.shape, f.dtype) for f in fulls],
               input_output_aliases={t: t for t in range(n)},
               scratch_shapes=[pltpu.SemaphoreType.DMA((n,)), pltpu.SemaphoreType.DMA((n,))])(*fulls)


def _add_halves(g, a, send_dtype, *, name):
    _, h, cols = a.shape
    th = _row_tile(h, cols)
    g4 = g.reshape(4, 2, h, cols)
    idx = jnp.stack([lax.axis_index("c"), 2 * lax.axis_index("x") + lax.axis_index("y")]).astype(jnp.int32)

    def shard(k, ir):
        return (ir[1] + 1 + k) % 4

    def body(idx_ref, g_ref, a_ref, p_ref, s_ref):
        v = g_ref[...].astype(F32) + a_ref[...].astype(F32)
        s_ref[...] = v.astype(send_dtype)

        @pl.when(pl.program_id(1) == 3)
        def _():
            p_ref[...] = v

    return _pc(body, name=name,
               grid_spec=pltpu.PrefetchScalarGridSpec(
                   num_scalar_prefetch=1, grid=(h // th, 4),
                   in_specs=[pl.BlockSpec((None, None, th, cols), lambda i, k, ir: (shard(k, ir), ir[0], i, 0)),
                             pl.BlockSpec((None, th, cols), lambda i, k, ir: (shard(k, ir), i, 0))],
                   out_specs=[pl.BlockSpec((th, cols), lambda i, k, ir: (i, 0)),
                              pl.BlockSpec((None, th, cols), lambda i, k, ir: (shard(k, ir), i, 0))]),
               out_shape=[jax.ShapeDtypeStruct((h, cols), F32), jax.ShapeDtypeStruct(a.shape, send_dtype)],
               compiler_params=pltpu.CompilerParams(dimension_semantics=("arbitrary", "arbitrary")))(idx, g4, a)


def _add_chips(p, b, *, name, order=None):
    h, cols = p.shape
    th = _row_tile(h, cols)
    idx = lax.axis_index("c").astype(jnp.int32).reshape(1)
    extra = [] if order is None else [order]

    def body(idx_ref, p_ref, b_ref, *rest):
        r_ref = rest[-1]
        r_ref[...] = ((p_ref[...] + b_ref[0].astype(F32)) + b_ref[1].astype(F32)) + b_ref[2].astype(F32)

    return _pc(body, name=name,
               grid_spec=pltpu.PrefetchScalarGridSpec(
                   num_scalar_prefetch=1, grid=(h // th,),
                   in_specs=[pl.BlockSpec((th, cols), lambda i, ir: (i, 0)),
                             pl.BlockSpec((3, th, cols), lambda i, ir: (0, i, 0))] + [ANY] * len(extra),
                   out_specs=pl.BlockSpec((None, th, cols), lambda i, ir: (ir[0], i, 0))),
               out_shape=jax.ShapeDtypeStruct((2, h, cols), F32),
               compiler_params=pltpu.CompilerParams(dimension_semantics=("arbitrary",)))(idx, p, b, *extra)


def _add_halves_all(grads, recv, send_dtypes, tag):
    parts, sends = [], []
    for t, (g, a) in enumerate(zip(grads, recv)):
        p, s = _add_halves(g, a, send_dtypes[t], name=f"rs_add_halves_{tag}{t}")
        parts.append(p)
        sends.append(s)
    return parts, sends


def _rs_finish(parts, others, tag, order=None):
    halves = [_add_chips(p, b, order=order, name=f"rs_add_chips_{tag}{t}") for t, (p, b) in enumerate(zip(parts, others))]
    full = _rs_final(halves, name=f"rs_final_{tag}")
    return [f.reshape(-1, f.shape[-1]) for f in full]


def _s5_discretize(lam_re, lam_im, log_dt, b_re, b_im):
    lam = lax.complex(lam_re, lam_im)
    dt = jnp.exp(log_dt)[:, None]
    lam_bar = jnp.exp(lam * dt)
    b_bar = ((lam_bar - 1.0) / lam)[..., None] * lax.complex(b_re, b_im)
    return jnp.real(lam_bar), jnp.imag(lam_bar), jnp.real(b_bar), jnp.imag(b_bar)


def _lanes_from_gp(re, im, cfg):
    v = jnp.stack([re, im]).reshape(2, cfg.NB, GROUPS_PER_BLOCK, SSM_STATE)
    return jnp.transpose(v, (1, 0, 2, 3)).reshape(1, cfg.NL)


def _gp_from_lanes(v, cfg):
    v = jnp.transpose(v.reshape(cfg.NB, 2, GROUPS_PER_BLOCK, SSM_STATE), (1, 0, 2, 3)).reshape(2, cfg.G, SSM_STATE)
    return v[0], v[1]


def _bb_band(bb_re, bb_im, cfg):
    eye = jnp.eye(GROUPS_PER_BLOCK, dtype=F32)
    bb = jnp.stack([bb_re, bb_im]).reshape(2, cfg.NB, GROUPS_PER_BLOCK, SSM_STATE, SSM_GROUP)
    return jnp.einsum('rjgpc,gh->jgcrhp', bb, eye).reshape(cfg.DS, 2 * GROUPS_PER_BLOCK * SSM_STATE)


def _bb_from_band(m, cfg):
    eye = jnp.eye(GROUPS_PER_BLOCK, dtype=F32)
    m = m.reshape(cfg.NB, GROUPS_PER_BLOCK, SSM_GROUP, 2, GROUPS_PER_BLOCK, SSM_STATE)
    v = jnp.einsum('jgcrhp,gh->rjgpc', m, eye).reshape(2, cfg.G, SSM_STATE, SSM_GROUP)
    return v[0], v[1]


def _cc_band(c_re, c_im, cfg):
    eye = jnp.eye(GROUPS_PER_BLOCK, dtype=F32)
    cc = jnp.stack([c_re, -c_im]).reshape(2, cfg.NB, GROUPS_PER_BLOCK, SSM_GROUP, SSM_STATE)
    return jnp.einsum('rjgcp,gh->jrhpgc', cc, eye).reshape(cfg.NL, GROUPS_PER_BLOCK * SSM_GROUP)


def _cc_from_band(m, cfg):
    eye = jnp.eye(GROUPS_PER_BLOCK, dtype=F32)
    m = m.reshape(cfg.NB, 2, GROUPS_PER_BLOCK, SSM_STATE, GROUPS_PER_BLOCK, SSM_GROUP)
    v = jnp.einsum('jrhpgc,gh->rjgcp', m, eye).reshape(2, cfg.G, SSM_GROUP, SSM_STATE)
    return v[0], -v[1]


PACK_COLS = 512
PACK_ROW_ALIGN = 64


def _pack(arrs):
    flat = jnp.concatenate([a.reshape(-1).astype(F32) for a in arrs])
    unit = PACK_COLS * PACK_ROW_ALIGN
    total = -(-flat.shape[0] // unit) * unit
    return jnp.pad(flat, (0, total - flat.shape[0])).reshape(-1, PACK_COLS)


def _unpack(p, shapes):
    flat = p.reshape(-1)
    out, off = [], 0
    for shp in shapes:
        size = math.prod(shp)
        out.append(flat[off:off + size].reshape(shp))
        off += size
    return out


def _adamw(w, g, m, v, *, name, emit_grad=False):
    c1 = 1.0 / (1.0 - ADAM_B1 ** ADAM_STEP)
    c2 = 1.0 / (1.0 - ADAM_B2 ** ADAM_STEP)

    if w.ndim == 2:
        outs = _adamw(w[None], g[None], m[None], v[None], name=name, emit_grad=emit_grad)
        return [o[0] for o in outs]
    lead, rows, cols = w.shape
    tc = _tile(cols, 512)
    tm = _tile(rows, max(8, ADAMW_BLOCK_BYTES // (4 * tc)), 8)
    n_out = 4 if emit_grad else 3

    def body(w_ref, g_ref, m_ref, v_ref, *o_refs):
        gv = g_ref[...]
        mn = ADAM_B1 * m_ref[...] + (1.0 - ADAM_B1) * gv
        vn = ADAM_B2 * v_ref[...] + (1.0 - ADAM_B2) * (gv * gv)
        delta = -ADAM_LR * ((mn * c1) / (jnp.sqrt(vn * c2) + ADAM_EPS) + ADAM_WD * w_ref[...])
        for o_ref, val in zip(o_refs, ((gv, delta, mn, vn) if emit_grad else (delta, mn, vn))):
            o_ref[...] = val

    blk = pl.BlockSpec((None, tm, tc), lambda n, i, j: (n, i, j))
    return _pc(body, name=name, grid=(lead, rows // tm, cols // tc), in_specs=[blk] * 4, out_specs=[blk] * n_out,
               out_shape=[jax.ShapeDtypeStruct((lead, rows, cols), F32)] * n_out,
               compiler_params=pltpu.CompilerParams(dimension_semantics=("parallel", "parallel", "parallel")))(w, g, m, v)


def _to_comm_layout(name, w, cfg):
    w = w[0]
    if name == 'w_in':
        return jnp.pad(w, ((0, 0), (0, cfg.DINP - cfg.DIN)))
    if name == 'w_q_b':
        hs = w.shape[1] // (QK_NOPE + QK_ROPE)
        wt = w.T.reshape(hs, QK_NOPE + QK_ROPE, cfg.QL)
        return jnp.pad(wt, ((0, 0), (0, HEAD_SLOT - QK_NOPE - QK_ROPE), (0, 0))).reshape(hs * HEAD_SLOT, cfg.QL)
    if name == 'w_kv_b':
        return w.T
    if name == 'w_up':
        return w.T.reshape(2, cfg.F // 4, cfg.D)
    return w


def _from_comm_layout(name, g, cfg):
    if name == 'w_in':
        g = g[:, :cfg.DIN]
    elif name == 'w_q_b':
        hs = g.shape[0] // HEAD_SLOT
        g = g.reshape(hs, HEAD_SLOT, cfg.QL)[:, :QK_NOPE + QK_ROPE].reshape(hs * (QK_NOPE + QK_ROPE), cfg.QL).T
    elif name == 'w_kv_b':
        g = g.T
    elif name == 'w_up':
        g = g.reshape(2, cfg.FQ, cfg.D)[:, :cfg.F // 4].reshape(cfg.F // 2, cfg.D).T
    elif name == 'w_down':
        g = g[:cfg.F // 4]
    return g[None]


def _ff_pad(v, cfg):
    k = v.shape[0]
    return jnp.pad(v.reshape(k, 4, cfg.F // 4), ((0, 0), (0, 0), (0, cfg.FQ - cfg.F // 4))).reshape(k, cfg.FP)


def _ff_unpad(v, cfg):
    k = v.shape[0]
    return v.reshape(k, 4, cfg.FQ)[:, :, :cfg.F // 4].reshape(k, cfg.F)


def _step(cfg, w, m, v, x, loss_target):
    lp, d, ds, nl = cfg.LP, cfg.D, cfg.DS, cfg.NL
    xi, yi = lax.axis_index("x"), lax.axis_index("y")
    me = 2 * xi + yi

    def place(n, order=None):
        rows_to = cfg.FQ if n in ('w_up', 'w_down') else None
        return _place_shard(_to_comm_layout(n, w[n], cfg), BF16, order=order, rows_to=rows_to, name=f"place_{n}")

    first = [place('w_in'), _place_shard(w['meta_tokens'], F32, name="place_meta")]
    f_send, f_recv, f_flying, f_token = _split_start(first, _allgather_ici_copies, 6, name="allgather_first_start")
    conv_w_shard = jnp.pad(w['conv_w'][0], ((0, ROW_ALIGN - 3), (0, cfg.FQ - cfg.F // 4)))
    mid = [place(n, f_token) for n in BIG[1:5]] + [_place_shard(conv_w_shard, F32, order=f_token, name="place_conv_w")]
    mid_send, mid_recv, mid_flying, mid_token = _split_start(mid, _allgather_ici_copies, 3 * len(mid), before=f_token,
                                                             name="allgather_mid_start")
    ffn = [place(n, mid_token) for n in BIG[5:7]]
    ffn_send, ffn_recv, ffn_flying, ffn_token = _split_start(ffn, _allgather_ici_copies, 6, before=mid_token,
                                                             name="allgather_ffn_start")
    conv_b = _ff_pad(w['conv_b'], cfg)

    pos = (jnp.arange(lp, dtype=jnp.int32) - PAD).astype(F32)
    inv_freq = 1.0 / (ROPE_BASE ** (jnp.arange(0, QK_ROPE, 2, dtype=F32) / QK_ROPE))
    ang = pos[:, None] * inv_freq[None, :]
    zpad = jnp.zeros((lp, LANE - QK_ROPE), F32)
    cos_t = jnp.concatenate([jnp.cos(ang), jnp.cos(ang), zpad], axis=1)
    sin_t = jnp.concatenate([jnp.sin(ang), jnp.sin(ang), zpad], axis=1)

    s5_in = (w['lam_re'][0], w['lam_im'][0], w['log_dt'][0], w['b_re'][0], w['b_im'][0])
    (a_re, a_im, bb_re, bb_im), s5_vjp = jax.vjp(_s5_discretize, *s5_in)
    lam_dt = lax.complex(s5_in[0], s5_in[1]) * jnp.exp(s5_in[2])[:, None]
    a_pow = jnp.exp(jnp.arange(1, 9, dtype=F32)[:, None, None] * lam_dt[None])
    r8 = jnp.arange(8)
    step_f = jnp.stack([jnp.where((r8 >= k)[:, None, None], a_pow[k - 1][None], 0.0) for k in (1, 2, 4)]).reshape(24, cfg.G, -1)
    step_b = jnp.stack([jnp.where((r8 < 8 - k)[:, None, None], a_pow[k - 1][None], 0.0) for k in (1, 2, 4)]).reshape(24, cfg.G, -1)
    rows_f = jnp.concatenate([a_pow, step_f])
    rows_b = jnp.conj(jnp.concatenate([a_pow[::-1], step_b]))

    def lane_rows(t):
        v = jnp.stack([jnp.real(t), jnp.imag(t)], axis=1).reshape(t.shape[0], 2, cfg.NB, GROUPS_PER_BLOCK, SSM_STATE)
        return jnp.transpose(v, (0, 2, 1, 3, 4)).reshape(t.shape[0], cfg.NL)

    pw_fwd, pw_bwd = lane_rows(rows_f), lane_rows(rows_b)
    bb_band = _bb_band(bb_re, bb_im, cfg).astype(BF16)
    cc_band = _cc_band(w['c_re'][0], w['c_im'][0], cfg).astype(BF16)
    d_skip, b_glu = w['d_skip'], w['b_glu']

    f_landed = _split_wait(f_send, f_recv, f_flying, _allgather_ici_copies, [ffn_token, cos_t, sin_t, pw_fwd, pw_bwd, bb_band, cc_band],
                           name="allgather_first_wait")
    w_in, meta_full = _allgather_forward(f_landed, name="allgather_first_forward")
    meta = jnp.transpose(meta_full.reshape(4, N_META, d // 4), (1, 0, 2)).reshape(N_META, d)
    mix_norm = w['mix_norm'] + ffn_token[0:1, 0:1]

    h0 = jnp.concatenate([jnp.zeros((PAD, d), F32), meta, x[0]], axis=0)
    xn = _rms_fwd(h0, mix_norm, name="rms_mix")
    z = _mm(xn, w_in, name="mm_in", tn=_tile(cfg.DINP, 640))
    u = (z, ds, 0)
    q_a = (z, cfg.QL, ds // cfg.QL)
    kv_a = (z, cfg.KVL, (ds + cfg.QL) // cfg.KVL)
    k_pe = (z, LANE, (ds + cfg.QL + cfg.KVL) // LANE)

    hs, yc = _s5_fwd(z, bb_band, cc_band, pw_fwd, cfg, name="s5_fwd")

    def s5_y(ycv, uv, dk):
        return ycv + dk * uv

    gl = _ew(lambda rid, ycv, uv, dk: jax.nn.gelu(s5_y(ycv, uv, dk)), [yc, u], [d_skip], [(ds, BF16)], name="s5_gelu")[0]
    mid_landed = _split_wait(mid_send, mid_recv, mid_flying, _allgather_ici_copies, gl, name="allgather_mid_wait")
    w_glu, w_qt, w_kvt, w_out, conv_full = _allgather_forward(mid_landed, name="allgather_mid_forward")
    conv_w = jnp.transpose(conv_full.reshape(4, ROW_ALIGN, cfg.FQ)[:, :3], (1, 0, 2)).reshape(3, cfg.FP)
    tg = _mm(gl, w_glu, name="mm_glu")
    ya = _ew(lambda rid, ycv, uv, tv, dk, bg: jax.nn.gelu(s5_y(ycv, uv, dk)) * jax.nn.sigmoid(tv + bg),
             [yc, u, tg], [d_skip, b_glu], [(ds, F32)], name="s5_glu")[0]

    qn = _rms_fwd(q_a, w['q_a_norm'], name="rms_q")
    kvn = _rms_fwd(kv_a, w['kv_a_norm'], name="rms_kv")
    q_raw = _mm(qn, w_qt, tb=True, name="mm_q")
    qx = _ew(_rope_heads(_rope, cfg.H), [q_raw, cos_t, sin_t], [], [(cfg.H * HEAD_SLOT, BF16)], name="rope_q")[0]
    kv = _mm(kvn, w_kvt, tb=True, out_dtype=BF16, name="mm_kv")
    kr = _ew(lambda rid, kp, cs, sn: _rope(kp, cs, sn), [k_pe, cos_t, sin_t], [], [(LANE, BF16)], name="rope_k")[0]
    o, lse = _attn_fwd(qx, kv, kr, cfg, name="attn_fwd")

    def norm2(rid, yav, ov, gs, ga):
        return jnp.concatenate([_rms_parts(yav, gs)[0] * gs, _rms_parts(ov, ga)[0] * ga], axis=1)

    ffn_landed = _split_wait(ffn_send, ffn_recv, ffn_flying, _allgather_ici_copies, o, name="allgather_ffn_wait")
    ff_send, ff_recv, ff_flying, ff_token = _split_start(ffn_landed, _allgather_forward_copies, 6,
                                                         name="allgather_ffn_forward_start")
    yn = _ew(norm2, [ya, o], [w['out_norm_ssm'] + ff_token[0:1, 0:1], w['out_norm_attn']], [(cfg.DMIX, BF16)],
             name="rms_out")[0]
    h1 = _mm(yn, w_out, res=h0, name="mm_out")
    xn2 = _rms_fwd(h1, w['ffn_norm'], name="rms_ffn")
    w_upt, w_down = _split_wait(ff_send, ff_recv, ff_flying, _allgather_forward_copies, xn2,
                                name="allgather_ffn_forward_wait")
    up, act = _ffn_up(xn2, w_upt, conv_w, conv_b, name="ffn_up")
    h2 = _mm(act, w_down, res=h1, tm=_tile(lp, 544, ROW_ALIGN), name="mm_down")

    g_final = w['final_norm'].reshape(1, d)

    def head(rid, hv, tv, gv):
        xhat, r = _rms_parts(hv, gv)
        valid = rid >= PAD + N_META
        diff = jnp.where(valid, xhat * gv - tv, 0.0)
        dout = diff * (1.0 / d)
        dxhat = dout * gv
        dx = r * (dxhat - xhat * jnp.mean(dxhat * xhat, axis=-1, keepdims=True))
        return dx, dx, dout * xhat, 0.5 * diff * dout

    dh2, dh2_b, dg_final, loss_cols = _ew(head, [h2, (loss_target[0], d, 0, SKIP)], [g_final], [(d, F32), (d, BF16)], [d, d],
                                          tm=PAD + N_META, name="loss_head")
    loss = lax.psum(jnp.sum(loss_cols), ("x", "y", "c"))

    dw_down = _mm(act, dh2_b, ta=True, tn=d, tm=512, out_dtype=BF16, name="mm_dw_down")

    def sibling_start(g, tag):
        land = lax.empty((4, g.shape[0] // 8, g.shape[1]), g.dtype)
        return _split_start([g, land], _rs_sibling_copies, 4, name=f"rs_sibling_{tag}_start")

    dn_send, dn_recv, dn_flying, dn_token = sibling_start(dw_down, "down")
    dup, dconv_w, dconv_b = _ffn_dact(dh2_b, w_down, up, conv_w, conv_b + dn_token[0:1, 0:1], name="ffn_dact")
    tk_up, tm_up = _tile(cfg.FP, 1408), _tile(cfg.FP, 512)
    dw_upt = _mm(dup, xn2, ta=True, dims=(2 * cfg.FP, d, lp), tn=d, tm=tm_up, a_lead=True, out_dtype=BF16, name="mm_dw_up",
                 a_idx=lambda i, j, k: (i // (cfg.FP // tm_up), 0, i % (cfg.FP // tm_up)))
    up_send, up_recv, up_flying, up_token = sibling_start(dw_upt, "up")
    dxn2 = _mm(dup, w_upt, dims=(lp, d, 2 * cfg.FP), tk=tk_up, tn=1024, a_lead=True, name="mm_dxn2",
               a_idx=lambda i, j, k: (k // (cfg.FP // tk_up), i, k % (cfg.FP // tk_up)))
    dh1, dh1_b, dg_ffn = _rms_bwd(h1, w['ffn_norm'] + up_token[0:1, 0:1], dxn2, res=dh2, mask=True, with_bf16=True,
                                  name="rms_ffn_bwd")

    dyn = _mm(dh1_b, w_out, tb=True, name="mm_dyn")
    dw_out = _mm(yn, dh1_b, ta=True, tn=d, tm=512, name="mm_dw_out")
    up_done = _split_wait(up_send, up_recv, up_flying, _rs_sibling_copies, dw_out, name="rs_sibling_up_wait")
    dn_done = _split_wait(dn_send, dn_recv, dn_flying, _rs_sibling_copies, dw_out, name="rs_sibling_down_wait")
    early_parts, early_sends = _add_halves_all([up_done[0], dn_done[0]], [up_done[1], dn_done[1]], [BF16] * 2, "early")
    chip_lands = [lax.empty((3,) + s.shape[1:], s.dtype) for s in early_sends]
    ch_send, ch_recv, ch_flying, ch_token = _split_start(early_sends + chip_lands, _rs_chips_copies, 6,
                                                         name="rs_chips_early_start")
    dya, dg_ssm = _rms_bwd(ya, w['out_norm_ssm'] + ch_token[0:1, 0:1], (dyn, ds, 0), name="rms_ssm_bwd")
    do, dg_attn = _rms_bwd(o, w['out_norm_attn'], (dyn, cfg.DATTN, ds // cfg.DATTN), name="rms_attn_bwd")

    dqx, dkv, dkr = _attn_bwd(qx, kv, kr, o, lse, do, cfg, name="attn_bwd")
    dq_raw = _ew(_rope_heads(_unrope, cfg.H), [dqx, cos_t, sin_t], [], [(cfg.H * HEAD_SLOT, BF16)], name="unrope_q")[0]
    dk_pe = _ew(lambda rid, dk, cs, sn: _unrope(dk, cs, sn), [dkr, cos_t, sin_t], [], [(LANE, F32)], name="unrope_k")[0]
    dqn = _mm(dq_raw, w_qt, name="mm_dqn")
    dw_qt = _mm(dq_raw, qn, ta=True, tm=512, name="mm_dw_q")
    dkvn = _mm(dkv, w_kvt, name="mm_dkvn")
    dw_kvt = _mm(dkv, kvn, ta=True, tm=512, name="mm_dw_kv")
    dq_a, dg_q = _rms_bwd(q_a, w['q_a_norm'], dqn, name="rms_q_bwd")
    dkv_a, dg_kv = _rms_bwd(kv_a, w['kv_a_norm'], dkvn, name="rms_kv_bwd")

    def glu_bwd(rid, ycv, uv, tv, dyav, dk, bg):
        gelu = jax.nn.gelu(s5_y(ycv, uv, dk))
        sg = jax.nn.sigmoid(tv + bg)
        dt = dyav * gelu * sg * (1.0 - sg)
        return dt, dyav * sg, dt

    dt_b, dgl1, db_glu = _ew(glu_bwd, [yc, u, tg, dya], [d_skip, b_glu], [(ds, BF16), (ds, F32)], [ds], name="s5_glu_bwd")
    dgl = _mm(dt_b, w_glu, tb=True, res=dgl1, name="mm_dgl")
    dw_glu = _mm(gl, dt_b, ta=True, tm=512, name="mm_dw_glu")

    def gelu_bwd(rid, ycv, uv, dglv, dk):
        _, vjp = jax.vjp(jax.nn.gelu, s5_y(ycv, uv, dk))
        dy = vjp(dglv)[0]
        return dy, dy * dk, dy * uv

    mid_grads = [dw_out, dw_glu, dw_qt, dw_kvt]
    mid_lands = [lax.empty((4, g.shape[0] // 8, g.shape[1]), g.dtype) for g in mid_grads]
    ms_send, ms_recv, ms_flying, ms_token = _split_start(mid_grads + mid_lands, _rs_sibling_copies, 4 * len(mid_grads),
                                                         name="rs_sibling_mid_start")
    dy_b, du_skip, dd_skip = _ew(gelu_bwd, [yc, u, dgl], [d_skip + ms_token[0:1, 0:1]], [(ds, BF16), (ds, F32)], [ds],
                                 name="s5_gelu_bwd")
    ms_done = _split_wait(ms_send, ms_recv, ms_flying, _rs_sibling_copies, dy_b, name="rs_sibling_mid_wait")
    mid_parts, mid_sends = _add_halves_all(ms_done[:4], ms_done[4:], [BF16] * 4, "mid")
    mid_chip_lands = [lax.empty((3,) + s.shape[1:], s.dtype) for s in mid_sends]
    mc_send, mc_recv, mc_flying, mc_token = _split_start(mid_sends + mid_chip_lands, _rs_chips_copies, 3 * len(mid_sends),
                                                         name="rs_chips_mid_start")
    du, dbb_band, dcc_band, da_l = _s5_bwd(dy_b, hs, z, bb_band, cc_band, pw_bwd + mc_token[0:1, 0:1], du_skip, cfg,
                                           name="s5_bwd")

    dz = jnp.concatenate([du, dq_a, dkv_a, dk_pe], axis=1).astype(BF16)
    dxn = _mm(dz, w_in, tb=True, name="mm_dxn")
    dw_in = _mm(xn, dz, ta=True, tm=512, tn=_tile(cfg.DINP, 1024), name="mm_dw_in")
    def mix_bwd(rid, xv, dyv, resv, gv):
        dx, dg = _rms_bwd_block(xv, gv, dyv)
        dx = dx + resv
        return dx, dx, dg

    grad_x, dh0_head, dg_mix = _ew(mix_bwd, [h0, dxn, dh1], [mix_norm], [(d, F32, SKIP), (d, F32, FIRST)], [d],
                                   tm=PAD + N_META, name="rms_mix_bwd")
    grad_x = grad_x[None]

    da_re, da_im = _gp_from_lanes(da_l, cfg)
    dbb_re, dbb_im = _bb_from_band(dbb_band, cfg)
    dlam_re, dlam_im, dlog_dt, db_re, db_im = s5_vjp((da_re, da_im, dbb_re, dbb_im))
    dc_re, dc_im = _cc_from_band(dcc_band, cfg)
    local_small = {
        'meta_tokens': dh0_head[PAD:], 'mix_norm': dg_mix, 'lam_re': dlam_re, 'lam_im': dlam_im, 'log_dt': dlog_dt,
        'b_re': db_re, 'b_im': db_im, 'c_re': dc_re, 'c_im': dc_im, 'd_skip': dd_skip, 'b_glu': db_glu, 'q_a_norm': dg_q,
        'kv_a_norm': dg_kv, 'out_norm_ssm': dg_ssm, 'out_norm_attn': dg_attn, 'ffn_norm': dg_ffn,
        'conv_w': _ff_unpad(dconv_w, cfg), 'conv_b': _ff_unpad(dconv_b, cfg), 'final_norm': dg_final,
    }
    small_shapes = [local_small[n].shape for n in SMALL]

    small_pack = _pack([local_small[n] for n in SMALL])
    ch_done = _split_wait(ch_send, ch_recv, ch_flying, _rs_chips_copies, small_pack, name="rs_chips_early_wait")
    early_halves = [_add_chips(p, b, name=f"rs_add_chips_early{t}") for t, (p, b) in enumerate(zip(early_parts, ch_done[2:]))]
    fe_send, fe_recv, fe_flying, fe_token = _split_start(early_halves, _rs_final_copies, 2, name="rs_final_early_start")
    end_local = [dw_in, small_pack + fe_token[0:1, 0:1]]
    end_recv = _rs_sibling(end_local, name="rs_sibling_end")
    end_parts, end_sends = _add_halves_all(end_local, end_recv, [BF16, F32], "end")
    end_lands = [lax.empty((3,) + s.shape[1:], s.dtype) for s in end_sends]
    ec_send, ec_recv, ec_flying, ec_token = _split_start(end_sends + end_lands, _rs_chips_copies, 3 * len(end_sends),
                                                         name="rs_chips_end_start")
    fe_done = _split_wait(fe_send, fe_recv, fe_flying, _rs_final_copies, ec_token, name="rs_final_early_wait")
    red_up, red_down = [f.reshape(-1, f.shape[-1]) for f in fe_done]

    delta, new_m, new_v, grads = {}, {}, {}, {}
    padded_rows = ('w_down',)

    def adamw_big(n, red):
        shp = w[n].shape
        w2, m2, v2 = [t.reshape(shp[-2], shp[-1]) for t in (w[n], m[n], v[n])]
        if n in padded_rows:
            g2, dl, mn, vn = _adamw(w2, red, m2, v2, emit_grad=True, name=f"adamw_{n}")
            grads[n] = g2.reshape(shp)
        else:
            grads[n] = _from_comm_layout(n, red, cfg)
            dl, mn, vn = _adamw(w2, grads[n].reshape(shp[-2], shp[-1]), m2, v2, name=f"adamw_{n}")
        delta[n], new_m[n], new_v[n] = dl.reshape(shp), mn.reshape(shp), vn.reshape(shp)

    def adamw_up(red):
        q = cfg.F // 4
        wt, mt, vt = [jnp.transpose(t[0]).reshape(2, q, d) for t in (w['w_up'], m['w_up'], v['w_up'])]
        outs = _adamw(wt, red.reshape(2, cfg.FQ, d), mt, vt, emit_grad=True, name="adamw_w_up")
        grads['w_up'], delta['w_up'], new_m['w_up'], new_v['w_up'] = [jnp.transpose(t.reshape(2 * q, d))[None] for t in outs]

    adamw_up(red_up)
    adamw_big('w_down', red_down)
    mc_done = _split_wait(mc_send, mc_recv, mc_flying, _rs_chips_copies, delta['w_down'], name="rs_chips_mid_wait")
    ec_done = _split_wait(ec_send, ec_recv, ec_flying, _rs_chips_copies, mc_done[0], name="rs_chips_end_wait")
    red = _rs_finish(mid_parts + end_parts, list(mc_done[len(mid_sends):]) + list(ec_done[len(end_sends):]), "rest")
    small_full = _allgather([_place_shard(red[5], F32, name="place_small")], name="allgather_small")[0]
    small_sum = dict(zip(SMALL, _unpack(small_full, small_shapes)))
    for n, r in zip(['w_out', 'w_glu', 'w_q_b', 'w_kv_b'], red[:4]):
        adamw_big(n, r)
    in_t = [jnp.transpose(t[0]) for t in (w['w_in'], m['w_in'], v['w_in'])]
    outs = _adamw(in_t[0], jnp.transpose(red[4][:, :cfg.DIN]), in_t[1], in_t[2], emit_grad=True, name="adamw_w_in")
    grads['w_in'], delta['w_in'], new_m['w_in'], new_v['w_in'] = [jnp.transpose(t)[None] for t in outs]

    for n in SMALL:
        g = small_sum[n]
        if n == 'meta_tokens':
            g = lax.dynamic_slice_in_dim(g, me * (d // 4), d // 4, axis=1)
        elif n == 'conv_w':
            g = lax.dynamic_slice_in_dim(g, me * (cfg.F // 4), cfg.F // 4, axis=1)[None]
        else:
            g = g.reshape(w[n].shape)
        grads[n] = g

    shapes = [w[n].shape for n in SMALL]
    packs = [_pack([src[n] for n in SMALL]) for src in (w, grads, m, v)]
    for dst, p in zip((delta, new_m, new_v), _adamw(*packs, name="adamw_small")):
        dst.update(zip(SMALL, _unpack(p, shapes)))

    return (loss, grad_x, *[grads[n] for n in WEIGHTS], *[delta[n] for n in WEIGHTS],
            *[new_m[n] for n in WEIGHTS], *[new_v[n] for n in WEIGHTS])


def kernel(x, meta_tokens, mix_norm, w_in, lam_re, lam_im, log_dt, b_re, b_im, c_re, c_im, d_skip, w_glu, b_glu, q_a_norm, w_q_b, kv_a_norm, w_kv_b, out_norm_ssm, out_norm_attn, w_out, ffn_norm, w_up, conv_w, conv_b, w_down, final_norm, loss_target, m_meta_tokens, m_mix_norm, m_w_in, m_lam_re, m_lam_im, m_log_dt, m_b_re, m_b_im, m_c_re, m_c_im, m_d_skip, m_w_glu, m_b_glu, m_q_a_norm, m_w_q_b, m_kv_a_norm, m_w_kv_b, m_out_norm_ssm, m_out_norm_attn, m_w_out, m_ffn_norm, m_w_up, m_conv_w, m_conv_b, m_w_down, m_final_norm, v_meta_tokens, v_mix_norm, v_w_in, v_lam_re, v_lam_im, v_log_dt, v_b_re, v_b_im, v_c_re, v_c_im, v_d_skip, v_w_glu, v_b_glu, v_q_a_norm, v_w_q_b, v_kv_a_norm, v_w_kv_b, v_out_norm_ssm, v_out_norm_attn, v_w_out, v_ffn_norm, v_w_up, v_conv_w, v_conv_b, v_w_down, v_final_norm):
    args = dict(locals())
    w = {n: args[n] for n in WEIGHTS}
    m = {n: args["m_" + n] for n in WEIGHTS}
    v = {n: args["v_" + n] for n in WEIGHTS}
    return _step(PROD, w, m, v, x, loss_target)
```

```python
import functools
import math
from typing import NamedTuple

import jax
import jax.numpy as jnp
from jax import lax
from jax.experimental import pallas as pl
from jax.experimental.pallas import tpu as pltpu

F32, BF16 = jnp.float32, jnp.bfloat16
MESH = pl.DeviceIdType.MESH
LANE = 128
ROW_ALIGN = 16
N_META = 16
PAD = 112
CHUNK = 64
SSM_GROUP = 16
SSM_STATE = 64
GROUPS_PER_BLOCK = 8
QK_NOPE, QK_ROPE, V_HEAD = 128, 64, 128
HEAD_SLOT = 256
ROPE_BASE = 10000.0
EPS = 1e-6
ADAM_LR, ADAM_B1, ADAM_B2, ADAM_EPS, ADAM_WD, ADAM_STEP = 0.001, 0.9, 0.999, 1e-08, 0.01, 10
DT_F32_BLOCK_BYTES = 9 << 18
ADAMW_BLOCK_BYTES = 3 << 19
PLACE_BLOCK_BYTES = 6 << 20
SKIP, FIRST = "skip", "first"


class Cfg(NamedTuple):
    D: int
    S: int
    DS: int
    H: int
    QL: int
    KVL: int
    F: int

    @property
    def LP(self):
        return PAD + N_META + self.S

    @property
    def G(self):
        return self.DS // SSM_GROUP

    @property
    def NB(self):
        return self.G // GROUPS_PER_BLOCK

    @property
    def NL(self):
        return 2 * self.G * SSM_STATE

    @property
    def DATTN(self):
        return self.H * V_HEAD

    @property
    def DMIX(self):
        return self.DS + self.DATTN

    @property
    def DIN(self):
        return self.DS + self.QL + self.KVL + QK_ROPE

    @property
    def DINP(self):
        return self.DS + self.QL + self.KVL + LANE

    @property
    def FQ(self):
        return -(-(self.F // 4) // LANE) * LANE

    @property
    def FP(self):
        return 4 * self.FQ


PROD = Cfg(D=2048, S=2048, DS=1024, H=8, QL=512, KVL=256, F=5504)

WEIGHTS = ['meta_tokens', 'mix_norm', 'w_in', 'lam_re', 'lam_im', 'log_dt', 'b_re', 'b_im', 'c_re', 'c_im', 'd_skip',
           'w_glu', 'b_glu', 'q_a_norm', 'w_q_b', 'kv_a_norm', 'w_kv_b', 'out_norm_ssm', 'out_norm_attn', 'w_out',
           'ffn_norm', 'w_up', 'conv_w', 'conv_b', 'w_down', 'final_norm']
BIG = ['w_in', 'w_glu', 'w_q_b', 'w_kv_b', 'w_out', 'w_up', 'w_down']
SMALL = [n for n in WEIGHTS if n not in BIG]


def _pc(body, **kw):
    return pl.pallas_call(body, **kw)


def _tile(n, target, align=LANE):
    best = None
    d = align
    while d <= min(n, target):
        if n % d == 0:
            best = d
        d += align
    return best if best is not None else n


def _row_tile(rows, cols):
    return _tile(rows, max(ROW_ALIGN, DT_F32_BLOCK_BYTES // (4 * cols)), ROW_ALIGN)


def _mm(a, b, *, name, ta=False, tb=False, tm=None, tn=512, tk=None, out_dtype=F32, res=None,
        a_idx=None, b_idx=None, dims=None, a_lead=False):
    if dims is None:
        m, k = (a.shape[1], a.shape[0]) if ta else a.shape
        n = b.shape[0] if tb else b.shape[1]
    else:
        m, n, k = dims
    tm = _tile(m, tm or m, LANE if ta else ROW_ALIGN)
    tn = _tile(n, tn)
    tk = _tile(k, tk or k, ROW_ALIGN if (ta and not tb) else LANE)
    nm, nn, nk = m // tm, n // tn, k // tk
    a_idx = a_idx or ((lambda i, j, kk: (kk, i)) if ta else (lambda i, j, kk: (i, kk)))
    b_idx = b_idx or ((lambda i, j, kk: (j, kk)) if tb else (lambda i, j, kk: (kk, j)))
    dn = (((0 if ta else 1,), (1 if tb else 0,)), ((), ()))

    def body(*refs):
        a_ref, b_ref = refs[0], refs[1]
        r_ref = refs[2] if res is not None else None
        o_ref = refs[3] if res is not None else refs[2]
        d = lax.dot_general(a_ref[...].astype(BF16), b_ref[...].astype(BF16), dn, preferred_element_type=F32)

        def finish(r):
            if r_ref is not None:
                r = r + r_ref[...].astype(F32)
            o_ref[...] = r.astype(out_dtype)

        if nk == 1:
            finish(d)
        else:
            acc = refs[-1]
            kk = pl.program_id(2)

            @pl.when(kk == 0)
            def _():
                acc[...] = d

            @pl.when(kk > 0)
            def _():
                acc[...] += d

            @pl.when(kk == nk - 1)
            def _():
                finish(acc[...])

    a_blk = ((None,) if a_lead else ()) + ((tk, tm) if ta else (tm, tk))
    in_specs = [pl.BlockSpec(a_blk, a_idx), pl.BlockSpec((tn, tk) if tb else (tk, tn), b_idx)]
    args = [a, b]
    if res is not None:
        in_specs.append(pl.BlockSpec((tm, tn), lambda i, j, kk: (i, j)))
        args.append(res)
    return _pc(body, name=name, grid=(nm, nn, nk), in_specs=in_specs,
               out_specs=pl.BlockSpec((tm, tn), lambda i, j, kk: (i, j)),
               out_shape=jax.ShapeDtypeStruct((m, n), out_dtype),
               scratch_shapes=[pltpu.VMEM((tm, tn), F32)] if nk > 1 else [],
               compiler_params=pltpu.CompilerParams(dimension_semantics=("parallel", "parallel", "arbitrary")))(*args)


def _ew(fn, ins, vecs, outs, sums=(), *, name, tm=None):
    ins = [x if isinstance(x, tuple) else (x, x.shape[1], 0) for x in ins]
    ins = [x if len(x) == 4 else x + (None,) for x in ins]
    outs = [o if len(o) == 3 else o + (None,) for o in outs]
    rows = ins[0][0].shape[0]
    cmax = max([c for _, c, _, _ in ins] + [c for c, _, _ in outs])
    tm = tm or _row_tile(rows, cmax)
    n_in, n_vec, n_out, n_sum = len(ins), len(vecs), len(outs), len(sums)

    def body(*refs):
        i = pl.program_id(0)
        rid = i * tm + lax.broadcasted_iota(jnp.int32, (tm, 1), 0)
        vals = [r[...] for r in refs[:n_in + n_vec]]
        res = fn(rid, *vals)
        res = res if isinstance(res, (tuple, list)) else (res,)
        o_refs = refs[n_in + n_vec:]
        for o_ref, r, (_, _, mode) in zip(o_refs[:n_out], res[:n_out], outs):
            if mode == FIRST:
                @pl.when(i == 0)
                def _():
                    o_ref[...] = r.astype(o_ref.dtype)
            else:
                o_ref[...] = r.astype(o_ref.dtype)
        for o_ref, r in zip(o_refs[n_out:], res[n_out:]):
            part = jnp.sum(r.astype(F32), axis=0, keepdims=True)

            @pl.when(i == 0)
            def _():
                o_ref[...] = part

            @pl.when(i > 0)
            def _():
                o_ref[...] += part

    def row_idx(mode):
        if mode == SKIP:
            return lambda i, cb=0: (jnp.maximum(i - 1, 0), cb)
        if mode == FIRST:
            return lambda i, cb=0: (0, cb)
        return lambda i, cb=0: (i, cb)

    in_specs = [pl.BlockSpec((tm, c), functools.partial(row_idx(mode), cb=cb)) for _, c, cb, mode in ins]
    in_specs += [pl.BlockSpec(v.shape, functools.partial(lambda i, nd: (0,) * nd, nd=v.ndim)) for v in vecs]
    out_specs = [pl.BlockSpec((tm, c), row_idx(mode)) for c, _, mode in outs]
    out_specs += [pl.BlockSpec((1, c), lambda i: (0, 0)) for c in sums]
    out_rows = {None: rows, SKIP: rows - tm, FIRST: tm}
    out_shape = [jax.ShapeDtypeStruct((out_rows[mode], c), dt) for c, dt, mode in outs]
    out_shape += [jax.ShapeDtypeStruct((1, c), F32) for c in sums]
    return _pc(body, name=name, grid=(rows // tm,), in_specs=in_specs, out_specs=out_specs, out_shape=out_shape,
               compiler_params=pltpu.CompilerParams(dimension_semantics=("arbitrary",)))(*[x[0] for x in ins], *vecs)


def _rms_parts(x, g):
    r = lax.rsqrt(jnp.mean(x * x, axis=-1, keepdims=True) + EPS)
    return x * r, r


def _rms_bwd_block(x, g, dy):
    xhat, r = _rms_parts(x, g)
    dxhat = dy * g
    dx = r * (dxhat - xhat * jnp.mean(dxhat * xhat, axis=-1, keepdims=True))
    return dx, dy * xhat


def _rms_fwd(x, g, *, name):
    c = x[1] if isinstance(x, tuple) else x.shape[1]
    return _ew(lambda rid, xv, gv: _rms_parts(xv.astype(F32), gv)[0] * gv, [x], [g], [(c, BF16)], name=name)[0]


def _rms_bwd(x, g, dy, *, name, res=None, mask=False, with_bf16=False):
    c = x[1] if isinstance(x, tuple) else x.shape[1]

    def fn(rid, xv, dyv, *rest):
        gv = rest[-1]
        dx, dg = _rms_bwd_block(xv.astype(F32), gv, dyv.astype(F32))
        if res is not None:
            dx = dx + rest[0]
        if mask:
            dx = jnp.where(rid >= PAD, dx, 0.0)
        return (dx, dx, dg) if with_bf16 else (dx, dg)

    ins = [x, dy] + ([res] if res is not None else [])
    outs = [(c, F32)] + ([(c, BF16)] if with_bf16 else [])
    return _ew(fn, ins, [g], outs, [c], name=name)


S5_W = GROUPS_PER_BLOCK * SSM_STATE
S5_GW = GROUPS_PER_BLOCK * SSM_GROUP
S5_UNROLL = 8
S5_DA_ROWS = 272


def _s5_scan_in_place(ref, pw_ref, *, reverse):
    lp = ref.shape[0]
    tile_rows = 8
    chunk = _tile(lp, S5_DA_ROWS, tile_rows)
    tiles = chunk // tile_rows

    def chunk_body(c, carry):
        rows = pl.ds(pl.multiple_of(c * chunk, tile_rows), chunk)
        xr, xi = ref[rows, :S5_W], ref[rows, S5_W:]
        for level, k in enumerate((1, 2, 4)):
            base = tile_rows * (1 + level)
            mr, mi = pw_ref[base:base + tile_rows, :S5_W][None], pw_ref[base:base + tile_rows, S5_W:][None]
            shift = chunk - k if reverse else k
            sr = pltpu.roll(xr, shift, 0).reshape(tiles, tile_rows, S5_W)
            si = pltpu.roll(xi, shift, 0).reshape(tiles, tile_rows, S5_W)
            xr = xr + (mr * sr - mi * si).reshape(chunk, S5_W)
            xi = xi + (mr * si + mi * sr).reshape(chunk, S5_W)
        ref[rows, :S5_W] = xr
        ref[rows, S5_W:] = xi
        return carry

    lax.fori_loop(0, lp // chunk, chunk_body, 0)

    pr, pi = pw_ref[0:tile_rows, :S5_W], pw_ref[0:tile_rows, S5_W:]
    ntile = lp // tile_rows
    unroll = 4

    def step(n, carry):
        cr, ci = carry
        for q in range(unroll):
            j = n * unroll + q
            j = ntile - 1 - j if reverse else j
            rows = pl.ds(pl.multiple_of(j * tile_rows, tile_rows), tile_rows)
            nr = ref[rows, :S5_W] + (pr * cr - pi * ci)
            ni = ref[rows, S5_W:] + (pr * ci + pi * cr)
            ref[rows, :S5_W] = nr
            ref[rows, S5_W:] = ni
            cr, ci = (nr[0:1], ni[0:1]) if reverse else (nr[tile_rows - 1:], ni[tile_rows - 1:])
        return cr, ci

    z = jnp.zeros((1, S5_W), F32)
    lax.fori_loop(0, ntile // unroll, step, (z, z))


def _s5_fwd(z, bb_band, cc_band, a_l, cfg, *, name):
    lp, ds, nl = cfg.LP, cfg.DS, cfg.NL

    def body(u_ref, bb_ref, cc_ref, a_ref, hs_ref, y_ref):
        hs_ref[...] = jnp.dot(u_ref[...].astype(BF16), bb_ref[...], preferred_element_type=F32)
        _s5_scan_in_place(hs_ref, a_ref, reverse=False)
        y_ref[...] = jnp.dot(hs_ref[...].astype(BF16), cc_ref[...], preferred_element_type=F32)

    return _pc(body, name=name, grid=(cfg.NB,),
               in_specs=[pl.BlockSpec((lp, S5_GW), lambda j: (0, j)), pl.BlockSpec((S5_GW, 2 * S5_W), lambda j: (j, 0)),
                         pl.BlockSpec((2 * S5_W, S5_GW), lambda j: (j, 0)), pl.BlockSpec((32, 2 * S5_W), lambda j: (0, j))],
               out_specs=[pl.BlockSpec((lp, 2 * S5_W), lambda j: (0, j)), pl.BlockSpec((lp, S5_GW), lambda j: (0, j))],
               out_shape=[jax.ShapeDtypeStruct((lp, nl), F32), jax.ShapeDtypeStruct((lp, ds), F32)],
               compiler_params=pltpu.CompilerParams(dimension_semantics=("parallel",)))(z, bb_band, cc_band, a_l)


def _s5_bwd(dy, hs, z, bb_band, cc_band, a_l, du_skip, cfg, *, name):
    lp, ds, nl = cfg.LP, cfg.DS, cfg.NL
    nt = (((1,), (1,)), ((), ()))
    tn = (((0,), (0,)), ((), ()))

    def body(dy_ref, hs_ref, u_ref, bb_ref, cc_ref, a_ref, sk_ref, du_ref, dbb_ref, dcc_ref, da_ref, g_ref):
        dyv = dy_ref[...]
        g_ref[...] = lax.dot_general(dyv, cc_ref[...], nt, preferred_element_type=F32)
        _s5_scan_in_place(g_ref, a_ref, reverse=True)
        dcc_ref[...] = lax.dot_general(hs_ref[...].astype(BF16), dyv, tn, preferred_element_type=F32)
        gb = g_ref[...].astype(BF16)
        dbb_ref[...] = lax.dot_general(u_ref[...].astype(BF16), gb, tn, preferred_element_type=F32)
        du_ref[...] = lax.dot_general(gb, bb_ref[...], nt, preferred_element_type=F32) + sk_ref[...]
        dre = jnp.zeros((1, S5_W), F32)
        dim = jnp.zeros((1, S5_W), F32)
        for r0 in range(0, lp, S5_DA_ROWS):
            rows = min(S5_DA_ROWS, lp - r0)
            first = lax.broadcasted_iota(jnp.int32, (rows, 1), 0) == 0
            prev = hs_ref[r0 - 1:r0, :] if r0 else jnp.zeros((1, 2 * S5_W), F32)
            hr = jnp.where(first, prev[:, :S5_W], pltpu.roll(hs_ref[r0:r0 + rows, :S5_W], 1, 0))
            hi = jnp.where(first, prev[:, S5_W:], pltpu.roll(hs_ref[r0:r0 + rows, S5_W:], 1, 0))
            gr, gi = g_ref[r0:r0 + rows, :S5_W], g_ref[r0:r0 + rows, S5_W:]
            dre = dre + jnp.sum(gr * hr + gi * hi, axis=0, keepdims=True)
            dim = dim + jnp.sum(gi * hr - gr * hi, axis=0, keepdims=True)
        da_ref[:, :S5_W] = dre
        da_ref[:, S5_W:] = dim

    col_blk = pl.BlockSpec((lp, S5_GW), lambda j: (0, j))
    lane_blk = pl.BlockSpec((lp, 2 * S5_W), lambda j: (0, j))
    bb_blk = pl.BlockSpec((S5_GW, 2 * S5_W), lambda j: (j, 0))
    cc_blk = pl.BlockSpec((2 * S5_W, S5_GW), lambda j: (j, 0))
    a_blk = pl.BlockSpec((1, 2 * S5_W), lambda j: (0, j))
    pw_blk = pl.BlockSpec((32, 2 * S5_W), lambda j: (0, j))
    return _pc(body, name=name, grid=(cfg.NB,),
               in_specs=[col_blk, lane_blk, col_blk, bb_blk, cc_blk, pw_blk, col_blk],
               out_specs=[col_blk, bb_blk, cc_blk, a_blk],
               out_shape=[jax.ShapeDtypeStruct((lp, ds), F32), jax.ShapeDtypeStruct((ds, 2 * S5_W), F32),
                          jax.ShapeDtypeStruct((nl, S5_GW), F32), jax.ShapeDtypeStruct((1, nl), F32)],
               scratch_shapes=[pltpu.VMEM((lp, 2 * S5_W), F32)],
               compiler_params=pltpu.CompilerParams(dimension_semantics=("parallel",)))(dy, hs, z, bb_band, cc_band, a_l, du_skip)


def _conv_gate(pre, cw, cb):
    return cw[0:1] * pltpu.roll(pre, 2, 0) + cw[1:2] * pltpu.roll(pre, 1, 0) + cw[2:3] * pre + cb


def _ffn_up(xn2, w_upt, cw, cb, *, name):
    lp, d = xn2.shape
    fp = w_upt.shape[0] // 2
    tc = _tile(fp, 256)
    nb = fp // tc

    def body(x_ref, wg_ref, wv_ref, cw_ref, cb_ref, up_ref, act_ref):
        wcat = jnp.concatenate([wg_ref[...], wv_ref[...]], axis=0)
        r = lax.dot_general(x_ref[...], wcat, (((1,), (1,)), ((), ())), preferred_element_type=F32)
        pre, val = r[:, :tc].astype(BF16), r[:, tc:].astype(BF16)
        up_ref[0] = pre
        up_ref[1] = val
        gate = _conv_gate(pre.astype(F32), cw_ref[...], cb_ref[...])
        act_ref[...] = (jax.nn.silu(gate) * val.astype(F32)).astype(BF16)

    return _pc(body, name=name, grid=(nb,),
               in_specs=[pl.BlockSpec((lp, d), lambda j: (0, 0)), pl.BlockSpec((tc, d), lambda j: (j, 0)),
                         pl.BlockSpec((tc, d), lambda j: (nb + j, 0)),
                         pl.BlockSpec((3, tc), lambda j: (0, j)), pl.BlockSpec((1, tc), lambda j: (0, j))],
               out_specs=[pl.BlockSpec((2, lp, tc), lambda j: (0, 0, j)), pl.BlockSpec((lp, tc), lambda j: (0, j))],
               out_shape=[jax.ShapeDtypeStruct((2, lp, fp), BF16), jax.ShapeDtypeStruct((lp, fp), BF16)],
               compiler_params=pltpu.CompilerParams(dimension_semantics=("parallel",)))(xn2, w_upt, w_upt, cw, cb)


def _ffn_dact(dh2, w_down, up, cw, cb, *, name):
    lp, d = dh2.shape
    fp = w_down.shape[0]
    tc = _tile(fp, 256)
    nb = fp // tc

    def body(dh_ref, wd_ref, up_ref, cw_ref, cb_ref, dup_ref, dcw_ref, dcb_ref):
        da = lax.dot_general(dh_ref[...], wd_ref[...], (((1,), (1,)), ((), ())), preferred_element_type=F32)
        pre, val, cwv = up_ref[0].astype(F32), up_ref[1].astype(F32), cw_ref[...]
        gate = _conv_gate(pre, cwv, cb_ref[...])
        sg = jax.nn.sigmoid(gate)
        dup_ref[1] = (da * (gate * sg)).astype(BF16)
        dgate = da * val * (sg * (1.0 + gate * (1.0 - sg)))
        dpre = cwv[2:3] * dgate + cwv[1:2] * pltpu.roll(dgate, lp - 1, 0) + cwv[0:1] * pltpu.roll(dgate, lp - 2, 0)
        dup_ref[0] = dpre.astype(BF16)
        dcb_ref[...] = jnp.sum(dgate, axis=0, keepdims=True)
        dcw_ref[0:1, :] = jnp.sum(dgate * pltpu.roll(pre, 2, 0), axis=0, keepdims=True)
        dcw_ref[1:2, :] = jnp.sum(dgate * pltpu.roll(pre, 1, 0), axis=0, keepdims=True)
        dcw_ref[2:3, :] = jnp.sum(dgate * pre, axis=0, keepdims=True)

    return _pc(body, name=name, grid=(nb,),
               in_specs=[pl.BlockSpec((lp, d), lambda j: (0, 0)), pl.BlockSpec((tc, d), lambda j: (j, 0)),
                         pl.BlockSpec((2, lp, tc), lambda j: (0, 0, j)),
                         pl.BlockSpec((3, tc), lambda j: (0, j)), pl.BlockSpec((1, tc), lambda j: (0, j))],
               out_specs=[pl.BlockSpec((2, lp, tc), lambda j: (0, 0, j)),
                          pl.BlockSpec((3, tc), lambda j: (0, j)), pl.BlockSpec((1, tc), lambda j: (0, j))],
               out_shape=[jax.ShapeDtypeStruct((2, lp, fp), BF16), jax.ShapeDtypeStruct((3, fp), F32),
                          jax.ShapeDtypeStruct((1, fp), F32)],
               compiler_params=pltpu.CompilerParams(dimension_semantics=("parallel",)))(dh2, w_down, up, cw, cb)


def _key_limit(i, tq, lp):
    return min(lp, -(-((i + 1) * tq) // LANE) * LANE)


def _attn_mask(i, tq, nk):
    qrow = i * tq + lax.broadcasted_iota(jnp.int32, (tq, 1), 0)
    krow = lax.broadcasted_iota(jnp.int32, (1, nk), 1)
    return (krow >= PAD) & ((krow // CHUNK) <= (qrow // CHUNK)), qrow >= PAD


def _attn_scores(q, kn, kr, i, tq, scale):
    nt = (((1,), (1,)), ((), ()))
    s = lax.dot_general(q[:, :QK_NOPE], kn, nt, preferred_element_type=F32)
    s = s + lax.dot_general(q[:, QK_NOPE:], kr, nt, preferred_element_type=F32)
    mask, qvalid = _attn_mask(i, tq, kn.shape[0])
    return jnp.where(mask, s * scale, jnp.finfo(F32).min), qvalid


def _per_q_block(nq, fn):
    i = pl.program_id(1)
    for blk in range(nq):
        pl.when(i == blk)(functools.partial(fn, blk))


def _attn_fwd(qx, kv, kr, cfg, *, name):
    lp, h = cfg.LP, cfg.H
    tq = _tile(lp, 272, ROW_ALIGN)
    nq = lp // tq
    scale = 1.0 / math.sqrt(QK_NOPE + QK_ROPE)

    def body(q_ref, kn_ref, v_ref, kr_ref, o_ref, lse_ref):
        def block(blk):
            nk = _key_limit(blk, tq, lp)
            s, qvalid = _attn_scores(q_ref[...], kn_ref[:nk], kr_ref[:nk], blk, tq, scale)
            m = jnp.max(s, axis=-1, keepdims=True)
            p = jnp.exp(s - m)
            l = jnp.sum(p, axis=-1, keepdims=True)
            o = jnp.dot(p.astype(BF16), v_ref[:nk], preferred_element_type=F32) / l
            o_ref[...] = jnp.where(qvalid, o, 0.0)
            lse_ref[...] = m + jnp.log(l)

        _per_q_block(nq, block)

    return _pc(body, name=name, grid=(h, nq),
               in_specs=[pl.BlockSpec((tq, HEAD_SLOT), lambda hh, i: (i, hh)),
                         pl.BlockSpec((lp, QK_NOPE), lambda hh, i: (0, 2 * hh)),
                         pl.BlockSpec((lp, V_HEAD), lambda hh, i: (0, 2 * hh + 1)),
                         pl.BlockSpec((lp, LANE), lambda hh, i: (0, 0))],
               out_specs=[pl.BlockSpec((tq, V_HEAD), lambda hh, i: (i, hh)),
                          pl.BlockSpec((None, tq, 1), lambda hh, i: (hh, i, 0))],
               out_shape=[jax.ShapeDtypeStruct((lp, h * V_HEAD), F32), jax.ShapeDtypeStruct((h, lp, 1), F32)],
               compiler_params=pltpu.CompilerParams(dimension_semantics=("parallel", "parallel")))(qx, kv, kv, kr)


def _attn_bwd(qx, kv, kr, o, lse, do, cfg, *, name):
    lp, h = cfg.LP, cfg.H
    tq = _tile(lp, 272, ROW_ALIGN)
    nq = lp // tq
    scale = 1.0 / math.sqrt(QK_NOPE + QK_ROPE)
    tn_dims = (((0,), (0,)), ((), ()))

    def body(q_ref, kn_ref, v_ref, kr_ref, o_ref, lse_ref, do_ref, dq_ref, dkv_ref, dkr_ref, dkv_acc):
        hh, i = pl.program_id(0), pl.program_id(1)

        @pl.when(i == 0)
        def _():
            dkv_acc[...] = jnp.zeros_like(dkv_acc)

        @pl.when((i == 0) & (hh == 0))
        def _():
            dkr_ref[...] = jnp.zeros_like(dkr_ref)

        def block(blk):
            nk = _key_limit(blk, tq, lp)
            q, kn, v, krv = q_ref[...], kn_ref[:nk], v_ref[:nk], kr_ref[:nk]
            s, qvalid = _attn_scores(q, kn, krv, blk, tq, scale)
            dov = jnp.where(qvalid, do_ref[...], 0.0)
            p = jnp.exp(s - lse_ref[...])
            delta = jnp.sum(dov * o_ref[...], axis=-1, keepdims=True)
            dob = dov.astype(BF16)
            dp = lax.dot_general(dob, v, (((1,), (1,)), ((), ())), preferred_element_type=F32)
            ds = (p * (dp - delta) * scale).astype(BF16)
            dq_ref[:, :QK_NOPE] = jnp.dot(ds, kn, preferred_element_type=F32)
            dq_ref[:, QK_NOPE:] = jnp.dot(ds, krv, preferred_element_type=F32)
            dkv_acc[:nk, :QK_NOPE] += lax.dot_general(ds, q[:, :QK_NOPE], tn_dims, preferred_element_type=F32)
            dkv_acc[:nk, QK_NOPE:] += lax.dot_general(p.astype(BF16), dob, tn_dims, preferred_element_type=F32)
            dkr_ref[:nk, :] += lax.dot_general(ds, q[:, QK_NOPE:], tn_dims, preferred_element_type=F32)

        _per_q_block(nq, block)

        @pl.when(i == nq - 1)
        def _():
            dkv_ref[...] = dkv_acc[...].astype(BF16)

    return _pc(body, name=name, grid=(h, nq),
               in_specs=[pl.BlockSpec((tq, HEAD_SLOT), lambda hh, i: (i, hh)),
                         pl.BlockSpec((lp, QK_NOPE), lambda hh, i: (0, 2 * hh)),
                         pl.BlockSpec((lp, V_HEAD), lambda hh, i: (0, 2 * hh + 1)),
                         pl.BlockSpec((lp, LANE), lambda hh, i: (0, 0)),
                         pl.BlockSpec((tq, V_HEAD), lambda hh, i: (i, hh)),
                         pl.BlockSpec((None, tq, 1), lambda hh, i: (hh, i, 0)),
                         pl.BlockSpec((tq, V_HEAD), lambda hh, i: (i, hh))],
               out_specs=[pl.BlockSpec((tq, HEAD_SLOT), lambda hh, i: (i, hh)),
                          pl.BlockSpec((lp, QK_NOPE + V_HEAD), lambda hh, i: (0, hh)),
                          pl.BlockSpec((lp, LANE), lambda hh, i: (0, 0))],
               out_shape=[jax.ShapeDtypeStruct((lp, h * HEAD_SLOT), F32),
                          jax.ShapeDtypeStruct((lp, h * (QK_NOPE + V_HEAD)), BF16),
                          jax.ShapeDtypeStruct((lp, LANE), F32)],
               scratch_shapes=[pltpu.VMEM((lp, QK_NOPE + V_HEAD), F32)],
               compiler_params=pltpu.CompilerParams(dimension_semantics=("arbitrary", "arbitrary")))(qx, kv, kv, kr, o, lse, do)


def _rot_half(x):
    lane = lax.broadcasted_iota(jnp.int32, x.shape, 1)
    half = QK_ROPE // 2
    return jnp.where(lane < half, -pltpu.roll(x, LANE - half, 1), pltpu.roll(x, half, 1))


def _rope(x, cos, sin):
    return x * cos + _rot_half(x) * sin


def _unrope(dy, cos, sin):
    return dy * cos - _rot_half(dy * sin)


def _rope_heads(fn, h):
    def apply(rid, q, cos, sin):
        parts = []
        for hh in range(h):
            parts.append(q[:, hh * HEAD_SLOT: hh * HEAD_SLOT + QK_NOPE])
            parts.append(fn(q[:, hh * HEAD_SLOT + QK_NOPE: (hh + 1) * HEAD_SLOT], cos, sin))
        return jnp.concatenate(parts, axis=1)
    return apply


ANY = pl.BlockSpec(memory_space=pl.ANY)


def _place():
    x, y, c = lax.axis_index("x"), lax.axis_index("y"), lax.axis_index("c")
    chips = [(1 - x, y), (x, 1 - y), (1 - x, 1 - y)]
    return x, y, c, chips


def _rcopy(src, dst, send_sem, recv_sem, dev):
    return pltpu.make_async_remote_copy(src_ref=src, dst_ref=dst, send_sem=send_sem, recv_sem=recv_sem,
                                        device_id=dev, device_id_type=MESH)


def _place_shard(shard, dtype, *, name, order=None, rows_to=None):
    shard = shard if shard.ndim == 3 else shard[None]
    n, r, cols = shard.shape
    rp = rows_to or r
    tm = _tile(r, max(ROW_ALIGN, PLACE_BLOCK_BYTES // (4 * cols)), ROW_ALIGN)
    me = (2 * lax.axis_index("x") + lax.axis_index("y")).astype(jnp.int32).reshape(1)
    extra = [] if order is None else [order]

    def body(me_ref, s_ref, *rest):
        rest[-1][...] = s_ref[...].astype(dtype)

    full = _pc(body, name=name,
               grid_spec=pltpu.PrefetchScalarGridSpec(
                   num_scalar_prefetch=1, grid=(n, r // tm),
                   in_specs=[pl.BlockSpec((None, tm, cols), lambda q, i, mr: (q, i, 0))] + [ANY] * len(extra),
                   out_specs=pl.BlockSpec((None, tm, cols), lambda q, i, mr: (mr[0] * n + q, i, 0))),
               out_shape=jax.ShapeDtypeStruct((4 * n, rp, cols), dtype),
               compiler_params=pltpu.CompilerParams(dimension_semantics=("arbitrary", "arbitrary")))(me, shard, *extra)
    if rp > r:
        pad = rp - r
        assert r % pad == 0

        def zero(me_ref, f_ref, o_ref):
            o_ref[...] = jnp.zeros_like(o_ref)

        full = _pc(zero, name=name + "_pad",
                   grid_spec=pltpu.PrefetchScalarGridSpec(
                       num_scalar_prefetch=1, grid=(n,), in_specs=[ANY],
                       out_specs=pl.BlockSpec((None, pad, cols), lambda q, mr: (mr[0] * n + q, r // pad, 0))),
                   out_shape=jax.ShapeDtypeStruct(full.shape, dtype), input_output_aliases={1: 0},
                   compiler_params=pltpu.CompilerParams(dimension_semantics=("arbitrary",)))(me, full)
    return full.reshape(4 * n * rp, cols)


def _allgather(fulls, *, name):
    n = len(fulls)

    def body(*refs):
        outs = refs[n:2 * n]
        send_sems, recv_sems = refs[2 * n:]
        x, y, c, chips = _place()
        sib = (x, y, 1 - c)
        me = 2 * x + y

        def rows(t, s, half):
            hrows = outs[t].shape[0] // 8
            return outs[t].at[pl.ds((2 * s + half) * hrows, hrows)]

        sent = []
        for t in range(n):
            for j, (cx, cy) in enumerate(chips):
                cp = _rcopy(rows(t, me, c), rows(t, me, c), send_sems.at[6 * t + j], recv_sems.at[6 * t + j], (cx, cy, c))
                cp.start()
                sent.append(cp)
        for t in range(n):
            for j, (cx, cy) in enumerate(chips):
                landed = rows(t, 2 * cx + cy, c)
                _rcopy(landed, landed, send_sems.at[6 * t + j], recv_sems.at[6 * t + j], (cx, cy, c)).wait_recv()
                cp = _rcopy(landed, landed, send_sems.at[6 * t + 3 + j], recv_sems.at[6 * t + 3 + j], sib)
                cp.start()
                sent.append(cp)
        for t in range(n):
            for j, (cx, cy) in enumerate(chips):
                other = rows(t, 2 * cx + cy, 1 - c)
                _rcopy(other, other, send_sems.at[6 * t + 3 + j], recv_sems.at[6 * t + 3 + j], sib).wait_recv()
        for cp in sent:
            cp.wait_send()

    return _pc(body, name=name, in_specs=[ANY] * n, out_specs=[ANY] * n,
               out_shape=[jax.ShapeDtypeStruct(f.shape, f.dtype) for f in fulls],
               input_output_aliases={t: t for t in range(n)},
               scratch_shapes=[pltpu.SemaphoreType.DMA((6 * n,)), pltpu.SemaphoreType.DMA((6 * n,))])(*fulls)


HBM = pl.BlockSpec(memory_space=pltpu.HBM)
SEM = pl.BlockSpec(memory_space=pltpu.SEMAPHORE)
EFFECT = pltpu.SideEffectType.DATAFLOW_SIDE_EFFECTING
TOKEN = jax.ShapeDtypeStruct((8, LANE), F32)


def _in_hbm(a):
    return pltpu.with_memory_space_constraint(a, pltpu.HBM)


def _half_rows(ref, s, half):
    hrows = ref.shape[0] // 8
    return ref.at[pl.ds((2 * s + half) * hrows, hrows)]


def _split_start(bufs, copies, n_copies, *, name, before=None):
    n = len(bufs)
    extra = [] if before is None else [before]

    def body(*refs):
        send_sems, recv_sems, token = refs[n + len(extra)], refs[n + len(extra) + 1], refs[-1]
        for k, (src, dst, dev) in enumerate(copies(refs[:n])):
            _rcopy(src, dst, send_sems.at[k], recv_sems.at[k], dev).start()
        token[...] = jnp.zeros_like(token)

    res = _pc(body, name=name, in_specs=[HBM] * n + [ANY] * len(extra),
              out_specs=[SEM, SEM] + [HBM] * n + [pl.BlockSpec(memory_space=pltpu.VMEM)],
              out_shape=[pltpu.SemaphoreType.DMA((n_copies,)), pltpu.SemaphoreType.DMA((n_copies,))]
              + [pltpu.HBM(b.shape, b.dtype) for b in bufs] + [TOKEN],
              input_output_aliases={t: 2 + t for t in range(n)},
              compiler_params=pltpu.CompilerParams(has_side_effects=EFFECT))(*[_in_hbm(b) for b in bufs], *extra)
    return res[0], res[1], list(res[2:2 + n]), res[-1]


def _split_wait(send_sems, recv_sems, bufs, copies, after, *, name):
    n = len(bufs)
    after = list(after) if isinstance(after, (list, tuple)) else [after]

    def body(*refs):
        send_ref, recv_ref = refs[n], refs[n + 1]
        for k, (src, dst, dev) in enumerate(copies(refs[:n])):
            cp = _rcopy(src, dst, send_ref.at[k], recv_ref.at[k], dev)
            cp.wait_send()
            cp.wait_recv()

    return _pc(body, name=name, in_specs=[HBM] * n + [SEM, SEM] + [ANY] * len(after), out_specs=[HBM] * n,
               out_shape=[pltpu.HBM(b.shape, b.dtype) for b in bufs],
               input_output_aliases={t: t for t in range(n)},
               compiler_params=pltpu.CompilerParams(has_side_effects=EFFECT))(*bufs, send_sems, recv_sems, *after)


def _allgather_ici_copies(refs):
    x, y, c, chips = _place()
    return [(_half_rows(r, 2 * x + y, c), _half_rows(r, 2 * x + y, c), (cx, cy, c)) for r in refs for cx, cy in chips]


def _rs_chips_copies(refs):
    x, y, c, chips = _place()
    n = len(refs) // 2
    return [(refs[t].at[2 * cx + cy], refs[n + t].at[j], (cx, cy, c)) for t in range(n) for j, (cx, cy) in enumerate(chips)]


def _allgather_forward_copies(refs):
    x, y, c, chips = _place()
    return [(_half_rows(r, 2 * cx + cy, c), _half_rows(r, 2 * cx + cy, c), (x, y, 1 - c)) for r in refs for cx, cy in chips]


def _rs_final_copies(refs):
    x, y, c, _ = _place()
    return [(r.at[c], r.at[c], (x, y, 1 - c)) for r in refs]


def _rs_sibling_copies(refs):
    x, y, c, _ = _place()
    n = len(refs) // 2
    out = []
    for t in range(n):
        h = refs[t].shape[0] // 8
        out += [(refs[t].at[pl.ds((2 * s + 1 - c) * h, h)], refs[n + t].at[s], (x, y, 1 - c)) for s in range(4)]
    return out


def _allgather_forward(fulls, *, name):
    n = len(fulls)

    def body(*refs):
        outs = refs[n:2 * n]
        send_sems, recv_sems = refs[2 * n:]
        x, y, c, chips = _place()
        sent = []
        for t in range(n):
            for j, (cx, cy) in enumerate(chips):
                landed = _half_rows(outs[t], 2 * cx + cy, c)
                cp = _rcopy(landed, landed, send_sems.at[3 * t + j], recv_sems.at[3 * t + j], (x, y, 1 - c))
                cp.start()
                sent.append(cp)
        for t in range(n):
            for j, (cx, cy) in enumerate(chips):
                other = _half_rows(outs[t], 2 * cx + cy, 1 - c)
                _rcopy(other, other, send_sems.at[3 * t + j], recv_sems.at[3 * t + j], (x, y, 1 - c)).wait_recv()
        for cp in sent:
            cp.wait_send()

    return _pc(body, name=name, in_specs=[ANY] * n, out_specs=[ANY] * n,
               out_shape=[jax.ShapeDtypeStruct(f.shape, f.dtype) for f in fulls],
               input_output_aliases={t: t for t in range(n)},
               scratch_shapes=[pltpu.SemaphoreType.DMA((3 * n,)), pltpu.SemaphoreType.DMA((3 * n,))])(*fulls)


def _rs_sibling(grads, *, name):
    n = len(grads)

    def body(*refs):
        ins, outs = refs[:n], refs[n:2 * n]
        send_sems, recv_sems = refs[2 * n:]
        x, y, c, _ = _place()
        cps = []
        for t in range(n):
            h = ins[t].shape[0] // 8
            for s in range(4):
                cp = _rcopy(ins[t].at[pl.ds((2 * s + 1 - c) * h, h)], outs[t].at[s], send_sems.at[4 * t + s],
                            recv_sems.at[4 * t + s], (x, y, 1 - c))
                cp.start()
                cps.append(cp)
        for cp in cps:
            cp.wait()

    return _pc(body, name=name, in_specs=[ANY] * n, out_specs=[ANY] * n,
               out_shape=[jax.ShapeDtypeStruct((4, g.shape[0] // 8, g.shape[1]), g.dtype) for g in grads],
               scratch_shapes=[pltpu.SemaphoreType.DMA((4 * n,)), pltpu.SemaphoreType.DMA((4 * n,))])(*grads)


def _rs_chips(sends, *, name):
    n = len(sends)

    def body(*refs):
        s_refs, b_refs = refs[:n], refs[n:2 * n]
        send_sems, recv_sems = refs[2 * n:]
        x, y, c, chips = _place()
        cps = []
        for t in range(n):
            for j, (cx, cy) in enumerate(chips):
                cp = _rcopy(s_refs[t].at[2 * cx + cy], b_refs[t].at[j], send_sems.at[3 * t + j], recv_sems.at[3 * t + j],
                            (cx, cy, c))
                cp.start()
                cps.append(cp)
        for cp in cps:
            cp.wait()

    return _pc(body, name=name, in_specs=[ANY] * n, out_specs=[ANY] * n,
               out_shape=[jax.ShapeDtypeStruct((3,) + s.shape[1:], s.dtype) for s in sends],
               scratch_shapes=[pltpu.SemaphoreType.DMA((3 * n,)), pltpu.SemaphoreType.DMA((3 * n,))])(*sends)


def _rs_final(fulls, *, name):
    n = len(fulls)

    def body(*refs):
        outs = refs[n:2 * n]
        send_sems, recv_sems = refs[2 * n:]
        x, y, c, _ = _place()
        cps = []
        for t in range(n):
            cp = _rcopy(outs[t].at[c], outs[t].at[c], send_sems.at[t], recv_sems.at[t], (x, y, 1 - c))
            cp.start()
            cps.append(cp)
        for cp in cps:
            cp.wait()

    return _pc(body, name=name, in_specs=[ANY] * n, out_specs=[ANY] * n,
               out_shape=[jax.ShapeDtypeStruct(f.shape, f.dtype) for f in fulls],
               input_output_aliases={t: t for t in range(n)},
               scratch_shapes=[pltpu.SemaphoreType.DMA((n,)), pltpu.SemaphoreType.DMA((n,))])(*fulls)


def _add_halves(g, a, send_dtype, *, name):
    _, h, cols = a.shape
    th = _row_tile(h, cols)
    g4 = g.reshape(4, 2, h, cols)
    idx = jnp.stack([lax.axis_index("c"), 2 * lax.axis_index("x") + lax.axis_index("y")]).astype(jnp.int32)

    def shard(k, ir):
        return (ir[1] + 1 + k) % 4

    def body(idx_ref, g_ref, a_ref, p_ref, s_ref):
        v = g_ref[...].astype(F32) + a_ref[...].astype(F32)
        s_ref[...] = v.astype(send_dtype)

        @pl.when(pl.program_id(1) == 3)
        def _():
            p_ref[...] = v

    return _pc(body, name=name,
               grid_spec=pltpu.PrefetchScalarGridSpec(
                   num_scalar_prefetch=1, grid=(h // th, 4),
                   in_specs=[pl.BlockSpec((None, None, th, cols), lambda i, k, ir: (shard(k, ir), ir[0], i, 0)),
                             pl.BlockSpec((None, th, cols), lambda i, k, ir: (shard(k, ir), i, 0))],
                   out_specs=[pl.BlockSpec((th, cols), lambda i, k, ir: (i, 0)),
                              pl.BlockSpec((None, th, cols), lambda i, k, ir: (shard(k, ir), i, 0))]),
               out_shape=[jax.ShapeDtypeStruct((h, cols), F32), jax.ShapeDtypeStruct(a.shape, send_dtype)],
               compiler_params=pltpu.CompilerParams(dimension_semantics=("arbitrary", "arbitrary")))(idx, g4, a)


def _add_chips(p, b, *, name, order=None):
    h, cols = p.shape
    th = _row_tile(h, cols)
    idx = lax.axis_index("c").astype(jnp.int32).reshape(1)
    extra = [] if order is None else [order]

    def body(idx_ref, p_ref, b_ref, *rest):
        r_ref = rest[-1]
        r_ref[...] = ((p_ref[...] + b_ref[0].astype(F32)) + b_ref[1].astype(F32)) + b_ref[2].astype(F32)

    return _pc(body, name=name,
               grid_spec=pltpu.PrefetchScalarGridSpec(
                   num_scalar_prefetch=1, grid=(h // th,),
                   in_specs=[pl.BlockSpec((th, cols), lambda i, ir: (i, 0)),
                             pl.BlockSpec((3, th, cols), lambda i, ir: (0, i, 0))] + [ANY] * len(extra),
                   out_specs=pl.BlockSpec((None, th, cols), lambda i, ir: (ir[0], i, 0))),
               out_shape=jax.ShapeDtypeStruct((2, h, cols), F32),
               compiler_params=pltpu.CompilerParams(dimension_semantics=("arbitrary",)))(idx, p, b, *extra)


def _add_halves_all(grads, recv, send_dtypes, tag):
    parts, sends = [], []
    for t, (g, a) in enumerate(zip(grads, recv)):
        p, s = _add_halves(g, a, send_dtypes[t], name=f"rs_add_halves_{tag}{t}")
        parts.append(p)
        sends.append(s)
    return parts, sends


def _rs_finish(parts, others, tag, order=None):
    halves = [_add_chips(p, b, order=order, name=f"rs_add_chips_{tag}{t}") for t, (p, b) in enumerate(zip(parts, others))]
    full = _rs_final(halves, name=f"rs_final_{tag}")
    return [f.reshape(-1, f.shape[-1]) for f in full]


def _s5_discretize(lam_re, lam_im, log_dt, b_re, b_im):
    lam = lax.complex(lam_re, lam_im)
    dt = jnp.exp(log_dt)[:, None]
    lam_bar = jnp.exp(lam * dt)
    b_bar = ((lam_bar - 1.0) / lam)[..., None] * lax.complex(b_re, b_im)
    return jnp.real(lam_bar), jnp.imag(lam_bar), jnp.real(b_bar), jnp.imag(b_bar)


def _lanes_from_gp(re, im, cfg):
    v = jnp.stack([re, im]).reshape(2, cfg.NB, GROUPS_PER_BLOCK, SSM_STATE)
    return jnp.transpose(v, (1, 0, 2, 3)).reshape(1, cfg.NL)


def _gp_from_lanes(v, cfg):
    v = jnp.transpose(v.reshape(cfg.NB, 2, GROUPS_PER_BLOCK, SSM_STATE), (1, 0, 2, 3)).reshape(2, cfg.G, SSM_STATE)
    return v[0], v[1]


def _bb_band(bb_re, bb_im, cfg):
    eye = jnp.eye(GROUPS_PER_BLOCK, dtype=F32)
    bb = jnp.stack([bb_re, bb_im]).reshape(2, cfg.NB, GROUPS_PER_BLOCK, SSM_STATE, SSM_GROUP)
    return jnp.einsum('rjgpc,gh->jgcrhp', bb, eye).reshape(cfg.DS, 2 * GROUPS_PER_BLOCK * SSM_STATE)


def _bb_from_band(m, cfg):
    eye = jnp.eye(GROUPS_PER_BLOCK, dtype=F32)
    m = m.reshape(cfg.NB, GROUPS_PER_BLOCK, SSM_GROUP, 2, GROUPS_PER_BLOCK, SSM_STATE)
    v = jnp.einsum('jgcrhp,gh->rjgpc', m, eye).reshape(2, cfg.G, SSM_STATE, SSM_GROUP)
    return v[0], v[1]


def _cc_band(c_re, c_im, cfg):
    eye = jnp.eye(GROUPS_PER_BLOCK, dtype=F32)
    cc = jnp.stack([c_re, -c_im]).reshape(2, cfg.NB, GROUPS_PER_BLOCK, SSM_GROUP, SSM_STATE)
    return jnp.einsum('rjgcp,gh->jrhpgc', cc, eye).reshape(cfg.NL, GROUPS_PER_BLOCK * SSM_GROUP)


def _cc_from_band(m, cfg):
    eye = jnp.eye(GROUPS_PER_BLOCK, dtype=F32)
    m = m.reshape(cfg.NB, 2, GROUPS_PER_BLOCK, SSM_STATE, GROUPS_PER_BLOCK, SSM_GROUP)
    v = jnp.einsum('jrhpgc,gh->rjgcp', m, eye).reshape(2, cfg.G, SSM_GROUP, SSM_STATE)
    return v[0], -v[1]


PACK_COLS = 512
PACK_ROW_ALIGN = 64


def _pack(arrs):
    flat = jnp.concatenate([a.reshape(-1).astype(F32) for a in arrs])
    unit = PACK_COLS * PACK_ROW_ALIGN
    total = -(-flat.shape[0] // unit) * unit
    return jnp.pad(flat, (0, total - flat.shape[0])).reshape(-1, PACK_COLS)


def _unpack(p, shapes):
    flat = p.reshape(-1)
    out, off = [], 0
    for shp in shapes:
        size = math.prod(shp)
        out.append(flat[off:off + size].reshape(shp))
        off += size
    return out


def _adamw(w, g, m, v, *, name, emit_grad=False):
    c1 = 1.0 / (1.0 - ADAM_B1 ** ADAM_STEP)
    c2 = 1.0 / (1.0 - ADAM_B2 ** ADAM_STEP)

    if w.ndim == 2:
        outs = _adamw(w[None], g[None], m[None], v[None], name=name, emit_grad=emit_grad)
        return [o[0] for o in outs]
    lead, rows, cols = w.shape
    tc = _tile(cols, 512)
    tm = _tile(rows, max(8, ADAMW_BLOCK_BYTES // (4 * tc)), 8)
    n_out = 4 if emit_grad else 3

    def body(w_ref, g_ref, m_ref, v_ref, *o_refs):
        gv = g_ref[...]
        mn = ADAM_B1 * m_ref[...] + (1.0 - ADAM_B1) * gv
        vn = ADAM_B2 * v_ref[...] + (1.0 - ADAM_B2) * (gv * gv)
        delta = -ADAM_LR * ((mn * c1) / (jnp.sqrt(vn * c2) + ADAM_EPS) + ADAM_WD * w_ref[...])
        for o_ref, val in zip(o_refs, ((gv, delta, mn, vn) if emit_grad else (delta, mn, vn))):
            o_ref[...] = val

    blk = pl.BlockSpec((None, tm, tc), lambda n, i, j: (n, i, j))
    return _pc(body, name=name, grid=(lead, rows // tm, cols // tc), in_specs=[blk] * 4, out_specs=[blk] * n_out,
               out_shape=[jax.ShapeDtypeStruct((lead, rows, cols), F32)] * n_out,
               compiler_params=pltpu.CompilerParams(dimension_semantics=("parallel", "parallel", "parallel")))(w, g, m, v)


def _to_comm_layout(name, w, cfg):
    w = w[0]
    if name == 'w_in':
        return jnp.pad(w, ((0, 0), (0, cfg.DINP - cfg.DIN)))
    if name == 'w_q_b':
        hs = w.shape[1] // (QK_NOPE + QK_ROPE)
        wt = w.T.reshape(hs, QK_NOPE + QK_ROPE, cfg.QL)
        return jnp.pad(wt, ((0, 0), (0, HEAD_SLOT - QK_NOPE - QK_ROPE), (0, 0))).reshape(hs * HEAD_SLOT, cfg.QL)
    if name == 'w_kv_b':
        return w.T
    if name == 'w_up':
        return w.T.reshape(2, cfg.F // 4, cfg.D)
    return w


def _from_comm_layout(name, g, cfg):
    if name == 'w_in':
        g = g[:, :cfg.DIN]
    elif name == 'w_q_b':
        hs = g.shape[0] // HEAD_SLOT
        g = g.reshape(hs, HEAD_SLOT, cfg.QL)[:, :QK_NOPE + QK_ROPE].reshape(hs * (QK_NOPE + QK_ROPE), cfg.QL).T
    elif name == 'w_kv_b':
        g = g.T
    elif name == 'w_up':
        g = g.reshape(2, cfg.FQ, cfg.D)[:, :cfg.F // 4].reshape(cfg.F // 2, cfg.D).T
    elif name == 'w_down':
        g = g[:cfg.F // 4]
    return g[None]


def _ff_pad(v, cfg):
    k = v.shape[0]
    return jnp.pad(v.reshape(k, 4, cfg.F // 4), ((0, 0), (0, 0), (0, cfg.FQ - cfg.F // 4))).reshape(k, cfg.FP)


def _ff_unpad(v, cfg):
    k = v.shape[0]
    return v.reshape(k, 4, cfg.FQ)[:, :, :cfg.F // 4].reshape(k, cfg.F)


def _step(cfg, w, m, v, x, loss_target):
    lp, d, ds, nl = cfg.LP, cfg.D, cfg.DS, cfg.NL
    xi, yi = lax.axis_index("x"), lax.axis_index("y")
    me = 2 * xi + yi

    def place(n, order=None):
        rows_to = cfg.FQ if n in ('w_up', 'w_down') else None
        return _place_shard(_to_comm_layout(n, w[n], cfg), BF16, order=order, rows_to=rows_to, name=f"place_{n}")

    first = [place('w_in'), _place_shard(w['meta_tokens'], F32, name="place_meta")]
    f_send, f_recv, f_flying, f_token = _split_start(first, _allgather_ici_copies, 6, name="allgather_first_start")
    conv_w_shard = jnp.pad(w['conv_w'][0], ((0, ROW_ALIGN - 3), (0, cfg.FQ - cfg.F // 4)))
    mid = [place(n, f_token) for n in BIG[1:5]] + [_place_shard(conv_w_shard, F32, order=f_token, name="place_conv_w")]
    mid_send, mid_recv, mid_flying, mid_token = _split_start(mid, _allgather_ici_copies, 3 * len(mid), before=f_token,
                                                             name="allgather_mid_start")
    up_send, up_recv, up_flying, up_token = _split_start([place('w_up', mid_token)], _allgather_ici_copies, 3,
                                                         before=mid_token, name="allgather_up_start")
    dn_send, dn_recv, dn_flying, ffn_token = _split_start([place('w_down', up_token)], _allgather_ici_copies, 3,
                                                          before=up_token, name="allgather_down_start")
    conv_b = _ff_pad(w['conv_b'], cfg)

    pos = (jnp.arange(lp, dtype=jnp.int32) - PAD).astype(F32)
    inv_freq = 1.0 / (ROPE_BASE ** (jnp.arange(0, QK_ROPE, 2, dtype=F32) / QK_ROPE))
    ang = pos[:, None] * inv_freq[None, :]
    zpad = jnp.zeros((lp, LANE - QK_ROPE), F32)
    cos_t = jnp.concatenate([jnp.cos(ang), jnp.cos(ang), zpad], axis=1)
    sin_t = jnp.concatenate([jnp.sin(ang), jnp.sin(ang), zpad], axis=1)

    s5_in = (w['lam_re'][0], w['lam_im'][0], w['log_dt'][0], w['b_re'][0], w['b_im'][0])
    (a_re, a_im, bb_re, bb_im), s5_vjp = jax.vjp(_s5_discretize, *s5_in)
    lam_dt = lax.complex(s5_in[0], s5_in[1]) * jnp.exp(s5_in[2])[:, None]
    a_pow = jnp.exp(jnp.arange(1, 9, dtype=F32)[:, None, None] * lam_dt[None])
    r8 = jnp.arange(8)
    step_f = jnp.stack([jnp.where((r8 >= k)[:, None, None], a_pow[k - 1][None], 0.0) for k in (1, 2, 4)]).reshape(24, cfg.G, -1)
    step_b = jnp.stack([jnp.where((r8 < 8 - k)[:, None, None], a_pow[k - 1][None], 0.0) for k in (1, 2, 4)]).reshape(24, cfg.G, -1)
    rows_f = jnp.concatenate([a_pow, step_f])
    rows_b = jnp.conj(jnp.concatenate([a_pow[::-1], step_b]))

    def lane_rows(t):
        v = jnp.stack([jnp.real(t), jnp.imag(t)], axis=1).reshape(t.shape[0], 2, cfg.NB, GROUPS_PER_BLOCK, SSM_STATE)
        return jnp.transpose(v, (0, 2, 1, 3, 4)).reshape(t.shape[0], cfg.NL)

    pw_fwd, pw_bwd = lane_rows(rows_f), lane_rows(rows_b)
    bb_band = _bb_band(bb_re, bb_im, cfg).astype(BF16)
    cc_band = _cc_band(w['c_re'][0], w['c_im'][0], cfg).astype(BF16)
    d_skip, b_glu = w['d_skip'], w['b_glu']

    f_landed = _split_wait(f_send, f_recv, f_flying, _allgather_ici_copies, [ffn_token, cos_t, sin_t, pw_fwd, pw_bwd, bb_band, cc_band],
                           name="allgather_first_wait")
    w_in, meta_full = _allgather_forward(f_landed, name="allgather_first_forward")
    meta = jnp.transpose(meta_full.reshape(4, N_META, d // 4), (1, 0, 2)).reshape(N_META, d)
    mix_norm = w['mix_norm'] + ffn_token[0:1, 0:1]

    h0 = jnp.concatenate([jnp.zeros((PAD, d), F32), meta, x[0]], axis=0)
    xn = _rms_fwd(h0, mix_norm, name="rms_mix")
    z = _mm(xn, w_in, name="mm_in", tn=_tile(cfg.DINP, 640))
    u = (z, ds, 0)
    q_a = (z, cfg.QL, ds // cfg.QL)
    kv_a = (z, cfg.KVL, (ds + cfg.QL) // cfg.KVL)
    k_pe = (z, LANE, (ds + cfg.QL + cfg.KVL) // LANE)

    hs, yc = _s5_fwd(z, bb_band, cc_band, pw_fwd, cfg, name="s5_fwd")

    def s5_y(ycv, uv, dk):
        return ycv + dk * uv

    gl = _ew(lambda rid, ycv, uv, dk: jax.nn.gelu(s5_y(ycv, uv, dk)), [yc, u], [d_skip], [(ds, BF16)], name="s5_gelu")[0]
    mid_landed = _split_wait(mid_send, mid_recv, mid_flying, _allgather_ici_copies, gl, name="allgather_mid_wait")
    w_glu, w_qt, w_kvt, w_out, conv_full = _allgather_forward(mid_landed, name="allgather_mid_forward")
    conv_w = jnp.transpose(conv_full.reshape(4, ROW_ALIGN, cfg.FQ)[:, :3], (1, 0, 2)).reshape(3, cfg.FP)
    tg = _mm(gl, w_glu, name="mm_glu")
    ya = _ew(lambda rid, ycv, uv, tv, dk, bg: jax.nn.gelu(s5_y(ycv, uv, dk)) * jax.nn.sigmoid(tv + bg),
             [yc, u, tg], [d_skip, b_glu], [(ds, F32)], name="s5_glu")[0]

    qn = _rms_fwd(q_a, w['q_a_norm'], name="rms_q")
    kvn = _rms_fwd(kv_a, w['kv_a_norm'], name="rms_kv")
    q_raw = _mm(qn, w_qt, tb=True, name="mm_q")
    qx = _ew(_rope_heads(_rope, cfg.H), [q_raw, cos_t, sin_t], [], [(cfg.H * HEAD_SLOT, BF16)], name="rope_q")[0]
    kv = _mm(kvn, w_kvt, tb=True, out_dtype=BF16, name="mm_kv")
    kr = _ew(lambda rid, kp, cs, sn: _rope(kp, cs, sn), [k_pe, cos_t, sin_t], [], [(LANE, BF16)], name="rope_k")[0]
    o, lse = _attn_fwd(qx, kv, kr, cfg, name="attn_fwd")

    def norm2(rid, yav, ov, gs, ga):
        return jnp.concatenate([_rms_parts(yav, gs)[0] * gs, _rms_parts(ov, ga)[0] * ga], axis=1)

    up_landed = _split_wait(up_send, up_recv, up_flying, _allgather_ici_copies, o, name="allgather_up_wait")
    uf_send, uf_recv, uf_flying, uf_token = _split_start(up_landed, _allgather_forward_copies, 3,
                                                         name="allgather_up_forward_start")
    yn = _ew(norm2, [ya, o], [w['out_norm_ssm'] + uf_token[0:1, 0:1], w['out_norm_attn']], [(cfg.DMIX, BF16)],
             name="rms_out")[0]
    h1 = _mm(yn, w_out, res=h0, name="mm_out")
    xn2 = _rms_fwd(h1, w['ffn_norm'], name="rms_ffn")
    dn_landed = _split_wait(dn_send, dn_recv, dn_flying, _allgather_ici_copies, xn2, name="allgather_down_wait")
    df_send, df_recv, df_flying, df_token = _split_start(dn_landed, _allgather_forward_copies, 3,
                                                         name="allgather_down_forward_start")
    w_upt, = _split_wait(uf_send, uf_recv, uf_flying, _allgather_forward_copies, df_token,
                         name="allgather_up_forward_wait")
    up, act = _ffn_up(xn2, w_upt, conv_w, conv_b, name="ffn_up")
    w_down, = _split_wait(df_send, df_recv, df_flying, _allgather_forward_copies, act,
                          name="allgather_down_forward_wait")
    h2 = _mm(act, w_down, res=h1, tm=_tile(lp, 544, ROW_ALIGN), name="mm_down")

    g_final = w['final_norm'].reshape(1, d)

    def head(rid, hv, tv, gv):
        xhat, r = _rms_parts(hv, gv)
        valid = rid >= PAD + N_META
        diff = jnp.where(valid, xhat * gv - tv, 0.0)
        dout = diff * (1.0 / d)
        dxhat = dout * gv
        dx = r * (dxhat - xhat * jnp.mean(dxhat * xhat, axis=-1, keepdims=True))
        return dx, dx, dout * xhat, 0.5 * diff * dout

    dh2, dh2_b, dg_final, loss_cols = _ew(head, [h2, (loss_target[0], d, 0, SKIP)], [g_final], [(d, F32), (d, BF16)], [d, d],
                                          tm=PAD + N_META, name="loss_head")
    loss = lax.psum(jnp.sum(loss_cols), ("x", "y", "c"))

    dw_down = _mm(act, dh2_b, ta=True, tn=d, tm=512, out_dtype=BF16, name="mm_dw_down")

    def sibling_start(g, tag):
        land = lax.empty((4, g.shape[0] // 8, g.shape[1]), g.dtype)
        return _split_start([g, land], _rs_sibling_copies, 4, name=f"rs_sibling_{tag}_start")

    dn_send, dn_recv, dn_flying, dn_token = sibling_start(dw_down, "down")
    dup, dconv_w, dconv_b = _ffn_dact(dh2_b, w_down, up, conv_w, conv_b + dn_token[0:1, 0:1], name="ffn_dact")
    tk_up, tm_up = _tile(cfg.FP, 1408), _tile(cfg.FP, 512)
    dw_upt = _mm(dup, xn2, ta=True, dims=(2 * cfg.FP, d, lp), tn=d, tm=tm_up, a_lead=True, out_dtype=BF16, name="mm_dw_up",
                 a_idx=lambda i, j, k: (i // (cfg.FP // tm_up), 0, i % (cfg.FP // tm_up)))
    up_send, up_recv, up_flying, up_token = sibling_start(dw_upt, "up")
    dxn2 = _mm(dup, w_upt, dims=(lp, d, 2 * cfg.FP), tk=tk_up, tn=1024, a_lead=True, name="mm_dxn2",
               a_idx=lambda i, j, k: (k // (cfg.FP // tk_up), i, k % (cfg.FP // tk_up)))
    dh1, dh1_b, dg_ffn = _rms_bwd(h1, w['ffn_norm'] + up_token[0:1, 0:1], dxn2, res=dh2, mask=True, with_bf16=True,
                                  name="rms_ffn_bwd")

    dyn = _mm(dh1_b, w_out, tb=True, name="mm_dyn")
    dw_out = _mm(yn, dh1_b, ta=True, tn=d, tm=512, name="mm_dw_out")
    up_done = _split_wait(up_send, up_recv, up_flying, _rs_sibling_copies, dw_out, name="rs_sibling_up_wait")
    dn_done = _split_wait(dn_send, dn_recv, dn_flying, _rs_sibling_copies, dw_out, name="rs_sibling_down_wait")
    early_parts, early_sends = _add_halves_all([up_done[0], dn_done[0]], [up_done[1], dn_done[1]], [BF16] * 2, "early")
    chip_lands = [lax.empty((3,) + s.shape[1:], s.dtype) for s in early_sends]
    ch_send, ch_recv, ch_flying, ch_token = _split_start(early_sends + chip_lands, _rs_chips_copies, 6,
                                                         name="rs_chips_early_start")
    dya, dg_ssm = _rms_bwd(ya, w['out_norm_ssm'] + ch_token[0:1, 0:1], (dyn, ds, 0), name="rms_ssm_bwd")
    do, dg_attn = _rms_bwd(o, w['out_norm_attn'], (dyn, cfg.DATTN, ds // cfg.DATTN), name="rms_attn_bwd")

    dqx, dkv, dkr = _attn_bwd(qx, kv, kr, o, lse, do, cfg, name="attn_bwd")
    dq_raw = _ew(_rope_heads(_unrope, cfg.H), [dqx, cos_t, sin_t], [], [(cfg.H * HEAD_SLOT, BF16)], name="unrope_q")[0]
    dk_pe = _ew(lambda rid, dk, cs, sn: _unrope(dk, cs, sn), [dkr, cos_t, sin_t], [], [(LANE, F32)], name="unrope_k")[0]
    dqn = _mm(dq_raw, w_qt, name="mm_dqn")
    dw_qt = _mm(dq_raw, qn, ta=True, tm=512, name="mm_dw_q")
    dkvn = _mm(dkv, w_kvt, name="mm_dkvn")
    dw_kvt = _mm(dkv, kvn, ta=True, tm=512, name="mm_dw_kv")
    dq_a, dg_q = _rms_bwd(q_a, w['q_a_norm'], dqn, name="rms_q_bwd")
    dkv_a, dg_kv = _rms_bwd(kv_a, w['kv_a_norm'], dkvn, name="rms_kv_bwd")

    def glu_bwd(rid, ycv, uv, tv, dyav, dk, bg):
        gelu = jax.nn.gelu(s5_y(ycv, uv, dk))
        sg = jax.nn.sigmoid(tv + bg)
        dt = dyav * gelu * sg * (1.0 - sg)
        return dt, dyav * sg, dt

    dt_b, dgl1, db_glu = _ew(glu_bwd, [yc, u, tg, dya], [d_skip, b_glu], [(ds, BF16), (ds, F32)], [ds], name="s5_glu_bwd")
    dgl = _mm(dt_b, w_glu, tb=True, res=dgl1, name="mm_dgl")
    dw_glu = _mm(gl, dt_b, ta=True, tm=512, name="mm_dw_glu")

    def gelu_bwd(rid, ycv, uv, dglv, dk):
        _, vjp = jax.vjp(jax.nn.gelu, s5_y(ycv, uv, dk))
        dy = vjp(dglv)[0]
        return dy, dy * dk, dy * uv

    mid_grads = [dw_out, dw_glu, dw_qt, dw_kvt]
    mid_lands = [lax.empty((4, g.shape[0] // 8, g.shape[1]), g.dtype) for g in mid_grads]
    ms_send, ms_recv, ms_flying, ms_token = _split_start(mid_grads + mid_lands, _rs_sibling_copies, 4 * len(mid_grads),
                                                         name="rs_sibling_mid_start")
    dy_b, du_skip, dd_skip = _ew(gelu_bwd, [yc, u, dgl], [d_skip + ms_token[0:1, 0:1]], [(ds, BF16), (ds, F32)], [ds],
                                 name="s5_gelu_bwd")
    ms_done = _split_wait(ms_send, ms_recv, ms_flying, _rs_sibling_copies, dy_b, name="rs_sibling_mid_wait")
    mid_parts, mid_sends = _add_halves_all(ms_done[:4], ms_done[4:], [BF16] * 4, "mid")
    mid_chip_lands = [lax.empty((3,) + s.shape[1:], s.dtype) for s in mid_sends]
    mc_send, mc_recv, mc_flying, mc_token = _split_start(mid_sends + mid_chip_lands, _rs_chips_copies, 3 * len(mid_sends),
                                                         name="rs_chips_mid_start")
    du, dbb_band, dcc_band, da_l = _s5_bwd(dy_b, hs, z, bb_band, cc_band, pw_bwd + mc_token[0:1, 0:1], du_skip, cfg,
                                           name="s5_bwd")

    dz = jnp.concatenate([du, dq_a, dkv_a, dk_pe], axis=1).astype(BF16)
    dxn = _mm(dz, w_in, tb=True, name="mm_dxn")
    dw_in = _mm(xn, dz, ta=True, tm=512, tn=_tile(cfg.DINP, 1024), name="mm_dw_in")
    def mix_bwd(rid, xv, dyv, resv, gv):
        dx, dg = _rms_bwd_block(xv, gv, dyv)
        dx = dx + resv
        return dx, dx, dg

    grad_x, dh0_head, dg_mix = _ew(mix_bwd, [h0, dxn, dh1], [mix_norm], [(d, F32, SKIP), (d, F32, FIRST)], [d],
                                   tm=PAD + N_META, name="rms_mix_bwd")
    grad_x = grad_x[None]

    da_re, da_im = _gp_from_lanes(da_l, cfg)
    dbb_re, dbb_im = _bb_from_band(dbb_band, cfg)
    dlam_re, dlam_im, dlog_dt, db_re, db_im = s5_vjp((da_re, da_im, dbb_re, dbb_im))
    dc_re, dc_im = _cc_from_band(dcc_band, cfg)
    local_small = {
        'meta_tokens': dh0_head[PAD:], 'mix_norm': dg_mix, 'lam_re': dlam_re, 'lam_im': dlam_im, 'log_dt': dlog_dt,
        'b_re': db_re, 'b_im': db_im, 'c_re': dc_re, 'c_im': dc_im, 'd_skip': dd_skip, 'b_glu': db_glu, 'q_a_norm': dg_q,
        'kv_a_norm': dg_kv, 'out_norm_ssm': dg_ssm, 'out_norm_attn': dg_attn, 'ffn_norm': dg_ffn,
        'conv_w': _ff_unpad(dconv_w, cfg), 'conv_b': _ff_unpad(dconv_b, cfg), 'final_norm': dg_final,
    }
    small_shapes = [local_small[n].shape for n in SMALL]

    small_pack = _pack([local_small[n] for n in SMALL])
    ch_done = _split_wait(ch_send, ch_recv, ch_flying, _rs_chips_copies, small_pack, name="rs_chips_early_wait")
    early_halves = [_add_chips(p, b, name=f"rs_add_chips_early{t}") for t, (p, b) in enumerate(zip(early_parts, ch_done[2:]))]
    fe_send, fe_recv, fe_flying, fe_token = _split_start(early_halves, _rs_final_copies, 2, name="rs_final_early_start")
    end_local = [dw_in, small_pack + fe_token[0:1, 0:1]]
    end_recv = _rs_sibling(end_local, name="rs_sibling_end")
    end_parts, end_sends = _add_halves_all(end_local, end_recv, [BF16, F32], "end")
    end_lands = [lax.empty((3,) + s.shape[1:], s.dtype) for s in end_sends]
    ec_send, ec_recv, ec_flying, ec_token = _split_start(end_sends + end_lands, _rs_chips_copies, 3 * len(end_sends),
                                                         name="rs_chips_end_start")
    fe_done = _split_wait(fe_send, fe_recv, fe_flying, _rs_final_copies, ec_token, name="rs_final_early_wait")
    red_up, red_down = [f.reshape(-1, f.shape[-1]) for f in fe_done]

    delta, new_m, new_v, grads = {}, {}, {}, {}
    padded_rows = ('w_down',)

    def adamw_big(n, red):
        shp = w[n].shape
        w2, m2, v2 = [t.reshape(shp[-2], shp[-1]) for t in (w[n], m[n], v[n])]
        if n in padded_rows:
            g2, dl, mn, vn = _adamw(w2, red, m2, v2, emit_grad=True, name=f"adamw_{n}")
            grads[n] = g2.reshape(shp)
        else:
            grads[n] = _from_comm_layout(n, red, cfg)
            dl, mn, vn = _adamw(w2, grads[n].reshape(shp[-2], shp[-1]), m2, v2, name=f"adamw_{n}")
        delta[n], new_m[n], new_v[n] = dl.reshape(shp), mn.reshape(shp), vn.reshape(shp)

    def adamw_up(red):
        q = cfg.F // 4
        wt, mt, vt = [jnp.transpose(t[0]).reshape(2, q, d) for t in (w['w_up'], m['w_up'], v['w_up'])]
        outs = _adamw(wt, red.reshape(2, cfg.FQ, d), mt, vt, emit_grad=True, name="adamw_w_up")
        grads['w_up'], delta['w_up'], new_m['w_up'], new_v['w_up'] = [jnp.transpose(t.reshape(2 * q, d))[None] for t in outs]

    adamw_up(red_up)
    adamw_big('w_down', red_down)
    mc_done = _split_wait(mc_send, mc_recv, mc_flying, _rs_chips_copies, delta['w_down'], name="rs_chips_mid_wait")
    ec_done = _split_wait(ec_send, ec_recv, ec_flying, _rs_chips_copies, mc_done[0], name="rs_chips_end_wait")
    red = _rs_finish(mid_parts + end_parts, list(mc_done[len(mid_sends):]) + list(ec_done[len(end_sends):]), "rest")
    small_full = _allgather([_place_shard(red[5], F32, name="place_small")], name="allgather_small")[0]
    small_sum = dict(zip(SMALL, _unpack(small_full, small_shapes)))
    for n, r in zip(['w_out', 'w_glu', 'w_q_b', 'w_kv_b'], red[:4]):
        adamw_big(n, r)
    in_t = [jnp.transpose(t[0]) for t in (w['w_in'], m['w_in'], v['w_in'])]
    outs = _adamw(in_t[0], jnp.transpose(red[4][:, :cfg.DIN]), in_t[1], in_t[2], emit_grad=True, name="adamw_w_in")
    grads['w_in'], delta['w_in'], new_m['w_in'], new_v['w_in'] = [jnp.transpose(t)[None] for t in outs]

    for n in SMALL:
        g = small_sum[n]
        if n == 'meta_tokens':
            g = lax.dynamic_slice_in_dim(g, me * (d // 4), d // 4, axis=1)
        elif n == 'conv_w':
            g = lax.dynamic_slice_in_dim(g, me * (cfg.F // 4), cfg.F // 4, axis=1)[None]
        else:
            g = g.reshape(w[n].shape)
        grads[n] = g

    shapes = [w[n].shape for n in SMALL]
    packs = [_pack([src[n] for n in SMALL]) for src in (w, grads, m, v)]
    for dst, p in zip((delta, new_m, new_v), _adamw(*packs, name="adamw_small")):
        dst.update(zip(SMALL, _unpack(p, shapes)))

    return (loss, grad_x, *[grads[n] for n in WEIGHTS], *[delta[n] for n in WEIGHTS],
            *[new_m[n] for n in WEIGHTS], *[new_v[n] for n in WEIGHTS])


def kernel(x, meta_tokens, mix_norm, w_in, lam_re, lam_im, log_dt, b_re, b_im, c_re, c_im, d_skip, w_glu, b_glu, q_a_norm, w_q_b, kv_a_norm, w_kv_b, out_norm_ssm, out_norm_attn, w_out, ffn_norm, w_up, conv_w, conv_b, w_down, final_norm, loss_target, m_meta_tokens, m_mix_norm, m_w_in, m_lam_re, m_lam_im, m_log_dt, m_b_re, m_b_im, m_c_re, m_c_im, m_d_skip, m_w_glu, m_b_glu, m_q_a_norm, m_w_q_b, m_kv_a_norm, m_w_kv_b, m_out_norm_ssm, m_out_norm_attn, m_w_out, m_ffn_norm, m_w_up, m_conv_w, m_conv_b, m_w_down, m_final_norm, v_meta_tokens, v_mix_norm, v_w_in, v_lam_re, v_lam_im, v_log_dt, v_b_re, v_b_im, v_c_re, v_c_im, v_d_skip, v_w_glu, v_b_glu, v_q_a_norm, v_w_q_b, v_kv_a_norm, v_w_kv_b, v_out_norm_ssm, v_out_norm_attn, v_w_out, v_ffn_norm, v_w_up, v_conv_w, v_conv_b, v_w_down, v_final_norm):
    args = dict(locals())
    w = {n: args[n] for n in WEIGHTS}
    m = {n: args["m_" + n] for n in WEIGHTS}
    v = {n: args["v_" + n] for n in WEIGHTS}
    return _step(PROD, w, m, v, x, loss_target)
```

```python
import functools
import math
from typing import NamedTuple

import jax
import jax.numpy as jnp
from jax import lax
from jax.experimental import pallas as pl
from jax.experimental.pallas import tpu as pltpu

F32, BF16 = jnp.float32, jnp.bfloat16
MESH = pl.DeviceIdType.MESH
LANE = 128
ROW_ALIGN = 16
N_META = 16
PAD = 112
CHUNK = 64
SSM_GROUP = 16
SSM_STATE = 64
GROUPS_PER_BLOCK = 8
QK_NOPE, QK_ROPE, V_HEAD = 128, 64, 128
HEAD_SLOT = 256
ROPE_BASE = 10000.0
EPS = 1e-6
ADAM_LR, ADAM_B1, ADAM_B2, ADAM_EPS, ADAM_WD, ADAM_STEP = 0.001, 0.9, 0.999, 1e-08, 0.01, 10
DT_F32_BLOCK_BYTES = 9 << 18
ADAMW_BLOCK_BYTES = 3 << 19
PLACE_BLOCK_BYTES = 6 << 20
SKIP, FIRST = "skip", "first"


class Cfg(NamedTuple):
    D: int
    S: int
    DS: int
    H: int
    QL: int
    KVL: int
    F: int

    @property
    def LP(self):
        return PAD + N_META + self.S

    @property
    def G(self):
        return self.DS // SSM_GROUP

    @property
    def NB(self):
        return self.G // GROUPS_PER_BLOCK

    @property
    def NL(self):
        return 2 * self.G * SSM_STATE

    @property
    def DATTN(self):
        return self.H * V_HEAD

    @property
    def DMIX(self):
        return self.DS + self.DATTN

    @property
    def DIN(self):
        return self.DS + self.QL + self.KVL + QK_ROPE

    @property
    def DINP(self):
        return self.DS + self.QL + self.KVL + LANE

    @property
    def FQ(self):
        return -(-(self.F // 4) // LANE) * LANE

    @property
    def FP(self):
        return 4 * self.FQ


PROD = Cfg(D=2048, S=2048, DS=1024, H=8, QL=512, KVL=256, F=5504)

WEIGHTS = ['meta_tokens', 'mix_norm', 'w_in', 'lam_re', 'lam_im', 'log_dt', 'b_re', 'b_im', 'c_re', 'c_im', 'd_skip',
           'w_glu', 'b_glu', 'q_a_norm', 'w_q_b', 'kv_a_norm', 'w_kv_b', 'out_norm_ssm', 'out_norm_attn', 'w_out',
           'ffn_norm', 'w_up', 'conv_w', 'conv_b', 'w_down', 'final_norm']
BIG = ['w_in', 'w_glu', 'w_q_b', 'w_kv_b', 'w_out', 'w_up', 'w_down']
SMALL = [n for n in WEIGHTS if n not in BIG]


def _pc(body, **kw):
    return pl.pallas_call(body, **kw)


def _tile(n, target, align=LANE):
    best = None
    d = align
    while d <= min(n, target):
        if n % d == 0:
            best = d
        d += align
    return best if best is not None else n


def _row_tile(rows, cols):
    return _tile(rows, max(ROW_ALIGN, DT_F32_BLOCK_BYTES // (4 * cols)), ROW_ALIGN)


def _mm(a, b, *, name, ta=False, tb=False, tm=None, tn=512, tk=None, out_dtype=F32, res=None,
        a_idx=None, b_idx=None, dims=None, a_lead=False):
    if dims is None:
        m, k = (a.shape[1], a.shape[0]) if ta else a.shape
        n = b.shape[0] if tb else b.shape[1]
    else:
        m, n, k = dims
    tm = _tile(m, tm or m, LANE if ta else ROW_ALIGN)
    tn = _tile(n, tn)
    tk = _tile(k, tk or k, ROW_ALIGN if (ta and not tb) else LANE)
    nm, nn, nk = m // tm, n // tn, k // tk
    a_idx = a_idx or ((lambda i, j, kk: (kk, i)) if ta else (lambda i, j, kk: (i, kk)))
    b_idx = b_idx or ((lambda i, j, kk: (j, kk)) if tb else (lambda i, j, kk: (kk, j)))
    dn = (((0 if ta else 1,), (1 if tb else 0,)), ((), ()))

    def body(*refs):
        a_ref, b_ref = refs[0], refs[1]
        r_ref = refs[2] if res is not None else None
        o_ref = refs[3] if res is not None else refs[2]
        d = lax.dot_general(a_ref[...].astype(BF16), b_ref[...].astype(BF16), dn, preferred_element_type=F32)

        def finish(r):
            if r_ref is not None:
                r = r + r_ref[...].astype(F32)
            o_ref[...] = r.astype(out_dtype)

        if nk == 1:
            finish(d)
        else:
            acc = refs[-1]
            kk = pl.program_id(2)

            @pl.when(kk == 0)
            def _():
                acc[...] = d

            @pl.when(kk > 0)
            def _():
                acc[...] += d

            @pl.when(kk == nk - 1)
            def _():
                finish(acc[...])

    a_blk = ((None,) if a_lead else ()) + ((tk, tm) if ta else (tm, tk))
    in_specs = [pl.BlockSpec(a_blk, a_idx), pl.BlockSpec((tn, tk) if tb else (tk, tn), b_idx)]
    args = [a, b]
    if res is not None:
        in_specs.append(pl.BlockSpec((tm, tn), lambda i, j, kk: (i, j)))
        args.append(res)
    return _pc(body, name=name, grid=(nm, nn, nk), in_specs=in_specs,
               out_specs=pl.BlockSpec((tm, tn), lambda i, j, kk: (i, j)),
               out_shape=jax.ShapeDtypeStruct((m, n), out_dtype),
               scratch_shapes=[pltpu.VMEM((tm, tn), F32)] if nk > 1 else [],
               compiler_params=pltpu.CompilerParams(dimension_semantics=("parallel", "parallel", "arbitrary")))(*args)


def _ew(fn, ins, vecs, outs, sums=(), *, name, tm=None):
    ins = [x if isinstance(x, tuple) else (x, x.shape[1], 0) for x in ins]
    ins = [x if len(x) == 4 else x + (None,) for x in ins]
    outs = [o if len(o) == 3 else o + (None,) for o in outs]
    rows = ins[0][0].shape[0]
    cmax = max([c for _, c, _, _ in ins] + [c for c, _, _ in outs])
    tm = tm or _row_tile(rows, cmax)
    n_in, n_vec, n_out, n_sum = len(ins), len(vecs), len(outs), len(sums)

    def body(*refs):
        i = pl.program_id(0)
        rid = i * tm + lax.broadcasted_iota(jnp.int32, (tm, 1), 0)
        vals = [r[...] for r in refs[:n_in + n_vec]]
        res = fn(rid, *vals)
        res = res if isinstance(res, (tuple, list)) else (res,)
        o_refs = refs[n_in + n_vec:]
        for o_ref, r, (_, _, mode) in zip(o_refs[:n_out], res[:n_out], outs):
            if mode == FIRST:
                @pl.when(i == 0)
                def _():
                    o_ref[...] = r.astype(o_ref.dtype)
            else:
                o_ref[...] = r.astype(o_ref.dtype)
        for o_ref, r in zip(o_refs[n_out:], res[n_out:]):
            part = jnp.sum(r.astype(F32), axis=0, keepdims=True)

            @pl.when(i == 0)
            def _():
                o_ref[...] = part

            @pl.when(i > 0)
            def _():
                o_ref[...] += part

    def row_idx(mode):
        if mode == SKIP:
            return lambda i, cb=0: (jnp.maximum(i - 1, 0), cb)
        if mode == FIRST:
            return lambda i, cb=0: (0, cb)
        return lambda i, cb=0: (i, cb)

    in_specs = [pl.BlockSpec((tm, c), functools.partial(row_idx(mode), cb=cb)) for _, c, cb, mode in ins]
    in_specs += [pl.BlockSpec(v.shape, functools.partial(lambda i, nd: (0,) * nd, nd=v.ndim)) for v in vecs]
    out_specs = [pl.BlockSpec((tm, c), row_idx(mode)) for c, _, mode in outs]
    out_specs += [pl.BlockSpec((1, c), lambda i: (0, 0)) for c in sums]
    out_rows = {None: rows, SKIP: rows - tm, FIRST: tm}
    out_shape = [jax.ShapeDtypeStruct((out_rows[mode], c), dt) for c, dt, mode in outs]
    out_shape += [jax.ShapeDtypeStruct((1, c), F32) for c in sums]
    return _pc(body, name=name, grid=(rows // tm,), in_specs=in_specs, out_specs=out_specs, out_shape=out_shape,
               compiler_params=pltpu.CompilerParams(dimension_semantics=("arbitrary",)))(*[x[0] for x in ins], *vecs)


def _rms_parts(x, g):
    r = lax.rsqrt(jnp.mean(x * x, axis=-1, keepdims=True) + EPS)
    return x * r, r


def _rms_bwd_block(x, g, dy):
    xhat, r = _rms_parts(x, g)
    dxhat = dy * g
    dx = r * (dxhat - xhat * jnp.mean(dxhat * xhat, axis=-1, keepdims=True))
    return dx, dy * xhat


def _rms_fwd(x, g, *, name):
    c = x[1] if isinstance(x, tuple) else x.shape[1]
    return _ew(lambda rid, xv, gv: _rms_parts(xv.astype(F32), gv)[0] * gv, [x], [g], [(c, BF16)], name=name)[0]


def _rms_bwd(x, g, dy, *, name, res=None, mask=False, with_bf16=False):
    c = x[1] if isinstance(x, tuple) else x.shape[1]

    def fn(rid, xv, dyv, *rest):
        gv = rest[-1]
        dx, dg = _rms_bwd_block(xv.astype(F32), gv, dyv.astype(F32))
        if res is not None:
            dx = dx + rest[0]
        if mask:
            dx = jnp.where(rid >= PAD, dx, 0.0)
        return (dx, dx, dg) if with_bf16 else (dx, dg)

    ins = [x, dy] + ([res] if res is not None else [])
    outs = [(c, F32)] + ([(c, BF16)] if with_bf16 else [])
    return _ew(fn, ins, [g], outs, [c], name=name)


S5_W = GROUPS_PER_BLOCK * SSM_STATE
S5_GW = GROUPS_PER_BLOCK * SSM_GROUP
S5_UNROLL = 8
S5_DA_ROWS = 272


def _s5_scan_in_place(ref, pw_ref, *, reverse):
    lp = ref.shape[0]
    tile_rows = 8
    chunk = _tile(lp, S5_DA_ROWS, tile_rows)
    tiles = chunk // tile_rows

    def chunk_body(c, carry):
        rows = pl.ds(pl.multiple_of(c * chunk, tile_rows), chunk)
        xr, xi = ref[rows, :S5_W], ref[rows, S5_W:]
        for level, k in enumerate((1, 2, 4)):
            base = tile_rows * (1 + level)
            mr, mi = pw_ref[base:base + tile_rows, :S5_W][None], pw_ref[base:base + tile_rows, S5_W:][None]
            shift = chunk - k if reverse else k
            sr = pltpu.roll(xr, shift, 0).reshape(tiles, tile_rows, S5_W)
            si = pltpu.roll(xi, shift, 0).reshape(tiles, tile_rows, S5_W)
            xr = xr + (mr * sr - mi * si).reshape(chunk, S5_W)
            xi = xi + (mr * si + mi * sr).reshape(chunk, S5_W)
        ref[rows, :S5_W] = xr
        ref[rows, S5_W:] = xi
        return carry

    lax.fori_loop(0, lp // chunk, chunk_body, 0)

    pr, pi = pw_ref[0:tile_rows, :S5_W], pw_ref[0:tile_rows, S5_W:]
    ntile = lp // tile_rows
    unroll = 4

    def step(n, carry):
        cr, ci = carry
        for q in range(unroll):
            j = n * unroll + q
            j = ntile - 1 - j if reverse else j
            rows = pl.ds(pl.multiple_of(j * tile_rows, tile_rows), tile_rows)
            nr = ref[rows, :S5_W] + (pr * cr - pi * ci)
            ni = ref[rows, S5_W:] + (pr * ci + pi * cr)
            ref[rows, :S5_W] = nr
            ref[rows, S5_W:] = ni
            cr, ci = (nr[0:1], ni[0:1]) if reverse else (nr[tile_rows - 1:], ni[tile_rows - 1:])
        return cr, ci

    z = jnp.zeros((1, S5_W), F32)
    lax.fori_loop(0, ntile // unroll, step, (z, z))


def _s5_fwd(z, bb_band, cc_band, a_l, cfg, *, name):
    lp, ds, nl = cfg.LP, cfg.DS, cfg.NL

    def body(u_ref, bb_ref, cc_ref, a_ref, hs_ref, y_ref):
        hs_ref[...] = jnp.dot(u_ref[...].astype(BF16), bb_ref[...], preferred_element_type=F32)
        _s5_scan_in_place(hs_ref, a_ref, reverse=False)
        y_ref[...] = jnp.dot(hs_ref[...].astype(BF16), cc_ref[...], preferred_element_type=F32)

    return _pc(body, name=name, grid=(cfg.NB,),
               in_specs=[pl.BlockSpec((lp, S5_GW), lambda j: (0, j)), pl.BlockSpec((S5_GW, 2 * S5_W), lambda j: (j, 0)),
                         pl.BlockSpec((2 * S5_W, S5_GW), lambda j: (j, 0)), pl.BlockSpec((32, 2 * S5_W), lambda j: (0, j))],
               out_specs=[pl.BlockSpec((lp, 2 * S5_W), lambda j: (0, j)), pl.BlockSpec((lp, S5_GW), lambda j: (0, j))],
               out_shape=[jax.ShapeDtypeStruct((lp, nl), F32), jax.ShapeDtypeStruct((lp, ds), F32)],
               compiler_params=pltpu.CompilerParams(dimension_semantics=("parallel",)))(z, bb_band, cc_band, a_l)


def _s5_bwd(dy, hs, z, bb_band, cc_band, a_l, du_skip, cfg, *, name):
    lp, ds, nl = cfg.LP, cfg.DS, cfg.NL
    nt = (((1,), (1,)), ((), ()))
    tn = (((0,), (0,)), ((), ()))

    def body(dy_ref, hs_ref, u_ref, bb_ref, cc_ref, a_ref, sk_ref, du_ref, dbb_ref, dcc_ref, da_ref, g_ref):
        dyv = dy_ref[...]
        g_ref[...] = lax.dot_general(dyv, cc_ref[...], nt, preferred_element_type=F32)
        _s5_scan_in_place(g_ref, a_ref, reverse=True)
        dcc_ref[...] = lax.dot_general(hs_ref[...].astype(BF16), dyv, tn, preferred_element_type=F32)
        gb = g_ref[...].astype(BF16)
        dbb_ref[...] = lax.dot_general(u_ref[...].astype(BF16), gb, tn, preferred_element_type=F32)
        du_ref[...] = lax.dot_general(gb, bb_ref[...], nt, preferred_element_type=F32) + sk_ref[...]
        dre = jnp.zeros((1, S5_W), F32)
        dim = jnp.zeros((1, S5_W), F32)
        for r0 in range(0, lp, S5_DA_ROWS):
            rows = min(S5_DA_ROWS, lp - r0)
            first = lax.broadcasted_iota(jnp.int32, (rows, 1), 0) == 0
            prev = hs_ref[r0 - 1:r0, :] if r0 else jnp.zeros((1, 2 * S5_W), F32)
            hr = jnp.where(first, prev[:, :S5_W], pltpu.roll(hs_ref[r0:r0 + rows, :S5_W], 1, 0))
            hi = jnp.where(first, prev[:, S5_W:], pltpu.roll(hs_ref[r0:r0 + rows, S5_W:], 1, 0))
            gr, gi = g_ref[r0:r0 + rows, :S5_W], g_ref[r0:r0 + rows, S5_W:]
            dre = dre + jnp.sum(gr * hr + gi * hi, axis=0, keepdims=True)
            dim = dim + jnp.sum(gi * hr - gr * hi, axis=0, keepdims=True)
        da_ref[:, :S5_W] = dre
        da_ref[:, S5_W:] = dim

    col_blk = pl.BlockSpec((lp, S5_GW), lambda j: (0, j))
    lane_blk = pl.BlockSpec((lp, 2 * S5_W), lambda j: (0, j))
    bb_blk = pl.BlockSpec((S5_GW, 2 * S5_W), lambda j: (j, 0))
    cc_blk = pl.BlockSpec((2 * S5_W, S5_GW), lambda j: (j, 0))
    a_blk = pl.BlockSpec((1, 2 * S5_W), lambda j: (0, j))
    pw_blk = pl.BlockSpec((32, 2 * S5_W), lambda j: (0, j))
    return _pc(body, name=name, grid=(cfg.NB,),
               in_specs=[col_blk, lane_blk, col_blk, bb_blk, cc_blk, pw_blk, col_blk],
               out_specs=[col_blk, bb_blk, cc_blk, a_blk],
               out_shape=[jax.ShapeDtypeStruct((lp, ds), F32), jax.ShapeDtypeStruct((ds, 2 * S5_W), F32),
                          jax.ShapeDtypeStruct((nl, S5_GW), F32), jax.ShapeDtypeStruct((1, nl), F32)],
               scratch_shapes=[pltpu.VMEM((lp, 2 * S5_W), F32)],
               compiler_params=pltpu.CompilerParams(dimension_semantics=("parallel",)))(dy, hs, z, bb_band, cc_band, a_l, du_skip)


def _conv_gate(pre, cw, cb):
    return cw[0:1] * pltpu.roll(pre, 2, 0) + cw[1:2] * pltpu.roll(pre, 1, 0) + cw[2:3] * pre + cb


def _ffn_up(xn2, w_upt, cw, cb, *, name):
    lp, d = xn2.shape
    fp = w_upt.shape[0] // 2
    tc = _tile(fp, 256)
    nb = fp // tc

    def body(x_ref, wg_ref, wv_ref, cw_ref, cb_ref, up_ref, act_ref):
        wcat = jnp.concatenate([wg_ref[...], wv_ref[...]], axis=0)
        r = lax.dot_general(x_ref[...], wcat, (((1,), (1,)), ((), ())), preferred_element_type=F32)
        pre, val = r[:, :tc].astype(BF16), r[:, tc:].astype(BF16)
        up_ref[0] = pre
        up_ref[1] = val
        gate = _conv_gate(pre.astype(F32), cw_ref[...], cb_ref[...])
        act_ref[...] = (jax.nn.silu(gate) * val.astype(F32)).astype(BF16)

    return _pc(body, name=name, grid=(nb,),
               in_specs=[pl.BlockSpec((lp, d), lambda j: (0, 0)), pl.BlockSpec((tc, d), lambda j: (j, 0)),
                         pl.BlockSpec((tc, d), lambda j: (nb + j, 0)),
                         pl.BlockSpec((3, tc), lambda j: (0, j)), pl.BlockSpec((1, tc), lambda j: (0, j))],
               out_specs=[pl.BlockSpec((2, lp, tc), lambda j: (0, 0, j)), pl.BlockSpec((lp, tc), lambda j: (0, j))],
               out_shape=[jax.ShapeDtypeStruct((2, lp, fp), BF16), jax.ShapeDtypeStruct((lp, fp), BF16)],
               compiler_params=pltpu.CompilerParams(dimension_semantics=("parallel",)))(xn2, w_upt, w_upt, cw, cb)


def _ffn_dact(dh2, w_down, up, cw, cb, *, name):
    lp, d = dh2.shape
    fp = w_down.shape[0]
    tc = _tile(fp, 256)
    nb = fp // tc

    def body(dh_ref, wd_ref, up_ref, cw_ref, cb_ref, dup_ref, dcw_ref, dcb_ref):
        da = lax.dot_general(dh_ref[...], wd_ref[...], (((1,), (1,)), ((), ())), preferred_element_type=F32)
        pre, val, cwv = up_ref[0].astype(F32), up_ref[1].astype(F32), cw_ref[...]
        gate = _conv_gate(pre, cwv, cb_ref[...])
        sg = jax.nn.sigmoid(gate)
        dup_ref[1] = (da * (gate * sg)).astype(BF16)
        dgate = da * val * (sg * (1.0 + gate * (1.0 - sg)))
        dpre = cwv[2:3] * dgate + cwv[1:2] * pltpu.roll(dgate, lp - 1, 0) + cwv[0:1] * pltpu.roll(dgate, lp - 2, 0)
        dup_ref[0] = dpre.astype(BF16)
        dcb_ref[...] = jnp.sum(dgate, axis=0, keepdims=True)
        dcw_ref[0:1, :] = jnp.sum(dgate * pltpu.roll(pre, 2, 0), axis=0, keepdims=True)
        dcw_ref[1:2, :] = jnp.sum(dgate * pltpu.roll(pre, 1, 0), axis=0, keepdims=True)
        dcw_ref[2:3, :] = jnp.sum(dgate * pre, axis=0, keepdims=True)

    return _pc(body, name=name, grid=(nb,),
               in_specs=[pl.BlockSpec((lp, d), lambda j: (0, 0)), pl.BlockSpec((tc, d), lambda j: (j, 0)),
                         pl.BlockSpec((2, lp, tc), lambda j: (0, 0, j)),
                         pl.BlockSpec((3, tc), lambda j: (0, j)), pl.BlockSpec((1, tc), lambda j: (0, j))],
               out_specs=[pl.BlockSpec((2, lp, tc), lambda j: (0, 0, j)),
                          pl.BlockSpec((3, tc), lambda j: (0, j)), pl.BlockSpec((1, tc), lambda j: (0, j))],
               out_shape=[jax.ShapeDtypeStruct((2, lp, fp), BF16), jax.ShapeDtypeStruct((3, fp), F32),
                          jax.ShapeDtypeStruct((1, fp), F32)],
               compiler_params=pltpu.CompilerParams(dimension_semantics=("parallel",)))(dh2, w_down, up, cw, cb)


ATTN_Q_ROWS = 544


def _key_limit(i, tq, lp):
    return min(lp, -(-((i + 1) * tq) // LANE) * LANE)


def _attn_mask(i, tq, nk):
    qrow = i * tq + lax.broadcasted_iota(jnp.int32, (tq, 1), 0)
    krow = lax.broadcasted_iota(jnp.int32, (1, nk), 1)
    return (krow >= PAD) & ((krow // CHUNK) <= (qrow // CHUNK)), qrow >= PAD


def _attn_scores(q, kn, kr, i, tq, scale):
    nt = (((1,), (1,)), ((), ()))
    s = lax.dot_general(q[:, :QK_NOPE], kn, nt, preferred_element_type=F32)
    s = s + lax.dot_general(q[:, QK_NOPE:], kr, nt, preferred_element_type=F32)
    mask, qvalid = _attn_mask(i, tq, kn.shape[0])
    return jnp.where(mask, s * scale, jnp.finfo(F32).min), qvalid


def _per_q_block(nq, fn):
    i = pl.program_id(1)
    for blk in range(nq):
        pl.when(i == blk)(functools.partial(fn, blk))


def _attn_fwd(qx, kv, kr, cfg, *, name):
    lp, h = cfg.LP, cfg.H
    tq = _tile(lp, ATTN_Q_ROWS, ROW_ALIGN)
    nq = lp // tq
    scale = 1.0 / math.sqrt(QK_NOPE + QK_ROPE)

    def body(q_ref, kn_ref, v_ref, kr_ref, o_ref, lse_ref):
        def block(blk):
            nk = _key_limit(blk, tq, lp)
            s, qvalid = _attn_scores(q_ref[...], kn_ref[:nk], kr_ref[:nk], blk, tq, scale)
            m = jnp.max(s, axis=-1, keepdims=True)
            p = jnp.exp(s - m)
            l = jnp.sum(p, axis=-1, keepdims=True)
            o = jnp.dot(p.astype(BF16), v_ref[:nk], preferred_element_type=F32) / l
            o_ref[...] = jnp.where(qvalid, o, 0.0)
            lse_ref[...] = m + jnp.log(l)

        _per_q_block(nq, block)

    return _pc(body, name=name, grid=(h, nq),
               in_specs=[pl.BlockSpec((tq, HEAD_SLOT), lambda hh, i: (i, hh)),
                         pl.BlockSpec((lp, QK_NOPE), lambda hh, i: (0, 2 * hh)),
                         pl.BlockSpec((lp, V_HEAD), lambda hh, i: (0, 2 * hh + 1)),
                         pl.BlockSpec((lp, LANE), lambda hh, i: (0, 0))],
               out_specs=[pl.BlockSpec((tq, V_HEAD), lambda hh, i: (i, hh)),
                          pl.BlockSpec((None, tq, 1), lambda hh, i: (hh, i, 0))],
               out_shape=[jax.ShapeDtypeStruct((lp, h * V_HEAD), F32), jax.ShapeDtypeStruct((h, lp, 1), F32)],
               compiler_params=pltpu.CompilerParams(dimension_semantics=("parallel", "parallel")))(qx, kv, kv, kr)


def _attn_bwd(qx, kv, kr, o, lse, do, cfg, *, name):
    lp, h = cfg.LP, cfg.H
    tq = _tile(lp, ATTN_Q_ROWS, ROW_ALIGN)
    nq = lp // tq
    scale = 1.0 / math.sqrt(QK_NOPE + QK_ROPE)
    tn_dims = (((0,), (0,)), ((), ()))

    def body(q_ref, kn_ref, v_ref, kr_ref, o_ref, lse_ref, do_ref, dq_ref, dkv_ref, dkr_ref, dkv_acc):
        hh, i = pl.program_id(0), pl.program_id(1)

        @pl.when(i == 0)
        def _():
            dkv_acc[...] = jnp.zeros_like(dkv_acc)

        @pl.when((i == 0) & (hh == 0))
        def _():
            dkr_ref[...] = jnp.zeros_like(dkr_ref)

        def block(blk):
            nk = _key_limit(blk, tq, lp)
            q, kn, v, krv = q_ref[...], kn_ref[:nk], v_ref[:nk], kr_ref[:nk]
            s, qvalid = _attn_scores(q, kn, krv, blk, tq, scale)
            dov = jnp.where(qvalid, do_ref[...], 0.0)
            p = jnp.exp(s - lse_ref[...])
            delta = jnp.sum(dov * o_ref[...], axis=-1, keepdims=True)
            dob = dov.astype(BF16)
            dp = lax.dot_general(dob, v, (((1,), (1,)), ((), ())), preferred_element_type=F32)
            ds = (p * (dp - delta) * scale).astype(BF16)
            dq_ref[:, :QK_NOPE] = jnp.dot(ds, kn, preferred_element_type=F32)
            dq_ref[:, QK_NOPE:] = jnp.dot(ds, krv, preferred_element_type=F32)
            dkv_acc[:nk, :QK_NOPE] += lax.dot_general(ds, q[:, :QK_NOPE], tn_dims, preferred_element_type=F32)
            dkv_acc[:nk, QK_NOPE:] += lax.dot_general(p.astype(BF16), dob, tn_dims, preferred_element_type=F32)
            dkr_ref[:nk, :] += lax.dot_general(ds, q[:, QK_NOPE:], tn_dims, preferred_element_type=F32)

        _per_q_block(nq, block)

        @pl.when(i == nq - 1)
        def _():
            dkv_ref[...] = dkv_acc[...].astype(BF16)

    return _pc(body, name=name, grid=(h, nq),
               in_specs=[pl.BlockSpec((tq, HEAD_SLOT), lambda hh, i: (i, hh)),
                         pl.BlockSpec((lp, QK_NOPE), lambda hh, i: (0, 2 * hh)),
                         pl.BlockSpec((lp, V_HEAD), lambda hh, i: (0, 2 * hh + 1)),
                         pl.BlockSpec((lp, LANE), lambda hh, i: (0, 0)),
                         pl.BlockSpec((tq, V_HEAD), lambda hh, i: (i, hh)),
                         pl.BlockSpec((None, tq, 1), lambda hh, i: (hh, i, 0)),
                         pl.BlockSpec((tq, V_HEAD), lambda hh, i: (i, hh))],
               out_specs=[pl.BlockSpec((tq, HEAD_SLOT), lambda hh, i: (i, hh)),
                          pl.BlockSpec((lp, QK_NOPE + V_HEAD), lambda hh, i: (0, hh)),
                          pl.BlockSpec((lp, LANE), lambda hh, i: (0, 0))],
               out_shape=[jax.ShapeDtypeStruct((lp, h * HEAD_SLOT), F32),
                          jax.ShapeDtypeStruct((lp, h * (QK_NOPE + V_HEAD)), BF16),
                          jax.ShapeDtypeStruct((lp, LANE), F32)],
               scratch_shapes=[pltpu.VMEM((lp, QK_NOPE + V_HEAD), F32)],
               compiler_params=pltpu.CompilerParams(dimension_semantics=("arbitrary", "arbitrary")))(qx, kv, kv, kr, o, lse, do)


def _rot_half(x):
    lane = lax.broadcasted_iota(jnp.int32, x.shape, 1)
    half = QK_ROPE // 2
    return jnp.where(lane < half, -pltpu.roll(x, LANE - half, 1), pltpu.roll(x, half, 1))


def _rope(x, cos, sin):
    return x * cos + _rot_half(x) * sin


def _unrope(dy, cos, sin):
    return dy * cos - _rot_half(dy * sin)


def _rope_heads(fn, h):
    def apply(rid, q, cos, sin):
        parts = []
        for hh in range(h):
            parts.append(q[:, hh * HEAD_SLOT: hh * HEAD_SLOT + QK_NOPE])
            parts.append(fn(q[:, hh * HEAD_SLOT + QK_NOPE: (hh + 1) * HEAD_SLOT], cos, sin))
        return jnp.concatenate(parts, axis=1)
    return apply


ANY = pl.BlockSpec(memory_space=pl.ANY)


def _place():
    x, y, c = lax.axis_index("x"), lax.axis_index("y"), lax.axis_index("c")
    chips = [(1 - x, y), (x, 1 - y), (1 - x, 1 - y)]
    return x, y, c, chips


def _rcopy(src, dst, send_sem, recv_sem, dev):
    return pltpu.make_async_remote_copy(src_ref=src, dst_ref=dst, send_sem=send_sem, recv_sem=recv_sem,
                                        device_id=dev, device_id_type=MESH)


def _place_shard(shard, dtype, *, name, order=None, rows_to=None):
    shard = shard if shard.ndim == 3 else shard[None]
    n, r, cols = shard.shape
    rp = rows_to or r
    tm = _tile(r, max(ROW_ALIGN, PLACE_BLOCK_BYTES // (4 * cols)), ROW_ALIGN)
    me = (2 * lax.axis_index("x") + lax.axis_index("y")).astype(jnp.int32).reshape(1)
    extra = [] if order is None else [order]

    def body(me_ref, s_ref, *rest):
        rest[-1][...] = s_ref[...].astype(dtype)

    full = _pc(body, name=name,
               grid_spec=pltpu.PrefetchScalarGridSpec(
                   num_scalar_prefetch=1, grid=(n, r // tm),
                   in_specs=[pl.BlockSpec((None, tm, cols), lambda q, i, mr: (q, i, 0))] + [ANY] * len(extra),
                   out_specs=pl.BlockSpec((None, tm, cols), lambda q, i, mr: (mr[0] * n + q, i, 0))),
               out_shape=jax.ShapeDtypeStruct((4 * n, rp, cols), dtype),
               compiler_params=pltpu.CompilerParams(dimension_semantics=("arbitrary", "arbitrary")))(me, shard, *extra)
    if rp > r:
        pad = rp - r
        assert r % pad == 0

        def zero(me_ref, f_ref, o_ref):
            o_ref[...] = jnp.zeros_like(o_ref)

        full = _pc(zero, name=name + "_pad",
                   grid_spec=pltpu.PrefetchScalarGridSpec(
                       num_scalar_prefetch=1, grid=(n,), in_specs=[ANY],
                       out_specs=pl.BlockSpec((None, pad, cols), lambda q, mr: (mr[0] * n + q, r // pad, 0))),
                   out_shape=jax.ShapeDtypeStruct(full.shape, dtype), input_output_aliases={1: 0},
                   compiler_params=pltpu.CompilerParams(dimension_semantics=("arbitrary",)))(me, full)
    return full.reshape(4 * n * rp, cols)


def _allgather(fulls, *, name):
    n = len(fulls)

    def body(*refs):
        outs = refs[n:2 * n]
        send_sems, recv_sems = refs[2 * n:]
        x, y, c, chips = _place()
        sib = (x, y, 1 - c)
        me = 2 * x + y

        def rows(t, s, half):
            hrows = outs[t].shape[0] // 8
            return outs[t].at[pl.ds((2 * s + half) * hrows, hrows)]

        sent = []
        for t in range(n):
            for j, (cx, cy) in enumerate(chips):
                cp = _rcopy(rows(t, me, c), rows(t, me, c), send_sems.at[6 * t + j], recv_sems.at[6 * t + j], (cx, cy, c))
                cp.start()
                sent.append(cp)
        for t in range(n):
            for j, (cx, cy) in enumerate(chips):
                landed = rows(t, 2 * cx + cy, c)
                _rcopy(landed, landed, send_sems.at[6 * t + j], recv_sems.at[6 * t + j], (cx, cy, c)).wait_recv()
                cp = _rcopy(landed, landed, send_sems.at[6 * t + 3 + j], recv_sems.at[6 * t + 3 + j], sib)
                cp.start()
                sent.append(cp)
        for t in range(n):
            for j, (cx, cy) in enumerate(chips):
                other = rows(t, 2 * cx + cy, 1 - c)
                _rcopy(other, other, send_sems.at[6 * t + 3 + j], recv_sems.at[6 * t + 3 + j], sib).wait_recv()
        for cp in sent:
            cp.wait_send()

    return _pc(body, name=name, in_specs=[ANY] * n, out_specs=[ANY] * n,
               out_shape=[jax.ShapeDtypeStruct(f.shape, f.dtype) for f in fulls],
               input_output_aliases={t: t for t in range(n)},
               scratch_shapes=[pltpu.SemaphoreType.DMA((6 * n,)), pltpu.SemaphoreType.DMA((6 * n,))])(*fulls)


HBM = pl.BlockSpec(memory_space=pltpu.HBM)
SEM = pl.BlockSpec(memory_space=pltpu.SEMAPHORE)
EFFECT = pltpu.SideEffectType.DATAFLOW_SIDE_EFFECTING
TOKEN = jax.ShapeDtypeStruct((8, LANE), F32)


def _in_hbm(a):
    return pltpu.with_memory_space_constraint(a, pltpu.HBM)


def _half_rows(ref, s, half):
    hrows = ref.shape[0] // 8
    return ref.at[pl.ds((2 * s + half) * hrows, hrows)]


def _split_start(bufs, copies, n_copies, *, name, before=None):
    n = len(bufs)
    extra = [] if before is None else [before]

    def body(*refs):
        send_sems, recv_sems, token = refs[n + len(extra)], refs[n + len(extra) + 1], refs[-1]
        for k, (src, dst, dev) in enumerate(copies(refs[:n])):
            _rcopy(src, dst, send_sems.at[k], recv_sems.at[k], dev).start()
        token[...] = jnp.zeros_like(token)

    res = _pc(body, name=name, in_specs=[HBM] * n + [ANY] * len(extra),
              out_specs=[SEM, SEM] + [HBM] * n + [pl.BlockSpec(memory_space=pltpu.VMEM)],
              out_shape=[pltpu.SemaphoreType.DMA((n_copies,)), pltpu.SemaphoreType.DMA((n_copies,))]
              + [pltpu.HBM(b.shape, b.dtype) for b in bufs] + [TOKEN],
              input_output_aliases={t: 2 + t for t in range(n)},
              compiler_params=pltpu.CompilerParams(has_side_effects=EFFECT))(*[_in_hbm(b) for b in bufs], *extra)
    return res[0], res[1], list(res[2:2 + n]), res[-1]


def _split_wait(send_sems, recv_sems, bufs, copies, after, *, name):
    n = len(bufs)
    after = list(after) if isinstance(after, (list, tuple)) else [after]

    def body(*refs):
        send_ref, recv_ref = refs[n], refs[n + 1]
        for k, (src, dst, dev) in enumerate(copies(refs[:n])):
            cp = _rcopy(src, dst, send_ref.at[k], recv_ref.at[k], dev)
            cp.wait_send()
            cp.wait_recv()

    return _pc(body, name=name, in_specs=[HBM] * n + [SEM, SEM] + [ANY] * len(after), out_specs=[HBM] * n,
               out_shape=[pltpu.HBM(b.shape, b.dtype) for b in bufs],
               input_output_aliases={t: t for t in range(n)},
               compiler_params=pltpu.CompilerParams(has_side_effects=EFFECT))(*bufs, send_sems, recv_sems, *after)


def _allgather_ici_copies(refs):
    x, y, c, chips = _place()
    return [(_half_rows(r, 2 * x + y, c), _half_rows(r, 2 * x + y, c), (cx, cy, c)) for r in refs for cx, cy in chips]


def _rs_chips_copies(refs):
    x, y, c, chips = _place()
    n = len(refs) // 2
    return [(refs[t].at[2 * cx + cy], refs[n + t].at[j], (cx, cy, c)) for t in range(n) for j, (cx, cy) in enumerate(chips)]


def _allgather_forward_copies(refs):
    x, y, c, chips = _place()
    return [(_half_rows(r, 2 * cx + cy, c), _half_rows(r, 2 * cx + cy, c), (x, y, 1 - c)) for r in refs for cx, cy in chips]


def _rs_final_copies(refs):
    x, y, c, _ = _place()
    return [(r.at[c], r.at[c], (x, y, 1 - c)) for r in refs]


def _rs_sibling_copies(refs):
    x, y, c, _ = _place()
    n = len(refs) // 2
    out = []
    for t in range(n):
        h = refs[t].shape[0] // 8
        out += [(refs[t].at[pl.ds((2 * s + 1 - c) * h, h)], refs[n + t].at[s], (x, y, 1 - c)) for s in range(4)]
    return out


def _allgather_forward(fulls, *, name):
    n = len(fulls)

    def body(*refs):
        outs = refs[n:2 * n]
        send_sems, recv_sems = refs[2 * n:]
        x, y, c, chips = _place()
        sent = []
        for t in range(n):
            for j, (cx, cy) in enumerate(chips):
                landed = _half_rows(outs[t], 2 * cx + cy, c)
                cp = _rcopy(landed, landed, send_sems.at[3 * t + j], recv_sems.at[3 * t + j], (x, y, 1 - c))
                cp.start()
                sent.append(cp)
        for t in range(n):
            for j, (cx, cy) in enumerate(chips):
                other = _half_rows(outs[t], 2 * cx + cy, 1 - c)
                _rcopy(other, other, send_sems.at[3 * t + j], recv_sems.at[3 * t + j], (x, y, 1 - c)).wait_recv()
        for cp in sent:
            cp.wait_send()

    return _pc(body, name=name, in_specs=[ANY] * n, out_specs=[ANY] * n,
               out_shape=[jax.ShapeDtypeStruct(f.shape, f.dtype) for f in fulls],
               input_output_aliases={t: t for t in range(n)},
               scratch_shapes=[pltpu.SemaphoreType.DMA((3 * n,)), pltpu.SemaphoreType.DMA((3 * n,))])(*fulls)


def _rs_sibling(grads, *, name):
    n = len(grads)

    def body(*refs):
        ins, outs = refs[:n], refs[n:2 * n]
        send_sems, recv_sems = refs[2 * n:]
        x, y, c, _ = _place()
        cps = []
        for t in range(n):
            h = ins[t].shape[0] // 8
            for s in range(4):
                cp = _rcopy(ins[t].at[pl.ds((2 * s + 1 - c) * h, h)], outs[t].at[s], send_sems.at[4 * t + s],
                            recv_sems.at[4 * t + s], (x, y, 1 - c))
                cp.start()
                cps.append(cp)
        for cp in cps:
            cp.wait()

    return _pc(body, name=name, in_specs=[ANY] * n, out_specs=[ANY] * n,
               out_shape=[jax.ShapeDtypeStruct((4, g.shape[0] // 8, g.shape[1]), g.dtype) for g in grads],
               scratch_shapes=[pltpu.SemaphoreType.DMA((4 * n,)), pltpu.SemaphoreType.DMA((4 * n,))])(*grads)


def _rs_chips(sends, *, name):
    n = len(sends)

    def body(*refs):
        s_refs, b_refs = refs[:n], refs[n:2 * n]
        send_sems, recv_sems = refs[2 * n:]
        x, y, c, chips = _place()
        cps = []
        for t in range(n):
            for j, (cx, cy) in enumerate(chips):
                cp = _rcopy(s_refs[t].at[2 * cx + cy], b_refs[t].at[j], send_sems.at[3 * t + j], recv_sems.at[3 * t + j],
                            (cx, cy, c))
                cp.start()
                cps.append(cp)
        for cp in cps:
            cp.wait()

    return _pc(body, name=name, in_specs=[ANY] * n, out_specs=[ANY] * n,
               out_shape=[jax.ShapeDtypeStruct((3,) + s.shape[1:], s.dtype) for s in sends],
               scratch_shapes=[pltpu.SemaphoreType.DMA((3 * n,)), pltpu.SemaphoreType.DMA((3 * n,))])(*sends)


def _rs_final(fulls, *, name):
    n = len(fulls)

    def body(*refs):
        outs = refs[n:2 * n]
        send_sems, recv_sems = refs[2 * n:]
        x, y, c, _ = _place()
        cps = []
        for t in range(n):
            cp = _rcopy(outs[t].at[c], outs[t].at[c], send_sems.at[t], recv_sems.at[t], (x, y, 1 - c))
            cp.start()
            cps.append(cp)
        for cp in cps:
            cp.wait()

    return _pc(body, name=name, in_specs=[ANY] * n, out_specs=[ANY] * n,
               out_shape=[jax.ShapeDtypeStruct(f.shape, f.dtype) for f in fulls],
               input_output_aliases={t: t for t in range(n)},
               scratch_shapes=[pltpu.SemaphoreType.DMA((n,)), pltpu.SemaphoreType.DMA((n,))])(*fulls)


def _add_halves(g, a, send_dtype, *, name):
    _, h, cols = a.shape
    th = _row_tile(h, cols)
    g4 = g.reshape(4, 2, h, cols)
    idx = jnp.stack([lax.axis_index("c"), 2 * lax.axis_index("x") + lax.axis_index("y")]).astype(jnp.int32)

    def shard(k, ir):
        return (ir[1] + 1 + k) % 4

    def body(idx_ref, g_ref, a_ref, p_ref, s_ref):
        v = g_ref[...].astype(F32) + a_ref[...].astype(F32)
        s_ref[...] = v.astype(send_dtype)

        @pl.when(pl.program_id(1) == 3)
        def _():
            p_ref[...] = v

    return _pc(body, name=name,
               grid_spec=pltpu.PrefetchScalarGridSpec(
                   num_scalar_prefetch=1, grid=(h // th, 4),
                   in_specs=[pl.BlockSpec((None, None, th, cols), lambda i, k, ir: (shard(k, ir), ir[0], i, 0)),
                             pl.BlockSpec((None, th, cols), lambda i, k, ir: (shard(k, ir), i, 0))],
                   out_specs=[pl.BlockSpec((th, cols), lambda i, k, ir: (i, 0)),
                              pl.BlockSpec((None, th, cols), lambda i, k, ir: (shard(k, ir), i, 0))]),
               out_shape=[jax.ShapeDtypeStruct((h, cols), F32), jax.ShapeDtypeStruct(a.shape, send_dtype)],
               compiler_params=pltpu.CompilerParams(dimension_semantics=("arbitrary", "arbitrary")))(idx, g4, a)


def _add_chips(p, b, *, name, order=None):
    h, cols = p.shape
    th = _row_tile(h, cols)
    idx = lax.axis_index("c").astype(jnp.int32).reshape(1)
    extra = [] if order is None else [order]

    def body(idx_ref, p_ref, b_ref, *rest):
        r_ref = rest[-1]
        r_ref[...] = ((p_ref[...] + b_ref[0].astype(F32)) + b_ref[1].astype(F32)) + b_ref[2].astype(F32)

    return _pc(body, name=name,
               grid_spec=pltpu.PrefetchScalarGridSpec(
                   num_scalar_prefetch=1, grid=(h // th,),
                   in_specs=[pl.BlockSpec((th, cols), lambda i, ir: (i, 0)),
                             pl.BlockSpec((3, th, cols), lambda i, ir: (0, i, 0))] + [ANY] * len(extra),
                   out_specs=pl.BlockSpec((None, th, cols), lambda i, ir: (ir[0], i, 0))),
               out_shape=jax.ShapeDtypeStruct((2, h, cols), F32),
               compiler_params=pltpu.CompilerParams(dimension_semantics=("arbitrary",)))(idx, p, b, *extra)


def _add_halves_all(grads, recv, send_dtypes, tag):
    parts, sends = [], []
    for t, (g, a) in enumerate(zip(grads, recv)):
        p, s = _add_halves(g, a, send_dtypes[t], name=f"rs_add_halves_{tag}{t}")
        parts.append(p)
        sends.append(s)
    return parts, sends


def _rs_finish(parts, others, tag, order=None):
    halves = [_add_chips(p, b, order=order, name=f"rs_add_chips_{tag}{t}") for t, (p, b) in enumerate(zip(parts, others))]
    full = _rs_final(halves, name=f"rs_final_{tag}")
    return [f.reshape(-1, f.shape[-1]) for f in full]


def _s5_discretize(lam_re, lam_im, log_dt, b_re, b_im):
    lam = lax.complex(lam_re, lam_im)
    dt = jnp.exp(log_dt)[:, None]
    lam_bar = jnp.exp(lam * dt)
    b_bar = ((lam_bar - 1.0) / lam)[..., None] * lax.complex(b_re, b_im)
    return jnp.real(lam_bar), jnp.imag(lam_bar), jnp.real(b_bar), jnp.imag(b_bar)


def _lanes_from_gp(re, im, cfg):
    v = jnp.stack([re, im]).reshape(2, cfg.NB, GROUPS_PER_BLOCK, SSM_STATE)
    return jnp.transpose(v, (1, 0, 2, 3)).reshape(1, cfg.NL)


def _gp_from_lanes(v, cfg):
    v = jnp.transpose(v.reshape(cfg.NB, 2, GROUPS_PER_BLOCK, SSM_STATE), (1, 0, 2, 3)).reshape(2, cfg.G, SSM_STATE)
    return v[0], v[1]


def _bb_band(bb_re, bb_im, cfg):
    eye = jnp.eye(GROUPS_PER_BLOCK, dtype=F32)
    bb = jnp.stack([bb_re, bb_im]).reshape(2, cfg.NB, GROUPS_PER_BLOCK, SSM_STATE, SSM_GROUP)
    return jnp.einsum('rjgpc,gh->jgcrhp', bb, eye).reshape(cfg.DS, 2 * GROUPS_PER_BLOCK * SSM_STATE)


def _bb_from_band(m, cfg):
    eye = jnp.eye(GROUPS_PER_BLOCK, dtype=F32)
    m = m.reshape(cfg.NB, GROUPS_PER_BLOCK, SSM_GROUP, 2, GROUPS_PER_BLOCK, SSM_STATE)
    v = jnp.einsum('jgcrhp,gh->rjgpc', m, eye).reshape(2, cfg.G, SSM_STATE, SSM_GROUP)
    return v[0], v[1]


def _cc_band(c_re, c_im, cfg):
    eye = jnp.eye(GROUPS_PER_BLOCK, dtype=F32)
    cc = jnp.stack([c_re, -c_im]).reshape(2, cfg.NB, GROUPS_PER_BLOCK, SSM_GROUP, SSM_STATE)
    return jnp.einsum('rjgcp,gh->jrhpgc', cc, eye).reshape(cfg.NL, GROUPS_PER_BLOCK * SSM_GROUP)


def _cc_from_band(m, cfg):
    eye = jnp.eye(GROUPS_PER_BLOCK, dtype=F32)
    m = m.reshape(cfg.NB, 2, GROUPS_PER_BLOCK, SSM_STATE, GROUPS_PER_BLOCK, SSM_GROUP)
    v = jnp.einsum('jrhpgc,gh->rjgcp', m, eye).reshape(2, cfg.G, SSM_GROUP, SSM_STATE)
    return v[0], -v[1]


PACK_COLS = 512
PACK_ROW_ALIGN = 64


def _pack(arrs):
    flat = jnp.concatenate([a.reshape(-1).astype(F32) for a in arrs])
    unit = PACK_COLS * PACK_ROW_ALIGN
    total = -(-flat.shape[0] // unit) * unit
    return jnp.pad(flat, (0, total - flat.shape[0])).reshape(-1, PACK_COLS)


def _unpack(p, shapes):
    flat = p.reshape(-1)
    out, off = [], 0
    for shp in shapes:
        size = math.prod(shp)
        out.append(flat[off:off + size].reshape(shp))
        off += size
    return out


def _adamw(w, g, m, v, *, name, emit_grad=False):
    c1 = 1.0 / (1.0 - ADAM_B1 ** ADAM_STEP)
    c2 = 1.0 / (1.0 - ADAM_B2 ** ADAM_STEP)

    if w.ndim == 2:
        outs = _adamw(w[None], g[None], m[None], v[None], name=name, emit_grad=emit_grad)
        return [o[0] for o in outs]
    lead, rows, cols = w.shape
    tc = _tile(cols, 512)
    tm = _tile(rows, max(8, ADAMW_BLOCK_BYTES // (4 * tc)), 8)
    n_out = 4 if emit_grad else 3

    def body(w_ref, g_ref, m_ref, v_ref, *o_refs):
        gv = g_ref[...]
        mn = ADAM_B1 * m_ref[...] + (1.0 - ADAM_B1) * gv
        vn = ADAM_B2 * v_ref[...] + (1.0 - ADAM_B2) * (gv * gv)
        delta = -ADAM_LR * ((mn * c1) / (jnp.sqrt(vn * c2) + ADAM_EPS) + ADAM_WD * w_ref[...])
        for o_ref, val in zip(o_refs, ((gv, delta, mn, vn) if emit_grad else (delta, mn, vn))):
            o_ref[...] = val

    blk = pl.BlockSpec((None, tm, tc), lambda n, i, j: (n, i, j))
    return _pc(body, name=name, grid=(lead, rows // tm, cols // tc), in_specs=[blk] * 4, out_specs=[blk] * n_out,
               out_shape=[jax.ShapeDtypeStruct((lead, rows, cols), F32)] * n_out,
               compiler_params=pltpu.CompilerParams(dimension_semantics=("parallel", "parallel", "parallel")))(w, g, m, v)


def _to_comm_layout(name, w, cfg):
    w = w[0]
    if name == 'w_in':
        return jnp.pad(w, ((0, 0), (0, cfg.DINP - cfg.DIN)))
    if name == 'w_q_b':
        hs = w.shape[1] // (QK_NOPE + QK_ROPE)
        wt = w.T.reshape(hs, QK_NOPE + QK_ROPE, cfg.QL)
        return jnp.pad(wt, ((0, 0), (0, HEAD_SLOT - QK_NOPE - QK_ROPE), (0, 0))).reshape(hs * HEAD_SLOT, cfg.QL)
    if name == 'w_kv_b':
        return w.T
    if name == 'w_up':
        return w.T.reshape(2, cfg.F // 4, cfg.D)
    return w


def _from_comm_layout(name, g, cfg):
    if name == 'w_in':
        g = g[:, :cfg.DIN]
    elif name == 'w_q_b':
        hs = g.shape[0] // HEAD_SLOT
        g = g.reshape(hs, HEAD_SLOT, cfg.QL)[:, :QK_NOPE + QK_ROPE].reshape(hs * (QK_NOPE + QK_ROPE), cfg.QL).T
    elif name == 'w_kv_b':
        g = g.T
    elif name == 'w_up':
        g = g.reshape(2, cfg.FQ, cfg.D)[:, :cfg.F // 4].reshape(cfg.F // 2, cfg.D).T
    elif name == 'w_down':
        g = g[:cfg.F // 4]
    return g[None]


def _ff_pad(v, cfg):
    k = v.shape[0]
    return jnp.pad(v.reshape(k, 4, cfg.F // 4), ((0, 0), (0, 0), (0, cfg.FQ - cfg.F // 4))).reshape(k, cfg.FP)


def _ff_unpad(v, cfg):
    k = v.shape[0]
    return v.reshape(k, 4, cfg.FQ)[:, :, :cfg.F // 4].reshape(k, cfg.F)


def _step(cfg, w, m, v, x, loss_target):
    lp, d, ds, nl = cfg.LP, cfg.D, cfg.DS, cfg.NL
    xi, yi = lax.axis_index("x"), lax.axis_index("y")
    me = 2 * xi + yi

    def place(n, order=None):
        rows_to = cfg.FQ if n in ('w_up', 'w_down') else None
        return _place_shard(_to_comm_layout(n, w[n], cfg), BF16, order=order, rows_to=rows_to, name=f"place_{n}")

    first = [place('w_in'), _place_shard(w['meta_tokens'], F32, name="place_meta")]
    f_send, f_recv, f_flying, f_token = _split_start(first, _allgather_ici_copies, 6, name="allgather_first_start")
    conv_w_shard = jnp.pad(w['conv_w'][0], ((0, ROW_ALIGN - 3), (0, cfg.FQ - cfg.F // 4)))
    mid = [place(n, f_token) for n in BIG[1:5]] + [_place_shard(conv_w_shard, F32, order=f_token, name="place_conv_w")]
    mid_send, mid_recv, mid_flying, mid_token = _split_start(mid, _allgather_ici_copies, 3 * len(mid), before=f_token,
                                                             name="allgather_mid_start")
    up_send, up_recv, up_flying, up_token = _split_start([place('w_up', mid_token)], _allgather_ici_copies, 3,
                                                         before=mid_token, name="allgather_up_start")
    dn_send, dn_recv, dn_flying, ffn_token = _split_start([place('w_down', up_token)], _allgather_ici_copies, 3,
                                                          before=up_token, name="allgather_down_start")
    conv_b = _ff_pad(w['conv_b'], cfg)

    pos = (jnp.arange(lp, dtype=jnp.int32) - PAD).astype(F32)
    inv_freq = 1.0 / (ROPE_BASE ** (jnp.arange(0, QK_ROPE, 2, dtype=F32) / QK_ROPE))
    ang = pos[:, None] * inv_freq[None, :]
    zpad = jnp.zeros((lp, LANE - QK_ROPE), F32)
    cos_t = jnp.concatenate([jnp.cos(ang), jnp.cos(ang), zpad], axis=1)
    sin_t = jnp.concatenate([jnp.sin(ang), jnp.sin(ang), zpad], axis=1)

    s5_in = (w['lam_re'][0], w['lam_im'][0], w['log_dt'][0], w['b_re'][0], w['b_im'][0])
    (a_re, a_im, bb_re, bb_im), s5_vjp = jax.vjp(_s5_discretize, *s5_in)
    lam_dt = lax.complex(s5_in[0], s5_in[1]) * jnp.exp(s5_in[2])[:, None]
    a_pow = jnp.exp(jnp.arange(1, 9, dtype=F32)[:, None, None] * lam_dt[None])
    r8 = jnp.arange(8)
    step_f = jnp.stack([jnp.where((r8 >= k)[:, None, None], a_pow[k - 1][None], 0.0) for k in (1, 2, 4)]).reshape(24, cfg.G, -1)
    step_b = jnp.stack([jnp.where((r8 < 8 - k)[:, None, None], a_pow[k - 1][None], 0.0) for k in (1, 2, 4)]).reshape(24, cfg.G, -1)
    rows_f = jnp.concatenate([a_pow, step_f])
    rows_b = jnp.conj(jnp.concatenate([a_pow[::-1], step_b]))

    def lane_rows(t):
        v = jnp.stack([jnp.real(t), jnp.imag(t)], axis=1).reshape(t.shape[0], 2, cfg.NB, GROUPS_PER_BLOCK, SSM_STATE)
        return jnp.transpose(v, (0, 2, 1, 3, 4)).reshape(t.shape[0], cfg.NL)

    pw_fwd, pw_bwd = lane_rows(rows_f), lane_rows(rows_b)
    bb_band = _bb_band(bb_re, bb_im, cfg).astype(BF16)
    cc_band = _cc_band(w['c_re'][0], w['c_im'][0], cfg).astype(BF16)
    d_skip, b_glu = w['d_skip'], w['b_glu']

    f_landed = _split_wait(f_send, f_recv, f_flying, _allgather_ici_copies, [ffn_token, cos_t, sin_t, pw_fwd, pw_bwd, bb_band, cc_band],
                           name="allgather_first_wait")
    w_in, meta_full = _allgather_forward(f_landed, name="allgather_first_forward")
    meta = jnp.transpose(meta_full.reshape(4, N_META, d // 4), (1, 0, 2)).reshape(N_META, d)
    mix_norm = w['mix_norm'] + ffn_token[0:1, 0:1]

    h0 = jnp.concatenate([jnp.zeros((PAD, d), F32), meta, x[0]], axis=0)
    xn = _rms_fwd(h0, mix_norm, name="rms_mix")
    z = _mm(xn, w_in, name="mm_in", tn=_tile(cfg.DINP, 640))
    u = (z, ds, 0)
    q_a = (z, cfg.QL, ds // cfg.QL)
    kv_a = (z, cfg.KVL, (ds + cfg.QL) // cfg.KVL)
    k_pe = (z, LANE, (ds + cfg.QL + cfg.KVL) // LANE)

    hs, yc = _s5_fwd(z, bb_band, cc_band, pw_fwd, cfg, name="s5_fwd")

    def s5_y(ycv, uv, dk):
        return ycv + dk * uv

    gl = _ew(lambda rid, ycv, uv, dk: jax.nn.gelu(s5_y(ycv, uv, dk)), [yc, u], [d_skip], [(ds, BF16)], name="s5_gelu")[0]
    mid_landed = _split_wait(mid_send, mid_recv, mid_flying, _allgather_ici_copies, gl, name="allgather_mid_wait")
    w_glu, w_qt, w_kvt, w_out, conv_full = _allgather_forward(mid_landed, name="allgather_mid_forward")
    conv_w = jnp.transpose(conv_full.reshape(4, ROW_ALIGN, cfg.FQ)[:, :3], (1, 0, 2)).reshape(3, cfg.FP)
    tg = _mm(gl, w_glu, name="mm_glu")
    ya = _ew(lambda rid, ycv, uv, tv, dk, bg: jax.nn.gelu(s5_y(ycv, uv, dk)) * jax.nn.sigmoid(tv + bg),
             [yc, u, tg], [d_skip, b_glu], [(ds, F32)], name="s5_glu")[0]

    qn = _rms_fwd(q_a, w['q_a_norm'], name="rms_q")
    kvn = _rms_fwd(kv_a, w['kv_a_norm'], name="rms_kv")
    q_raw = _mm(qn, w_qt, tb=True, name="mm_q")
    qx = _ew(_rope_heads(_rope, cfg.H), [q_raw, cos_t, sin_t], [], [(cfg.H * HEAD_SLOT, BF16)], name="rope_q")[0]
    kv = _mm(kvn, w_kvt, tb=True, out_dtype=BF16, name="mm_kv")
    kr = _ew(lambda rid, kp, cs, sn: _rope(kp, cs, sn), [k_pe, cos_t, sin_t], [], [(LANE, BF16)], name="rope_k")[0]
    o, lse = _attn_fwd(qx, kv, kr, cfg, name="attn_fwd")

    def norm2(rid, yav, ov, gs, ga):
        return jnp.concatenate([_rms_parts(yav, gs)[0] * gs, _rms_parts(ov, ga)[0] * ga], axis=1)

    up_landed = _split_wait(up_send, up_recv, up_flying, _allgather_ici_copies, o, name="allgather_up_wait")
    uf_send, uf_recv, uf_flying, uf_token = _split_start(up_landed, _allgather_forward_copies, 3,
                                                         name="allgather_up_forward_start")
    yn = _ew(norm2, [ya, o], [w['out_norm_ssm'] + uf_token[0:1, 0:1], w['out_norm_attn']], [(cfg.DMIX, BF16)],
             name="rms_out")[0]
    h1 = _mm(yn, w_out, res=h0, name="mm_out")
    xn2 = _rms_fwd(h1, w['ffn_norm'], name="rms_ffn")
    dn_landed = _split_wait(dn_send, dn_recv, dn_flying, _allgather_ici_copies, xn2, name="allgather_down_wait")
    df_send, df_recv, df_flying, df_token = _split_start(dn_landed, _allgather_forward_copies, 3,
                                                         name="allgather_down_forward_start")
    w_upt, = _split_wait(uf_send, uf_recv, uf_flying, _allgather_forward_copies, df_token,
                         name="allgather_up_forward_wait")
    up, act = _ffn_up(xn2, w_upt, conv_w, conv_b, name="ffn_up")
    w_down, = _split_wait(df_send, df_recv, df_flying, _allgather_forward_copies, act,
                          name="allgather_down_forward_wait")
    h2 = _mm(act, w_down, res=h1, tm=_tile(lp, 544, ROW_ALIGN), name="mm_down")

    g_final = w['final_norm'].reshape(1, d)

    def head(rid, hv, tv, gv):
        xhat, r = _rms_parts(hv, gv)
        valid = rid >= PAD + N_META
        diff = jnp.where(valid, xhat * gv - tv, 0.0)
        dout = diff * (1.0 / d)
        dxhat = dout * gv
        dx = r * (dxhat - xhat * jnp.mean(dxhat * xhat, axis=-1, keepdims=True))
        return dx, dx, dout * xhat, 0.5 * diff * dout

    dh2, dh2_b, dg_final, loss_cols = _ew(head, [h2, (loss_target[0], d, 0, SKIP)], [g_final], [(d, F32), (d, BF16)], [d, d],
                                          tm=PAD + N_META, name="loss_head")
    loss = lax.psum(jnp.sum(loss_cols), ("x", "y", "c"))

    dw_down = _mm(act, dh2_b, ta=True, tn=d, tm=512, out_dtype=BF16, name="mm_dw_down")

    def sibling_start(g, tag):
        land = lax.empty((4, g.shape[0] // 8, g.shape[1]), g.dtype)
        return _split_start([g, land], _rs_sibling_copies, 4, name=f"rs_sibling_{tag}_start")

    dn_send, dn_recv, dn_flying, dn_token = sibling_start(dw_down, "down")
    dup, dconv_w, dconv_b = _ffn_dact(dh2_b, w_down, up, conv_w, conv_b + dn_token[0:1, 0:1], name="ffn_dact")
    tk_up, tm_up = _tile(cfg.FP, 1408), _tile(cfg.FP, 512)
    dw_upt = _mm(dup, xn2, ta=True, dims=(2 * cfg.FP, d, lp), tn=d, tm=tm_up, a_lead=True, out_dtype=BF16, name="mm_dw_up",
                 a_idx=lambda i, j, k: (i // (cfg.FP // tm_up), 0, i % (cfg.FP // tm_up)))
    up_send, up_recv, up_flying, up_token = sibling_start(dw_upt, "up")
    dxn2 = _mm(dup, w_upt, dims=(lp, d, 2 * cfg.FP), tk=tk_up, tn=1024, a_lead=True, name="mm_dxn2",
               a_idx=lambda i, j, k: (k // (cfg.FP // tk_up), i, k % (cfg.FP // tk_up)))
    dh1, dh1_b, dg_ffn = _rms_bwd(h1, w['ffn_norm'] + up_token[0:1, 0:1], dxn2, res=dh2, mask=True, with_bf16=True,
                                  name="rms_ffn_bwd")

    dyn = _mm(dh1_b, w_out, tb=True, name="mm_dyn")
    dw_out = _mm(yn, dh1_b, ta=True, tn=d, tm=512, name="mm_dw_out")
    up_done = _split_wait(up_send, up_recv, up_flying, _rs_sibling_copies, dw_out, name="rs_sibling_up_wait")
    dn_done = _split_wait(dn_send, dn_recv, dn_flying, _rs_sibling_copies, dw_out, name="rs_sibling_down_wait")
    early_parts, early_sends = _add_halves_all([up_done[0], dn_done[0]], [up_done[1], dn_done[1]], [BF16] * 2, "early")
    chip_lands = [lax.empty((3,) + s.shape[1:], s.dtype) for s in early_sends]
    ch_send, ch_recv, ch_flying, ch_token = _split_start(early_sends + chip_lands, _rs_chips_copies, 6,
                                                         name="rs_chips_early_start")
    dya, dg_ssm = _rms_bwd(ya, w['out_norm_ssm'] + ch_token[0:1, 0:1], (dyn, ds, 0), name="rms_ssm_bwd")
    do, dg_attn = _rms_bwd(o, w['out_norm_attn'], (dyn, cfg.DATTN, ds // cfg.DATTN), name="rms_attn_bwd")

    dqx, dkv, dkr = _attn_bwd(qx, kv, kr, o, lse, do, cfg, name="attn_bwd")
    dq_raw = _ew(_rope_heads(_unrope, cfg.H), [dqx, cos_t, sin_t], [], [(cfg.H * HEAD_SLOT, BF16)], name="unrope_q")[0]
    dk_pe = _ew(lambda rid, dk, cs, sn: _unrope(dk, cs, sn), [dkr, cos_t, sin_t], [], [(LANE, F32)], name="unrope_k")[0]
    dqn = _mm(dq_raw, w_qt, name="mm_dqn")
    dw_qt = _mm(dq_raw, qn, ta=True, tm=512, name="mm_dw_q")
    dkvn = _mm(dkv, w_kvt, name="mm_dkvn")
    dw_kvt = _mm(dkv, kvn, ta=True, tm=512, name="mm_dw_kv")
    dq_a, dg_q = _rms_bwd(q_a, w['q_a_norm'], dqn, name="rms_q_bwd")
    dkv_a, dg_kv = _rms_bwd(kv_a, w['kv_a_norm'], dkvn, name="rms_kv_bwd")

    def glu_bwd(rid, ycv, uv, tv, dyav, dk, bg):
        gelu = jax.nn.gelu(s5_y(ycv, uv, dk))
        sg = jax.nn.sigmoid(tv + bg)
        dt = dyav * gelu * sg * (1.0 - sg)
        return dt, dyav * sg, dt

    dt_b, dgl1, db_glu = _ew(glu_bwd, [yc, u, tg, dya], [d_skip, b_glu], [(ds, BF16), (ds, F32)], [ds], name="s5_glu_bwd")
    dgl = _mm(dt_b, w_glu, tb=True, res=dgl1, name="mm_dgl")
    dw_glu = _mm(gl, dt_b, ta=True, tm=512, name="mm_dw_glu")

    def gelu_bwd(rid, ycv, uv, dglv, dk):
        _, vjp = jax.vjp(jax.nn.gelu, s5_y(ycv, uv, dk))
        dy = vjp(dglv)[0]
        return dy, dy * dk, dy * uv

    mid_grads = [dw_out, dw_glu, dw_qt, dw_kvt]
    mid_lands = [lax.empty((4, g.shape[0] // 8, g.shape[1]), g.dtype) for g in mid_grads]
    ms_send, ms_recv, ms_flying, ms_token = _split_start(mid_grads + mid_lands, _rs_sibling_copies, 4 * len(mid_grads),
                                                         name="rs_sibling_mid_start")
    dy_b, du_skip, dd_skip = _ew(gelu_bwd, [yc, u, dgl], [d_skip + ms_token[0:1, 0:1]], [(ds, BF16), (ds, F32)], [ds],
                                 name="s5_gelu_bwd")
    ms_done = _split_wait(ms_send, ms_recv, ms_flying, _rs_sibling_copies, dy_b, name="rs_sibling_mid_wait")
    mid_parts, mid_sends = _add_halves_all(ms_done[:4], ms_done[4:], [BF16] * 4, "mid")
    mid_chip_lands = [lax.empty((3,) + s.shape[1:], s.dtype) for s in mid_sends]
    mc_send, mc_recv, mc_flying, mc_token = _split_start(mid_sends + mid_chip_lands, _rs_chips_copies, 3 * len(mid_sends),
                                                         name="rs_chips_mid_start")
    du, dbb_band, dcc_band, da_l = _s5_bwd(dy_b, hs, z, bb_band, cc_band, pw_bwd + mc_token[0:1, 0:1], du_skip, cfg,
                                           name="s5_bwd")

    dz = jnp.concatenate([du, dq_a, dkv_a, dk_pe], axis=1).astype(BF16)
    dxn = _mm(dz, w_in, tb=True, name="mm_dxn")
    dw_in = _mm(xn, dz, ta=True, tm=512, tn=_tile(cfg.DINP, 1024), name="mm_dw_in")
    def mix_bwd(rid, xv, dyv, resv, gv):
        dx, dg = _rms_bwd_block(xv, gv, dyv)
        dx = dx + resv
        return dx, dx, dg

    grad_x, dh0_head, dg_mix = _ew(mix_bwd, [h0, dxn, dh1], [mix_norm], [(d, F32, SKIP), (d, F32, FIRST)], [d],
                                   tm=PAD + N_META, name="rms_mix_bwd")
    grad_x = grad_x[None]

    da_re, da_im = _gp_from_lanes(da_l, cfg)
    dbb_re, dbb_im = _bb_from_band(dbb_band, cfg)
    dlam_re, dlam_im, dlog_dt, db_re, db_im = s5_vjp((da_re, da_im, dbb_re, dbb_im))
    dc_re, dc_im = _cc_from_band(dcc_band, cfg)
    local_small = {
        'meta_tokens': dh0_head[PAD:], 'mix_norm': dg_mix, 'lam_re': dlam_re, 'lam_im': dlam_im, 'log_dt': dlog_dt,
        'b_re': db_re, 'b_im': db_im, 'c_re': dc_re, 'c_im': dc_im, 'd_skip': dd_skip, 'b_glu': db_glu, 'q_a_norm': dg_q,
        'kv_a_norm': dg_kv, 'out_norm_ssm': dg_ssm, 'out_norm_attn': dg_attn, 'ffn_norm': dg_ffn,
        'conv_w': _ff_unpad(dconv_w, cfg), 'conv_b': _ff_unpad(dconv_b, cfg), 'final_norm': dg_final,
    }
    small_shapes = [local_small[n].shape for n in SMALL]

    small_pack = _pack([local_small[n] for n in SMALL])
    ch_done = _split_wait(ch_send, ch_recv, ch_flying, _rs_chips_copies, small_pack, name="rs_chips_early_wait")
    early_halves = [_add_chips(p, b, name=f"rs_add_chips_early{t}") for t, (p, b) in enumerate(zip(early_parts, ch_done[2:]))]
    fe_send, fe_recv, fe_flying, fe_token = _split_start(early_halves, _rs_final_copies, 2, name="rs_final_early_start")
    end_local = [dw_in, small_pack + fe_token[0:1, 0:1]]
    end_recv = _rs_sibling(end_local, name="rs_sibling_end")
    end_parts, end_sends = _add_halves_all(end_local, end_recv, [BF16, F32], "end")
    end_lands = [lax.empty((3,) + s.shape[1:], s.dtype) for s in end_sends]
    ec_send, ec_recv, ec_flying, ec_token = _split_start(end_sends + end_lands, _rs_chips_copies, 3 * len(end_sends),
                                                         name="rs_chips_end_start")
    fe_done = _split_wait(fe_send, fe_recv, fe_flying, _rs_final_copies, ec_token, name="rs_final_early_wait")
    red_up, red_down = [f.reshape(-1, f.shape[-1]) for f in fe_done]

    delta, new_m, new_v, grads = {}, {}, {}, {}
    padded_rows = ('w_down',)

    def adamw_big(n, red):
        shp = w[n].shape
        w2, m2, v2 = [t.reshape(shp[-2], shp[-1]) for t in (w[n], m[n], v[n])]
        if n in padded_rows:
            g2, dl, mn, vn = _adamw(w2, red, m2, v2, emit_grad=True, name=f"adamw_{n}")
            grads[n] = g2.reshape(shp)
        else:
            grads[n] = _from_comm_layout(n, red, cfg)
            dl, mn, vn = _adamw(w2, grads[n].reshape(shp[-2], shp[-1]), m2, v2, name=f"adamw_{n}")
        delta[n], new_m[n], new_v[n] = dl.reshape(shp), mn.reshape(shp), vn.reshape(shp)

    def adamw_up(red):
        q = cfg.F // 4
        wt, mt, vt = [jnp.transpose(t[0]).reshape(2, q, d) for t in (w['w_up'], m['w_up'], v['w_up'])]
        outs = _adamw(wt, red.reshape(2, cfg.FQ, d), mt, vt, emit_grad=True, name="adamw_w_up")
        grads['w_up'], delta['w_up'], new_m['w_up'], new_v['w_up'] = [jnp.transpose(t.reshape(2 * q, d))[None] for t in outs]

    adamw_up(red_up)
    adamw_big('w_down', red_down)
    mc_done = _split_wait(mc_send, mc_recv, mc_flying, _rs_chips_copies, delta['w_down'], name="rs_chips_mid_wait")
    ec_done = _split_wait(ec_send, ec_recv, ec_flying, _rs_chips_copies, mc_done[0], name="rs_chips_end_wait")
    red = _rs_finish(mid_parts + end_parts, list(mc_done[len(mid_sends):]) + list(ec_done[len(end_sends):]), "rest")
    small_full = _allgather([_place_shard(red[5], F32, name="place_small")], name="allgather_small")[0]
    small_sum = dict(zip(SMALL, _unpack(small_full, small_shapes)))
    for n, r in zip(['w_out', 'w_glu', 'w_q_b', 'w_kv_b'], red[:4]):
        adamw_big(n, r)
    in_t = [jnp.transpose(t[0]) for t in (w['w_in'], m['w_in'], v['w_in'])]
    outs = _adamw(in_t[0], jnp.transpose(red[4][:, :cfg.DIN]), in_t[1], in_t[2], emit_grad=True, name="adamw_w_in")
    grads['w_in'], delta['w_in'], new_m['w_in'], new_v['w_in'] = [jnp.transpose(t)[None] for t in outs]

    for n in SMALL:
        g = small_sum[n]
        if n == 'meta_tokens':
            g = lax.dynamic_slice_in_dim(g, me * (d // 4), d // 4, axis=1)
        elif n == 'conv_w':
            g = lax.dynamic_slice_in_dim(g, me * (cfg.F // 4), cfg.F // 4, axis=1)[None]
        else:
            g = g.reshape(w[n].shape)
        grads[n] = g

    shapes = [w[n].shape for n in SMALL]
    packs = [_pack([src[n] for n in SMALL]) for src in (w, grads, m, v)]
    for dst, p in zip((delta, new_m, new_v), _adamw(*packs, name="adamw_small")):
        dst.update(zip(SMALL, _unpack(p, shapes)))

    return (loss, grad_x, *[grads[n] for n in WEIGHTS], *[delta[n] for n in WEIGHTS],
            *[new_m[n] for n in WEIGHTS], *[new_v[n] for n in WEIGHTS])


def kernel(x, meta_tokens, mix_norm, w_in, lam_re, lam_im, log_dt, b_re, b_im, c_re, c_im, d_skip, w_glu, b_glu, q_a_norm, w_q_b, kv_a_norm, w_kv_b, out_norm_ssm, out_norm_attn, w_out, ffn_norm, w_up, conv_w, conv_b, w_down, final_norm, loss_target, m_meta_tokens, m_mix_norm, m_w_in, m_lam_re, m_lam_im, m_log_dt, m_b_re, m_b_im, m_c_re, m_c_im, m_d_skip, m_w_glu, m_b_glu, m_q_a_norm, m_w_q_b, m_kv_a_norm, m_w_kv_b, m_out_norm_ssm, m_out_norm_attn, m_w_out, m_ffn_norm, m_w_up, m_conv_w, m_conv_b, m_w_down, m_final_norm, v_meta_tokens, v_mix_norm, v_w_in, v_lam_re, v_lam_im, v_log_dt, v_b_re, v_b_im, v_c_re, v_c_im, v_d_skip, v_w_glu, v_b_glu, v_q_a_norm, v_w_q_b, v_kv_a_norm, v_w_kv_b, v_out_norm_ssm, v_out_norm_attn, v_w_out, v_ffn_norm, v_w_up, v_conv_w, v_conv_b, v_w_down, v_final_norm):
    args = dict(locals())
    w = {n: args[n] for n in WEIGHTS}
    m = {n: args["m_" + n] for n in WEIGHTS}
    v = {n: args["v_" + n] for n in WEIGHTS}
    return _step(PROD, w, m, v, x, loss_target)
```

```python
import functools
import math
from typing import NamedTuple

import jax
import jax.numpy as jnp
from jax import lax
from jax.experimental import pallas as pl
from jax.experimental.pallas import tpu as pltpu

F32, BF16 = jnp.float32, jnp.bfloat16
MESH = pl.DeviceIdType.MESH
LANE = 128
ROW_ALIGN = 16
N_META = 16
PAD = 112
CHUNK = 64
SSM_GROUP = 16
SSM_STATE = 64
GROUPS_PER_BLOCK = 8
QK_NOPE, QK_ROPE, V_HEAD = 128, 64, 128
HEAD_SLOT = 256
ROPE_BASE = 10000.0
EPS = 1e-6
ADAM_LR, ADAM_B1, ADAM_B2, ADAM_EPS, ADAM_WD, ADAM_STEP = 0.001, 0.9, 0.999, 1e-08, 0.01, 10
DT_F32_BLOCK_BYTES = 9 << 18
ADAMW_BLOCK_BYTES = 3 << 19
PLACE_BLOCK_BYTES = 6 << 20
SKIP, FIRST = "skip", "first"


class Cfg(NamedTuple):
    D: int
    S: int
    DS: int
    H: int
    QL: int
    KVL: int
    F: int

    @property
    def LP(self):
        return PAD + N_META + self.S

    @property
    def G(self):
        return self.DS // SSM_GROUP

    @property
    def NB(self):
        return self.G // GROUPS_PER_BLOCK

    @property
    def NL(self):
        return 2 * self.G * SSM_STATE

    @property
    def DATTN(self):
        return self.H * V_HEAD

    @property
    def DMIX(self):
        return self.DS + self.DATTN

    @property
    def DIN(self):
        return self.DS + self.QL + self.KVL + QK_ROPE

    @property
    def DINP(self):
        return self.DS + self.QL + self.KVL + LANE

    @property
    def FQ(self):
        return -(-(self.F // 4) // LANE) * LANE

    @property
    def FP(self):
        return 4 * self.FQ


PROD = Cfg(D=2048, S=2048, DS=1024, H=8, QL=512, KVL=256, F=5504)

WEIGHTS = ['meta_tokens', 'mix_norm', 'w_in', 'lam_re', 'lam_im', 'log_dt', 'b_re', 'b_im', 'c_re', 'c_im', 'd_skip',
           'w_glu', 'b_glu', 'q_a_norm', 'w_q_b', 'kv_a_norm', 'w_kv_b', 'out_norm_ssm', 'out_norm_attn', 'w_out',
           'ffn_norm', 'w_up', 'conv_w', 'conv_b', 'w_down', 'final_norm']
BIG = ['w_in', 'w_glu', 'w_q_b', 'w_kv_b', 'w_out', 'w_up', 'w_down']
SMALL = [n for n in WEIGHTS if n not in BIG]


def _pc(body, **kw):
    return pl.pallas_call(body, **kw)


def _tile(n, target, align=LANE):
    best = None
    d = align
    while d <= min(n, target):
        if n % d == 0:
            best = d
        d += align
    return best if best is not None else n


def _row_tile(rows, cols):
    return _tile(rows, max(ROW_ALIGN, DT_F32_BLOCK_BYTES // (4 * cols)), ROW_ALIGN)


def _mm(a, b, *, name, ta=False, tb=False, tm=None, tn=512, tk=None, out_dtype=F32, res=None,
        a_idx=None, b_idx=None, dims=None, a_lead=False):
    if dims is None:
        m, k = (a.shape[1], a.shape[0]) if ta else a.shape
        n = b.shape[0] if tb else b.shape[1]
    else:
        m, n, k = dims
    tm = _tile(m, tm or m, LANE if ta else ROW_ALIGN)
    tn = _tile(n, tn)
    tk = _tile(k, tk or k, ROW_ALIGN if (ta and not tb) else LANE)
    nm, nn, nk = m // tm, n // tn, k // tk
    a_idx = a_idx or ((lambda i, j, kk: (kk, i)) if ta else (lambda i, j, kk: (i, kk)))
    b_idx = b_idx or ((lambda i, j, kk: (j, kk)) if tb else (lambda i, j, kk: (kk, j)))
    dn = (((0 if ta else 1,), (1 if tb else 0,)), ((), ()))

    def body(*refs):
        a_ref, b_ref = refs[0], refs[1]
        r_ref = refs[2] if res is not None else None
        o_ref = refs[3] if res is not None else refs[2]
        d = lax.dot_general(a_ref[...].astype(BF16), b_ref[...].astype(BF16), dn, preferred_element_type=F32)

        def finish(r):
            if r_ref is not None:
                r = r + r_ref[...].astype(F32)
            o_ref[...] = r.astype(out_dtype)

        if nk == 1:
            finish(d)
        else:
            acc = refs[-1]
            kk = pl.program_id(2)

            @pl.when(kk == 0)
            def _():
                acc[...] = d

            @pl.when(kk > 0)
            def _():
                acc[...] += d

            @pl.when(kk == nk - 1)
            def _():
                finish(acc[...])

    a_blk = ((None,) if a_lead else ()) + ((tk, tm) if ta else (tm, tk))
    in_specs = [pl.BlockSpec(a_blk, a_idx), pl.BlockSpec((tn, tk) if tb else (tk, tn), b_idx)]
    args = [a, b]
    if res is not None:
        in_specs.append(pl.BlockSpec((tm, tn), lambda i, j, kk: (i, j)))
        args.append(res)
    return _pc(body, name=name, grid=(nm, nn, nk), in_specs=in_specs,
               out_specs=pl.BlockSpec((tm, tn), lambda i, j, kk: (i, j)),
               out_shape=jax.ShapeDtypeStruct((m, n), out_dtype),
               scratch_shapes=[pltpu.VMEM((tm, tn), F32)] if nk > 1 else [],
               compiler_params=pltpu.CompilerParams(dimension_semantics=("parallel", "parallel", "arbitrary")))(*args)


def _ew(fn, ins, vecs, outs, sums=(), *, name, tm=None):
    ins = [x if isinstance(x, tuple) else (x, x.shape[1], 0) for x in ins]
    ins = [x if len(x) == 4 else x + (None,) for x in ins]
    outs = [o if len(o) == 3 else o + (None,) for o in outs]
    rows = ins[0][0].shape[0]
    cmax = max([c for _, c, _, _ in ins] + [c for c, _, _ in outs])
    tm = tm or _row_tile(rows, cmax)
    n_in, n_vec, n_out, n_sum = len(ins), len(vecs), len(outs), len(sums)

    def body(*refs):
        i = pl.program_id(0)
        rid = i * tm + lax.broadcasted_iota(jnp.int32, (tm, 1), 0)
        vals = [r[...] for r in refs[:n_in + n_vec]]
        res = fn(rid, *vals)
        res = res if isinstance(res, (tuple, list)) else (res,)
        o_refs = refs[n_in + n_vec:]
        for o_ref, r, (_, _, mode) in zip(o_refs[:n_out], res[:n_out], outs):
            if mode == FIRST:
                @pl.when(i == 0)
                def _():
                    o_ref[...] = r.astype(o_ref.dtype)
            else:
                o_ref[...] = r.astype(o_ref.dtype)
        for o_ref, r in zip(o_refs[n_out:], res[n_out:]):
            part = jnp.sum(r.astype(F32), axis=0, keepdims=True)

            @pl.when(i == 0)
            def _():
                o_ref[...] = part

            @pl.when(i > 0)
            def _():
                o_ref[...] += part

    def row_idx(mode):
        if mode == SKIP:
            return lambda i, cb=0: (jnp.maximum(i - 1, 0), cb)
        if mode == FIRST:
            return lambda i, cb=0: (0, cb)
        return lambda i, cb=0: (i, cb)

    in_specs = [pl.BlockSpec((tm, c), functools.partial(row_idx(mode), cb=cb)) for _, c, cb, mode in ins]
    in_specs += [pl.BlockSpec(v.shape, functools.partial(lambda i, nd: (0,) * nd, nd=v.ndim)) for v in vecs]
    out_specs = [pl.BlockSpec((tm, c), row_idx(mode)) for c, _, mode in outs]
    out_specs += [pl.BlockSpec((1, c), lambda i: (0, 0)) for c in sums]
    out_rows = {None: rows, SKIP: rows - tm, FIRST: tm}
    out_shape = [jax.ShapeDtypeStruct((out_rows[mode], c), dt) for c, dt, mode in outs]
    out_shape += [jax.ShapeDtypeStruct((1, c), F32) for c in sums]
    return _pc(body, name=name, grid=(rows // tm,), in_specs=in_specs, out_specs=out_specs, out_shape=out_shape,
               compiler_params=pltpu.CompilerParams(dimension_semantics=("arbitrary",)))(*[x[0] for x in ins], *vecs)


def _rms_parts(x, g):
    r = lax.rsqrt(jnp.mean(x * x, axis=-1, keepdims=True) + EPS)
    return x * r, r


def _rms_bwd_block(x, g, dy):
    xhat, r = _rms_parts(x, g)
    dxhat = dy * g
    dx = r * (dxhat - xhat * jnp.mean(dxhat * xhat, axis=-1, keepdims=True))
    return dx, dy * xhat


def _rms_fwd(x, g, *, name):
    c = x[1] if isinstance(x, tuple) else x.shape[1]
    return _ew(lambda rid, xv, gv: _rms_parts(xv.astype(F32), gv)[0] * gv, [x], [g], [(c, BF16)], name=name)[0]


def _rms_bwd(x, g, dy, *, name, res=None, mask=False, with_bf16=False):
    c = x[1] if isinstance(x, tuple) else x.shape[1]

    def fn(rid, xv, dyv, *rest):
        gv = rest[-1]
        dx, dg = _rms_bwd_block(xv.astype(F32), gv, dyv.astype(F32))
        if res is not None:
            dx = dx + rest[0]
        if mask:
            dx = jnp.where(rid >= PAD, dx, 0.0)
        return (dx, dx, dg) if with_bf16 else (dx, dg)

    ins = [x, dy] + ([res] if res is not None else [])
    outs = [(c, F32)] + ([(c, BF16)] if with_bf16 else [])
    return _ew(fn, ins, [g], outs, [c], name=name)


S5_W = GROUPS_PER_BLOCK * SSM_STATE
S5_GW = GROUPS_PER_BLOCK * SSM_GROUP
S5_UNROLL = 8
S5_DA_ROWS = 272


def _s5_scan_in_place(ref, pw_ref, *, reverse):
    lp = ref.shape[0]
    tile_rows = 8
    chunk = _tile(lp, S5_DA_ROWS, tile_rows)
    tiles = chunk // tile_rows

    def chunk_body(c, carry):
        rows = pl.ds(pl.multiple_of(c * chunk, tile_rows), chunk)
        xr, xi = ref[rows, :S5_W], ref[rows, S5_W:]
        for level, k in enumerate((1, 2, 4)):
            base = tile_rows * (1 + level)
            mr, mi = pw_ref[base:base + tile_rows, :S5_W][None], pw_ref[base:base + tile_rows, S5_W:][None]
            shift = chunk - k if reverse else k
            sr = pltpu.roll(xr, shift, 0).reshape(tiles, tile_rows, S5_W)
            si = pltpu.roll(xi, shift, 0).reshape(tiles, tile_rows, S5_W)
            xr = xr + (mr * sr - mi * si).reshape(chunk, S5_W)
            xi = xi + (mr * si + mi * sr).reshape(chunk, S5_W)
        ref[rows, :S5_W] = xr
        ref[rows, S5_W:] = xi
        return carry

    lax.fori_loop(0, lp // chunk, chunk_body, 0)

    pr, pi = pw_ref[0:tile_rows, :S5_W], pw_ref[0:tile_rows, S5_W:]
    ntile = lp // tile_rows
    unroll = 4

    def step(n, carry):
        cr, ci = carry
        for q in range(unroll):
            j = n * unroll + q
            j = ntile - 1 - j if reverse else j
            rows = pl.ds(pl.multiple_of(j * tile_rows, tile_rows), tile_rows)
            nr = ref[rows, :S5_W] + (pr * cr - pi * ci)
            ni = ref[rows, S5_W:] + (pr * ci + pi * cr)
            ref[rows, :S5_W] = nr
            ref[rows, S5_W:] = ni
            cr, ci = (nr[0:1], ni[0:1]) if reverse else (nr[tile_rows - 1:], ni[tile_rows - 1:])
        return cr, ci

    z = jnp.zeros((1, S5_W), F32)
    lax.fori_loop(0, ntile // unroll, step, (z, z))


def _s5_fwd(z, bb_band, cc_band, a_l, cfg, *, name):
    lp, ds, nl = cfg.LP, cfg.DS, cfg.NL

    def body(u_ref, bb_ref, cc_ref, a_ref, hs_ref, y_ref):
        hs_ref[...] = jnp.dot(u_ref[...].astype(BF16), bb_ref[...], preferred_element_type=F32)
        _s5_scan_in_place(hs_ref, a_ref, reverse=False)
        y_ref[...] = jnp.dot(hs_ref[...].astype(BF16), cc_ref[...], preferred_element_type=F32)

    return _pc(body, name=name, grid=(cfg.NB,),
               in_specs=[pl.BlockSpec((lp, S5_GW), lambda j: (0, j)), pl.BlockSpec((S5_GW, 2 * S5_W), lambda j: (j, 0)),
                         pl.BlockSpec((2 * S5_W, S5_GW), lambda j: (j, 0)), pl.BlockSpec((32, 2 * S5_W), lambda j: (0, j))],
               out_specs=[pl.BlockSpec((lp, 2 * S5_W), lambda j: (0, j)), pl.BlockSpec((lp, S5_GW), lambda j: (0, j))],
               out_shape=[jax.ShapeDtypeStruct((lp, nl), F32), jax.ShapeDtypeStruct((lp, ds), F32)],
               compiler_params=pltpu.CompilerParams(dimension_semantics=("parallel",)))(z, bb_band, cc_band, a_l)


def _s5_bwd(dy, hs, z, bb_band, cc_band, a_l, du_skip, cfg, *, name):
    lp, ds, nl = cfg.LP, cfg.DS, cfg.NL
    nt = (((1,), (1,)), ((), ()))
    tn = (((0,), (0,)), ((), ()))

    def body(dy_ref, hs_ref, u_ref, bb_ref, cc_ref, a_ref, sk_ref, du_ref, dbb_ref, dcc_ref, da_ref, g_ref):
        dyv = dy_ref[...]
        g_ref[...] = lax.dot_general(dyv, cc_ref[...], nt, preferred_element_type=F32)
        _s5_scan_in_place(g_ref, a_ref, reverse=True)
        dcc_ref[...] = lax.dot_general(hs_ref[...].astype(BF16), dyv, tn, preferred_element_type=F32)
        gb = g_ref[...].astype(BF16)
        dbb_ref[...] = lax.dot_general(u_ref[...].astype(BF16), gb, tn, preferred_element_type=F32)
        du_ref[...] = lax.dot_general(gb, bb_ref[...], nt, preferred_element_type=F32) + sk_ref[...]
        dre = jnp.zeros((1, S5_W), F32)
        dim = jnp.zeros((1, S5_W), F32)
        for r0 in range(0, lp, S5_DA_ROWS):
            rows = min(S5_DA_ROWS, lp - r0)
            first = lax.broadcasted_iota(jnp.int32, (rows, 1), 0) == 0
            prev = hs_ref[r0 - 1:r0, :] if r0 else jnp.zeros((1, 2 * S5_W), F32)
            hr = jnp.where(first, prev[:, :S5_W], pltpu.roll(hs_ref[r0:r0 + rows, :S5_W], 1, 0))
            hi = jnp.where(first, prev[:, S5_W:], pltpu.roll(hs_ref[r0:r0 + rows, S5_W:], 1, 0))
            gr, gi = g_ref[r0:r0 + rows, :S5_W], g_ref[r0:r0 + rows, S5_W:]
            dre = dre + jnp.sum(gr * hr + gi * hi, axis=0, keepdims=True)
            dim = dim + jnp.sum(gi * hr - gr * hi, axis=0, keepdims=True)
        da_ref[:, :S5_W] = dre
        da_ref[:, S5_W:] = dim

    col_blk = pl.BlockSpec((lp, S5_GW), lambda j: (0, j))
    lane_blk = pl.BlockSpec((lp, 2 * S5_W), lambda j: (0, j))
    bb_blk = pl.BlockSpec((S5_GW, 2 * S5_W), lambda j: (j, 0))
    cc_blk = pl.BlockSpec((2 * S5_W, S5_GW), lambda j: (j, 0))
    a_blk = pl.BlockSpec((1, 2 * S5_W), lambda j: (0, j))
    pw_blk = pl.BlockSpec((32, 2 * S5_W), lambda j: (0, j))
    return _pc(body, name=name, grid=(cfg.NB,),
               in_specs=[col_blk, lane_blk, col_blk, bb_blk, cc_blk, pw_blk, col_blk],
               out_specs=[col_blk, bb_blk, cc_blk, a_blk],
               out_shape=[jax.ShapeDtypeStruct((lp, ds), F32), jax.ShapeDtypeStruct((ds, 2 * S5_W), F32),
                          jax.ShapeDtypeStruct((nl, S5_GW), F32), jax.ShapeDtypeStruct((1, nl), F32)],
               scratch_shapes=[pltpu.VMEM((lp, 2 * S5_W), F32)],
               compiler_params=pltpu.CompilerParams(dimension_semantics=("parallel",)))(dy, hs, z, bb_band, cc_band, a_l, du_skip)


def _conv_gate(pre, cw, cb):
    return cw[0:1] * pltpu.roll(pre, 2, 0) + cw[1:2] * pltpu.roll(pre, 1, 0) + cw[2:3] * pre + cb


def _ffn_up(xn2, w_upt, cw, cb, *, name):
    lp, d = xn2.shape
    fp = w_upt.shape[0] // 2
    tc = _tile(fp, 256)
    nb = fp // tc

    def body(x_ref, wg_ref, wv_ref, cw_ref, cb_ref, up_ref, act_ref):
        wcat = jnp.concatenate([wg_ref[...], wv_ref[...]], axis=0)
        r = lax.dot_general(x_ref[...], wcat, (((1,), (1,)), ((), ())), preferred_element_type=F32)
        pre, val = r[:, :tc].astype(BF16), r[:, tc:].astype(BF16)
        up_ref[0] = pre
        up_ref[1] = val
        gate = _conv_gate(pre.astype(F32), cw_ref[...], cb_ref[...])
        act_ref[...] = (jax.nn.silu(gate) * val.astype(F32)).astype(BF16)

    return _pc(body, name=name, grid=(nb,),
               in_specs=[pl.BlockSpec((lp, d), lambda j: (0, 0)), pl.BlockSpec((tc, d), lambda j: (j, 0)),
                         pl.BlockSpec((tc, d), lambda j: (nb + j, 0)),
                         pl.BlockSpec((3, tc), lambda j: (0, j)), pl.BlockSpec((1, tc), lambda j: (0, j))],
               out_specs=[pl.BlockSpec((2, lp, tc), lambda j: (0, 0, j)), pl.BlockSpec((lp, tc), lambda j: (0, j))],
               out_shape=[jax.ShapeDtypeStruct((2, lp, fp), BF16), jax.ShapeDtypeStruct((lp, fp), BF16)],
               compiler_params=pltpu.CompilerParams(dimension_semantics=("parallel",)))(xn2, w_upt, w_upt, cw, cb)


def _ffn_dact(dh2, w_down, up, cw, cb, *, name):
    lp, d = dh2.shape
    fp = w_down.shape[0]
    tc = _tile(fp, 256)
    nb = fp // tc

    def body(dh_ref, wd_ref, up_ref, cw_ref, cb_ref, dup_ref, dcw_ref, dcb_ref):
        da = lax.dot_general(dh_ref[...], wd_ref[...], (((1,), (1,)), ((), ())), preferred_element_type=F32)
        pre, val, cwv = up_ref[0].astype(F32), up_ref[1].astype(F32), cw_ref[...]
        gate = _conv_gate(pre, cwv, cb_ref[...])
        sg = jax.nn.sigmoid(gate)
        dup_ref[1] = (da * (gate * sg)).astype(BF16)
        dgate = da * val * (sg * (1.0 + gate * (1.0 - sg)))
        dpre = cwv[2:3] * dgate + cwv[1:2] * pltpu.roll(dgate, lp - 1, 0) + cwv[0:1] * pltpu.roll(dgate, lp - 2, 0)
        dup_ref[0] = dpre.astype(BF16)
        dcb_ref[...] = jnp.sum(dgate, axis=0, keepdims=True)
        dcw_ref[0:1, :] = jnp.sum(dgate * pltpu.roll(pre, 2, 0), axis=0, keepdims=True)
        dcw_ref[1:2, :] = jnp.sum(dgate * pltpu.roll(pre, 1, 0), axis=0, keepdims=True)
        dcw_ref[2:3, :] = jnp.sum(dgate * pre, axis=0, keepdims=True)

    return _pc(body, name=name, grid=(nb,),
               in_specs=[pl.BlockSpec((lp, d), lambda j: (0, 0)), pl.BlockSpec((tc, d), lambda j: (j, 0)),
                         pl.BlockSpec((2, lp, tc), lambda j: (0, 0, j)),
                         pl.BlockSpec((3, tc), lambda j: (0, j)), pl.BlockSpec((1, tc), lambda j: (0, j))],
               out_specs=[pl.BlockSpec((2, lp, tc), lambda j: (0, 0, j)),
                          pl.BlockSpec((3, tc), lambda j: (0, j)), pl.BlockSpec((1, tc), lambda j: (0, j))],
               out_shape=[jax.ShapeDtypeStruct((2, lp, fp), BF16), jax.ShapeDtypeStruct((3, fp), F32),
                          jax.ShapeDtypeStruct((1, fp), F32)],
               compiler_params=pltpu.CompilerParams(dimension_semantics=("parallel",)))(dh2, w_down, up, cw, cb)


ATTN_Q_ROWS = 544


def _key_limit(i, tq, lp):
    return min(lp, -(-((i + 1) * tq) // LANE) * LANE)


def _attn_mask(i, tq, nk):
    qrow = i * tq + lax.broadcasted_iota(jnp.int32, (tq, 1), 0)
    krow = lax.broadcasted_iota(jnp.int32, (1, nk), 1)
    return (krow >= PAD) & ((krow // CHUNK) <= (qrow // CHUNK)), qrow >= PAD


def _attn_scores(q, kn, kr, i, tq, scale):
    nt = (((1,), (1,)), ((), ()))
    s = lax.dot_general(q[:, :QK_NOPE], kn, nt, preferred_element_type=F32)
    s = s + lax.dot_general(q[:, QK_NOPE:], kr, nt, preferred_element_type=F32)
    mask, qvalid = _attn_mask(i, tq, kn.shape[0])
    return jnp.where(mask, s * scale, jnp.finfo(F32).min), qvalid


def _per_q_block(nq, fn):
    i = pl.program_id(1)
    for blk in range(nq):
        pl.when(i == blk)(functools.partial(fn, blk))


def _attn_fwd(qx, kv, kr, cfg, *, name):
    lp, h = cfg.LP, cfg.H
    tq = _tile(lp, ATTN_Q_ROWS, ROW_ALIGN)
    nq = lp // tq
    scale = 1.0 / math.sqrt(QK_NOPE + QK_ROPE)

    def body(q_ref, kn_ref, v_ref, kr_ref, o_ref, lse_ref):
        def block(blk):
            nk = _key_limit(blk, tq, lp)
            s, qvalid = _attn_scores(q_ref[...], kn_ref[:nk], kr_ref[:nk], blk, tq, scale)
            m = jnp.max(s, axis=-1, keepdims=True)
            p = jnp.exp(s - m)
            l = jnp.sum(p, axis=-1, keepdims=True)
            o = jnp.dot(p.astype(BF16), v_ref[:nk], preferred_element_type=F32) / l
            o_ref[...] = jnp.where(qvalid, o, 0.0)
            lse_ref[...] = m + jnp.log(l)

        _per_q_block(nq, block)

    return _pc(body, name=name, grid=(h, nq),
               in_specs=[pl.BlockSpec((tq, HEAD_SLOT), lambda hh, i: (i, hh)),
                         pl.BlockSpec((lp, QK_NOPE), lambda hh, i: (0, 2 * hh)),
                         pl.BlockSpec((lp, V_HEAD), lambda hh, i: (0, 2 * hh + 1)),
                         pl.BlockSpec((lp, LANE), lambda hh, i: (0, 0))],
               out_specs=[pl.BlockSpec((tq, V_HEAD), lambda hh, i: (i, hh)),
                          pl.BlockSpec((None, tq, 1), lambda hh, i: (hh, i, 0))],
               out_shape=[jax.ShapeDtypeStruct((lp, h * V_HEAD), F32), jax.ShapeDtypeStruct((h, lp, 1), F32)],
               compiler_params=pltpu.CompilerParams(dimension_semantics=("parallel", "parallel")))(qx, kv, kv, kr)


def _attn_bwd(qx, kv, kr, o, lse, do, cfg, *, name):
    lp, h = cfg.LP, cfg.H
    tq = _tile(lp, ATTN_Q_ROWS, ROW_ALIGN)
    nq = lp // tq
    scale = 1.0 / math.sqrt(QK_NOPE + QK_ROPE)
    tn_dims = (((0,), (0,)), ((), ()))

    def body(q_ref, kn_ref, v_ref, kr_ref, o_ref, lse_ref, do_ref, dq_ref, dkv_ref, dkr_ref, dkv_acc):
        hh, i = pl.program_id(0), pl.program_id(1)

        @pl.when(i == 0)
        def _():
            dkv_acc[...] = jnp.zeros_like(dkv_acc)

        @pl.when((i == 0) & (hh == 0))
        def _():
            dkr_ref[...] = jnp.zeros_like(dkr_ref)

        def block(blk):
            nk = _key_limit(blk, tq, lp)
            q, kn, v, krv = q_ref[...], kn_ref[:nk], v_ref[:nk], kr_ref[:nk]
            s, qvalid = _attn_scores(q, kn, krv, blk, tq, scale)
            dov = jnp.where(qvalid, do_ref[...], 0.0)
            p = jnp.exp(s - lse_ref[...])
            delta = jnp.sum(dov * o_ref[...], axis=-1, keepdims=True)
            dob = dov.astype(BF16)
            dp = lax.dot_general(dob, v, (((1,), (1,)), ((), ())), preferred_element_type=F32)
            ds = (p * (dp - delta) * scale).astype(BF16)
            dq_ref[:, :QK_NOPE] = jnp.dot(ds, kn, preferred_element_type=F32)
            dq_ref[:, QK_NOPE:] = jnp.dot(ds, krv, preferred_element_type=F32)
            dkv_acc[:nk, :QK_NOPE] += lax.dot_general(ds, q[:, :QK_NOPE], tn_dims, preferred_element_type=F32)
            dkv_acc[:nk, QK_NOPE:] += lax.dot_general(p.astype(BF16), dob, tn_dims, preferred_element_type=F32)
            dkr_ref[:nk, :] += lax.dot_general(ds, q[:, QK_NOPE:], tn_dims, preferred_element_type=F32)

        _per_q_block(nq, block)

        @pl.when(i == nq - 1)
        def _():
            dkv_ref[...] = dkv_acc[...].astype(BF16)

    return _pc(body, name=name, grid=(h, nq),
               in_specs=[pl.BlockSpec((tq, HEAD_SLOT), lambda hh, i: (i, hh)),
                         pl.BlockSpec((lp, QK_NOPE), lambda hh, i: (0, 2 * hh)),
                         pl.BlockSpec((lp, V_HEAD), lambda hh, i: (0, 2 * hh + 1)),
                         pl.BlockSpec((lp, LANE), lambda hh, i: (0, 0)),
                         pl.BlockSpec((tq, V_HEAD), lambda hh, i: (i, hh)),
                         pl.BlockSpec((None, tq, 1), lambda hh, i: (hh, i, 0)),
                         pl.BlockSpec((tq, V_HEAD), lambda hh, i: (i, hh))],
               out_specs=[pl.BlockSpec((tq, HEAD_SLOT), lambda hh, i: (i, hh)),
                          pl.BlockSpec((lp, QK_NOPE + V_HEAD), lambda hh, i: (0, hh)),
                          pl.BlockSpec((lp, LANE), lambda hh, i: (0, 0))],
               out_shape=[jax.ShapeDtypeStruct((lp, h * HEAD_SLOT), F32),
                          jax.ShapeDtypeStruct((lp, h * (QK_NOPE + V_HEAD)), BF16),
                          jax.ShapeDtypeStruct((lp, LANE), F32)],
               scratch_shapes=[pltpu.VMEM((lp, QK_NOPE + V_HEAD), F32)],
               compiler_params=pltpu.CompilerParams(dimension_semantics=("arbitrary", "arbitrary")))(qx, kv, kv, kr, o, lse, do)


def _rot_half(x):
    lane = lax.broadcasted_iota(jnp.int32, x.shape, 1)
    half = QK_ROPE // 2
    return jnp.where(lane < half, -pltpu.roll(x, LANE - half, 1), pltpu.roll(x, half, 1))


def _rope(x, cos, sin):
    return x * cos + _rot_half(x) * sin


def _unrope(dy, cos, sin):
    return dy * cos - _rot_half(dy * sin)


def _rope_heads(fn, h):
    def apply(rid, q, cos, sin):
        parts = []
        for hh in range(h):
            parts.append(q[:, hh * HEAD_SLOT: hh * HEAD_SLOT + QK_NOPE])
            parts.append(fn(q[:, hh * HEAD_SLOT + QK_NOPE: (hh + 1) * HEAD_SLOT], cos, sin))
        return jnp.concatenate(parts, axis=1)
    return apply


ANY = pl.BlockSpec(memory_space=pl.ANY)


def _place():
    x, y, c = lax.axis_index("x"), lax.axis_index("y"), lax.axis_index("c")
    chips = [(1 - x, y), (x, 1 - y), (1 - x, 1 - y)]
    return x, y, c, chips


def _rcopy(src, dst, send_sem, recv_sem, dev):
    return pltpu.make_async_remote_copy(src_ref=src, dst_ref=dst, send_sem=send_sem, recv_sem=recv_sem,
                                        device_id=dev, device_id_type=MESH)


def _place_shard(shard, dtype, *, name, order=None, rows_to=None):
    shard = shard if shard.ndim == 3 else shard[None]
    n, r, cols = shard.shape
    rp = rows_to or r
    tm = _tile(r, max(ROW_ALIGN, PLACE_BLOCK_BYTES // (4 * cols)), ROW_ALIGN)
    me = (2 * lax.axis_index("x") + lax.axis_index("y")).astype(jnp.int32).reshape(1)
    extra = [] if order is None else [order]

    def body(me_ref, s_ref, *rest):
        rest[-1][...] = s_ref[...].astype(dtype)

    full = _pc(body, name=name,
               grid_spec=pltpu.PrefetchScalarGridSpec(
                   num_scalar_prefetch=1, grid=(n, r // tm),
                   in_specs=[pl.BlockSpec((None, tm, cols), lambda q, i, mr: (q, i, 0))] + [ANY] * len(extra),
                   out_specs=pl.BlockSpec((None, tm, cols), lambda q, i, mr: (mr[0] * n + q, i, 0))),
               out_shape=jax.ShapeDtypeStruct((4 * n, rp, cols), dtype),
               compiler_params=pltpu.CompilerParams(dimension_semantics=("arbitrary", "arbitrary")))(me, shard, *extra)
    if rp > r:
        pad = rp - r
        assert r % pad == 0

        def zero(me_ref, f_ref, o_ref):
            o_ref[...] = jnp.zeros_like(o_ref)

        full = _pc(zero, name=name + "_pad",
                   grid_spec=pltpu.PrefetchScalarGridSpec(
                       num_scalar_prefetch=1, grid=(n,), in_specs=[ANY],
                       out_specs=pl.BlockSpec((None, pad, cols), lambda q, mr: (mr[0] * n + q, r // pad, 0))),
                   out_shape=jax.ShapeDtypeStruct(full.shape, dtype), input_output_aliases={1: 0},
                   compiler_params=pltpu.CompilerParams(dimension_semantics=("arbitrary",)))(me, full)
    return full.reshape(4 * n * rp, cols)


def _allgather(fulls, *, name):
    n = len(fulls)

    def body(*refs):
        outs = refs[n:2 * n]
        send_sems, recv_sems = refs[2 * n:]
        x, y, c, chips = _place()
        sib = (x, y, 1 - c)
        me = 2 * x + y

        def rows(t, s, half):
            hrows = outs[t].shape[0] // 8
            return outs[t].at[pl.ds((2 * s + half) * hrows, hrows)]

        sent = []
        for t in range(n):
            for j, (cx, cy) in enumerate(chips):
                cp = _rcopy(rows(t, me, c), rows(t, me, c), send_sems.at[6 * t + j], recv_sems.at[6 * t + j], (cx, cy, c))
                cp.start()
                sent.append(cp)
        for t in range(n):
            for j, (cx, cy) in enumerate(chips):
                landed = rows(t, 2 * cx + cy, c)
                _rcopy(landed, landed, send_sems.at[6 * t + j], recv_sems.at[6 * t + j], (cx, cy, c)).wait_recv()
                cp = _rcopy(landed, landed, send_sems.at[6 * t + 3 + j], recv_sems.at[6 * t + 3 + j], sib)
                cp.start()
                sent.append(cp)
        for t in range(n):
            for j, (cx, cy) in enumerate(chips):
                other = rows(t, 2 * cx + cy, 1 - c)
                _rcopy(other, other, send_sems.at[6 * t + 3 + j], recv_sems.at[6 * t + 3 + j], sib).wait_recv()
        for cp in sent:
            cp.wait_send()

    return _pc(body, name=name, in_specs=[ANY] * n, out_specs=[ANY] * n,
               out_shape=[jax.ShapeDtypeStruct(f.shape, f.dtype) for f in fulls],
               input_output_aliases={t: t for t in range(n)},
               scratch_shapes=[pltpu.SemaphoreType.DMA((6 * n,)), pltpu.SemaphoreType.DMA((6 * n,))])(*fulls)


HBM = pl.BlockSpec(memory_space=pltpu.HBM)
SEM = pl.BlockSpec(memory_space=pltpu.SEMAPHORE)
EFFECT = pltpu.SideEffectType.DATAFLOW_SIDE_EFFECTING
TOKEN = jax.ShapeDtypeStruct((8, LANE), F32)


def _in_hbm(a):
    return pltpu.with_memory_space_constraint(a, pltpu.HBM)


def _half_rows(ref, s, half):
    hrows = ref.shape[0] // 8
    return ref.at[pl.ds((2 * s + half) * hrows, hrows)]


def _split_start(bufs, copies, n_copies, *, name, before=None):
    n = len(bufs)
    extra = [] if before is None else [before]

    def body(*refs):
        send_sems, recv_sems, token = refs[n + len(extra)], refs[n + len(extra) + 1], refs[-1]
        for k, (src, dst, dev) in enumerate(copies(refs[:n])):
            _rcopy(src, dst, send_sems.at[k], recv_sems.at[k], dev).start()
        token[...] = jnp.zeros_like(token)

    res = _pc(body, name=name, in_specs=[HBM] * n + [ANY] * len(extra),
              out_specs=[SEM, SEM] + [HBM] * n + [pl.BlockSpec(memory_space=pltpu.VMEM)],
              out_shape=[pltpu.SemaphoreType.DMA((n_copies,)), pltpu.SemaphoreType.DMA((n_copies,))]
              + [pltpu.HBM(b.shape, b.dtype) for b in bufs] + [TOKEN],
              input_output_aliases={t: 2 + t for t in range(n)},
              compiler_params=pltpu.CompilerParams(has_side_effects=EFFECT))(*[_in_hbm(b) for b in bufs], *extra)
    return res[0], res[1], list(res[2:2 + n]), res[-1]


def _split_wait(send_sems, recv_sems, bufs, copies, after, *, name):
    n = len(bufs)
    after = list(after) if isinstance(after, (list, tuple)) else [after]

    def body(*refs):
        send_ref, recv_ref = refs[n], refs[n + 1]
        for k, (src, dst, dev) in enumerate(copies(refs[:n])):
            cp = _rcopy(src, dst, send_ref.at[k], recv_ref.at[k], dev)
            cp.wait_send()
            cp.wait_recv()

    return _pc(body, name=name, in_specs=[HBM] * n + [SEM, SEM] + [ANY] * len(after), out_specs=[HBM] * n,
               out_shape=[pltpu.HBM(b.shape, b.dtype) for b in bufs],
               input_output_aliases={t: t for t in range(n)},
               compiler_params=pltpu.CompilerParams(has_side_effects=EFFECT))(*bufs, send_sems, recv_sems, *after)


def _allgather_ici_copies(refs):
    x, y, c, chips = _place()
    return [(_half_rows(r, 2 * x + y, c), _half_rows(r, 2 * x + y, c), (cx, cy, c)) for r in refs for cx, cy in chips]


def _rs_chips_copies(refs):
    x, y, c, chips = _place()
    n = len(refs) // 2
    return [(refs[t].at[2 * cx + cy], refs[n + t].at[j], (cx, cy, c)) for t in range(n) for j, (cx, cy) in enumerate(chips)]


def _allgather_forward_copies(refs):
    x, y, c, chips = _place()
    return [(_half_rows(r, 2 * cx + cy, c), _half_rows(r, 2 * cx + cy, c), (x, y, 1 - c)) for r in refs for cx, cy in chips]


def _rs_final_copies(refs):
    x, y, c, _ = _place()
    return [(r.at[c], r.at[c], (x, y, 1 - c)) for r in refs]


def _rs_sibling_copies(refs):
    x, y, c, _ = _place()
    n = len(refs) // 2
    out = []
    for t in range(n):
        h = refs[t].shape[0] // 8
        out += [(refs[t].at[pl.ds((2 * s + 1 - c) * h, h)], refs[n + t].at[s], (x, y, 1 - c)) for s in range(4)]
    return out


def _allgather_forward(fulls, *, name):
    n = len(fulls)

    def body(*refs):
        outs = refs[n:2 * n]
        send_sems, recv_sems = refs[2 * n:]
        x, y, c, chips = _place()
        sent = []
        for t in range(n):
            for j, (cx, cy) in enumerate(chips):
                landed = _half_rows(outs[t], 2 * cx + cy, c)
                cp = _rcopy(landed, landed, send_sems.at[3 * t + j], recv_sems.at[3 * t + j], (x, y, 1 - c))
                cp.start()
                sent.append(cp)
        for t in range(n):
            for j, (cx, cy) in enumerate(chips):
                other = _half_rows(outs[t], 2 * cx + cy, 1 - c)
                _rcopy(other, other, send_sems.at[3 * t + j], recv_sems.at[3 * t + j], (x, y, 1 - c)).wait_recv()
        for cp in sent:
            cp.wait_send()

    return _pc(body, name=name, in_specs=[ANY] * n, out_specs=[ANY] * n,
               out_shape=[jax.ShapeDtypeStruct(f.shape, f.dtype) for f in fulls],
               input_output_aliases={t: t for t in range(n)},
               scratch_shapes=[pltpu.SemaphoreType.DMA((3 * n,)), pltpu.SemaphoreType.DMA((3 * n,))])(*fulls)


def _rs_sibling(grads, *, name):
    n = len(grads)

    def body(*refs):
        ins, outs = refs[:n], refs[n:2 * n]
        send_sems, recv_sems = refs[2 * n:]
        x, y, c, _ = _place()
        cps = []
        for t in range(n):
            h = ins[t].shape[0] // 8
            for s in range(4):
                cp = _rcopy(ins[t].at[pl.ds((2 * s + 1 - c) * h, h)], outs[t].at[s], send_sems.at[4 * t + s],
                            recv_sems.at[4 * t + s], (x, y, 1 - c))
                cp.start()
                cps.append(cp)
        for cp in cps:
            cp.wait()

    return _pc(body, name=name, in_specs=[ANY] * n, out_specs=[ANY] * n,
               out_shape=[jax.ShapeDtypeStruct((4, g.shape[0] // 8, g.shape[1]), g.dtype) for g in grads],
               scratch_shapes=[pltpu.SemaphoreType.DMA((4 * n,)), pltpu.SemaphoreType.DMA((4 * n,))])(*grads)


def _rs_chips(sends, *, name):
    n = len(sends)

    def body(*refs):
        s_refs, b_refs = refs[:n], refs[n:2 * n]
        send_sems, recv_sems = refs[2 * n:]
        x, y, c, chips = _place()
        cps = []
        for t in range(n):
            for j, (cx, cy) in enumerate(chips):
                cp = _rcopy(s_refs[t].at[2 * cx + cy], b_refs[t].at[j], send_sems.at[3 * t + j], recv_sems.at[3 * t + j],
                            (cx, cy, c))
                cp.start()
                cps.append(cp)
        for cp in cps:
            cp.wait()

    return _pc(body, name=name, in_specs=[ANY] * n, out_specs=[ANY] * n,
               out_shape=[jax.ShapeDtypeStruct((3,) + s.shape[1:], s.dtype) for s in sends],
               scratch_shapes=[pltpu.SemaphoreType.DMA((3 * n,)), pltpu.SemaphoreType.DMA((3 * n,))])(*sends)


def _rs_final(fulls, *, name):
    n = len(fulls)

    def body(*refs):
        outs = refs[n:2 * n]
        send_sems, recv_sems = refs[2 * n:]
        x, y, c, _ = _place()
        cps = []
        for t in range(n):
            cp = _rcopy(outs[t].at[c], outs[t].at[c], send_sems.at[t], recv_sems.at[t], (x, y, 1 - c))
            cp.start()
            cps.append(cp)
        for cp in cps:
            cp.wait()

    return _pc(body, name=name, in_specs=[ANY] * n, out_specs=[ANY] * n,
               out_shape=[jax.ShapeDtypeStruct(f.shape, f.dtype) for f in fulls],
               input_output_aliases={t: t for t in range(n)},
               scratch_shapes=[pltpu.SemaphoreType.DMA((n,)), pltpu.SemaphoreType.DMA((n,))])(*fulls)


def _add_halves(g, a, send_dtype, *, name):
    _, h, cols = a.shape
    th = _row_tile(h, cols)
    g4 = g.reshape(4, 2, h, cols)
    idx = jnp.stack([lax.axis_index("c"), 2 * lax.axis_index("x") + lax.axis_index("y")]).astype(jnp.int32)

    def shard(k, ir):
        return (ir[1] + 1 + k) % 4

    def body(idx_ref, g_ref, a_ref, p_ref, s_ref):
        v = g_ref[...].astype(F32) + a_ref[...].astype(F32)
        s_ref[...] = v.astype(send_dtype)

        @pl.when(pl.program_id(1) == 3)
        def _():
            p_ref[...] = v

    return _pc(body, name=name,
               grid_spec=pltpu.PrefetchScalarGridSpec(
                   num_scalar_prefetch=1, grid=(h // th, 4),
                   in_specs=[pl.BlockSpec((None, None, th, cols), lambda i, k, ir: (shard(k, ir), ir[0], i, 0)),
                             pl.BlockSpec((None, th, cols), lambda i, k, ir: (shard(k, ir), i, 0))],
                   out_specs=[pl.BlockSpec((th, cols), lambda i, k, ir: (i, 0)),
                              pl.BlockSpec((None, th, cols), lambda i, k, ir: (shard(k, ir), i, 0))]),
               out_shape=[jax.ShapeDtypeStruct((h, cols), F32), jax.ShapeDtypeStruct(a.shape, send_dtype)],
               compiler_params=pltpu.CompilerParams(dimension_semantics=("arbitrary", "arbitrary")))(idx, g4, a)


def _add_chips(p, b, *, name, order=None):
    h, cols = p.shape
    th = _row_tile(h, cols)
    idx = lax.axis_index("c").astype(jnp.int32).reshape(1)
    extra = [] if order is None else [order]

    def body(idx_ref, p_ref, b_ref, *rest):
        r_ref = rest[-1]
        r_ref[...] = ((p_ref[...] + b_ref[0].astype(F32)) + b_ref[1].astype(F32)) + b_ref[2].astype(F32)

    return _pc(body, name=name,
               grid_spec=pltpu.PrefetchScalarGridSpec(
                   num_scalar_prefetch=1, grid=(h // th,),
                   in_specs=[pl.BlockSpec((th, cols), lambda i, ir: (i, 0)),
                             pl.BlockSpec((3, th, cols), lambda i, ir: (0, i, 0))] + [ANY] * len(extra),
                   out_specs=pl.BlockSpec((None, th, cols), lambda i, ir: (ir[0], i, 0))),
               out_shape=jax.ShapeDtypeStruct((2, h, cols), F32),
               compiler_params=pltpu.CompilerParams(dimension_semantics=("arbitrary",)))(idx, p, b, *extra)


def _add_halves_all(grads, recv, send_dtypes, tag):
    parts, sends = [], []
    for t, (g, a) in enumerate(zip(grads, recv)):
        p, s = _add_halves(g, a, send_dtypes[t], name=f"rs_add_halves_{tag}{t}")
        parts.append(p)
        sends.append(s)
    return parts, sends


def _rs_finish(parts, others, tag, order=None):
    halves = [_add_chips(p, b, order=order, name=f"rs_add_chips_{tag}{t}") for t, (p, b) in enumerate(zip(parts, others))]
    full = _rs_final(halves, name=f"rs_final_{tag}")
    return [f.reshape(-1, f.shape[-1]) for f in full]


def _s5_discretize(lam_re, lam_im, log_dt, b_re, b_im):
    lam = lax.complex(lam_re, lam_im)
    dt = jnp.exp(log_dt)[:, None]
    lam_bar = jnp.exp(lam * dt)
    b_bar = ((lam_bar - 1.0) / lam)[..., None] * lax.complex(b_re, b_im)
    return jnp.real(lam_bar), jnp.imag(lam_bar), jnp.real(b_bar), jnp.imag(b_bar)


def _lanes_from_gp(re, im, cfg):
    v = jnp.stack([re, im]).reshape(2, cfg.NB, GROUPS_PER_BLOCK, SSM_STATE)
    return jnp.transpose(v, (1, 0, 2, 3)).reshape(1, cfg.NL)


def _gp_from_lanes(v, cfg):
    v = jnp.transpose(v.reshape(cfg.NB, 2, GROUPS_PER_BLOCK, SSM_STATE), (1, 0, 2, 3)).reshape(2, cfg.G, SSM_STATE)
    return v[0], v[1]


def _bb_band(bb_re, bb_im, cfg):
    eye = jnp.eye(GROUPS_PER_BLOCK, dtype=F32)
    bb = jnp.stack([bb_re, bb_im]).reshape(2, cfg.NB, GROUPS_PER_BLOCK, SSM_STATE, SSM_GROUP)
    return jnp.einsum('rjgpc,gh->jgcrhp', bb, eye).reshape(cfg.DS, 2 * GROUPS_PER_BLOCK * SSM_STATE)


def _bb_from_band(m, cfg):
    eye = jnp.eye(GROUPS_PER_BLOCK, dtype=F32)
    m = m.reshape(cfg.NB, GROUPS_PER_BLOCK, SSM_GROUP, 2, GROUPS_PER_BLOCK, SSM_STATE)
    v = jnp.einsum('jgcrhp,gh->rjgpc', m, eye).reshape(2, cfg.G, SSM_STATE, SSM_GROUP)
    return v[0], v[1]


def _cc_band(c_re, c_im, cfg):
    eye = jnp.eye(GROUPS_PER_BLOCK, dtype=F32)
    cc = jnp.stack([c_re, -c_im]).reshape(2, cfg.NB, GROUPS_PER_BLOCK, SSM_GROUP, SSM_STATE)
    return jnp.einsum('rjgcp,gh->jrhpgc', cc, eye).reshape(cfg.NL, GROUPS_PER_BLOCK * SSM_GROUP)


def _cc_from_band(m, cfg):
    eye = jnp.eye(GROUPS_PER_BLOCK, dtype=F32)
    m = m.reshape(cfg.NB, 2, GROUPS_PER_BLOCK, SSM_STATE, GROUPS_PER_BLOCK, SSM_GROUP)
    v = jnp.einsum('jrhpgc,gh->rjgcp', m, eye).reshape(2, cfg.G, SSM_GROUP, SSM_STATE)
    return v[0], -v[1]


PACK_COLS = 512
PACK_ROW_ALIGN = 64


def _pack(arrs):
    flat = jnp.concatenate([a.reshape(-1).astype(F32) for a in arrs])
    unit = PACK_COLS * PACK_ROW_ALIGN
    total = -(-flat.shape[0] // unit) * unit
    return jnp.pad(flat, (0, total - flat.shape[0])).reshape(-1, PACK_COLS)


def _unpack(p, shapes):
    flat = p.reshape(-1)
    out, off = [], 0
    for shp in shapes:
        size = math.prod(shp)
        out.append(flat[off:off + size].reshape(shp))
        off += size
    return out


def _adamw(w, g, m, v, *, name, emit_grad=False):
    c1 = 1.0 / (1.0 - ADAM_B1 ** ADAM_STEP)
    c2 = 1.0 / (1.0 - ADAM_B2 ** ADAM_STEP)

    if w.ndim == 2:
        outs = _adamw(w[None], g[None], m[None], v[None], name=name, emit_grad=emit_grad)
        return [o[0] for o in outs]
    lead, rows, cols = w.shape
    tc = _tile(cols, 512)
    tm = _tile(rows, max(8, ADAMW_BLOCK_BYTES // (4 * tc)), 8)
    n_out = 4 if emit_grad else 3

    def body(w_ref, g_ref, m_ref, v_ref, *o_refs):
        gv = g_ref[...]
        mn = ADAM_B1 * m_ref[...] + (1.0 - ADAM_B1) * gv
        vn = ADAM_B2 * v_ref[...] + (1.0 - ADAM_B2) * (gv * gv)
        delta = -ADAM_LR * ((mn * c1) / (jnp.sqrt(vn * c2) + ADAM_EPS) + ADAM_WD * w_ref[...])
        for o_ref, val in zip(o_refs, ((gv, delta, mn, vn) if emit_grad else (delta, mn, vn))):
            o_ref[...] = val

    blk = pl.BlockSpec((None, tm, tc), lambda n, i, j: (n, i, j))
    return _pc(body, name=name, grid=(lead, rows // tm, cols // tc), in_specs=[blk] * 4, out_specs=[blk] * n_out,
               out_shape=[jax.ShapeDtypeStruct((lead, rows, cols), F32)] * n_out,
               compiler_params=pltpu.CompilerParams(dimension_semantics=("parallel", "parallel", "parallel")))(w, g, m, v)


def _to_comm_layout(name, w, cfg):
    w = w[0]
    if name == 'w_in':
        return jnp.pad(w, ((0, 0), (0, cfg.DINP - cfg.DIN)))
    if name == 'w_q_b':
        hs = w.shape[1] // (QK_NOPE + QK_ROPE)
        wt = w.T.reshape(hs, QK_NOPE + QK_ROPE, cfg.QL)
        return jnp.pad(wt, ((0, 0), (0, HEAD_SLOT - QK_NOPE - QK_ROPE), (0, 0))).reshape(hs * HEAD_SLOT, cfg.QL)
    if name == 'w_kv_b':
        return w.T
    if name == 'w_up':
        return w.T.reshape(2, cfg.F // 4, cfg.D)
    return w


def _from_comm_layout(name, g, cfg):
    if name == 'w_in':
        g = g[:, :cfg.DIN]
    elif name == 'w_q_b':
        hs = g.shape[0] // HEAD_SLOT
        g = g.reshape(hs, HEAD_SLOT, cfg.QL)[:, :QK_NOPE + QK_ROPE].reshape(hs * (QK_NOPE + QK_ROPE), cfg.QL).T
    elif name == 'w_kv_b':
        g = g.T
    elif name == 'w_up':
        g = g.reshape(2, cfg.FQ, cfg.D)[:, :cfg.F // 4].reshape(cfg.F // 2, cfg.D).T
    elif name == 'w_down':
        g = g[:cfg.F // 4]
    return g[None]


def _ff_pad(v, cfg):
    k = v.shape[0]
    return jnp.pad(v.reshape(k, 4, cfg.F // 4), ((0, 0), (0, 0), (0, cfg.FQ - cfg.F // 4))).reshape(k, cfg.FP)


def _ff_unpad(v, cfg):
    k = v.shape[0]
    return v.reshape(k, 4, cfg.FQ)[:, :, :cfg.F // 4].reshape(k, cfg.F)


def _step(cfg, w, m, v, x, loss_target):
    lp, d, ds, nl = cfg.LP, cfg.D, cfg.DS, cfg.NL
    xi, yi = lax.axis_index("x"), lax.axis_index("y")
    me = 2 * xi + yi

    def place(n, order=None):
        rows_to = cfg.FQ if n in ('w_up', 'w_down') else None
        return _place_shard(_to_comm_layout(n, w[n], cfg), BF16, order=order, rows_to=rows_to, name=f"place_{n}")

    first = [place('w_in'), _place_shard(w['meta_tokens'], F32, name="place_meta")]
    f_send, f_recv, f_flying, f_token = _split_start(first, _allgather_ici_copies, 6, name="allgather_first_start")
    conv_w_shard = jnp.pad(w['conv_w'][0], ((0, ROW_ALIGN - 3), (0, cfg.FQ - cfg.F // 4)))
    mid = [place(n, f_token) for n in BIG[1:5]] + [_place_shard(conv_w_shard, F32, order=f_token, name="place_conv_w")]
    mid_send, mid_recv, mid_flying, mid_token = _split_start(mid, _allgather_ici_copies, 3 * len(mid), before=f_token,
                                                             name="allgather_mid_start")
    up_send, up_recv, up_flying, up_token = _split_start([place('w_up', mid_token)], _allgather_ici_copies, 3,
                                                         before=mid_token, name="allgather_up_start")
    dn_send, dn_recv, dn_flying, ffn_token = _split_start([place('w_down', up_token)], _allgather_ici_copies, 3,
                                                          before=up_token, name="allgather_down_start")
    conv_b = _ff_pad(w['conv_b'], cfg)

    pos = (jnp.arange(lp, dtype=jnp.int32) - PAD).astype(F32)
    inv_freq = 1.0 / (ROPE_BASE ** (jnp.arange(0, QK_ROPE, 2, dtype=F32) / QK_ROPE))
    ang = pos[:, None] * inv_freq[None, :]
    zpad = jnp.zeros((lp, LANE - QK_ROPE), F32)
    cos_t = jnp.concatenate([jnp.cos(ang), jnp.cos(ang), zpad], axis=1)
    sin_t = jnp.concatenate([jnp.sin(ang), jnp.sin(ang), zpad], axis=1)

    s5_in = (w['lam_re'][0], w['lam_im'][0], w['log_dt'][0], w['b_re'][0], w['b_im'][0])
    (a_re, a_im, bb_re, bb_im), s5_vjp = jax.vjp(_s5_discretize, *s5_in)
    lam_dt = lax.complex(s5_in[0], s5_in[1]) * jnp.exp(s5_in[2])[:, None]
    a_pow = jnp.exp(jnp.arange(1, 9, dtype=F32)[:, None, None] * lam_dt[None])
    r8 = jnp.arange(8)
    step_f = jnp.stack([jnp.where((r8 >= k)[:, None, None], a_pow[k - 1][None], 0.0) for k in (1, 2, 4)]).reshape(24, cfg.G, -1)
    step_b = jnp.stack([jnp.where((r8 < 8 - k)[:, None, None], a_pow[k - 1][None], 0.0) for k in (1, 2, 4)]).reshape(24, cfg.G, -1)
    rows_f = jnp.concatenate([a_pow, step_f])
    rows_b = jnp.conj(jnp.concatenate([a_pow[::-1], step_b]))

    def lane_rows(t):
        v = jnp.stack([jnp.real(t), jnp.imag(t)], axis=1).reshape(t.shape[0], 2, cfg.NB, GROUPS_PER_BLOCK, SSM_STATE)
        return jnp.transpose(v, (0, 2, 1, 3, 4)).reshape(t.shape[0], cfg.NL)

    pw_fwd, pw_bwd = lane_rows(rows_f), lane_rows(rows_b)
    bb_band = _bb_band(bb_re, bb_im, cfg).astype(BF16)
    cc_band = _cc_band(w['c_re'][0], w['c_im'][0], cfg).astype(BF16)
    d_skip, b_glu = w['d_skip'], w['b_glu']

    f_landed = _split_wait(f_send, f_recv, f_flying, _allgather_ici_copies, [ffn_token, cos_t, sin_t, pw_fwd, pw_bwd, bb_band, cc_band],
                           name="allgather_first_wait")
    w_in, meta_full = _allgather_forward(f_landed, name="allgather_first_forward")
    meta = jnp.transpose(meta_full.reshape(4, N_META, d // 4), (1, 0, 2)).reshape(N_META, d)
    mix_norm = w['mix_norm'] + ffn_token[0:1, 0:1]

    h0 = jnp.concatenate([jnp.zeros((PAD, d), F32), meta, x[0]], axis=0)
    xn = _rms_fwd(h0, mix_norm, name="rms_mix")
    z = _mm(xn, w_in, name="mm_in", tn=_tile(cfg.DINP, 640))
    u = (z, ds, 0)
    q_a = (z, cfg.QL, ds // cfg.QL)
    kv_a = (z, cfg.KVL, (ds + cfg.QL) // cfg.KVL)
    k_pe = (z, LANE, (ds + cfg.QL + cfg.KVL) // LANE)

    hs, yc = _s5_fwd(z, bb_band, cc_band, pw_fwd, cfg, name="s5_fwd")

    def s5_y(ycv, uv, dk):
        return ycv + dk * uv

    gl = _ew(lambda rid, ycv, uv, dk: jax.nn.gelu(s5_y(ycv, uv, dk)), [yc, u], [d_skip], [(ds, BF16)], name="s5_gelu")[0]
    mid_landed = _split_wait(mid_send, mid_recv, mid_flying, _allgather_ici_copies, gl, name="allgather_mid_wait")
    w_glu, w_qt, w_kvt, w_out, conv_full = _allgather_forward(mid_landed, name="allgather_mid_forward")
    conv_w = jnp.transpose(conv_full.reshape(4, ROW_ALIGN, cfg.FQ)[:, :3], (1, 0, 2)).reshape(3, cfg.FP)
    tg = _mm(gl, w_glu, name="mm_glu")
    ya = _ew(lambda rid, ycv, uv, tv, dk, bg: jax.nn.gelu(s5_y(ycv, uv, dk)) * jax.nn.sigmoid(tv + bg),
             [yc, u, tg], [d_skip, b_glu], [(ds, F32)], name="s5_glu")[0]

    qn = _rms_fwd(q_a, w['q_a_norm'], name="rms_q")
    kvn = _rms_fwd(kv_a, w['kv_a_norm'], name="rms_kv")
    q_raw = _mm(qn, w_qt, tb=True, name="mm_q")
    qx = _ew(_rope_heads(_rope, cfg.H), [q_raw, cos_t, sin_t], [], [(cfg.H * HEAD_SLOT, BF16)], name="rope_q")[0]
    kv = _mm(kvn, w_kvt, tb=True, out_dtype=BF16, name="mm_kv")
    kr = _ew(lambda rid, kp, cs, sn: _rope(kp, cs, sn), [k_pe, cos_t, sin_t], [], [(LANE, BF16)], name="rope_k")[0]
    o, lse = _attn_fwd(qx, kv, kr, cfg, name="attn_fwd")

    def norm2(rid, yav, ov, gs, ga):
        return jnp.concatenate([_rms_parts(yav, gs)[0] * gs, _rms_parts(ov, ga)[0] * ga], axis=1)

    up_landed = _split_wait(up_send, up_recv, up_flying, _allgather_ici_copies, o, name="allgather_up_wait")
    uf_send, uf_recv, uf_flying, uf_token = _split_start(up_landed, _allgather_forward_copies, 3,
                                                         name="allgather_up_forward_start")
    yn = _ew(norm2, [ya, o], [w['out_norm_ssm'] + uf_token[0:1, 0:1], w['out_norm_attn']], [(cfg.DMIX, BF16)],
             name="rms_out")[0]
    h1 = _mm(yn, w_out, res=h0, name="mm_out")
    xn2 = _rms_fwd(h1, w['ffn_norm'], name="rms_ffn")
    dn_landed = _split_wait(dn_send, dn_recv, dn_flying, _allgather_ici_copies, xn2, name="allgather_down_wait")
    df_send, df_recv, df_flying, df_token = _split_start(dn_landed, _allgather_forward_copies, 3,
                                                         name="allgather_down_forward_start")
    w_upt, = _split_wait(uf_send, uf_recv, uf_flying, _allgather_forward_copies, df_token,
                         name="allgather_up_forward_wait")
    up, act = _ffn_up(xn2, w_upt, conv_w, conv_b, name="ffn_up")
    w_down, = _split_wait(df_send, df_recv, df_flying, _allgather_forward_copies, act,
                          name="allgather_down_forward_wait")
    h2 = _mm(act, w_down, res=h1, tm=_tile(lp, 1088, ROW_ALIGN), name="mm_down")

    g_final = w['final_norm'].reshape(1, d)

    def head(rid, hv, tv, gv):
        xhat, r = _rms_parts(hv, gv)
        valid = rid >= PAD + N_META
        diff = jnp.where(valid, xhat * gv - tv, 0.0)
        dout = diff * (1.0 / d)
        dxhat = dout * gv
        dx = r * (dxhat - xhat * jnp.mean(dxhat * xhat, axis=-1, keepdims=True))
        return dx, dx, dout * xhat, 0.5 * diff * dout

    dh2, dh2_b, dg_final, loss_cols = _ew(head, [h2, (loss_target[0], d, 0, SKIP)], [g_final], [(d, F32), (d, BF16)], [d, d],
                                          tm=PAD + N_META, name="loss_head")
    loss = lax.psum(jnp.sum(loss_cols), ("x", "y", "c"))

    dw_down = _mm(act, dh2_b, ta=True, tn=d, tm=512, out_dtype=BF16, name="mm_dw_down")

    def sibling_start(g, tag):
        land = lax.empty((4, g.shape[0] // 8, g.shape[1]), g.dtype)
        return _split_start([g, land], _rs_sibling_copies, 4, name=f"rs_sibling_{tag}_start")

    dn_send, dn_recv, dn_flying, dn_token = sibling_start(dw_down, "down")
    dup, dconv_w, dconv_b = _ffn_dact(dh2_b, w_down, up, conv_w, conv_b + dn_token[0:1, 0:1], name="ffn_dact")
    tk_up, tm_up = _tile(cfg.FP, 2816), _tile(cfg.FP, 512)
    dw_upt = _mm(dup, xn2, ta=True, dims=(2 * cfg.FP, d, lp), tn=d, tm=tm_up, a_lead=True, out_dtype=BF16, name="mm_dw_up",
                 a_idx=lambda i, j, k: (i // (cfg.FP // tm_up), 0, i % (cfg.FP // tm_up)))
    up_send, up_recv, up_flying, up_token = sibling_start(dw_upt, "up")
    dxn2 = _mm(dup, w_upt, dims=(lp, d, 2 * cfg.FP), tk=tk_up, tn=512, a_lead=True, name="mm_dxn2",
               a_idx=lambda i, j, k: (k // (cfg.FP // tk_up), i, k % (cfg.FP // tk_up)))
    dh1, dh1_b, dg_ffn = _rms_bwd(h1, w['ffn_norm'] + up_token[0:1, 0:1], dxn2, res=dh2, mask=True, with_bf16=True,
                                  name="rms_ffn_bwd")

    dyn = _mm(dh1_b, w_out, tb=True, name="mm_dyn")
    dw_out = _mm(yn, dh1_b, ta=True, tn=d, tm=512, name="mm_dw_out")
    up_done = _split_wait(up_send, up_recv, up_flying, _rs_sibling_copies, dw_out, name="rs_sibling_up_wait")
    dn_done = _split_wait(dn_send, dn_recv, dn_flying, _rs_sibling_copies, dw_out, name="rs_sibling_down_wait")
    early_parts, early_sends = _add_halves_all([up_done[0], dn_done[0]], [up_done[1], dn_done[1]], [BF16] * 2, "early")
    chip_lands = [lax.empty((3,) + s.shape[1:], s.dtype) for s in early_sends]
    ch_send, ch_recv, ch_flying, ch_token = _split_start(early_sends + chip_lands, _rs_chips_copies, 6,
                                                         name="rs_chips_early_start")
    dya, dg_ssm = _rms_bwd(ya, w['out_norm_ssm'] + ch_token[0:1, 0:1], (dyn, ds, 0), name="rms_ssm_bwd")
    do, dg_attn = _rms_bwd(o, w['out_norm_attn'], (dyn, cfg.DATTN, ds // cfg.DATTN), name="rms_attn_bwd")

    dqx, dkv, dkr = _attn_bwd(qx, kv, kr, o, lse, do, cfg, name="attn_bwd")
    dq_raw = _ew(_rope_heads(_unrope, cfg.H), [dqx, cos_t, sin_t], [], [(cfg.H * HEAD_SLOT, BF16)], name="unrope_q")[0]
    dk_pe = _ew(lambda rid, dk, cs, sn: _unrope(dk, cs, sn), [dkr, cos_t, sin_t], [], [(LANE, F32)], name="unrope_k")[0]
    dqn = _mm(dq_raw, w_qt, name="mm_dqn")
    dw_qt = _mm(dq_raw, qn, ta=True, tm=512, name="mm_dw_q")
    dkvn = _mm(dkv, w_kvt, name="mm_dkvn")
    dw_kvt = _mm(dkv, kvn, ta=True, tm=512, name="mm_dw_kv")
    dq_a, dg_q = _rms_bwd(q_a, w['q_a_norm'], dqn, name="rms_q_bwd")
    dkv_a, dg_kv = _rms_bwd(kv_a, w['kv_a_norm'], dkvn, name="rms_kv_bwd")

    def glu_bwd(rid, ycv, uv, tv, dyav, dk, bg):
        gelu = jax.nn.gelu(s5_y(ycv, uv, dk))
        sg = jax.nn.sigmoid(tv + bg)
        dt = dyav * gelu * sg * (1.0 - sg)
        return dt, dyav * sg, dt

    dt_b, dgl1, db_glu = _ew(glu_bwd, [yc, u, tg, dya], [d_skip, b_glu], [(ds, BF16), (ds, F32)], [ds], name="s5_glu_bwd")
    dgl = _mm(dt_b, w_glu, tb=True, res=dgl1, name="mm_dgl")
    dw_glu = _mm(gl, dt_b, ta=True, tm=512, name="mm_dw_glu")

    def gelu_bwd(rid, ycv, uv, dglv, dk):
        _, vjp = jax.vjp(jax.nn.gelu, s5_y(ycv, uv, dk))
        dy = vjp(dglv)[0]
        return dy, dy * dk, dy * uv

    mid_grads = [dw_out, dw_glu, dw_qt, dw_kvt]
    mid_lands = [lax.empty((4, g.shape[0] // 8, g.shape[1]), g.dtype) for g in mid_grads]
    ms_send, ms_recv, ms_flying, ms_token = _split_start(mid_grads + mid_lands, _rs_sibling_copies, 4 * len(mid_grads),
                                                         name="rs_sibling_mid_start")
    dy_b, du_skip, dd_skip = _ew(gelu_bwd, [yc, u, dgl], [d_skip + ms_token[0:1, 0:1]], [(ds, BF16), (ds, F32)], [ds],
                                 name="s5_gelu_bwd")
    ms_done = _split_wait(ms_send, ms_recv, ms_flying, _rs_sibling_copies, dy_b, name="rs_sibling_mid_wait")
    mid_parts, mid_sends = _add_halves_all(ms_done[:4], ms_done[4:], [BF16] * 4, "mid")
    mid_chip_lands = [lax.empty((3,) + s.shape[1:], s.dtype) for s in mid_sends]
    mc_send, mc_recv, mc_flying, mc_token = _split_start(mid_sends + mid_chip_lands, _rs_chips_copies, 3 * len(mid_sends),
                                                         name="rs_chips_mid_start")
    du, dbb_band, dcc_band, da_l = _s5_bwd(dy_b, hs, z, bb_band, cc_band, pw_bwd + mc_token[0:1, 0:1], du_skip, cfg,
                                           name="s5_bwd")

    dz = jnp.concatenate([du, dq_a, dkv_a, dk_pe], axis=1).astype(BF16)
    dxn = _mm(dz, w_in, tb=True, name="mm_dxn")
    dw_in = _mm(xn, dz, ta=True, tm=512, tn=_tile(cfg.DINP, 1024), name="mm_dw_in")
    def mix_bwd(rid, xv, dyv, resv, gv):
        dx, dg = _rms_bwd_block(xv, gv, dyv)
        dx = dx + resv
        return dx, dx, dg

    grad_x, dh0_head, dg_mix = _ew(mix_bwd, [h0, dxn, dh1], [mix_norm], [(d, F32, SKIP), (d, F32, FIRST)], [d],
                                   tm=PAD + N_META, name="rms_mix_bwd")
    grad_x = grad_x[None]

    da_re, da_im = _gp_from_lanes(da_l, cfg)
    dbb_re, dbb_im = _bb_from_band(dbb_band, cfg)
    dlam_re, dlam_im, dlog_dt, db_re, db_im = s5_vjp((da_re, da_im, dbb_re, dbb_im))
    dc_re, dc_im = _cc_from_band(dcc_band, cfg)
    local_small = {
        'meta_tokens': dh0_head[PAD:], 'mix_norm': dg_mix, 'lam_re': dlam_re, 'lam_im': dlam_im, 'log_dt': dlog_dt,
        'b_re': db_re, 'b_im': db_im, 'c_re': dc_re, 'c_im': dc_im, 'd_skip': dd_skip, 'b_glu': db_glu, 'q_a_norm': dg_q,
        'kv_a_norm': dg_kv, 'out_norm_ssm': dg_ssm, 'out_norm_attn': dg_attn, 'ffn_norm': dg_ffn,
        'conv_w': _ff_unpad(dconv_w, cfg), 'conv_b': _ff_unpad(dconv_b, cfg), 'final_norm': dg_final,
    }
    small_shapes = [local_small[n].shape for n in SMALL]

    small_pack = _pack([local_small[n] for n in SMALL])
    ch_done = _split_wait(ch_send, ch_recv, ch_flying, _rs_chips_copies, small_pack, name="rs_chips_early_wait")
    early_halves = [_add_chips(p, b, name=f"rs_add_chips_early{t}") for t, (p, b) in enumerate(zip(early_parts, ch_done[2:]))]
    fe_send, fe_recv, fe_flying, fe_token = _split_start(early_halves, _rs_final_copies, 2, name="rs_final_early_start")
    end_local = [dw_in, small_pack + fe_token[0:1, 0:1]]
    end_recv = _rs_sibling(end_local, name="rs_sibling_end")
    end_parts, end_sends = _add_halves_all(end_local, end_recv, [BF16, F32], "end")
    end_lands = [lax.empty((3,) + s.shape[1:], s.dtype) for s in end_sends]
    ec_send, ec_recv, ec_flying, ec_token = _split_start(end_sends + end_lands, _rs_chips_copies, 3 * len(end_sends),
                                                         name="rs_chips_end_start")
    fe_done = _split_wait(fe_send, fe_recv, fe_flying, _rs_final_copies, ec_token, name="rs_final_early_wait")
    red_up, red_down = [f.reshape(-1, f.shape[-1]) for f in fe_done]

    delta, new_m, new_v, grads = {}, {}, {}, {}
    padded_rows = ('w_down',)

    def adamw_big(n, red):
        shp = w[n].shape
        w2, m2, v2 = [t.reshape(shp[-2], shp[-1]) for t in (w[n], m[n], v[n])]
        if n in padded_rows:
            g2, dl, mn, vn = _adamw(w2, red, m2, v2, emit_grad=True, name=f"adamw_{n}")
            grads[n] = g2.reshape(shp)
        else:
            grads[n] = _from_comm_layout(n, red, cfg)
            dl, mn, vn = _adamw(w2, grads[n].reshape(shp[-2], shp[-1]), m2, v2, name=f"adamw_{n}")
        delta[n], new_m[n], new_v[n] = dl.reshape(shp), mn.reshape(shp), vn.reshape(shp)

    def adamw_up(red):
        q = cfg.F // 4
        wt, mt, vt = [jnp.transpose(t[0]).reshape(2, q, d) for t in (w['w_up'], m['w_up'], v['w_up'])]
        outs = _adamw(wt, red.reshape(2, cfg.FQ, d), mt, vt, emit_grad=True, name="adamw_w_up")
        grads['w_up'], delta['w_up'], new_m['w_up'], new_v['w_up'] = [jnp.transpose(t.reshape(2 * q, d))[None] for t in outs]

    adamw_up(red_up)
    adamw_big('w_down', red_down)
    mc_done = _split_wait(mc_send, mc_recv, mc_flying, _rs_chips_copies, delta['w_down'], name="rs_chips_mid_wait")
    ec_done = _split_wait(ec_send, ec_recv, ec_flying, _rs_chips_copies, mc_done[0], name="rs_chips_end_wait")
    red = _rs_finish(mid_parts + end_parts, list(mc_done[len(mid_sends):]) + list(ec_done[len(end_sends):]), "rest")
    small_full = _allgather([_place_shard(red[5], F32, name="place_small")], name="allgather_small")[0]
    small_sum = dict(zip(SMALL, _unpack(small_full, small_shapes)))
    for n, r in zip(['w_out', 'w_glu', 'w_q_b', 'w_kv_b'], red[:4]):
        adamw_big(n, r)
    in_t = [jnp.transpose(t[0]) for t in (w['w_in'], m['w_in'], v['w_in'])]
    outs = _adamw(in_t[0], jnp.transpose(red[4][:, :cfg.DIN]), in_t[1], in_t[2], emit_grad=True, name="adamw_w_in")
    grads['w_in'], delta['w_in'], new_m['w_in'], new_v['w_in'] = [jnp.transpose(t)[None] for t in outs]

    for n in SMALL:
        g = small_sum[n]
        if n == 'meta_tokens':
            g = lax.dynamic_slice_in_dim(g, me * (d // 4), d // 4, axis=1)
        elif n == 'conv_w':
            g = lax.dynamic_slice_in_dim(g, me * (cfg.F // 4), cfg.F // 4, axis=1)[None]
        else:
            g = g.reshape(w[n].shape)
        grads[n] = g

    shapes = [w[n].shape for n in SMALL]
    packs = [_pack([src[n] for n in SMALL]) for src in (w, grads, m, v)]
    for dst, p in zip((delta, new_m, new_v), _adamw(*packs, name="adamw_small")):
        dst.update(zip(SMALL, _unpack(p, shapes)))

    return (loss, grad_x, *[grads[n] for n in WEIGHTS], *[delta[n] for n in WEIGHTS],
            *[new_m[n] for n in WEIGHTS], *[new_v[n] for n in WEIGHTS])


def kernel(x, meta_tokens, mix_norm, w_in, lam_re, lam_im, log_dt, b_re, b_im, c_re, c_im, d_skip, w_glu, b_glu, q_a_norm, w_q_b, kv_a_norm, w_kv_b, out_norm_ssm, out_norm_attn, w_out, ffn_norm, w_up, conv_w, conv_b, w_down, final_norm, loss_target, m_meta_tokens, m_mix_norm, m_w_in, m_lam_re, m_lam_im, m_log_dt, m_b_re, m_b_im, m_c_re, m_c_im, m_d_skip, m_w_glu, m_b_glu, m_q_a_norm, m_w_q_b, m_kv_a_norm, m_w_kv_b, m_out_norm_ssm, m_out_norm_attn, m_w_out, m_ffn_norm, m_w_up, m_conv_w, m_conv_b, m_w_down, m_final_norm, v_meta_tokens, v_mix_norm, v_w_in, v_lam_re, v_lam_im, v_log_dt, v_b_re, v_b_im, v_c_re, v_c_im, v_d_skip, v_w_glu, v_b_glu, v_q_a_norm, v_w_q_b, v_kv_a_norm, v_w_kv_b, v_out_norm_ssm, v_out_norm_attn, v_w_out, v_ffn_norm, v_w_up, v_conv_w, v_conv_b, v_w_down, v_final_norm):
    args = dict(locals())
    w = {n: args[n] for n in WEIGHTS}
    m = {n: args["m_" + n] for n in WEIGHTS}
    v = {n: args["v_" + n] for n in WEIGHTS}
    return _step(PROD, w, m, v, x, loss_target)
```

```python
import functools
import math
from typing import NamedTuple

import jax
import jax.numpy as jnp
from jax import lax
from jax.experimental import pallas as pl
from jax.experimental.pallas import tpu as pltpu

F32, BF16 = jnp.float32, jnp.bfloat16
MESH = pl.DeviceIdType.MESH
LANE = 128
ROW_ALIGN = 16
N_META = 16
PAD = 112
CHUNK = 64
SSM_GROUP = 16
SSM_STATE = 64
GROUPS_PER_BLOCK = 8
QK_NOPE, QK_ROPE, V_HEAD = 128, 64, 128
HEAD_SLOT = 256
ROPE_BASE = 10000.0
EPS = 1e-6
ADAM_LR, ADAM_B1, ADAM_B2, ADAM_EPS, ADAM_WD, ADAM_STEP = 0.001, 0.9, 0.999, 1e-08, 0.01, 10
DT_F32_BLOCK_BYTES = 9 << 18
ADAMW_BLOCK_BYTES = 3 << 19
PLACE_BLOCK_BYTES = 6 << 20
SKIP, FIRST = "skip", "first"


class Cfg(NamedTuple):
    D: int
    S: int
    DS: int
    H: int
    QL: int
    KVL: int
    F: int

    @property
    def LP(self):
        return PAD + N_META + self.S

    @property
    def G(self):
        return self.DS // SSM_GROUP

    @property
    def NB(self):
        return self.G // GROUPS_PER_BLOCK

    @property
    def NL(self):
        return 2 * self.G * SSM_STATE

    @property
    def DATTN(self):
        return self.H * V_HEAD

    @property
    def DMIX(self):
        return self.DS + self.DATTN

    @property
    def DIN(self):
        return self.DS + self.QL + self.KVL + QK_ROPE

    @property
    def DINP(self):
        return self.DS + self.QL + self.KVL + LANE

    @property
    def FQ(self):
        return -(-(self.F // 4) // LANE) * LANE

    @property
    def FP(self):
        return 4 * self.FQ


PROD = Cfg(D=2048, S=2048, DS=1024, H=8, QL=512, KVL=256, F=5504)

WEIGHTS = ['meta_tokens', 'mix_norm', 'w_in', 'lam_re', 'lam_im', 'log_dt', 'b_re', 'b_im', 'c_re', 'c_im', 'd_skip',
           'w_glu', 'b_glu', 'q_a_norm', 'w_q_b', 'kv_a_norm', 'w_kv_b', 'out_norm_ssm', 'out_norm_attn', 'w_out',
           'ffn_norm', 'w_up', 'conv_w', 'conv_b', 'w_down', 'final_norm']
BIG = ['w_in', 'w_glu', 'w_q_b', 'w_kv_b', 'w_out', 'w_up', 'w_down']
SMALL = [n for n in WEIGHTS if n not in BIG]


def _pc(body, **kw):
    return pl.pallas_call(body, **kw)


def _tile(n, target, align=LANE):
    best = None
    d = align
    while d <= min(n, target):
        if n % d == 0:
            best = d
        d += align
    return best if best is not None else n


def _row_tile(rows, cols):
    return _tile(rows, max(ROW_ALIGN, DT_F32_BLOCK_BYTES // (4 * cols)), ROW_ALIGN)


def _mm(a, b, *, name, ta=False, tb=False, tm=None, tn=512, tk=None, out_dtype=F32, res=None,
        a_idx=None, b_idx=None, dims=None, a_lead=False):
    if dims is None:
        m, k = (a.shape[1], a.shape[0]) if ta else a.shape
        n = b.shape[0] if tb else b.shape[1]
    else:
        m, n, k = dims
    tm = _tile(m, tm or m, LANE if ta else ROW_ALIGN)
    tn = _tile(n, tn)
    tk = _tile(k, tk or k, ROW_ALIGN if (ta and not tb) else LANE)
    nm, nn, nk = m // tm, n // tn, k // tk
    a_idx = a_idx or ((lambda i, j, kk: (kk, i)) if ta else (lambda i, j, kk: (i, kk)))
    b_idx = b_idx or ((lambda i, j, kk: (j, kk)) if tb else (lambda i, j, kk: (kk, j)))
    dn = (((0 if ta else 1,), (1 if tb else 0,)), ((), ()))

    def body(*refs):
        a_ref, b_ref = refs[0], refs[1]
        r_ref = refs[2] if res is not None else None
        o_ref = refs[3] if res is not None else refs[2]
        d = lax.dot_general(a_ref[...].astype(BF16), b_ref[...].astype(BF16), dn, preferred_element_type=F32)

        def finish(r):
            if r_ref is not None:
                r = r + r_ref[...].astype(F32)
            o_ref[...] = r.astype(out_dtype)

        if nk == 1:
            finish(d)
        else:
            acc = refs[-1]
            kk = pl.program_id(2)

            @pl.when(kk == 0)
            def _():
                acc[...] = d

            @pl.when(kk > 0)
            def _():
                acc[...] += d

            @pl.when(kk == nk - 1)
            def _():
                finish(acc[...])

    a_blk = ((None,) if a_lead else ()) + ((tk, tm) if ta else (tm, tk))
    in_specs = [pl.BlockSpec(a_blk, a_idx), pl.BlockSpec((tn, tk) if tb else (tk, tn), b_idx)]
    args = [a, b]
    if res is not None:
        in_specs.append(pl.BlockSpec((tm, tn), lambda i, j, kk: (i, j)))
        args.append(res)
    return _pc(body, name=name, grid=(nm, nn, nk), in_specs=in_specs,
               out_specs=pl.BlockSpec((tm, tn), lambda i, j, kk: (i, j)),
               out_shape=jax.ShapeDtypeStruct((m, n), out_dtype),
               scratch_shapes=[pltpu.VMEM((tm, tn), F32)] if nk > 1 else [],
               compiler_params=pltpu.CompilerParams(dimension_semantics=("parallel", "parallel", "arbitrary")))(*args)


def _ew(fn, ins, vecs, outs, sums=(), *, name, tm=None):
    ins = [x if isinstance(x, tuple) else (x, x.shape[1], 0) for x in ins]
    ins = [x if len(x) == 4 else x + (None,) for x in ins]
    outs = [o if len(o) == 3 else o + (None,) for o in outs]
    rows = ins[0][0].shape[0]
    cmax = max([c for _, c, _, _ in ins] + [c for c, _, _ in outs])
    tm = tm or _row_tile(rows, cmax)
    n_in, n_vec, n_out, n_sum = len(ins), len(vecs), len(outs), len(sums)

    def body(*refs):
        i = pl.program_id(0)
        rid = i * tm + lax.broadcasted_iota(jnp.int32, (tm, 1), 0)
        vals = [r[...] for r in refs[:n_in + n_vec]]
        res = fn(rid, *vals)
        res = res if isinstance(res, (tuple, list)) else (res,)
        o_refs = refs[n_in + n_vec:]
        for o_ref, r, (_, _, mode) in zip(o_refs[:n_out], res[:n_out], outs):
            if mode == FIRST:
                @pl.when(i == 0)
                def _():
                    o_ref[...] = r.astype(o_ref.dtype)
            else:
                o_ref[...] = r.astype(o_ref.dtype)
        for o_ref, r in zip(o_refs[n_out:], res[n_out:]):
            part = jnp.sum(r.astype(F32), axis=0, keepdims=True)

            @pl.when(i == 0)
            def _():
                o_ref[...] = part

            @pl.when(i > 0)
            def _():
                o_ref[...] += part

    def row_idx(mode):
        if mode == SKIP:
            return lambda i, cb=0: (jnp.maximum(i - 1, 0), cb)
        if mode == FIRST:
            return lambda i, cb=0: (0, cb)
        return lambda i, cb=0: (i, cb)

    in_specs = [pl.BlockSpec((tm, c), functools.partial(row_idx(mode), cb=cb)) for _, c, cb, mode in ins]
    in_specs += [pl.BlockSpec(v.shape, functools.partial(lambda i, nd: (0,) * nd, nd=v.ndim)) for v in vecs]
    out_specs = [pl.BlockSpec((tm, c), row_idx(mode)) for c, _, mode in outs]
    out_specs += [pl.BlockSpec((1, c), lambda i: (0, 0)) for c in sums]
    out_rows = {None: rows, SKIP: rows - tm, FIRST: tm}
    out_shape = [jax.ShapeDtypeStruct((out_rows[mode], c), dt) for c, dt, mode in outs]
    out_shape += [jax.ShapeDtypeStruct((1, c), F32) for c in sums]
    return _pc(body, name=name, grid=(rows // tm,), in_specs=in_specs, out_specs=out_specs, out_shape=out_shape,
               compiler_params=pltpu.CompilerParams(dimension_semantics=("arbitrary",)))(*[x[0] for x in ins], *vecs)


def _rms_parts(x, g):
    r = lax.rsqrt(jnp.mean(x * x, axis=-1, keepdims=True) + EPS)
    return x * r, r


def _rms_bwd_block(x, g, dy):
    xhat, r = _rms_parts(x, g)
    dxhat = dy * g
    dx = r * (dxhat - xhat * jnp.mean(dxhat * xhat, axis=-1, keepdims=True))
    return dx, dy * xhat


def _rms_fwd(x, g, *, name):
    c = x[1] if isinstance(x, tuple) else x.shape[1]
    return _ew(lambda rid, xv, gv: _rms_parts(xv.astype(F32), gv)[0] * gv, [x], [g], [(c, BF16)], name=name)[0]


def _rms_bwd(x, g, dy, *, name, res=None, mask=False, with_bf16=False):
    c = x[1] if isinstance(x, tuple) else x.shape[1]

    def fn(rid, xv, dyv, *rest):
        gv = rest[-1]
        dx, dg = _rms_bwd_block(xv.astype(F32), gv, dyv.astype(F32))
        if res is not None:
            dx = dx + rest[0]
        if mask:
            dx = jnp.where(rid >= PAD, dx, 0.0)
        return (dx, dx, dg) if with_bf16 else (dx, dg)

    ins = [x, dy] + ([res] if res is not None else [])
    outs = [(c, F32)] + ([(c, BF16)] if with_bf16 else [])
    return _ew(fn, ins, [g], outs, [c], name=name)


S5_W = GROUPS_PER_BLOCK * SSM_STATE
S5_GW = GROUPS_PER_BLOCK * SSM_GROUP
S5_DA_ROWS = 272


def _s5_scan_in_place(ref, pw_ref, *, reverse):
    lp = ref.shape[0]
    tile_rows = 8
    chunk = _tile(lp, S5_DA_ROWS, tile_rows)
    tiles = chunk // tile_rows

    def chunk_body(c, carry):
        rows = pl.ds(pl.multiple_of(c * chunk, tile_rows), chunk)
        xr, xi = ref[rows, :S5_W], ref[rows, S5_W:]
        for level, k in enumerate((1, 2, 4)):
            base = tile_rows * (1 + level)
            mr, mi = pw_ref[base:base + tile_rows, :S5_W][None], pw_ref[base:base + tile_rows, S5_W:][None]
            shift = chunk - k if reverse else k
            sr = pltpu.roll(xr, shift, 0).reshape(tiles, tile_rows, S5_W)
            si = pltpu.roll(xi, shift, 0).reshape(tiles, tile_rows, S5_W)
            xr = xr + (mr * sr - mi * si).reshape(chunk, S5_W)
            xi = xi + (mr * si + mi * sr).reshape(chunk, S5_W)
        ref[rows, :S5_W] = xr
        ref[rows, S5_W:] = xi
        return carry

    lax.fori_loop(0, lp // chunk, chunk_body, 0)

    pr, pi = pw_ref[0:tile_rows, :S5_W], pw_ref[0:tile_rows, S5_W:]
    ntile = lp // tile_rows
    unroll = 4

    def step(n, carry):
        cr, ci = carry
        for q in range(unroll):
            j = n * unroll + q
            j = ntile - 1 - j if reverse else j
            rows = pl.ds(pl.multiple_of(j * tile_rows, tile_rows), tile_rows)
            nr = ref[rows, :S5_W] + (pr * cr - pi * ci)
            ni = ref[rows, S5_W:] + (pr * ci + pi * cr)
            ref[rows, :S5_W] = nr
            ref[rows, S5_W:] = ni
            cr, ci = (nr[0:1], ni[0:1]) if reverse else (nr[tile_rows - 1:], ni[tile_rows - 1:])
        return cr, ci

    z = jnp.zeros((1, S5_W), F32)
    lax.fori_loop(0, ntile // unroll, step, (z, z))


def _s5_fwd(z, bb_band, cc_band, pw, cfg, *, name):
    lp, ds, nl = cfg.LP, cfg.DS, cfg.NL

    def body(u_ref, bb_ref, cc_ref, a_ref, hs_ref, y_ref):
        hs_ref[...] = jnp.dot(u_ref[...].astype(BF16), bb_ref[...], preferred_element_type=F32)
        _s5_scan_in_place(hs_ref, a_ref, reverse=False)
        y_ref[...] = jnp.dot(hs_ref[...].astype(BF16), cc_ref[...], preferred_element_type=F32)

    return _pc(body, name=name, grid=(cfg.NB,),
               in_specs=[pl.BlockSpec((lp, S5_GW), lambda j: (0, j)), pl.BlockSpec((S5_GW, 2 * S5_W), lambda j: (j, 0)),
                         pl.BlockSpec((2 * S5_W, S5_GW), lambda j: (j, 0)), pl.BlockSpec((32, 2 * S5_W), lambda j: (0, j))],
               out_specs=[pl.BlockSpec((lp, 2 * S5_W), lambda j: (0, j)), pl.BlockSpec((lp, S5_GW), lambda j: (0, j))],
               out_shape=[jax.ShapeDtypeStruct((lp, nl), F32), jax.ShapeDtypeStruct((lp, ds), F32)],
               compiler_params=pltpu.CompilerParams(dimension_semantics=("parallel",)))(z, bb_band, cc_band, pw)


def _s5_bwd(dy, hs, z, bb_band, cc_band, pw, du_skip, cfg, *, name):
    lp, ds, nl = cfg.LP, cfg.DS, cfg.NL
    nt = (((1,), (1,)), ((), ()))
    tn = (((0,), (0,)), ((), ()))

    def body(dy_ref, hs_ref, u_ref, bb_ref, cc_ref, a_ref, sk_ref, du_ref, dbb_ref, dcc_ref, da_ref, g_ref):
        dyv = dy_ref[...]
        g_ref[...] = lax.dot_general(dyv, cc_ref[...], nt, preferred_element_type=F32)
        _s5_scan_in_place(g_ref, a_ref, reverse=True)
        dcc_ref[...] = lax.dot_general(hs_ref[...].astype(BF16), dyv, tn, preferred_element_type=F32)
        gb = g_ref[...].astype(BF16)
        dbb_ref[...] = lax.dot_general(u_ref[...].astype(BF16), gb, tn, preferred_element_type=F32)
        du_ref[...] = lax.dot_general(gb, bb_ref[...], nt, preferred_element_type=F32) + sk_ref[...]
        dre = jnp.zeros((1, S5_W), F32)
        dim = jnp.zeros((1, S5_W), F32)
        for r0 in range(0, lp, S5_DA_ROWS):
            rows = min(S5_DA_ROWS, lp - r0)
            first = lax.broadcasted_iota(jnp.int32, (rows, 1), 0) == 0
            prev = hs_ref[r0 - 1:r0, :] if r0 else jnp.zeros((1, 2 * S5_W), F32)
            hr = jnp.where(first, prev[:, :S5_W], pltpu.roll(hs_ref[r0:r0 + rows, :S5_W], 1, 0))
            hi = jnp.where(first, prev[:, S5_W:], pltpu.roll(hs_ref[r0:r0 + rows, S5_W:], 1, 0))
            gr, gi = g_ref[r0:r0 + rows, :S5_W], g_ref[r0:r0 + rows, S5_W:]
            dre = dre + jnp.sum(gr * hr + gi * hi, axis=0, keepdims=True)
            dim = dim + jnp.sum(gi * hr - gr * hi, axis=0, keepdims=True)
        da_ref[:, :S5_W] = dre
        da_ref[:, S5_W:] = dim

    col_blk = pl.BlockSpec((lp, S5_GW), lambda j: (0, j))
    lane_blk = pl.BlockSpec((lp, 2 * S5_W), lambda j: (0, j))
    bb_blk = pl.BlockSpec((S5_GW, 2 * S5_W), lambda j: (j, 0))
    cc_blk = pl.BlockSpec((2 * S5_W, S5_GW), lambda j: (j, 0))
    a_blk = pl.BlockSpec((1, 2 * S5_W), lambda j: (0, j))
    pw_blk = pl.BlockSpec((32, 2 * S5_W), lambda j: (0, j))
    return _pc(body, name=name, grid=(cfg.NB,),
               in_specs=[col_blk, lane_blk, col_blk, bb_blk, cc_blk, pw_blk, col_blk],
               out_specs=[col_blk, bb_blk, cc_blk, a_blk],
               out_shape=[jax.ShapeDtypeStruct((lp, ds), F32), jax.ShapeDtypeStruct((ds, 2 * S5_W), F32),
                          jax.ShapeDtypeStruct((nl, S5_GW), F32), jax.ShapeDtypeStruct((1, nl), F32)],
               scratch_shapes=[pltpu.VMEM((lp, 2 * S5_W), F32)],
               compiler_params=pltpu.CompilerParams(dimension_semantics=("parallel",)))(dy, hs, z, bb_band, cc_band, pw, du_skip)


def _conv_gate(pre, cw, cb):
    return cw[0:1] * pltpu.roll(pre, 2, 0) + cw[1:2] * pltpu.roll(pre, 1, 0) + cw[2:3] * pre + cb


def _ffn_up(xn2, w_upt, cw, cb, *, name):
    lp, d = xn2.shape
    fp = w_upt.shape[0] // 2
    tc = _tile(fp, 256)
    nb = fp // tc

    def body(x_ref, wg_ref, wv_ref, cw_ref, cb_ref, up_ref, act_ref):
        wcat = jnp.concatenate([wg_ref[...], wv_ref[...]], axis=0)
        r = lax.dot_general(x_ref[...], wcat, (((1,), (1,)), ((), ())), preferred_element_type=F32)
        pre, val = r[:, :tc].astype(BF16), r[:, tc:].astype(BF16)
        up_ref[0] = pre
        up_ref[1] = val
        gate = _conv_gate(pre.astype(F32), cw_ref[...], cb_ref[...])
        act_ref[...] = (jax.nn.silu(gate) * val.astype(F32)).astype(BF16)

    return _pc(body, name=name, grid=(nb,),
               in_specs=[pl.BlockSpec((lp, d), lambda j: (0, 0)), pl.BlockSpec((tc, d), lambda j: (j, 0)),
                         pl.BlockSpec((tc, d), lambda j: (nb + j, 0)),
                         pl.BlockSpec((3, tc), lambda j: (0, j)), pl.BlockSpec((1, tc), lambda j: (0, j))],
               out_specs=[pl.BlockSpec((2, lp, tc), lambda j: (0, 0, j)), pl.BlockSpec((lp, tc), lambda j: (0, j))],
               out_shape=[jax.ShapeDtypeStruct((2, lp, fp), BF16), jax.ShapeDtypeStruct((lp, fp), BF16)],
               compiler_params=pltpu.CompilerParams(dimension_semantics=("parallel",)))(xn2, w_upt, w_upt, cw, cb)


def _ffn_dact(dh2, w_down, up, cw, cb, *, name):
    lp, d = dh2.shape
    fp = w_down.shape[0]
    tc = _tile(fp, 256)
    nb = fp // tc

    def body(dh_ref, wd_ref, up_ref, cw_ref, cb_ref, dup_ref, dcw_ref, dcb_ref):
        da = lax.dot_general(dh_ref[...], wd_ref[...], (((1,), (1,)), ((), ())), preferred_element_type=F32)
        pre, val, cwv = up_ref[0].astype(F32), up_ref[1].astype(F32), cw_ref[...]
        gate = _conv_gate(pre, cwv, cb_ref[...])
        sg = jax.nn.sigmoid(gate)
        dup_ref[1] = (da * (gate * sg)).astype(BF16)
        dgate = da * val * (sg * (1.0 + gate * (1.0 - sg)))
        dpre = cwv[2:3] * dgate + cwv[1:2] * pltpu.roll(dgate, lp - 1, 0) + cwv[0:1] * pltpu.roll(dgate, lp - 2, 0)
        dup_ref[0] = dpre.astype(BF16)
        dcb_ref[...] = jnp.sum(dgate, axis=0, keepdims=True)
        dcw_ref[0:1, :] = jnp.sum(dgate * pltpu.roll(pre, 2, 0), axis=0, keepdims=True)
        dcw_ref[1:2, :] = jnp.sum(dgate * pltpu.roll(pre, 1, 0), axis=0, keepdims=True)
        dcw_ref[2:3, :] = jnp.sum(dgate * pre, axis=0, keepdims=True)

    return _pc(body, name=name, grid=(nb,),
               in_specs=[pl.BlockSpec((lp, d), lambda j: (0, 0)), pl.BlockSpec((tc, d), lambda j: (j, 0)),
                         pl.BlockSpec((2, lp, tc), lambda j: (0, 0, j)),
                         pl.BlockSpec((3, tc), lambda j: (0, j)), pl.BlockSpec((1, tc), lambda j: (0, j))],
               out_specs=[pl.BlockSpec((2, lp, tc), lambda j: (0, 0, j)),
                          pl.BlockSpec((3, tc), lambda j: (0, j)), pl.BlockSpec((1, tc), lambda j: (0, j))],
               out_shape=[jax.ShapeDtypeStruct((2, lp, fp), BF16), jax.ShapeDtypeStruct((3, fp), F32),
                          jax.ShapeDtypeStruct((1, fp), F32)],
               compiler_params=pltpu.CompilerParams(dimension_semantics=("parallel",)))(dh2, w_down, up, cw, cb)


ATTN_Q_ROWS = 544


def _key_limit(i, tq, lp):
    return min(lp, -(-((i + 1) * tq) // LANE) * LANE)


def _attn_mask(i, tq, nk):
    qrow = i * tq + lax.broadcasted_iota(jnp.int32, (tq, 1), 0)
    krow = lax.broadcasted_iota(jnp.int32, (1, nk), 1)
    return (krow >= PAD) & ((krow // CHUNK) <= (qrow // CHUNK)), qrow >= PAD


def _attn_scores(q, kn, kr, i, tq, scale):
    nt = (((1,), (1,)), ((), ()))
    s = lax.dot_general(q[:, :QK_NOPE], kn, nt, preferred_element_type=F32)
    s = s + lax.dot_general(q[:, QK_NOPE:], kr, nt, preferred_element_type=F32)
    mask, qvalid = _attn_mask(i, tq, kn.shape[0])
    return jnp.where(mask, s * scale, jnp.finfo(F32).min), qvalid


def _per_q_block(nq, fn):
    i = pl.program_id(1)
    for blk in range(nq):
        pl.when(i == blk)(functools.partial(fn, blk))


def _attn_fwd(qx, kv, kr, cfg, *, name):
    lp, h = cfg.LP, cfg.H
    tq = _tile(lp, ATTN_Q_ROWS, ROW_ALIGN)
    nq = lp // tq
    scale = 1.0 / math.sqrt(QK_NOPE + QK_ROPE)

    def body(q_ref, kn_ref, v_ref, kr_ref, o_ref, lse_ref):
        def block(blk):
            nk = _key_limit(blk, tq, lp)
            s, qvalid = _attn_scores(q_ref[...], kn_ref[:nk], kr_ref[:nk], blk, tq, scale)
            m = jnp.max(s, axis=-1, keepdims=True)
            p = jnp.exp(s - m)
            l = jnp.sum(p, axis=-1, keepdims=True)
            o = jnp.dot(p.astype(BF16), v_ref[:nk], preferred_element_type=F32) / l
            o_ref[...] = jnp.where(qvalid, o, 0.0)
            lse_ref[...] = m + jnp.log(l)

        _per_q_block(nq, block)

    return _pc(body, name=name, grid=(h, nq),
               in_specs=[pl.BlockSpec((tq, HEAD_SLOT), lambda hh, i: (i, hh)),
                         pl.BlockSpec((lp, QK_NOPE), lambda hh, i: (0, 2 * hh)),
                         pl.BlockSpec((lp, V_HEAD), lambda hh, i: (0, 2 * hh + 1)),
                         pl.BlockSpec((lp, LANE), lambda hh, i: (0, 0))],
               out_specs=[pl.BlockSpec((tq, V_HEAD), lambda hh, i: (i, hh)),
                          pl.BlockSpec((None, tq, 1), lambda hh, i: (hh, i, 0))],
               out_shape=[jax.ShapeDtypeStruct((lp, h * V_HEAD), F32), jax.ShapeDtypeStruct((h, lp, 1), F32)],
               compiler_params=pltpu.CompilerParams(dimension_semantics=("parallel", "parallel")))(qx, kv, kv, kr)


def _attn_bwd(qx, kv, kr, o, lse, do, cfg, *, name):
    lp, h = cfg.LP, cfg.H
    tq = _tile(lp, ATTN_Q_ROWS, ROW_ALIGN)
    nq = lp // tq
    scale = 1.0 / math.sqrt(QK_NOPE + QK_ROPE)
    tn_dims = (((0,), (0,)), ((), ()))

    def body(q_ref, kn_ref, v_ref, kr_ref, o_ref, lse_ref, do_ref, dq_ref, dkv_ref, dkr_ref, dkv_acc):
        hh, i = pl.program_id(0), pl.program_id(1)

        @pl.when(i == 0)
        def _():
            dkv_acc[...] = jnp.zeros_like(dkv_acc)

        @pl.when((i == 0) & (hh == 0))
        def _():
            dkr_ref[...] = jnp.zeros_like(dkr_ref)

        def block(blk):
            nk = _key_limit(blk, tq, lp)
            q, kn, v, krv = q_ref[...], kn_ref[:nk], v_ref[:nk], kr_ref[:nk]
            s, qvalid = _attn_scores(q, kn, krv, blk, tq, scale)
            dov = jnp.where(qvalid, do_ref[...], 0.0)
            p = jnp.exp(s - lse_ref[...])
            delta = jnp.sum(dov * o_ref[...], axis=-1, keepdims=True)
            dob = dov.astype(BF16)
            dp = lax.dot_general(dob, v, (((1,), (1,)), ((), ())), preferred_element_type=F32)
            ds = (p * (dp - delta) * scale).astype(BF16)
            dq_ref[:, :QK_NOPE] = jnp.dot(ds, kn, preferred_element_type=F32)
            dq_ref[:, QK_NOPE:] = jnp.dot(ds, krv, preferred_element_type=F32)
            dkv_acc[:nk, :QK_NOPE] += lax.dot_general(ds, q[:, :QK_NOPE], tn_dims, preferred_element_type=F32)
            dkv_acc[:nk, QK_NOPE:] += lax.dot_general(p.astype(BF16), dob, tn_dims, preferred_element_type=F32)
            dkr_ref[:nk, :] += lax.dot_general(ds, q[:, QK_NOPE:], tn_dims, preferred_element_type=F32)

        _per_q_block(nq, block)

        @pl.when(i == nq - 1)
        def _():
            dkv_ref[...] = dkv_acc[...].astype(BF16)

    return _pc(body, name=name, grid=(h, nq),
               in_specs=[pl.BlockSpec((tq, HEAD_SLOT), lambda hh, i: (i, hh)),
                         pl.BlockSpec((lp, QK_NOPE), lambda hh, i: (0, 2 * hh)),
                         pl.BlockSpec((lp, V_HEAD), lambda hh, i: (0, 2 * hh + 1)),
                         pl.BlockSpec((lp, LANE), lambda hh, i: (0, 0)),
                         pl.BlockSpec((tq, V_HEAD), lambda hh, i: (i, hh)),
                         pl.BlockSpec((None, tq, 1), lambda hh, i: (hh, i, 0)),
                         pl.BlockSpec((tq, V_HEAD), lambda hh, i: (i, hh))],
               out_specs=[pl.BlockSpec((tq, HEAD_SLOT), lambda hh, i: (i, hh)),
                          pl.BlockSpec((lp, QK_NOPE + V_HEAD), lambda hh, i: (0, hh)),
                          pl.BlockSpec((lp, LANE), lambda hh, i: (0, 0))],
               out_shape=[jax.ShapeDtypeStruct((lp, h * HEAD_SLOT), F32),
                          jax.ShapeDtypeStruct((lp, h * (QK_NOPE + V_HEAD)), BF16),
                          jax.ShapeDtypeStruct((lp, LANE), F32)],
               scratch_shapes=[pltpu.VMEM((lp, QK_NOPE + V_HEAD), F32)],
               compiler_params=pltpu.CompilerParams(dimension_semantics=("arbitrary", "arbitrary")))(qx, kv, kv, kr, o, lse, do)


def _rot_half(x):
    lane = lax.broadcasted_iota(jnp.int32, x.shape, 1)
    half = QK_ROPE // 2
    return jnp.where(lane < half, -pltpu.roll(x, LANE - half, 1), pltpu.roll(x, half, 1))


def _rope(x, cos, sin):
    return x * cos + _rot_half(x) * sin


def _unrope(dy, cos, sin):
    return dy * cos - _rot_half(dy * sin)


def _rope_heads(fn, h):
    def apply(rid, q, cos, sin):
        parts = []
        for hh in range(h):
            parts.append(q[:, hh * HEAD_SLOT: hh * HEAD_SLOT + QK_NOPE])
            parts.append(fn(q[:, hh * HEAD_SLOT + QK_NOPE: (hh + 1) * HEAD_SLOT], cos, sin))
        return jnp.concatenate(parts, axis=1)
    return apply


ANY = pl.BlockSpec(memory_space=pl.ANY)


def _place():
    x, y, c = lax.axis_index("x"), lax.axis_index("y"), lax.axis_index("c")
    chips = [(1 - x, y), (x, 1 - y), (1 - x, 1 - y)]
    return x, y, c, chips


def _rcopy(src, dst, send_sem, recv_sem, dev):
    return pltpu.make_async_remote_copy(src_ref=src, dst_ref=dst, send_sem=send_sem, recv_sem=recv_sem,
                                        device_id=dev, device_id_type=MESH)


def _place_shard(shard, dtype, *, name, order=None, rows_to=None):
    shard = shard if shard.ndim == 3 else shard[None]
    n, r, cols = shard.shape
    rp = rows_to or r
    tm = _tile(r, max(ROW_ALIGN, PLACE_BLOCK_BYTES // (4 * cols)), ROW_ALIGN)
    me = (2 * lax.axis_index("x") + lax.axis_index("y")).astype(jnp.int32).reshape(1)
    extra = [] if order is None else [order]

    def body(me_ref, s_ref, *rest):
        rest[-1][...] = s_ref[...].astype(dtype)

    full = _pc(body, name=name,
               grid_spec=pltpu.PrefetchScalarGridSpec(
                   num_scalar_prefetch=1, grid=(n, r // tm),
                   in_specs=[pl.BlockSpec((None, tm, cols), lambda q, i, mr: (q, i, 0))] + [ANY] * len(extra),
                   out_specs=pl.BlockSpec((None, tm, cols), lambda q, i, mr: (mr[0] * n + q, i, 0))),
               out_shape=jax.ShapeDtypeStruct((4 * n, rp, cols), dtype),
               compiler_params=pltpu.CompilerParams(dimension_semantics=("arbitrary", "arbitrary")))(me, shard, *extra)
    if rp > r:
        pad = rp - r
        assert r % pad == 0

        def zero(me_ref, f_ref, o_ref):
            o_ref[...] = jnp.zeros_like(o_ref)

        full = _pc(zero, name=name + "_pad",
                   grid_spec=pltpu.PrefetchScalarGridSpec(
                       num_scalar_prefetch=1, grid=(n,), in_specs=[ANY],
                       out_specs=pl.BlockSpec((None, pad, cols), lambda q, mr: (mr[0] * n + q, r // pad, 0))),
                   out_shape=jax.ShapeDtypeStruct(full.shape, dtype), input_output_aliases={1: 0},
                   compiler_params=pltpu.CompilerParams(dimension_semantics=("arbitrary",)))(me, full)
    return full.reshape(4 * n * rp, cols)


def _allgather(fulls, *, name):
    n = len(fulls)

    def body(*refs):
        outs = refs[n:2 * n]
        send_sems, recv_sems = refs[2 * n:]
        x, y, c, chips = _place()
        sib = (x, y, 1 - c)
        me = 2 * x + y

        def rows(t, s, half):
            hrows = outs[t].shape[0] // 8
            return outs[t].at[pl.ds((2 * s + half) * hrows, hrows)]

        sent = []
        for t in range(n):
            for j, (cx, cy) in enumerate(chips):
                cp = _rcopy(rows(t, me, c), rows(t, me, c), send_sems.at[6 * t + j], recv_sems.at[6 * t + j], (cx, cy, c))
                cp.start()
                sent.append(cp)
        for t in range(n):
            for j, (cx, cy) in enumerate(chips):
                landed = rows(t, 2 * cx + cy, c)
                _rcopy(landed, landed, send_sems.at[6 * t + j], recv_sems.at[6 * t + j], (cx, cy, c)).wait_recv()
                cp = _rcopy(landed, landed, send_sems.at[6 * t + 3 + j], recv_sems.at[6 * t + 3 + j], sib)
                cp.start()
                sent.append(cp)
        for t in range(n):
            for j, (cx, cy) in enumerate(chips):
                other = rows(t, 2 * cx + cy, 1 - c)
                _rcopy(other, other, send_sems.at[6 * t + 3 + j], recv_sems.at[6 * t + 3 + j], sib).wait_recv()
        for cp in sent:
            cp.wait_send()

    return _pc(body, name=name, in_specs=[ANY] * n, out_specs=[ANY] * n,
               out_shape=[jax.ShapeDtypeStruct(f.shape, f.dtype) for f in fulls],
               input_output_aliases={t: t for t in range(n)},
               scratch_shapes=[pltpu.SemaphoreType.DMA((6 * n,)), pltpu.SemaphoreType.DMA((6 * n,))])(*fulls)


HBM = pl.BlockSpec(memory_space=pltpu.HBM)
SEM = pl.BlockSpec(memory_space=pltpu.SEMAPHORE)
EFFECT = pltpu.SideEffectType.DATAFLOW_SIDE_EFFECTING
TOKEN = jax.ShapeDtypeStruct((8, LANE), F32)


def _in_hbm(a):
    return pltpu.with_memory_space_constraint(a, pltpu.HBM)


def _half_rows(ref, s, half):
    hrows = ref.shape[0] // 8
    return ref.at[pl.ds((2 * s + half) * hrows, hrows)]


def _split_start(bufs, copies, n_copies, *, name, before=None):
    n = len(bufs)
    extra = [] if before is None else [before]

    def body(*refs):
        send_sems, recv_sems, token = refs[n + len(extra)], refs[n + len(extra) + 1], refs[-1]
        for k, (src, dst, dev) in enumerate(copies(refs[:n])):
            _rcopy(src, dst, send_sems.at[k], recv_sems.at[k], dev).start()
        token[...] = jnp.zeros_like(token)

    res = _pc(body, name=name, in_specs=[HBM] * n + [ANY] * len(extra),
              out_specs=[SEM, SEM] + [HBM] * n + [pl.BlockSpec(memory_space=pltpu.VMEM)],
              out_shape=[pltpu.SemaphoreType.DMA((n_copies,)), pltpu.SemaphoreType.DMA((n_copies,))]
              + [pltpu.HBM(b.shape, b.dtype) for b in bufs] + [TOKEN],
              input_output_aliases={t: 2 + t for t in range(n)},
              compiler_params=pltpu.CompilerParams(has_side_effects=EFFECT))(*[_in_hbm(b) for b in bufs], *extra)
    return res[0], res[1], list(res[2:2 + n]), res[-1]


def _split_wait(send_sems, recv_sems, bufs, copies, after, *, name):
    n = len(bufs)
    after = list(after) if isinstance(after, (list, tuple)) else [after]

    def body(*refs):
        send_ref, recv_ref = refs[n], refs[n + 1]
        for k, (src, dst, dev) in enumerate(copies(refs[:n])):
            cp = _rcopy(src, dst, send_ref.at[k], recv_ref.at[k], dev)
            cp.wait_send()
            cp.wait_recv()

    return _pc(body, name=name, in_specs=[HBM] * n + [SEM, SEM] + [ANY] * len(after), out_specs=[HBM] * n,
               out_shape=[pltpu.HBM(b.shape, b.dtype) for b in bufs],
               input_output_aliases={t: t for t in range(n)},
               compiler_params=pltpu.CompilerParams(has_side_effects=EFFECT))(*bufs, send_sems, recv_sems, *after)


def _allgather_ici_copies(refs):
    x, y, c, chips = _place()
    return [(_half_rows(r, 2 * x + y, c), _half_rows(r, 2 * x + y, c), (cx, cy, c)) for r in refs for cx, cy in chips]


def _rs_chips_copies(refs):
    x, y, c, chips = _place()
    n = len(refs) // 2
    return [(refs[t].at[2 * cx + cy], refs[n + t].at[j], (cx, cy, c)) for t in range(n) for j, (cx, cy) in enumerate(chips)]


def _allgather_forward_copies(refs):
    x, y, c, chips = _place()
    return [(_half_rows(r, 2 * cx + cy, c), _half_rows(r, 2 * cx + cy, c), (x, y, 1 - c)) for r in refs for cx, cy in chips]


def _rs_final_copies(refs):
    x, y, c, _ = _place()
    return [(r.at[c], r.at[c], (x, y, 1 - c)) for r in refs]


def _rs_sibling_copies(refs):
    x, y, c, _ = _place()
    n = len(refs) // 2
    out = []
    for t in range(n):
        h = refs[t].shape[0] // 8
        out += [(refs[t].at[pl.ds((2 * s + 1 - c) * h, h)], refs[n + t].at[s], (x, y, 1 - c)) for s in range(4)]
    return out


def _allgather_forward(fulls, *, name):
    n = len(fulls)

    def body(*refs):
        outs = refs[n:2 * n]
        send_sems, recv_sems = refs[2 * n:]
        x, y, c, chips = _place()
        sent = []
        for t in range(n):
            for j, (cx, cy) in enumerate(chips):
                landed = _half_rows(outs[t], 2 * cx + cy, c)
                cp = _rcopy(landed, landed, send_sems.at[3 * t + j], recv_sems.at[3 * t + j], (x, y, 1 - c))
                cp.start()
                sent.append(cp)
        for t in range(n):
            for j, (cx, cy) in enumerate(chips):
                other = _half_rows(outs[t], 2 * cx + cy, 1 - c)
                _rcopy(other, other, send_sems.at[3 * t + j], recv_sems.at[3 * t + j], (x, y, 1 - c)).wait_recv()
        for cp in sent:
            cp.wait_send()

    return _pc(body, name=name, in_specs=[ANY] * n, out_specs=[ANY] * n,
               out_shape=[jax.ShapeDtypeStruct(f.shape, f.dtype) for f in fulls],
               input_output_aliases={t: t for t in range(n)},
               scratch_shapes=[pltpu.SemaphoreType.DMA((3 * n,)), pltpu.SemaphoreType.DMA((3 * n,))])(*fulls)


def _rs_sibling(grads, *, name):
    n = len(grads)

    def body(*refs):
        ins, outs = refs[:n], refs[n:2 * n]
        send_sems, recv_sems = refs[2 * n:]
        x, y, c, _ = _place()
        cps = []
        for t in range(n):
            h = ins[t].shape[0] // 8
            for s in range(4):
                cp = _rcopy(ins[t].at[pl.ds((2 * s + 1 - c) * h, h)], outs[t].at[s], send_sems.at[4 * t + s],
                            recv_sems.at[4 * t + s], (x, y, 1 - c))
                cp.start()
                cps.append(cp)
        for cp in cps:
            cp.wait()

    return _pc(body, name=name, in_specs=[ANY] * n, out_specs=[ANY] * n,
               out_shape=[jax.ShapeDtypeStruct((4, g.shape[0] // 8, g.shape[1]), g.dtype) for g in grads],
               scratch_shapes=[pltpu.SemaphoreType.DMA((4 * n,)), pltpu.SemaphoreType.DMA((4 * n,))])(*grads)


def _rs_final(fulls, *, name):
    n = len(fulls)

    def body(*refs):
        outs = refs[n:2 * n]
        send_sems, recv_sems = refs[2 * n:]
        x, y, c, _ = _place()
        cps = []
        for t in range(n):
            cp = _rcopy(outs[t].at[c], outs[t].at[c], send_sems.at[t], recv_sems.at[t], (x, y, 1 - c))
            cp.start()
            cps.append(cp)
        for cp in cps:
            cp.wait()

    return _pc(body, name=name, in_specs=[ANY] * n, out_specs=[ANY] * n,
               out_shape=[jax.ShapeDtypeStruct(f.shape, f.dtype) for f in fulls],
               input_output_aliases={t: t for t in range(n)},
               scratch_shapes=[pltpu.SemaphoreType.DMA((n,)), pltpu.SemaphoreType.DMA((n,))])(*fulls)


def _add_halves(g, a, send_dtype, *, name):
    _, h, cols = a.shape
    th = _tile(h, max(ROW_ALIGN, PLACE_BLOCK_BYTES // (4 * cols)), ROW_ALIGN)
    g4 = g.reshape(4, 2, h, cols)
    idx = jnp.stack([lax.axis_index("c"), 2 * lax.axis_index("x") + lax.axis_index("y")]).astype(jnp.int32)

    def shard(k, ir):
        return (ir[1] + 1 + k) % 4

    def body(idx_ref, g_ref, a_ref, p_ref, s_ref):
        v = g_ref[...].astype(F32) + a_ref[...].astype(F32)
        s_ref[...] = v.astype(send_dtype)

        @pl.when(pl.program_id(1) == 3)
        def _():
            p_ref[...] = v

    return _pc(body, name=name,
               grid_spec=pltpu.PrefetchScalarGridSpec(
                   num_scalar_prefetch=1, grid=(h // th, 4),
                   in_specs=[pl.BlockSpec((None, None, th, cols), lambda i, k, ir: (shard(k, ir), ir[0], i, 0)),
                             pl.BlockSpec((None, th, cols), lambda i, k, ir: (shard(k, ir), i, 0))],
                   out_specs=[pl.BlockSpec((th, cols), lambda i, k, ir: (i, 0)),
                              pl.BlockSpec((None, th, cols), lambda i, k, ir: (shard(k, ir), i, 0))]),
               out_shape=[jax.ShapeDtypeStruct((h, cols), F32), jax.ShapeDtypeStruct(a.shape, send_dtype)],
               compiler_params=pltpu.CompilerParams(dimension_semantics=("arbitrary", "arbitrary")))(idx, g4, a)


def _add_chips(p, b, *, name, order=None):
    h, cols = p.shape
    th = _tile(h, max(ROW_ALIGN, PLACE_BLOCK_BYTES // (8 * cols)), ROW_ALIGN)
    idx = lax.axis_index("c").astype(jnp.int32).reshape(1)
    extra = [] if order is None else [order]

    def body(idx_ref, p_ref, b_ref, *rest):
        r_ref = rest[-1]
        r_ref[...] = ((p_ref[...] + b_ref[0].astype(F32)) + b_ref[1].astype(F32)) + b_ref[2].astype(F32)

    return _pc(body, name=name,
               grid_spec=pltpu.PrefetchScalarGridSpec(
                   num_scalar_prefetch=1, grid=(h // th,),
                   in_specs=[pl.BlockSpec((th, cols), lambda i, ir: (i, 0)),
                             pl.BlockSpec((3, th, cols), lambda i, ir: (0, i, 0))] + [ANY] * len(extra),
                   out_specs=pl.BlockSpec((None, th, cols), lambda i, ir: (ir[0], i, 0))),
               out_shape=jax.ShapeDtypeStruct((2, h, cols), F32),
               compiler_params=pltpu.CompilerParams(dimension_semantics=("arbitrary",)))(idx, p, b, *extra)


def _add_halves_all(grads, recv, send_dtypes, tag):
    parts, sends = [], []
    for t, (g, a) in enumerate(zip(grads, recv)):
        p, s = _add_halves(g, a, send_dtypes[t], name=f"rs_add_halves_{tag}{t}")
        parts.append(p)
        sends.append(s)
    return parts, sends


def _rs_finish(parts, others, tag, order=None):
    halves = [_add_chips(p, b, order=order, name=f"rs_add_chips_{tag}{t}") for t, (p, b) in enumerate(zip(parts, others))]
    full = _rs_final(halves, name=f"rs_final_{tag}")
    return [f.reshape(-1, f.shape[-1]) for f in full]


def _s5_discretize(lam_re, lam_im, log_dt, b_re, b_im):
    lam = lax.complex(lam_re, lam_im)
    dt = jnp.exp(log_dt)[:, None]
    lam_bar = jnp.exp(lam * dt)
    b_bar = ((lam_bar - 1.0) / lam)[..., None] * lax.complex(b_re, b_im)
    return jnp.real(lam_bar), jnp.imag(lam_bar), jnp.real(b_bar), jnp.imag(b_bar)


def _gp_from_lanes(v, cfg):
    v = jnp.transpose(v.reshape(cfg.NB, 2, GROUPS_PER_BLOCK, SSM_STATE), (1, 0, 2, 3)).reshape(2, cfg.G, SSM_STATE)
    return v[0], v[1]


def _bb_band(bb_re, bb_im, cfg):
    eye = jnp.eye(GROUPS_PER_BLOCK, dtype=F32)
    bb = jnp.stack([bb_re, bb_im]).reshape(2, cfg.NB, GROUPS_PER_BLOCK, SSM_STATE, SSM_GROUP)
    return jnp.einsum('rjgpc,gh->jgcrhp', bb, eye).reshape(cfg.DS, 2 * GROUPS_PER_BLOCK * SSM_STATE)


def _bb_from_band(m, cfg):
    eye = jnp.eye(GROUPS_PER_BLOCK, dtype=F32)
    m = m.reshape(cfg.NB, GROUPS_PER_BLOCK, SSM_GROUP, 2, GROUPS_PER_BLOCK, SSM_STATE)
    v = jnp.einsum('jgcrhp,gh->rjgpc', m, eye).reshape(2, cfg.G, SSM_STATE, SSM_GROUP)
    return v[0], v[1]


def _cc_band(c_re, c_im, cfg):
    eye = jnp.eye(GROUPS_PER_BLOCK, dtype=F32)
    cc = jnp.stack([c_re, -c_im]).reshape(2, cfg.NB, GROUPS_PER_BLOCK, SSM_GROUP, SSM_STATE)
    return jnp.einsum('rjgcp,gh->jrhpgc', cc, eye).reshape(cfg.NL, GROUPS_PER_BLOCK * SSM_GROUP)


def _cc_from_band(m, cfg):
    eye = jnp.eye(GROUPS_PER_BLOCK, dtype=F32)
    m = m.reshape(cfg.NB, 2, GROUPS_PER_BLOCK, SSM_STATE, GROUPS_PER_BLOCK, SSM_GROUP)
    v = jnp.einsum('jrhpgc,gh->rjgcp', m, eye).reshape(2, cfg.G, SSM_GROUP, SSM_STATE)
    return v[0], -v[1]


PACK_COLS = 512
PACK_ROW_ALIGN = 64


def _pack(arrs):
    flat = jnp.concatenate([a.reshape(-1).astype(F32) for a in arrs])
    unit = PACK_COLS * PACK_ROW_ALIGN
    total = -(-flat.shape[0] // unit) * unit
    return jnp.pad(flat, (0, total - flat.shape[0])).reshape(-1, PACK_COLS)


def _unpack(p, shapes):
    flat = p.reshape(-1)
    out, off = [], 0
    for shp in shapes:
        size = math.prod(shp)
        out.append(flat[off:off + size].reshape(shp))
        off += size
    return out


def _adamw(w, g, m, v, *, name, emit_grad=False):
    c1 = 1.0 / (1.0 - ADAM_B1 ** ADAM_STEP)
    c2 = 1.0 / (1.0 - ADAM_B2 ** ADAM_STEP)

    if w.ndim == 2:
        outs = _adamw(w[None], g[None], m[None], v[None], name=name, emit_grad=emit_grad)
        return [o[0] for o in outs]
    lead, rows, cols = w.shape
    tc = _tile(cols, 512)
    tm = _tile(rows, max(8, ADAMW_BLOCK_BYTES // (4 * tc)), 8)
    n_out = 4 if emit_grad else 3

    def body(w_ref, g_ref, m_ref, v_ref, *o_refs):
        gv = g_ref[...]
        mn = ADAM_B1 * m_ref[...] + (1.0 - ADAM_B1) * gv
        vn = ADAM_B2 * v_ref[...] + (1.0 - ADAM_B2) * (gv * gv)
        delta = -ADAM_LR * ((mn * c1) / (jnp.sqrt(vn * c2) + ADAM_EPS) + ADAM_WD * w_ref[...])
        for o_ref, val in zip(o_refs, ((gv, delta, mn, vn) if emit_grad else (delta, mn, vn))):
            o_ref[...] = val

    blk = pl.BlockSpec((None, tm, tc), lambda n, i, j: (n, i, j))
    return _pc(body, name=name, grid=(lead, rows // tm, cols // tc), in_specs=[blk] * 4, out_specs=[blk] * n_out,
               out_shape=[jax.ShapeDtypeStruct((lead, rows, cols), F32)] * n_out,
               compiler_params=pltpu.CompilerParams(dimension_semantics=("parallel", "parallel", "parallel")))(w, g, m, v)


def _to_comm_layout(name, w, cfg):
    w = w[0]
    if name == 'w_in':
        return jnp.pad(w, ((0, 0), (0, cfg.DINP - cfg.DIN)))
    if name == 'w_q_b':
        hs = w.shape[1] // (QK_NOPE + QK_ROPE)
        wt = w.T.reshape(hs, QK_NOPE + QK_ROPE, cfg.QL)
        return jnp.pad(wt, ((0, 0), (0, HEAD_SLOT - QK_NOPE - QK_ROPE), (0, 0))).reshape(hs * HEAD_SLOT, cfg.QL)
    if name == 'w_kv_b':
        return w.T
    if name == 'w_up':
        return w.T.reshape(2, cfg.F // 4, cfg.D)
    return w


def _from_comm_layout(name, g, cfg):
    if name == 'w_in':
        g = g[:, :cfg.DIN]
    elif name == 'w_q_b':
        hs = g.shape[0] // HEAD_SLOT
        g = g.reshape(hs, HEAD_SLOT, cfg.QL)[:, :QK_NOPE + QK_ROPE].reshape(hs * (QK_NOPE + QK_ROPE), cfg.QL).T
    elif name == 'w_kv_b':
        g = g.T
    elif name == 'w_up':
        g = g.reshape(2, cfg.FQ, cfg.D)[:, :cfg.F // 4].reshape(cfg.F // 2, cfg.D).T
    elif name == 'w_down':
        g = g[:cfg.F // 4]
    return g[None]


def _ff_pad(v, cfg):
    k = v.shape[0]
    return jnp.pad(v.reshape(k, 4, cfg.F // 4), ((0, 0), (0, 0), (0, cfg.FQ - cfg.F // 4))).reshape(k, cfg.FP)


def _ff_unpad(v, cfg):
    k = v.shape[0]
    return v.reshape(k, 4, cfg.FQ)[:, :, :cfg.F // 4].reshape(k, cfg.F)


def _step(cfg, w, m, v, x, loss_target):
    lp, d, ds, nl = cfg.LP, cfg.D, cfg.DS, cfg.NL
    xi, yi = lax.axis_index("x"), lax.axis_index("y")
    me = 2 * xi + yi

    def place(n, order=None):
        rows_to = cfg.FQ if n in ('w_up', 'w_down') else None
        return _place_shard(_to_comm_layout(n, w[n], cfg), BF16, order=order, rows_to=rows_to, name=f"place_{n}")

    first = [place('w_in'), _place_shard(w['meta_tokens'], F32, name="place_meta")]
    f_send, f_recv, f_flying, f_token = _split_start(first, _allgather_ici_copies, 6, name="allgather_first_start")
    conv_w_shard = jnp.pad(w['conv_w'][0], ((0, ROW_ALIGN - 3), (0, cfg.FQ - cfg.F // 4)))
    mid = [place(n, f_token) for n in BIG[1:5]] + [_place_shard(conv_w_shard, F32, order=f_token, name="place_conv_w")]
    mid_send, mid_recv, mid_flying, mid_token = _split_start(mid, _allgather_ici_copies, 3 * len(mid), before=f_token,
                                                             name="allgather_mid_start")
    up_send, up_recv, up_flying, up_token = _split_start([place('w_up', mid_token)], _allgather_ici_copies, 3,
                                                         before=mid_token, name="allgather_up_start")
    dn_send, dn_recv, dn_flying, ffn_token = _split_start([place('w_down', up_token)], _allgather_ici_copies, 3,
                                                          before=up_token, name="allgather_down_start")
    conv_b = _ff_pad(w['conv_b'], cfg)

    pos = (jnp.arange(lp, dtype=jnp.int32) - PAD).astype(F32)
    inv_freq = 1.0 / (ROPE_BASE ** (jnp.arange(0, QK_ROPE, 2, dtype=F32) / QK_ROPE))
    ang = pos[:, None] * inv_freq[None, :]
    zpad = jnp.zeros((lp, LANE - QK_ROPE), F32)
    cos_t = jnp.concatenate([jnp.cos(ang), jnp.cos(ang), zpad], axis=1)
    sin_t = jnp.concatenate([jnp.sin(ang), jnp.sin(ang), zpad], axis=1)

    s5_in = (w['lam_re'][0], w['lam_im'][0], w['log_dt'][0], w['b_re'][0], w['b_im'][0])
    (a_re, a_im, bb_re, bb_im), s5_vjp = jax.vjp(_s5_discretize, *s5_in)
    lam_dt = lax.complex(s5_in[0], s5_in[1]) * jnp.exp(s5_in[2])[:, None]
    a_pow = jnp.exp(jnp.arange(1, 9, dtype=F32)[:, None, None] * lam_dt[None])
    r8 = jnp.arange(8)
    step_f = jnp.stack([jnp.where((r8 >= k)[:, None, None], a_pow[k - 1][None], 0.0) for k in (1, 2, 4)]).reshape(24, cfg.G, -1)
    step_b = jnp.stack([jnp.where((r8 < 8 - k)[:, None, None], a_pow[k - 1][None], 0.0) for k in (1, 2, 4)]).reshape(24, cfg.G, -1)
    rows_f = jnp.concatenate([a_pow, step_f])
    rows_b = jnp.conj(jnp.concatenate([a_pow[::-1], step_b]))

    def lane_rows(t):
        v = jnp.stack([jnp.real(t), jnp.imag(t)], axis=1).reshape(t.shape[0], 2, cfg.NB, GROUPS_PER_BLOCK, SSM_STATE)
        return jnp.transpose(v, (0, 2, 1, 3, 4)).reshape(t.shape[0], cfg.NL)

    pw_fwd, pw_bwd = lane_rows(rows_f), lane_rows(rows_b)
    bb_band = _bb_band(bb_re, bb_im, cfg).astype(BF16)
    cc_band = _cc_band(w['c_re'][0], w['c_im'][0], cfg).astype(BF16)
    d_skip, b_glu = w['d_skip'], w['b_glu']

    f_landed = _split_wait(f_send, f_recv, f_flying, _allgather_ici_copies, [ffn_token, cos_t, sin_t, pw_fwd, pw_bwd, bb_band, cc_band],
                           name="allgather_first_wait")
    w_in, meta_full = _allgather_forward(f_landed, name="allgather_first_forward")
    meta = jnp.transpose(meta_full.reshape(4, N_META, d // 4), (1, 0, 2)).reshape(N_META, d)
    mix_norm = w['mix_norm'] + ffn_token[0:1, 0:1]

    h0 = jnp.concatenate([jnp.zeros((PAD, d), F32), meta, x[0]], axis=0)
    xn = _rms_fwd(h0, mix_norm, name="rms_mix")
    z = _mm(xn, w_in, name="mm_in", tn=_tile(cfg.DINP, 640))
    u = (z, ds, 0)
    q_a = (z, cfg.QL, ds // cfg.QL)
    kv_a = (z, cfg.KVL, (ds + cfg.QL) // cfg.KVL)
    k_pe = (z, LANE, (ds + cfg.QL + cfg.KVL) // LANE)

    hs, yc = _s5_fwd(z, bb_band, cc_band, pw_fwd, cfg, name="s5_fwd")

    def s5_y(ycv, uv, dk):
        return ycv + dk * uv

    gl = _ew(lambda rid, ycv, uv, dk: jax.nn.gelu(s5_y(ycv, uv, dk)), [yc, u], [d_skip], [(ds, BF16)], name="s5_gelu")[0]
    mid_landed = _split_wait(mid_send, mid_recv, mid_flying, _allgather_ici_copies, gl, name="allgather_mid_wait")
    w_glu, w_qt, w_kvt, w_out, conv_full = _allgather_forward(mid_landed, name="allgather_mid_forward")
    conv_w = jnp.transpose(conv_full.reshape(4, ROW_ALIGN, cfg.FQ)[:, :3], (1, 0, 2)).reshape(3, cfg.FP)
    tg = _mm(gl, w_glu, name="mm_glu")
    ya = _ew(lambda rid, ycv, uv, tv, dk, bg: jax.nn.gelu(s5_y(ycv, uv, dk)) * jax.nn.sigmoid(tv + bg),
             [yc, u, tg], [d_skip, b_glu], [(ds, F32)], name="s5_glu")[0]

    qn = _rms_fwd(q_a, w['q_a_norm'], name="rms_q")
    kvn = _rms_fwd(kv_a, w['kv_a_norm'], name="rms_kv")
    q_raw = _mm(qn, w_qt, tb=True, name="mm_q")
    qx = _ew(_rope_heads(_rope, cfg.H), [q_raw, cos_t, sin_t], [], [(cfg.H * HEAD_SLOT, BF16)], name="rope_q")[0]
    kv = _mm(kvn, w_kvt, tb=True, out_dtype=BF16, name="mm_kv")
    kr = _ew(lambda rid, kp, cs, sn: _rope(kp, cs, sn), [k_pe, cos_t, sin_t], [], [(LANE, BF16)], name="rope_k")[0]
    o, lse = _attn_fwd(qx, kv, kr, cfg, name="attn_fwd")

    def norm2(rid, yav, ov, gs, ga):
        return jnp.concatenate([_rms_parts(yav, gs)[0] * gs, _rms_parts(ov, ga)[0] * ga], axis=1)

    up_landed = _split_wait(up_send, up_recv, up_flying, _allgather_ici_copies, o, name="allgather_up_wait")
    uf_send, uf_recv, uf_flying, uf_token = _split_start(up_landed, _allgather_forward_copies, 3,
                                                         name="allgather_up_forward_start")
    yn = _ew(norm2, [ya, o], [w['out_norm_ssm'] + uf_token[0:1, 0:1], w['out_norm_attn']], [(cfg.DMIX, BF16)],
             name="rms_out")[0]
    h1 = _mm(yn, w_out, res=h0, name="mm_out")
    xn2 = _rms_fwd(h1, w['ffn_norm'], name="rms_ffn")
    dn_landed = _split_wait(dn_send, dn_recv, dn_flying, _allgather_ici_copies, xn2, name="allgather_down_wait")
    df_send, df_recv, df_flying, df_token = _split_start(dn_landed, _allgather_forward_copies, 3,
                                                         name="allgather_down_forward_start")
    w_upt, = _split_wait(uf_send, uf_recv, uf_flying, _allgather_forward_copies, df_token,
                         name="allgather_up_forward_wait")
    up, act = _ffn_up(xn2, w_upt, conv_w, conv_b, name="ffn_up")
    w_down, = _split_wait(df_send, df_recv, df_flying, _allgather_forward_copies, act,
                          name="allgather_down_forward_wait")
    h2 = _mm(act, w_down, res=h1, tm=_tile(lp, 1088, ROW_ALIGN), name="mm_down")

    g_final = w['final_norm'].reshape(1, d)

    def head(rid, hv, tv, gv):
        xhat, r = _rms_parts(hv, gv)
        valid = rid >= PAD + N_META
        diff = jnp.where(valid, xhat * gv - tv, 0.0)
        dout = diff * (1.0 / d)
        dxhat = dout * gv
        dx = r * (dxhat - xhat * jnp.mean(dxhat * xhat, axis=-1, keepdims=True))
        return dx, dx, dout * xhat, 0.5 * diff * dout

    dh2, dh2_b, dg_final, loss_cols = _ew(head, [h2, (loss_target[0], d, 0, SKIP)], [g_final], [(d, F32), (d, BF16)], [d, d],
                                          tm=PAD + N_META, name="loss_head")
    loss = lax.psum(jnp.sum(loss_cols), ("x", "y", "c"))

    dw_down = _mm(act, dh2_b, ta=True, tn=d, tm=512, out_dtype=BF16, name="mm_dw_down")

    def sibling_start(g, tag):
        land = lax.empty((4, g.shape[0] // 8, g.shape[1]), g.dtype)
        return _split_start([g, land], _rs_sibling_copies, 4, name=f"rs_sibling_{tag}_start")

    dn_send, dn_recv, dn_flying, dn_token = sibling_start(dw_down, "down")
    dup, dconv_w, dconv_b = _ffn_dact(dh2_b, w_down, up, conv_w, conv_b + dn_token[0:1, 0:1], name="ffn_dact")
    tk_up, tm_up = _tile(cfg.FP, 2816), _tile(cfg.FP, 512)
    dw_upt = _mm(dup, xn2, ta=True, dims=(2 * cfg.FP, d, lp), tn=d, tm=tm_up, a_lead=True, out_dtype=BF16, name="mm_dw_up",
                 a_idx=lambda i, j, k: (i // (cfg.FP // tm_up), 0, i % (cfg.FP // tm_up)))
    up_send, up_recv, up_flying, up_token = sibling_start(dw_upt, "up")
    dxn2 = _mm(dup, w_upt, dims=(lp, d, 2 * cfg.FP), tk=tk_up, tn=512, a_lead=True, name="mm_dxn2",
               a_idx=lambda i, j, k: (k // (cfg.FP // tk_up), i, k % (cfg.FP // tk_up)))
    dh1, dh1_b, dg_ffn = _rms_bwd(h1, w['ffn_norm'] + up_token[0:1, 0:1], dxn2, res=dh2, mask=True, with_bf16=True,
                                  name="rms_ffn_bwd")

    dyn = _mm(dh1_b, w_out, tb=True, name="mm_dyn")
    dw_out = _mm(yn, dh1_b, ta=True, tn=d, tm=512, name="mm_dw_out")
    up_done = _split_wait(up_send, up_recv, up_flying, _rs_sibling_copies, dw_out, name="rs_sibling_up_wait")
    dn_done = _split_wait(dn_send, dn_recv, dn_flying, _rs_sibling_copies, dw_out, name="rs_sibling_down_wait")
    early_parts, early_sends = _add_halves_all([up_done[0], dn_done[0]], [up_done[1], dn_done[1]], [BF16] * 2, "early")
    chip_lands = [lax.empty((3,) + s.shape[1:], s.dtype) for s in early_sends]
    ch_send, ch_recv, ch_flying, ch_token = _split_start(early_sends + chip_lands, _rs_chips_copies, 6,
                                                         name="rs_chips_early_start")
    dya, dg_ssm = _rms_bwd(ya, w['out_norm_ssm'] + ch_token[0:1, 0:1], (dyn, ds, 0), name="rms_ssm_bwd")
    do, dg_attn = _rms_bwd(o, w['out_norm_attn'], (dyn, cfg.DATTN, ds // cfg.DATTN), name="rms_attn_bwd")

    dqx, dkv, dkr = _attn_bwd(qx, kv, kr, o, lse, do, cfg, name="attn_bwd")
    dq_raw = _ew(_rope_heads(_unrope, cfg.H), [dqx, cos_t, sin_t], [], [(cfg.H * HEAD_SLOT, BF16)], name="unrope_q")[0]
    dk_pe = _ew(lambda rid, dk, cs, sn: _unrope(dk, cs, sn), [dkr, cos_t, sin_t], [], [(LANE, F32)], name="unrope_k")[0]
    dqn = _mm(dq_raw, w_qt, name="mm_dqn")
    dw_qt = _mm(dq_raw, qn, ta=True, tm=512, name="mm_dw_q")
    dkvn = _mm(dkv, w_kvt, name="mm_dkvn")
    dw_kvt = _mm(dkv, kvn, ta=True, tm=512, name="mm_dw_kv")
    dq_a, dg_q = _rms_bwd(q_a, w['q_a_norm'], dqn, name="rms_q_bwd")
    dkv_a, dg_kv = _rms_bwd(kv_a, w['kv_a_norm'], dkvn, name="rms_kv_bwd")

    def glu_bwd(rid, ycv, uv, tv, dyav, dk, bg):
        gelu = jax.nn.gelu(s5_y(ycv, uv, dk))
        sg = jax.nn.sigmoid(tv + bg)
        dt = dyav * gelu * sg * (1.0 - sg)
        return dt, dyav * sg, dt

    dt_b, dgl1, db_glu = _ew(glu_bwd, [yc, u, tg, dya], [d_skip, b_glu], [(ds, BF16), (ds, F32)], [ds], name="s5_glu_bwd")
    dgl = _mm(dt_b, w_glu, tb=True, res=dgl1, name="mm_dgl")
    dw_glu = _mm(gl, dt_b, ta=True, tm=512, name="mm_dw_glu")

    def gelu_bwd(rid, ycv, uv, dglv, dk):
        _, vjp = jax.vjp(jax.nn.gelu, s5_y(ycv, uv, dk))
        dy = vjp(dglv)[0]
        return dy, dy * dk, dy * uv

    mid_grads = [dw_out, dw_glu, dw_qt, dw_kvt]
    mid_lands = [lax.empty((4, g.shape[0] // 8, g.shape[1]), g.dtype) for g in mid_grads]
    ms_send, ms_recv, ms_flying, ms_token = _split_start(mid_grads + mid_lands, _rs_sibling_copies, 4 * len(mid_grads),
                                                         name="rs_sibling_mid_start")
    dy_b, du_skip, dd_skip = _ew(gelu_bwd, [yc, u, dgl], [d_skip + ms_token[0:1, 0:1]], [(ds, BF16), (ds, F32)], [ds],
                                 name="s5_gelu_bwd")
    ms_done = _split_wait(ms_send, ms_recv, ms_flying, _rs_sibling_copies, dy_b, name="rs_sibling_mid_wait")
    mid_parts, mid_sends = _add_halves_all(ms_done[:4], ms_done[4:], [BF16] * 4, "mid")
    mid_chip_lands = [lax.empty((3,) + s.shape[1:], s.dtype) for s in mid_sends]
    mc_send, mc_recv, mc_flying, mc_token = _split_start(mid_sends + mid_chip_lands, _rs_chips_copies, 3 * len(mid_sends),
                                                         name="rs_chips_mid_start")
    du, dbb_band, dcc_band, da_l = _s5_bwd(dy_b, hs, z, bb_band, cc_band, pw_bwd + mc_token[0:1, 0:1], du_skip, cfg,
                                           name="s5_bwd")

    dz = jnp.concatenate([du, dq_a, dkv_a, dk_pe], axis=1).astype(BF16)
    dxn = _mm(dz, w_in, tb=True, name="mm_dxn")
    dw_in = _mm(xn, dz, ta=True, tm=512, tn=_tile(cfg.DINP, 1024), name="mm_dw_in")
    def mix_bwd(rid, xv, dyv, resv, gv):
        dx, dg = _rms_bwd_block(xv, gv, dyv)
        dx = dx + resv
        return dx, dx, dg

    grad_x, dh0_head, dg_mix = _ew(mix_bwd, [h0, dxn, dh1], [mix_norm], [(d, F32, SKIP), (d, F32, FIRST)], [d],
                                   tm=PAD + N_META, name="rms_mix_bwd")
    grad_x = grad_x[None]

    da_re, da_im = _gp_from_lanes(da_l, cfg)
    dbb_re, dbb_im = _bb_from_band(dbb_band, cfg)
    dlam_re, dlam_im, dlog_dt, db_re, db_im = s5_vjp((da_re, da_im, dbb_re, dbb_im))
    dc_re, dc_im = _cc_from_band(dcc_band, cfg)
    local_small = {
        'meta_tokens': dh0_head[PAD:], 'mix_norm': dg_mix, 'lam_re': dlam_re, 'lam_im': dlam_im, 'log_dt': dlog_dt,
        'b_re': db_re, 'b_im': db_im, 'c_re': dc_re, 'c_im': dc_im, 'd_skip': dd_skip, 'b_glu': db_glu, 'q_a_norm': dg_q,
        'kv_a_norm': dg_kv, 'out_norm_ssm': dg_ssm, 'out_norm_attn': dg_attn, 'ffn_norm': dg_ffn,
        'conv_w': _ff_unpad(dconv_w, cfg), 'conv_b': _ff_unpad(dconv_b, cfg), 'final_norm': dg_final,
    }
    small_shapes = [local_small[n].shape for n in SMALL]

    small_pack = _pack([local_small[n] for n in SMALL])
    ch_done = _split_wait(ch_send, ch_recv, ch_flying, _rs_chips_copies, small_pack, name="rs_chips_early_wait")
    early_halves = [_add_chips(p, b, name=f"rs_add_chips_early{t}") for t, (p, b) in enumerate(zip(early_parts, ch_done[2:]))]
    fe_send, fe_recv, fe_flying, fe_token = _split_start(early_halves, _rs_final_copies, 2, name="rs_final_early_start")
    end_local = [dw_in, small_pack + fe_token[0:1, 0:1]]
    end_recv = _rs_sibling(end_local, name="rs_sibling_end")
    end_parts, end_sends = _add_halves_all(end_local, end_recv, [BF16, F32], "end")
    end_lands = [lax.empty((3,) + s.shape[1:], s.dtype) for s in end_sends]
    ec_send, ec_recv, ec_flying, ec_token = _split_start(end_sends + end_lands, _rs_chips_copies, 3 * len(end_sends),
                                                         name="rs_chips_end_start")
    fe_done = _split_wait(fe_send, fe_recv, fe_flying, _rs_final_copies, ec_token, name="rs_final_early_wait")
    red_up, red_down = [f.reshape(-1, f.shape[-1]) for f in fe_done]

    delta, new_m, new_v, grads = {}, {}, {}, {}
    padded_rows = ('w_down',)

    def adamw_big(n, red):
        shp = w[n].shape
        w2, m2, v2 = [t.reshape(shp[-2], shp[-1]) for t in (w[n], m[n], v[n])]
        if n in padded_rows:
            g2, dl, mn, vn = _adamw(w2, red, m2, v2, emit_grad=True, name=f"adamw_{n}")
            grads[n] = g2.reshape(shp)
        else:
            grads[n] = _from_comm_layout(n, red, cfg)
            dl, mn, vn = _adamw(w2, grads[n].reshape(shp[-2], shp[-1]), m2, v2, name=f"adamw_{n}")
        delta[n], new_m[n], new_v[n] = dl.reshape(shp), mn.reshape(shp), vn.reshape(shp)

    def adamw_up(red):
        q = cfg.F // 4
        wt, mt, vt = [jnp.transpose(t[0]).reshape(2, q, d) for t in (w['w_up'], m['w_up'], v['w_up'])]
        outs = _adamw(wt, red.reshape(2, cfg.FQ, d), mt, vt, emit_grad=True, name="adamw_w_up")
        grads['w_up'], delta['w_up'], new_m['w_up'], new_v['w_up'] = [jnp.transpose(t.reshape(2 * q, d))[None] for t in outs]

    adamw_up(red_up)
    adamw_big('w_down', red_down)
    mc_done = _split_wait(mc_send, mc_recv, mc_flying, _rs_chips_copies, delta['w_down'], name="rs_chips_mid_wait")
    ec_done = _split_wait(ec_send, ec_recv, ec_flying, _rs_chips_copies, mc_done[0], name="rs_chips_end_wait")
    red = _rs_finish(mid_parts + end_parts, list(mc_done[len(mid_sends):]) + list(ec_done[len(end_sends):]), "rest")
    small_full = _allgather([_place_shard(red[5], F32, name="place_small")], name="allgather_small")[0]
    small_sum = dict(zip(SMALL, _unpack(small_full, small_shapes)))
    for n, r in zip(['w_out', 'w_glu', 'w_q_b', 'w_kv_b'], red[:4]):
        adamw_big(n, r)
    in_t = [jnp.transpose(t[0]) for t in (w['w_in'], m['w_in'], v['w_in'])]
    outs = _adamw(in_t[0], jnp.transpose(red[4][:, :cfg.DIN]), in_t[1], in_t[2], emit_grad=True, name="adamw_w_in")
    grads['w_in'], delta['w_in'], new_m['w_in'], new_v['w_in'] = [jnp.transpose(t)[None] for t in outs]

    for n in SMALL:
        g = small_sum[n]
        if n == 'meta_tokens':
            g = lax.dynamic_slice_in_dim(g, me * (d // 4), d // 4, axis=1)
        elif n == 'conv_w':
            g = lax.dynamic_slice_in_dim(g, me * (cfg.F // 4), cfg.F // 4, axis=1)[None]
        else:
            g = g.reshape(w[n].shape)
        grads[n] = g

    shapes = [w[n].shape for n in SMALL]
    packs = [_pack([src[n] for n in SMALL]) for src in (w, grads, m, v)]
    for dst, p in zip((delta, new_m, new_v), _adamw(*packs, name="adamw_small")):
        dst.update(zip(SMALL, _unpack(p, shapes)))

    return (loss, grad_x, *[grads[n] for n in WEIGHTS], *[delta[n] for n in WEIGHTS],
            *[new_m[n] for n in WEIGHTS], *[new_v[n] for n in WEIGHTS])


def kernel(x, meta_tokens, mix_norm, w_in, lam_re, lam_im, log_dt, b_re, b_im, c_re, c_im, d_skip, w_glu, b_glu, q_a_norm, w_q_b, kv_a_norm, w_kv_b, out_norm_ssm, out_norm_attn, w_out, ffn_norm, w_up, conv_w, conv_b, w_down, final_norm, loss_target, m_meta_tokens, m_mix_norm, m_w_in, m_lam_re, m_lam_im, m_log_dt, m_b_re, m_b_im, m_c_re, m_c_im, m_d_skip, m_w_glu, m_b_glu, m_q_a_norm, m_w_q_b, m_kv_a_norm, m_w_kv_b, m_out_norm_ssm, m_out_norm_attn, m_w_out, m_ffn_norm, m_w_up, m_conv_w, m_conv_b, m_w_down, m_final_norm, v_meta_tokens, v_mix_norm, v_w_in, v_lam_re, v_lam_im, v_log_dt, v_b_re, v_b_im, v_c_re, v_c_im, v_d_skip, v_w_glu, v_b_glu, v_q_a_norm, v_w_q_b, v_kv_a_norm, v_w_kv_b, v_out_norm_ssm, v_out_norm_attn, v_w_out, v_ffn_norm, v_w_up, v_conv_w, v_conv_b, v_w_down, v_final_norm):
    args = dict(locals())
    w = {n: args[n] for n in WEIGHTS}
    m = {n: args["m_" + n] for n in WEIGHTS}
    v = {n: args["v_" + n] for n in WEIGHTS}
    return _step(PROD, w, m, v, x, loss_target)
```

```python
import functools
import math
from typing import NamedTuple

import jax
import jax.numpy as jnp
from jax import lax
from jax.experimental import pallas as pl
from jax.experimental.pallas import tpu as pltpu

F32, BF16 = jnp.float32, jnp.bfloat16
MESH = pl.DeviceIdType.MESH
LANE = 128
ROW_ALIGN = 16
N_META = 16
PAD = 112
CHUNK = 64
SSM_GROUP = 16
SSM_STATE = 64
GROUPS_PER_BLOCK = 8
QK_NOPE, QK_ROPE, V_HEAD = 128, 64, 128
HEAD_SLOT = 256
ROPE_BASE = 10000.0
EPS = 1e-6
ADAM_LR, ADAM_B1, ADAM_B2, ADAM_EPS, ADAM_WD, ADAM_STEP = 0.001, 0.9, 0.999, 1e-08, 0.01, 10
DT_F32_BLOCK_BYTES = 9 << 18
ADAMW_BLOCK_BYTES = 3 << 19
PLACE_BLOCK_BYTES = 6 << 20
SKIP, FIRST = "skip", "first"
STREAM_BUFFERS = 3


class Cfg(NamedTuple):
    D: int
    S: int
    DS: int
    H: int
    QL: int
    KVL: int
    F: int

    @property
    def LP(self):
        return PAD + N_META + self.S

    @property
    def G(self):
        return self.DS // SSM_GROUP

    @property
    def NB(self):
        return self.G // GROUPS_PER_BLOCK

    @property
    def NL(self):
        return 2 * self.G * SSM_STATE

    @property
    def DATTN(self):
        return self.H * V_HEAD

    @property
    def DMIX(self):
        return self.DS + self.DATTN

    @property
    def DIN(self):
        return self.DS + self.QL + self.KVL + QK_ROPE

    @property
    def DINP(self):
        return self.DS + self.QL + self.KVL + LANE

    @property
    def FQ(self):
        return -(-(self.F // 4) // LANE) * LANE

    @property
    def FP(self):
        return 4 * self.FQ


PROD = Cfg(D=2048, S=2048, DS=1024, H=8, QL=512, KVL=256, F=5504)

WEIGHTS = ['meta_tokens', 'mix_norm', 'w_in', 'lam_re', 'lam_im', 'log_dt', 'b_re', 'b_im', 'c_re', 'c_im', 'd_skip',
           'w_glu', 'b_glu', 'q_a_norm', 'w_q_b', 'kv_a_norm', 'w_kv_b', 'out_norm_ssm', 'out_norm_attn', 'w_out',
           'ffn_norm', 'w_up', 'conv_w', 'conv_b', 'w_down', 'final_norm']
BIG = ['w_in', 'w_glu', 'w_q_b', 'w_kv_b', 'w_out', 'w_up', 'w_down']
SMALL = [n for n in WEIGHTS if n not in BIG]


def _pc(body, **kw):
    return pl.pallas_call(body, **kw)


def _tile(n, target, align=LANE):
    best = None
    d = align
    while d <= min(n, target):
        if n % d == 0:
            best = d
        d += align
    return best if best is not None else n


def _row_tile(rows, cols):
    return _tile(rows, max(ROW_ALIGN, DT_F32_BLOCK_BYTES // (4 * cols)), ROW_ALIGN)


def _mm(a, b, *, name, ta=False, tb=False, tm=None, tn=512, tk=None, out_dtype=F32, res=None,
        a_idx=None, b_idx=None, dims=None, a_lead=False):
    if dims is None:
        m, k = (a.shape[1], a.shape[0]) if ta else a.shape
        n = b.shape[0] if tb else b.shape[1]
    else:
        m, n, k = dims
    tm = _tile(m, tm or m, LANE if ta else ROW_ALIGN)
    tn = _tile(n, tn)
    tk = _tile(k, tk or k, ROW_ALIGN if (ta and not tb) else LANE)
    nm, nn, nk = m // tm, n // tn, k // tk
    a_idx = a_idx or ((lambda i, j, kk: (kk, i)) if ta else (lambda i, j, kk: (i, kk)))
    b_idx = b_idx or ((lambda i, j, kk: (j, kk)) if tb else (lambda i, j, kk: (kk, j)))
    dn = (((0 if ta else 1,), (1 if tb else 0,)), ((), ()))

    def body(*refs):
        a_ref, b_ref = refs[0], refs[1]
        r_ref = refs[2] if res is not None else None
        o_ref = refs[3] if res is not None else refs[2]
        d = lax.dot_general(a_ref[...].astype(BF16), b_ref[...].astype(BF16), dn, preferred_element_type=F32)

        def finish(r):
            if r_ref is not None:
                r = r + r_ref[...].astype(F32)
            o_ref[...] = r.astype(out_dtype)

        if nk == 1:
            finish(d)
        else:
            acc = refs[-1]
            kk = pl.program_id(2)

            @pl.when(kk == 0)
            def _():
                acc[...] = d

            @pl.when(kk > 0)
            def _():
                acc[...] += d

            @pl.when(kk == nk - 1)
            def _():
                finish(acc[...])

    a_blk = ((None,) if a_lead else ()) + ((tk, tm) if ta else (tm, tk))
    in_specs = [pl.BlockSpec(a_blk, a_idx), pl.BlockSpec((tn, tk) if tb else (tk, tn), b_idx)]
    args = [a, b]
    if res is not None:
        in_specs.append(pl.BlockSpec((tm, tn), lambda i, j, kk: (i, j)))
        args.append(res)
    return _pc(body, name=name, grid=(nm, nn, nk), in_specs=in_specs,
               out_specs=pl.BlockSpec((tm, tn), lambda i, j, kk: (i, j)),
               out_shape=jax.ShapeDtypeStruct((m, n), out_dtype),
               scratch_shapes=[pltpu.VMEM((tm, tn), F32)] if nk > 1 else [],
               compiler_params=pltpu.CompilerParams(dimension_semantics=("parallel", "parallel", "arbitrary")))(*args)


def _ew(fn, ins, vecs, outs, sums=(), *, name, tm=None):
    ins = [x if isinstance(x, tuple) else (x, x.shape[1], 0) for x in ins]
    ins = [x if len(x) == 4 else x + (None,) for x in ins]
    outs = [o if len(o) == 3 else o + (None,) for o in outs]
    rows = ins[0][0].shape[0]
    cmax = max([c for _, c, _, _ in ins] + [c for c, _, _ in outs])
    tm = tm or _row_tile(rows, cmax)
    n_in, n_vec, n_out, n_sum = len(ins), len(vecs), len(outs), len(sums)

    def body(*refs):
        i = pl.program_id(0)
        rid = i * tm + lax.broadcasted_iota(jnp.int32, (tm, 1), 0)
        vals = [r[...] for r in refs[:n_in + n_vec]]
        res = fn(rid, *vals)
        res = res if isinstance(res, (tuple, list)) else (res,)
        o_refs = refs[n_in + n_vec:]
        for o_ref, r, (_, _, mode) in zip(o_refs[:n_out], res[:n_out], outs):
            if mode == FIRST:
                @pl.when(i == 0)
                def _():
                    o_ref[...] = r.astype(o_ref.dtype)
            else:
                o_ref[...] = r.astype(o_ref.dtype)
        for o_ref, r in zip(o_refs[n_out:], res[n_out:]):
            part = jnp.sum(r.astype(F32), axis=0, keepdims=True)

            @pl.when(i == 0)
            def _():
                o_ref[...] = part

            @pl.when(i > 0)
            def _():
                o_ref[...] += part

    def row_idx(mode):
        if mode == SKIP:
            return lambda i, cb=0: (jnp.maximum(i - 1, 0), cb)
        if mode == FIRST:
            return lambda i, cb=0: (0, cb)
        return lambda i, cb=0: (i, cb)

    in_specs = [pl.BlockSpec((tm, c), functools.partial(row_idx(mode), cb=cb)) for _, c, cb, mode in ins]
    in_specs += [pl.BlockSpec(v.shape, functools.partial(lambda i, nd: (0,) * nd, nd=v.ndim)) for v in vecs]
    out_specs = [pl.BlockSpec((tm, c), row_idx(mode)) for c, _, mode in outs]
    out_specs += [pl.BlockSpec((1, c), lambda i: (0, 0)) for c in sums]
    out_rows = {None: rows, SKIP: rows - tm, FIRST: tm}
    out_shape = [jax.ShapeDtypeStruct((out_rows[mode], c), dt) for c, dt, mode in outs]
    out_shape += [jax.ShapeDtypeStruct((1, c), F32) for c in sums]
    return _pc(body, name=name, grid=(rows // tm,), in_specs=in_specs, out_specs=out_specs, out_shape=out_shape,
               compiler_params=pltpu.CompilerParams(dimension_semantics=("arbitrary",)))(*[x[0] for x in ins], *vecs)


def _rms_parts(x, g):
    r = lax.rsqrt(jnp.mean(x * x, axis=-1, keepdims=True) + EPS)
    return x * r, r


def _rms_bwd_block(x, g, dy):
    xhat, r = _rms_parts(x, g)
    dxhat = dy * g
    dx = r * (dxhat - xhat * jnp.mean(dxhat * xhat, axis=-1, keepdims=True))
    return dx, dy * xhat


def _rms_fwd(x, g, *, name):
    c = x[1] if isinstance(x, tuple) else x.shape[1]
    return _ew(lambda rid, xv, gv: _rms_parts(xv.astype(F32), gv)[0] * gv, [x], [g], [(c, BF16)], name=name)[0]


def _rms_bwd(x, g, dy, *, name, res=None, mask=False, with_bf16=False):
    c = x[1] if isinstance(x, tuple) else x.shape[1]

    def fn(rid, xv, dyv, *rest):
        gv = rest[-1]
        dx, dg = _rms_bwd_block(xv.astype(F32), gv, dyv.astype(F32))
        if res is not None:
            dx = dx + rest[0]
        if mask:
            dx = jnp.where(rid >= PAD, dx, 0.0)
        return (dx, dx, dg) if with_bf16 else (dx, dg)

    ins = [x, dy] + ([res] if res is not None else [])
    outs = [(c, F32)] + ([(c, BF16)] if with_bf16 else [])
    return _ew(fn, ins, [g], outs, [c], name=name)


S5_W = GROUPS_PER_BLOCK * SSM_STATE
S5_GW = GROUPS_PER_BLOCK * SSM_GROUP
S5_DA_ROWS = 272


def _s5_scan_in_place(ref, pw_ref, *, reverse):
    lp = ref.shape[0]
    tile_rows = 8
    chunk = _tile(lp, S5_DA_ROWS, tile_rows)
    tiles = chunk // tile_rows

    def chunk_body(c, carry):
        rows = pl.ds(pl.multiple_of(c * chunk, tile_rows), chunk)
        xr, xi = ref[rows, :S5_W], ref[rows, S5_W:]
        for level, k in enumerate((1, 2, 4)):
            base = tile_rows * (1 + level)
            mr, mi = pw_ref[base:base + tile_rows, :S5_W][None], pw_ref[base:base + tile_rows, S5_W:][None]
            shift = chunk - k if reverse else k
            sr = pltpu.roll(xr, shift, 0).reshape(tiles, tile_rows, S5_W)
            si = pltpu.roll(xi, shift, 0).reshape(tiles, tile_rows, S5_W)
            xr = xr + (mr * sr - mi * si).reshape(chunk, S5_W)
            xi = xi + (mr * si + mi * sr).reshape(chunk, S5_W)
        ref[rows, :S5_W] = xr
        ref[rows, S5_W:] = xi
        return carry

    lax.fori_loop(0, lp // chunk, chunk_body, 0)

    pr, pi = pw_ref[0:tile_rows, :S5_W], pw_ref[0:tile_rows, S5_W:]
    ntile = lp // tile_rows
    unroll = 4

    def step(n, carry):
        cr, ci = carry
        for q in range(unroll):
            j = n * unroll + q
            j = ntile - 1 - j if reverse else j
            rows = pl.ds(pl.multiple_of(j * tile_rows, tile_rows), tile_rows)
            nr = ref[rows, :S5_W] + (pr * cr - pi * ci)
            ni = ref[rows, S5_W:] + (pr * ci + pi * cr)
            ref[rows, :S5_W] = nr
            ref[rows, S5_W:] = ni
            cr, ci = (nr[0:1], ni[0:1]) if reverse else (nr[tile_rows - 1:], ni[tile_rows - 1:])
        return cr, ci

    z = jnp.zeros((1, S5_W), F32)
    lax.fori_loop(0, ntile // unroll, step, (z, z))


def _s5_fwd(z, bb_band, cc_band, pw, cfg, *, name):
    lp, ds, nl = cfg.LP, cfg.DS, cfg.NL

    def body(u_ref, bb_ref, cc_ref, a_ref, hs_ref, y_ref):
        hs_ref[...] = jnp.dot(u_ref[...].astype(BF16), bb_ref[...], preferred_element_type=F32)
        _s5_scan_in_place(hs_ref, a_ref, reverse=False)
        y_ref[...] = jnp.dot(hs_ref[...].astype(BF16), cc_ref[...], preferred_element_type=F32)

    return _pc(body, name=name, grid=(cfg.NB,),
               in_specs=[pl.BlockSpec((lp, S5_GW), lambda j: (0, j)), pl.BlockSpec((S5_GW, 2 * S5_W), lambda j: (j, 0)),
                         pl.BlockSpec((2 * S5_W, S5_GW), lambda j: (j, 0)), pl.BlockSpec((32, 2 * S5_W), lambda j: (0, j))],
               out_specs=[pl.BlockSpec((lp, 2 * S5_W), lambda j: (0, j)), pl.BlockSpec((lp, S5_GW), lambda j: (0, j))],
               out_shape=[jax.ShapeDtypeStruct((lp, nl), F32), jax.ShapeDtypeStruct((lp, ds), F32)],
               compiler_params=pltpu.CompilerParams(dimension_semantics=("parallel",)))(z, bb_band, cc_band, pw)


def _s5_bwd(dy, hs, z, bb_band, cc_band, pw, du_skip, cfg, *, name):
    lp, ds, nl = cfg.LP, cfg.DS, cfg.NL
    nt = (((1,), (1,)), ((), ()))
    tn = (((0,), (0,)), ((), ()))

    def body(dy_ref, hs_ref, u_ref, bb_ref, cc_ref, a_ref, sk_ref, du_ref, dbb_ref, dcc_ref, da_ref, g_ref):
        dyv = dy_ref[...]
        g_ref[...] = lax.dot_general(dyv, cc_ref[...], nt, preferred_element_type=F32)
        _s5_scan_in_place(g_ref, a_ref, reverse=True)
        dcc_ref[...] = lax.dot_general(hs_ref[...].astype(BF16), dyv, tn, preferred_element_type=F32)
        gb = g_ref[...].astype(BF16)
        dbb_ref[...] = lax.dot_general(u_ref[...].astype(BF16), gb, tn, preferred_element_type=F32)
        du_ref[...] = lax.dot_general(gb, bb_ref[...], nt, preferred_element_type=F32) + sk_ref[...]
        dre = jnp.zeros((1, S5_W), F32)
        dim = jnp.zeros((1, S5_W), F32)
        for r0 in range(0, lp, S5_DA_ROWS):
            rows = min(S5_DA_ROWS, lp - r0)
            first = lax.broadcasted_iota(jnp.int32, (rows, 1), 0) == 0
            prev = hs_ref[r0 - 1:r0, :] if r0 else jnp.zeros((1, 2 * S5_W), F32)
            hr = jnp.where(first, prev[:, :S5_W], pltpu.roll(hs_ref[r0:r0 + rows, :S5_W], 1, 0))
            hi = jnp.where(first, prev[:, S5_W:], pltpu.roll(hs_ref[r0:r0 + rows, S5_W:], 1, 0))
            gr, gi = g_ref[r0:r0 + rows, :S5_W], g_ref[r0:r0 + rows, S5_W:]
            dre = dre + jnp.sum(gr * hr + gi * hi, axis=0, keepdims=True)
            dim = dim + jnp.sum(gi * hr - gr * hi, axis=0, keepdims=True)
        da_ref[:, :S5_W] = dre
        da_ref[:, S5_W:] = dim

    col_blk = pl.BlockSpec((lp, S5_GW), lambda j: (0, j))
    lane_blk = pl.BlockSpec((lp, 2 * S5_W), lambda j: (0, j))
    bb_blk = pl.BlockSpec((S5_GW, 2 * S5_W), lambda j: (j, 0))
    cc_blk = pl.BlockSpec((2 * S5_W, S5_GW), lambda j: (j, 0))
    a_blk = pl.BlockSpec((1, 2 * S5_W), lambda j: (0, j))
    pw_blk = pl.BlockSpec((32, 2 * S5_W), lambda j: (0, j))
    return _pc(body, name=name, grid=(cfg.NB,),
               in_specs=[col_blk, lane_blk, col_blk, bb_blk, cc_blk, pw_blk, col_blk],
               out_specs=[col_blk, bb_blk, cc_blk, a_blk],
               out_shape=[jax.ShapeDtypeStruct((lp, ds), F32), jax.ShapeDtypeStruct((ds, 2 * S5_W), F32),
                          jax.ShapeDtypeStruct((nl, S5_GW), F32), jax.ShapeDtypeStruct((1, nl), F32)],
               scratch_shapes=[pltpu.VMEM((lp, 2 * S5_W), F32)],
               compiler_params=pltpu.CompilerParams(dimension_semantics=("parallel",)))(dy, hs, z, bb_band, cc_band, pw, du_skip)


def _conv_gate(pre, cw, cb):
    return cw[0:1] * pltpu.roll(pre, 2, 0) + cw[1:2] * pltpu.roll(pre, 1, 0) + cw[2:3] * pre + cb


def _ffn_up(xn2, w_upt, cw, cb, *, name):
    lp, d = xn2.shape
    fp = w_upt.shape[0] // 2
    tc = _tile(fp, 256)
    nb = fp // tc

    def body(x_ref, wg_ref, wv_ref, cw_ref, cb_ref, up_ref, act_ref):
        wcat = jnp.concatenate([wg_ref[...], wv_ref[...]], axis=0)
        r = lax.dot_general(x_ref[...], wcat, (((1,), (1,)), ((), ())), preferred_element_type=F32)
        pre, val = r[:, :tc].astype(BF16), r[:, tc:].astype(BF16)
        up_ref[0] = pre
        up_ref[1] = val
        gate = _conv_gate(pre.astype(F32), cw_ref[...], cb_ref[...])
        act_ref[...] = (jax.nn.silu(gate) * val.astype(F32)).astype(BF16)

    return _pc(body, name=name, grid=(nb,),
               in_specs=[pl.BlockSpec((lp, d), lambda j: (0, 0)), pl.BlockSpec((tc, d), lambda j: (j, 0)),
                         pl.BlockSpec((tc, d), lambda j: (nb + j, 0)),
                         pl.BlockSpec((3, tc), lambda j: (0, j)), pl.BlockSpec((1, tc), lambda j: (0, j))],
               out_specs=[pl.BlockSpec((2, lp, tc), lambda j: (0, 0, j)), pl.BlockSpec((lp, tc), lambda j: (0, j))],
               out_shape=[jax.ShapeDtypeStruct((2, lp, fp), BF16), jax.ShapeDtypeStruct((lp, fp), BF16)],
               compiler_params=pltpu.CompilerParams(dimension_semantics=("parallel",)))(xn2, w_upt, w_upt, cw, cb)


def _ffn_dact(dh2, w_down, up, cw, cb, *, name):
    lp, d = dh2.shape
    fp = w_down.shape[0]
    tc = _tile(fp, 256)
    nb = fp // tc

    def body(dh_ref, wd_ref, up_ref, cw_ref, cb_ref, dup_ref, dcw_ref, dcb_ref):
        da = lax.dot_general(dh_ref[...], wd_ref[...], (((1,), (1,)), ((), ())), preferred_element_type=F32)
        pre, val, cwv = up_ref[0].astype(F32), up_ref[1].astype(F32), cw_ref[...]
        gate = _conv_gate(pre, cwv, cb_ref[...])
        sg = jax.nn.sigmoid(gate)
        dup_ref[1] = (da * (gate * sg)).astype(BF16)
        dgate = da * val * (sg * (1.0 + gate * (1.0 - sg)))
        dpre = cwv[2:3] * dgate + cwv[1:2] * pltpu.roll(dgate, lp - 1, 0) + cwv[0:1] * pltpu.roll(dgate, lp - 2, 0)
        dup_ref[0] = dpre.astype(BF16)
        dcb_ref[...] = jnp.sum(dgate, axis=0, keepdims=True)
        dcw_ref[0:1, :] = jnp.sum(dgate * pltpu.roll(pre, 2, 0), axis=0, keepdims=True)
        dcw_ref[1:2, :] = jnp.sum(dgate * pltpu.roll(pre, 1, 0), axis=0, keepdims=True)
        dcw_ref[2:3, :] = jnp.sum(dgate * pre, axis=0, keepdims=True)

    return _pc(body, name=name, grid=(nb,),
               in_specs=[pl.BlockSpec((lp, d), lambda j: (0, 0)), pl.BlockSpec((tc, d), lambda j: (j, 0)),
                         pl.BlockSpec((2, lp, tc), lambda j: (0, 0, j)),
                         pl.BlockSpec((3, tc), lambda j: (0, j)), pl.BlockSpec((1, tc), lambda j: (0, j))],
               out_specs=[pl.BlockSpec((2, lp, tc), lambda j: (0, 0, j)),
                          pl.BlockSpec((3, tc), lambda j: (0, j)), pl.BlockSpec((1, tc), lambda j: (0, j))],
               out_shape=[jax.ShapeDtypeStruct((2, lp, fp), BF16), jax.ShapeDtypeStruct((3, fp), F32),
                          jax.ShapeDtypeStruct((1, fp), F32)],
               compiler_params=pltpu.CompilerParams(dimension_semantics=("parallel",)))(dh2, w_down, up, cw, cb)


ATTN_Q_ROWS = 544


def _key_limit(i, tq, lp):
    return min(lp, -(-((i + 1) * tq) // LANE) * LANE)


def _attn_mask(i, tq, nk):
    qrow = i * tq + lax.broadcasted_iota(jnp.int32, (tq, 1), 0)
    krow = lax.broadcasted_iota(jnp.int32, (1, nk), 1)
    return (krow >= PAD) & ((krow // CHUNK) <= (qrow // CHUNK)), qrow >= PAD


def _attn_scores(q, kn, kr, i, tq, scale):
    nt = (((1,), (1,)), ((), ()))
    s = lax.dot_general(q[:, :QK_NOPE], kn, nt, preferred_element_type=F32)
    s = s + lax.dot_general(q[:, QK_NOPE:], kr, nt, preferred_element_type=F32)
    mask, qvalid = _attn_mask(i, tq, kn.shape[0])
    return jnp.where(mask, s * scale, jnp.finfo(F32).min), qvalid


def _per_q_block(nq, fn):
    i = pl.program_id(1)
    for blk in range(nq):
        pl.when(i == blk)(functools.partial(fn, blk))


def _attn_fwd(qx, kv, kr, cfg, *, name):
    lp, h = cfg.LP, cfg.H
    tq = _tile(lp, ATTN_Q_ROWS, ROW_ALIGN)
    nq = lp // tq
    scale = 1.0 / math.sqrt(QK_NOPE + QK_ROPE)

    def body(q_ref, kn_ref, v_ref, kr_ref, o_ref, lse_ref):
        def block(blk):
            nk = _key_limit(blk, tq, lp)
            s, qvalid = _attn_scores(q_ref[...], kn_ref[:nk], kr_ref[:nk], blk, tq, scale)
            m = jnp.max(s, axis=-1, keepdims=True)
            p = jnp.exp(s - m)
            l = jnp.sum(p, axis=-1, keepdims=True)
            o = jnp.dot(p.astype(BF16), v_ref[:nk], preferred_element_type=F32) / l
            o_ref[...] = jnp.where(qvalid, o, 0.0)
            lse_ref[...] = m + jnp.log(l)

        _per_q_block(nq, block)

    return _pc(body, name=name, grid=(h, nq),
               in_specs=[pl.BlockSpec((tq, HEAD_SLOT), lambda hh, i: (i, hh)),
                         pl.BlockSpec((lp, QK_NOPE), lambda hh, i: (0, 2 * hh)),
                         pl.BlockSpec((lp, V_HEAD), lambda hh, i: (0, 2 * hh + 1)),
                         pl.BlockSpec((lp, LANE), lambda hh, i: (0, 0))],
               out_specs=[pl.BlockSpec((tq, V_HEAD), lambda hh, i: (i, hh)),
                          pl.BlockSpec((None, tq, 1), lambda hh, i: (hh, i, 0))],
               out_shape=[jax.ShapeDtypeStruct((lp, h * V_HEAD), F32), jax.ShapeDtypeStruct((h, lp, 1), F32)],
               compiler_params=pltpu.CompilerParams(dimension_semantics=("parallel", "parallel")))(qx, kv, kv, kr)


def _attn_bwd(qx, kv, kr, o, lse, do, cfg, *, name):
    lp, h = cfg.LP, cfg.H
    tq = _tile(lp, ATTN_Q_ROWS, ROW_ALIGN)
    nq = lp // tq
    scale = 1.0 / math.sqrt(QK_NOPE + QK_ROPE)
    tn_dims = (((0,), (0,)), ((), ()))

    def body(q_ref, kn_ref, v_ref, kr_ref, o_ref, lse_ref, do_ref, dq_ref, dkv_ref, dkr_ref, dkv_acc):
        hh, i = pl.program_id(0), pl.program_id(1)

        @pl.when(i == 0)
        def _():
            dkv_acc[...] = jnp.zeros_like(dkv_acc)

        @pl.when((i == 0) & (hh == 0))
        def _():
            dkr_ref[...] = jnp.zeros_like(dkr_ref)

        def block(blk):
            nk = _key_limit(blk, tq, lp)
            q, kn, v, krv = q_ref[...], kn_ref[:nk], v_ref[:nk], kr_ref[:nk]
            s, qvalid = _attn_scores(q, kn, krv, blk, tq, scale)
            dov = jnp.where(qvalid, do_ref[...], 0.0)
            p = jnp.exp(s - lse_ref[...])
            delta = jnp.sum(dov * o_ref[...], axis=-1, keepdims=True)
            dob = dov.astype(BF16)
            dp = lax.dot_general(dob, v, (((1,), (1,)), ((), ())), preferred_element_type=F32)
            ds = (p * (dp - delta) * scale).astype(BF16)
            dq_ref[:, :QK_NOPE] = jnp.dot(ds, kn, preferred_element_type=F32)
            dq_ref[:, QK_NOPE:] = jnp.dot(ds, krv, preferred_element_type=F32)
            dkv_acc[:nk, :QK_NOPE] += lax.dot_general(ds, q[:, :QK_NOPE], tn_dims, preferred_element_type=F32)
            dkv_acc[:nk, QK_NOPE:] += lax.dot_general(p.astype(BF16), dob, tn_dims, preferred_element_type=F32)
            dkr_ref[:nk, :] += lax.dot_general(ds, q[:, QK_NOPE:], tn_dims, preferred_element_type=F32)

        _per_q_block(nq, block)

        @pl.when(i == nq - 1)
        def _():
            dkv_ref[...] = dkv_acc[...].astype(BF16)

    return _pc(body, name=name, grid=(h, nq),
               in_specs=[pl.BlockSpec((tq, HEAD_SLOT), lambda hh, i: (i, hh)),
                         pl.BlockSpec((lp, QK_NOPE), lambda hh, i: (0, 2 * hh)),
                         pl.BlockSpec((lp, V_HEAD), lambda hh, i: (0, 2 * hh + 1)),
                         pl.BlockSpec((lp, LANE), lambda hh, i: (0, 0)),
                         pl.BlockSpec((tq, V_HEAD), lambda hh, i: (i, hh)),
                         pl.BlockSpec((None, tq, 1), lambda hh, i: (hh, i, 0)),
                         pl.BlockSpec((tq, V_HEAD), lambda hh, i: (i, hh))],
               out_specs=[pl.BlockSpec((tq, HEAD_SLOT), lambda hh, i: (i, hh)),
                          pl.BlockSpec((lp, QK_NOPE + V_HEAD), lambda hh, i: (0, hh)),
                          pl.BlockSpec((lp, LANE), lambda hh, i: (0, 0))],
               out_shape=[jax.ShapeDtypeStruct((lp, h * HEAD_SLOT), F32),
                          jax.ShapeDtypeStruct((lp, h * (QK_NOPE + V_HEAD)), BF16),
                          jax.ShapeDtypeStruct((lp, LANE), F32)],
               scratch_shapes=[pltpu.VMEM((lp, QK_NOPE + V_HEAD), F32)],
               compiler_params=pltpu.CompilerParams(dimension_semantics=("arbitrary", "arbitrary")))(qx, kv, kv, kr, o, lse, do)


def _rot_half(x):
    lane = lax.broadcasted_iota(jnp.int32, x.shape, 1)
    half = QK_ROPE // 2
    return jnp.where(lane < half, -pltpu.roll(x, LANE - half, 1), pltpu.roll(x, half, 1))


def _rope(x, cos, sin):
    return x * cos + _rot_half(x) * sin


def _unrope(dy, cos, sin):
    return dy * cos - _rot_half(dy * sin)


def _rope_heads(fn, h):
    def apply(rid, q, cos, sin):
        parts = []
        for hh in range(h):
            parts.append(q[:, hh * HEAD_SLOT: hh * HEAD_SLOT + QK_NOPE])
            parts.append(fn(q[:, hh * HEAD_SLOT + QK_NOPE: (hh + 1) * HEAD_SLOT], cos, sin))
        return jnp.concatenate(parts, axis=1)
    return apply


ANY = pl.BlockSpec(memory_space=pl.ANY)


def _place():
    x, y, c = lax.axis_index("x"), lax.axis_index("y"), lax.axis_index("c")
    chips = [(1 - x, y), (x, 1 - y), (1 - x, 1 - y)]
    return x, y, c, chips


def _rcopy(src, dst, send_sem, recv_sem, dev):
    return pltpu.make_async_remote_copy(src_ref=src, dst_ref=dst, send_sem=send_sem, recv_sem=recv_sem,
                                        device_id=dev, device_id_type=MESH)


def _place_shard(shard, dtype, *, name, order=None, rows_to=None):
    shard = shard if shard.ndim == 3 else shard[None]
    n, r, cols = shard.shape
    rp = rows_to or r
    tm = _tile(r, max(ROW_ALIGN, PLACE_BLOCK_BYTES // (4 * cols)), ROW_ALIGN)
    me = (2 * lax.axis_index("x") + lax.axis_index("y")).astype(jnp.int32).reshape(1)
    extra = [] if order is None else [order]

    def body(me_ref, s_ref, *rest):
        rest[-1][...] = s_ref[...].astype(dtype)

    full = _pc(body, name=name,
               grid_spec=pltpu.PrefetchScalarGridSpec(
                   num_scalar_prefetch=1, grid=(n, r // tm),
                   in_specs=[pl.BlockSpec((None, tm, cols), lambda q, i, mr: (q, i, 0))] + [ANY] * len(extra),
                   out_specs=pl.BlockSpec((None, tm, cols), lambda q, i, mr: (mr[0] * n + q, i, 0))),
               out_shape=jax.ShapeDtypeStruct((4 * n, rp, cols), dtype),
               compiler_params=pltpu.CompilerParams(dimension_semantics=("arbitrary", "arbitrary")))(me, shard, *extra)
    if rp > r:
        pad = rp - r
        assert r % pad == 0

        def zero(me_ref, f_ref, o_ref):
            o_ref[...] = jnp.zeros_like(o_ref)

        full = _pc(zero, name=name + "_pad",
                   grid_spec=pltpu.PrefetchScalarGridSpec(
                       num_scalar_prefetch=1, grid=(n,), in_specs=[ANY],
                       out_specs=pl.BlockSpec((None, pad, cols), lambda q, mr: (mr[0] * n + q, r // pad, 0))),
                   out_shape=jax.ShapeDtypeStruct(full.shape, dtype), input_output_aliases={1: 0},
                   compiler_params=pltpu.CompilerParams(dimension_semantics=("arbitrary",)))(me, full)
    return full.reshape(4 * n * rp, cols)


def _allgather(fulls, *, name):
    n = len(fulls)

    def body(*refs):
        outs = refs[n:2 * n]
        send_sems, recv_sems = refs[2 * n:]
        x, y, c, chips = _place()
        sib = (x, y, 1 - c)
        me = 2 * x + y

        def rows(t, s, half):
            hrows = outs[t].shape[0] // 8
            return outs[t].at[pl.ds((2 * s + half) * hrows, hrows)]

        sent = []
        for t in range(n):
            for j, (cx, cy) in enumerate(chips):
                cp = _rcopy(rows(t, me, c), rows(t, me, c), send_sems.at[6 * t + j], recv_sems.at[6 * t + j], (cx, cy, c))
                cp.start()
                sent.append(cp)
        for t in range(n):
            for j, (cx, cy) in enumerate(chips):
                landed = rows(t, 2 * cx + cy, c)
                _rcopy(landed, landed, send_sems.at[6 * t + j], recv_sems.at[6 * t + j], (cx, cy, c)).wait_recv()
                cp = _rcopy(landed, landed, send_sems.at[6 * t + 3 + j], recv_sems.at[6 * t + 3 + j], sib)
                cp.start()
                sent.append(cp)
        for t in range(n):
            for j, (cx, cy) in enumerate(chips):
                other = rows(t, 2 * cx + cy, 1 - c)
                _rcopy(other, other, send_sems.at[6 * t + 3 + j], recv_sems.at[6 * t + 3 + j], sib).wait_recv()
        for cp in sent:
            cp.wait_send()

    return _pc(body, name=name, in_specs=[ANY] * n, out_specs=[ANY] * n,
               out_shape=[jax.ShapeDtypeStruct(f.shape, f.dtype) for f in fulls],
               input_output_aliases={t: t for t in range(n)},
               scratch_shapes=[pltpu.SemaphoreType.DMA((6 * n,)), pltpu.SemaphoreType.DMA((6 * n,))])(*fulls)


HBM = pl.BlockSpec(memory_space=pltpu.HBM)
SEM = pl.BlockSpec(memory_space=pltpu.SEMAPHORE)
EFFECT = pltpu.SideEffectType.DATAFLOW_SIDE_EFFECTING
TOKEN = jax.ShapeDtypeStruct((8, LANE), F32)


def _in_hbm(a):
    return pltpu.with_memory_space_constraint(a, pltpu.HBM)


def _half_rows(ref, s, half):
    hrows = ref.shape[0] // 8
    return ref.at[pl.ds((2 * s + half) * hrows, hrows)]


def _split_start(bufs, copies, n_copies, *, name, before=None):
    n = len(bufs)
    extra = [] if before is None else [before]

    def body(*refs):
        send_sems, recv_sems, token = refs[n + len(extra)], refs[n + len(extra) + 1], refs[-1]
        for k, (src, dst, dev) in enumerate(copies(refs[:n])):
            _rcopy(src, dst, send_sems.at[k], recv_sems.at[k], dev).start()
        token[...] = jnp.zeros_like(token)

    res = _pc(body, name=name, in_specs=[HBM] * n + [ANY] * len(extra),
              out_specs=[SEM, SEM] + [HBM] * n + [pl.BlockSpec(memory_space=pltpu.VMEM)],
              out_shape=[pltpu.SemaphoreType.DMA((n_copies,)), pltpu.SemaphoreType.DMA((n_copies,))]
              + [pltpu.HBM(b.shape, b.dtype) for b in bufs] + [TOKEN],
              input_output_aliases={t: 2 + t for t in range(n)},
              compiler_params=pltpu.CompilerParams(has_side_effects=EFFECT))(*[_in_hbm(b) for b in bufs], *extra)
    return res[0], res[1], list(res[2:2 + n]), res[-1]


def _split_wait(send_sems, recv_sems, bufs, copies, after, *, name):
    n = len(bufs)
    after = list(after) if isinstance(after, (list, tuple)) else [after]

    def body(*refs):
        send_ref, recv_ref = refs[n], refs[n + 1]
        for k, (src, dst, dev) in enumerate(copies(refs[:n])):
            cp = _rcopy(src, dst, send_ref.at[k], recv_ref.at[k], dev)
            cp.wait_send()
            cp.wait_recv()

    return _pc(body, name=name, in_specs=[HBM] * n + [SEM, SEM] + [ANY] * len(after), out_specs=[HBM] * n,
               out_shape=[pltpu.HBM(b.shape, b.dtype) for b in bufs],
               input_output_aliases={t: t for t in range(n)},
               compiler_params=pltpu.CompilerParams(has_side_effects=EFFECT))(*bufs, send_sems, recv_sems, *after)


def _allgather_ici_copies(refs):
    x, y, c, chips = _place()
    return [(_half_rows(r, 2 * x + y, c), _half_rows(r, 2 * x + y, c), (cx, cy, c)) for r in refs for cx, cy in chips]


def _rs_chips_copies(refs):
    x, y, c, chips = _place()
    n = len(refs) // 2
    return [(refs[t].at[2 * cx + cy], refs[n + t].at[j], (cx, cy, c)) for t in range(n) for j, (cx, cy) in enumerate(chips)]


def _allgather_forward_copies(refs):
    x, y, c, chips = _place()
    return [(_half_rows(r, 2 * cx + cy, c), _half_rows(r, 2 * cx + cy, c), (x, y, 1 - c)) for r in refs for cx, cy in chips]


def _rs_final_copies(refs):
    x, y, c, _ = _place()
    return [(r.at[c], r.at[c], (x, y, 1 - c)) for r in refs]


def _rs_sibling_copies(refs):
    x, y, c, _ = _place()
    n = len(refs) // 2
    out = []
    for t in range(n):
        h = refs[t].shape[0] // 8
        out += [(refs[t].at[pl.ds((2 * s + 1 - c) * h, h)], refs[n + t].at[s], (x, y, 1 - c)) for s in range(4)]
    return out


def _allgather_forward(fulls, *, name):
    n = len(fulls)

    def body(*refs):
        outs = refs[n:2 * n]
        send_sems, recv_sems = refs[2 * n:]
        x, y, c, chips = _place()
        sent = []
        for t in range(n):
            for j, (cx, cy) in enumerate(chips):
                landed = _half_rows(outs[t], 2 * cx + cy, c)
                cp = _rcopy(landed, landed, send_sems.at[3 * t + j], recv_sems.at[3 * t + j], (x, y, 1 - c))
                cp.start()
                sent.append(cp)
        for t in range(n):
            for j, (cx, cy) in enumerate(chips):
                other = _half_rows(outs[t], 2 * cx + cy, 1 - c)
                _rcopy(other, other, send_sems.at[3 * t + j], recv_sems.at[3 * t + j], (x, y, 1 - c)).wait_recv()
        for cp in sent:
            cp.wait_send()

    return _pc(body, name=name, in_specs=[ANY] * n, out_specs=[ANY] * n,
               out_shape=[jax.ShapeDtypeStruct(f.shape, f.dtype) for f in fulls],
               input_output_aliases={t: t for t in range(n)},
               scratch_shapes=[pltpu.SemaphoreType.DMA((3 * n,)), pltpu.SemaphoreType.DMA((3 * n,))])(*fulls)


def _rs_sibling(grads, *, name):
    n = len(grads)

    def body(*refs):
        ins, outs = refs[:n], refs[n:2 * n]
        send_sems, recv_sems = refs[2 * n:]
        x, y, c, _ = _place()
        cps = []
        for t in range(n):
            h = ins[t].shape[0] // 8
            for s in range(4):
                cp = _rcopy(ins[t].at[pl.ds((2 * s + 1 - c) * h, h)], outs[t].at[s], send_sems.at[4 * t + s],
                            recv_sems.at[4 * t + s], (x, y, 1 - c))
                cp.start()
                cps.append(cp)
        for cp in cps:
            cp.wait()

    return _pc(body, name=name, in_specs=[ANY] * n, out_specs=[ANY] * n,
               out_shape=[jax.ShapeDtypeStruct((4, g.shape[0] // 8, g.shape[1]), g.dtype) for g in grads],
               scratch_shapes=[pltpu.SemaphoreType.DMA((4 * n,)), pltpu.SemaphoreType.DMA((4 * n,))])(*grads)


def _rs_final(fulls, *, name):
    n = len(fulls)

    def body(*refs):
        outs = refs[n:2 * n]
        send_sems, recv_sems = refs[2 * n:]
        x, y, c, _ = _place()
        cps = []
        for t in range(n):
            cp = _rcopy(outs[t].at[c], outs[t].at[c], send_sems.at[t], recv_sems.at[t], (x, y, 1 - c))
            cp.start()
            cps.append(cp)
        for cp in cps:
            cp.wait()

    return _pc(body, name=name, in_specs=[ANY] * n, out_specs=[ANY] * n,
               out_shape=[jax.ShapeDtypeStruct(f.shape, f.dtype) for f in fulls],
               input_output_aliases={t: t for t in range(n)},
               scratch_shapes=[pltpu.SemaphoreType.DMA((n,)), pltpu.SemaphoreType.DMA((n,))])(*fulls)


def _add_halves(g, a, send_dtype, *, name):
    _, h, cols = a.shape
    th = _tile(h, max(ROW_ALIGN, PLACE_BLOCK_BYTES // (4 * cols)), ROW_ALIGN)
    g4 = g.reshape(4, 2, h, cols)
    idx = jnp.stack([lax.axis_index("c"), 2 * lax.axis_index("x") + lax.axis_index("y")]).astype(jnp.int32)

    def shard(k, ir):
        return (ir[1] + 1 + k) % 4

    def body(idx_ref, g_ref, a_ref, p_ref, s_ref):
        v = g_ref[...].astype(F32) + a_ref[...].astype(F32)
        s_ref[...] = v.astype(send_dtype)

        @pl.when(pl.program_id(1) == 3)
        def _():
            p_ref[...] = v

    return _pc(body, name=name,
               grid_spec=pltpu.PrefetchScalarGridSpec(
                   num_scalar_prefetch=1, grid=(h // th, 4),
                   in_specs=[pl.BlockSpec((None, None, th, cols), lambda i, k, ir: (shard(k, ir), ir[0], i, 0)),
                             pl.BlockSpec((None, th, cols), lambda i, k, ir: (shard(k, ir), i, 0))],
                   out_specs=[pl.BlockSpec((th, cols), lambda i, k, ir: (i, 0)),
                              pl.BlockSpec((None, th, cols), lambda i, k, ir: (shard(k, ir), i, 0))]),
               out_shape=[jax.ShapeDtypeStruct((h, cols), F32), jax.ShapeDtypeStruct(a.shape, send_dtype)],
               compiler_params=pltpu.CompilerParams(dimension_semantics=("arbitrary", "arbitrary")))(idx, g4, a)


def _add_chips(p, b, *, name, order=None):
    h, cols = p.shape
    th = _tile(h, max(ROW_ALIGN, PLACE_BLOCK_BYTES // (8 * cols)), ROW_ALIGN)
    idx = lax.axis_index("c").astype(jnp.int32).reshape(1)
    extra = [] if order is None else [order]

    def body(idx_ref, p_ref, b_ref, *rest):
        r_ref = rest[-1]
        r_ref[...] = ((p_ref[...] + b_ref[0].astype(F32)) + b_ref[1].astype(F32)) + b_ref[2].astype(F32)

    return _pc(body, name=name,
               grid_spec=pltpu.PrefetchScalarGridSpec(
                   num_scalar_prefetch=1, grid=(h // th,),
                   in_specs=[pl.BlockSpec((th, cols), lambda i, ir: (i, 0)),
                             pl.BlockSpec((3, th, cols), lambda i, ir: (0, i, 0))] + [ANY] * len(extra),
                   out_specs=pl.BlockSpec((None, th, cols), lambda i, ir: (ir[0], i, 0))),
               out_shape=jax.ShapeDtypeStruct((2, h, cols), F32),
               compiler_params=pltpu.CompilerParams(dimension_semantics=("arbitrary",)))(idx, p, b, *extra)


def _add_halves_all(grads, recv, send_dtypes, tag):
    parts, sends = [], []
    for t, (g, a) in enumerate(zip(grads, recv)):
        p, s = _add_halves(g, a, send_dtypes[t], name=f"rs_add_halves_{tag}{t}")
        parts.append(p)
        sends.append(s)
    return parts, sends


def _rs_finish(parts, others, tag, order=None):
    halves = [_add_chips(p, b, order=order, name=f"rs_add_chips_{tag}{t}") for t, (p, b) in enumerate(zip(parts, others))]
    full = _rs_final(halves, name=f"rs_final_{tag}")
    return [f.reshape(-1, f.shape[-1]) for f in full]


def _s5_discretize(lam_re, lam_im, log_dt, b_re, b_im):
    lam = lax.complex(lam_re, lam_im)
    dt = jnp.exp(log_dt)[:, None]
    lam_bar = jnp.exp(lam * dt)
    b_bar = ((lam_bar - 1.0) / lam)[..., None] * lax.complex(b_re, b_im)
    return jnp.real(lam_bar), jnp.imag(lam_bar), jnp.real(b_bar), jnp.imag(b_bar)


def _gp_from_lanes(v, cfg):
    v = jnp.transpose(v.reshape(cfg.NB, 2, GROUPS_PER_BLOCK, SSM_STATE), (1, 0, 2, 3)).reshape(2, cfg.G, SSM_STATE)
    return v[0], v[1]


def _bb_band(bb_re, bb_im, cfg):
    eye = jnp.eye(GROUPS_PER_BLOCK, dtype=F32)
    bb = jnp.stack([bb_re, bb_im]).reshape(2, cfg.NB, GROUPS_PER_BLOCK, SSM_STATE, SSM_GROUP)
    return jnp.einsum('rjgpc,gh->jgcrhp', bb, eye).reshape(cfg.DS, 2 * GROUPS_PER_BLOCK * SSM_STATE)


def _bb_from_band(m, cfg):
    eye = jnp.eye(GROUPS_PER_BLOCK, dtype=F32)
    m = m.reshape(cfg.NB, GROUPS_PER_BLOCK, SSM_GROUP, 2, GROUPS_PER_BLOCK, SSM_STATE)
    v = jnp.einsum('jgcrhp,gh->rjgpc', m, eye).reshape(2, cfg.G, SSM_STATE, SSM_GROUP)
    return v[0], v[1]


def _cc_band(c_re, c_im, cfg):
    eye = jnp.eye(GROUPS_PER_BLOCK, dtype=F32)
    cc = jnp.stack([c_re, -c_im]).reshape(2, cfg.NB, GROUPS_PER_BLOCK, SSM_GROUP, SSM_STATE)
    return jnp.einsum('rjgcp,gh->jrhpgc', cc, eye).reshape(cfg.NL, GROUPS_PER_BLOCK * SSM_GROUP)


def _cc_from_band(m, cfg):
    eye = jnp.eye(GROUPS_PER_BLOCK, dtype=F32)
    m = m.reshape(cfg.NB, 2, GROUPS_PER_BLOCK, SSM_STATE, GROUPS_PER_BLOCK, SSM_GROUP)
    v = jnp.einsum('jrhpgc,gh->rjgcp', m, eye).reshape(2, cfg.G, SSM_GROUP, SSM_STATE)
    return v[0], -v[1]


PACK_COLS = 512
PACK_ROW_ALIGN = 64


def _pack(arrs):
    flat = jnp.concatenate([a.reshape(-1).astype(F32) for a in arrs])
    unit = PACK_COLS * PACK_ROW_ALIGN
    total = -(-flat.shape[0] // unit) * unit
    return jnp.pad(flat, (0, total - flat.shape[0])).reshape(-1, PACK_COLS)


def _unpack(p, shapes):
    flat = p.reshape(-1)
    out, off = [], 0
    for shp in shapes:
        size = math.prod(shp)
        out.append(flat[off:off + size].reshape(shp))
        off += size
    return out


def _adamw(w, g, m, v, *, name, emit_grad=False):
    c1 = 1.0 / (1.0 - ADAM_B1 ** ADAM_STEP)
    c2 = 1.0 / (1.0 - ADAM_B2 ** ADAM_STEP)

    if w.ndim == 2:
        outs = _adamw(w[None], g[None], m[None], v[None], name=name, emit_grad=emit_grad)
        return [o[0] for o in outs]
    lead, rows, cols = w.shape
    tc = _tile(cols, 512)
    tm = _tile(rows, max(8, ADAMW_BLOCK_BYTES // (4 * tc)), 8)
    n_out = 4 if emit_grad else 3

    ni, nj = rows // tm, cols // tc
    nblk = lead * ni * nj
    nbuf = STREAM_BUFFERS

    def body(*refs):
        srcs, dsts = refs[:4], refs[4:4 + n_out]
        in_buf, out_buf, in_sem, out_sem = refs[4 + n_out:]

        def window(ref, b):
            n, i, j = b // (ni * nj), (b // nj) % ni, b % nj
            return ref.at[n, pl.ds(i * tm, tm), pl.ds(j * tc, tc)]

        def in_copy(k, b):
            return pltpu.make_async_copy(window(srcs[k], b), in_buf.at[k, b % nbuf], in_sem.at[k, b % nbuf])

        def out_copy(k, b):
            return pltpu.make_async_copy(out_buf.at[k, b % 2], window(dsts[k], b), out_sem.at[k, b % 2])

        for b in range(min(nbuf - 1, nblk)):
            for k in range(4):
                in_copy(k, b).start()

        def step(b, carry):
            @pl.when(b + nbuf - 1 < nblk)
            def _():
                for k in range(4):
                    in_copy(k, b + nbuf - 1).start()

            for k in range(4):
                in_copy(k, b).wait()

            @pl.when(b >= 2)
            def _():
                for k in range(n_out):
                    out_copy(k, b - 2).wait()

            slot = b % nbuf
            wv, gv, mv, vv = (in_buf[k, slot] for k in range(4))
            mn = ADAM_B1 * mv + (1.0 - ADAM_B1) * gv
            vn = ADAM_B2 * vv + (1.0 - ADAM_B2) * (gv * gv)
            delta = -ADAM_LR * ((mn * c1) / (jnp.sqrt(vn * c2) + ADAM_EPS) + ADAM_WD * wv)
            for k, val in enumerate((gv, delta, mn, vn) if emit_grad else (delta, mn, vn)):
                out_buf[k, b % 2] = val
                out_copy(k, b).start()
            return carry

        lax.fori_loop(0, nblk, step, 0)
        for b in range(max(nblk - 2, 0), nblk):
            for k in range(n_out):
                out_copy(k, b).wait()

    return _pc(body, name=name, in_specs=[ANY] * 4, out_specs=[ANY] * n_out,
               out_shape=[jax.ShapeDtypeStruct((lead, rows, cols), F32)] * n_out,
               scratch_shapes=[pltpu.VMEM((4, nbuf, tm, tc), F32), pltpu.VMEM((n_out, 2, tm, tc), F32),
                               pltpu.SemaphoreType.DMA((4, nbuf)), pltpu.SemaphoreType.DMA((n_out, 2))])(w, g, m, v)


def _to_comm_layout(name, w, cfg):
    w = w[0]
    if name == 'w_in':
        return jnp.pad(w, ((0, 0), (0, cfg.DINP - cfg.DIN)))
    if name == 'w_q_b':
        hs = w.shape[1] // (QK_NOPE + QK_ROPE)
        wt = w.T.reshape(hs, QK_NOPE + QK_ROPE, cfg.QL)
        return jnp.pad(wt, ((0, 0), (0, HEAD_SLOT - QK_NOPE - QK_ROPE), (0, 0))).reshape(hs * HEAD_SLOT, cfg.QL)
    if name == 'w_kv_b':
        return w.T
    if name == 'w_up':
        return w.T.reshape(2, cfg.F // 4, cfg.D)
    return w


def _from_comm_layout(name, g, cfg):
    if name == 'w_in':
        g = g[:, :cfg.DIN]
    elif name == 'w_q_b':
        hs = g.shape[0] // HEAD_SLOT
        g = g.reshape(hs, HEAD_SLOT, cfg.QL)[:, :QK_NOPE + QK_ROPE].reshape(hs * (QK_NOPE + QK_ROPE), cfg.QL).T
    elif name == 'w_kv_b':
        g = g.T
    elif name == 'w_up':
        g = g.reshape(2, cfg.FQ, cfg.D)[:, :cfg.F // 4].reshape(cfg.F // 2, cfg.D).T
    elif name == 'w_down':
        g = g[:cfg.F // 4]
    return g[None]


def _ff_pad(v, cfg):
    k = v.shape[0]
    return jnp.pad(v.reshape(k, 4, cfg.F // 4), ((0, 0), (0, 0), (0, cfg.FQ - cfg.F // 4))).reshape(k, cfg.FP)


def _ff_unpad(v, cfg):
    k = v.shape[0]
    return v.reshape(k, 4, cfg.FQ)[:, :, :cfg.F // 4].reshape(k, cfg.F)


def _step(cfg, w, m, v, x, loss_target):
    lp, d, ds, nl = cfg.LP, cfg.D, cfg.DS, cfg.NL
    xi, yi = lax.axis_index("x"), lax.axis_index("y")
    me = 2 * xi + yi

    def place(n, order=None):
        rows_to = cfg.FQ if n in ('w_up', 'w_down') else None
        return _place_shard(_to_comm_layout(n, w[n], cfg), BF16, order=order, rows_to=rows_to, name=f"place_{n}")

    first = [place('w_in'), _place_shard(w['meta_tokens'], F32, name="place_meta")]
    f_send, f_recv, f_flying, f_token = _split_start(first, _allgather_ici_copies, 6, name="allgather_first_start")
    conv_w_shard = jnp.pad(w['conv_w'][0], ((0, ROW_ALIGN - 3), (0, cfg.FQ - cfg.F // 4)))
    mid = [place(n, f_token) for n in BIG[1:5]] + [_place_shard(conv_w_shard, F32, order=f_token, name="place_conv_w")]
    mid_send, mid_recv, mid_flying, mid_token = _split_start(mid, _allgather_ici_copies, 3 * len(mid), before=f_token,
                                                             name="allgather_mid_start")
    up_send, up_recv, up_flying, up_token = _split_start([place('w_up', mid_token)], _allgather_ici_copies, 3,
                                                         before=mid_token, name="allgather_up_start")
    dn_send, dn_recv, dn_flying, ffn_token = _split_start([place('w_down', up_token)], _allgather_ici_copies, 3,
                                                          before=up_token, name="allgather_down_start")
    conv_b = _ff_pad(w['conv_b'], cfg)

    pos = (jnp.arange(lp, dtype=jnp.int32) - PAD).astype(F32)
    inv_freq = 1.0 / (ROPE_BASE ** (jnp.arange(0, QK_ROPE, 2, dtype=F32) / QK_ROPE))
    ang = pos[:, None] * inv_freq[None, :]
    zpad = jnp.zeros((lp, LANE - QK_ROPE), F32)
    cos_t = jnp.concatenate([jnp.cos(ang), jnp.cos(ang), zpad], axis=1)
    sin_t = jnp.concatenate([jnp.sin(ang), jnp.sin(ang), zpad], axis=1)

    s5_in = (w['lam_re'][0], w['lam_im'][0], w['log_dt'][0], w['b_re'][0], w['b_im'][0])
    (a_re, a_im, bb_re, bb_im), s5_vjp = jax.vjp(_s5_discretize, *s5_in)
    lam_dt = lax.complex(s5_in[0], s5_in[1]) * jnp.exp(s5_in[2])[:, None]
    a_pow = jnp.exp(jnp.arange(1, 9, dtype=F32)[:, None, None] * lam_dt[None])
    r8 = jnp.arange(8)
    step_f = jnp.stack([jnp.where((r8 >= k)[:, None, None], a_pow[k - 1][None], 0.0) for k in (1, 2, 4)]).reshape(24, cfg.G, -1)
    step_b = jnp.stack([jnp.where((r8 < 8 - k)[:, None, None], a_pow[k - 1][None], 0.0) for k in (1, 2, 4)]).reshape(24, cfg.G, -1)
    rows_f = jnp.concatenate([a_pow, step_f])
    rows_b = jnp.conj(jnp.concatenate([a_pow[::-1], step_b]))

    def lane_rows(t):
        v = jnp.stack([jnp.real(t), jnp.imag(t)], axis=1).reshape(t.shape[0], 2, cfg.NB, GROUPS_PER_BLOCK, SSM_STATE)
        return jnp.transpose(v, (0, 2, 1, 3, 4)).reshape(t.shape[0], cfg.NL)

    pw_fwd, pw_bwd = lane_rows(rows_f), lane_rows(rows_b)
    bb_band = _bb_band(bb_re, bb_im, cfg).astype(BF16)
    cc_band = _cc_band(w['c_re'][0], w['c_im'][0], cfg).astype(BF16)
    d_skip, b_glu = w['d_skip'], w['b_glu']

    f_landed = _split_wait(f_send, f_recv, f_flying, _allgather_ici_copies, [ffn_token, cos_t, sin_t, pw_fwd, pw_bwd, bb_band, cc_band],
                           name="allgather_first_wait")
    w_in, meta_full = _allgather_forward(f_landed, name="allgather_first_forward")
    meta = jnp.transpose(meta_full.reshape(4, N_META, d // 4), (1, 0, 2)).reshape(N_META, d)
    mix_norm = w['mix_norm'] + ffn_token[0:1, 0:1]

    h0 = jnp.concatenate([jnp.zeros((PAD, d), F32), meta, x[0]], axis=0)
    xn = _rms_fwd(h0, mix_norm, name="rms_mix")
    z = _mm(xn, w_in, name="mm_in", tn=_tile(cfg.DINP, 640))
    u = (z, ds, 0)
    q_a = (z, cfg.QL, ds // cfg.QL)
    kv_a = (z, cfg.KVL, (ds + cfg.QL) // cfg.KVL)
    k_pe = (z, LANE, (ds + cfg.QL + cfg.KVL) // LANE)

    hs, yc = _s5_fwd(z, bb_band, cc_band, pw_fwd, cfg, name="s5_fwd")

    def s5_y(ycv, uv, dk):
        return ycv + dk * uv

    gl = _ew(lambda rid, ycv, uv, dk: jax.nn.gelu(s5_y(ycv, uv, dk)), [yc, u], [d_skip], [(ds, BF16)], name="s5_gelu")[0]
    mid_landed = _split_wait(mid_send, mid_recv, mid_flying, _allgather_ici_copies, gl, name="allgather_mid_wait")
    w_glu, w_qt, w_kvt, w_out, conv_full = _allgather_forward(mid_landed, name="allgather_mid_forward")
    conv_w = jnp.transpose(conv_full.reshape(4, ROW_ALIGN, cfg.FQ)[:, :3], (1, 0, 2)).reshape(3, cfg.FP)
    tg = _mm(gl, w_glu, name="mm_glu")
    ya = _ew(lambda rid, ycv, uv, tv, dk, bg: jax.nn.gelu(s5_y(ycv, uv, dk)) * jax.nn.sigmoid(tv + bg),
             [yc, u, tg], [d_skip, b_glu], [(ds, F32)], name="s5_glu")[0]

    qn = _rms_fwd(q_a, w['q_a_norm'], name="rms_q")
    kvn = _rms_fwd(kv_a, w['kv_a_norm'], name="rms_kv")
    q_raw = _mm(qn, w_qt, tb=True, name="mm_q")
    qx = _ew(_rope_heads(_rope, cfg.H), [q_raw, cos_t, sin_t], [], [(cfg.H * HEAD_SLOT, BF16)], name="rope_q")[0]
    kv = _mm(kvn, w_kvt, tb=True, out_dtype=BF16, name="mm_kv")
    kr = _ew(lambda rid, kp, cs, sn: _rope(kp, cs, sn), [k_pe, cos_t, sin_t], [], [(LANE, BF16)], name="rope_k")[0]
    o, lse = _attn_fwd(qx, kv, kr, cfg, name="attn_fwd")

    def norm2(rid, yav, ov, gs, ga):
        return jnp.concatenate([_rms_parts(yav, gs)[0] * gs, _rms_parts(ov, ga)[0] * ga], axis=1)

    up_landed = _split_wait(up_send, up_recv, up_flying, _allgather_ici_copies, o, name="allgather_up_wait")
    uf_send, uf_recv, uf_flying, uf_token = _split_start(up_landed, _allgather_forward_copies, 3,
                                                         name="allgather_up_forward_start")
    yn = _ew(norm2, [ya, o], [w['out_norm_ssm'] + uf_token[0:1, 0:1], w['out_norm_attn']], [(cfg.DMIX, BF16)],
             name="rms_out")[0]
    h1 = _mm(yn, w_out, res=h0, name="mm_out")
    xn2 = _rms_fwd(h1, w['ffn_norm'], name="rms_ffn")
    dn_landed = _split_wait(dn_send, dn_recv, dn_flying, _allgather_ici_copies, xn2, name="allgather_down_wait")
    df_send, df_recv, df_flying, df_token = _split_start(dn_landed, _allgather_forward_copies, 3,
                                                         name="allgather_down_forward_start")
    w_upt, = _split_wait(uf_send, uf_recv, uf_flying, _allgather_forward_copies, df_token,
                         name="allgather_up_forward_wait")
    up, act = _ffn_up(xn2, w_upt, conv_w, conv_b, name="ffn_up")
    w_down, = _split_wait(df_send, df_recv, df_flying, _allgather_forward_copies, act,
                          name="allgather_down_forward_wait")
    h2 = _mm(act, w_down, res=h1, tm=_tile(lp, 1088, ROW_ALIGN), name="mm_down")

    g_final = w['final_norm'].reshape(1, d)

    def head(rid, hv, tv, gv):
        xhat, r = _rms_parts(hv, gv)
        valid = rid >= PAD + N_META
        diff = jnp.where(valid, xhat * gv - tv, 0.0)
        dout = diff * (1.0 / d)
        dxhat = dout * gv
        dx = r * (dxhat - xhat * jnp.mean(dxhat * xhat, axis=-1, keepdims=True))
        return dx, dx, dout * xhat, 0.5 * diff * dout

    dh2, dh2_b, dg_final, loss_cols = _ew(head, [h2, (loss_target[0], d, 0, SKIP)], [g_final], [(d, F32), (d, BF16)], [d, d],
                                          tm=PAD + N_META, name="loss_head")
    loss = lax.psum(jnp.sum(loss_cols), ("x", "y", "c"))

    dw_down = _mm(act, dh2_b, ta=True, tn=d, tm=512, out_dtype=BF16, name="mm_dw_down")

    def sibling_start(g, tag):
        land = lax.empty((4, g.shape[0] // 8, g.shape[1]), g.dtype)
        return _split_start([g, land], _rs_sibling_copies, 4, name=f"rs_sibling_{tag}_start")

    dn_send, dn_recv, dn_flying, dn_token = sibling_start(dw_down, "down")
    dup, dconv_w, dconv_b = _ffn_dact(dh2_b, w_down, up, conv_w, conv_b + dn_token[0:1, 0:1], name="ffn_dact")
    tk_up, tm_up = _tile(cfg.FP, 2816), _tile(cfg.FP, 512)
    dw_upt = _mm(dup, xn2, ta=True, dims=(2 * cfg.FP, d, lp), tn=d, tm=tm_up, a_lead=True, out_dtype=BF16, name="mm_dw_up",
                 a_idx=lambda i, j, k: (i // (cfg.FP // tm_up), 0, i % (cfg.FP // tm_up)))
    up_send, up_recv, up_flying, up_token = sibling_start(dw_upt, "up")
    dxn2 = _mm(dup, w_upt, dims=(lp, d, 2 * cfg.FP), tk=tk_up, tn=512, a_lead=True, name="mm_dxn2",
               a_idx=lambda i, j, k: (k // (cfg.FP // tk_up), i, k % (cfg.FP // tk_up)))
    dh1, dh1_b, dg_ffn = _rms_bwd(h1, w['ffn_norm'] + up_token[0:1, 0:1], dxn2, res=dh2, mask=True, with_bf16=True,
                                  name="rms_ffn_bwd")

    dyn = _mm(dh1_b, w_out, tb=True, name="mm_dyn")
    dw_out = _mm(yn, dh1_b, ta=True, tn=d, tm=512, name="mm_dw_out")
    up_done = _split_wait(up_send, up_recv, up_flying, _rs_sibling_copies, dw_out, name="rs_sibling_up_wait")
    dn_done = _split_wait(dn_send, dn_recv, dn_flying, _rs_sibling_copies, dw_out, name="rs_sibling_down_wait")
    early_parts, early_sends = _add_halves_all([up_done[0], dn_done[0]], [up_done[1], dn_done[1]], [BF16] * 2, "early")
    chip_lands = [lax.empty((3,) + s.shape[1:], s.dtype) for s in early_sends]
    ch_send, ch_recv, ch_flying, ch_token = _split_start(early_sends + chip_lands, _rs_chips_copies, 6,
                                                         name="rs_chips_early_start")
    dya, dg_ssm = _rms_bwd(ya, w['out_norm_ssm'] + ch_token[0:1, 0:1], (dyn, ds, 0), name="rms_ssm_bwd")
    do, dg_attn = _rms_bwd(o, w['out_norm_attn'], (dyn, cfg.DATTN, ds // cfg.DATTN), name="rms_attn_bwd")

    dqx, dkv, dkr = _attn_bwd(qx, kv, kr, o, lse, do, cfg, name="attn_bwd")
    dq_raw = _ew(_rope_heads(_unrope, cfg.H), [dqx, cos_t, sin_t], [], [(cfg.H * HEAD_SLOT, BF16)], name="unrope_q")[0]
    dk_pe = _ew(lambda rid, dk, cs, sn: _unrope(dk, cs, sn), [dkr, cos_t, sin_t], [], [(LANE, F32)], name="unrope_k")[0]
    dqn = _mm(dq_raw, w_qt, name="mm_dqn")
    dw_qt = _mm(dq_raw, qn, ta=True, tm=512, name="mm_dw_q")
    dkvn = _mm(dkv, w_kvt, name="mm_dkvn")
    dw_kvt = _mm(dkv, kvn, ta=True, tm=512, name="mm_dw_kv")
    dq_a, dg_q = _rms_bwd(q_a, w['q_a_norm'], dqn, name="rms_q_bwd")
    dkv_a, dg_kv = _rms_bwd(kv_a, w['kv_a_norm'], dkvn, name="rms_kv_bwd")

    def glu_bwd(rid, ycv, uv, tv, dyav, dk, bg):
        gelu = jax.nn.gelu(s5_y(ycv, uv, dk))
        sg = jax.nn.sigmoid(tv + bg)
        dt = dyav * gelu * sg * (1.0 - sg)
        return dt, dyav * sg, dt

    dt_b, dgl1, db_glu = _ew(glu_bwd, [yc, u, tg, dya], [d_skip, b_glu], [(ds, BF16), (ds, F32)], [ds], name="s5_glu_bwd")
    dgl = _mm(dt_b, w_glu, tb=True, res=dgl1, name="mm_dgl")
    dw_glu = _mm(gl, dt_b, ta=True, tm=512, name="mm_dw_glu")

    def gelu_bwd(rid, ycv, uv, dglv, dk):
        _, vjp = jax.vjp(jax.nn.gelu, s5_y(ycv, uv, dk))
        dy = vjp(dglv)[0]
        return dy, dy * dk, dy * uv

    mid_grads = [dw_out, dw_glu, dw_qt, dw_kvt]
    mid_lands = [lax.empty((4, g.shape[0] // 8, g.shape[1]), g.dtype) for g in mid_grads]
    ms_send, ms_recv, ms_flying, ms_token = _split_start(mid_grads + mid_lands, _rs_sibling_copies, 4 * len(mid_grads),
                                                         name="rs_sibling_mid_start")
    dy_b, du_skip, dd_skip = _ew(gelu_bwd, [yc, u, dgl], [d_skip + ms_token[0:1, 0:1]], [(ds, BF16), (ds, F32)], [ds],
                                 name="s5_gelu_bwd")
    ms_done = _split_wait(ms_send, ms_recv, ms_flying, _rs_sibling_copies, dy_b, name="rs_sibling_mid_wait")
    mid_parts, mid_sends = _add_halves_all(ms_done[:4], ms_done[4:], [BF16] * 4, "mid")
    mid_chip_lands = [lax.empty((3,) + s.shape[1:], s.dtype) for s in mid_sends]
    mc_send, mc_recv, mc_flying, mc_token = _split_start(mid_sends + mid_chip_lands, _rs_chips_copies, 3 * len(mid_sends),
                                                         name="rs_chips_mid_start")
    du, dbb_band, dcc_band, da_l = _s5_bwd(dy_b, hs, z, bb_band, cc_band, pw_bwd + mc_token[0:1, 0:1], du_skip, cfg,
                                           name="s5_bwd")

    dz = jnp.concatenate([du, dq_a, dkv_a, dk_pe], axis=1).astype(BF16)
    dxn = _mm(dz, w_in, tb=True, name="mm_dxn")
    dw_in = _mm(xn, dz, ta=True, tm=512, tn=_tile(cfg.DINP, 1024), name="mm_dw_in")
    def mix_bwd(rid, xv, dyv, resv, gv):
        dx, dg = _rms_bwd_block(xv, gv, dyv)
        dx = dx + resv
        return dx, dx, dg

    grad_x, dh0_head, dg_mix = _ew(mix_bwd, [h0, dxn, dh1], [mix_norm], [(d, F32, SKIP), (d, F32, FIRST)], [d],
                                   tm=PAD + N_META, name="rms_mix_bwd")
    grad_x = grad_x[None]

    da_re, da_im = _gp_from_lanes(da_l, cfg)
    dbb_re, dbb_im = _bb_from_band(dbb_band, cfg)
    dlam_re, dlam_im, dlog_dt, db_re, db_im = s5_vjp((da_re, da_im, dbb_re, dbb_im))
    dc_re, dc_im = _cc_from_band(dcc_band, cfg)
    local_small = {
        'meta_tokens': dh0_head[PAD:], 'mix_norm': dg_mix, 'lam_re': dlam_re, 'lam_im': dlam_im, 'log_dt': dlog_dt,
        'b_re': db_re, 'b_im': db_im, 'c_re': dc_re, 'c_im': dc_im, 'd_skip': dd_skip, 'b_glu': db_glu, 'q_a_norm': dg_q,
        'kv_a_norm': dg_kv, 'out_norm_ssm': dg_ssm, 'out_norm_attn': dg_attn, 'ffn_norm': dg_ffn,
        'conv_w': _ff_unpad(dconv_w, cfg), 'conv_b': _ff_unpad(dconv_b, cfg), 'final_norm': dg_final,
    }
    small_shapes = [local_small[n].shape for n in SMALL]

    small_pack = _pack([local_small[n] for n in SMALL])
    ch_done = _split_wait(ch_send, ch_recv, ch_flying, _rs_chips_copies, small_pack, name="rs_chips_early_wait")
    early_halves = [_add_chips(p, b, name=f"rs_add_chips_early{t}") for t, (p, b) in enumerate(zip(early_parts, ch_done[2:]))]
    fe_send, fe_recv, fe_flying, fe_token = _split_start(early_halves, _rs_final_copies, 2, name="rs_final_early_start")
    end_local = [dw_in, small_pack + fe_token[0:1, 0:1]]
    end_recv = _rs_sibling(end_local, name="rs_sibling_end")
    end_parts, end_sends = _add_halves_all(end_local, end_recv, [BF16, F32], "end")
    end_lands = [lax.empty((3,) + s.shape[1:], s.dtype) for s in end_sends]
    ec_send, ec_recv, ec_flying, ec_token = _split_start(end_sends + end_lands, _rs_chips_copies, 3 * len(end_sends),
                                                         name="rs_chips_end_start")
    fe_done = _split_wait(fe_send, fe_recv, fe_flying, _rs_final_copies, ec_token, name="rs_final_early_wait")
    red_up, red_down = [f.reshape(-1, f.shape[-1]) for f in fe_done]

    delta, new_m, new_v, grads = {}, {}, {}, {}
    padded_rows = ('w_down',)

    def adamw_big(n, red):
        shp = w[n].shape
        w2, m2, v2 = [t.reshape(shp[-2], shp[-1]) for t in (w[n], m[n], v[n])]
        if n in padded_rows:
            g2, dl, mn, vn = _adamw(w2, red, m2, v2, emit_grad=True, name=f"adamw_{n}")
            grads[n] = g2.reshape(shp)
        else:
            grads[n] = _from_comm_layout(n, red, cfg)
            dl, mn, vn = _adamw(w2, grads[n].reshape(shp[-2], shp[-1]), m2, v2, name=f"adamw_{n}")
        delta[n], new_m[n], new_v[n] = dl.reshape(shp), mn.reshape(shp), vn.reshape(shp)

    def adamw_up(red):
        q = cfg.F // 4
        wt, mt, vt = [jnp.transpose(t[0]).reshape(2, q, d) for t in (w['w_up'], m['w_up'], v['w_up'])]
        outs = _adamw(wt, red.reshape(2, cfg.FQ, d), mt, vt, emit_grad=True, name="adamw_w_up")
        grads['w_up'], delta['w_up'], new_m['w_up'], new_v['w_up'] = [jnp.transpose(t.reshape(2 * q, d))[None] for t in outs]

    adamw_up(red_up)
    adamw_big('w_down', red_down)
    mc_done = _split_wait(mc_send, mc_recv, mc_flying, _rs_chips_copies, delta['w_down'], name="rs_chips_mid_wait")
    ec_done = _split_wait(ec_send, ec_recv, ec_flying, _rs_chips_copies, mc_done[0], name="rs_chips_end_wait")
    red = _rs_finish(mid_parts + end_parts, list(mc_done[len(mid_sends):]) + list(ec_done[len(end_sends):]), "rest")
    small_full = _allgather([_place_shard(red[5], F32, name="place_small")], name="allgather_small")[0]
    small_sum = dict(zip(SMALL, _unpack(small_full, small_shapes)))
    for n, r in zip(['w_out', 'w_glu', 'w_q_b', 'w_kv_b'], red[:4]):
        adamw_big(n, r)
    in_t = [jnp.transpose(t[0]) for t in (w['w_in'], m['w_in'], v['w_in'])]
    outs = _adamw(in_t[0], jnp.transpose(red[4][:, :cfg.DIN]), in_t[1], in_t[2], emit_grad=True, name="adamw_w_in")
    grads['w_in'], delta['w_in'], new_m['w_in'], new_v['w_in'] = [jnp.transpose(t)[None] for t in outs]

    for n in SMALL:
        g = small_sum[n]
        if n == 'meta_tokens':
            g = lax.dynamic_slice_in_dim(g, me * (d // 4), d // 4, axis=1)
        elif n == 'conv_w':
            g = lax.dynamic_slice_in_dim(g, me * (cfg.F // 4), cfg.F // 4, axis=1)[None]
        else:
            g = g.reshape(w[n].shape)
        grads[n] = g

    shapes = [w[n].shape for n in SMALL]
    packs = [_pack([src[n] for n in SMALL]) for src in (w, grads, m, v)]
    for dst, p in zip((delta, new_m, new_v), _adamw(*packs, name="adamw_small")):
        dst.update(zip(SMALL, _unpack(p, shapes)))

    return (loss, grad_x, *[grads[n] for n in WEIGHTS], *[delta[n] for n in WEIGHTS],
            *[new_m[n] for n in WEIGHTS], *[new_v[n] for n in WEIGHTS])


def kernel(x, meta_tokens, mix_norm, w_in, lam_re, lam_im, log_dt, b_re, b_im, c_re, c_im, d_skip, w_glu, b_glu, q_a_norm, w_q_b, kv_a_norm, w_kv_b, out_norm_ssm, out_norm_attn, w_out, ffn_norm, w_up, conv_w, conv_b, w_down, final_norm, loss_target, m_meta_tokens, m_mix_norm, m_w_in, m_lam_re, m_lam_im, m_log_dt, m_b_re, m_b_im, m_c_re, m_c_im, m_d_skip, m_w_glu, m_b_glu, m_q_a_norm, m_w_q_b, m_kv_a_norm, m_w_kv_b, m_out_norm_ssm, m_out_norm_attn, m_w_out, m_ffn_norm, m_w_up, m_conv_w, m_conv_b, m_w_down, m_final_norm, v_meta_tokens, v_mix_norm, v_w_in, v_lam_re, v_lam_im, v_log_dt, v_b_re, v_b_im, v_c_re, v_c_im, v_d_skip, v_w_glu, v_b_glu, v_q_a_norm, v_w_q_b, v_kv_a_norm, v_w_kv_b, v_out_norm_ssm, v_out_norm_attn, v_w_out, v_ffn_norm, v_w_up, v_conv_w, v_conv_b, v_w_down, v_final_norm):
    args = dict(locals())
    w = {n: args[n] for n in WEIGHTS}
    m = {n: args["m_" + n] for n in WEIGHTS}
    v = {n: args["v_" + n] for n in WEIGHTS}
    return _step(PROD, w, m, v, x, loss_target)
```
